```python
import jax, jax.numpy as jnp
from jax import lax
import numpy as np

D_MODEL = 1024
BATCH = 2
SEQ = 8192
DEPTH = 1

N_META = 16
D_MIX = D_MODEL
RWKV_W = D_MIX // 2
RWKV_HEAD = 64
RWKV_HEADS = RWKV_W // RWKV_HEAD
DECAY_RANK = 64
AAA_RANK = 64
GATE_RANK = 160
LRU_W = D_MIX - RWKV_W
LRU_BLOCKS = 8
LRU_BLOCK = LRU_W // LRU_BLOCKS
CONV_WIDTH = 4
LRU_C = 8.0
N_GROUPS = 4
EXPERTS_PER_GROUP = 8
N_EXPERTS = N_GROUPS * EXPERTS_PER_GROUP
TOP_K = 2
D_EXPERT = 512
MOE_BLOCK = 128
LN_EPS = 1e-5
GN_EPS = 64e-5
DEEPNORM_ALPHA = (2 * DEPTH) ** 0.25
DEEPNORM_BETA = (8 * DEPTH) ** -0.25

OFF_R = 0
OFF_K = OFF_R + RWKV_W
OFF_V = OFF_K + RWKV_W
OFF_W = OFF_V + RWKV_W
OFF_A = OFF_W + DECAY_RANK
OFF_G = OFF_A + AAA_RANK
RWKV_COLS = OFF_G + GATE_RANK
OFF_LX = RWKV_COLS
OFF_LG = OFF_LX + LRU_W
N_IN = OFF_LG + LRU_W

kernel_name = "hymba_rwkv7_rglru_hmoe_deepnorm"

F32 = jnp.float32


def layer_norm(x, g, b):
    xf = x.astype(F32)
    mu = jnp.mean(xf, -1, keepdims=True)
    var = jnp.mean(jnp.square(xf - mu), -1, keepdims=True)
    return ((xf - mu) * lax.rsqrt(var + LN_EPS) * g.astype(F32) + b.astype(F32)).astype(x.dtype)


def rwkv7_recurrence(r, w, k, v, a, b):
    Bs, T, H, N = r.shape
    xs = tuple(jnp.moveaxis(t, 1, 0) for t in (r, w, k, v, a, b))

    def step(S, inp):
        r_t, w_t, k_t, v_t, a_t, b_t = inp
        sa = jnp.einsum('bhvk,bhk->bhv', S, a_t)
        S = S * w_t[:, :, None, :] + sa[..., None] * b_t[:, :, None, :] + v_t[..., None] * k_t[:, :, None, :]
        return S, jnp.einsum('bhvk,bhk->bhv', S, r_t)

    _, ys = lax.scan(step, jnp.zeros((Bs, H, N, N), F32), xs)
    return jnp.moveaxis(ys, 0, 1)


def linear_scan(a, b):
    def comb(left, right):
        al, bl = left
        ar, br = right
        return al * ar, ar * bl + br
    _, h = lax.associative_scan(comb, (a, b), axis=1)
    return h


def token_mixers(h, w_in, mu_shift, w0, w_decay_up, a0, w_a_up, w_g_up, k_k, k_a, r_k, gn_g, gn_b,
                 conv_w, conv_b, w_rg, b_rg, w_ig, b_ig, lru_lambda, w_out):
    Bs, T, _ = h.shape
    u = h @ w_in

    ur = u[..., :RWKV_COLS]
    ur_prev = jnp.pad(ur[:, :-1], ((0, 0), (1, 0), (0, 0)))
    ur = (ur + (ur_prev - ur) * mu_shift).astype(F32)
    r = ur[..., OFF_R:OFF_K]
    k = ur[..., OFF_K:OFF_V]
    v = ur[..., OFF_V:OFF_W]
    zw = ur[..., OFF_W:OFF_A]
    za = ur[..., OFF_A:OFF_G]
    zg = ur[..., OFF_G:RWKV_COLS]
    w_log = -jax.nn.softplus(-(w0.astype(F32) + jnp.tanh(zw) @ w_decay_up.astype(F32))) - 0.5
    decay = jnp.exp(-jnp.exp(w_log))
    a = jax.nn.sigmoid(a0.astype(F32) + za @ w_a_up.astype(F32))
    g = jax.nn.sigmoid(zg) @ w_g_up.astype(F32)
    hd = lambda t: t.reshape(Bs, T, RWKV_HEADS, RWKV_HEAD)
    kk = hd(k * k_k.astype(F32))
    kk = kk / jnp.maximum(jnp.sqrt(jnp.sum(kk * kk, -1, keepdims=True)), 1e-12)
    k = k * (1.0 + (a - 1.0) * k_a.astype(F32))
    rh, kh, vh = hd(r), hd(k), hd(v)
    y = rwkv7_recurrence(rh, hd(decay), kh, vh, -kk, kk * hd(a))
    ym = jnp.mean(y, -1, keepdims=True)
    yv = jnp.mean(jnp.square(y - ym), -1, keepdims=True)
    y = ((y - ym) * lax.rsqrt(yv + GN_EPS)).reshape(Bs, T, RWKV_W) * gn_g.astype(F32) + gn_b.astype(F32)
    bonus = jnp.sum(rh * kh * r_k.astype(F32), -1, keepdims=True) * vh
    y_rwkv = ((y + bonus.reshape(Bs, T, RWKV_W)) * g).astype(h.dtype)

    xl = u[..., OFF_LX:OFF_LG]
    gl = u[..., OFF_LG:N_IN]
    xc = lax.conv_general_dilated(xl, conv_w[:, None, :], window_strides=(1,),
                                  padding=[(CONV_WIDTH - 1, 0)],
                                  dimension_numbers=('NWC', 'WIO', 'NWC'),
                                  feature_group_count=LRU_W) + conv_b
    xb = xc.reshape(Bs, T, LRU_BLOCKS, LRU_BLOCK)
    gate_r = jax.nn.sigmoid((jnp.einsum('btni,nij->btnj', xb, w_rg).reshape(Bs, T, LRU_W) + b_rg).astype(F32))
    gate_i = jax.nn.sigmoid((jnp.einsum('btni,nij->btnj', xb, w_ig).reshape(Bs, T, LRU_W) + b_ig).astype(F32))
    log_a = LRU_C * gate_r * jax.nn.log_sigmoid(lru_lambda.astype(F32))
    mult = jnp.sqrt(jnp.maximum(-jnp.expm1(2.0 * log_a), 0.0))
    hs = linear_scan(jnp.exp(log_a), mult * gate_i * xc.astype(F32))
    y_lru = (hs * jax.nn.gelu(gl.astype(F32))).astype(h.dtype)

    return jnp.concatenate([y_rwkv, y_lru], axis=-1) @ w_out


def hierarchical_moe(h, w_rgrp, b_rgrp, w_rexp, b_rexp, w_gate, w_up, w_down):
    Bs, T, D = h.shape
    ht = h.reshape(-1, D)
    M = ht.shape[0]
    A = M * TOP_K
    glog = (ht @ w_rgrp + b_rgrp).astype(F32)
    gprob = jax.nn.softmax(glog, -1)
    _, g_sel = lax.top_k(glog, 1)
    p_g = jnp.take_along_axis(gprob, g_sel, 1)[:, 0]
    elog = (ht @ w_rexp + b_rexp).astype(F32).reshape(M, N_GROUPS, EXPERTS_PER_GROUP)
    elog_g = jnp.take_along_axis(elog, g_sel[:, :, None], 1)[:, 0]
    top_v, top_i = lax.top_k(elog_g, TOP_K)
    gate = (p_g[:, None] * jax.nn.softmax(top_v, -1)).reshape(-1)
    eid = (g_sel * EXPERTS_PER_GROUP + top_i).reshape(-1)
    tok = jnp.repeat(jnp.arange(M), TOP_K)
    order = jnp.argsort(eid)
    eid_s, tok_s, gate_s = eid[order], tok[order], gate[order]
    counts = jnp.bincount(eid, length=N_EXPERTS)
    starts = jnp.cumsum(counts) - counts
    pcounts = (counts + MOE_BLOCK - 1) // MOE_BLOCK * MOE_BLOCK
    pends = jnp.cumsum(pcounts)
    pstarts = pends - pcounts
    dest = pstarts[eid_s] + (jnp.arange(A) - starts[eid_s])
    P = (A + N_EXPERTS * (MOE_BLOCK - 1) + MOE_BLOCK - 1) // MOE_BLOCK * MOE_BLOCK
    nblk = P // MOE_BLOCK
    buf = jnp.zeros((P, D), h.dtype).at[dest].set(ht[tok_s])
    blk_e = jnp.minimum(jnp.searchsorted(pends, jnp.arange(nblk) * MOE_BLOCK, side='right'), N_EXPERTS - 1)

    def expert_block(args):
        xblk, e = args
        return (jax.nn.silu(xblk @ w_gate[e]) * (xblk @ w_up[e])) @ w_down[e]

    ybuf = lax.map(expert_block, (buf.reshape(nblk, MOE_BLOCK, D), blk_e)).reshape(P, D)
    y = (ybuf[dest].astype(F32) * gate_s[:, None]).astype(h.dtype)
    out = jnp.zeros((M, D), h.dtype).at[tok_s].add(y)
    return out.reshape(Bs, T, D)


def setup_inputs(seed: int = 0) -> dict:
    key = jax.random.key(seed)
    ks = jax.random.split(key, 40)
    nrm = lambda k, shape, s: jax.random.normal(k, shape, F32) * s
    L, D = DEPTH, D_MODEL
    col_scale = jnp.ones((N_IN,), F32).at[OFF_V:OFF_W].set(DEEPNORM_BETA)
    u_lam = jax.random.uniform(ks[20], (L, LRU_W), F32, 0.9, 0.999) ** (1.0 / LRU_C)
    return {
        "x": nrm(ks[0], (BATCH, SEQ, D), 1.0),
        "meta": nrm(ks[1], (N_META, D), 1.0),
        "ln0_g": 1.0 + nrm(ks[2], (D,), 0.02),
        "ln0_b": nrm(ks[3], (D,), 0.02),
        "w_in": nrm(ks[4], (L, D, N_IN), D ** -0.5) * col_scale,
        "mu_shift": jax.random.uniform(ks[5], (L, RWKV_COLS), F32),
        "w0": jnp.linspace(-6.5, -1.5, RWKV_W, dtype=F32)[None] + nrm(ks[6], (L, RWKV_W), 0.1),
        "w_decay_up": nrm(ks[7], (L, DECAY_RANK, RWKV_W), 0.1 * DECAY_RANK ** -0.5),
        "a0": nrm(ks[8], (L, RWKV_W), 0.1),
        "w_a_up": nrm(ks[9], (L, AAA_RANK, RWKV_W), 0.1 * AAA_RANK ** -0.5),
        "w_g_up": nrm(ks[10], (L, GATE_RANK, RWKV_W), GATE_RANK ** -0.5),
        "k_k": 0.85 + nrm(ks[11], (L, RWKV_W), 0.02),
        "k_a": 1.0 + nrm(ks[12], (L, RWKV_W), 0.02),
        "r_k": nrm(ks[13], (L, RWKV_HEADS, RWKV_HEAD), 0.1),
        "gn_g": 1.0 + nrm(ks[14], (L, RWKV_W), 0.02),
        "gn_b": nrm(ks[15], (L, RWKV_W), 0.02),
        "conv_w": nrm(ks[16], (L, CONV_WIDTH, LRU_W), CONV_WIDTH ** -0.5),
        "conv_b": nrm(ks[17], (L, LRU_W), 0.02),
        "w_rg": nrm(ks[18], (L, LRU_BLOCKS, LRU_BLOCK, LRU_BLOCK), LRU_BLOCK ** -0.5),
        "b_rg": nrm(ks[19], (L, LRU_W), 0.02),
        "w_ig": nrm(ks[21], (L, LRU_BLOCKS, LRU_BLOCK, LRU_BLOCK), LRU_BLOCK ** -0.5),
        "b_ig": nrm(ks[22], (L, LRU_W), 0.02),
        "lru_lambda": jnp.log(u_lam) - jnp.log1p(-u_lam),
        "w_out": nrm(ks[23], (L, D_MIX, D), D_MIX ** -0.5 * DEEPNORM_BETA),
        "ln1_g": 1.0 + nrm(ks[24], (L, D), 0.02),
        "ln1_b": nrm(ks[25], (L, D), 0.02),
        "w_router_grp": nrm(ks[26], (L, D, N_GROUPS), D ** -0.5),
        "b_router_grp": nrm(ks[27], (L, N_GROUPS), 0.01),
        "w_router_exp": nrm(ks[28], (L, D, N_EXPERTS), D ** -0.5),
        "b_router_exp": nrm(ks[29], (L, N_EXPERTS), 0.01),
        "w_exp_gate": nrm(ks[30], (L, N_EXPERTS, D, D_EXPERT), D ** -0.5),
        "w_exp_up": nrm(ks[31], (L, N_EXPERTS, D, D_EXPERT), D ** -0.5),
        "w_exp_down": nrm(ks[32], (L, N_EXPERTS, D_EXPERT, D), D_EXPERT ** -0.5 * DEEPNORM_BETA),
        "ln2_g": 1.0 + nrm(ks[33], (L, D), 0.02),
        "ln2_b": nrm(ks[34], (L, D), 0.02),
    }


def reference(x, meta, ln0_g, ln0_b, w_in, mu_shift, w0, w_decay_up, a0, w_a_up, w_g_up, k_k, k_a, r_k,
              gn_g, gn_b, conv_w, conv_b, w_rg, b_rg, w_ig, b_ig, lru_lambda, w_out, ln1_g, ln1_b,
              w_router_grp, b_router_grp, w_router_exp, b_router_exp, w_exp_gate, w_exp_up, w_exp_down,
              ln2_g, ln2_b):
    Bs = x.shape[0]
    h = jnp.concatenate([jnp.broadcast_to(meta[None].astype(x.dtype), (Bs, N_META, D_MODEL)), x], axis=1)
    h = layer_norm(h, ln0_g, ln0_b)
    for l in range(DEPTH):
        mix = token_mixers(h, w_in[l], mu_shift[l], w0[l], w_decay_up[l], a0[l], w_a_up[l], w_g_up[l],
                           k_k[l], k_a[l], r_k[l], gn_g[l], gn_b[l], conv_w[l], conv_b[l], w_rg[l], b_rg[l],
                           w_ig[l], b_ig[l], lru_lambda[l], w_out[l])
        h = layer_norm(DEEPNORM_ALPHA * h + mix, ln1_g[l], ln1_b[l])
        ffn = hierarchical_moe(h, w_router_grp[l], b_router_grp[l], w_router_exp[l], b_router_exp[l],
                               w_exp_gate[l], w_exp_up[l], w_exp_down[l])
        h = layer_norm(DEEPNORM_ALPHA * h + ffn, ln2_g[l], ln2_b[l])
    return h[:, N_META:]
```

```python
import functools
import math

import jax
import jax.numpy as jnp
from jax import lax
from jax.experimental import pallas as pl
from jax.experimental.pallas import tpu as pltpu

F32 = jnp.float32
BF16 = jnp.bfloat16

D_MODEL = 1024
N_META = 16
RWKV_W = 512
RWKV_HEAD = 64
DECAY_RANK = 64
AAA_RANK = 64
GATE_RANK = 160
LRU_W = 512
LRU_BLOCKS = 8
CONV_WIDTH = 4
LRU_C = 8.0
N_GROUPS = 4
EXPERTS_PER_GROUP = 8
N_EXPERTS = N_GROUPS * EXPERTS_PER_GROUP
TOP_K = 2
D_EXPERT = 512
LN_EPS = 1e-5
GN_EPS = 64e-5
DEEPNORM_ALPHA = 2.0 ** 0.25

LANE = 128
OFF_R, OFF_K, OFF_V = 0, RWKV_W, 2 * RWKV_W
OFF_ZW = 3 * RWKV_W
OFF_ZA = OFF_ZW + LANE
OFF_ZG = OFF_ZA + LANE
ZG_SLOT = 2 * LANE
UR_W = OFF_ZG + ZG_SLOT
UL_W = 2 * LRU_W

TAIL = 128
CHUNK = 64
HEADS_PER_GROUP = 4
GW = HEADS_PER_GROUP * RWKV_HEAD
N_HGROUPS = RWKV_W // GW
LRU_TILE = 128
MOE_TILE = 256
VMEM_LIMIT = 56 * 1024 * 1024


def _cparams(sem):
    return pltpu.CompilerParams(dimension_semantics=sem, vmem_limit_bytes=VMEM_LIMIT)


def _layer_norm(x, g, b):
    mu = jnp.mean(x, -1, keepdims=True)
    xc = x - mu
    var = jnp.mean(xc * xc, -1, keepdims=True)
    return xc * lax.rsqrt(var + LN_EPS) * g + b


def _dot(a, b):
    return jnp.dot(a, b, preferred_element_type=F32)


def _dot_hi(a, b):
    return jnp.dot(a, b, precision=lax.Precision.HIGHEST, preferred_element_type=F32)


def _dot_nt(a, b):
    return lax.dot_general(a, b, (((1,), (1,)), ((), ())), preferred_element_type=F32)


def _dot_tn(a, b):
    return lax.dot_general(a, b, (((0,), (0,)), ((), ())), preferred_element_type=F32)


def _split_dot(x, w_bf16):
    hi = x.astype(BF16)
    lo = (x - hi.astype(F32)).astype(BF16)
    return _dot(hi, w_bf16) + _dot(lo, w_bf16)


def _inproj_kernel(x_ref, g_ref, b_ref, wr_ref, wl_ref, ur_ref, ul_ref):
    h = _layer_norm(x_ref[0], g_ref[...], b_ref[...]).astype(BF16)
    ur_ref[0] = _dot(h, wr_ref[...])
    ul_ref[0] = _dot(h, wl_ref[...])


def _inproj_tail_kernel(x_ref, g_ref, b_ref, wr_ref, wl_ref, ur_in, ul_in, ur_ref, ul_ref):
    del ur_in, ul_in
    h = _layer_norm(x_ref[...], g_ref[...], b_ref[...]).astype(BF16)
    rows = lax.broadcasted_iota(jnp.int32, (TAIL, 1), 0)
    valid = (rows >= TAIL - N_META).astype(F32)
    ur_ref[0] = _dot(h, wr_ref[...]) * valid
    ul_ref[0] = _dot(h, wl_ref[...]) * valid


def _in_projection(x, meta, ln0_g, ln0_b, w_r, w_l):
    B, T, D = x.shape
    tm = 512
    t_pad = T + TAIL
    vec = lambda n: pl.BlockSpec((1, n), lambda *_: (0, 0))
    full = lambda a: pl.BlockSpec(a.shape, lambda *_: (0,) * a.ndim)
    ur, ul = pl.pallas_call(
        _inproj_kernel,
        out_shape=(jax.ShapeDtypeStruct((B, t_pad, UR_W), F32), jax.ShapeDtypeStruct((B, t_pad, UL_W), F32)),
        grid=(B, T // tm),
        in_specs=[pl.BlockSpec((1, tm, D), lambda b, i: (b, i, 0)), vec(D), vec(D), full(w_r), full(w_l)],
        out_specs=(pl.BlockSpec((1, tm, UR_W), lambda b, i: (b, i, 0)),
                   pl.BlockSpec((1, tm, UL_W), lambda b, i: (b, i, 0))),
        compiler_params=_cparams(("parallel", "parallel")),
        name="inproj",
    )(x, ln0_g, ln0_b, w_r, w_l)
    tail_x = jnp.concatenate([jnp.zeros((TAIL - N_META, D), F32), meta.astype(F32)], axis=0)
    any_spec = pl.BlockSpec(memory_space=pl.ANY)
    ur, ul = pl.pallas_call(
        _inproj_tail_kernel,
        out_shape=(jax.ShapeDtypeStruct(ur.shape, F32), jax.ShapeDtypeStruct(ul.shape, F32)),
        grid=(B,),
        in_specs=[full(tail_x), vec(D), vec(D), full(w_r), full(w_l), any_spec, any_spec],
        out_specs=(pl.BlockSpec((1, TAIL, UR_W), lambda b: (b, T // TAIL, 0)),
                   pl.BlockSpec((1, TAIL, UL_W), lambda b: (b, T // TAIL, 0))),
        input_output_aliases={5: 0, 6: 1},
        compiler_params=_cparams(("arbitrary",)),
        name="inproj_tail",
    )(tail_x, ln0_g, ln0_b, w_r, w_l, ur, ul)
    return ur, ul


def _rwkv_kernel(u_ref, mu_ref, w0_ref, wdu_ref, a0_ref, wau_ref, wgu_ref, kk_ref, ka_ref, rk_ref,
                 gng_ref, gnb_ref, bones_ref, tril_ref, eye_ref, bm_ref, msl_ref, mil_ref,
                 m8_ref, m16_ref, m32_ref, m64_ref, y_ref, s_ref, prev_ref):
    c = pl.program_id(1)

    @pl.when(c == 0)
    def _():
        s_ref[...] = jnp.zeros_like(s_ref)
        prev_ref[...] = jnp.zeros_like(prev_ref)

    u = u_ref[0]
    row = lax.broadcasted_iota(jnp.int32, u.shape, 0)
    u_prev = jnp.where(row == 0, prev_ref[...], pltpu.roll(u, 1, 0))
    prev_ref[...] = u[CHUNK - 1:CHUNK, :]
    x = u + (u_prev - u) * mu_ref[...]
    r = x[:, OFF_R:OFF_R + RWKV_W]
    k = x[:, OFF_K:OFF_K + RWKV_W]
    v = x[:, OFF_V:OFF_V + RWKV_W]
    zw = x[:, OFF_ZW:OFF_ZW + LANE]
    za = x[:, OFF_ZA:OFF_ZA + LANE]
    zg = x[:, OFF_ZG:OFF_ZG + ZG_SLOT]

    bones = bones_ref[...]
    head_sum = lambda t: _split_dot(t, bones)

    z = w0_ref[...] + _dot_hi(jnp.tanh(zw), wdu_ref[...])
    logw = -math.exp(-0.5) * jax.nn.sigmoid(z)
    a = jax.nn.sigmoid(a0_ref[...] + _dot_hi(za, wau_ref[...]))
    g = _dot(jax.nn.sigmoid(zg).astype(BF16), wgu_ref[...])
    kk = k * kk_ref[...]
    kk = kk / jnp.maximum(jnp.sqrt(head_sum(kk * kk)), 1e-12)
    k = k * (1.0 + (a - 1.0) * ka_ref[...])
    kka = kk * a

    cl = _dot_hi(tril_ref[...], logw)
    cl_last = cl[CHUNK - 1:CHUNK, :]
    e_neg = jnp.exp(-cl)
    e_end = jnp.exp(cl_last - cl)
    rt = r * jnp.exp(cl)
    kt = k * e_neg
    at = -kk * jnp.exp(cl - logw)
    bt = kka * e_neg
    kw = k * e_end
    bw = kka * e_end
    w_end = jnp.exp(cl_last)

    bm = bm_ref[...]
    eye = eye_ref[...]
    tile4 = lambda t: jnp.concatenate([t] * HEADS_PER_GROUP, axis=0)
    fold4 = lambda t: sum(t[i * CHUNK:(i + 1) * CHUNK] for i in range(HEADS_PER_GROUP))
    ys = []
    for hg in range(N_HGROUPS):
        sl = slice(hg * GW, (hg + 1) * GW)
        v4 = v[:, sl]
        lhs = jnp.concatenate([tile4(at[:, sl]) * bm, tile4(rt[:, sl]) * bm], axis=0).astype(BF16)
        rhs = jnp.concatenate([tile4(bt[:, sl]), tile4(kt[:, sl])], axis=0).astype(BF16)
        aa = _dot_nt(lhs, rhs)
        a_ab = aa[:GW, :GW] * msl_ref[...]
        a_ak = (aa[:GW, GW:] * msl_ref[...]).astype(BF16)
        a_rb = (aa[GW:, :GW] * mil_ref[...]).astype(BF16)
        a_rk = (aa[GW:, GW:] * mil_ref[...]).astype(BF16)

        a0 = (a_ab * m8_ref[...]).astype(BF16)
        a2 = _dot(a0, a0).astype(BF16)
        a4 = _dot(a2, a2).astype(BF16)
        p1 = eye + a0.astype(F32)
        p1 = p1 + _dot(p1.astype(BF16), a2)
        t = p1 + _dot(p1.astype(BF16), a4)
        for m_ref in (m16_ref, m32_ref, m64_ref):
            tb = t.astype(BF16)
            off = (a_ab * m_ref[...]).astype(BF16)
            t = t + _dot(_dot(tb, off).astype(BF16), tb)
        tb = t.astype(BF16)

        s = s_ref[hg]
        sb = s.astype(BF16)
        vt = tile4(v4).astype(BF16)
        xx = _dot_nt(lhs[:GW], sb) + _dot(a_ak, vt)
        uu = _dot(tb, xx.astype(BF16)) * bm
        yy = (_dot_nt(lhs[GW:], sb) + _dot(a_rb, uu.astype(BF16)) + _dot(a_rk, vt)) * bm
        ys.append(fold4(yy))
        u4 = fold4(uu)
        upd = _dot_tn(jnp.concatenate([u4, v4], axis=0).astype(BF16),
                      jnp.concatenate([bw[:, sl], kw[:, sl]], axis=0).astype(BF16))
        s_ref[hg] = s * w_end[:, sl] + upd * bm

    y = jnp.concatenate(ys, axis=1)
    inv_n = 1.0 / RWKV_HEAD
    ym = head_sum(y) * inv_n
    yc = y - ym
    yv = head_sum(yc * yc) * inv_n
    yn = yc * lax.rsqrt(yv + GN_EPS) * gng_ref[...] + gnb_ref[...]
    bonus = head_sum(r * k * rk_ref[...]) * v
    y_ref[0] = ((yn + bonus) * g).astype(y_ref.dtype)


def _rwkv_masks():
    i = jnp.arange(GW)[:, None]
    j = jnp.arange(GW)[None, :]
    same = lambda n: (i // n) == (j // n)
    f = lambda m: m.astype(F32)
    bm = f(same(RWKV_HEAD))
    msl = f(same(RWKV_HEAD) & (i > j))
    mil = f(same(RWKV_HEAD) & (i >= j))
    m8 = f(same(8))
    m16 = f(same(16) & ~same(8))
    m32 = f(same(32) & ~same(16))
    m64 = f(same(64) & ~same(32))
    eye = f(i == j)
    ti = jnp.arange(CHUNK)
    tril = f(ti[:, None] >= ti[None, :])
    hi = jnp.arange(RWKV_W)
    bones = ((hi[:, None] // RWKV_HEAD) == (hi[None, :] // RWKV_HEAD)).astype(BF16)
    return bones, tril, eye, bm, msl, mil, m8, m16, m32, m64


def _rwkv_mixer(ur, params, n_x_chunks):
    B, t_pad, _ = ur.shape
    n_chunks = n_x_chunks + 1
    tail_chunk = t_pad // CHUNK - 1
    consts = _rwkv_masks()
    full = lambda a: pl.BlockSpec(a.shape, lambda *_: (0,) * a.ndim)
    chunk_map = lambda b, c: (b, jnp.where(c == 0, tail_chunk, c - 1), 0)
    return pl.pallas_call(
        _rwkv_kernel,
        out_shape=jax.ShapeDtypeStruct((B, t_pad, RWKV_W), BF16),
        grid=(B, n_chunks),
        in_specs=[pl.BlockSpec((1, CHUNK, UR_W), chunk_map)] + [full(p) for p in params] + [full(m) for m in consts],
        out_specs=pl.BlockSpec((1, CHUNK, RWKV_W), chunk_map),
        scratch_shapes=[pltpu.VMEM((N_HGROUPS, GW, GW), F32), pltpu.VMEM((1, UR_W), F32)],
        compiler_params=_cparams(("parallel", "arbitrary")),
        name="rwkv7",
    )(ur, *params, *consts)


def _gelu_tanh(x):
    return 0.5 * x * (1.0 + jnp.tanh(math.sqrt(2.0 / math.pi) * (x + 0.044715 * (x * x * x))))


def _lru_kernel(u_ref, cw_ref, cb_ref, wrg_ref, brg_ref, wig_ref, big_ref, lam_ref, y_ref, xprev_ref, hprev_ref):
    c = pl.program_id(1)

    @pl.when(c == 0)
    def _():
        xprev_ref[...] = jnp.zeros_like(xprev_ref)
        hprev_ref[...] = jnp.zeros_like(hprev_ref)

    xl = u_ref[0, :, :LRU_W]
    gl = u_ref[0, :, LRU_W:]
    row = lax.broadcasted_iota(jnp.int32, (LRU_TILE, LRU_W), 0)
    row8 = lax.broadcasted_iota(jnp.int32, (8, LRU_W), 0)
    xprev = xprev_ref[...]
    xc = cb_ref[...] + cw_ref[CONV_WIDTH - 1:CONV_WIDTH, :] * xl
    for d in range(1, CONV_WIDTH):
        rolled = pltpu.roll(xl, d, 0)
        head = jnp.where(row8 < d, pltpu.roll(xprev, d, 0), rolled[:8])
        shifted = jnp.concatenate([head, rolled[8:]], axis=0)
        xc = xc + cw_ref[CONV_WIDTH - 1 - d:CONV_WIDTH - d, :] * shifted
    xprev_ref[...] = xl[LRU_TILE - 8:, :]

    xcb = xc.astype(BF16)
    gate_r = jax.nn.sigmoid(_dot(xcb, wrg_ref[...]) + brg_ref[...])
    gate_i = jax.nn.sigmoid(_dot(xcb, wig_ref[...]) + big_ref[...])
    lam = lam_ref[...]
    log_sig = -(jnp.maximum(-lam, 0.0) + jnp.log1p(jnp.exp(-jnp.abs(lam))))
    log_a = LRU_C * gate_r * log_sig
    a = jnp.exp(log_a)
    mult = jnp.sqrt(jnp.maximum(1.0 - jnp.exp(2.0 * log_a), 0.0))
    b = mult * gate_i * xc
    b = jnp.where((c == 0) & (row < LRU_TILE - N_META), 0.0, b)

    d = 1
    while d < LRU_TILE:
        keep = row >= d
        a_sh = jnp.where(keep, pltpu.roll(a, d, 0), 1.0)
        b_sh = jnp.where(keep, pltpu.roll(b, d, 0), 0.0)
        b = a * b_sh + b
        a = a * a_sh
        d *= 2
    h = b + a * hprev_ref[...]
    hprev_ref[...] = h[LRU_TILE - 1:, :]
    y_ref[0] = (h * _gelu_tanh(gl)).astype(y_ref.dtype)


def _lru_mixer(ul, params, n_x_tiles):
    B, t_pad, _ = ul.shape
    tail_tile = t_pad // LRU_TILE - 1
    full = lambda a: pl.BlockSpec(a.shape, lambda *_: (0,) * a.ndim)
    tile_map = lambda b, c: (b, jnp.where(c == 0, tail_tile, c - 1), 0)
    return pl.pallas_call(
        _lru_kernel,
        out_shape=jax.ShapeDtypeStruct((B, t_pad, LRU_W), BF16),
        grid=(B, n_x_tiles + 1),
        in_specs=[pl.BlockSpec((1, LRU_TILE, UL_W), tile_map)] + [full(p) for p in params],
        out_specs=pl.BlockSpec((1, LRU_TILE, LRU_W), tile_map),
        scratch_shapes=[pltpu.VMEM((8, LRU_W), F32), pltpu.VMEM((1, LRU_W), F32)],
        compiler_params=_cparams(("parallel", "arbitrary")),
        name="rglru",
    )(ul, *params)


def _outproj_kernel(x_ref, yr_ref, yl_ref, g0_ref, b0_ref, wor_ref, wol_ref, g1_ref, b1_ref,
                    wrt_hi_ref, wrt_lo_ref, brt_ref, h1_ref, h1b_ref, lg_ref):
    h0 = _layer_norm(x_ref[0], g0_ref[...], b0_ref[...])
    mix = _dot(yr_ref[0], wor_ref[...]) + _dot(yl_ref[0], wol_ref[...])
    h1 = _layer_norm(DEEPNORM_ALPHA * h0 + mix, g1_ref[...], b1_ref[...])
    h1_ref[0] = h1
    h1b_ref[0] = h1.astype(BF16)
    hi = h1.astype(BF16)
    lo = (h1 - hi.astype(F32)).astype(BF16)
    w_hi = wrt_hi_ref[...]
    lg_ref[0] = _dot(hi, w_hi) + (_dot(hi, wrt_lo_ref[...]) + _dot(lo, w_hi)) + brt_ref[...]


def _out_projection(x, y_rwkv, y_lru, ln0_g, ln0_b, wo_r, wo_l, ln1_g, ln1_b, wrt_hi, wrt_lo, brt):
    B, T, D = x.shape
    tm = 512
    vec = lambda n: pl.BlockSpec((1, n), lambda *_: (0, 0))
    full = lambda a: pl.BlockSpec(a.shape, lambda *_: (0,) * a.ndim)
    rows = lambda w: pl.BlockSpec((1, tm, w), lambda b, i: (b, i, 0))
    return pl.pallas_call(
        _outproj_kernel,
        out_shape=(jax.ShapeDtypeStruct((B, T, D), F32), jax.ShapeDtypeStruct((B, T, D), BF16),
                   jax.ShapeDtypeStruct((B, T, LANE), F32)),
        grid=(B, T // tm),
        in_specs=[rows(D), rows(RWKV_W), rows(LRU_W), vec(D), vec(D), full(wo_r), full(wo_l), vec(D), vec(D),
                  full(wrt_hi), full(wrt_lo), vec(LANE)],
        out_specs=(rows(D), rows(D), rows(LANE)),
        compiler_params=_cparams(("parallel", "parallel")),
        name="outproj",
    )(x, y_rwkv, y_lru, ln0_g, ln0_b, wo_r, wo_l, ln1_g, ln1_b, wrt_hi, wrt_lo, brt)


def _moe_kernel(te_ref, nv_ref, x_ref, wg_ref, wu_ref, wd_ref, o_ref, wgb_ref, wub_ref, wdb_ref):
    i = pl.program_id(0)
    e = te_ref[i]
    e_prev = te_ref[jnp.maximum(i - 1, 0)]

    @pl.when((i == 0) | (e != e_prev))
    def _():
        wgb_ref[...] = wg_ref[0].astype(BF16)
        wub_ref[...] = wu_ref[0].astype(BF16)
        wdb_ref[...] = wd_ref[0].astype(BF16)

    @pl.when(i < nv_ref[0])
    def _():
        xb = x_ref[...]
        hg = _dot(xb, wgb_ref[...])
        hu = _dot(xb, wub_ref[...])
        mid = (hg * jax.nn.sigmoid(hg) * hu).astype(BF16)
        o_ref[...] = _dot(mid, wdb_ref[...]).astype(o_ref.dtype)

    @pl.when(i >= nv_ref[0])
    def _():
        o_ref[...] = jnp.zeros_like(o_ref)


def _moe_experts(xbuf, tile_expert, n_valid, w_gate, w_up, w_down):
    P, D = xbuf.shape
    n_tiles = P // MOE_TILE
    grid_spec = pltpu.PrefetchScalarGridSpec(
        num_scalar_prefetch=2,
        grid=(n_tiles,),
        in_specs=[pl.BlockSpec((MOE_TILE, D), lambda i, te, nv: (i, 0)),
                  pl.BlockSpec((1, D, D_EXPERT), lambda i, te, nv: (te[i], 0, 0)),
                  pl.BlockSpec((1, D, D_EXPERT), lambda i, te, nv: (te[i], 0, 0)),
                  pl.BlockSpec((1, D_EXPERT, D), lambda i, te, nv: (te[i], 0, 0))],
        out_specs=pl.BlockSpec((MOE_TILE, D), lambda i, te, nv: (i, 0)),
        scratch_shapes=[pltpu.VMEM((D, D_EXPERT), BF16), pltpu.VMEM((D, D_EXPERT), BF16),
                        pltpu.VMEM((D_EXPERT, D), BF16)],
    )
    return pl.pallas_call(
        _moe_kernel,
        out_shape=jax.ShapeDtypeStruct((P, D), BF16),
        grid_spec=grid_spec,
        compiler_params=_cparams(("arbitrary",)),
        name="moe_experts",
    )(tile_expert, n_valid, xbuf, w_gate, w_up, w_down)


def _combine_kernel(h_ref, y_ref, gate_ref, g_ref, b_ref, o_ref):
    D = h_ref.shape[-1]
    gate = gate_ref[...]
    ffn = gate[:, 0:1] * y_ref[:, :D].astype(F32) + gate[:, 1:2] * y_ref[:, D:].astype(F32)
    o_ref[...] = _layer_norm(DEEPNORM_ALPHA * h_ref[...] + ffn, g_ref[...], b_ref[...])


def _combine(h1, ypair, gates, ln2_g, ln2_b):
    M, D = h1.shape
    tm = 512
    vec = lambda n: pl.BlockSpec((1, n), lambda *_: (0, 0))
    return pl.pallas_call(
        _combine_kernel,
        out_shape=jax.ShapeDtypeStruct((M, D), F32),
        grid=(M // tm,),
        in_specs=[pl.BlockSpec((tm, D), lambda i: (i, 0)), pl.BlockSpec((tm, TOP_K * D), lambda i: (i, 0)),
                  pl.BlockSpec((tm, TOP_K), lambda i: (i, 0)), vec(D), vec(D)],
        out_specs=pl.BlockSpec((tm, D), lambda i: (i, 0)),
        compiler_params=_cparams(("parallel",)),
        name="combine",
    )(h1, ypair, gates, ln2_g, ln2_b)


def _routing_plan(logits):
    M = logits.shape[0]
    glog = logits[:, :N_GROUPS]
    elog = logits[:, N_GROUPS:N_GROUPS + N_EXPERTS].reshape(M, N_GROUPS, EXPERTS_PER_GROUP)
    gprob = jax.nn.softmax(glog, -1)
    _, g_sel = lax.top_k(glog, 1)
    p_g = jnp.take_along_axis(gprob, g_sel, 1)[:, 0]
    elog_g = jnp.take_along_axis(elog, g_sel[:, :, None], 1)[:, 0]
    top_v, top_i = lax.top_k(elog_g, TOP_K)
    gates = p_g[:, None] * jax.nn.softmax(top_v, -1)
    eid = (g_sel * EXPERTS_PER_GROUP + top_i).reshape(-1)
    A = M * TOP_K
    onehot = (eid[:, None] == jnp.arange(N_EXPERTS, dtype=eid.dtype)[None, :]).astype(jnp.int32)
    csum = jnp.cumsum(onehot, axis=0)
    rank = jnp.take_along_axis(csum, eid[:, None], 1)[:, 0] - 1
    counts = csum[-1]
    pcounts = (counts + MOE_TILE - 1) // MOE_TILE * MOE_TILE
    pends = jnp.cumsum(pcounts)
    pstarts = pends - pcounts
    dest = (pstarts[eid] + rank).astype(jnp.int32)
    n_tiles = (A + N_EXPERTS * (MOE_TILE - 1) + MOE_TILE - 1) // MOE_TILE
    n_valid = (pends[-1] // MOE_TILE).astype(jnp.int32)
    tile_start = jnp.arange(n_tiles, dtype=jnp.int32) * MOE_TILE
    tile_expert = jnp.searchsorted(pends, jnp.minimum(tile_start, pends[-1] - 1), side='right')
    tile_expert = jnp.minimum(tile_expert, N_EXPERTS - 1).astype(jnp.int32)
    tok = jnp.arange(A, dtype=jnp.int32) // TOP_K
    row_tok = jnp.zeros((n_tiles * MOE_TILE,), jnp.int32).at[dest].set(tok)
    return gates, dest, row_tok, tile_expert, n_valid.reshape(1)


def kernel(x, meta, ln0_g, ln0_b, w_in, mu_shift, w0, w_decay_up, a0, w_a_up, w_g_up, k_k, k_a, r_k, gn_g, gn_b, conv_w, conv_b, w_rg, b_rg, w_ig, b_ig, lru_lambda, w_out, ln1_g, ln1_b, w_router_grp, b_router_grp, w_router_exp, b_router_exp, w_exp_gate, w_exp_up, w_exp_down, ln2_g, ln2_b):
    B, T, D = x.shape
    assert D == D_MODEL and T % 512 == 0 and w_in.shape[0] == 1
    row = lambda p: p.reshape(1, -1).astype(F32)
    n_rw = 3 * RWKV_W
    w_in0 = w_in[0]

    def slots(p, fill=0.0):
        pad = lambda a, n: jnp.pad(a, [(0, 0)] * (a.ndim - 1) + [(0, n - a.shape[-1])], constant_values=fill)
        zw = p[..., n_rw:n_rw + DECAY_RANK]
        za = p[..., n_rw + DECAY_RANK:n_rw + DECAY_RANK + AAA_RANK]
        zg = p[..., n_rw + DECAY_RANK + AAA_RANK:n_rw + DECAY_RANK + AAA_RANK + GATE_RANK]
        return jnp.concatenate([p[..., :n_rw], pad(zw, LANE), pad(za, LANE), pad(zg, ZG_SLOT)], axis=-1)

    rwkv_cols = n_rw + DECAY_RANK + AAA_RANK + GATE_RANK
    w_r = slots(w_in0[:, :rwkv_cols]).astype(BF16)
    w_l = w_in0[:, rwkv_cols:].astype(BF16)
    ur, ul = _in_projection(x, meta, row(ln0_g), row(ln0_b), w_r, w_l)

    pad_rows = lambda a, n: jnp.pad(a, ((0, n - a.shape[0]), (0, 0)))
    rwkv_params = (slots(mu_shift[0][None, :]).astype(F32), row(w0[0]), pad_rows(w_decay_up[0], LANE),
                   row(a0[0]), pad_rows(w_a_up[0], LANE), pad_rows(w_g_up[0], ZG_SLOT).astype(BF16),
                   row(k_k[0]), row(k_a[0]), row(r_k[0]), row(gn_g[0]), row(gn_b[0]))
    y_rwkv = _rwkv_mixer(ur, rwkv_params, T // CHUNK)

    blockdiag = lambda w: jax.scipy.linalg.block_diag(*[w[i] for i in range(LRU_BLOCKS)]).astype(BF16)
    lru_params = (conv_w[0], row(conv_b[0]), blockdiag(w_rg[0]), row(b_rg[0]), blockdiag(w_ig[0]), row(b_ig[0]),
                  row(lru_lambda[0]))
    y_lru = _lru_mixer(ul, lru_params, T // LRU_TILE)

    w_rt = jnp.concatenate([w_router_grp[0], w_router_exp[0]], axis=1)
    w_rt = jnp.pad(w_rt, ((0, 0), (0, LANE - w_rt.shape[1])))
    wrt_hi = w_rt.astype(BF16)
    wrt_lo = (w_rt - wrt_hi.astype(F32)).astype(BF16)
    b_rt = jnp.concatenate([b_router_grp[0], b_router_exp[0]])
    b_rt = jnp.pad(b_rt, (0, LANE - b_rt.shape[0])).reshape(1, LANE)
    wo = w_out[0].astype(BF16)
    h1, h1b, logits = _out_projection(x, y_rwkv, y_lru, row(ln0_g), row(ln0_b), wo[:RWKV_W], wo[RWKV_W:],
                                      row(ln1_g[0]), row(ln1_b[0]), wrt_hi, wrt_lo, b_rt)

    M = B * T
    gates, dest, row_tok, tile_expert, n_valid = _routing_plan(logits.reshape(M, LANE))
    xbuf = jnp.take(h1b.reshape(M, D), row_tok, axis=0)
    ybuf = _moe_experts(xbuf, tile_expert, n_valid, w_exp_gate[0], w_exp_up[0], w_exp_down[0])
    ypair = jnp.take(ybuf, dest, axis=0).reshape(M, TOP_K * D)
    out = _combine(h1.reshape(M, D), ypair, gates, row(ln2_g[0]), row(ln2_b[0]))
    return out.reshape(B, T, D)
```

```python
import functools
import math

import jax
import jax.numpy as jnp
from jax import lax
from jax.experimental import pallas as pl
from jax.experimental.pallas import tpu as pltpu

F32 = jnp.float32
BF16 = jnp.bfloat16

D_MODEL = 1024
N_META = 16
RWKV_W = 512
RWKV_HEAD = 64
DECAY_RANK = 64
AAA_RANK = 64
GATE_RANK = 160
LRU_W = 512
LRU_BLOCKS = 8
CONV_WIDTH = 4
LRU_C = 8.0
N_GROUPS = 4
EXPERTS_PER_GROUP = 8
N_EXPERTS = N_GROUPS * EXPERTS_PER_GROUP
TOP_K = 2
D_EXPERT = 512
LN_EPS = 1e-5
GN_EPS = 64e-5
DEEPNORM_ALPHA = 2.0 ** 0.25

LANE = 128
OFF_R, OFF_K, OFF_V = 0, RWKV_W, 2 * RWKV_W
OFF_ZW = 3 * RWKV_W
OFF_ZA = OFF_ZW + LANE
OFF_ZG = OFF_ZA + LANE
ZG_SLOT = 2 * LANE
UR_W = OFF_ZG + ZG_SLOT
UL_W = 2 * LRU_W

TAIL = 128
CHUNK = 64
HEADS_PER_GROUP = 4
GW = HEADS_PER_GROUP * RWKV_HEAD
N_HGROUPS = RWKV_W // GW
LRU_TILE = 128
MOE_TILE = 256
VMEM_LIMIT = 56 * 1024 * 1024


def _cparams(sem):
    return pltpu.CompilerParams(dimension_semantics=sem, vmem_limit_bytes=VMEM_LIMIT)


def _layer_norm(x, g, b):
    mu = jnp.mean(x, -1, keepdims=True)
    xc = x - mu
    var = jnp.mean(xc * xc, -1, keepdims=True)
    return xc * lax.rsqrt(var + LN_EPS) * g + b


def _dot(a, b):
    return jnp.dot(a, b, preferred_element_type=F32)


def _dot_hi(a, b):
    return jnp.dot(a, b, precision=lax.Precision.HIGHEST, preferred_element_type=F32)


def _dot_nt(a, b):
    return lax.dot_general(a, b, (((1,), (1,)), ((), ())), preferred_element_type=F32)


def _dot_tn(a, b):
    return lax.dot_general(a, b, (((0,), (0,)), ((), ())), preferred_element_type=F32)


def _split_dot(x, w_bf16):
    hi = x.astype(BF16)
    lo = (x - hi.astype(F32)).astype(BF16)
    return _dot(hi, w_bf16) + _dot(lo, w_bf16)


def _inproj_kernel(x_ref, g_ref, b_ref, wr_ref, wl_ref, ur_ref, ul_ref):
    h = _layer_norm(x_ref[0], g_ref[...], b_ref[...]).astype(BF16)
    ur_ref[0] = _dot(h, wr_ref[...])
    ul_ref[0] = _dot(h, wl_ref[...])


def _inproj_tail_kernel(x_ref, g_ref, b_ref, wr_ref, wl_ref, ur_in, ul_in, ur_ref, ul_ref):
    del ur_in, ul_in
    h = _layer_norm(x_ref[...], g_ref[...], b_ref[...]).astype(BF16)
    rows = lax.broadcasted_iota(jnp.int32, (TAIL, 1), 0)
    valid = (rows >= TAIL - N_META).astype(F32)
    ur_ref[0] = _dot(h, wr_ref[...]) * valid
    ul_ref[0] = _dot(h, wl_ref[...]) * valid


def _in_projection(x, meta, ln0_g, ln0_b, w_r, w_l):
    B, T, D = x.shape
    tm = 512
    t_pad = T + TAIL
    vec = lambda n: pl.BlockSpec((1, n), lambda *_: (0, 0))
    full = lambda a: pl.BlockSpec(a.shape, lambda *_: (0,) * a.ndim)
    ur, ul = pl.pallas_call(
        _inproj_kernel,
        out_shape=(jax.ShapeDtypeStruct((B, t_pad, UR_W), F32), jax.ShapeDtypeStruct((B, t_pad, UL_W), F32)),
        grid=(B, T // tm),
        in_specs=[pl.BlockSpec((1, tm, D), lambda b, i: (b, i, 0)), vec(D), vec(D), full(w_r), full(w_l)],
        out_specs=(pl.BlockSpec((1, tm, UR_W), lambda b, i: (b, i, 0)),
                   pl.BlockSpec((1, tm, UL_W), lambda b, i: (b, i, 0))),
        compiler_params=_cparams(("parallel", "parallel")),
        name="inproj",
    )(x, ln0_g, ln0_b, w_r, w_l)
    tail_x = jnp.concatenate([jnp.zeros((TAIL - N_META, D), F32), meta.astype(F32)], axis=0)
    any_spec = pl.BlockSpec(memory_space=pl.ANY)
    ur, ul = pl.pallas_call(
        _inproj_tail_kernel,
        out_shape=(jax.ShapeDtypeStruct(ur.shape, F32), jax.ShapeDtypeStruct(ul.shape, F32)),
        grid=(B,),
        in_specs=[full(tail_x), vec(D), vec(D), full(w_r), full(w_l), any_spec, any_spec],
        out_specs=(pl.BlockSpec((1, TAIL, UR_W), lambda b: (b, T // TAIL, 0)),
                   pl.BlockSpec((1, TAIL, UL_W), lambda b: (b, T // TAIL, 0))),
        input_output_aliases={5: 0, 6: 1},
        compiler_params=_cparams(("arbitrary",)),
        name="inproj_tail",
    )(tail_x, ln0_g, ln0_b, w_r, w_l, ur, ul)
    return ur, ul


def _rwkv_kernel(u_ref, mu_ref, w0_ref, wdu_ref, a0_ref, wau_ref, wgu_ref, kk_ref, ka_ref, rk_ref,
                 gng_ref, gnb_ref, bones_ref, tril_ref, eye_ref, bm_ref, msl_ref, mil_ref,
                 m8_ref, m16_ref, m32_ref, m64_ref, y_ref, s_ref, prev_ref):
    c = pl.program_id(1)

    @pl.when(c == 0)
    def _():
        s_ref[...] = jnp.zeros_like(s_ref)
        prev_ref[...] = jnp.zeros_like(prev_ref)

    u = u_ref[0]
    row = lax.broadcasted_iota(jnp.int32, u.shape, 0)
    u_prev = jnp.where(row == 0, prev_ref[...], pltpu.roll(u, 1, 0))
    prev_ref[...] = u[CHUNK - 1:CHUNK, :]
    x = u + (u_prev - u) * mu_ref[...]
    r = x[:, OFF_R:OFF_R + RWKV_W]
    k = x[:, OFF_K:OFF_K + RWKV_W]
    v = x[:, OFF_V:OFF_V + RWKV_W]
    zw = x[:, OFF_ZW:OFF_ZW + LANE]
    za = x[:, OFF_ZA:OFF_ZA + LANE]
    zg = x[:, OFF_ZG:OFF_ZG + ZG_SLOT]

    bones = bones_ref[...]
    head_sum = lambda t: _split_dot(t, bones)

    z = w0_ref[...] + _dot_hi(jnp.tanh(zw), wdu_ref[...])
    logw = -math.exp(-0.5) * jax.nn.sigmoid(z)
    a = jax.nn.sigmoid(a0_ref[...] + _dot_hi(za, wau_ref[...]))
    g = _dot(jax.nn.sigmoid(zg).astype(BF16), wgu_ref[...])
    kk = k * kk_ref[...]
    kk = kk / jnp.maximum(jnp.sqrt(head_sum(kk * kk)), 1e-12)
    k = k * (1.0 + (a - 1.0) * ka_ref[...])
    kka = kk * a

    cl = _dot_hi(tril_ref[...], logw)
    cl_last = cl[CHUNK - 1:CHUNK, :]
    e_neg = jnp.exp(-cl)
    e_end = jnp.exp(cl_last - cl)
    rt = r * jnp.exp(cl)
    kt = k * e_neg
    at = -kk * jnp.exp(cl - logw)
    bt = kka * e_neg
    kw = k * e_end
    bw = kka * e_end
    w_end = jnp.exp(cl_last)

    bm = bm_ref[...]
    eye = eye_ref[...]
    tile4 = lambda t: jnp.concatenate([t] * HEADS_PER_GROUP, axis=0)
    fold4 = lambda t: sum(t[i * CHUNK:(i + 1) * CHUNK] for i in range(HEADS_PER_GROUP))
    ys = []
    for hg in range(N_HGROUPS):
        sl = slice(hg * GW, (hg + 1) * GW)
        v4 = v[:, sl]
        lhs = jnp.concatenate([tile4(at[:, sl]) * bm, tile4(rt[:, sl]) * bm], axis=0).astype(BF16)
        rhs = jnp.concatenate([tile4(bt[:, sl]), tile4(kt[:, sl])], axis=0).astype(BF16)
        aa = _dot_nt(lhs, rhs)
        a_ab = aa[:GW, :GW] * msl_ref[...]
        a_ak = (aa[:GW, GW:] * msl_ref[...]).astype(BF16)
        a_rb = (aa[GW:, :GW] * mil_ref[...]).astype(BF16)
        a_rk = (aa[GW:, GW:] * mil_ref[...]).astype(BF16)

        a0 = (a_ab * m8_ref[...]).astype(BF16)
        a2 = _dot(a0, a0).astype(BF16)
        a4 = _dot(a2, a2).astype(BF16)
        p1 = eye + a0.astype(F32)
        p1 = p1 + _dot(p1.astype(BF16), a2)
        t = p1 + _dot(p1.astype(BF16), a4)
        for m_ref in (m16_ref, m32_ref, m64_ref):
            tb = t.astype(BF16)
            off = (a_ab * m_ref[...]).astype(BF16)
            t = t + _dot(_dot(tb, off).astype(BF16), tb)
        tb = t.astype(BF16)

        s = s_ref[hg]
        sb = s.astype(BF16)
        vt = tile4(v4).astype(BF16)
        xx = _dot_nt(lhs[:GW], sb) + _dot(a_ak, vt)
        uu = _dot(tb, xx.astype(BF16)) * bm
        yy = (_dot_nt(lhs[GW:], sb) + _dot(a_rb, uu.astype(BF16)) + _dot(a_rk, vt)) * bm
        ys.append(fold4(yy))
        u4 = fold4(uu)
        upd = _dot_tn(jnp.concatenate([u4, v4], axis=0).astype(BF16),
                      jnp.concatenate([bw[:, sl], kw[:, sl]], axis=0).astype(BF16))
        s_ref[hg] = s * w_end[:, sl] + upd * bm

    y = jnp.concatenate(ys, axis=1)
    inv_n = 1.0 / RWKV_HEAD
    ym = head_sum(y) * inv_n
    yc = y - ym
    yv = head_sum(yc * yc) * inv_n
    yn = yc * lax.rsqrt(yv + GN_EPS) * gng_ref[...] + gnb_ref[...]
    bonus = head_sum(r * k * rk_ref[...]) * v
    y_ref[0] = ((yn + bonus) * g).astype(y_ref.dtype)


def _rwkv_masks():
    i = jnp.arange(GW)[:, None]
    j = jnp.arange(GW)[None, :]
    same = lambda n: (i // n) == (j // n)
    f = lambda m: m.astype(F32)
    bm = f(same(RWKV_HEAD))
    msl = f(same(RWKV_HEAD) & (i > j))
    mil = f(same(RWKV_HEAD) & (i >= j))
    m8 = f(same(8))
    m16 = f(same(16) & ~same(8))
    m32 = f(same(32) & ~same(16))
    m64 = f(same(64) & ~same(32))
    eye = f(i == j)
    ti = jnp.arange(CHUNK)
    tril = f(ti[:, None] >= ti[None, :])
    hi = jnp.arange(RWKV_W)
    bones = ((hi[:, None] // RWKV_HEAD) == (hi[None, :] // RWKV_HEAD)).astype(BF16)
    return bones, tril, eye, bm, msl, mil, m8, m16, m32, m64


def _rwkv_mixer(ur, params, n_x_chunks):
    B, t_pad, _ = ur.shape
    n_chunks = n_x_chunks + 1
    tail_chunk = t_pad // CHUNK - 1
    consts = _rwkv_masks()
    full = lambda a: pl.BlockSpec(a.shape, lambda *_: (0,) * a.ndim)
    chunk_map = lambda b, c: (b, jnp.where(c == 0, tail_chunk, c - 1), 0)
    return pl.pallas_call(
        _rwkv_kernel,
        out_shape=jax.ShapeDtypeStruct((B, t_pad, RWKV_W), BF16),
        grid=(B, n_chunks),
        in_specs=[pl.BlockSpec((1, CHUNK, UR_W), chunk_map)] + [full(p) for p in params] + [full(m) for m in consts],
        out_specs=pl.BlockSpec((1, CHUNK, RWKV_W), chunk_map),
        scratch_shapes=[pltpu.VMEM((N_HGROUPS, GW, GW), F32), pltpu.VMEM((1, UR_W), F32)],
        compiler_params=_cparams(("parallel", "arbitrary")),
        name="rwkv7",
    )(ur, *params, *consts)


def _gelu_tanh(x):
    return 0.5 * x * (1.0 + jnp.tanh(math.sqrt(2.0 / math.pi) * (x + 0.044715 * (x * x * x))))


def _lru_kernel(u_ref, cw_ref, cb_ref, wrg_ref, brg_ref, wig_ref, big_ref, lam_ref, y_ref, xprev_ref, hprev_ref):
    c = pl.program_id(1)

    @pl.when(c == 0)
    def _():
        xprev_ref[...] = jnp.zeros_like(xprev_ref)
        hprev_ref[...] = jnp.zeros_like(hprev_ref)

    xl = u_ref[0, :, :LRU_W]
    gl = u_ref[0, :, LRU_W:]
    row = lax.broadcasted_iota(jnp.int32, (LRU_TILE, LRU_W), 0)
    row8 = lax.broadcasted_iota(jnp.int32, (8, LRU_W), 0)
    xprev = xprev_ref[...]
    xc = cb_ref[...] + cw_ref[CONV_WIDTH - 1:CONV_WIDTH, :] * xl
    for d in range(1, CONV_WIDTH):
        rolled = pltpu.roll(xl, d, 0)
        head = jnp.where(row8 < d, pltpu.roll(xprev, d, 0), rolled[:8])
        shifted = jnp.concatenate([head, rolled[8:]], axis=0)
        xc = xc + cw_ref[CONV_WIDTH - 1 - d:CONV_WIDTH - d, :] * shifted
    xprev_ref[...] = xl[LRU_TILE - 8:, :]

    xcb = xc.astype(BF16)
    gate_r = jax.nn.sigmoid(_dot(xcb, wrg_ref[...]) + brg_ref[...])
    gate_i = jax.nn.sigmoid(_dot(xcb, wig_ref[...]) + big_ref[...])
    lam = lam_ref[...]
    log_sig = -(jnp.maximum(-lam, 0.0) + jnp.log1p(jnp.exp(-jnp.abs(lam))))
    log_a = LRU_C * gate_r * log_sig
    a = jnp.exp(log_a)
    mult = jnp.sqrt(jnp.maximum(1.0 - jnp.exp(2.0 * log_a), 0.0))
    b = mult * gate_i * xc
    b = jnp.where((c == 0) & (row < LRU_TILE - N_META), 0.0, b)

    d = 1
    while d < LRU_TILE:
        keep = row >= d
        a_sh = jnp.where(keep, pltpu.roll(a, d, 0), 1.0)
        b_sh = jnp.where(keep, pltpu.roll(b, d, 0), 0.0)
        b = a * b_sh + b
        a = a * a_sh
        d *= 2
    h = b + a * hprev_ref[...]
    hprev_ref[...] = h[LRU_TILE - 1:, :]
    y_ref[0] = (h * _gelu_tanh(gl)).astype(y_ref.dtype)


def _lru_mixer(ul, params, n_x_tiles):
    B, t_pad, _ = ul.shape
    tail_tile = t_pad // LRU_TILE - 1
    full = lambda a: pl.BlockSpec(a.shape, lambda *_: (0,) * a.ndim)
    tile_map = lambda b, c: (b, jnp.where(c == 0, tail_tile, c - 1), 0)
    return pl.pallas_call(
        _lru_kernel,
        out_shape=jax.ShapeDtypeStruct((B, t_pad, LRU_W), BF16),
        grid=(B, n_x_tiles + 1),
        in_specs=[pl.BlockSpec((1, LRU_TILE, UL_W), tile_map)] + [full(p) for p in params],
        out_specs=pl.BlockSpec((1, LRU_TILE, LRU_W), tile_map),
        scratch_shapes=[pltpu.VMEM((8, LRU_W), F32), pltpu.VMEM((1, LRU_W), F32)],
        compiler_params=_cparams(("parallel", "arbitrary")),
        name="rglru",
    )(ul, *params)


def _outproj_kernel(x_ref, yr_ref, yl_ref, g0_ref, b0_ref, wor_ref, wol_ref, g1_ref, b1_ref,
                    wrt_hi_ref, wrt_lo_ref, brt_ref, h1_ref, h1b_ref, lg_ref):
    h0 = _layer_norm(x_ref[0], g0_ref[...], b0_ref[...])
    mix = _dot(yr_ref[0], wor_ref[...]) + _dot(yl_ref[0], wol_ref[...])
    h1 = _layer_norm(DEEPNORM_ALPHA * h0 + mix, g1_ref[...], b1_ref[...])
    h1_ref[0] = h1
    h1b_ref[0] = h1.astype(BF16)
    hi = h1.astype(BF16)
    lo = (h1 - hi.astype(F32)).astype(BF16)
    w_hi = wrt_hi_ref[...]
    lg = _dot(hi, w_hi) + (_dot(hi, wrt_lo_ref[...]) + _dot(lo, w_hi)) + brt_ref[...]
    lg_ref[0] = _route(lg)


def _route(lg):
    lane = lax.broadcasted_iota(jnp.int32, lg.shape, 1)
    neg = jnp.float32(-jnp.inf)
    rmax = lambda t: jnp.max(t, axis=1, keepdims=True)
    first = lambda hit: jnp.min(jnp.where(hit, lane, LANE), axis=1, keepdims=True)
    is_grp = lane < N_GROUPS
    gl = jnp.where(is_grp, lg, neg)
    gmax = rmax(gl)
    g_sel = first(gl == gmax)
    p_g = 1.0 / jnp.sum(jnp.where(is_grp, jnp.exp(lg - gmax), 0.0), axis=1, keepdims=True)
    ex = lane - N_GROUPS
    in_grp = (ex >= 0) & (ex < N_EXPERTS) & (jnp.right_shift(ex, 3) == g_sel)
    el = jnp.where(in_grp, lg, neg)
    v1 = rmax(el)
    i1 = first(el == v1)
    el2 = jnp.where(lane == i1, neg, el)
    v2 = rmax(el2)
    i2 = first(el2 == v2)
    t = jnp.exp(v2 - v1)
    gate1 = p_g / (1.0 + t)
    gate2 = p_g * t / (1.0 + t)
    e1 = (i1 - N_GROUPS).astype(F32)
    e2 = (i2 - N_GROUPS).astype(F32)
    out = jnp.where(lane == 0, e1, jnp.where(lane == 1, e2, jnp.where(lane == 2, gate1, jnp.where(lane == 3, gate2, 0.0))))
    return out


def _out_projection(x, y_rwkv, y_lru, ln0_g, ln0_b, wo_r, wo_l, ln1_g, ln1_b, wrt_hi, wrt_lo, brt):
    B, T, D = x.shape
    tm = 512
    vec = lambda n: pl.BlockSpec((1, n), lambda *_: (0, 0))
    full = lambda a: pl.BlockSpec(a.shape, lambda *_: (0,) * a.ndim)
    rows = lambda w: pl.BlockSpec((1, tm, w), lambda b, i: (b, i, 0))
    return pl.pallas_call(
        _outproj_kernel,
        out_shape=(jax.ShapeDtypeStruct((B, T, D), F32), jax.ShapeDtypeStruct((B, T, D), BF16),
                   jax.ShapeDtypeStruct((B, T, LANE), F32)),
        grid=(B, T // tm),
        in_specs=[rows(D), rows(RWKV_W), rows(LRU_W), vec(D), vec(D), full(wo_r), full(wo_l), vec(D), vec(D),
                  full(wrt_hi), full(wrt_lo), vec(LANE)],
        out_specs=(rows(D), rows(D), rows(LANE)),
        compiler_params=_cparams(("parallel", "parallel")),
        name="outproj",
    )(x, y_rwkv, y_lru, ln0_g, ln0_b, wo_r, wo_l, ln1_g, ln1_b, wrt_hi, wrt_lo, brt)


def _moe_kernel(te_ref, nv_ref, x_ref, wg_ref, wu_ref, wd_ref, o_ref, wgb_ref, wub_ref, wdb_ref):
    i = pl.program_id(0)
    e = te_ref[i]
    e_prev = te_ref[jnp.maximum(i - 1, 0)]

    @pl.when((i == 0) | (e != e_prev))
    def _():
        wgb_ref[...] = wg_ref[0].astype(BF16)
        wub_ref[...] = wu_ref[0].astype(BF16)
        wdb_ref[...] = wd_ref[0].astype(BF16)

    @pl.when(i < nv_ref[0])
    def _():
        xb = x_ref[...]
        hg = _dot(xb, wgb_ref[...])
        hu = _dot(xb, wub_ref[...])
        mid = (hg * jax.nn.sigmoid(hg) * hu).astype(BF16)
        o_ref[...] = _dot(mid, wdb_ref[...]).astype(o_ref.dtype)

    @pl.when(i >= nv_ref[0])
    def _():
        o_ref[...] = jnp.zeros_like(o_ref)


def _moe_experts(xbuf, tile_expert, n_valid, w_gate, w_up, w_down):
    P, D = xbuf.shape
    n_tiles = P // MOE_TILE
    grid_spec = pltpu.PrefetchScalarGridSpec(
        num_scalar_prefetch=2,
        grid=(n_tiles,),
        in_specs=[pl.BlockSpec((MOE_TILE, D), lambda i, te, nv: (i, 0)),
                  pl.BlockSpec((1, D, D_EXPERT), lambda i, te, nv: (te[i], 0, 0)),
                  pl.BlockSpec((1, D, D_EXPERT), lambda i, te, nv: (te[i], 0, 0)),
                  pl.BlockSpec((1, D_EXPERT, D), lambda i, te, nv: (te[i], 0, 0))],
        out_specs=pl.BlockSpec((MOE_TILE, D), lambda i, te, nv: (i, 0)),
        scratch_shapes=[pltpu.VMEM((D, D_EXPERT), BF16), pltpu.VMEM((D, D_EXPERT), BF16),
                        pltpu.VMEM((D_EXPERT, D), BF16)],
    )
    return pl.pallas_call(
        _moe_kernel,
        out_shape=jax.ShapeDtypeStruct((P, D), BF16),
        grid_spec=grid_spec,
        compiler_params=_cparams(("arbitrary",)),
        name="moe_experts",
    )(tile_expert, n_valid, xbuf, w_gate, w_up, w_down)


def _combine_kernel(h_ref, y_ref, gate_ref, g_ref, b_ref, o_ref):
    D = h_ref.shape[-1]
    gate = gate_ref[...]
    ffn = gate[:, 0:1] * y_ref[:, :D].astype(F32) + gate[:, 1:2] * y_ref[:, D:].astype(F32)
    o_ref[...] = _layer_norm(DEEPNORM_ALPHA * h_ref[...] + ffn, g_ref[...], b_ref[...])


def _combine(h1, ypair, gates, ln2_g, ln2_b):
    M, D = h1.shape
    tm = 512
    vec = lambda n: pl.BlockSpec((1, n), lambda *_: (0, 0))
    return pl.pallas_call(
        _combine_kernel,
        out_shape=jax.ShapeDtypeStruct((M, D), F32),
        grid=(M // tm,),
        in_specs=[pl.BlockSpec((tm, D), lambda i: (i, 0)), pl.BlockSpec((tm, TOP_K * D), lambda i: (i, 0)),
                  pl.BlockSpec((tm, TOP_K), lambda i: (i, 0)), vec(D), vec(D)],
        out_specs=pl.BlockSpec((tm, D), lambda i: (i, 0)),
        compiler_params=_cparams(("parallel",)),
        name="combine",
    )(h1, ypair, gates, ln2_g, ln2_b)


def _routing_plan(route):
    M = route.shape[0]
    eid = route[:, :TOP_K].astype(jnp.int32).reshape(-1)
    gates = route[:, TOP_K:2 * TOP_K]
    A = M * TOP_K
    onehot = (eid[:, None] == jnp.arange(N_EXPERTS, dtype=eid.dtype)[None, :]).astype(jnp.int32)
    csum = jnp.cumsum(onehot, axis=0)
    rank = jnp.sum(csum * onehot, axis=1) - 1
    counts = csum[-1]
    pcounts = (counts + MOE_TILE - 1) // MOE_TILE * MOE_TILE
    pends = jnp.cumsum(pcounts)
    pstarts = pends - pcounts
    dest = (jnp.sum(onehot * pstarts[None, :], axis=1) + rank).astype(jnp.int32)
    n_tiles = (A + N_EXPERTS * (MOE_TILE - 1) + MOE_TILE - 1) // MOE_TILE
    n_valid = (pends[-1] // MOE_TILE).astype(jnp.int32)
    tile_start = jnp.minimum(jnp.arange(n_tiles, dtype=jnp.int32) * MOE_TILE, pends[-1] - 1)
    tile_expert = jnp.sum((pends[None, :] <= tile_start[:, None]).astype(jnp.int32), axis=1)
    tile_expert = jnp.minimum(tile_expert, N_EXPERTS - 1).astype(jnp.int32)
    tok = jnp.arange(A, dtype=jnp.int32) // TOP_K
    row_tok = jnp.zeros((n_tiles * MOE_TILE,), jnp.int32).at[dest].set(tok)
    return gates, dest, row_tok, tile_expert, n_valid.reshape(1)


def kernel(x, meta, ln0_g, ln0_b, w_in, mu_shift, w0, w_decay_up, a0, w_a_up, w_g_up, k_k, k_a, r_k, gn_g, gn_b, conv_w, conv_b, w_rg, b_rg, w_ig, b_ig, lru_lambda, w_out, ln1_g, ln1_b, w_router_grp, b_router_grp, w_router_exp, b_router_exp, w_exp_gate, w_exp_up, w_exp_down, ln2_g, ln2_b):
    B, T, D = x.shape
    assert D == D_MODEL and T % 512 == 0 and w_in.shape[0] == 1
    row = lambda p: p.reshape(1, -1).astype(F32)
    n_rw = 3 * RWKV_W
    w_in0 = w_in[0]

    def slots(p, fill=0.0):
        pad = lambda a, n: jnp.pad(a, [(0, 0)] * (a.ndim - 1) + [(0, n - a.shape[-1])], constant_values=fill)
        zw = p[..., n_rw:n_rw + DECAY_RANK]
        za = p[..., n_rw + DECAY_RANK:n_rw + DECAY_RANK + AAA_RANK]
        zg = p[..., n_rw + DECAY_RANK + AAA_RANK:n_rw + DECAY_RANK + AAA_RANK + GATE_RANK]
        return jnp.concatenate([p[..., :n_rw], pad(zw, LANE), pad(za, LANE), pad(zg, ZG_SLOT)], axis=-1)

    rwkv_cols = n_rw + DECAY_RANK + AAA_RANK + GATE_RANK
    w_r = slots(w_in0[:, :rwkv_cols]).astype(BF16)
    w_l = w_in0[:, rwkv_cols:].astype(BF16)
    ur, ul = _in_projection(x, meta, row(ln0_g), row(ln0_b), w_r, w_l)

    pad_rows = lambda a, n: jnp.pad(a, ((0, n - a.shape[0]), (0, 0)))
    rwkv_params = (slots(mu_shift[0][None, :]).astype(F32), row(w0[0]), pad_rows(w_decay_up[0], LANE),
                   row(a0[0]), pad_rows(w_a_up[0], LANE), pad_rows(w_g_up[0], ZG_SLOT).astype(BF16),
                   row(k_k[0]), row(k_a[0]), row(r_k[0]), row(gn_g[0]), row(gn_b[0]))
    y_rwkv = _rwkv_mixer(ur, rwkv_params, T // CHUNK)

    blockdiag = lambda w: jax.scipy.linalg.block_diag(*[w[i] for i in range(LRU_BLOCKS)]).astype(BF16)
    lru_params = (conv_w[0], row(conv_b[0]), blockdiag(w_rg[0]), row(b_rg[0]), blockdiag(w_ig[0]), row(b_ig[0]),
                  row(lru_lambda[0]))
    y_lru = _lru_mixer(ul, lru_params, T // LRU_TILE)

    w_rt = jnp.concatenate([w_router_grp[0], w_router_exp[0]], axis=1)
    w_rt = jnp.pad(w_rt, ((0, 0), (0, LANE - w_rt.shape[1])))
    wrt_hi = w_rt.astype(BF16)
    wrt_lo = (w_rt - wrt_hi.astype(F32)).astype(BF16)
    b_rt = jnp.concatenate([b_router_grp[0], b_router_exp[0]])
    b_rt = jnp.pad(b_rt, (0, LANE - b_rt.shape[0])).reshape(1, LANE)
    wo = w_out[0].astype(BF16)
    h1, h1b, route = _out_projection(x, y_rwkv, y_lru, row(ln0_g), row(ln0_b), wo[:RWKV_W], wo[RWKV_W:],
                                      row(ln1_g[0]), row(ln1_b[0]), wrt_hi, wrt_lo, b_rt)

    M = B * T
    gates, dest, row_tok, tile_expert, n_valid = _routing_plan(route.reshape(M, LANE))
    xbuf = h1b.reshape(M, D)[row_tok]
    ybuf = _moe_experts(xbuf, tile_expert, n_valid, w_exp_gate[0], w_exp_up[0], w_exp_down[0])
    ypair = ybuf[dest].reshape(M, TOP_K * D)
    out = _combine(h1.reshape(M, D), ypair, gates, row(ln2_g[0]), row(ln2_b[0]))
    return out.reshape(B, T, D)
```

```python
import math

import jax
import jax.numpy as jnp
from jax import lax
from jax.experimental import pallas as pl
from jax.experimental.pallas import tpu as pltpu

F32 = jnp.float32
BF16 = jnp.bfloat16

D_MODEL = 1024
N_META = 16
RWKV_W = 512
RWKV_HEAD = 64
DECAY_RANK = 64
AAA_RANK = 64
GATE_RANK = 160
LRU_W = 512
LRU_BLOCKS = 8
CONV_WIDTH = 4
LRU_C = 8.0
N_GROUPS = 4
EXPERTS_PER_GROUP = 8
N_EXPERTS = N_GROUPS * EXPERTS_PER_GROUP
TOP_K = 2
D_EXPERT = 512
LN_EPS = 1e-5
GN_EPS = 64e-5
DEEPNORM_ALPHA = 2.0 ** 0.25

LANE = 128
OFF_R, OFF_K, OFF_V = 0, RWKV_W, 2 * RWKV_W
OFF_ZW = 3 * RWKV_W
OFF_ZA = OFF_ZW + LANE
OFF_ZG = OFF_ZA + LANE
ZG_SLOT = 2 * LANE
UR_W = OFF_ZG + ZG_SLOT
UL_W = 2 * LRU_W

TAIL = 128
CHUNK = 64
HEADS_PER_GROUP = 4
GW = HEADS_PER_GROUP * RWKV_HEAD
N_HGROUPS = RWKV_W // GW
LRU_TILE = 128
MOE_TILE = 256
COMBINE_TILE = 256
INVERT_BLOCK = 4096
V7X_VMEM_BYTES = 64 * 1024 * 1024
VMEM_LIMIT = V7X_VMEM_BYTES - 8 * 1024 * 1024


def _cparams(sem):
    return pltpu.CompilerParams(dimension_semantics=sem, vmem_limit_bytes=VMEM_LIMIT)


def _layer_norm(x, g, b):
    mu = jnp.mean(x, -1, keepdims=True)
    xc = x - mu
    var = jnp.mean(xc * xc, -1, keepdims=True)
    return xc * lax.rsqrt(var + LN_EPS) * g + b


def _dot(a, b):
    return jnp.dot(a, b, preferred_element_type=F32)


def _dot_hi(a, b):
    return jnp.dot(a, b, precision=lax.Precision.HIGHEST, preferred_element_type=F32)


def _dot_nt(a, b):
    return lax.dot_general(a, b, (((1,), (1,)), ((), ())), preferred_element_type=F32)


def _dot_tn(a, b):
    return lax.dot_general(a, b, (((0,), (0,)), ((), ())), preferred_element_type=F32)


def _split_dot(x, w_bf16):
    hi = x.astype(BF16)
    lo = (x - hi.astype(F32)).astype(BF16)
    return _dot(hi, w_bf16) + _dot(lo, w_bf16)


def _full(a):
    return pl.BlockSpec(a.shape, lambda *_: (0,) * a.ndim)


def _inproj_kernel(x_ref, g_ref, b_ref, wr_ref, wl_ref, ur_ref, ul_ref):
    h = _layer_norm(x_ref[0], g_ref[...], b_ref[...]).astype(BF16)
    ur_ref[0] = _dot(h, wr_ref[...])
    ul_ref[0] = _dot(h, wl_ref[...])


def _inproj_tail_kernel(x_ref, g_ref, b_ref, wr_ref, wl_ref, ur_ref, ul_ref):
    h = _layer_norm(x_ref[...], g_ref[...], b_ref[...]).astype(BF16)
    rows = lax.broadcasted_iota(jnp.int32, (TAIL, 1), 0)
    valid = (rows >= TAIL - N_META).astype(F32)
    ur_ref[...] = _dot(h, wr_ref[...]) * valid
    ul_ref[...] = _dot(h, wl_ref[...]) * valid


def _in_projection(x, meta, ln0_g, ln0_b, w_r, w_l):
    B, T, D = x.shape
    tm = 512
    ur, ul = pl.pallas_call(
        _inproj_kernel,
        out_shape=(jax.ShapeDtypeStruct((B, T, UR_W), F32), jax.ShapeDtypeStruct((B, T, UL_W), F32)),
        grid=(B, T // tm),
        in_specs=[pl.BlockSpec((1, tm, D), lambda b, i: (b, i, 0)), _full(ln0_g), _full(ln0_b), _full(w_r), _full(w_l)],
        out_specs=(pl.BlockSpec((1, tm, UR_W), lambda b, i: (b, i, 0)),
                   pl.BlockSpec((1, tm, UL_W), lambda b, i: (b, i, 0))),
        compiler_params=_cparams(("parallel", "parallel")),
        name="inproj",
    )(x, ln0_g, ln0_b, w_r, w_l)
    tail_x = jnp.concatenate([jnp.zeros((TAIL - N_META, D), F32), meta.astype(F32)], axis=0)
    ur_t, ul_t = pl.pallas_call(
        _inproj_tail_kernel,
        out_shape=(jax.ShapeDtypeStruct((TAIL, UR_W), F32), jax.ShapeDtypeStruct((TAIL, UL_W), F32)),
        grid=(1,),
        in_specs=[_full(tail_x), _full(ln0_g), _full(ln0_b), _full(w_r), _full(w_l)],
        out_specs=(pl.BlockSpec((TAIL, UR_W), lambda i: (0, 0)), pl.BlockSpec((TAIL, UL_W), lambda i: (0, 0))),
        compiler_params=_cparams(("arbitrary",)),
        name="inproj_tail",
    )(tail_x, ln0_g, ln0_b, w_r, w_l)
    return ur, ul, ur_t, ul_t


def _rwkv_kernel(u_ref, ut_ref, mu_ref, w0_ref, wdu_ref, a0_ref, wau_ref, wgu_ref, kk_ref, ka_ref, rk_ref,
                 gng_ref, gnb_ref, bones_ref, tril_ref, eye_ref, bm_ref, msl_ref, mil_ref,
                 m8_ref, m16_ref, m32_ref, m64_ref, y_ref, s_ref, prev_ref):
    c = pl.program_id(1)

    @pl.when(c == 0)
    def _():
        s_ref[...] = jnp.zeros_like(s_ref)
        prev_ref[...] = jnp.zeros_like(prev_ref)

    u = jnp.where(c == 0, ut_ref[...], u_ref[0])
    row = lax.broadcasted_iota(jnp.int32, u.shape, 0)
    u_prev = jnp.where(row == 0, prev_ref[...], pltpu.roll(u, 1, 0))
    prev_ref[...] = u[CHUNK - 1:CHUNK, :]
    x = u + (u_prev - u) * mu_ref[...]
    r = x[:, OFF_R:OFF_R + RWKV_W]
    k = x[:, OFF_K:OFF_K + RWKV_W]
    v = x[:, OFF_V:OFF_V + RWKV_W]
    zw = x[:, OFF_ZW:OFF_ZW + LANE]
    za = x[:, OFF_ZA:OFF_ZA + LANE]
    zg = x[:, OFF_ZG:OFF_ZG + ZG_SLOT]

    bones = bones_ref[...]
    head_sum = lambda t: _split_dot(t, bones)

    z = w0_ref[...] + _dot_hi(jnp.tanh(zw), wdu_ref[...])
    logw = -math.exp(-0.5) * jax.nn.sigmoid(z)
    a = jax.nn.sigmoid(a0_ref[...] + _dot_hi(za, wau_ref[...]))
    g = _dot(jax.nn.sigmoid(zg).astype(BF16), wgu_ref[...])
    kk = k * kk_ref[...]
    kk = kk / jnp.maximum(jnp.sqrt(head_sum(kk * kk)), 1e-12)
    k = k * (1.0 + (a - 1.0) * ka_ref[...])
    kka = kk * a

    cl = _dot_hi(tril_ref[...], logw)
    cl_last = cl[CHUNK - 1:CHUNK, :]
    e_neg = jnp.exp(-cl)
    e_end = jnp.exp(cl_last - cl)
    rt = r * jnp.exp(cl)
    kt = k * e_neg
    at = -kk * jnp.exp(cl - logw)
    bt = kka * e_neg
    kw = k * e_end
    bw = kka * e_end
    w_end = jnp.exp(cl_last)

    bm = bm_ref[...]
    eye = eye_ref[...]
    tile4 = lambda t: jnp.concatenate([t] * HEADS_PER_GROUP, axis=0)
    fold4 = lambda t: sum(t[i * CHUNK:(i + 1) * CHUNK] for i in range(HEADS_PER_GROUP))
    ys = []
    for hg in range(N_HGROUPS):
        sl = slice(hg * GW, (hg + 1) * GW)
        v4 = v[:, sl]
        lhs = jnp.concatenate([tile4(at[:, sl]) * bm, tile4(rt[:, sl]) * bm], axis=0).astype(BF16)
        rhs = jnp.concatenate([tile4(bt[:, sl]), tile4(kt[:, sl])], axis=0).astype(BF16)
        aa = _dot_nt(lhs, rhs)
        a_ab = aa[:GW, :GW] * msl_ref[...]
        a_ak = (aa[:GW, GW:] * msl_ref[...]).astype(BF16)
        a_rb = (aa[GW:, :GW] * mil_ref[...]).astype(BF16)
        a_rk = (aa[GW:, GW:] * mil_ref[...]).astype(BF16)

        a0 = (a_ab * m8_ref[...]).astype(BF16)
        a2 = _dot(a0, a0).astype(BF16)
        a4 = _dot(a2, a2).astype(BF16)
        p1 = eye + a0.astype(F32)
        p1 = p1 + _dot(p1.astype(BF16), a2)
        t = p1 + _dot(p1.astype(BF16), a4)
        for m_ref in (m16_ref, m32_ref, m64_ref):
            tb = t.astype(BF16)
            off = (a_ab * m_ref[...]).astype(BF16)
            t = t + _dot(_dot(tb, off).astype(BF16), tb)
        tb = t.astype(BF16)

        s = s_ref[hg]
        sb = s.astype(BF16)
        vt = tile4(v4).astype(BF16)
        xx = _dot_nt(lhs[:GW], sb) + _dot(a_ak, vt)
        uu = _dot(tb, xx.astype(BF16)) * bm
        yy = (_dot_nt(lhs[GW:], sb) + _dot(a_rb, uu.astype(BF16)) + _dot(a_rk, vt)) * bm
        ys.append(fold4(yy))
        u4 = fold4(uu)
        upd = _dot_tn(jnp.concatenate([u4, v4], axis=0).astype(BF16),
                      jnp.concatenate([bw[:, sl], kw[:, sl]], axis=0).astype(BF16))
        s_ref[hg] = s * w_end[:, sl] + upd * bm

    y = jnp.concatenate(ys, axis=1)
    inv_n = 1.0 / RWKV_HEAD
    ym = head_sum(y) * inv_n
    yc = y - ym
    yv = head_sum(yc * yc) * inv_n
    yn = yc * lax.rsqrt(yv + GN_EPS) * gng_ref[...] + gnb_ref[...]
    bonus = head_sum(r * k * rk_ref[...]) * v
    y_ref[0] = ((yn + bonus) * g).astype(y_ref.dtype)


def _rwkv_masks():
    i = jnp.arange(GW)[:, None]
    j = jnp.arange(GW)[None, :]
    same = lambda n: (i // n) == (j // n)
    f = lambda m: m.astype(F32)
    bm = f(same(RWKV_HEAD))
    msl = f(same(RWKV_HEAD) & (i > j))
    mil = f(same(RWKV_HEAD) & (i >= j))
    m8 = f(same(8))
    m16 = f(same(16) & ~same(8))
    m32 = f(same(32) & ~same(16))
    m64 = f(same(64) & ~same(32))
    eye = f(i == j)
    ti = jnp.arange(CHUNK)
    tril = f(ti[:, None] >= ti[None, :])
    hi = jnp.arange(RWKV_W)
    bones = ((hi[:, None] // RWKV_HEAD) == (hi[None, :] // RWKV_HEAD)).astype(BF16)
    return bones, tril, eye, bm, msl, mil, m8, m16, m32, m64


def _rwkv_mixer(ur, ur_tail, params):
    B, T, _ = ur.shape
    consts = _rwkv_masks()
    x_map = lambda b, c: (b, jnp.maximum(c - 1, 0), 0)
    return pl.pallas_call(
        _rwkv_kernel,
        out_shape=jax.ShapeDtypeStruct((B, T, RWKV_W), BF16),
        grid=(B, T // CHUNK + 1),
        in_specs=[pl.BlockSpec((1, CHUNK, UR_W), x_map),
                  pl.BlockSpec((CHUNK, UR_W), lambda b, c: (TAIL // CHUNK - 1, 0))]
                 + [_full(p) for p in params] + [_full(m) for m in consts],
        out_specs=pl.BlockSpec((1, CHUNK, RWKV_W), x_map),
        scratch_shapes=[pltpu.VMEM((N_HGROUPS, GW, GW), F32), pltpu.VMEM((1, UR_W), F32)],
        compiler_params=_cparams(("parallel", "arbitrary")),
        name="rwkv7",
    )(ur, ur_tail, *params, *consts)


def _gelu_tanh(x):
    return 0.5 * x * (1.0 + jnp.tanh(math.sqrt(2.0 / math.pi) * (x + 0.044715 * (x * x * x))))


def _lru_kernel(u_ref, ut_ref, cw_ref, cb_ref, wrg_ref, brg_ref, wig_ref, big_ref, lam_ref, y_ref,
                xprev_ref, hprev_ref):
    c = pl.program_id(1)

    @pl.when(c == 0)
    def _():
        xprev_ref[...] = jnp.zeros_like(xprev_ref)
        hprev_ref[...] = jnp.zeros_like(hprev_ref)

    u = jnp.where(c == 0, ut_ref[...], u_ref[0])
    xl = u[:, :LRU_W]
    gl = u[:, LRU_W:]
    row = lax.broadcasted_iota(jnp.int32, (LRU_TILE, LRU_W), 0)
    row8 = lax.broadcasted_iota(jnp.int32, (8, LRU_W), 0)
    xprev = xprev_ref[...]
    xc = cb_ref[...] + cw_ref[CONV_WIDTH - 1:CONV_WIDTH, :] * xl
    for d in range(1, CONV_WIDTH):
        rolled = pltpu.roll(xl, d, 0)
        head = jnp.where(row8 < d, pltpu.roll(xprev, d, 0), rolled[:8])
        shifted = jnp.concatenate([head, rolled[8:]], axis=0)
        xc = xc + cw_ref[CONV_WIDTH - 1 - d:CONV_WIDTH - d, :] * shifted
    xprev_ref[...] = xl[LRU_TILE - 8:, :]

    xcb = xc.astype(BF16)
    gate_r = jax.nn.sigmoid(_dot(xcb, wrg_ref[...]) + brg_ref[...])
    gate_i = jax.nn.sigmoid(_dot(xcb, wig_ref[...]) + big_ref[...])
    lam = lam_ref[...]
    log_sig = -(jnp.maximum(-lam, 0.0) + jnp.log1p(jnp.exp(-jnp.abs(lam))))
    log_a = LRU_C * gate_r * log_sig
    a = jnp.exp(log_a)
    mult = jnp.sqrt(jnp.maximum(1.0 - jnp.exp(2.0 * log_a), 0.0))
    b = mult * gate_i * xc
    b = jnp.where((c == 0) & (row < LRU_TILE - N_META), 0.0, b)

    d = 1
    while d < LRU_TILE:
        keep = row >= d
        a_sh = jnp.where(keep, pltpu.roll(a, d, 0), 1.0)
        b_sh = jnp.where(keep, pltpu.roll(b, d, 0), 0.0)
        b = a * b_sh + b
        a = a * a_sh
        d *= 2
    h = b + a * hprev_ref[...]
    hprev_ref[...] = h[LRU_TILE - 1:, :]
    y_ref[0] = (h * _gelu_tanh(gl)).astype(y_ref.dtype)


def _lru_mixer(ul, ul_tail, params):
    B, T, _ = ul.shape
    assert TAIL == LRU_TILE
    x_map = lambda b, c: (b, jnp.maximum(c - 1, 0), 0)
    return pl.pallas_call(
        _lru_kernel,
        out_shape=jax.ShapeDtypeStruct((B, T, LRU_W), BF16),
        grid=(B, T // LRU_TILE + 1),
        in_specs=[pl.BlockSpec((1, LRU_TILE, UL_W), x_map), _full(ul_tail)] + [_full(p) for p in params],
        out_specs=pl.BlockSpec((1, LRU_TILE, LRU_W), x_map),
        scratch_shapes=[pltpu.VMEM((8, LRU_W), F32), pltpu.VMEM((1, LRU_W), F32)],
        compiler_params=_cparams(("parallel", "arbitrary")),
        name="rglru",
    )(ul, ul_tail, *params)


def _route(lg):
    lane = lax.broadcasted_iota(jnp.int32, lg.shape, 1)
    neg = jnp.float32(-jnp.inf)
    rmax = lambda t: jnp.max(t, axis=1, keepdims=True)
    first = lambda hit: jnp.min(jnp.where(hit, lane, LANE), axis=1, keepdims=True)
    is_grp = lane < N_GROUPS
    gl = jnp.where(is_grp, lg, neg)
    gmax = rmax(gl)
    g_sel = first(gl == gmax)
    p_g = 1.0 / jnp.sum(jnp.where(is_grp, jnp.exp(lg - gmax), 0.0), axis=1, keepdims=True)
    ex = lane - N_GROUPS
    in_grp = (ex >= 0) & (ex < N_EXPERTS) & (jnp.right_shift(ex, 3) == g_sel)
    el = jnp.where(in_grp, lg, neg)
    v1 = rmax(el)
    i1 = first(el == v1)
    el2 = jnp.where(lane == i1, neg, el)
    v2 = rmax(el2)
    i2 = first(el2 == v2)
    t = jnp.exp(v2 - v1)
    gate1 = p_g / (1.0 + t)
    gate2 = p_g * t / (1.0 + t)
    e1 = (i1 - N_GROUPS).astype(F32)
    e2 = (i2 - N_GROUPS).astype(F32)
    return jnp.where(lane == 0, e1, jnp.where(lane == 1, e2, jnp.where(lane == 2, gate1, jnp.where(lane == 3, gate2, 0.0))))


def _outproj_kernel(x_ref, yr_ref, yl_ref, g0_ref, b0_ref, wor_ref, wol_ref, g1_ref, b1_ref,
                    wrt_hi_ref, wrt_lo_ref, brt_ref, h1_ref, rt_ref):
    h0 = _layer_norm(x_ref[0], g0_ref[...], b0_ref[...])
    mix = _dot(yr_ref[0], wor_ref[...]) + _dot(yl_ref[0], wol_ref[...])
    h1 = _layer_norm(DEEPNORM_ALPHA * h0 + mix, g1_ref[...], b1_ref[...])
    h1_ref[0] = h1
    hi = h1.astype(BF16)
    lo = (h1 - hi.astype(F32)).astype(BF16)
    w_hi = wrt_hi_ref[...]
    lg = _dot(hi, w_hi) + (_dot(hi, wrt_lo_ref[...]) + _dot(lo, w_hi)) + brt_ref[...]
    rt_ref[0] = _route(lg)


def _out_projection(x, y_rwkv, y_lru, ln0_g, ln0_b, wo_r, wo_l, ln1_g, ln1_b, wrt_hi, wrt_lo, brt):
    B, T, D = x.shape
    tm = 512
    rows = lambda w: pl.BlockSpec((1, tm, w), lambda b, i: (b, i, 0))
    return pl.pallas_call(
        _outproj_kernel,
        out_shape=(jax.ShapeDtypeStruct((B, T, D), F32), jax.ShapeDtypeStruct((B, T, LANE), F32)),
        grid=(B, T // tm),
        in_specs=[rows(D), rows(RWKV_W), rows(LRU_W), _full(ln0_g), _full(ln0_b), _full(wo_r), _full(wo_l),
                  _full(ln1_g), _full(ln1_b), _full(wrt_hi), _full(wrt_lo), _full(brt)],
        out_specs=(rows(D), rows(LANE)),
        compiler_params=_cparams(("parallel", "parallel")),
        name="outproj",
    )(x, y_rwkv, y_lru, ln0_g, ln0_b, wo_r, wo_l, ln1_g, ln1_b, wrt_hi, wrt_lo, brt)


def _invert_kernel(dest_ref, out_ref):
    i = pl.program_id(0)

    @pl.when(i == 0)
    def _():
        def zero(j, carry):
            out_ref[j] = 0
            return carry
        lax.fori_loop(0, out_ref.shape[0], zero, 0, unroll=8)

    base = i * INVERT_BLOCK

    def body(j, carry):
        out_ref[dest_ref[j]] = base + j
        return carry

    lax.fori_loop(0, INVERT_BLOCK, body, 0, unroll=8)


def _invert_slots(dest, n_slots):
    A = dest.shape[0]
    return pl.pallas_call(
        _invert_kernel,
        out_shape=jax.ShapeDtypeStruct((n_slots,), jnp.int32),
        grid=(A // INVERT_BLOCK,),
        in_specs=[pl.BlockSpec((INVERT_BLOCK,), lambda i: (i,), memory_space=pltpu.SMEM)],
        out_specs=pl.BlockSpec(memory_space=pltpu.SMEM),
        compiler_params=_cparams(("arbitrary",)),
        name="invert_slots",
    )(dest)


def _row_copy(src_hbm, src_row, dst_ref, dst_row, sem):
    return pltpu.make_async_copy(src_hbm.at[pl.ds(src_row, 1), :], dst_ref.at[pl.ds(dst_row, 1), :], sem)


def _moe_kernel(te_ref, nv_ref, ra_cur_ref, ra_nxt_ref, h_hbm, wg_ref, wu_ref, wd_ref, o_ref,
                xbuf, sem, wgb_ref, wub_ref, wdb_ref):
    i = pl.program_id(0)
    n_valid = nv_ref[0]
    slot = lax.rem(i, 2)

    def start_gather(ra_ref, s):
        def body(j, carry):
            tok = lax.shift_right_logical(ra_ref[j], 1)
            _row_copy(h_hbm, tok, xbuf.at[s], j, sem.at[s]).start()
            return carry
        lax.fori_loop(0, MOE_TILE, body, 0, unroll=8)

    @pl.when(i == 0)
    def _():
        start_gather(ra_cur_ref, 0)

    @pl.when(i + 1 < n_valid)
    def _():
        start_gather(ra_nxt_ref, 1 - slot)

    e = te_ref[i]
    e_prev = te_ref[jnp.maximum(i - 1, 0)]

    @pl.when((i == 0) | (e != e_prev))
    def _():
        wgb_ref[...] = wg_ref[0].astype(BF16)
        wub_ref[...] = wu_ref[0].astype(BF16)
        wdb_ref[...] = wd_ref[0].astype(BF16)

    @pl.when(i < n_valid)
    def _():
        def wait_row(j, carry):
            _row_copy(h_hbm, 0, xbuf.at[slot], j, sem.at[slot]).wait()
            return carry
        lax.fori_loop(0, MOE_TILE, wait_row, 0)
        xb = xbuf[slot].astype(BF16)
        hg = _dot(xb, wgb_ref[...])
        hu = _dot(xb, wub_ref[...])
        mid = (hg * jax.nn.sigmoid(hg) * hu).astype(BF16)
        o_ref[...] = _dot(mid, wdb_ref[...])

    @pl.when(i >= n_valid)
    def _():
        o_ref[...] = jnp.zeros_like(o_ref)


def _moe_experts(h1, row_asg, tile_expert, n_valid, w_gate, w_up, w_down):
    M, D = h1.shape
    n_tiles = row_asg.shape[0] // MOE_TILE
    smem_tile = lambda f: pl.BlockSpec((MOE_TILE,), f, memory_space=pltpu.SMEM)
    grid_spec = pltpu.PrefetchScalarGridSpec(
        num_scalar_prefetch=2,
        grid=(n_tiles,),
        in_specs=[smem_tile(lambda i, te, nv: (i,)),
                  smem_tile(lambda i, te, nv: (jnp.minimum(i + 1, n_tiles - 1),)),
                  pl.BlockSpec(memory_space=pl.ANY),
                  pl.BlockSpec((1, D, D_EXPERT), lambda i, te, nv: (te[i], 0, 0)),
                  pl.BlockSpec((1, D, D_EXPERT), lambda i, te, nv: (te[i], 0, 0)),
                  pl.BlockSpec((1, D_EXPERT, D), lambda i, te, nv: (te[i], 0, 0))],
        out_specs=pl.BlockSpec((MOE_TILE, D), lambda i, te, nv: (i, 0)),
        scratch_shapes=[pltpu.VMEM((2, MOE_TILE, D), F32), pltpu.SemaphoreType.DMA((2,)),
                        pltpu.VMEM((D, D_EXPERT), BF16), pltpu.VMEM((D, D_EXPERT), BF16),
                        pltpu.VMEM((D_EXPERT, D), BF16)],
    )
    return pl.pallas_call(
        _moe_kernel,
        out_shape=jax.ShapeDtypeStruct((n_tiles * MOE_TILE, D), F32),
        grid_spec=grid_spec,
        compiler_params=_cparams(("arbitrary",)),
        name="moe_experts",
    )(tile_expert, n_valid, row_asg, row_asg, h1, w_gate, w_up, w_down)


def _combine_kernel(d_cur_ref, d_nxt_ref, h_ref, gate_ref, g_ref, b_ref, y_hbm, o_ref, ybuf, sem):
    i = pl.program_id(0)
    n = pl.num_programs(0)
    slot = lax.rem(i, 2)

    def start_gather(d_ref, s):
        def body(t, carry):
            for k in range(TOP_K):
                _row_copy(y_hbm, d_ref[TOP_K * t + k], ybuf.at[s, k], t, sem.at[s]).start()
            return carry
        lax.fori_loop(0, COMBINE_TILE, body, 0, unroll=4)

    @pl.when(i == 0)
    def _():
        start_gather(d_cur_ref, 0)

    @pl.when(i + 1 < n)
    def _():
        start_gather(d_nxt_ref, 1 - slot)

    def wait_rows(t, carry):
        for k in range(TOP_K):
            _row_copy(y_hbm, 0, ybuf.at[slot, k], t, sem.at[slot]).wait()
        return carry
    lax.fori_loop(0, COMBINE_TILE, wait_rows, 0)

    gate = gate_ref[...]
    ffn = gate[:, 0:1] * ybuf[slot, 0] + gate[:, 1:2] * ybuf[slot, 1]
    o_ref[...] = _layer_norm(DEEPNORM_ALPHA * h_ref[...] + ffn, g_ref[...], b_ref[...])


def _combine(h1, ybuf, dest, gates, ln2_g, ln2_b):
    M, D = h1.shape
    tm = COMBINE_TILE
    n = M // tm
    smem_tile = lambda f: pl.BlockSpec((TOP_K * tm,), f, memory_space=pltpu.SMEM)
    return pl.pallas_call(
        _combine_kernel,
        out_shape=jax.ShapeDtypeStruct((M, D), F32),
        grid=(n,),
        in_specs=[smem_tile(lambda i: (i,)), smem_tile(lambda i: (jnp.minimum(i + 1, n - 1),)),
                  pl.BlockSpec((tm, D), lambda i: (i, 0)), pl.BlockSpec((tm, TOP_K), lambda i: (i, 0)),
                  _full(ln2_g), _full(ln2_b), pl.BlockSpec(memory_space=pl.ANY)],
        out_specs=pl.BlockSpec((tm, D), lambda i: (i, 0)),
        scratch_shapes=[pltpu.VMEM((2, TOP_K, tm, D), F32), pltpu.SemaphoreType.DMA((2,))],
        compiler_params=_cparams(("arbitrary",)),
        name="combine",
    )(dest, dest, h1, gates, ln2_g, ln2_b, ybuf)


def _routing_plan(route):
    M = route.shape[0]
    eid = route[:, :TOP_K].astype(jnp.int32).reshape(-1)
    gates = route[:, TOP_K:2 * TOP_K]
    A = M * TOP_K
    onehot = (eid[:, None] == jnp.arange(N_EXPERTS, dtype=eid.dtype)[None, :]).astype(jnp.int32)
    csum = jnp.cumsum(onehot, axis=0)
    rank = jnp.sum(csum * onehot, axis=1) - 1
    counts = csum[-1]
    pcounts = (counts + MOE_TILE - 1) // MOE_TILE * MOE_TILE
    pends = jnp.cumsum(pcounts)
    pstarts = pends - pcounts
    dest = (jnp.sum(onehot * pstarts[None, :], axis=1) + rank).astype(jnp.int32)
    n_tiles = (A + N_EXPERTS * (MOE_TILE - 1) + MOE_TILE - 1) // MOE_TILE
    n_valid = (pends[-1] // MOE_TILE).astype(jnp.int32)
    tile_start = jnp.minimum(jnp.arange(n_tiles, dtype=jnp.int32) * MOE_TILE, pends[-1] - 1)
    tile_expert = jnp.sum((pends[None, :] <= tile_start[:, None]).astype(jnp.int32), axis=1)
    tile_expert = jnp.minimum(tile_expert, N_EXPERTS - 1).astype(jnp.int32)
    return gates, dest, n_tiles * MOE_TILE, tile_expert, n_valid.reshape(1)


def kernel(x, meta, ln0_g, ln0_b, w_in, mu_shift, w0, w_decay_up, a0, w_a_up, w_g_up, k_k, k_a, r_k, gn_g, gn_b, conv_w, conv_b, w_rg, b_rg, w_ig, b_ig, lru_lambda, w_out, ln1_g, ln1_b, w_router_grp, b_router_grp, w_router_exp, b_router_exp, w_exp_gate, w_exp_up, w_exp_down, ln2_g, ln2_b):
    B, T, D = x.shape
    assert D == D_MODEL and T % 512 == 0 and w_in.shape[0] == 1
    assert (B * T * TOP_K) % INVERT_BLOCK == 0
    row = lambda p: p.reshape(1, -1).astype(F32)
    n_rw = 3 * RWKV_W
    w_in0 = w_in[0]

    def slots(p):
        pad = lambda a, n: jnp.pad(a, [(0, 0)] * (a.ndim - 1) + [(0, n - a.shape[-1])])
        zw = p[..., n_rw:n_rw + DECAY_RANK]
        za = p[..., n_rw + DECAY_RANK:n_rw + DECAY_RANK + AAA_RANK]
        zg = p[..., n_rw + DECAY_RANK + AAA_RANK:n_rw + DECAY_RANK + AAA_RANK + GATE_RANK]
        return jnp.concatenate([p[..., :n_rw], pad(zw, LANE), pad(za, LANE), pad(zg, ZG_SLOT)], axis=-1)

    rwkv_cols = n_rw + DECAY_RANK + AAA_RANK + GATE_RANK
    w_r = slots(w_in0[:, :rwkv_cols]).astype(BF16)
    w_l = w_in0[:, rwkv_cols:].astype(BF16)
    ur, ul, ur_t, ul_t = _in_projection(x, meta, row(ln0_g), row(ln0_b), w_r, w_l)

    pad_rows = lambda a, n: jnp.pad(a, ((0, n - a.shape[0]), (0, 0)))
    rwkv_params = (slots(mu_shift[0][None, :]).astype(F32), row(w0[0]), pad_rows(w_decay_up[0], LANE),
                   row(a0[0]), pad_rows(w_a_up[0], LANE), pad_rows(w_g_up[0], ZG_SLOT).astype(BF16),
                   row(k_k[0]), row(k_a[0]), row(r_k[0]), row(gn_g[0]), row(gn_b[0]))
    y_rwkv = _rwkv_mixer(ur, ur_t, rwkv_params)

    blockdiag = lambda w: jax.scipy.linalg.block_diag(*[w[i] for i in range(LRU_BLOCKS)]).astype(BF16)
    lru_params = (conv_w[0], row(conv_b[0]), blockdiag(w_rg[0]), row(b_rg[0]), blockdiag(w_ig[0]), row(b_ig[0]),
                  row(lru_lambda[0]))
    y_lru = _lru_mixer(ul, ul_t, lru_params)

    w_rt = jnp.concatenate([w_router_grp[0], w_router_exp[0]], axis=1)
    w_rt = jnp.pad(w_rt, ((0, 0), (0, LANE - w_rt.shape[1])))
    wrt_hi = w_rt.astype(BF16)
    wrt_lo = (w_rt - wrt_hi.astype(F32)).astype(BF16)
    b_rt = jnp.concatenate([b_router_grp[0], b_router_exp[0]])
    b_rt = jnp.pad(b_rt, (0, LANE - b_rt.shape[0])).reshape(1, LANE)
    wo = w_out[0].astype(BF16)
    h1, route = _out_projection(x, y_rwkv, y_lru, row(ln0_g), row(ln0_b), wo[:RWKV_W], wo[RWKV_W:],
                                row(ln1_g[0]), row(ln1_b[0]), wrt_hi, wrt_lo, b_rt)

    M = B * T
    h1 = h1.reshape(M, D)
    gates, dest, n_slots, tile_expert, n_valid = _routing_plan(route.reshape(M, LANE))
    row_asg = _invert_slots(dest, n_slots)
    ybuf = _moe_experts(h1, row_asg, tile_expert, n_valid, w_exp_gate[0], w_exp_up[0], w_exp_down[0])
    out = _combine(h1, ybuf, dest, gates, row(ln2_g[0]), row(ln2_b[0]))
    return out.reshape(B, T, D)
```

```python
import math

import jax
import jax.numpy as jnp
from jax import lax
from jax.experimental import pallas as pl
from jax.experimental.pallas import tpu as pltpu

F32 = jnp.float32
BF16 = jnp.bfloat16

D_MODEL = 1024
N_META = 16
RWKV_W = 512
RWKV_HEAD = 64
DECAY_RANK = 64
AAA_RANK = 64
GATE_RANK = 160
LRU_W = 512
LRU_BLOCKS = 8
CONV_WIDTH = 4
LRU_C = 8.0
N_GROUPS = 4
EXPERTS_PER_GROUP = 8
N_EXPERTS = N_GROUPS * EXPERTS_PER_GROUP
TOP_K = 2
D_EXPERT = 512
LN_EPS = 1e-5
GN_EPS = 64e-5
DEEPNORM_ALPHA = 2.0 ** 0.25

LANE = 128
OFF_R, OFF_K, OFF_V = 0, RWKV_W, 2 * RWKV_W
OFF_ZW = 3 * RWKV_W
OFF_ZA = OFF_ZW + LANE
OFF_ZG = OFF_ZA + LANE
ZG_SLOT = 2 * LANE
UR_W = OFF_ZG + ZG_SLOT
UL_W = 2 * LRU_W

TAIL = 128
CHUNK = 64
HEADS_PER_GROUP = 4
GW = HEADS_PER_GROUP * RWKV_HEAD
N_HGROUPS = RWKV_W // GW
LRU_TILE = 128
MOE_TILE = 256
COMBINE_TILE = 256
INVERT_BLOCK = 4096
V7X_VMEM_BYTES = 64 * 1024 * 1024
VMEM_LIMIT = V7X_VMEM_BYTES - 8 * 1024 * 1024


def _cparams(sem, flags=None):
    return pltpu.CompilerParams(dimension_semantics=sem, vmem_limit_bytes=VMEM_LIMIT, flags=flags)


def _layer_norm(x, g, b):
    mu = jnp.mean(x, -1, keepdims=True)
    xc = x - mu
    var = jnp.mean(xc * xc, -1, keepdims=True)
    return xc * lax.rsqrt(var + LN_EPS) * g + b


def _dot(a, b):
    return jnp.dot(a, b, preferred_element_type=F32)


def _dot_hi(a, b):
    return jnp.dot(a, b, precision=lax.Precision.HIGHEST, preferred_element_type=F32)


def _dot_nt(a, b):
    return lax.dot_general(a, b, (((1,), (1,)), ((), ())), preferred_element_type=F32)


def _dot_tn(a, b):
    return lax.dot_general(a, b, (((0,), (0,)), ((), ())), preferred_element_type=F32)


def _split_dot(x, w_bf16):
    hi = x.astype(BF16)
    lo = (x - hi.astype(F32)).astype(BF16)
    return _dot(hi, w_bf16) + _dot(lo, w_bf16)


def _full(a):
    return pl.BlockSpec(a.shape, lambda *_: (0,) * a.ndim)


def _inproj_kernel(x_ref, g_ref, b_ref, wr_ref, wl_ref, ur_ref, ul_ref):
    h = _layer_norm(x_ref[0], g_ref[...], b_ref[...]).astype(BF16)
    ur_ref[0] = _dot(h, wr_ref[...])
    ul_ref[0] = _dot(h, wl_ref[...])


def _inproj_tail_kernel(x_ref, g_ref, b_ref, wr_ref, wl_ref, ur_ref, ul_ref):
    h = _layer_norm(x_ref[...], g_ref[...], b_ref[...]).astype(BF16)
    rows = lax.broadcasted_iota(jnp.int32, (TAIL, 1), 0)
    valid = (rows >= TAIL - N_META).astype(F32)
    ur_ref[...] = _dot(h, wr_ref[...]) * valid
    ul_ref[...] = _dot(h, wl_ref[...]) * valid


def _in_projection(x, meta, ln0_g, ln0_b, w_r, w_l):
    B, T, D = x.shape
    tm = 512
    ur, ul = pl.pallas_call(
        _inproj_kernel,
        out_shape=(jax.ShapeDtypeStruct((B, T, UR_W), F32), jax.ShapeDtypeStruct((B, T, UL_W), F32)),
        grid=(B, T // tm),
        in_specs=[pl.BlockSpec((1, tm, D), lambda b, i: (b, i, 0)), _full(ln0_g), _full(ln0_b), _full(w_r), _full(w_l)],
        out_specs=(pl.BlockSpec((1, tm, UR_W), lambda b, i: (b, i, 0)),
                   pl.BlockSpec((1, tm, UL_W), lambda b, i: (b, i, 0))),
        compiler_params=_cparams(("parallel", "parallel")),
        name="inproj",
    )(x, ln0_g, ln0_b, w_r, w_l)
    tail_x = jnp.concatenate([jnp.zeros((TAIL - N_META, D), F32), meta.astype(F32)], axis=0)
    ur_t, ul_t = pl.pallas_call(
        _inproj_tail_kernel,
        out_shape=(jax.ShapeDtypeStruct((TAIL, UR_W), F32), jax.ShapeDtypeStruct((TAIL, UL_W), F32)),
        grid=(1,),
        in_specs=[_full(tail_x), _full(ln0_g), _full(ln0_b), _full(w_r), _full(w_l)],
        out_specs=(pl.BlockSpec((TAIL, UR_W), lambda i: (0, 0)), pl.BlockSpec((TAIL, UL_W), lambda i: (0, 0))),
        compiler_params=_cparams(("arbitrary",)),
        name="inproj_tail",
    )(tail_x, ln0_g, ln0_b, w_r, w_l)
    return ur, ul, ur_t, ul_t


def _rwkv_kernel(u_ref, ut_ref, mu_ref, w0_ref, wdu_ref, a0_ref, wau_ref, wgu_ref, kk_ref, ka_ref, rk_ref,
                 gng_ref, gnb_ref, bones_ref, tril_ref, eye_ref, bm_ref, msl_ref, mil_ref,
                 m8_ref, m16_ref, m32_ref, m64_ref, y_ref, s_ref, prev_ref):
    c = pl.program_id(0)
    nb = u_ref.shape[0]
    rows_of = lambda b: slice(b * CHUNK, (b + 1) * CHUNK)
    per_batch = lambda f: jnp.concatenate([f(b) for b in range(nb)], axis=0)

    @pl.when(c == 0)
    def _():
        s_ref[...] = jnp.zeros_like(s_ref)
        prev_ref[...] = jnp.zeros_like(prev_ref)

    u_x = u_ref[...].reshape(nb * CHUNK, UR_W)
    u = jnp.where(c == 0, per_batch(lambda b: ut_ref[...]), u_x)
    row = lax.broadcasted_iota(jnp.int32, u.shape, 0)
    prev_rows = per_batch(lambda b: jnp.broadcast_to(prev_ref[b:b + 1, :], (CHUNK, UR_W)))
    u_prev = jnp.where(jnp.bitwise_and(row, CHUNK - 1) == 0, prev_rows, pltpu.roll(u, 1, 0))
    for b in range(nb):
        prev_ref[b:b + 1, :] = u[(b + 1) * CHUNK - 1:(b + 1) * CHUNK, :]
    x = u + (u_prev - u) * mu_ref[...]
    r = x[:, OFF_R:OFF_R + RWKV_W]
    k = x[:, OFF_K:OFF_K + RWKV_W]
    v = x[:, OFF_V:OFF_V + RWKV_W]
    zw = x[:, OFF_ZW:OFF_ZW + LANE]
    za = x[:, OFF_ZA:OFF_ZA + LANE]
    zg = x[:, OFF_ZG:OFF_ZG + ZG_SLOT]

    bones = bones_ref[...]
    head_sum = lambda t: _split_dot(t, bones)

    z = w0_ref[...] + _dot_hi(jnp.tanh(zw), wdu_ref[...])
    logw = -math.exp(-0.5) * jax.nn.sigmoid(z)
    a = jax.nn.sigmoid(a0_ref[...] + _dot_hi(za, wau_ref[...]))
    g = _dot(jax.nn.sigmoid(zg).astype(BF16), wgu_ref[...])
    kk = k * kk_ref[...]
    kk = kk / jnp.maximum(jnp.sqrt(head_sum(kk * kk)), 1e-12)
    k = k * (1.0 + (a - 1.0) * ka_ref[...])
    kka = kk * a

    cl = _dot_hi(tril_ref[...], logw)
    cl_last = per_batch(lambda b: jnp.broadcast_to(cl[(b + 1) * CHUNK - 1:(b + 1) * CHUNK, :], (CHUNK, RWKV_W)))
    e_neg = jnp.exp(-cl)
    e_end = jnp.exp(cl_last - cl)
    rt = r * jnp.exp(cl)
    kt = k * e_neg
    at = -kk * jnp.exp(cl - logw)
    bt = kka * e_neg
    kw = k * e_end
    bw = kka * e_end
    w_end = jnp.exp(cl_last)

    bm = bm_ref[...]
    eye = eye_ref[...]
    msl = msl_ref[...]
    mil = mil_ref[...]
    lane = lax.broadcasted_iota(jnp.int32, (2 * GW, LANE), 1)
    tile4 = lambda t: jnp.concatenate([t] * HEADS_PER_GROUP, axis=0)
    fold4 = lambda t: sum(t[i * CHUNK:(i + 1) * CHUNK] for i in range(HEADS_PER_GROUP))
    wide = lambda t: jnp.concatenate([t, t], axis=1)
    y_rows = []
    for b in range(nb):
        rb = rows_of(b)
        ys = []
        for hg in range(N_HGROUPS):
            sl = slice(hg * GW, (hg + 1) * GW)
            v4 = v[rb, sl]
            lhs = jnp.concatenate([tile4(at[rb, sl]) * bm, tile4(rt[rb, sl]) * bm], axis=0).astype(BF16)
            rhs = jnp.concatenate([bt[rb, sl], kt[rb, sl]], axis=0).astype(BF16)
            aa = _dot_nt(lhs, rhs)
            aa_sw = pltpu.roll(aa, CHUNK, 1)
            vs_b = wide(jnp.where(lane < CHUNK, aa, aa_sw))
            vs_k = wide(jnp.where(lane < CHUNK, aa_sw, aa))
            a_ab = vs_b[:GW] * msl
            a_ak = (vs_k[:GW] * msl).astype(BF16)
            a_rb = (vs_b[GW:] * mil).astype(BF16)
            a_rk = (vs_k[GW:] * mil).astype(BF16)

            a0 = (a_ab * m8_ref[...]).astype(BF16)
            a2 = _dot(a0, a0).astype(BF16)
            a4 = _dot(a2, a2).astype(BF16)
            p1 = eye + a0.astype(F32)
            p1 = p1 + _dot(p1.astype(BF16), a2)
            t = p1 + _dot(p1.astype(BF16), a4)
            for m_ref in (m16_ref, m32_ref, m64_ref):
                tb = t.astype(BF16)
                off = (a_ab * m_ref[...]).astype(BF16)
                t = t + _dot(_dot(tb, off).astype(BF16), tb)
            tb = t.astype(BF16)

            s = s_ref[b, hg]
            sb = s.astype(BF16)
            vt = tile4(v4).astype(BF16)
            xx = _dot_nt(lhs[:GW], sb) + _dot(a_ak, vt)
            uu = _dot(tb, xx.astype(BF16)) * bm
            yy = (_dot_nt(lhs[GW:], sb) + _dot(a_rb, uu.astype(BF16)) + _dot(a_rk, vt)) * bm
            ys.append(fold4(yy))
            u4 = fold4(uu)
            upd = _dot_tn(jnp.concatenate([u4, v4], axis=0).astype(BF16),
                          jnp.concatenate([bw[rb, sl], kw[rb, sl]], axis=0).astype(BF16))
            s_ref[b, hg] = s * w_end[b * CHUNK:b * CHUNK + 1, sl] + upd * bm
        y_rows.append(jnp.concatenate(ys, axis=1))

    y = jnp.concatenate(y_rows, axis=0)
    inv_n = 1.0 / RWKV_HEAD
    ym = head_sum(y) * inv_n
    yc = y - ym
    yv = head_sum(yc * yc) * inv_n
    yn = yc * lax.rsqrt(yv + GN_EPS) * gng_ref[...] + gnb_ref[...]
    bonus = head_sum(r * k * rk_ref[...]) * v
    y_ref[...] = ((yn + bonus) * g).astype(y_ref.dtype).reshape(y_ref.shape)


def _rwkv_masks(nb):
    i = jnp.arange(GW)[:, None]
    j = jnp.arange(GW)[None, :]
    same = lambda n: (i // n) == (j // n)
    f = lambda m: m.astype(F32)
    bm = f(same(RWKV_HEAD))
    msl = f(same(RWKV_HEAD) & (i > j))
    mil = f(same(RWKV_HEAD) & (i >= j))
    m8 = f(same(8))
    m16 = f(same(16) & ~same(8))
    m32 = f(same(32) & ~same(16))
    m64 = f(same(64) & ~same(32))
    eye = f(i == j)
    ti = jnp.arange(nb * CHUNK)
    tril = f((ti[:, None] >= ti[None, :]) & ((ti[:, None] // CHUNK) == (ti[None, :] // CHUNK)))
    hi = jnp.arange(RWKV_W)
    bones = ((hi[:, None] // RWKV_HEAD) == (hi[None, :] // RWKV_HEAD)).astype(BF16)
    return bones, tril, eye, bm, msl, mil, m8, m16, m32, m64


def _rwkv_mixer(ur, ur_tail, params):
    B, T, _ = ur.shape
    consts = _rwkv_masks(B)
    x_map = lambda c: (0, jnp.maximum(c - 1, 0), 0)
    return pl.pallas_call(
        _rwkv_kernel,
        out_shape=jax.ShapeDtypeStruct((B, T, RWKV_W), BF16),
        grid=(T // CHUNK + 1,),
        in_specs=[pl.BlockSpec((B, CHUNK, UR_W), x_map),
                  pl.BlockSpec((CHUNK, UR_W), lambda c: (TAIL // CHUNK - 1, 0))]
                 + [_full(p) for p in params] + [_full(m) for m in consts],
        out_specs=pl.BlockSpec((B, CHUNK, RWKV_W), x_map),
        scratch_shapes=[pltpu.VMEM((B, N_HGROUPS, GW, GW), F32), pltpu.VMEM((B, UR_W), F32)],
        compiler_params=_cparams(("arbitrary",)),
        name="rwkv7",
    )(ur, ur_tail, *params, *consts)


def _rwkv_pipe_mixer(ur, ur_tail, params):
    B, T, _ = ur.shape
    blk = TAIL
    assert T % blk == 0 and blk % CHUNK == 0
    n_blocks = T // blk
    rows = B * blk
    consts = _rwkv_masks(rows // CHUNK)
    in_map = lambda s: (0, jnp.clip(s - 1, 0, n_blocks - 1), 0)
    out_map = lambda s: (0, jnp.clip(s - 2, 0, n_blocks - 1), 0)
    slot2 = lambda w, dt: pltpu.VMEM((2, rows, w), dt)
    return pl.pallas_call(
        _rwkv_pipe_kernel,
        out_shape=jax.ShapeDtypeStruct((B, T, RWKV_W), BF16),
        grid=(n_blocks + 2,),
        in_specs=[pl.BlockSpec((B, blk, UR_W), in_map), _full(ur_tail)]
                 + [_full(p) for p in params] + [_full(m) for m in consts],
        out_specs=pl.BlockSpec((B, blk, RWKV_W), out_map),
        scratch_shapes=[pltpu.VMEM((B, N_HGROUPS, GW, GW), F32), pltpu.VMEM((B, UR_W), F32),
                        slot2(RWKV_W, F32), slot2(RWKV_W, BF16), slot2(RWKV_W, BF16), slot2(RWKV_W, F32),
                        pltpu.VMEM((2, 8 * rows // CHUNK, RWKV_W), F32), slot2(RWKV_W, F32), slot2(RWKV_W, F32),
                        pltpu.VMEM((rows, RWKV_W), F32)],
        compiler_params=_cparams(("arbitrary",)),
        name="rwkv7",
    )(ur, ur_tail, *params, *consts)


def _rwkv_pipe_kernel(u_ref, ut_ref, mu_ref, w0_ref, wdu_ref, a0_ref, wau_ref, wgu_ref, kk_ref, ka_ref, rk_ref,
                      gng_ref, gnb_ref, bones_ref, tril_ref, eye_ref, bm_ref, msl_ref, mil_ref,
                      m8_ref, m16_ref, m32_ref, m64_ref, y_ref,
                      s_ref, prev_ref, y0_s, q_s, mc_s, nc_s, we_s, bonus_s, g_s, yraw_s):
    s_id = pl.program_id(0)
    nb = u_ref.shape[0]
    blk = u_ref.shape[1]
    npc = blk // CHUNK
    nseq = nb * npc
    seq_rows = lambda q: slice(q * CHUNK, (q + 1) * CHUNK)
    per_seq = lambda f: jnp.concatenate([f(q) for q in range(nseq)], axis=0)
    w_slot = lax.rem(s_id, 2)
    r_slot = 1 - w_slot

    @pl.when(s_id == 0)
    def _():
        s_ref[...] = jnp.zeros_like(s_ref)
        prev_ref[...] = jnp.zeros_like(prev_ref)
        for ref in (y0_s, q_s, mc_s, nc_s, we_s, bonus_s, g_s):
            ref[1] = jnp.zeros(ref.shape[1:], ref.dtype)

    u_x = u_ref[...].reshape(nb * blk, UR_W)
    u = jnp.where(s_id == 0, jnp.concatenate([ut_ref[...]] * nb, axis=0), u_x)
    row = lax.broadcasted_iota(jnp.int32, u.shape, 0)
    prev_rows = jnp.concatenate([jnp.broadcast_to(prev_ref[b:b + 1, :], (blk, UR_W)) for b in range(nb)], axis=0)
    u_prev = jnp.where(jnp.bitwise_and(row, blk - 1) == 0, prev_rows, pltpu.roll(u, 1, 0))
    for b in range(nb):
        prev_ref[b:b + 1, :] = u[(b + 1) * blk - 1:(b + 1) * blk, :]
    x = u + (u_prev - u) * mu_ref[...]
    r = x[:, OFF_R:OFF_R + RWKV_W]
    k = x[:, OFF_K:OFF_K + RWKV_W]
    v = x[:, OFF_V:OFF_V + RWKV_W]
    zw = x[:, OFF_ZW:OFF_ZW + LANE]
    za = x[:, OFF_ZA:OFF_ZA + LANE]
    zg = x[:, OFF_ZG:OFF_ZG + ZG_SLOT]

    bones = bones_ref[...]
    head_sum = lambda t: _dot(t.astype(BF16), bones)

    z = w0_ref[...] + _dot(jnp.tanh(zw).astype(BF16), wdu_ref[...])
    logw = -math.exp(-0.5) * jax.nn.sigmoid(z)
    a = jax.nn.sigmoid(a0_ref[...] + _dot(za.astype(BF16), wau_ref[...]))
    g = _dot(jax.nn.sigmoid(zg).astype(BF16), wgu_ref[...])
    kk = k * kk_ref[...]
    kk = kk / jnp.maximum(jnp.sqrt(head_sum(kk * kk)), 1e-12)
    k = k * (1.0 + (a - 1.0) * ka_ref[...])
    kka = kk * a
    bonus_s[w_slot] = head_sum(r * k * rk_ref[...]) * v
    g_s[w_slot] = g

    cl = logw
    row_in_chunk = jnp.bitwise_and(lax.broadcasted_iota(jnp.int32, cl.shape, 0), CHUNK - 1)
    d = 1
    while d < CHUNK:
        cl = cl + jnp.where(row_in_chunk >= d, pltpu.roll(cl, d, 0), 0.0)
        d *= 2
    cl_last = per_seq(lambda q: jnp.broadcast_to(cl[(q + 1) * CHUNK - 1:(q + 1) * CHUNK, :], (CHUNK, RWKV_W)))
    e_neg = jnp.exp(-cl)
    e_end = jnp.exp(cl_last - cl)
    rt = r * jnp.exp(cl)
    kt = k * e_neg
    at = -kk * jnp.exp(cl - logw)
    bt = kka * e_neg
    kw = k * e_end
    bw = kka * e_end
    w_end = jnp.exp(cl_last)

    bm = bm_ref[...]
    bm16 = bm.astype(BF16)
    eye = eye_ref[...]
    msl = msl_ref[...]
    mil = mil_ref[...]
    lane = lax.broadcasted_iota(jnp.int32, (2 * GW, LANE), 1)
    tile4 = lambda t: jnp.concatenate([t] * HEADS_PER_GROUP, axis=0)
    fold4 = lambda t: sum(t[i * CHUNK:(i + 1) * CHUNK] for i in range(HEADS_PER_GROUP))
    wide = lambda t: jnp.concatenate([t, t], axis=1)
    for q in range(nseq):
        we_s[w_slot, q * 8:(q + 1) * 8, :] = w_end[q * CHUNK:q * CHUNK + 8, :]

    probs = [(seq_rows(q), slice(hg * GW, (hg + 1) * GW)) for q in range(nseq) for hg in range(N_HGROUPS)]
    each = lambda f, *ls: [f(*xs) for xs in zip(*ls)]
    b16 = lambda t: t.astype(BF16)
    v4 = [v[rq, sl] for rq, sl in probs]
    r_st = [tile4(rt[rq, sl]) * bm for rq, sl in probs]
    a_st = [b16(tile4(at[rq, sl]) * bm) for rq, sl in probs]
    lhs = each(lambda x, y: jnp.concatenate([x, b16(y)], axis=0), a_st, r_st)
    rhs = [b16(jnp.concatenate([bt[rq, sl], kt[rq, sl]], axis=0)) for rq, sl in probs]
    aa = each(_dot_nt, lhs, rhs)
    aa_sw = each(lambda t: pltpu.roll(t, CHUNK, 1), aa)
    vs_b = each(lambda x, y: wide(jnp.where(lane < CHUNK, x, y)), aa, aa_sw)
    vs_k = each(lambda x, y: wide(jnp.where(lane < CHUNK, y, x)), aa, aa_sw)
    a_ab = each(lambda t: t[:GW] * msl, vs_b)
    a_ak = each(lambda t: b16(t[:GW] * msl), vs_k)
    a_rb = each(lambda t: b16(t[GW:] * mil), vs_b)
    a_rk = each(lambda t: b16(t[GW:] * mil), vs_k)

    a0 = each(lambda t: b16(t * m8_ref[...]), a_ab)
    a2 = each(lambda t: b16(_dot(t, t)), a0)
    a4 = each(lambda t: b16(_dot(t, t)), a2)
    p1 = each(lambda t: eye + t.astype(F32), a0)
    p1 = each(lambda p, t: p + _dot(b16(p), t), p1, a2)
    tt = each(lambda p, t: p + _dot(b16(p), t), p1, a4)
    for m_ref in (m16_ref, m32_ref, m64_ref):
        tb = each(b16, tt)
        off = each(lambda t: b16(t * m_ref[...]), a_ab)
        half = each(lambda x, y: b16(_dot(x, y)), tb, off)
        tt = each(lambda t, x, y: t + _dot(x, y), tt, half, tb)
    tb = each(b16, tt)

    vt = each(lambda t: b16(tile4(t)), v4)
    x0 = each(lambda x, y: b16(_dot(x, y)), a_ak, vt)
    u0 = each(lambda x, y: _dot(x, y) * bm, tb, x0)
    ta = each(_dot, tb, a_st)
    y0 = each(lambda x, y, z, w: (_dot(x, b16(y)) + _dot(z, w)) * bm, a_rb, u0, a_rk, vt)
    qq = each(lambda x, y, z: x + _dot(y, b16(z)), r_st, a_rb, ta)
    left = each(lambda x, y, z: b16(jnp.concatenate([jnp.concatenate([fold4(x), fold4(y)], axis=1),
                                                     jnp.concatenate([jnp.zeros_like(z), z], axis=1)], axis=0)),
                ta, u0, v4)
    right = [b16(jnp.concatenate([bw[rq, sl], kw[rq, sl]], axis=0)) for rq, sl in probs]
    mn = each(_dot_tn, left, right)
    for i, (rq, sl) in enumerate(probs):
        y0_s[w_slot, rq, sl] = fold4(y0[i])
        q_s[w_slot, rq, sl] = b16(fold4(qq[i]))
        mc_s[w_slot, rq, sl] = b16(fold4(mn[i][:GW] * bm))
        nc_s[w_slot, rq, sl] = fold4(mn[i][GW:] * bm)

    for b in range(nb):
        for hg in range(N_HGROUPS):
            sl = slice(hg * GW, (hg + 1) * GW)
            s = s_ref[b, hg]
            for j in range(npc):
                q = b * npc + j
                rq = seq_rows(q)
                sb = s.astype(BF16)
                q_st = tile4(q_s[r_slot, rq, sl]) * bm16
                mc_bd = tile4(mc_s[r_slot, rq, sl]) * bm16
                nc_bd = tile4(nc_s[r_slot, rq, sl]) * bm
                yraw_s[rq, sl] = y0_s[r_slot, rq, sl] + fold4(_dot_nt(q_st, sb))
                s = s * we_s[r_slot, q * 8:q * 8 + 1, sl] + _dot(sb, mc_bd) + nc_bd
            s_ref[b, hg] = s

    y = yraw_s[...]
    inv_n = 1.0 / RWKV_HEAD
    ym = head_sum(y) * inv_n
    yc = y - ym
    yv = head_sum(yc * yc) * inv_n
    yn = yc * lax.rsqrt(yv + GN_EPS) * gng_ref[...] + gnb_ref[...]
    y_ref[...] = ((yn + bonus_s[r_slot]) * g_s[r_slot]).astype(y_ref.dtype).reshape(y_ref.shape)


def _gelu_tanh(x):
    return 0.5 * x * (1.0 + jnp.tanh(math.sqrt(2.0 / math.pi) * (x + 0.044715 * (x * x * x))))


def _lru_kernel(u_ref, ut_ref, cw_ref, cb_ref, wrg_ref, brg_ref, wig_ref, big_ref, lam_ref, y_ref,
                xprev_ref, hprev_ref):
    c = pl.program_id(1)

    @pl.when(c == 0)
    def _():
        xprev_ref[...] = jnp.zeros_like(xprev_ref)
        hprev_ref[...] = jnp.zeros_like(hprev_ref)

    u = jnp.where(c == 0, ut_ref[...], u_ref[0])
    xl = u[:, :LRU_W]
    gl = u[:, LRU_W:]
    row = lax.broadcasted_iota(jnp.int32, (LRU_TILE, LRU_W), 0)
    row8 = lax.broadcasted_iota(jnp.int32, (8, LRU_W), 0)
    xprev = xprev_ref[...]
    xc = cb_ref[...] + cw_ref[CONV_WIDTH - 1:CONV_WIDTH, :] * xl
    for d in range(1, CONV_WIDTH):
        rolled = pltpu.roll(xl, d, 0)
        head = jnp.where(row8 < d, pltpu.roll(xprev, d, 0), rolled[:8])
        shifted = jnp.concatenate([head, rolled[8:]], axis=0)
        xc = xc + cw_ref[CONV_WIDTH - 1 - d:CONV_WIDTH - d, :] * shifted
    xprev_ref[...] = xl[LRU_TILE - 8:, :]

    xcb = xc.astype(BF16)
    gate_r = jax.nn.sigmoid(_dot(xcb, wrg_ref[...]) + brg_ref[...])
    gate_i = jax.nn.sigmoid(_dot(xcb, wig_ref[...]) + big_ref[...])
    lam = lam_ref[...]
    log_sig = -(jnp.maximum(-lam, 0.0) + jnp.log1p(jnp.exp(-jnp.abs(lam))))
    log_a = LRU_C * gate_r * log_sig
    a = jnp.exp(log_a)
    mult = jnp.sqrt(jnp.maximum(1.0 - jnp.exp(2.0 * log_a), 0.0))
    b = mult * gate_i * xc
    b = jnp.where((c == 0) & (row < LRU_TILE - N_META), 0.0, b)

    d = 1
    while d < LRU_TILE:
        keep = row >= d
        a_sh = jnp.where(keep, pltpu.roll(a, d, 0), 1.0)
        b_sh = jnp.where(keep, pltpu.roll(b, d, 0), 0.0)
        b = a * b_sh + b
        a = a * a_sh
        d *= 2
    h = b + a * hprev_ref[...]
    hprev_ref[...] = h[LRU_TILE - 1:, :]
    y_ref[0] = (h * _gelu_tanh(gl)).astype(y_ref.dtype)


def _lru_mixer(ul, ul_tail, params):
    B, T, _ = ul.shape
    assert TAIL == LRU_TILE
    x_map = lambda b, c: (b, jnp.maximum(c - 1, 0), 0)
    return pl.pallas_call(
        _lru_kernel,
        out_shape=jax.ShapeDtypeStruct((B, T, LRU_W), BF16),
        grid=(B, T // LRU_TILE + 1),
        in_specs=[pl.BlockSpec((1, LRU_TILE, UL_W), x_map), _full(ul_tail)] + [_full(p) for p in params],
        out_specs=pl.BlockSpec((1, LRU_TILE, LRU_W), x_map),
        scratch_shapes=[pltpu.VMEM((8, LRU_W), F32), pltpu.VMEM((1, LRU_W), F32)],
        compiler_params=_cparams(("parallel", "arbitrary")),
        name="rglru",
    )(ul, ul_tail, *params)


def _route(lg):
    lane = lax.broadcasted_iota(jnp.int32, lg.shape, 1)
    neg = jnp.float32(-jnp.inf)
    rmax = lambda t: jnp.max(t, axis=1, keepdims=True)
    first = lambda hit: jnp.min(jnp.where(hit, lane, LANE), axis=1, keepdims=True)
    is_grp = lane < N_GROUPS
    gl = jnp.where(is_grp, lg, neg)
    gmax = rmax(gl)
    g_sel = first(gl == gmax)
    p_g = 1.0 / jnp.sum(jnp.where(is_grp, jnp.exp(lg - gmax), 0.0), axis=1, keepdims=True)
    ex = lane - N_GROUPS
    in_grp = (ex >= 0) & (ex < N_EXPERTS) & (jnp.right_shift(ex, 3) == g_sel)
    el = jnp.where(in_grp, lg, neg)
    v1 = rmax(el)
    i1 = first(el == v1)
    el2 = jnp.where(lane == i1, neg, el)
    v2 = rmax(el2)
    i2 = first(el2 == v2)
    t = jnp.exp(v2 - v1)
    gate1 = p_g / (1.0 + t)
    gate2 = p_g * t / (1.0 + t)
    e1 = (i1 - N_GROUPS).astype(F32)
    e2 = (i2 - N_GROUPS).astype(F32)
    return jnp.where(lane == 0, e1, jnp.where(lane == 1, e2, jnp.where(lane == 2, gate1, jnp.where(lane == 3, gate2, 0.0))))


def _outproj_kernel(x_ref, yr_ref, yl_ref, g0_ref, b0_ref, wor_ref, wol_ref, g1_ref, b1_ref,
                    wrt_hi_ref, wrt_lo_ref, brt_ref, h1_ref, rt_ref):
    h0 = _layer_norm(x_ref[0], g0_ref[...], b0_ref[...])
    mix = _dot(yr_ref[0], wor_ref[...]) + _dot(yl_ref[0], wol_ref[...])
    h1 = _layer_norm(DEEPNORM_ALPHA * h0 + mix, g1_ref[...], b1_ref[...])
    h1_ref[0] = h1
    hi = h1.astype(BF16)
    lo = (h1 - hi.astype(F32)).astype(BF16)
    w_hi = wrt_hi_ref[...]
    lg = _dot(hi, w_hi) + (_dot(hi, wrt_lo_ref[...]) + _dot(lo, w_hi)) + brt_ref[...]
    rt_ref[0] = _route(lg)


def _out_projection(x, y_rwkv, y_lru, ln0_g, ln0_b, wo_r, wo_l, ln1_g, ln1_b, wrt_hi, wrt_lo, brt):
    B, T, D = x.shape
    tm = 512
    rows = lambda w: pl.BlockSpec((1, tm, w), lambda b, i: (b, i, 0))
    return pl.pallas_call(
        _outproj_kernel,
        out_shape=(jax.ShapeDtypeStruct((B, T, D), F32), jax.ShapeDtypeStruct((B, T, LANE), F32)),
        grid=(B, T // tm),
        in_specs=[rows(D), rows(RWKV_W), rows(LRU_W), _full(ln0_g), _full(ln0_b), _full(wo_r), _full(wo_l),
                  _full(ln1_g), _full(ln1_b), _full(wrt_hi), _full(wrt_lo), _full(brt)],
        out_specs=(rows(D), rows(LANE)),
        compiler_params=_cparams(("parallel", "parallel")),
        name="outproj",
    )(x, y_rwkv, y_lru, ln0_g, ln0_b, wo_r, wo_l, ln1_g, ln1_b, wrt_hi, wrt_lo, brt)


def _invert_kernel(dest_ref, out_ref):
    i = pl.program_id(0)

    @pl.when(i == 0)
    def _():
        def zero(j, carry):
            out_ref[j] = 0
            return carry
        lax.fori_loop(0, out_ref.shape[0], zero, 0, unroll=8)

    base = i * INVERT_BLOCK

    def body(j, carry):
        out_ref[dest_ref[j]] = base + j
        return carry

    lax.fori_loop(0, INVERT_BLOCK, body, 0, unroll=8)


def _invert_slots(dest, n_slots):
    A = dest.shape[0]
    return pl.pallas_call(
        _invert_kernel,
        out_shape=jax.ShapeDtypeStruct((n_slots,), jnp.int32),
        grid=(A // INVERT_BLOCK,),
        in_specs=[pl.BlockSpec((INVERT_BLOCK,), lambda i: (i,), memory_space=pltpu.SMEM)],
        out_specs=pl.BlockSpec(memory_space=pltpu.SMEM),
        compiler_params=_cparams(("arbitrary",)),
        name="invert_slots",
    )(dest)


def _row_copy(src_hbm, src_row, dst_ref, dst_row, sem):
    return pltpu.make_async_copy(src_hbm.at[pl.ds(src_row, 1), :], dst_ref.at[pl.ds(dst_row, 1), :], sem)


def _moe_kernel(te_ref, nv_ref, ra_cur_ref, ra_nxt_ref, h_hbm, wg_ref, wu_ref, wd_ref, o_ref,
                xbuf, sem, wgb_ref, wub_ref, wdb_ref):
    i = pl.program_id(0)
    n_valid = nv_ref[0]
    slot = lax.rem(i, 2)

    def start_gather(ra_ref, s):
        def body(j, carry):
            tok = lax.shift_right_logical(ra_ref[j], 1)
            _row_copy(h_hbm, tok, xbuf.at[s], j, sem.at[s]).start()
            return carry
        lax.fori_loop(0, MOE_TILE, body, 0, unroll=8)

    @pl.when(i == 0)
    def _():
        start_gather(ra_cur_ref, 0)

    @pl.when(i + 1 < n_valid)
    def _():
        start_gather(ra_nxt_ref, 1 - slot)

    e = te_ref[i]
    e_prev = te_ref[jnp.maximum(i - 1, 0)]

    @pl.when((i == 0) | (e != e_prev))
    def _():
        wgb_ref[...] = wg_ref[0].astype(BF16)
        wub_ref[...] = wu_ref[0].astype(BF16)
        wdb_ref[...] = wd_ref[0].astype(BF16)

    @pl.when(i < n_valid)
    def _():
        def wait_row(j, carry):
            _row_copy(h_hbm, 0, xbuf.at[slot], j, sem.at[slot]).wait()
            return carry
        lax.fori_loop(0, MOE_TILE, wait_row, 0)
        xb = xbuf[slot].astype(BF16)
        hg = _dot(xb, wgb_ref[...])
        hu = _dot(xb, wub_ref[...])
        mid = (hg * jax.nn.sigmoid(hg) * hu).astype(BF16)
        o_ref[...] = _dot(mid, wdb_ref[...])

    @pl.when(i >= n_valid)
    def _():
        o_ref[...] = jnp.zeros_like(o_ref)


def _moe_experts(h1, row_asg, tile_expert, n_valid, w_gate, w_up, w_down):
    M, D = h1.shape
    n_tiles = row_asg.shape[0] // MOE_TILE
    smem_tile = lambda f: pl.BlockSpec((MOE_TILE,), f, memory_space=pltpu.SMEM)
    grid_spec = pltpu.PrefetchScalarGridSpec(
        num_scalar_prefetch=2,
        grid=(n_tiles,),
        in_specs=[smem_tile(lambda i, te, nv: (i,)),
                  smem_tile(lambda i, te, nv: (jnp.minimum(i + 1, n_tiles - 1),)),
                  pl.BlockSpec(memory_space=pl.ANY),
                  pl.BlockSpec((1, D, D_EXPERT), lambda i, te, nv: (te[i], 0, 0)),
                  pl.BlockSpec((1, D, D_EXPERT), lambda i, te, nv: (te[i], 0, 0)),
                  pl.BlockSpec((1, D_EXPERT, D), lambda i, te, nv: (te[i], 0, 0))],
        out_specs=pl.BlockSpec((MOE_TILE, D), lambda i, te, nv: (i, 0)),
        scratch_shapes=[pltpu.VMEM((2, MOE_TILE, D), F32), pltpu.SemaphoreType.DMA((2,)),
                        pltpu.VMEM((D, D_EXPERT), BF16), pltpu.VMEM((D, D_EXPERT), BF16),
                        pltpu.VMEM((D_EXPERT, D), BF16)],
    )
    return pl.pallas_call(
        _moe_kernel,
        out_shape=jax.ShapeDtypeStruct((n_tiles * MOE_TILE, D), F32),
        grid_spec=grid_spec,
        compiler_params=_cparams(("arbitrary",)),
        name="moe_experts",
    )(tile_expert, n_valid, row_asg, row_asg, h1, w_gate, w_up, w_down)


def _combine_kernel(d_cur_ref, d_nxt_ref, h_ref, gate_ref, g_ref, b_ref, y_hbm, o_ref, ybuf, sem):
    i = pl.program_id(0)
    n = pl.num_programs(0)
    slot = lax.rem(i, 2)

    def start_gather(d_ref, s):
        def body(t, carry):
            for k in range(TOP_K):
                _row_copy(y_hbm, d_ref[TOP_K * t + k], ybuf.at[s, k], t, sem.at[s]).start()
            return carry
        lax.fori_loop(0, COMBINE_TILE, body, 0, unroll=4)

    @pl.when(i == 0)
    def _():
        start_gather(d_cur_ref, 0)

    @pl.when(i + 1 < n)
    def _():
        start_gather(d_nxt_ref, 1 - slot)

    def wait_rows(t, carry):
        for k in range(TOP_K):
            _row_copy(y_hbm, 0, ybuf.at[slot, k], t, sem.at[slot]).wait()
        return carry
    lax.fori_loop(0, COMBINE_TILE, wait_rows, 0)

    gate = gate_ref[...]
    ffn = gate[:, 0:1] * ybuf[slot, 0] + gate[:, 1:2] * ybuf[slot, 1]
    o_ref[...] = _layer_norm(DEEPNORM_ALPHA * h_ref[...] + ffn, g_ref[...], b_ref[...])


def _combine(h1, ybuf, dest, gates, ln2_g, ln2_b):
    M, D = h1.shape
    tm = COMBINE_TILE
    n = M // tm
    smem_tile = lambda f: pl.BlockSpec((TOP_K * tm,), f, memory_space=pltpu.SMEM)
    return pl.pallas_call(
        _combine_kernel,
        out_shape=jax.ShapeDtypeStruct((M, D), F32),
        grid=(n,),
        in_specs=[smem_tile(lambda i: (i,)), smem_tile(lambda i: (jnp.minimum(i + 1, n - 1),)),
                  pl.BlockSpec((tm, D), lambda i: (i, 0)), pl.BlockSpec((tm, TOP_K), lambda i: (i, 0)),
                  _full(ln2_g), _full(ln2_b), pl.BlockSpec(memory_space=pl.ANY)],
        out_specs=pl.BlockSpec((tm, D), lambda i: (i, 0)),
        scratch_shapes=[pltpu.VMEM((2, TOP_K, tm, D), F32), pltpu.SemaphoreType.DMA((2,))],
        compiler_params=_cparams(("arbitrary",)),
        name="combine",
    )(dest, dest, h1, gates, ln2_g, ln2_b, ybuf)


def _routing_plan(route):
    M = route.shape[0]
    eid = route[:, :TOP_K].astype(jnp.int32).reshape(-1)
    gates = route[:, TOP_K:2 * TOP_K]
    A = M * TOP_K
    onehot = (eid[:, None] == jnp.arange(N_EXPERTS, dtype=eid.dtype)[None, :]).astype(jnp.int32)
    csum = jnp.cumsum(onehot, axis=0)
    rank = jnp.sum(csum * onehot, axis=1) - 1
    counts = csum[-1]
    pcounts = (counts + MOE_TILE - 1) // MOE_TILE * MOE_TILE
    pends = jnp.cumsum(pcounts)
    pstarts = pends - pcounts
    dest = (jnp.sum(onehot * pstarts[None, :], axis=1) + rank).astype(jnp.int32)
    n_tiles = (A + N_EXPERTS * (MOE_TILE - 1) + MOE_TILE - 1) // MOE_TILE
    n_valid = (pends[-1] // MOE_TILE).astype(jnp.int32)
    tile_start = jnp.minimum(jnp.arange(n_tiles, dtype=jnp.int32) * MOE_TILE, pends[-1] - 1)
    tile_expert = jnp.sum((pends[None, :] <= tile_start[:, None]).astype(jnp.int32), axis=1)
    tile_expert = jnp.minimum(tile_expert, N_EXPERTS - 1).astype(jnp.int32)
    return gates, dest, n_tiles * MOE_TILE, tile_expert, n_valid.reshape(1)


def kernel(x, meta, ln0_g, ln0_b, w_in, mu_shift, w0, w_decay_up, a0, w_a_up, w_g_up, k_k, k_a, r_k, gn_g, gn_b, conv_w, conv_b, w_rg, b_rg, w_ig, b_ig, lru_lambda, w_out, ln1_g, ln1_b, w_router_grp, b_router_grp, w_router_exp, b_router_exp, w_exp_gate, w_exp_up, w_exp_down, ln2_g, ln2_b):
    B, T, D = x.shape
    assert D == D_MODEL and T % 512 == 0 and w_in.shape[0] == 1
    assert (B * T * TOP_K) % INVERT_BLOCK == 0
    row = lambda p: p.reshape(1, -1).astype(F32)
    n_rw = 3 * RWKV_W
    w_in0 = w_in[0]

    def slots(p):
        pad = lambda a, n: jnp.pad(a, [(0, 0)] * (a.ndim - 1) + [(0, n - a.shape[-1])])
        zw = p[..., n_rw:n_rw + DECAY_RANK]
        za = p[..., n_rw + DECAY_RANK:n_rw + DECAY_RANK + AAA_RANK]
        zg = p[..., n_rw + DECAY_RANK + AAA_RANK:n_rw + DECAY_RANK + AAA_RANK + GATE_RANK]
        return jnp.concatenate([p[..., :n_rw], pad(zw, LANE), pad(za, LANE), pad(zg, ZG_SLOT)], axis=-1)

    rwkv_cols = n_rw + DECAY_RANK + AAA_RANK + GATE_RANK
    w_r = slots(w_in0[:, :rwkv_cols]).astype(BF16)
    w_l = w_in0[:, rwkv_cols:].astype(BF16)
    ur, ul, ur_t, ul_t = _in_projection(x, meta, row(ln0_g), row(ln0_b), w_r, w_l)

    pad_rows = lambda a, n: jnp.pad(a, ((0, n - a.shape[0]), (0, 0)))
    rwkv_params = (slots(mu_shift[0][None, :]).astype(F32), row(w0[0]), pad_rows(w_decay_up[0], LANE).astype(BF16),
                   row(a0[0]), pad_rows(w_a_up[0], LANE).astype(BF16), pad_rows(w_g_up[0], ZG_SLOT).astype(BF16),
                   row(k_k[0]), row(k_a[0]), row(r_k[0]), row(gn_g[0]), row(gn_b[0]))
    y_rwkv = _rwkv_pipe_mixer(ur, ur_t, rwkv_params)

    blockdiag = lambda w: jax.scipy.linalg.block_diag(*[w[i] for i in range(LRU_BLOCKS)]).astype(BF16)
    lru_params = (conv_w[0], row(conv_b[0]), blockdiag(w_rg[0]), row(b_rg[0]), blockdiag(w_ig[0]), row(b_ig[0]),
                  row(lru_lambda[0]))
    y_lru = _lru_mixer(ul, ul_t, lru_params)

    w_rt = jnp.concatenate([w_router_grp[0], w_router_exp[0]], axis=1)
    w_rt = jnp.pad(w_rt, ((0, 0), (0, LANE - w_rt.shape[1])))
    wrt_hi = w_rt.astype(BF16)
    wrt_lo = (w_rt - wrt_hi.astype(F32)).astype(BF16)
    b_rt = jnp.concatenate([b_router_grp[0], b_router_exp[0]])
    b_rt = jnp.pad(b_rt, (0, LANE - b_rt.shape[0])).reshape(1, LANE)
    wo = w_out[0].astype(BF16)
    h1, route = _out_projection(x, y_rwkv, y_lru, row(ln0_g), row(ln0_b), wo[:RWKV_W], wo[RWKV_W:],
                                row(ln1_g[0]), row(ln1_b[0]), wrt_hi, wrt_lo, b_rt)

    M = B * T
    h1 = h1.reshape(M, D)
    gates, dest, n_slots, tile_expert, n_valid = _routing_plan(route.reshape(M, LANE))
    row_asg = _invert_slots(dest, n_slots)
    ybuf = _moe_experts(h1, row_asg, tile_expert, n_valid, w_exp_gate[0], w_exp_up[0], w_exp_down[0])
    out = _combine(h1, ybuf, dest, gates, row(ln2_g[0]), row(ln2_b[0]))
    return out.reshape(B, T, D)
```

```python
import math

import jax
import jax.numpy as jnp
from jax import lax
from jax.experimental import pallas as pl
from jax.experimental.pallas import tpu as pltpu

F32 = jnp.float32
BF16 = jnp.bfloat16

D_MODEL = 1024
N_META = 16
RWKV_W = 512
RWKV_HEAD = 64
DECAY_RANK = 64
AAA_RANK = 64
GATE_RANK = 160
LRU_W = 512
LRU_BLOCKS = 8
CONV_WIDTH = 4
LRU_C = 8.0
N_GROUPS = 4
EXPERTS_PER_GROUP = 8
N_EXPERTS = N_GROUPS * EXPERTS_PER_GROUP
TOP_K = 2
D_EXPERT = 512
LN_EPS = 1e-5
GN_EPS = 64e-5
DEEPNORM_ALPHA = 2.0 ** 0.25

LANE = 128
OFF_R, OFF_K, OFF_V = 0, RWKV_W, 2 * RWKV_W
OFF_ZW = 3 * RWKV_W
OFF_ZA = OFF_ZW + LANE
OFF_ZG = OFF_ZA + LANE
ZG_SLOT = 2 * LANE
UR_W = OFF_ZG + ZG_SLOT
UL_W = 2 * LRU_W

TAIL = 128
CHUNK = 64
HEADS_PER_GROUP = 4
GW = HEADS_PER_GROUP * RWKV_HEAD
N_HGROUPS = RWKV_W // GW
LRU_TILE = 128
MOE_TILE = 256
COMBINE_TILE = 256
INVERT_BLOCK = 4096
V7X_VMEM_BYTES = 64 * 1024 * 1024
VMEM_LIMIT = V7X_VMEM_BYTES - 8 * 1024 * 1024


def _cparams(sem, flags=None):
    return pltpu.CompilerParams(dimension_semantics=sem, vmem_limit_bytes=VMEM_LIMIT, flags=flags)


def _layer_norm(x, g, b):
    mu = jnp.mean(x, -1, keepdims=True)
    xc = x - mu
    var = jnp.mean(xc * xc, -1, keepdims=True)
    return xc * lax.rsqrt(var + LN_EPS) * g + b


def _dot(a, b):
    return jnp.dot(a, b, preferred_element_type=F32)


def _dot_hi(a, b):
    return jnp.dot(a, b, precision=lax.Precision.HIGHEST, preferred_element_type=F32)


def _dot_nt(a, b):
    return lax.dot_general(a, b, (((1,), (1,)), ((), ())), preferred_element_type=F32)


def _dot_tn(a, b):
    return lax.dot_general(a, b, (((0,), (0,)), ((), ())), preferred_element_type=F32)


def _split_dot(x, w_bf16):
    hi = x.astype(BF16)
    lo = (x - hi.astype(F32)).astype(BF16)
    return _dot(hi, w_bf16) + _dot(lo, w_bf16)


def _full(a):
    return pl.BlockSpec(a.shape, lambda *_: (0,) * a.ndim)


def _inproj_kernel(x_ref, g_ref, b_ref, wr_ref, wl_ref, ur_ref, ul_ref):
    h = _layer_norm(x_ref[0], g_ref[...], b_ref[...]).astype(BF16)
    ur_ref[0] = _dot(h, wr_ref[...])
    ul_ref[0] = _dot(h, wl_ref[...])


def _inproj_tail_kernel(x_ref, g_ref, b_ref, wr_ref, wl_ref, ur_ref, ul_ref):
    h = _layer_norm(x_ref[...], g_ref[...], b_ref[...]).astype(BF16)
    rows = lax.broadcasted_iota(jnp.int32, (TAIL, 1), 0)
    valid = (rows >= TAIL - N_META).astype(F32)
    ur_ref[...] = _dot(h, wr_ref[...]) * valid
    ul_ref[...] = _dot(h, wl_ref[...]) * valid


def _in_projection(x, meta, ln0_g, ln0_b, w_r, w_l):
    B, T, D = x.shape
    tm = 512
    ur, ul = pl.pallas_call(
        _inproj_kernel,
        out_shape=(jax.ShapeDtypeStruct((B, T, UR_W), F32), jax.ShapeDtypeStruct((B, T, UL_W), F32)),
        grid=(B, T // tm),
        in_specs=[pl.BlockSpec((1, tm, D), lambda b, i: (b, i, 0)), _full(ln0_g), _full(ln0_b), _full(w_r), _full(w_l)],
        out_specs=(pl.BlockSpec((1, tm, UR_W), lambda b, i: (b, i, 0)),
                   pl.BlockSpec((1, tm, UL_W), lambda b, i: (b, i, 0))),
        compiler_params=_cparams(("parallel", "parallel")),
        name="inproj",
    )(x, ln0_g, ln0_b, w_r, w_l)
    tail_x = jnp.concatenate([jnp.zeros((TAIL - N_META, D), F32), meta.astype(F32)], axis=0)
    ur_t, ul_t = pl.pallas_call(
        _inproj_tail_kernel,
        out_shape=(jax.ShapeDtypeStruct((TAIL, UR_W), F32), jax.ShapeDtypeStruct((TAIL, UL_W), F32)),
        grid=(1,),
        in_specs=[_full(tail_x), _full(ln0_g), _full(ln0_b), _full(w_r), _full(w_l)],
        out_specs=(pl.BlockSpec((TAIL, UR_W), lambda i: (0, 0)), pl.BlockSpec((TAIL, UL_W), lambda i: (0, 0))),
        compiler_params=_cparams(("arbitrary",)),
        name="inproj_tail",
    )(tail_x, ln0_g, ln0_b, w_r, w_l)
    return ur, ul, ur_t, ul_t


def _rwkv_kernel(u_ref, ut_ref, mu_ref, w0_ref, wdu_ref, a0_ref, wau_ref, wgu_ref, kk_ref, ka_ref, rk_ref,
                 gng_ref, gnb_ref, bones_ref, tril_ref, eye_ref, bm_ref, msl_ref, mil_ref,
                 m8_ref, m16_ref, m32_ref, m64_ref, y_ref, s_ref, prev_ref):
    c = pl.program_id(0)
    nb = u_ref.shape[0]
    rows_of = lambda b: slice(b * CHUNK, (b + 1) * CHUNK)
    per_batch = lambda f: jnp.concatenate([f(b) for b in range(nb)], axis=0)

    @pl.when(c == 0)
    def _():
        s_ref[...] = jnp.zeros_like(s_ref)
        prev_ref[...] = jnp.zeros_like(prev_ref)

    u_x = u_ref[...].reshape(nb * CHUNK, UR_W)
    u = jnp.where(c == 0, per_batch(lambda b: ut_ref[...]), u_x)
    row = lax.broadcasted_iota(jnp.int32, u.shape, 0)
    prev_rows = per_batch(lambda b: jnp.broadcast_to(prev_ref[b:b + 1, :], (CHUNK, UR_W)))
    u_prev = jnp.where(jnp.bitwise_and(row, CHUNK - 1) == 0, prev_rows, pltpu.roll(u, 1, 0))
    for b in range(nb):
        prev_ref[b:b + 1, :] = u[(b + 1) * CHUNK - 1:(b + 1) * CHUNK, :]
    x = u + (u_prev - u) * mu_ref[...]
    r = x[:, OFF_R:OFF_R + RWKV_W]
    k = x[:, OFF_K:OFF_K + RWKV_W]
    v = x[:, OFF_V:OFF_V + RWKV_W]
    zw = x[:, OFF_ZW:OFF_ZW + LANE]
    za = x[:, OFF_ZA:OFF_ZA + LANE]
    zg = x[:, OFF_ZG:OFF_ZG + ZG_SLOT]

    bones = bones_ref[...]
    head_sum = lambda t: _split_dot(t, bones)

    z = w0_ref[...] + _dot_hi(jnp.tanh(zw), wdu_ref[...])
    logw = -math.exp(-0.5) * jax.nn.sigmoid(z)
    a = jax.nn.sigmoid(a0_ref[...] + _dot_hi(za, wau_ref[...]))
    g = _dot(jax.nn.sigmoid(zg).astype(BF16), wgu_ref[...])
    kk = k * kk_ref[...]
    kk = kk / jnp.maximum(jnp.sqrt(head_sum(kk * kk)), 1e-12)
    k = k * (1.0 + (a - 1.0) * ka_ref[...])
    kka = kk * a

    cl = _dot_hi(tril_ref[...], logw)
    cl_last = per_batch(lambda b: jnp.broadcast_to(cl[(b + 1) * CHUNK - 1:(b + 1) * CHUNK, :], (CHUNK, RWKV_W)))
    e_neg = jnp.exp(-cl)
    e_end = jnp.exp(cl_last - cl)
    rt = r * jnp.exp(cl)
    kt = k * e_neg
    at = -kk * jnp.exp(cl - logw)
    bt = kka * e_neg
    kw = k * e_end
    bw = kka * e_end
    w_end = jnp.exp(cl_last)

    bm = bm_ref[...]
    eye = eye_ref[...]
    msl = msl_ref[...]
    mil = mil_ref[...]
    lane = lax.broadcasted_iota(jnp.int32, (2 * GW, LANE), 1)
    tile4 = lambda t: jnp.concatenate([t] * HEADS_PER_GROUP, axis=0)
    fold4 = lambda t: sum(t[i * CHUNK:(i + 1) * CHUNK] for i in range(HEADS_PER_GROUP))
    wide = lambda t: jnp.concatenate([t, t], axis=1)
    y_rows = []
    for b in range(nb):
        rb = rows_of(b)
        ys = []
        for hg in range(N_HGROUPS):
            sl = slice(hg * GW, (hg + 1) * GW)
            v4 = v[rb, sl]
            lhs = jnp.concatenate([tile4(at[rb, sl]) * bm, tile4(rt[rb, sl]) * bm], axis=0).astype(BF16)
            rhs = jnp.concatenate([bt[rb, sl], kt[rb, sl]], axis=0).astype(BF16)
            aa = _dot_nt(lhs, rhs)
            aa_sw = pltpu.roll(aa, CHUNK, 1)
            vs_b = wide(jnp.where(lane < CHUNK, aa, aa_sw))
            vs_k = wide(jnp.where(lane < CHUNK, aa_sw, aa))
            a_ab = vs_b[:GW] * msl
            a_ak = (vs_k[:GW] * msl).astype(BF16)
            a_rb = (vs_b[GW:] * mil).astype(BF16)
            a_rk = (vs_k[GW:] * mil).astype(BF16)

            a0 = (a_ab * m8_ref[...]).astype(BF16)
            a2 = _dot(a0, a0).astype(BF16)
            a4 = _dot(a2, a2).astype(BF16)
            p1 = eye + a0.astype(F32)
            p1 = p1 + _dot(p1.astype(BF16), a2)
            t = p1 + _dot(p1.astype(BF16), a4)
            for m_ref in (m16_ref, m32_ref, m64_ref):
                tb = t.astype(BF16)
                off = (a_ab * m_ref[...]).astype(BF16)
                t = t + _dot(_dot(tb, off).astype(BF16), tb)
            tb = t.astype(BF16)

            s = s_ref[b, hg]
            sb = s.astype(BF16)
            vt = tile4(v4).astype(BF16)
            xx = _dot_nt(lhs[:GW], sb) + _dot(a_ak, vt)
            uu = _dot(tb, xx.astype(BF16)) * bm
            yy = (_dot_nt(lhs[GW:], sb) + _dot(a_rb, uu.astype(BF16)) + _dot(a_rk, vt)) * bm
            ys.append(fold4(yy))
            u4 = fold4(uu)
            upd = _dot_tn(jnp.concatenate([u4, v4], axis=0).astype(BF16),
                          jnp.concatenate([bw[rb, sl], kw[rb, sl]], axis=0).astype(BF16))
            s_ref[b, hg] = s * w_end[b * CHUNK:b * CHUNK + 1, sl] + upd * bm
        y_rows.append(jnp.concatenate(ys, axis=1))

    y = jnp.concatenate(y_rows, axis=0)
    inv_n = 1.0 / RWKV_HEAD
    ym = head_sum(y) * inv_n
    yc = y - ym
    yv = head_sum(yc * yc) * inv_n
    yn = yc * lax.rsqrt(yv + GN_EPS) * gng_ref[...] + gnb_ref[...]
    bonus = head_sum(r * k * rk_ref[...]) * v
    y_ref[...] = ((yn + bonus) * g).astype(y_ref.dtype).reshape(y_ref.shape)


def _rwkv_masks(nb):
    i = jnp.arange(GW)[:, None]
    j = jnp.arange(GW)[None, :]
    same = lambda n: (i // n) == (j // n)
    f = lambda m: m.astype(F32)
    bm = f(same(RWKV_HEAD))
    msl = f(same(RWKV_HEAD) & (i > j))
    mil = f(same(RWKV_HEAD) & (i >= j))
    m8 = f(same(8))
    m16 = f(same(16) & ~same(8))
    m32 = f(same(32) & ~same(16))
    m64 = f(same(64) & ~same(32))
    eye = f(i == j)
    ti = jnp.arange(nb * CHUNK)
    tril = f((ti[:, None] >= ti[None, :]) & ((ti[:, None] // CHUNK) == (ti[None, :] // CHUNK)))
    hi = jnp.arange(RWKV_W)
    bones = ((hi[:, None] // RWKV_HEAD) == (hi[None, :] // RWKV_HEAD)).astype(BF16)
    return bones, tril, eye, bm, msl, mil, m8, m16, m32, m64


def _rwkv_mixer(ur, ur_tail, params):
    B, T, _ = ur.shape
    consts = _rwkv_masks(B)
    x_map = lambda c: (0, jnp.maximum(c - 1, 0), 0)
    return pl.pallas_call(
        _rwkv_kernel,
        out_shape=jax.ShapeDtypeStruct((B, T, RWKV_W), BF16),
        grid=(T // CHUNK + 1,),
        in_specs=[pl.BlockSpec((B, CHUNK, UR_W), x_map),
                  pl.BlockSpec((CHUNK, UR_W), lambda c: (TAIL // CHUNK - 1, 0))]
                 + [_full(p) for p in params] + [_full(m) for m in consts],
        out_specs=pl.BlockSpec((B, CHUNK, RWKV_W), x_map),
        scratch_shapes=[pltpu.VMEM((B, N_HGROUPS, GW, GW), F32), pltpu.VMEM((B, UR_W), F32)],
        compiler_params=_cparams(("arbitrary",)),
        name="rwkv7",
    )(ur, ur_tail, *params, *consts)


def _rwkv_pipe_mixer(ur, ur_tail, params):
    B, T, _ = ur.shape
    blk = TAIL
    assert T % blk == 0 and blk % CHUNK == 0
    n_blocks = T // blk
    rows = B * blk
    consts = _rwkv_masks(rows // CHUNK)
    in_map = lambda s: (0, jnp.clip(s - 1, 0, n_blocks - 1), 0)
    out_map = lambda s: (0, jnp.clip(s - 2, 0, n_blocks - 1), 0)
    slot2 = lambda w, dt: pltpu.VMEM((2, rows, w), dt)
    return pl.pallas_call(
        _rwkv_pipe_kernel,
        out_shape=jax.ShapeDtypeStruct((B, T, RWKV_W), BF16),
        grid=(n_blocks + 2,),
        in_specs=[pl.BlockSpec((B, blk, UR_W), in_map), _full(ur_tail)]
                 + [_full(p) for p in params] + [_full(m) for m in consts],
        out_specs=pl.BlockSpec((B, blk, RWKV_W), out_map),
        scratch_shapes=[pltpu.VMEM((B, N_HGROUPS, GW, GW), F32), pltpu.VMEM((B, UR_W), F32),
                        slot2(RWKV_W, F32), slot2(RWKV_W, BF16), slot2(RWKV_W, BF16), slot2(RWKV_W, F32),
                        pltpu.VMEM((2, 8 * rows // CHUNK, RWKV_W), F32), slot2(RWKV_W, F32), slot2(RWKV_W, F32),
                        pltpu.VMEM((rows, RWKV_W), F32)],
        compiler_params=_cparams(("arbitrary",)),
        name="rwkv7",
    )(ur, ur_tail, *params, *consts)


def _rwkv_pipe_kernel(u_ref, ut_ref, mu_ref, w0_ref, wdu_ref, a0_ref, wau_ref, wgu_ref, kk_ref, ka_ref, rk_ref,
                      gng_ref, gnb_ref, bones_ref, tril_ref, eye_ref, bm_ref, msl_ref, mil_ref,
                      m8_ref, m16_ref, m32_ref, m64_ref, y_ref,
                      s_ref, prev_ref, y0_s, q_s, mc_s, nc_s, we_s, bonus_s, g_s, yraw_s):
    s_id = pl.program_id(0)
    nb = u_ref.shape[0]
    blk = u_ref.shape[1]
    npc = blk // CHUNK
    nseq = nb * npc
    seq_rows = lambda q: slice(q * CHUNK, (q + 1) * CHUNK)
    per_seq = lambda f: jnp.concatenate([f(q) for q in range(nseq)], axis=0)
    w_slot = lax.rem(s_id, 2)
    r_slot = 1 - w_slot

    @pl.when(s_id == 0)
    def _():
        s_ref[...] = jnp.zeros_like(s_ref)
        prev_ref[...] = jnp.zeros_like(prev_ref)
        for ref in (y0_s, q_s, mc_s, nc_s, we_s, bonus_s, g_s):
            ref[1] = jnp.zeros(ref.shape[1:], ref.dtype)

    u_x = u_ref[...].reshape(nb * blk, UR_W)
    u = jnp.where(s_id == 0, jnp.concatenate([ut_ref[...]] * nb, axis=0), u_x)
    row = lax.broadcasted_iota(jnp.int32, u.shape, 0)
    prev_rows = jnp.concatenate([jnp.broadcast_to(prev_ref[b:b + 1, :], (blk, UR_W)) for b in range(nb)], axis=0)
    u_prev = jnp.where(jnp.bitwise_and(row, blk - 1) == 0, prev_rows, pltpu.roll(u, 1, 0))
    for b in range(nb):
        prev_ref[b:b + 1, :] = u[(b + 1) * blk - 1:(b + 1) * blk, :]
    x = u + (u_prev - u) * mu_ref[...]
    r = x[:, OFF_R:OFF_R + RWKV_W]
    k = x[:, OFF_K:OFF_K + RWKV_W]
    v = x[:, OFF_V:OFF_V + RWKV_W]
    zw = x[:, OFF_ZW:OFF_ZW + LANE]
    za = x[:, OFF_ZA:OFF_ZA + LANE]
    zg = x[:, OFF_ZG:OFF_ZG + ZG_SLOT]

    bones = bones_ref[...]
    head_sum = lambda t: _dot(t.astype(BF16), bones)

    z = w0_ref[...] + _dot(jnp.tanh(zw).astype(BF16), wdu_ref[...])
    logw = -math.exp(-0.5) * jax.nn.sigmoid(z)
    a = jax.nn.sigmoid(a0_ref[...] + _dot(za.astype(BF16), wau_ref[...]))
    g = _dot(jax.nn.sigmoid(zg).astype(BF16), wgu_ref[...])
    kk = k * kk_ref[...]
    kk = kk / jnp.maximum(jnp.sqrt(head_sum(kk * kk)), 1e-12)
    k = k * (1.0 + (a - 1.0) * ka_ref[...])
    kka = kk * a
    bonus_s[w_slot] = head_sum(r * k * rk_ref[...]) * v
    g_s[w_slot] = g

    cl = logw
    row_in_chunk = jnp.bitwise_and(lax.broadcasted_iota(jnp.int32, cl.shape, 0), CHUNK - 1)
    d = 1
    while d < CHUNK:
        cl = cl + jnp.where(row_in_chunk >= d, pltpu.roll(cl, d, 0), 0.0)
        d *= 2
    cl_last = per_seq(lambda q: jnp.broadcast_to(cl[(q + 1) * CHUNK - 1:(q + 1) * CHUNK, :], (CHUNK, RWKV_W)))
    e_neg = jnp.exp(-cl)
    e_end = jnp.exp(cl_last - cl)
    rt = r * jnp.exp(cl)
    kt = k * e_neg
    at = -kk * jnp.exp(cl - logw)
    bt = kka * e_neg
    kw = k * e_end
    bw = kka * e_end
    w_end = jnp.exp(cl_last)

    bm = bm_ref[...]
    bm16 = bm.astype(BF16)
    eye = eye_ref[...]
    msl = msl_ref[...]
    mil = mil_ref[...]
    lane = lax.broadcasted_iota(jnp.int32, (2 * GW, LANE), 1)
    tile4 = lambda t: jnp.concatenate([t] * HEADS_PER_GROUP, axis=0)
    fold4 = lambda t: sum(t[i * CHUNK:(i + 1) * CHUNK] for i in range(HEADS_PER_GROUP))
    wide = lambda t: jnp.concatenate([t, t], axis=1)
    for q in range(nseq):
        we_s[w_slot, q * 8:(q + 1) * 8, :] = w_end[q * CHUNK:q * CHUNK + 8, :]

    probs = [(seq_rows(q), slice(hg * GW, (hg + 1) * GW)) for q in range(nseq) for hg in range(N_HGROUPS)]
    each = lambda f, *ls: [f(*xs) for xs in zip(*ls)]
    b16 = lambda t: t.astype(BF16)
    v4 = [v[rq, sl] for rq, sl in probs]
    r_st = [tile4(rt[rq, sl]) * bm for rq, sl in probs]
    a_st = [b16(tile4(at[rq, sl]) * bm) for rq, sl in probs]
    lhs = each(lambda x, y: jnp.concatenate([x, b16(y)], axis=0), a_st, r_st)
    rhs = [b16(jnp.concatenate([bt[rq, sl], kt[rq, sl]], axis=0)) for rq, sl in probs]
    aa = each(_dot_nt, lhs, rhs)
    aa_sw = each(lambda t: pltpu.roll(t, CHUNK, 1), aa)
    vs_b = each(lambda x, y: wide(jnp.where(lane < CHUNK, x, y)), aa, aa_sw)
    vs_k = each(lambda x, y: wide(jnp.where(lane < CHUNK, y, x)), aa, aa_sw)
    a_ab = each(lambda t: t[:GW] * msl, vs_b)
    a_ak = each(lambda t: b16(t[:GW] * msl), vs_k)
    a_rb = each(lambda t: b16(t[GW:] * mil), vs_b)
    a_rk = each(lambda t: b16(t[GW:] * mil), vs_k)

    a0 = each(lambda t: b16(t * m8_ref[...]), a_ab)
    a2 = each(lambda t: b16(_dot(t, t)), a0)
    a4 = each(lambda t: b16(_dot(t, t)), a2)
    p1 = each(lambda t: eye + t.astype(F32), a0)
    p1 = each(lambda p, t: p + _dot(b16(p), t), p1, a2)
    tt = each(lambda p, t: p + _dot(b16(p), t), p1, a4)
    for m_ref in (m16_ref, m32_ref, m64_ref):
        tb = each(b16, tt)
        off = each(lambda t: b16(t * m_ref[...]), a_ab)
        half = each(lambda x, y: b16(_dot(x, y)), tb, off)
        tt = each(lambda t, x, y: t + _dot(x, y), tt, half, tb)
    tb = each(b16, tt)

    vt = each(lambda t: b16(tile4(t)), v4)
    x0 = each(lambda x, y: b16(_dot(x, y)), a_ak, vt)
    u0 = each(lambda x, y: _dot(x, y) * bm, tb, x0)
    ta = each(_dot, tb, a_st)
    y0 = each(lambda x, y, z, w: (_dot(x, b16(y)) + _dot(z, w)) * bm, a_rb, u0, a_rk, vt)
    qq = each(lambda x, y, z: x + _dot(y, b16(z)), r_st, a_rb, ta)
    left = each(lambda x, y, z: b16(jnp.concatenate([jnp.concatenate([fold4(x), fold4(y)], axis=1),
                                                     jnp.concatenate([jnp.zeros_like(z), z], axis=1)], axis=0)),
                ta, u0, v4)
    right = [b16(jnp.concatenate([bw[rq, sl], kw[rq, sl]], axis=0)) for rq, sl in probs]
    mn = each(_dot_tn, left, right)
    for i, (rq, sl) in enumerate(probs):
        y0_s[w_slot, rq, sl] = fold4(y0[i])
        q_s[w_slot, rq, sl] = b16(fold4(qq[i]))
        mc_s[w_slot, rq, sl] = b16(fold4(mn[i][:GW] * bm))
        nc_s[w_slot, rq, sl] = fold4(mn[i][GW:] * bm)

    for b in range(nb):
        for hg in range(N_HGROUPS):
            sl = slice(hg * GW, (hg + 1) * GW)
            s = s_ref[b, hg]
            for j in range(npc):
                q = b * npc + j
                rq = seq_rows(q)
                sb = s.astype(BF16)
                q_st = tile4(q_s[r_slot, rq, sl]) * bm16
                mc_bd = tile4(mc_s[r_slot, rq, sl]) * bm16
                nc_bd = tile4(nc_s[r_slot, rq, sl]) * bm
                yraw_s[rq, sl] = y0_s[r_slot, rq, sl] + fold4(_dot_nt(q_st, sb))
                s = s * we_s[r_slot, q * 8:q * 8 + 1, sl] + _dot(sb, mc_bd) + nc_bd
            s_ref[b, hg] = s

    y = yraw_s[...]
    inv_n = 1.0 / RWKV_HEAD
    ym = head_sum(y) * inv_n
    yc = y - ym
    yv = head_sum(yc * yc) * inv_n
    yn = yc * lax.rsqrt(yv + GN_EPS) * gng_ref[...] + gnb_ref[...]
    y_ref[...] = ((yn + bonus_s[r_slot]) * g_s[r_slot]).astype(y_ref.dtype).reshape(y_ref.shape)


def _gelu_tanh(x):
    return 0.5 * x * (1.0 + jnp.tanh(math.sqrt(2.0 / math.pi) * (x + 0.044715 * (x * x * x))))


def _lru_kernel(u_ref, ut_ref, cw_ref, cb_ref, wrg_ref, brg_ref, wig_ref, big_ref, lam_ref, y_ref,
                xprev_ref, hprev_ref):
    c = pl.program_id(1)

    @pl.when(c == 0)
    def _():
        xprev_ref[...] = jnp.zeros_like(xprev_ref)
        hprev_ref[...] = jnp.zeros_like(hprev_ref)

    u = jnp.where(c == 0, ut_ref[...], u_ref[0])
    xl = u[:, :LRU_W]
    gl = u[:, LRU_W:]
    row = lax.broadcasted_iota(jnp.int32, (LRU_TILE, LRU_W), 0)
    row8 = lax.broadcasted_iota(jnp.int32, (8, LRU_W), 0)
    xprev = xprev_ref[...]
    xc = cb_ref[...] + cw_ref[CONV_WIDTH - 1:CONV_WIDTH, :] * xl
    for d in range(1, CONV_WIDTH):
        rolled = pltpu.roll(xl, d, 0)
        head = jnp.where(row8 < d, pltpu.roll(xprev, d, 0), rolled[:8])
        shifted = jnp.concatenate([head, rolled[8:]], axis=0)
        xc = xc + cw_ref[CONV_WIDTH - 1 - d:CONV_WIDTH - d, :] * shifted
    xprev_ref[...] = xl[LRU_TILE - 8:, :]

    xcb = xc.astype(BF16)
    gate_r = jax.nn.sigmoid(_dot(xcb, wrg_ref[...]) + brg_ref[...])
    gate_i = jax.nn.sigmoid(_dot(xcb, wig_ref[...]) + big_ref[...])
    lam = lam_ref[...]
    log_sig = -(jnp.maximum(-lam, 0.0) + jnp.log1p(jnp.exp(-jnp.abs(lam))))
    log_a = LRU_C * gate_r * log_sig
    a = jnp.exp(log_a)
    mult = jnp.sqrt(jnp.maximum(1.0 - jnp.exp(2.0 * log_a), 0.0))
    b = mult * gate_i * xc
    b = jnp.where((c == 0) & (row < LRU_TILE - N_META), 0.0, b)

    d = 1
    while d < LRU_TILE:
        keep = row >= d
        a_sh = jnp.where(keep, pltpu.roll(a, d, 0), 1.0)
        b_sh = jnp.where(keep, pltpu.roll(b, d, 0), 0.0)
        b = a * b_sh + b
        a = a * a_sh
        d *= 2
    h = b + a * hprev_ref[...]
    hprev_ref[...] = h[LRU_TILE - 1:, :]
    y_ref[0] = (h * _gelu_tanh(gl)).astype(y_ref.dtype)


def _lru_mixer(ul, ul_tail, params):
    B, T, _ = ul.shape
    assert TAIL == LRU_TILE
    x_map = lambda b, c: (b, jnp.maximum(c - 1, 0), 0)
    return pl.pallas_call(
        _lru_kernel,
        out_shape=jax.ShapeDtypeStruct((B, T, LRU_W), BF16),
        grid=(B, T // LRU_TILE + 1),
        in_specs=[pl.BlockSpec((1, LRU_TILE, UL_W), x_map), _full(ul_tail)] + [_full(p) for p in params],
        out_specs=pl.BlockSpec((1, LRU_TILE, LRU_W), x_map),
        scratch_shapes=[pltpu.VMEM((8, LRU_W), F32), pltpu.VMEM((1, LRU_W), F32)],
        compiler_params=_cparams(("parallel", "arbitrary")),
        name="rglru",
    )(ul, ul_tail, *params)


def _route(lg):
    lane = lax.broadcasted_iota(jnp.int32, lg.shape, 1)
    neg = jnp.float32(-jnp.inf)
    rmax = lambda t: jnp.max(t, axis=1, keepdims=True)
    first = lambda hit: jnp.min(jnp.where(hit, lane, LANE), axis=1, keepdims=True)
    is_grp = lane < N_GROUPS
    gl = jnp.where(is_grp, lg, neg)
    gmax = rmax(gl)
    g_sel = first(gl == gmax)
    p_g = 1.0 / jnp.sum(jnp.where(is_grp, jnp.exp(lg - gmax), 0.0), axis=1, keepdims=True)
    ex = lane - N_GROUPS
    in_grp = (ex >= 0) & (ex < N_EXPERTS) & (jnp.right_shift(ex, 3) == g_sel)
    el = jnp.where(in_grp, lg, neg)
    v1 = rmax(el)
    i1 = first(el == v1)
    el2 = jnp.where(lane == i1, neg, el)
    v2 = rmax(el2)
    i2 = first(el2 == v2)
    t = jnp.exp(v2 - v1)
    gate1 = p_g / (1.0 + t)
    gate2 = p_g * t / (1.0 + t)
    e1 = (i1 - N_GROUPS).astype(F32)
    e2 = (i2 - N_GROUPS).astype(F32)
    return jnp.where(lane == 0, e1, jnp.where(lane == 1, e2, jnp.where(lane == 2, gate1, jnp.where(lane == 3, gate2, 0.0))))


SLABS = D_MODEL // LANE


def _store_token_tiles(ref, val):
    n = val.shape[0]
    for s in range(SLABS):
        ref[pl.ds(s, n, stride=SLABS), :] = val[:, s * LANE:(s + 1) * LANE]


def _load_token_slabs(ref, n):
    return [ref[pl.ds(s, n, stride=SLABS), :] for s in range(SLABS)]


def _outproj_kernel(x_ref, yr_ref, yl_ref, g0_ref, b0_ref, wor_ref, wol_ref, g1_ref, b1_ref,
                    wrt_hi_ref, wrt_lo_ref, brt_ref, h1_ref, rt_ref):
    h0 = _layer_norm(x_ref[0], g0_ref[...], b0_ref[...])
    mix = _dot(yr_ref[0], wor_ref[...]) + _dot(yl_ref[0], wol_ref[...])
    h1 = _layer_norm(DEEPNORM_ALPHA * h0 + mix, g1_ref[...], b1_ref[...])
    _store_token_tiles(h1_ref.at[0], h1)
    hi = h1.astype(BF16)
    lo = (h1 - hi.astype(F32)).astype(BF16)
    w_hi = wrt_hi_ref[...]
    lg = _dot(hi, w_hi) + (_dot(hi, wrt_lo_ref[...]) + _dot(lo, w_hi)) + brt_ref[...]
    rt_ref[0] = _route(lg)


def _out_projection(x, y_rwkv, y_lru, ln0_g, ln0_b, wo_r, wo_l, ln1_g, ln1_b, wrt_hi, wrt_lo, brt):
    B, T, D = x.shape
    tm = 512
    rows = lambda w: pl.BlockSpec((1, tm, w), lambda b, i: (b, i, 0))
    return pl.pallas_call(
        _outproj_kernel,
        out_shape=(jax.ShapeDtypeStruct((B, T * SLABS, LANE), F32), jax.ShapeDtypeStruct((B, T, LANE), F32)),
        grid=(B, T // tm),
        in_specs=[rows(D), rows(RWKV_W), rows(LRU_W), _full(ln0_g), _full(ln0_b), _full(wo_r), _full(wo_l),
                  _full(ln1_g), _full(ln1_b), _full(wrt_hi), _full(wrt_lo), _full(brt)],
        out_specs=(pl.BlockSpec((1, tm * SLABS, LANE), lambda b, i: (b, i, 0)), rows(LANE)),
        compiler_params=_cparams(("parallel", "parallel")),
        name="outproj",
    )(x, y_rwkv, y_lru, ln0_g, ln0_b, wo_r, wo_l, ln1_g, ln1_b, wrt_hi, wrt_lo, brt)


def _invert_kernel(dest_ref, out_ref):
    i = pl.program_id(0)

    @pl.when(i == 0)
    def _():
        def zero(j, carry):
            out_ref[j] = 0
            return carry
        lax.fori_loop(0, out_ref.shape[0], zero, 0, unroll=8)

    base = i * INVERT_BLOCK

    def body(j, carry):
        out_ref[dest_ref[j]] = base + j
        return carry

    lax.fori_loop(0, INVERT_BLOCK, body, 0, unroll=8)


def _invert_slots(dest, n_slots):
    A = dest.shape[0]
    return pl.pallas_call(
        _invert_kernel,
        out_shape=jax.ShapeDtypeStruct((n_slots,), jnp.int32),
        grid=(A // INVERT_BLOCK,),
        in_specs=[pl.BlockSpec((INVERT_BLOCK,), lambda i: (i,), memory_space=pltpu.SMEM)],
        out_specs=pl.BlockSpec(memory_space=pltpu.SMEM),
        compiler_params=_cparams(("arbitrary",)),
        name="invert_slots",
    )(dest)


def _row_copy(src_hbm, src_row, dst_ref, dst_row, sem):
    return pltpu.make_async_copy(src_hbm.at[pl.ds(src_row * SLABS, SLABS), :],
                                 dst_ref.at[pl.ds(dst_row * SLABS, SLABS), :], sem)


DMA_UNROLL = 8


def _moe_kernel(te_ref, nv_ref, ra_cur_ref, ra_nxt_ref, h_hbm, wg_ref, wu_ref, wd_ref, o_ref,
                xbuf, sem, wgb_ref, wub_ref, wdb_ref):
    i = pl.program_id(0)
    n_valid = nv_ref[0]
    slot = lax.rem(i, 2)

    def start_gather(ra_ref, s):
        def body(jj, carry):
            for u in range(DMA_UNROLL):
                j = jj * DMA_UNROLL + u
                tok = lax.shift_right_logical(ra_ref[j], 1)
                _row_copy(h_hbm, tok, xbuf.at[s], j, sem.at[s]).start(priority=u % 2)
            return carry
        lax.fori_loop(0, MOE_TILE // DMA_UNROLL, body, 0)

    @pl.when(i == 0)
    def _():
        start_gather(ra_cur_ref, 0)

    @pl.when(i + 1 < n_valid)
    def _():
        start_gather(ra_nxt_ref, 1 - slot)

    e = te_ref[i]
    e_prev = te_ref[jnp.maximum(i - 1, 0)]

    @pl.when((i == 0) | (e != e_prev))
    def _():
        wgb_ref[...] = wg_ref[0].astype(BF16)
        wub_ref[...] = wu_ref[0].astype(BF16)
        wdb_ref[...] = wd_ref[0].astype(BF16)

    @pl.when(i < n_valid)
    def _():
        def wait_row(j, carry):
            _row_copy(h_hbm, 0, xbuf.at[slot], j, sem.at[slot]).wait()
            return carry
        lax.fori_loop(0, MOE_TILE, wait_row, 0)
        xb = jnp.concatenate(_load_token_slabs(xbuf.at[slot], MOE_TILE), axis=1).astype(BF16)
        hg = _dot(xb, wgb_ref[...])
        hu = _dot(xb, wub_ref[...])
        mid = (hg * jax.nn.sigmoid(hg) * hu).astype(BF16)
        _store_token_tiles(o_ref, _dot(mid, wdb_ref[...]))

    @pl.when(i >= n_valid)
    def _():
        o_ref[...] = jnp.zeros_like(o_ref)


def _moe_experts(h1, row_asg, tile_expert, n_valid, w_gate, w_up, w_down):
    D = D_MODEL
    n_tiles = row_asg.shape[0] // MOE_TILE
    smem_tile = lambda f: pl.BlockSpec((MOE_TILE,), f, memory_space=pltpu.SMEM)
    grid_spec = pltpu.PrefetchScalarGridSpec(
        num_scalar_prefetch=2,
        grid=(n_tiles,),
        in_specs=[smem_tile(lambda i, te, nv: (i,)),
                  smem_tile(lambda i, te, nv: (jnp.minimum(i + 1, n_tiles - 1),)),
                  pl.BlockSpec(memory_space=pl.ANY),
                  pl.BlockSpec((1, D, D_EXPERT), lambda i, te, nv: (te[i], 0, 0)),
                  pl.BlockSpec((1, D, D_EXPERT), lambda i, te, nv: (te[i], 0, 0)),
                  pl.BlockSpec((1, D_EXPERT, D), lambda i, te, nv: (te[i], 0, 0))],
        out_specs=pl.BlockSpec((MOE_TILE * SLABS, LANE), lambda i, te, nv: (i, 0)),
        scratch_shapes=[pltpu.VMEM((2, MOE_TILE * SLABS, LANE), F32), pltpu.SemaphoreType.DMA((2,)),
                        pltpu.VMEM((D, D_EXPERT), BF16), pltpu.VMEM((D, D_EXPERT), BF16),
                        pltpu.VMEM((D_EXPERT, D), BF16)],
    )
    return pl.pallas_call(
        _moe_kernel,
        out_shape=jax.ShapeDtypeStruct((n_tiles * MOE_TILE * SLABS, LANE), F32),
        grid_spec=grid_spec,
        compiler_params=_cparams(("arbitrary",)),
        name="moe_experts",
    )(tile_expert, n_valid, row_asg, row_asg, h1, w_gate, w_up, w_down)


def _combine_kernel(d_cur_ref, d_nxt_ref, h_ref, gate_ref, g_ref, b_ref, y_hbm, o_ref, ybuf, sem):
    i = pl.program_id(0)
    n = pl.num_programs(0)
    slot = lax.rem(i, 2)

    def start_gather(d_ref, s):
        def body(tt, carry):
            for u in range(DMA_UNROLL // TOP_K):
                t = tt * (DMA_UNROLL // TOP_K) + u
                for k in range(TOP_K):
                    _row_copy(y_hbm, d_ref[TOP_K * t + k], ybuf.at[s, k], t, sem.at[s]).start(priority=k % 2)
            return carry
        lax.fori_loop(0, COMBINE_TILE * TOP_K // DMA_UNROLL, body, 0)

    @pl.when(i == 0)
    def _():
        start_gather(d_cur_ref, 0)

    @pl.when(i + 1 < n)
    def _():
        start_gather(d_nxt_ref, 1 - slot)

    def wait_rows(t, carry):
        for k in range(TOP_K):
            _row_copy(y_hbm, 0, ybuf.at[slot, k], t, sem.at[slot]).wait()
        return carry
    lax.fori_loop(0, COMBINE_TILE, wait_rows, 0)

    gate = gate_ref[...]
    tm = COMBINE_TILE
    ga = jnp.broadcast_to(gate[:, 0:1], (tm, LANE))
    gb = jnp.broadcast_to(gate[:, 1:2], (tm, LANE))
    hs = _load_token_slabs(h_ref, tm)
    ya = _load_token_slabs(ybuf.at[slot, 0], tm)
    yb = _load_token_slabs(ybuf.at[slot, 1], tm)
    z = [DEEPNORM_ALPHA * h + (ga * a + gb * b) for h, a, b in zip(hs, ya, yb)]
    inv_d = 1.0 / D_MODEL
    mu = sum(jnp.sum(t, axis=1, keepdims=True) for t in z) * inv_d
    zc = [t - mu for t in z]
    var = sum(jnp.sum(t * t, axis=1, keepdims=True) for t in zc) * inv_d
    rstd = lax.rsqrt(var + LN_EPS)
    for s in range(SLABS):
        cols = slice(s * LANE, (s + 1) * LANE)
        o_ref[:, cols] = zc[s] * rstd * g_ref[:, cols] + b_ref[:, cols]


def _combine(h1, ybuf, dest, gates, ln2_g, ln2_b):
    D = D_MODEL
    M = h1.shape[0] // SLABS
    tm = COMBINE_TILE
    n = M // tm
    smem_tile = lambda f: pl.BlockSpec((TOP_K * tm,), f, memory_space=pltpu.SMEM)
    return pl.pallas_call(
        _combine_kernel,
        out_shape=jax.ShapeDtypeStruct((M, D), F32),
        grid=(n,),
        in_specs=[smem_tile(lambda i: (i,)), smem_tile(lambda i: (jnp.minimum(i + 1, n - 1),)),
                  pl.BlockSpec((tm * SLABS, LANE), lambda i: (i, 0)), pl.BlockSpec((tm, TOP_K), lambda i: (i, 0)),
                  _full(ln2_g), _full(ln2_b), pl.BlockSpec(memory_space=pl.ANY)],
        out_specs=pl.BlockSpec((tm, D), lambda i: (i, 0)),
        scratch_shapes=[pltpu.VMEM((2, TOP_K, tm * SLABS, LANE), F32), pltpu.SemaphoreType.DMA((2,))],
        compiler_params=_cparams(("arbitrary",)),
        name="combine",
    )(dest, dest, h1, gates, ln2_g, ln2_b, ybuf)


def _routing_plan(route):
    M = route.shape[0]
    eid = route[:, :TOP_K].astype(jnp.int32).reshape(-1)
    gates = route[:, TOP_K:2 * TOP_K]
    A = M * TOP_K
    onehot = (eid[:, None] == jnp.arange(N_EXPERTS, dtype=eid.dtype)[None, :]).astype(jnp.int32)
    csum = jnp.cumsum(onehot, axis=0)
    rank = jnp.sum(csum * onehot, axis=1) - 1
    counts = csum[-1]
    pcounts = (counts + MOE_TILE - 1) // MOE_TILE * MOE_TILE
    pends = jnp.cumsum(pcounts)
    pstarts = pends - pcounts
    dest = (jnp.sum(onehot * pstarts[None, :], axis=1) + rank).astype(jnp.int32)
    n_tiles = (A + N_EXPERTS * (MOE_TILE - 1) + MOE_TILE - 1) // MOE_TILE
    n_valid = (pends[-1] // MOE_TILE).astype(jnp.int32)
    tile_start = jnp.minimum(jnp.arange(n_tiles, dtype=jnp.int32) * MOE_TILE, pends[-1] - 1)
    tile_expert = jnp.sum((pends[None, :] <= tile_start[:, None]).astype(jnp.int32), axis=1)
    tile_expert = jnp.minimum(tile_expert, N_EXPERTS - 1).astype(jnp.int32)
    return gates, dest, n_tiles * MOE_TILE, tile_expert, n_valid.reshape(1)


def kernel(x, meta, ln0_g, ln0_b, w_in, mu_shift, w0, w_decay_up, a0, w_a_up, w_g_up, k_k, k_a, r_k, gn_g, gn_b, conv_w, conv_b, w_rg, b_rg, w_ig, b_ig, lru_lambda, w_out, ln1_g, ln1_b, w_router_grp, b_router_grp, w_router_exp, b_router_exp, w_exp_gate, w_exp_up, w_exp_down, ln2_g, ln2_b):
    B, T, D = x.shape
    assert D == D_MODEL and T % 512 == 0 and w_in.shape[0] == 1
    assert (B * T * TOP_K) % INVERT_BLOCK == 0
    row = lambda p: p.reshape(1, -1).astype(F32)
    n_rw = 3 * RWKV_W
    w_in0 = w_in[0]

    def slots(p):
        pad = lambda a, n: jnp.pad(a, [(0, 0)] * (a.ndim - 1) + [(0, n - a.shape[-1])])
        zw = p[..., n_rw:n_rw + DECAY_RANK]
        za = p[..., n_rw + DECAY_RANK:n_rw + DECAY_RANK + AAA_RANK]
        zg = p[..., n_rw + DECAY_RANK + AAA_RANK:n_rw + DECAY_RANK + AAA_RANK + GATE_RANK]
        return jnp.concatenate([p[..., :n_rw], pad(zw, LANE), pad(za, LANE), pad(zg, ZG_SLOT)], axis=-1)

    rwkv_cols = n_rw + DECAY_RANK + AAA_RANK + GATE_RANK
    w_r = slots(w_in0[:, :rwkv_cols]).astype(BF16)
    w_l = w_in0[:, rwkv_cols:].astype(BF16)
    ur, ul, ur_t, ul_t = _in_projection(x, meta, row(ln0_g), row(ln0_b), w_r, w_l)

    pad_rows = lambda a, n: jnp.pad(a, ((0, n - a.shape[0]), (0, 0)))
    rwkv_params = (slots(mu_shift[0][None, :]).astype(F32), row(w0[0]), pad_rows(w_decay_up[0], LANE).astype(BF16),
                   row(a0[0]), pad_rows(w_a_up[0], LANE).astype(BF16), pad_rows(w_g_up[0], ZG_SLOT).astype(BF16),
                   row(k_k[0]), row(k_a[0]), row(r_k[0]), row(gn_g[0]), row(gn_b[0]))
    y_rwkv = _rwkv_pipe_mixer(ur, ur_t, rwkv_params)

    blockdiag = lambda w: jax.scipy.linalg.block_diag(*[w[i] for i in range(LRU_BLOCKS)]).astype(BF16)
    lru_params = (conv_w[0], row(conv_b[0]), blockdiag(w_rg[0]), row(b_rg[0]), blockdiag(w_ig[0]), row(b_ig[0]),
                  row(lru_lambda[0]))
    y_lru = _lru_mixer(ul, ul_t, lru_params)

    w_rt = jnp.concatenate([w_router_grp[0], w_router_exp[0]], axis=1)
    w_rt = jnp.pad(w_rt, ((0, 0), (0, LANE - w_rt.shape[1])))
    wrt_hi = w_rt.astype(BF16)
    wrt_lo = (w_rt - wrt_hi.astype(F32)).astype(BF16)
    b_rt = jnp.concatenate([b_router_grp[0], b_router_exp[0]])
    b_rt = jnp.pad(b_rt, (0, LANE - b_rt.shape[0])).reshape(1, LANE)
    wo = w_out[0].astype(BF16)
    h1, route = _out_projection(x, y_rwkv, y_lru, row(ln0_g), row(ln0_b), wo[:RWKV_W], wo[RWKV_W:],
                                row(ln1_g[0]), row(ln1_b[0]), wrt_hi, wrt_lo, b_rt)

    M = B * T
    h1 = h1.reshape(M * SLABS, LANE)
    gates, dest, n_slots, tile_expert, n_valid = _routing_plan(route.reshape(M, LANE))
    row_asg = _invert_slots(dest, n_slots)
    ybuf = _moe_experts(h1, row_asg, tile_expert, n_valid, w_exp_gate[0], w_exp_up[0], w_exp_down[0])
    out = _combine(h1, ybuf, dest, gates, row(ln2_g[0]), row(ln2_b[0]))
    return out.reshape(B, T, D)
```

```python
import math

import jax
import jax.numpy as jnp
from jax import lax
from jax.experimental import pallas as pl
from jax.experimental.pallas import tpu as pltpu

F32 = jnp.float32
BF16 = jnp.bfloat16

D_MODEL = 1024
N_META = 16
RWKV_W = 512
RWKV_HEAD = 64
DECAY_RANK = 64
AAA_RANK = 64
GATE_RANK = 160
LRU_W = 512
LRU_BLOCKS = 8
CONV_WIDTH = 4
LRU_C = 8.0
N_GROUPS = 4
EXPERTS_PER_GROUP = 8
N_EXPERTS = N_GROUPS * EXPERTS_PER_GROUP
TOP_K = 2
D_EXPERT = 512
LN_EPS = 1e-5
GN_EPS = 64e-5
DEEPNORM_ALPHA = 2.0 ** 0.25

LANE = 128
OFF_R, OFF_K, OFF_V = 0, RWKV_W, 2 * RWKV_W
OFF_ZW = 3 * RWKV_W
OFF_ZA = OFF_ZW + LANE
OFF_ZG = OFF_ZA + LANE
ZG_SLOT = 2 * LANE
UR_W = OFF_ZG + ZG_SLOT
UL_W = 2 * LRU_W

TAIL = 128
CHUNK = 64
HEADS_PER_GROUP = 4
GW = HEADS_PER_GROUP * RWKV_HEAD
N_HGROUPS = RWKV_W // GW
LRU_TILE = 128
MOE_TILE = 256
COMBINE_TILE = 256
INVERT_BLOCK = 4096
V7X_VMEM_BYTES = 64 * 1024 * 1024
VMEM_LIMIT = V7X_VMEM_BYTES - 8 * 1024 * 1024


def _cparams(sem, flags=None):
    return pltpu.CompilerParams(dimension_semantics=sem, vmem_limit_bytes=VMEM_LIMIT, flags=flags)


def _layer_norm(x, g, b):
    mu = jnp.mean(x, -1, keepdims=True)
    xc = x - mu
    var = jnp.mean(xc * xc, -1, keepdims=True)
    return xc * lax.rsqrt(var + LN_EPS) * g + b


def _dot(a, b):
    return jnp.dot(a, b, preferred_element_type=F32)


def _dot_hi(a, b):
    return jnp.dot(a, b, precision=lax.Precision.HIGHEST, preferred_element_type=F32)


def _dot_nt(a, b):
    return lax.dot_general(a, b, (((1,), (1,)), ((), ())), preferred_element_type=F32)


def _dot_tn(a, b):
    return lax.dot_general(a, b, (((0,), (0,)), ((), ())), preferred_element_type=F32)


def _split_dot(x, w_bf16):
    hi = x.astype(BF16)
    lo = (x - hi.astype(F32)).astype(BF16)
    return _dot(hi, w_bf16) + _dot(lo, w_bf16)


def _full(a):
    return pl.BlockSpec(a.shape, lambda *_: (0,) * a.ndim)


def _inproj_kernel(x_ref, g_ref, b_ref, wr_ref, wl_ref, ur_ref, ul_ref):
    h = _layer_norm(x_ref[0], g_ref[...], b_ref[...]).astype(BF16)
    ur_ref[0] = _dot(h, wr_ref[...])
    ul_ref[0] = _dot(h, wl_ref[...])


def _inproj_tail_kernel(x_ref, g_ref, b_ref, wr_ref, wl_ref, ur_ref, ul_ref):
    h = _layer_norm(x_ref[...], g_ref[...], b_ref[...]).astype(BF16)
    rows = lax.broadcasted_iota(jnp.int32, (TAIL, 1), 0)
    valid = (rows >= TAIL - N_META).astype(F32)
    ur_ref[...] = _dot(h, wr_ref[...]) * valid
    ul_ref[...] = _dot(h, wl_ref[...]) * valid


def _in_projection(x, meta, ln0_g, ln0_b, w_r, w_l):
    B, T, D = x.shape
    tm = 512
    ur, ul = pl.pallas_call(
        _inproj_kernel,
        out_shape=(jax.ShapeDtypeStruct((B, T, UR_W), F32), jax.ShapeDtypeStruct((B, T, UL_W), F32)),
        grid=(B, T // tm),
        in_specs=[pl.BlockSpec((1, tm, D), lambda b, i: (b, i, 0)), _full(ln0_g), _full(ln0_b), _full(w_r), _full(w_l)],
        out_specs=(pl.BlockSpec((1, tm, UR_W), lambda b, i: (b, i, 0)),
                   pl.BlockSpec((1, tm, UL_W), lambda b, i: (b, i, 0))),
        compiler_params=_cparams(("parallel", "parallel")),
        name="inproj",
    )(x, ln0_g, ln0_b, w_r, w_l)
    tail_x = jnp.concatenate([jnp.zeros((TAIL - N_META, D), F32), meta.astype(F32)], axis=0)
    ur_t, ul_t = pl.pallas_call(
        _inproj_tail_kernel,
        out_shape=(jax.ShapeDtypeStruct((TAIL, UR_W), F32), jax.ShapeDtypeStruct((TAIL, UL_W), F32)),
        grid=(1,),
        in_specs=[_full(tail_x), _full(ln0_g), _full(ln0_b), _full(w_r), _full(w_l)],
        out_specs=(pl.BlockSpec((TAIL, UR_W), lambda i: (0, 0)), pl.BlockSpec((TAIL, UL_W), lambda i: (0, 0))),
        compiler_params=_cparams(("arbitrary",)),
        name="inproj_tail",
    )(tail_x, ln0_g, ln0_b, w_r, w_l)
    return ur, ul, ur_t, ul_t


def _rwkv_kernel(u_ref, ut_ref, mu_ref, w0_ref, wdu_ref, a0_ref, wau_ref, wgu_ref, kk_ref, ka_ref, rk_ref,
                 gng_ref, gnb_ref, bones_ref, tril_ref, eye_ref, bm_ref, msl_ref, mil_ref,
                 m8_ref, m16_ref, m32_ref, m64_ref, y_ref, s_ref, prev_ref):
    c = pl.program_id(0)
    nb = u_ref.shape[0]
    rows_of = lambda b: slice(b * CHUNK, (b + 1) * CHUNK)
    per_batch = lambda f: jnp.concatenate([f(b) for b in range(nb)], axis=0)

    @pl.when(c == 0)
    def _():
        s_ref[...] = jnp.zeros_like(s_ref)
        prev_ref[...] = jnp.zeros_like(prev_ref)

    u_x = u_ref[...].reshape(nb * CHUNK, UR_W)
    u = jnp.where(c == 0, per_batch(lambda b: ut_ref[...]), u_x)
    row = lax.broadcasted_iota(jnp.int32, u.shape, 0)
    prev_rows = per_batch(lambda b: jnp.broadcast_to(prev_ref[b:b + 1, :], (CHUNK, UR_W)))
    u_prev = jnp.where(jnp.bitwise_and(row, CHUNK - 1) == 0, prev_rows, pltpu.roll(u, 1, 0))
    for b in range(nb):
        prev_ref[b:b + 1, :] = u[(b + 1) * CHUNK - 1:(b + 1) * CHUNK, :]
    x = u + (u_prev - u) * mu_ref[...]
    r = x[:, OFF_R:OFF_R + RWKV_W]
    k = x[:, OFF_K:OFF_K + RWKV_W]
    v = x[:, OFF_V:OFF_V + RWKV_W]
    zw = x[:, OFF_ZW:OFF_ZW + LANE]
    za = x[:, OFF_ZA:OFF_ZA + LANE]
    zg = x[:, OFF_ZG:OFF_ZG + ZG_SLOT]

    bones = bones_ref[...]
    head_sum = lambda t: _split_dot(t, bones)

    z = w0_ref[...] + _dot_hi(jnp.tanh(zw), wdu_ref[...])
    logw = -math.exp(-0.5) * jax.nn.sigmoid(z)
    a = jax.nn.sigmoid(a0_ref[...] + _dot_hi(za, wau_ref[...]))
    g = _dot(jax.nn.sigmoid(zg).astype(BF16), wgu_ref[...])
    kk = k * kk_ref[...]
    kk = kk / jnp.maximum(jnp.sqrt(head_sum(kk * kk)), 1e-12)
    k = k * (1.0 + (a - 1.0) * ka_ref[...])
    kka = kk * a

    cl = _dot_hi(tril_ref[...], logw)
    cl_last = per_batch(lambda b: jnp.broadcast_to(cl[(b + 1) * CHUNK - 1:(b + 1) * CHUNK, :], (CHUNK, RWKV_W)))
    e_neg = jnp.exp(-cl)
    e_end = jnp.exp(cl_last - cl)
    rt = r * jnp.exp(cl)
    kt = k * e_neg
    at = -kk * jnp.exp(cl - logw)
    bt = kka * e_neg
    kw = k * e_end
    bw = kka * e_end
    w_end = jnp.exp(cl_last)

    bm = bm_ref[...]
    eye = eye_ref[...]
    msl = msl_ref[...]
    mil = mil_ref[...]
    lane = lax.broadcasted_iota(jnp.int32, (2 * GW, LANE), 1)
    tile4 = lambda t: jnp.concatenate([t] * HEADS_PER_GROUP, axis=0)
    fold4 = lambda t: sum(t[i * CHUNK:(i + 1) * CHUNK] for i in range(HEADS_PER_GROUP))
    wide = lambda t: jnp.concatenate([t, t], axis=1)
    y_rows = []
    for b in range(nb):
        rb = rows_of(b)
        ys = []
        for hg in range(N_HGROUPS):
            sl = slice(hg * GW, (hg + 1) * GW)
            v4 = v[rb, sl]
            lhs = jnp.concatenate([tile4(at[rb, sl]) * bm, tile4(rt[rb, sl]) * bm], axis=0).astype(BF16)
            rhs = jnp.concatenate([bt[rb, sl], kt[rb, sl]], axis=0).astype(BF16)
            aa = _dot_nt(lhs, rhs)
            aa_sw = pltpu.roll(aa, CHUNK, 1)
            vs_b = wide(jnp.where(lane < CHUNK, aa, aa_sw))
            vs_k = wide(jnp.where(lane < CHUNK, aa_sw, aa))
            a_ab = vs_b[:GW] * msl
            a_ak = (vs_k[:GW] * msl).astype(BF16)
            a_rb = (vs_b[GW:] * mil).astype(BF16)
            a_rk = (vs_k[GW:] * mil).astype(BF16)

            a0 = (a_ab * m8_ref[...]).astype(BF16)
            a2 = _dot(a0, a0).astype(BF16)
            a4 = _dot(a2, a2).astype(BF16)
            p1 = eye + a0.astype(F32)
            p1 = p1 + _dot(p1.astype(BF16), a2)
            t = p1 + _dot(p1.astype(BF16), a4)
            for m_ref in (m16_ref, m32_ref, m64_ref):
                tb = t.astype(BF16)
                off = (a_ab * m_ref[...]).astype(BF16)
                t = t + _dot(_dot(tb, off).astype(BF16), tb)
            tb = t.astype(BF16)

            s = s_ref[b, hg]
            sb = s.astype(BF16)
            vt = tile4(v4).astype(BF16)
            xx = _dot_nt(lhs[:GW], sb) + _dot(a_ak, vt)
            uu = _dot(tb, xx.astype(BF16)) * bm
            yy = (_dot_nt(lhs[GW:], sb) + _dot(a_rb, uu.astype(BF16)) + _dot(a_rk, vt)) * bm
            ys.append(fold4(yy))
            u4 = fold4(uu)
            upd = _dot_tn(jnp.concatenate([u4, v4], axis=0).astype(BF16),
                          jnp.concatenate([bw[rb, sl], kw[rb, sl]], axis=0).astype(BF16))
            s_ref[b, hg] = s * w_end[b * CHUNK:b * CHUNK + 1, sl] + upd * bm
        y_rows.append(jnp.concatenate(ys, axis=1))

    y = jnp.concatenate(y_rows, axis=0)
    inv_n = 1.0 / RWKV_HEAD
    ym = head_sum(y) * inv_n
    yc = y - ym
    yv = head_sum(yc * yc) * inv_n
    yn = yc * lax.rsqrt(yv + GN_EPS) * gng_ref[...] + gnb_ref[...]
    bonus = head_sum(r * k * rk_ref[...]) * v
    y_ref[...] = ((yn + bonus) * g).astype(y_ref.dtype).reshape(y_ref.shape)


def _rwkv_masks(nb):
    i = jnp.arange(GW)[:, None]
    j = jnp.arange(GW)[None, :]
    same = lambda n: (i // n) == (j // n)
    f = lambda m: m.astype(F32)
    bm = f(same(RWKV_HEAD))
    msl = f(same(RWKV_HEAD) & (i > j))
    mil = f(same(RWKV_HEAD) & (i >= j))
    m8 = f(same(8))
    m16 = f(same(16) & ~same(8))
    m32 = f(same(32) & ~same(16))
    m64 = f(same(64) & ~same(32))
    eye = f(i == j)
    ti = jnp.arange(nb * CHUNK)
    tril = f((ti[:, None] >= ti[None, :]) & ((ti[:, None] // CHUNK) == (ti[None, :] // CHUNK)))
    hi = jnp.arange(RWKV_W)
    bones = ((hi[:, None] // RWKV_HEAD) == (hi[None, :] // RWKV_HEAD)).astype(BF16)
    return bones, tril, eye, bm, msl, mil, m8, m16, m32, m64


def _rwkv_mixer(ur, ur_tail, params):
    B, T, _ = ur.shape
    consts = _rwkv_masks(B)
    x_map = lambda c: (0, jnp.maximum(c - 1, 0), 0)
    return pl.pallas_call(
        _rwkv_kernel,
        out_shape=jax.ShapeDtypeStruct((B, T, RWKV_W), BF16),
        grid=(T // CHUNK + 1,),
        in_specs=[pl.BlockSpec((B, CHUNK, UR_W), x_map),
                  pl.BlockSpec((CHUNK, UR_W), lambda c: (TAIL // CHUNK - 1, 0))]
                 + [_full(p) for p in params] + [_full(m) for m in consts],
        out_specs=pl.BlockSpec((B, CHUNK, RWKV_W), x_map),
        scratch_shapes=[pltpu.VMEM((B, N_HGROUPS, GW, GW), F32), pltpu.VMEM((B, UR_W), F32)],
        compiler_params=_cparams(("arbitrary",)),
        name="rwkv7",
    )(ur, ur_tail, *params, *consts)


def _rwkv_pipe_mixer(ur, ur_tail, params):
    B, T, _ = ur.shape
    blk = TAIL
    assert T % blk == 0 and blk % CHUNK == 0
    n_blocks = T // blk
    rows = B * blk
    consts = _rwkv_masks(rows // CHUNK)
    in_map = lambda s: (0, jnp.clip(s - 1, 0, n_blocks - 1), 0)
    out_map = lambda s: (0, jnp.clip(s - 2, 0, n_blocks - 1), 0)
    slot2 = lambda w, dt: pltpu.VMEM((2, rows, w), dt)
    return pl.pallas_call(
        _rwkv_pipe_kernel,
        out_shape=jax.ShapeDtypeStruct((B, T, RWKV_W), BF16),
        grid=(n_blocks + 2,),
        in_specs=[pl.BlockSpec((B, blk, UR_W), in_map), _full(ur_tail)]
                 + [_full(p) for p in params] + [_full(m) for m in consts],
        out_specs=pl.BlockSpec((B, blk, RWKV_W), out_map),
        scratch_shapes=[pltpu.VMEM((B, N_HGROUPS, GW, GW), F32), pltpu.VMEM((B, UR_W), F32),
                        slot2(RWKV_W, F32), slot2(RWKV_W, BF16), slot2(RWKV_W, BF16), slot2(RWKV_W, F32),
                        pltpu.VMEM((2, 8 * rows // CHUNK, RWKV_W), F32), slot2(RWKV_W, F32), slot2(RWKV_W, F32),
                        pltpu.VMEM((rows, RWKV_W), F32)],
        compiler_params=_cparams(("arbitrary",)),
        name="rwkv7",
    )(ur, ur_tail, *params, *consts)


def _rwkv_pipe_kernel(u_ref, ut_ref, mu_ref, w0_ref, wdu_ref, a0_ref, wau_ref, wgu_ref, kk_ref, ka_ref, rk_ref,
                      gng_ref, gnb_ref, bones_ref, tril_ref, eye_ref, bm_ref, msl_ref, mil_ref,
                      m8_ref, m16_ref, m32_ref, m64_ref, y_ref,
                      s_ref, prev_ref, y0_s, q_s, mc_s, nc_s, we_s, bonus_s, g_s, yraw_s):
    s_id = pl.program_id(0)
    nb = u_ref.shape[0]
    blk = u_ref.shape[1]
    npc = blk // CHUNK
    nseq = nb * npc
    seq_rows = lambda q: slice(q * CHUNK, (q + 1) * CHUNK)
    per_seq = lambda f: jnp.concatenate([f(q) for q in range(nseq)], axis=0)
    w_slot = lax.rem(s_id, 2)
    r_slot = 1 - w_slot

    @pl.when(s_id == 0)
    def _():
        s_ref[...] = jnp.zeros_like(s_ref)
        prev_ref[...] = jnp.zeros_like(prev_ref)
        for ref in (y0_s, q_s, mc_s, nc_s, we_s, bonus_s, g_s):
            ref[1] = jnp.zeros(ref.shape[1:], ref.dtype)

    u_x = u_ref[...].reshape(nb * blk, UR_W)
    u = jnp.where(s_id == 0, jnp.concatenate([ut_ref[...]] * nb, axis=0), u_x)
    row = lax.broadcasted_iota(jnp.int32, u.shape, 0)
    prev_rows = jnp.concatenate([jnp.broadcast_to(prev_ref[b:b + 1, :], (blk, UR_W)) for b in range(nb)], axis=0)
    u_prev = jnp.where(jnp.bitwise_and(row, blk - 1) == 0, prev_rows, pltpu.roll(u, 1, 0))
    for b in range(nb):
        prev_ref[b:b + 1, :] = u[(b + 1) * blk - 1:(b + 1) * blk, :]
    x = u + (u_prev - u) * mu_ref[...]
    r = x[:, OFF_R:OFF_R + RWKV_W]
    k = x[:, OFF_K:OFF_K + RWKV_W]
    v = x[:, OFF_V:OFF_V + RWKV_W]
    zw = x[:, OFF_ZW:OFF_ZW + LANE]
    za = x[:, OFF_ZA:OFF_ZA + LANE]
    zg = x[:, OFF_ZG:OFF_ZG + ZG_SLOT]

    bones = bones_ref[...]
    head_sum = lambda t: _dot(t.astype(BF16), bones)

    z = w0_ref[...] + _dot(jnp.tanh(zw).astype(BF16), wdu_ref[...])
    logw = -math.exp(-0.5) * jax.nn.sigmoid(z)
    a = jax.nn.sigmoid(a0_ref[...] + _dot(za.astype(BF16), wau_ref[...]))
    g = _dot(jax.nn.sigmoid(zg).astype(BF16), wgu_ref[...])
    kk = k * kk_ref[...]
    kk = kk / jnp.maximum(jnp.sqrt(head_sum(kk * kk)), 1e-12)
    k = k * (1.0 + (a - 1.0) * ka_ref[...])
    kka = kk * a
    bonus_s[w_slot] = head_sum(r * k * rk_ref[...]) * v
    g_s[w_slot] = g

    cl = logw
    row_in_chunk = jnp.bitwise_and(lax.broadcasted_iota(jnp.int32, cl.shape, 0), CHUNK - 1)
    d = 1
    while d < CHUNK:
        cl = cl + jnp.where(row_in_chunk >= d, pltpu.roll(cl, d, 0), 0.0)
        d *= 2
    cl_last = per_seq(lambda q: jnp.broadcast_to(cl[(q + 1) * CHUNK - 1:(q + 1) * CHUNK, :], (CHUNK, RWKV_W)))
    e_neg = jnp.exp(-cl)
    e_end = jnp.exp(cl_last - cl)
    rt = r * jnp.exp(cl)
    kt = k * e_neg
    at = -kk * jnp.exp(cl - logw)
    bt = kka * e_neg
    kw = k * e_end
    bw = kka * e_end
    w_end = jnp.exp(cl_last)

    bm = bm_ref[...]
    bm16 = bm.astype(BF16)
    eye = eye_ref[...]
    msl = msl_ref[...]
    mil = mil_ref[...]
    lane = lax.broadcasted_iota(jnp.int32, (2 * GW, LANE), 1)
    tile4 = lambda t: jnp.concatenate([t] * HEADS_PER_GROUP, axis=0)
    fold4 = lambda t: sum(t[i * CHUNK:(i + 1) * CHUNK] for i in range(HEADS_PER_GROUP))
    wide = lambda t: jnp.concatenate([t, t], axis=1)
    for q in range(nseq):
        we_s[w_slot, q * 8:(q + 1) * 8, :] = w_end[q * CHUNK:q * CHUNK + 8, :]

    probs = [(seq_rows(q), slice(hg * GW, (hg + 1) * GW)) for q in range(nseq) for hg in range(N_HGROUPS)]
    each = lambda f, *ls: [f(*xs) for xs in zip(*ls)]
    b16 = lambda t: t.astype(BF16)
    v4 = [v[rq, sl] for rq, sl in probs]
    r_st = [tile4(rt[rq, sl]) * bm for rq, sl in probs]
    a_st = [b16(tile4(at[rq, sl]) * bm) for rq, sl in probs]
    lhs = each(lambda x, y: jnp.concatenate([x, b16(y)], axis=0), a_st, r_st)
    rhs = [b16(jnp.concatenate([bt[rq, sl], kt[rq, sl]], axis=0)) for rq, sl in probs]
    aa = each(_dot_nt, lhs, rhs)
    aa_sw = each(lambda t: pltpu.roll(t, CHUNK, 1), aa)
    vs_b = each(lambda x, y: wide(jnp.where(lane < CHUNK, x, y)), aa, aa_sw)
    vs_k = each(lambda x, y: wide(jnp.where(lane < CHUNK, y, x)), aa, aa_sw)
    a_ab = each(lambda t: t[:GW] * msl, vs_b)
    a_ak = each(lambda t: b16(t[:GW] * msl), vs_k)
    a_rb = each(lambda t: b16(t[GW:] * mil), vs_b)
    a_rk = each(lambda t: b16(t[GW:] * mil), vs_k)

    a0 = each(lambda t: b16(t * m8_ref[...]), a_ab)
    a2 = each(lambda t: b16(_dot(t, t)), a0)
    a4 = each(lambda t: b16(_dot(t, t)), a2)
    p1 = each(lambda t: eye + t.astype(F32), a0)
    p1 = each(lambda p, t: p + _dot(b16(p), t), p1, a2)
    tt = each(lambda p, t: p + _dot(b16(p), t), p1, a4)
    for m_ref in (m16_ref, m32_ref, m64_ref):
        tb = each(b16, tt)
        off = each(lambda t: b16(t * m_ref[...]), a_ab)
        half = each(lambda x, y: b16(_dot(x, y)), tb, off)
        tt = each(lambda t, x, y: t + _dot(x, y), tt, half, tb)
    tb = each(b16, tt)

    vt = each(lambda t: b16(tile4(t)), v4)
    x0 = each(lambda x, y: b16(_dot(x, y)), a_ak, vt)
    u0 = each(lambda x, y: _dot(x, y) * bm, tb, x0)
    ta = each(_dot, tb, a_st)
    y0 = each(lambda x, y, z, w: (_dot(x, b16(y)) + _dot(z, w)) * bm, a_rb, u0, a_rk, vt)
    qq = each(lambda x, y, z: x + _dot(y, b16(z)), r_st, a_rb, ta)
    left = each(lambda x, y, z: b16(jnp.concatenate([jnp.concatenate([fold4(x), fold4(y)], axis=1),
                                                     jnp.concatenate([jnp.zeros_like(z), z], axis=1)], axis=0)),
                ta, u0, v4)
    right = [b16(jnp.concatenate([bw[rq, sl], kw[rq, sl]], axis=0)) for rq, sl in probs]
    mn = each(_dot_tn, left, right)
    for i, (rq, sl) in enumerate(probs):
        y0_s[w_slot, rq, sl] = fold4(y0[i])
        q_s[w_slot, rq, sl] = b16(fold4(qq[i]))
        mc_s[w_slot, rq, sl] = b16(fold4(mn[i][:GW] * bm))
        nc_s[w_slot, rq, sl] = fold4(mn[i][GW:] * bm)

    for b in range(nb):
        for hg in range(N_HGROUPS):
            sl = slice(hg * GW, (hg + 1) * GW)
            s = s_ref[b, hg]
            for j in range(npc):
                q = b * npc + j
                rq = seq_rows(q)
                sb = s.astype(BF16)
                q_st = tile4(q_s[r_slot, rq, sl]) * bm16
                mc_bd = tile4(mc_s[r_slot, rq, sl]) * bm16
                nc_bd = tile4(nc_s[r_slot, rq, sl]) * bm
                yraw_s[rq, sl] = y0_s[r_slot, rq, sl] + fold4(_dot_nt(q_st, sb))
                s = s * we_s[r_slot, q * 8:q * 8 + 1, sl] + _dot(sb, mc_bd) + nc_bd
            s_ref[b, hg] = s

    y = yraw_s[...]
    inv_n = 1.0 / RWKV_HEAD
    ym = head_sum(y) * inv_n
    yc = y - ym
    yv = head_sum(yc * yc) * inv_n
    yn = yc * lax.rsqrt(yv + GN_EPS) * gng_ref[...] + gnb_ref[...]
    y_ref[...] = ((yn + bonus_s[r_slot]) * g_s[r_slot]).astype(y_ref.dtype).reshape(y_ref.shape)


def _gelu_tanh(x):
    return 0.5 * x * (1.0 + jnp.tanh(math.sqrt(2.0 / math.pi) * (x + 0.044715 * (x * x * x))))


def _lru_kernel(u_ref, ut_ref, cw_ref, cb_ref, wrg_ref, brg_ref, wig_ref, big_ref, lam_ref, y_ref,
                xprev_ref, hprev_ref):
    c = pl.program_id(1)

    @pl.when(c == 0)
    def _():
        xprev_ref[...] = jnp.zeros_like(xprev_ref)
        hprev_ref[...] = jnp.zeros_like(hprev_ref)

    u = jnp.where(c == 0, ut_ref[...], u_ref[0])
    xl = u[:, :LRU_W]
    gl = u[:, LRU_W:]
    row = lax.broadcasted_iota(jnp.int32, (LRU_TILE, LRU_W), 0)
    row8 = lax.broadcasted_iota(jnp.int32, (8, LRU_W), 0)
    xprev = xprev_ref[...]
    xc = cb_ref[...] + cw_ref[CONV_WIDTH - 1:CONV_WIDTH, :] * xl
    for d in range(1, CONV_WIDTH):
        rolled = pltpu.roll(xl, d, 0)
        head = jnp.where(row8 < d, pltpu.roll(xprev, d, 0), rolled[:8])
        shifted = jnp.concatenate([head, rolled[8:]], axis=0)
        xc = xc + cw_ref[CONV_WIDTH - 1 - d:CONV_WIDTH - d, :] * shifted
    xprev_ref[...] = xl[LRU_TILE - 8:, :]

    xcb = xc.astype(BF16)
    gate_r = jax.nn.sigmoid(_dot(xcb, wrg_ref[...]) + brg_ref[...])
    gate_i = jax.nn.sigmoid(_dot(xcb, wig_ref[...]) + big_ref[...])
    lam = lam_ref[...]
    log_sig = -(jnp.maximum(-lam, 0.0) + jnp.log1p(jnp.exp(-jnp.abs(lam))))
    log_a = LRU_C * gate_r * log_sig
    a = jnp.exp(log_a)
    mult = jnp.sqrt(jnp.maximum(1.0 - jnp.exp(2.0 * log_a), 0.0))
    b = mult * gate_i * xc
    b = jnp.where((c == 0) & (row < LRU_TILE - N_META), 0.0, b)

    d = 1
    while d < LRU_TILE:
        keep = row >= d
        a_sh = jnp.where(keep, pltpu.roll(a, d, 0), 1.0)
        b_sh = jnp.where(keep, pltpu.roll(b, d, 0), 0.0)
        b = a * b_sh + b
        a = a * a_sh
        d *= 2
    h = b + a * hprev_ref[...]
    hprev_ref[...] = h[LRU_TILE - 1:, :]
    y_ref[0] = (h * _gelu_tanh(gl)).astype(y_ref.dtype)


def _lru_mixer(ul, ul_tail, params):
    B, T, _ = ul.shape
    assert TAIL == LRU_TILE
    x_map = lambda b, c: (b, jnp.maximum(c - 1, 0), 0)
    return pl.pallas_call(
        _lru_kernel,
        out_shape=jax.ShapeDtypeStruct((B, T, LRU_W), BF16),
        grid=(B, T // LRU_TILE + 1),
        in_specs=[pl.BlockSpec((1, LRU_TILE, UL_W), x_map), _full(ul_tail)] + [_full(p) for p in params],
        out_specs=pl.BlockSpec((1, LRU_TILE, LRU_W), x_map),
        scratch_shapes=[pltpu.VMEM((8, LRU_W), F32), pltpu.VMEM((1, LRU_W), F32)],
        compiler_params=_cparams(("parallel", "arbitrary")),
        name="rglru",
    )(ul, ul_tail, *params)


def _route(lg):
    lane = lax.broadcasted_iota(jnp.int32, lg.shape, 1)
    neg = jnp.float32(-jnp.inf)
    rmax = lambda t: jnp.max(t, axis=1, keepdims=True)
    first = lambda hit: jnp.min(jnp.where(hit, lane, LANE), axis=1, keepdims=True)
    is_grp = lane < N_GROUPS
    gl = jnp.where(is_grp, lg, neg)
    gmax = rmax(gl)
    g_sel = first(gl == gmax)
    p_g = 1.0 / jnp.sum(jnp.where(is_grp, jnp.exp(lg - gmax), 0.0), axis=1, keepdims=True)
    ex = lane - N_GROUPS
    in_grp = (ex >= 0) & (ex < N_EXPERTS) & (jnp.right_shift(ex, 3) == g_sel)
    el = jnp.where(in_grp, lg, neg)
    v1 = rmax(el)
    i1 = first(el == v1)
    el2 = jnp.where(lane == i1, neg, el)
    v2 = rmax(el2)
    i2 = first(el2 == v2)
    t = jnp.exp(v2 - v1)
    gate1 = p_g / (1.0 + t)
    gate2 = p_g * t / (1.0 + t)
    e1 = (i1 - N_GROUPS).astype(F32)
    e2 = (i2 - N_GROUPS).astype(F32)
    return jnp.where(lane == 0, e1, jnp.where(lane == 1, e2, jnp.where(lane == 2, gate1, jnp.where(lane == 3, gate2, 0.0))))


SLABS = D_MODEL // LANE


def _store_token_tiles(ref, val):
    n = val.shape[0]
    for s in range(SLABS):
        ref[pl.ds(s, n, stride=SLABS), :] = val[:, s * LANE:(s + 1) * LANE]


def _load_token_slabs(ref, n):
    return [ref[pl.ds(s, n, stride=SLABS), :] for s in range(SLABS)]


def _outproj_kernel(x_ref, yr_ref, yl_ref, g0_ref, b0_ref, wor_ref, wol_ref, g1_ref, b1_ref,
                    wrt_hi_ref, wrt_lo_ref, brt_ref, h1_ref, rt_ref):
    h0 = _layer_norm(x_ref[0], g0_ref[...], b0_ref[...])
    mix = _dot(yr_ref[0], wor_ref[...]) + _dot(yl_ref[0], wol_ref[...])
    h1 = _layer_norm(DEEPNORM_ALPHA * h0 + mix, g1_ref[...], b1_ref[...])
    _store_token_tiles(h1_ref.at[0], h1)
    hi = h1.astype(BF16)
    lo = (h1 - hi.astype(F32)).astype(BF16)
    w_hi = wrt_hi_ref[...]
    lg = _dot(hi, w_hi) + (_dot(hi, wrt_lo_ref[...]) + _dot(lo, w_hi)) + brt_ref[...]
    rt_ref[0] = _route(lg)


def _out_projection(x, y_rwkv, y_lru, ln0_g, ln0_b, wo_r, wo_l, ln1_g, ln1_b, wrt_hi, wrt_lo, brt):
    B, T, D = x.shape
    tm = 512
    rows = lambda w: pl.BlockSpec((1, tm, w), lambda b, i: (b, i, 0))
    return pl.pallas_call(
        _outproj_kernel,
        out_shape=(jax.ShapeDtypeStruct((B, T * SLABS, LANE), F32), jax.ShapeDtypeStruct((B, T, LANE), F32)),
        grid=(B, T // tm),
        in_specs=[rows(D), rows(RWKV_W), rows(LRU_W), _full(ln0_g), _full(ln0_b), _full(wo_r), _full(wo_l),
                  _full(ln1_g), _full(ln1_b), _full(wrt_hi), _full(wrt_lo), _full(brt)],
        out_specs=(pl.BlockSpec((1, tm * SLABS, LANE), lambda b, i: (b, i, 0)), rows(LANE)),
        compiler_params=_cparams(("parallel", "parallel")),
        name="outproj",
    )(x, y_rwkv, y_lru, ln0_g, ln0_b, wo_r, wo_l, ln1_g, ln1_b, wrt_hi, wrt_lo, brt)


def _invert_kernel(pad_lo_ref, pad_hi_ref, dest_ref, out_ref):
    i = pl.program_id(0)

    @pl.when(i == 0)
    def _():
        def zero(j, carry):
            out_ref[j] = 0
            return carry
        for e in range(N_EXPERTS):
            lax.fori_loop(pad_lo_ref[e], pad_hi_ref[e], zero, 0)
        lax.fori_loop(pad_hi_ref[N_EXPERTS - 1], out_ref.shape[0], zero, 0)

    base = i * INVERT_BLOCK

    def body(j, carry):
        out_ref[dest_ref[j]] = base + j
        return carry

    lax.fori_loop(0, INVERT_BLOCK, body, 0, unroll=8)


def _invert_slots(dest, n_slots, pad_lo, pad_hi):
    A = dest.shape[0]
    grid_spec = pltpu.PrefetchScalarGridSpec(
        num_scalar_prefetch=2,
        grid=(A // INVERT_BLOCK,),
        in_specs=[pl.BlockSpec((INVERT_BLOCK,), lambda i, lo, hi: (i,), memory_space=pltpu.SMEM)],
        out_specs=pl.BlockSpec(memory_space=pltpu.SMEM),
    )
    return pl.pallas_call(
        _invert_kernel,
        out_shape=jax.ShapeDtypeStruct((n_slots,), jnp.int32),
        grid_spec=grid_spec,
        compiler_params=_cparams(("arbitrary",)),
        name="invert_slots",
    )(pad_lo, pad_hi, dest)


def _row_copy(src_hbm, src_row, dst_ref, dst_row, sem):
    return pltpu.make_async_copy(src_hbm.at[pl.ds(src_row * SLABS, SLABS), :],
                                 dst_ref.at[pl.ds(dst_row * SLABS, SLABS), :], sem)


def _wait_tiles(src_hbm, dst_ref, sem):
    pltpu.make_async_copy(src_hbm.at[pl.ds(0, dst_ref.shape[0]), :], dst_ref, sem).wait()


DMA_UNROLL = 8


def _moe_kernel(te_ref, nv_ref, ra_cur_ref, ra_nxt_ref, h_hbm, wg_ref, wu_ref, wd_ref, o_ref,
                xbuf, sem, wgb_ref, wub_ref, wdb_ref):
    i = pl.program_id(0)
    n_valid = nv_ref[0]
    slot = lax.rem(i, 2)

    def start_gather(ra_ref, s):
        def body(jj, carry):
            for u in range(DMA_UNROLL):
                j = jj * DMA_UNROLL + u
                tok = lax.shift_right_logical(ra_ref[j], 1)
                _row_copy(h_hbm, tok, xbuf.at[s], j, sem.at[s]).start(priority=u % 2)
            return carry
        lax.fori_loop(0, MOE_TILE // DMA_UNROLL, body, 0)

    @pl.when(i == 0)
    def _():
        start_gather(ra_cur_ref, 0)

    @pl.when(i + 1 < n_valid)
    def _():
        start_gather(ra_nxt_ref, 1 - slot)

    e = te_ref[i]
    e_prev = te_ref[jnp.maximum(i - 1, 0)]

    @pl.when((i == 0) | (e != e_prev))
    def _():
        wgb_ref[...] = wg_ref[0].astype(BF16)
        wub_ref[...] = wu_ref[0].astype(BF16)
        wdb_ref[...] = wd_ref[0].astype(BF16)

    @pl.when(i < n_valid)
    def _():
        _wait_tiles(h_hbm, xbuf.at[slot], sem.at[slot])
        xb =jnp.concatenate(_load_token_slabs(xbuf.at[slot], MOE_TILE), axis=1).astype(BF16)
        hg = _dot(xb, wgb_ref[...])
        hu = _dot(xb, wub_ref[...])
        mid = (hg * jax.nn.sigmoid(hg) * hu).astype(BF16)
        _store_token_tiles(o_ref, _dot(mid, wdb_ref[...]))

    @pl.when(i >= n_valid)
    def _():
        o_ref[...] = jnp.zeros_like(o_ref)


def _moe_experts(h1, row_asg, tile_expert, n_valid, w_gate, w_up, w_down):
    D = D_MODEL
    n_tiles = row_asg.shape[0] // MOE_TILE
    smem_tile = lambda f: pl.BlockSpec((MOE_TILE,), f, memory_space=pltpu.SMEM)
    grid_spec = pltpu.PrefetchScalarGridSpec(
        num_scalar_prefetch=2,
        grid=(n_tiles,),
        in_specs=[smem_tile(lambda i, te, nv: (i,)),
                  smem_tile(lambda i, te, nv: (jnp.minimum(i + 1, n_tiles - 1),)),
                  pl.BlockSpec(memory_space=pl.ANY),
                  pl.BlockSpec((1, D, D_EXPERT), lambda i, te, nv: (te[i], 0, 0)),
                  pl.BlockSpec((1, D, D_EXPERT), lambda i, te, nv: (te[i], 0, 0)),
                  pl.BlockSpec((1, D_EXPERT, D), lambda i, te, nv: (te[i], 0, 0))],
        out_specs=pl.BlockSpec((MOE_TILE * SLABS, LANE), lambda i, te, nv: (i, 0)),
        scratch_shapes=[pltpu.VMEM((2, MOE_TILE * SLABS, LANE), F32), pltpu.SemaphoreType.DMA((2,)),
                        pltpu.VMEM((D, D_EXPERT), BF16), pltpu.VMEM((D, D_EXPERT), BF16),
                        pltpu.VMEM((D_EXPERT, D), BF16)],
    )
    return pl.pallas_call(
        _moe_kernel,
        out_shape=jax.ShapeDtypeStruct((n_tiles * MOE_TILE * SLABS, LANE), F32),
        grid_spec=grid_spec,
        compiler_params=_cparams(("arbitrary",)),
        name="moe_experts",
    )(tile_expert, n_valid, row_asg, row_asg, h1, w_gate, w_up, w_down)


def _combine_kernel(d_cur_ref, d_nxt_ref, h_ref, gate_ref, g_ref, b_ref, y_hbm, o_ref, ybuf, sem):
    i = pl.program_id(0)
    n = pl.num_programs(0)
    slot = lax.rem(i, 2)

    def start_gather(d_ref, s):
        def body(tt, carry):
            for u in range(DMA_UNROLL // TOP_K):
                t = tt * (DMA_UNROLL // TOP_K) + u
                for k in range(TOP_K):
                    _row_copy(y_hbm, d_ref[TOP_K * t + k], ybuf.at[s, k], t, sem.at[s]).start(priority=k % 2)
            return carry
        lax.fori_loop(0, COMBINE_TILE * TOP_K // DMA_UNROLL, body, 0)

    @pl.when(i == 0)
    def _():
        start_gather(d_cur_ref, 0)

    @pl.when(i + 1 < n)
    def _():
        start_gather(d_nxt_ref, 1 - slot)

    for k in range(TOP_K):
        _wait_tiles(y_hbm, ybuf.at[slot, k], sem.at[slot])

    gate = gate_ref[...]
    tm = COMBINE_TILE
    ga = jnp.broadcast_to(gate[:, 0:1], (tm, LANE))
    gb = jnp.broadcast_to(gate[:, 1:2], (tm, LANE))
    hs = _load_token_slabs(h_ref, tm)
    ya = _load_token_slabs(ybuf.at[slot, 0], tm)
    yb = _load_token_slabs(ybuf.at[slot, 1], tm)
    z = [DEEPNORM_ALPHA * h + (ga * a + gb * b) for h, a, b in zip(hs, ya, yb)]
    inv_d = 1.0 / D_MODEL
    mu = sum(jnp.sum(t, axis=1, keepdims=True) for t in z) * inv_d
    zc = [t - mu for t in z]
    var = sum(jnp.sum(t * t, axis=1, keepdims=True) for t in zc) * inv_d
    rstd = lax.rsqrt(var + LN_EPS)
    for s in range(SLABS):
        cols = slice(s * LANE, (s + 1) * LANE)
        o_ref[:, cols] = zc[s] * rstd * g_ref[:, cols] + b_ref[:, cols]


def _combine(h1, ybuf, dest, gates, ln2_g, ln2_b):
    D = D_MODEL
    M = h1.shape[0] // SLABS
    tm = COMBINE_TILE
    n = M // tm
    smem_tile = lambda f: pl.BlockSpec((TOP_K * tm,), f, memory_space=pltpu.SMEM)
    return pl.pallas_call(
        _combine_kernel,
        out_shape=jax.ShapeDtypeStruct((M, D), F32),
        grid=(n,),
        in_specs=[smem_tile(lambda i: (i,)), smem_tile(lambda i: (jnp.minimum(i + 1, n - 1),)),
                  pl.BlockSpec((tm * SLABS, LANE), lambda i: (i, 0)), pl.BlockSpec((tm, TOP_K), lambda i: (i, 0)),
                  _full(ln2_g), _full(ln2_b), pl.BlockSpec(memory_space=pl.ANY)],
        out_specs=pl.BlockSpec((tm, D), lambda i: (i, 0)),
        scratch_shapes=[pltpu.VMEM((2, TOP_K, tm * SLABS, LANE), F32), pltpu.SemaphoreType.DMA((2,))],
        compiler_params=_cparams(("arbitrary",)),
        name="combine",
    )(dest, dest, h1, gates, ln2_g, ln2_b, ybuf)


def _routing_plan(route):
    M = route.shape[0]
    eid = route[:, :TOP_K].astype(jnp.int32).reshape(-1)
    gates = route[:, TOP_K:2 * TOP_K]
    A = M * TOP_K
    onehot = (eid[:, None] == jnp.arange(N_EXPERTS, dtype=eid.dtype)[None, :]).astype(jnp.int32)
    csum = jnp.cumsum(onehot, axis=0)
    rank = jnp.sum(csum * onehot, axis=1) - 1
    counts = csum[-1]
    pcounts = (counts + MOE_TILE - 1) // MOE_TILE * MOE_TILE
    pends = jnp.cumsum(pcounts)
    pstarts = pends - pcounts
    dest = (jnp.sum(onehot * pstarts[None, :], axis=1) + rank).astype(jnp.int32)
    n_tiles = (A + N_EXPERTS * (MOE_TILE - 1) + MOE_TILE - 1) // MOE_TILE
    n_valid = (pends[-1] // MOE_TILE).astype(jnp.int32)
    tile_start = jnp.minimum(jnp.arange(n_tiles, dtype=jnp.int32) * MOE_TILE, pends[-1] - 1)
    tile_expert = jnp.sum((pends[None, :] <= tile_start[:, None]).astype(jnp.int32), axis=1)
    tile_expert = jnp.minimum(tile_expert, N_EXPERTS - 1).astype(jnp.int32)
    pad_lo = (pstarts + counts).astype(jnp.int32)
    pad_hi = pends.astype(jnp.int32)
    return gates, dest, n_tiles * MOE_TILE, tile_expert, n_valid.reshape(1), pad_lo, pad_hi


def kernel(x, meta, ln0_g, ln0_b, w_in, mu_shift, w0, w_decay_up, a0, w_a_up, w_g_up, k_k, k_a, r_k, gn_g, gn_b, conv_w, conv_b, w_rg, b_rg, w_ig, b_ig, lru_lambda, w_out, ln1_g, ln1_b, w_router_grp, b_router_grp, w_router_exp, b_router_exp, w_exp_gate, w_exp_up, w_exp_down, ln2_g, ln2_b):
    B, T, D = x.shape
    assert D == D_MODEL and T % 512 == 0 and w_in.shape[0] == 1
    assert (B * T * TOP_K) % INVERT_BLOCK == 0
    row = lambda p: p.reshape(1, -1).astype(F32)
    n_rw = 3 * RWKV_W
    w_in0 = w_in[0]

    def slots(p):
        pad = lambda a, n: jnp.pad(a, [(0, 0)] * (a.ndim - 1) + [(0, n - a.shape[-1])])
        zw = p[..., n_rw:n_rw + DECAY_RANK]
        za = p[..., n_rw + DECAY_RANK:n_rw + DECAY_RANK + AAA_RANK]
        zg = p[..., n_rw + DECAY_RANK + AAA_RANK:n_rw + DECAY_RANK + AAA_RANK + GATE_RANK]
        return jnp.concatenate([p[..., :n_rw], pad(zw, LANE), pad(za, LANE), pad(zg, ZG_SLOT)], axis=-1)

    rwkv_cols = n_rw + DECAY_RANK + AAA_RANK + GATE_RANK
    w_r = slots(w_in0[:, :rwkv_cols]).astype(BF16)
    w_l = w_in0[:, rwkv_cols:].astype(BF16)
    ur, ul, ur_t, ul_t = _in_projection(x, meta, row(ln0_g), row(ln0_b), w_r, w_l)

    pad_rows = lambda a, n: jnp.pad(a, ((0, n - a.shape[0]), (0, 0)))
    rwkv_params = (slots(mu_shift[0][None, :]).astype(F32), row(w0[0]), pad_rows(w_decay_up[0], LANE).astype(BF16),
                   row(a0[0]), pad_rows(w_a_up[0], LANE).astype(BF16), pad_rows(w_g_up[0], ZG_SLOT).astype(BF16),
                   row(k_k[0]), row(k_a[0]), row(r_k[0]), row(gn_g[0]), row(gn_b[0]))
    y_rwkv = _rwkv_pipe_mixer(ur, ur_t, rwkv_params)

    blockdiag = lambda w: jax.scipy.linalg.block_diag(*[w[i] for i in range(LRU_BLOCKS)]).astype(BF16)
    lru_params = (conv_w[0], row(conv_b[0]), blockdiag(w_rg[0]), row(b_rg[0]), blockdiag(w_ig[0]), row(b_ig[0]),
                  row(lru_lambda[0]))
    y_lru = _lru_mixer(ul, ul_t, lru_params)

    w_rt = jnp.concatenate([w_router_grp[0], w_router_exp[0]], axis=1)
    w_rt = jnp.pad(w_rt, ((0, 0), (0, LANE - w_rt.shape[1])))
    wrt_hi = w_rt.astype(BF16)
    wrt_lo = (w_rt - wrt_hi.astype(F32)).astype(BF16)
    b_rt = jnp.concatenate([b_router_grp[0], b_router_exp[0]])
    b_rt = jnp.pad(b_rt, (0, LANE - b_rt.shape[0])).reshape(1, LANE)
    wo = w_out[0].astype(BF16)
    h1, route = _out_projection(x, y_rwkv, y_lru, row(ln0_g), row(ln0_b), wo[:RWKV_W], wo[RWKV_W:],
                                row(ln1_g[0]), row(ln1_b[0]), wrt_hi, wrt_lo, b_rt)

    M = B * T
    h1 = h1.reshape(M * SLABS, LANE)
    gates, dest, n_slots, tile_expert, n_valid, pad_lo, pad_hi = _routing_plan(route.reshape(M, LANE))
    row_asg = _invert_slots(dest, n_slots, pad_lo, pad_hi)
    ybuf = _moe_experts(h1, row_asg, tile_expert, n_valid, w_exp_gate[0], w_exp_up[0], w_exp_down[0])
    out = _combine(h1, ybuf, dest, gates, row(ln2_g[0]), row(ln2_b[0]))
    return out.reshape(B, T, D)
```

```python
import functools
import math

import jax
import jax.numpy as jnp
from jax import lax
from jax.experimental import pallas as pl
from jax.experimental.pallas import tpu as pltpu

F32 = jnp.float32
BF16 = jnp.bfloat16

D_MODEL = 1024
N_META = 16
RWKV_W = 512
RWKV_HEAD = 64
DECAY_RANK = 64
AAA_RANK = 64
GATE_RANK = 160
LRU_W = 512
LRU_BLOCKS = 8
CONV_WIDTH = 4
LRU_C = 8.0
N_GROUPS = 4
EXPERTS_PER_GROUP = 8
N_EXPERTS = N_GROUPS * EXPERTS_PER_GROUP
TOP_K = 2
D_EXPERT = 512
LN_EPS = 1e-5
GN_EPS = 64e-5
DEEPNORM_ALPHA = 2.0 ** 0.25

LANE = 128
OFF_R, OFF_K, OFF_V = 0, RWKV_W, 2 * RWKV_W
OFF_ZW = 3 * RWKV_W
OFF_ZA = OFF_ZW + LANE
OFF_ZG = OFF_ZA + LANE
ZG_SLOT = 2 * LANE
UR_W = OFF_ZG + ZG_SLOT
UL_W = 2 * LRU_W

TAIL = 128
CHUNK = 64
HEADS_PER_GROUP = 4
GW = HEADS_PER_GROUP * RWKV_HEAD
N_HGROUPS = RWKV_W // GW
LRU_TILE = 128
MOE_TILE = 256
COMBINE_TILE = 256
INVERT_BLOCK = 4096
V7X_VMEM_BYTES = 64 * 1024 * 1024
VMEM_LIMIT = V7X_VMEM_BYTES - 8 * 1024 * 1024


def _cparams(sem, flags=None):
    return pltpu.CompilerParams(dimension_semantics=sem, vmem_limit_bytes=VMEM_LIMIT, flags=flags)


def _layer_norm(x, g, b):
    mu = jnp.mean(x, -1, keepdims=True)
    xc = x - mu
    var = jnp.mean(xc * xc, -1, keepdims=True)
    return xc * lax.rsqrt(var + LN_EPS) * g + b


def _dot(a, b):
    return jnp.dot(a, b, preferred_element_type=F32)


def _dot_hi(a, b):
    return jnp.dot(a, b, precision=lax.Precision.HIGHEST, preferred_element_type=F32)


def _dot_nt(a, b):
    return lax.dot_general(a, b, (((1,), (1,)), ((), ())), preferred_element_type=F32)


def _dot_tn(a, b):
    return lax.dot_general(a, b, (((0,), (0,)), ((), ())), preferred_element_type=F32)


def _split_dot(x, w_bf16):
    hi = x.astype(BF16)
    lo = (x - hi.astype(F32)).astype(BF16)
    return _dot(hi, w_bf16) + _dot(lo, w_bf16)


def _full(a):
    return pl.BlockSpec(a.shape, lambda *_: (0,) * a.ndim)


def _inproj_kernel(x_ref, g_ref, b_ref, wr_ref, wl_ref, ur_ref, ul_ref):
    h = _layer_norm(x_ref[0], g_ref[...], b_ref[...]).astype(BF16)
    ur_ref[0] = _dot(h, wr_ref[...])
    ul_ref[0] = _dot(h, wl_ref[...])


def _inproj_tail_kernel(x_ref, g_ref, b_ref, wr_ref, wl_ref, ur_ref, ul_ref):
    h = _layer_norm(x_ref[...], g_ref[...], b_ref[...]).astype(BF16)
    rows = lax.broadcasted_iota(jnp.int32, (TAIL, 1), 0)
    valid = (rows >= TAIL - N_META).astype(F32)
    ur_ref[...] = _dot(h, wr_ref[...]) * valid
    ul_ref[...] = _dot(h, wl_ref[...]) * valid


def _in_projection(x, meta, ln0_g, ln0_b, w_r, w_l):
    B, T, D = x.shape
    tm = 512
    ur, ul = pl.pallas_call(
        _inproj_kernel,
        out_shape=(jax.ShapeDtypeStruct((B, T, UR_W), F32), jax.ShapeDtypeStruct((B, T, UL_W), F32)),
        grid=(B, T // tm),
        in_specs=[pl.BlockSpec((1, tm, D), lambda b, i: (b, i, 0)), _full(ln0_g), _full(ln0_b), _full(w_r), _full(w_l)],
        out_specs=(pl.BlockSpec((1, tm, UR_W), lambda b, i: (b, i, 0)),
                   pl.BlockSpec((1, tm, UL_W), lambda b, i: (b, i, 0))),
        compiler_params=_cparams(("parallel", "parallel")),
        name="inproj",
    )(x, ln0_g, ln0_b, w_r, w_l)
    tail_x = jnp.concatenate([jnp.zeros((TAIL - N_META, D), F32), meta.astype(F32)], axis=0)
    ur_t, ul_t = pl.pallas_call(
        _inproj_tail_kernel,
        out_shape=(jax.ShapeDtypeStruct((TAIL, UR_W), F32), jax.ShapeDtypeStruct((TAIL, UL_W), F32)),
        grid=(1,),
        in_specs=[_full(tail_x), _full(ln0_g), _full(ln0_b), _full(w_r), _full(w_l)],
        out_specs=(pl.BlockSpec((TAIL, UR_W), lambda i: (0, 0)), pl.BlockSpec((TAIL, UL_W), lambda i: (0, 0))),
        compiler_params=_cparams(("arbitrary",)),
        name="inproj_tail",
    )(tail_x, ln0_g, ln0_b, w_r, w_l)
    return ur, ul, ur_t, ul_t


def _rwkv_kernel(u_ref, ut_ref, mu_ref, w0_ref, wdu_ref, a0_ref, wau_ref, wgu_ref, kk_ref, ka_ref, rk_ref,
                 gng_ref, gnb_ref, bones_ref, tril_ref, eye_ref, bm_ref, msl_ref, mil_ref,
                 m8_ref, m16_ref, m32_ref, m64_ref, y_ref, s_ref, prev_ref):
    c = pl.program_id(0)
    nb = u_ref.shape[0]
    rows_of = lambda b: slice(b * CHUNK, (b + 1) * CHUNK)
    per_batch = lambda f: jnp.concatenate([f(b) for b in range(nb)], axis=0)

    @pl.when(c == 0)
    def _():
        s_ref[...] = jnp.zeros_like(s_ref)
        prev_ref[...] = jnp.zeros_like(prev_ref)

    u_x = u_ref[...].reshape(nb * CHUNK, UR_W)
    u = jnp.where(c == 0, per_batch(lambda b: ut_ref[...]), u_x)
    row = lax.broadcasted_iota(jnp.int32, u.shape, 0)
    prev_rows = per_batch(lambda b: jnp.broadcast_to(prev_ref[b:b + 1, :], (CHUNK, UR_W)))
    u_prev = jnp.where(jnp.bitwise_and(row, CHUNK - 1) == 0, prev_rows, pltpu.roll(u, 1, 0))
    for b in range(nb):
        prev_ref[b:b + 1, :] = u[(b + 1) * CHUNK - 1:(b + 1) * CHUNK, :]
    x = u + (u_prev - u) * mu_ref[...]
    r = x[:, OFF_R:OFF_R + RWKV_W]
    k = x[:, OFF_K:OFF_K + RWKV_W]
    v = x[:, OFF_V:OFF_V + RWKV_W]
    zw = x[:, OFF_ZW:OFF_ZW + LANE]
    za = x[:, OFF_ZA:OFF_ZA + LANE]
    zg = x[:, OFF_ZG:OFF_ZG + ZG_SLOT]

    bones = bones_ref[...]
    head_sum = lambda t: _split_dot(t, bones)

    z = w0_ref[...] + _dot_hi(jnp.tanh(zw), wdu_ref[...])
    logw = -math.exp(-0.5) * jax.nn.sigmoid(z)
    a = jax.nn.sigmoid(a0_ref[...] + _dot_hi(za, wau_ref[...]))
    g = _dot(jax.nn.sigmoid(zg).astype(BF16), wgu_ref[...])
    kk = k * kk_ref[...]
    kk = kk / jnp.maximum(jnp.sqrt(head_sum(kk * kk)), 1e-12)
    k = k * (1.0 + (a - 1.0) * ka_ref[...])
    kka = kk * a

    cl = _dot_hi(tril_ref[...], logw)
    cl_last = per_batch(lambda b: jnp.broadcast_to(cl[(b + 1) * CHUNK - 1:(b + 1) * CHUNK, :], (CHUNK, RWKV_W)))
    e_neg = jnp.exp(-cl)
    e_end = jnp.exp(cl_last - cl)
    rt = r * jnp.exp(cl)
    kt = k * e_neg
    at = -kk * jnp.exp(cl - logw)
    bt = kka * e_neg
    kw = k * e_end
    bw = kka * e_end
    w_end = jnp.exp(cl_last)

    bm = bm_ref[...]
    eye = eye_ref[...]
    msl = msl_ref[...]
    mil = mil_ref[...]
    lane = lax.broadcasted_iota(jnp.int32, (2 * GW, LANE), 1)
    tile4 = lambda t: jnp.concatenate([t] * HEADS_PER_GROUP, axis=0)
    fold4 = lambda t: sum(t[i * CHUNK:(i + 1) * CHUNK] for i in range(HEADS_PER_GROUP))
    wide = lambda t: jnp.concatenate([t, t], axis=1)
    y_rows = []
    for b in range(nb):
        rb = rows_of(b)
        ys = []
        for hg in range(N_HGROUPS):
            sl = slice(hg * GW, (hg + 1) * GW)
            v4 = v[rb, sl]
            lhs = jnp.concatenate([tile4(at[rb, sl]) * bm, tile4(rt[rb, sl]) * bm], axis=0).astype(BF16)
            rhs = jnp.concatenate([bt[rb, sl], kt[rb, sl]], axis=0).astype(BF16)
            aa = _dot_nt(lhs, rhs)
            aa_sw = pltpu.roll(aa, CHUNK, 1)
            vs_b = wide(jnp.where(lane < CHUNK, aa, aa_sw))
            vs_k = wide(jnp.where(lane < CHUNK, aa_sw, aa))
            a_ab = vs_b[:GW] * msl
            a_ak = (vs_k[:GW] * msl).astype(BF16)
            a_rb = (vs_b[GW:] * mil).astype(BF16)
            a_rk = (vs_k[GW:] * mil).astype(BF16)

            a0 = (a_ab * m8_ref[...]).astype(BF16)
            a2 = _dot(a0, a0).astype(BF16)
            a4 = _dot(a2, a2).astype(BF16)
            p1 = eye + a0.astype(F32)
            p1 = p1 + _dot(p1.astype(BF16), a2)
            t = p1 + _dot(p1.astype(BF16), a4)
            for m_ref in (m16_ref, m32_ref, m64_ref):
                tb = t.astype(BF16)
                off = (a_ab * m_ref[...]).astype(BF16)
                t = t + _dot(_dot(tb, off).astype(BF16), tb)
            tb = t.astype(BF16)

            s = s_ref[b, hg]
            sb = s.astype(BF16)
            vt = tile4(v4).astype(BF16)
            xx = _dot_nt(lhs[:GW], sb) + _dot(a_ak, vt)
            uu = _dot(tb, xx.astype(BF16)) * bm
            yy = (_dot_nt(lhs[GW:], sb) + _dot(a_rb, uu.astype(BF16)) + _dot(a_rk, vt)) * bm
            ys.append(fold4(yy))
            u4 = fold4(uu)
            upd = _dot_tn(jnp.concatenate([u4, v4], axis=0).astype(BF16),
                          jnp.concatenate([bw[rb, sl], kw[rb, sl]], axis=0).astype(BF16))
            s_ref[b, hg] = s * w_end[b * CHUNK:b * CHUNK + 1, sl] + upd * bm
        y_rows.append(jnp.concatenate(ys, axis=1))

    y = jnp.concatenate(y_rows, axis=0)
    inv_n = 1.0 / RWKV_HEAD
    ym = head_sum(y) * inv_n
    yc = y - ym
    yv = head_sum(yc * yc) * inv_n
    yn = yc * lax.rsqrt(yv + GN_EPS) * gng_ref[...] + gnb_ref[...]
    bonus = head_sum(r * k * rk_ref[...]) * v
    y_ref[...] = ((yn + bonus) * g).astype(y_ref.dtype).reshape(y_ref.shape)


def _rwkv_masks(nb):
    i = jnp.arange(GW)[:, None]
    j = jnp.arange(GW)[None, :]
    same = lambda n: (i // n) == (j // n)
    f = lambda m: m.astype(F32)
    bm = f(same(RWKV_HEAD))
    msl = f(same(RWKV_HEAD) & (i > j))
    mil = f(same(RWKV_HEAD) & (i >= j))
    m8 = f(same(8))
    m16 = f(same(16) & ~same(8))
    m32 = f(same(32) & ~same(16))
    m64 = f(same(64) & ~same(32))
    eye = f(i == j)
    ti = jnp.arange(nb * CHUNK)
    tril = f((ti[:, None] >= ti[None, :]) & ((ti[:, None] // CHUNK) == (ti[None, :] // CHUNK)))
    hi = jnp.arange(RWKV_W)
    bones = ((hi[:, None] // RWKV_HEAD) == (hi[None, :] // RWKV_HEAD)).astype(BF16)
    return bones, tril, eye, bm, msl, mil, m8, m16, m32, m64


def _rwkv_mixer(ur, ur_tail, params):
    B, T, _ = ur.shape
    consts = _rwkv_masks(B)
    x_map = lambda c: (0, jnp.maximum(c - 1, 0), 0)
    return pl.pallas_call(
        _rwkv_kernel,
        out_shape=jax.ShapeDtypeStruct((B, T, RWKV_W), BF16),
        grid=(T // CHUNK + 1,),
        in_specs=[pl.BlockSpec((B, CHUNK, UR_W), x_map),
                  pl.BlockSpec((CHUNK, UR_W), lambda c: (TAIL // CHUNK - 1, 0))]
                 + [_full(p) for p in params] + [_full(m) for m in consts],
        out_specs=pl.BlockSpec((B, CHUNK, RWKV_W), x_map),
        scratch_shapes=[pltpu.VMEM((B, N_HGROUPS, GW, GW), F32), pltpu.VMEM((B, UR_W), F32)],
        compiler_params=_cparams(("arbitrary",)),
        name="rwkv7",
    )(ur, ur_tail, *params, *consts)


def _rwkv_pipe_mixer(ur, ur_tail, params):
    B, T, _ = ur.shape
    blk = TAIL
    assert T % blk == 0 and blk % CHUNK == 0
    n_blocks = T // blk
    rows = B * blk
    consts = _rwkv_masks(rows // CHUNK)
    in_map = lambda s: (0, jnp.clip(s - 1, 0, n_blocks - 1), 0)
    out_map = lambda s: (0, jnp.clip(s - 2, 0, n_blocks - 1), 0)
    slot2 = lambda w, dt: pltpu.VMEM((2, rows, w), dt)
    return pl.pallas_call(
        _rwkv_pipe_kernel,
        out_shape=jax.ShapeDtypeStruct((B, T, RWKV_W), BF16),
        grid=(n_blocks + 2,),
        in_specs=[pl.BlockSpec((B, blk, UR_W), in_map), _full(ur_tail)]
                 + [_full(p) for p in params] + [_full(m) for m in consts],
        out_specs=pl.BlockSpec((B, blk, RWKV_W), out_map),
        scratch_shapes=[pltpu.VMEM((B, N_HGROUPS, GW, GW), F32), pltpu.VMEM((B, UR_W), F32),
                        slot2(RWKV_W, F32), slot2(RWKV_W, BF16), slot2(RWKV_W, BF16), slot2(RWKV_W, F32),
                        pltpu.VMEM((2, 8 * rows // CHUNK, RWKV_W), F32), slot2(RWKV_W, F32), slot2(RWKV_W, F32),
                        pltpu.VMEM((rows, RWKV_W), F32)],
        compiler_params=_cparams(("arbitrary",)),
        name="rwkv7",
    )(ur, ur_tail, *params, *consts)


def _rwkv_pipe_kernel(u_ref, ut_ref, mu_ref, w0_ref, wdu_ref, a0_ref, wau_ref, wgu_ref, kk_ref, ka_ref, rk_ref,
                      gng_ref, gnb_ref, bones_ref, tril_ref, eye_ref, bm_ref, msl_ref, mil_ref,
                      m8_ref, m16_ref, m32_ref, m64_ref, y_ref,
                      s_ref, prev_ref, y0_s, q_s, mc_s, nc_s, we_s, bonus_s, g_s, yraw_s):
    s_id = pl.program_id(0)
    nb = u_ref.shape[0]
    blk = u_ref.shape[1]
    npc = blk // CHUNK
    nseq = nb * npc
    seq_rows = lambda q: slice(q * CHUNK, (q + 1) * CHUNK)
    per_seq = lambda f: jnp.concatenate([f(q) for q in range(nseq)], axis=0)
    w_slot = lax.rem(s_id, 2)
    r_slot = 1 - w_slot

    @pl.when(s_id == 0)
    def _():
        s_ref[...] = jnp.zeros_like(s_ref)
        prev_ref[...] = jnp.zeros_like(prev_ref)
        for ref in (y0_s, q_s, mc_s, nc_s, we_s, bonus_s, g_s):
            ref[1] = jnp.zeros(ref.shape[1:], ref.dtype)

    u_x = u_ref[...].reshape(nb * blk, UR_W)
    u = jnp.where(s_id == 0, jnp.concatenate([ut_ref[...]] * nb, axis=0), u_x)
    row = lax.broadcasted_iota(jnp.int32, u.shape, 0)
    prev_rows = jnp.concatenate([jnp.broadcast_to(prev_ref[b:b + 1, :], (blk, UR_W)) for b in range(nb)], axis=0)
    u_prev = jnp.where(jnp.bitwise_and(row, blk - 1) == 0, prev_rows, pltpu.roll(u, 1, 0))
    for b in range(nb):
        prev_ref[b:b + 1, :] = u[(b + 1) * blk - 1:(b + 1) * blk, :]
    x = u + (u_prev - u) * mu_ref[...]
    r = x[:, OFF_R:OFF_R + RWKV_W]
    k = x[:, OFF_K:OFF_K + RWKV_W]
    v = x[:, OFF_V:OFF_V + RWKV_W]
    zw = x[:, OFF_ZW:OFF_ZW + LANE]
    za = x[:, OFF_ZA:OFF_ZA + LANE]
    zg = x[:, OFF_ZG:OFF_ZG + ZG_SLOT]

    bones = bones_ref[...]
    head_sum = lambda t: _dot(t.astype(BF16), bones)

    z = w0_ref[...] + _dot(jnp.tanh(zw).astype(BF16), wdu_ref[...])
    logw = -math.exp(-0.5) * jax.nn.sigmoid(z)
    a = jax.nn.sigmoid(a0_ref[...] + _dot(za.astype(BF16), wau_ref[...]))
    g = _dot(jax.nn.sigmoid(zg).astype(BF16), wgu_ref[...])
    kk = k * kk_ref[...]
    kk = kk / jnp.maximum(jnp.sqrt(head_sum(kk * kk)), 1e-12)
    k = k * (1.0 + (a - 1.0) * ka_ref[...])
    kka = kk * a
    bonus_s[w_slot] = head_sum(r * k * rk_ref[...]) * v
    g_s[w_slot] = g

    cl = logw
    row_in_chunk = jnp.bitwise_and(lax.broadcasted_iota(jnp.int32, cl.shape, 0), CHUNK - 1)
    d = 1
    while d < CHUNK:
        cl = cl + jnp.where(row_in_chunk >= d, pltpu.roll(cl, d, 0), 0.0)
        d *= 2
    cl_last = per_seq(lambda q: jnp.broadcast_to(cl[(q + 1) * CHUNK - 1:(q + 1) * CHUNK, :], (CHUNK, RWKV_W)))
    e_neg = jnp.exp(-cl)
    e_end = jnp.exp(cl_last - cl)
    rt = r * jnp.exp(cl)
    kt = k * e_neg
    at = -kk * jnp.exp(cl - logw)
    bt = kka * e_neg
    kw = k * e_end
    bw = kka * e_end
    w_end = jnp.exp(cl_last)

    bm = bm_ref[...]
    bm16 = bm.astype(BF16)
    eye = eye_ref[...]
    msl = msl_ref[...]
    mil = mil_ref[...]
    lane = lax.broadcasted_iota(jnp.int32, (2 * GW, LANE), 1)
    tile4 = lambda t: jnp.concatenate([t] * HEADS_PER_GROUP, axis=0)
    fold4 = lambda t: sum(t[i * CHUNK:(i + 1) * CHUNK] for i in range(HEADS_PER_GROUP))
    wide = lambda t: jnp.concatenate([t, t], axis=1)
    for q in range(nseq):
        we_s[w_slot, q * 8:(q + 1) * 8, :] = w_end[q * CHUNK:q * CHUNK + 8, :]

    probs = [(seq_rows(q), slice(hg * GW, (hg + 1) * GW)) for q in range(nseq) for hg in range(N_HGROUPS)]
    each = lambda f, *ls: [f(*xs) for xs in zip(*ls)]
    b16 = lambda t: t.astype(BF16)
    v4 = [v[rq, sl] for rq, sl in probs]
    r_st = [tile4(rt[rq, sl]) * bm for rq, sl in probs]
    a_st = [b16(tile4(at[rq, sl]) * bm) for rq, sl in probs]
    lhs = each(lambda x, y: jnp.concatenate([x, b16(y)], axis=0), a_st, r_st)
    rhs = [b16(jnp.concatenate([bt[rq, sl], kt[rq, sl]], axis=0)) for rq, sl in probs]
    aa = each(_dot_nt, lhs, rhs)
    aa_sw = each(lambda t: pltpu.roll(t, CHUNK, 1), aa)
    vs_b = each(lambda x, y: wide(jnp.where(lane < CHUNK, x, y)), aa, aa_sw)
    vs_k = each(lambda x, y: wide(jnp.where(lane < CHUNK, y, x)), aa, aa_sw)
    a_ab = each(lambda t: t[:GW] * msl, vs_b)
    a_ak = each(lambda t: b16(t[:GW] * msl), vs_k)
    a_rb = each(lambda t: b16(t[GW:] * mil), vs_b)
    a_rk = each(lambda t: b16(t[GW:] * mil), vs_k)

    a0 = each(lambda t: b16(t * m8_ref[...]), a_ab)
    a2 = each(lambda t: b16(_dot(t, t)), a0)
    a4 = each(lambda t: b16(_dot(t, t)), a2)
    p1 = each(lambda t: eye + t.astype(F32), a0)
    p1 = each(lambda p, t: p + _dot(b16(p), t), p1, a2)
    tt = each(lambda p, t: p + _dot(b16(p), t), p1, a4)
    for m_ref in (m16_ref, m32_ref, m64_ref):
        tb = each(b16, tt)
        off = each(lambda t: b16(t * m_ref[...]), a_ab)
        half = each(lambda x, y: b16(_dot(x, y)), tb, off)
        tt = each(lambda t, x, y: t + _dot(x, y), tt, half, tb)
    tb = each(b16, tt)

    vt = each(lambda t: b16(tile4(t)), v4)
    x0 = each(lambda x, y: b16(_dot(x, y)), a_ak, vt)
    u0 = each(lambda x, y: _dot(x, y) * bm, tb, x0)
    ta = each(_dot, tb, a_st)
    y0 = each(lambda x, y, z, w: (_dot(x, b16(y)) + _dot(z, w)) * bm, a_rb, u0, a_rk, vt)
    qq = each(lambda x, y, z: x + _dot(y, b16(z)), r_st, a_rb, ta)
    left = each(lambda x, y, z: b16(jnp.concatenate([jnp.concatenate([fold4(x), fold4(y)], axis=1),
                                                     jnp.concatenate([jnp.zeros_like(z), z], axis=1)], axis=0)),
                ta, u0, v4)
    right = [b16(jnp.concatenate([bw[rq, sl], kw[rq, sl]], axis=0)) for rq, sl in probs]
    mn = each(_dot_tn, left, right)
    for i, (rq, sl) in enumerate(probs):
        y0_s[w_slot, rq, sl] = fold4(y0[i])
        q_s[w_slot, rq, sl] = b16(fold4(qq[i]))
        mc_s[w_slot, rq, sl] = b16(fold4(mn[i][:GW] * bm))
        nc_s[w_slot, rq, sl] = fold4(mn[i][GW:] * bm)

    for b in range(nb):
        for hg in range(N_HGROUPS):
            sl = slice(hg * GW, (hg + 1) * GW)
            s = s_ref[b, hg]
            for j in range(npc):
                q = b * npc + j
                rq = seq_rows(q)
                sb = s.astype(BF16)
                q_st = tile4(q_s[r_slot, rq, sl]) * bm16
                mc_bd = tile4(mc_s[r_slot, rq, sl]) * bm16
                nc_bd = tile4(nc_s[r_slot, rq, sl]) * bm
                yraw_s[rq, sl] = y0_s[r_slot, rq, sl] + fold4(_dot_nt(q_st, sb))
                s = s * we_s[r_slot, q * 8:q * 8 + 1, sl] + _dot(sb, mc_bd) + nc_bd
            s_ref[b, hg] = s

    y = yraw_s[...]
    inv_n = 1.0 / RWKV_HEAD
    ym = head_sum(y) * inv_n
    yc = y - ym
    yv = head_sum(yc * yc) * inv_n
    yn = yc * lax.rsqrt(yv + GN_EPS) * gng_ref[...] + gnb_ref[...]
    y_ref[...] = ((yn + bonus_s[r_slot]) * g_s[r_slot]).astype(y_ref.dtype).reshape(y_ref.shape)


def _gelu_tanh(x):
    return 0.5 * x * (1.0 + jnp.tanh(math.sqrt(2.0 / math.pi) * (x + 0.044715 * (x * x * x))))


def _lru_kernel(u_ref, ut_ref, cw_ref, cb_ref, wrg_ref, brg_ref, wig_ref, big_ref, lam_ref, y_ref,
                xprev_ref, hprev_ref):
    c = pl.program_id(1)

    @pl.when(c == 0)
    def _():
        xprev_ref[...] = jnp.zeros_like(xprev_ref)
        hprev_ref[...] = jnp.zeros_like(hprev_ref)

    u = jnp.where(c == 0, ut_ref[...], u_ref[0])
    xl = u[:, :LRU_W]
    gl = u[:, LRU_W:]
    row = lax.broadcasted_iota(jnp.int32, (LRU_TILE, LRU_W), 0)
    row8 = lax.broadcasted_iota(jnp.int32, (8, LRU_W), 0)
    xprev = xprev_ref[...]
    xc = cb_ref[...] + cw_ref[CONV_WIDTH - 1:CONV_WIDTH, :] * xl
    for d in range(1, CONV_WIDTH):
        rolled = pltpu.roll(xl, d, 0)
        head = jnp.where(row8 < d, pltpu.roll(xprev, d, 0), rolled[:8])
        shifted = jnp.concatenate([head, rolled[8:]], axis=0)
        xc = xc + cw_ref[CONV_WIDTH - 1 - d:CONV_WIDTH - d, :] * shifted
    xprev_ref[...] = xl[LRU_TILE - 8:, :]

    xcb = xc.astype(BF16)
    gate_r = jax.nn.sigmoid(_dot(xcb, wrg_ref[...]) + brg_ref[...])
    gate_i = jax.nn.sigmoid(_dot(xcb, wig_ref[...]) + big_ref[...])
    lam = lam_ref[...]
    log_sig = -(jnp.maximum(-lam, 0.0) + jnp.log1p(jnp.exp(-jnp.abs(lam))))
    log_a = LRU_C * gate_r * log_sig
    a = jnp.exp(log_a)
    mult = jnp.sqrt(jnp.maximum(1.0 - jnp.exp(2.0 * log_a), 0.0))
    b = mult * gate_i * xc
    b = jnp.where((c == 0) & (row < LRU_TILE - N_META), 0.0, b)

    d = 1
    while d < LRU_TILE:
        keep = row >= d
        a_sh = jnp.where(keep, pltpu.roll(a, d, 0), 1.0)
        b_sh = jnp.where(keep, pltpu.roll(b, d, 0), 0.0)
        b = a * b_sh + b
        a = a * a_sh
        d *= 2
    h = b + a * hprev_ref[...]
    hprev_ref[...] = h[LRU_TILE - 1:, :]
    y_ref[0] = (h * _gelu_tanh(gl)).astype(y_ref.dtype)


def _lru_mixer(ul, ul_tail, params):
    B, T, _ = ul.shape
    assert TAIL == LRU_TILE
    x_map = lambda b, c: (b, jnp.maximum(c - 1, 0), 0)
    return pl.pallas_call(
        _lru_kernel,
        out_shape=jax.ShapeDtypeStruct((B, T, LRU_W), BF16),
        grid=(B, T // LRU_TILE + 1),
        in_specs=[pl.BlockSpec((1, LRU_TILE, UL_W), x_map), _full(ul_tail)] + [_full(p) for p in params],
        out_specs=pl.BlockSpec((1, LRU_TILE, LRU_W), x_map),
        scratch_shapes=[pltpu.VMEM((8, LRU_W), F32), pltpu.VMEM((1, LRU_W), F32)],
        compiler_params=_cparams(("parallel", "arbitrary")),
        name="rglru",
    )(ul, ul_tail, *params)


def _route(lg):
    lane = lax.broadcasted_iota(jnp.int32, lg.shape, 1)
    neg = jnp.float32(-jnp.inf)
    rmax = lambda t: jnp.max(t, axis=1, keepdims=True)
    first = lambda hit: jnp.min(jnp.where(hit, lane, LANE), axis=1, keepdims=True)
    is_grp = lane < N_GROUPS
    gl = jnp.where(is_grp, lg, neg)
    gmax = rmax(gl)
    g_sel = first(gl == gmax)
    p_g = 1.0 / jnp.sum(jnp.where(is_grp, jnp.exp(lg - gmax), 0.0), axis=1, keepdims=True)
    ex = lane - N_GROUPS
    in_grp = (ex >= 0) & (ex < N_EXPERTS) & (jnp.right_shift(ex, 3) == g_sel)
    el = jnp.where(in_grp, lg, neg)
    v1 = rmax(el)
    i1 = first(el == v1)
    el2 = jnp.where(lane == i1, neg, el)
    v2 = rmax(el2)
    i2 = first(el2 == v2)
    t = jnp.exp(v2 - v1)
    gate1 = p_g / (1.0 + t)
    gate2 = p_g * t / (1.0 + t)
    e1 = (i1 - N_GROUPS).astype(F32)
    e2 = (i2 - N_GROUPS).astype(F32)
    return jnp.where(lane == 0, e1, jnp.where(lane == 1, e2, jnp.where(lane == 2, gate1, jnp.where(lane == 3, gate2, 0.0))))


SLABS = D_MODEL // LANE


def _store_token_tiles(ref, val):
    n = val.shape[0]
    for s in range(SLABS):
        ref[pl.ds(s, n, stride=SLABS), :] = val[:, s * LANE:(s + 1) * LANE]


def _load_token_slabs(ref, n):
    return [ref[pl.ds(s, n, stride=SLABS), :] for s in range(SLABS)]


def _outproj_kernel(x_ref, yr_ref, yl_ref, g0_ref, b0_ref, wor_ref, wol_ref, g1_ref, b1_ref,
                    wrt_hi_ref, wrt_lo_ref, brt_ref, h1_ref, rt_ref):
    h0 = _layer_norm(x_ref[0], g0_ref[...], b0_ref[...])
    mix = _dot(yr_ref[0], wor_ref[...]) + _dot(yl_ref[0], wol_ref[...])
    h1 = _layer_norm(DEEPNORM_ALPHA * h0 + mix, g1_ref[...], b1_ref[...])
    _store_token_tiles(h1_ref.at[0], h1)
    hi = h1.astype(BF16)
    lo = (h1 - hi.astype(F32)).astype(BF16)
    w_hi = wrt_hi_ref[...]
    lg = _dot(hi, w_hi) + (_dot(hi, wrt_lo_ref[...]) + _dot(lo, w_hi)) + brt_ref[...]
    rt_ref[0] = _route(lg)


def _out_projection(x, y_rwkv, y_lru, ln0_g, ln0_b, wo_r, wo_l, ln1_g, ln1_b, wrt_hi, wrt_lo, brt):
    B, T, D = x.shape
    tm = 512
    rows = lambda w: pl.BlockSpec((1, tm, w), lambda b, i: (b, i, 0))
    return pl.pallas_call(
        _outproj_kernel,
        out_shape=(jax.ShapeDtypeStruct((B, T * SLABS, LANE), F32), jax.ShapeDtypeStruct((B, T, LANE), F32)),
        grid=(B, T // tm),
        in_specs=[rows(D), rows(RWKV_W), rows(LRU_W), _full(ln0_g), _full(ln0_b), _full(wo_r), _full(wo_l),
                  _full(ln1_g), _full(ln1_b), _full(wrt_hi), _full(wrt_lo), _full(brt)],
        out_specs=(pl.BlockSpec((1, tm * SLABS, LANE), lambda b, i: (b, i, 0)), rows(LANE)),
        compiler_params=_cparams(("parallel", "parallel")),
        name="outproj",
    )(x, y_rwkv, y_lru, ln0_g, ln0_b, wo_r, wo_l, ln1_g, ln1_b, wrt_hi, wrt_lo, brt)


def _invert_kernel(pad_lo_ref, pad_hi_ref, dest_ref, out_ref):
    i = pl.program_id(0)
    n_asg = pl.num_programs(0) * INVERT_BLOCK

    @pl.when(i == 0)
    def _():
        def zero(j, carry):
            out_ref[j] = n_asg
            return carry
        for e in range(N_EXPERTS):
            lax.fori_loop(pad_lo_ref[e], pad_hi_ref[e], zero, 0)
        lax.fori_loop(pad_hi_ref[N_EXPERTS - 1], out_ref.shape[0], zero, 0)

    base = i * INVERT_BLOCK

    def body(jj, carry):
        j0 = jj * DMA_UNROLL
        slots = [dest_ref[j0 + u] for u in range(DMA_UNROLL)]
        for u in range(DMA_UNROLL):
            out_ref[slots[u]] = base + j0 + u
        return carry

    lax.fori_loop(0, INVERT_BLOCK // DMA_UNROLL, body, 0)


def _invert_slots(dest, n_slots, pad_lo, pad_hi):
    A = dest.shape[0]
    grid_spec = pltpu.PrefetchScalarGridSpec(
        num_scalar_prefetch=2,
        grid=(A // INVERT_BLOCK,),
        in_specs=[pl.BlockSpec((INVERT_BLOCK,), lambda i, lo, hi: (i,), memory_space=pltpu.SMEM)],
        out_specs=pl.BlockSpec(memory_space=pltpu.SMEM),
    )
    return pl.pallas_call(
        _invert_kernel,
        out_shape=jax.ShapeDtypeStruct((n_slots,), jnp.int32),
        grid_spec=grid_spec,
        compiler_params=_cparams(("arbitrary",)),
        name="invert_slots",
    )(pad_lo, pad_hi, dest)


def _row_copy(src_hbm, src_row, dst_ref, dst_row, sem):
    return pltpu.make_async_copy(src_hbm.at[pl.ds(src_row * SLABS, SLABS), :],
                                 dst_ref.at[pl.ds(dst_row * SLABS, SLABS), :], sem)


def _wait_tiles(src_hbm, dst_ref, sem):
    pltpu.make_async_copy(src_hbm.at[pl.ds(0, dst_ref.shape[0]), :], dst_ref, sem).wait()


def _wait_tiles_out(src_ref, dst_hbm, sem):
    pltpu.make_async_copy(src_ref, dst_hbm.at[pl.ds(0, src_ref.shape[0]), :], sem).wait()


DMA_UNROLL = 8


DMA_CHUNK = 64


def _moe_kernel(te_ref, src_cur_ref, src_nxt_ref, dst_prv_ref, dst_cur_ref, h_hbm, wg_ref, wu_ref, wd_ref,
                y_hbm, xbuf, obuf, sem_in, sem_out, wgb_ref, wub_ref, wdb_ref, *, pad_tile):
    i = pl.program_id(0)
    n = pl.num_programs(0)
    slot = lax.rem(i, 2)
    other = 1 - slot

    def gather(idx_ref, s, j, prio):
        _row_copy(h_hbm, idx_ref[j], xbuf.at[s], j, sem_in.at[s]).start(priority=prio)

    def scatter(dst_tile, s, j, prio):
        _row_copy(obuf.at[s], j, y_hbm, dst_tile, sem_out.at[s]).start(priority=prio)

    def looped(fn):
        def body(jj, carry):
            for u in range(DMA_UNROLL):
                fn(jj * DMA_UNROLL + u, u % 2)
            return carry
        lax.fori_loop(0, MOE_TILE // DMA_UNROLL, body, 0)

    @pl.when(i == 0)
    def _():
        obuf[...] = jnp.zeros_like(obuf)
        looped(lambda j, p: gather(src_cur_ref, 0, j, p))
        looped(lambda j, p: scatter(pad_tile + j, 0, j, p))

    e = te_ref[i]
    e_prev = te_ref[jnp.maximum(i - 1, 0)]

    @pl.when((i == 0) | (e != e_prev))
    def _():
        wgb_ref[...] = wg_ref[0].astype(BF16)
        wub_ref[...] = wu_ref[0].astype(BF16)
        wdb_ref[...] = wd_ref[0].astype(BF16)

    _wait_tiles(h_hbm, xbuf.at[slot], sem_in.at[slot])
    first = i == 0

    def copies(c):
        for j in range(c * DMA_CHUNK, (c + 1) * DMA_CHUNK):
            gather(src_nxt_ref, other, j, j % 2)
            scatter(jnp.where(first, pad_tile + MOE_TILE + j, dst_prv_ref[j]), other, j, (j + 1) % 2)

    copies(0)
    xb = jnp.concatenate(_load_token_slabs(xbuf.at[slot], MOE_TILE), axis=1).astype(BF16)
    hg = _dot(xb, wgb_ref[...])
    copies(1)
    hu = _dot(xb, wub_ref[...])
    copies(2)
    mid = (hg * jax.nn.sigmoid(hg) * hu).astype(BF16)
    y = _dot(mid, wdb_ref[...])
    copies(3)
    _wait_tiles_out(obuf.at[slot], y_hbm, sem_out.at[slot])
    _store_token_tiles(obuf.at[slot], y)

    @pl.when(i == n - 1)
    def _():
        looped(lambda j, p: scatter(dst_cur_ref[j], slot, j, p))
        _wait_tiles_out(obuf.at[other], y_hbm, sem_out.at[other])
        _wait_tiles_out(obuf.at[slot], y_hbm, sem_out.at[slot])
        _wait_tiles(h_hbm, xbuf.at[other], sem_in.at[other])


def _moe_experts(h1, src_tile, dst_tile, tile_expert, w_gate, w_up, w_down, n_out_tiles, pad_tile):
    D = D_MODEL
    n_tiles = src_tile.shape[0] // MOE_TILE
    smem_tile = lambda f: pl.BlockSpec((MOE_TILE,), f, memory_space=pltpu.SMEM)
    nxt = lambda i, te: (jnp.minimum(i + 1, n_tiles - 1),)
    prv = lambda i, te: (jnp.maximum(i - 1, 0),)
    cur = lambda i, te: (i,)
    grid_spec = pltpu.PrefetchScalarGridSpec(
        num_scalar_prefetch=1,
        grid=(n_tiles,),
        in_specs=[smem_tile(cur), smem_tile(nxt), smem_tile(prv), smem_tile(cur),
                  pl.BlockSpec(memory_space=pl.ANY),
                  pl.BlockSpec((1, D, D_EXPERT), lambda i, te: (te[i], 0, 0)),
                  pl.BlockSpec((1, D, D_EXPERT), lambda i, te: (te[i], 0, 0)),
                  pl.BlockSpec((1, D_EXPERT, D), lambda i, te: (te[i], 0, 0))],
        out_specs=pl.BlockSpec(memory_space=pl.ANY),
        scratch_shapes=[pltpu.VMEM((2, MOE_TILE * SLABS, LANE), F32), pltpu.VMEM((2, MOE_TILE * SLABS, LANE), F32),
                        pltpu.SemaphoreType.DMA((2,)), pltpu.SemaphoreType.DMA((2,)),
                        pltpu.VMEM((D, D_EXPERT), BF16), pltpu.VMEM((D, D_EXPERT), BF16),
                        pltpu.VMEM((D_EXPERT, D), BF16)],
    )
    return pl.pallas_call(
        functools.partial(_moe_kernel, pad_tile=pad_tile),
        out_shape=jax.ShapeDtypeStruct((n_out_tiles * SLABS, LANE), F32),
        grid_spec=grid_spec,
        compiler_params=_cparams(("arbitrary",)),
        name="moe_experts",
    )(tile_expert, src_tile, src_tile, dst_tile, dst_tile, h1, w_gate, w_up, w_down)


def _combine_kernel(h_ref, ya_ref, yb_ref, gate_ref, g_ref, b_ref, o_ref):
    gate = gate_ref[...]
    tm = COMBINE_TILE
    ga = jnp.broadcast_to(gate[:, 0:1], (tm, LANE))
    gb = jnp.broadcast_to(gate[:, 1:2], (tm, LANE))
    hs = _load_token_slabs(h_ref, tm)
    ya = _load_token_slabs(ya_ref, tm)
    yb = _load_token_slabs(yb_ref, tm)
    z = [DEEPNORM_ALPHA * h + (ga * a + gb * b) for h, a, b in zip(hs, ya, yb)]
    inv_d = 1.0 / D_MODEL
    mu = sum(jnp.sum(t, axis=1, keepdims=True) for t in z) * inv_d
    zc = [t - mu for t in z]
    var = sum(jnp.sum(t * t, axis=1, keepdims=True) for t in zc) * inv_d
    rstd = lax.rsqrt(var + LN_EPS)
    for s in range(SLABS):
        cols = slice(s * LANE, (s + 1) * LANE)
        o_ref[:, cols] = zc[s] * rstd * g_ref[:, cols] + b_ref[:, cols]


def _combine(h1, ypair, gates, ln2_g, ln2_b):
    D = D_MODEL
    M = h1.shape[0] // SLABS
    tm = COMBINE_TILE
    n = M // tm
    tiles = lambda f: pl.BlockSpec((tm * SLABS, LANE), f)
    return pl.pallas_call(
        _combine_kernel,
        out_shape=jax.ShapeDtypeStruct((M, D), F32),
        grid=(n,),
        in_specs=[tiles(lambda i: (i, 0)), tiles(lambda i: (i, 0)), tiles(lambda i: (n + i, 0)),
                  pl.BlockSpec((tm, TOP_K), lambda i: (i, 0)), _full(ln2_g), _full(ln2_b)],
        out_specs=pl.BlockSpec((tm, D), lambda i: (i, 0)),
        compiler_params=_cparams(("parallel",)),
        name="combine",
    )(h1, ypair, ypair, gates, ln2_g, ln2_b)


def _routing_plan(route):
    M = route.shape[0]
    eid = route[:, :TOP_K].astype(jnp.int32).reshape(-1)
    gates = route[:, TOP_K:2 * TOP_K]
    A = M * TOP_K
    onehot = (eid[:, None] == jnp.arange(N_EXPERTS, dtype=eid.dtype)[None, :]).astype(jnp.int32)
    csum = jnp.cumsum(onehot, axis=0)
    rank = jnp.sum(csum * onehot, axis=1) - 1
    counts = csum[-1]
    pcounts = (counts + MOE_TILE - 1) // MOE_TILE * MOE_TILE
    pends = jnp.cumsum(pcounts)
    pstarts = pends - pcounts
    dest = (jnp.sum(onehot * pstarts[None, :], axis=1) + rank).astype(jnp.int32)
    n_tiles = (A + N_EXPERTS * (MOE_TILE - 1) + MOE_TILE - 1) // MOE_TILE
    n_valid = (pends[-1] // MOE_TILE).astype(jnp.int32)
    tile_start = jnp.minimum(jnp.arange(n_tiles, dtype=jnp.int32) * MOE_TILE, pends[-1] - 1)
    tile_expert = jnp.sum((pends[None, :] <= tile_start[:, None]).astype(jnp.int32), axis=1)
    tile_expert = jnp.minimum(tile_expert, N_EXPERTS - 1).astype(jnp.int32)
    pad_lo = (pstarts + counts).astype(jnp.int32)
    pad_hi = pends.astype(jnp.int32)
    return gates, dest, n_tiles * MOE_TILE, tile_expert, n_valid.reshape(1), pad_lo, pad_hi


def kernel(x, meta, ln0_g, ln0_b, w_in, mu_shift, w0, w_decay_up, a0, w_a_up, w_g_up, k_k, k_a, r_k, gn_g, gn_b, conv_w, conv_b, w_rg, b_rg, w_ig, b_ig, lru_lambda, w_out, ln1_g, ln1_b, w_router_grp, b_router_grp, w_router_exp, b_router_exp, w_exp_gate, w_exp_up, w_exp_down, ln2_g, ln2_b):
    B, T, D = x.shape
    assert D == D_MODEL and T % 512 == 0 and w_in.shape[0] == 1
    assert (B * T * TOP_K) % INVERT_BLOCK == 0
    row = lambda p: p.reshape(1, -1).astype(F32)
    n_rw = 3 * RWKV_W
    w_in0 = w_in[0]

    def slots(p):
        pad = lambda a, n: jnp.pad(a, [(0, 0)] * (a.ndim - 1) + [(0, n - a.shape[-1])])
        zw = p[..., n_rw:n_rw + DECAY_RANK]
        za = p[..., n_rw + DECAY_RANK:n_rw + DECAY_RANK + AAA_RANK]
        zg = p[..., n_rw + DECAY_RANK + AAA_RANK:n_rw + DECAY_RANK + AAA_RANK + GATE_RANK]
        return jnp.concatenate([p[..., :n_rw], pad(zw, LANE), pad(za, LANE), pad(zg, ZG_SLOT)], axis=-1)

    rwkv_cols = n_rw + DECAY_RANK + AAA_RANK + GATE_RANK
    w_r = slots(w_in0[:, :rwkv_cols]).astype(BF16)
    w_l = w_in0[:, rwkv_cols:].astype(BF16)
    ur, ul, ur_t, ul_t = _in_projection(x, meta, row(ln0_g), row(ln0_b), w_r, w_l)

    pad_rows = lambda a, n: jnp.pad(a, ((0, n - a.shape[0]), (0, 0)))
    rwkv_params = (slots(mu_shift[0][None, :]).astype(F32), row(w0[0]), pad_rows(w_decay_up[0], LANE).astype(BF16),
                   row(a0[0]), pad_rows(w_a_up[0], LANE).astype(BF16), pad_rows(w_g_up[0], ZG_SLOT).astype(BF16),
                   row(k_k[0]), row(k_a[0]), row(r_k[0]), row(gn_g[0]), row(gn_b[0]))
    y_rwkv = _rwkv_pipe_mixer(ur, ur_t, rwkv_params)

    blockdiag = lambda w: jax.scipy.linalg.block_diag(*[w[i] for i in range(LRU_BLOCKS)]).astype(BF16)
    lru_params = (conv_w[0], row(conv_b[0]), blockdiag(w_rg[0]), row(b_rg[0]), blockdiag(w_ig[0]), row(b_ig[0]),
                  row(lru_lambda[0]))
    y_lru = _lru_mixer(ul, ul_t, lru_params)

    w_rt = jnp.concatenate([w_router_grp[0], w_router_exp[0]], axis=1)
    w_rt = jnp.pad(w_rt, ((0, 0), (0, LANE - w_rt.shape[1])))
    wrt_hi = w_rt.astype(BF16)
    wrt_lo = (w_rt - wrt_hi.astype(F32)).astype(BF16)
    b_rt = jnp.concatenate([b_router_grp[0], b_router_exp[0]])
    b_rt = jnp.pad(b_rt, (0, LANE - b_rt.shape[0])).reshape(1, LANE)
    wo = w_out[0].astype(BF16)
    h1, route = _out_projection(x, y_rwkv, y_lru, row(ln0_g), row(ln0_b), wo[:RWKV_W], wo[RWKV_W:],
                                row(ln1_g[0]), row(ln1_b[0]), wrt_hi, wrt_lo, b_rt)

    M = B * T
    h1 = h1.reshape(M * SLABS, LANE)
    gates, dest, n_slots, tile_expert, n_valid, pad_lo, pad_hi = _routing_plan(route.reshape(M, LANE))
    row_asg = _invert_slots(dest, n_slots, pad_lo, pad_hi)
    A = M * TOP_K
    slot_id = jnp.arange(n_slots, dtype=jnp.int32)
    is_pad = row_asg >= A
    src_tile = jnp.minimum(row_asg >> 1, M - 1)
    pad_dst = A + ((slot_id // MOE_TILE) % 2) * MOE_TILE + slot_id % MOE_TILE
    dst_tile = jnp.where(is_pad, pad_dst, (row_asg & 1) * M + (row_asg >> 1))
    ypair = _moe_experts(h1, src_tile, dst_tile, tile_expert, w_exp_gate[0], w_exp_up[0], w_exp_down[0],
                         n_out_tiles=A + 2 * MOE_TILE, pad_tile=A)
    out = _combine(h1, ypair, gates, row(ln2_g[0]), row(ln2_b[0]))
    return out.reshape(B, T, D)
```

```python
import math

import jax
import jax.numpy as jnp
from jax import lax
from jax.experimental import pallas as pl
from jax.experimental.pallas import tpu as pltpu

F32 = jnp.float32
BF16 = jnp.bfloat16

D_MODEL = 1024
N_META = 16
RWKV_W = 512
RWKV_HEAD = 64
DECAY_RANK = 64
AAA_RANK = 64
GATE_RANK = 160
LRU_W = 512
LRU_BLOCKS = 8
CONV_WIDTH = 4
LRU_C = 8.0
N_GROUPS = 4
EXPERTS_PER_GROUP = 8
N_EXPERTS = N_GROUPS * EXPERTS_PER_GROUP
TOP_K = 2
D_EXPERT = 512
LN_EPS = 1e-5
GN_EPS = 64e-5
DEEPNORM_ALPHA = 2.0 ** 0.25

LANE = 128
OFF_R, OFF_K, OFF_V = 0, RWKV_W, 2 * RWKV_W
OFF_ZW = 3 * RWKV_W
OFF_ZA = OFF_ZW + LANE
OFF_ZG = OFF_ZA + LANE
ZG_SLOT = 2 * LANE
UR_W = OFF_ZG + ZG_SLOT
UL_W = 2 * LRU_W

TAIL = 128
CHUNK = 64
HEADS_PER_GROUP = 4
GW = HEADS_PER_GROUP * RWKV_HEAD
N_HGROUPS = RWKV_W // GW
LRU_TILE = 128
MOE_TILE = 256
COMBINE_TILE = 256
INVERT_BLOCK = 4096
DMA_UNROLL = 8
V7X_VMEM_BYTES = 64 * 1024 * 1024
VMEM_LIMIT = V7X_VMEM_BYTES - 8 * 1024 * 1024


def _cparams(sem):
    return pltpu.CompilerParams(dimension_semantics=sem, vmem_limit_bytes=VMEM_LIMIT)


def _layer_norm(x, g, b):
    mu = jnp.mean(x, -1, keepdims=True)
    xc = x - mu
    var = jnp.mean(xc * xc, -1, keepdims=True)
    return xc * lax.rsqrt(var + LN_EPS) * g + b


def _dot(a, b):
    return jnp.dot(a, b, preferred_element_type=F32)


def _dot_nt(a, b):
    return lax.dot_general(a, b, (((1,), (1,)), ((), ())), preferred_element_type=F32)


def _dot_tn(a, b):
    return lax.dot_general(a, b, (((0,), (0,)), ((), ())), preferred_element_type=F32)


def _full(a):
    return pl.BlockSpec(a.shape, lambda *_: (0,) * a.ndim)


def _inproj_kernel(x_ref, g_ref, b_ref, wr_ref, wl_ref, ur_ref, ul_ref):
    h = _layer_norm(x_ref[0], g_ref[...], b_ref[...]).astype(BF16)
    ur_ref[0] = _dot(h, wr_ref[...])
    ul_ref[0] = _dot(h, wl_ref[...])


def _inproj_tail_kernel(x_ref, g_ref, b_ref, wr_ref, wl_ref, ur_ref, ul_ref):
    h = _layer_norm(x_ref[...], g_ref[...], b_ref[...]).astype(BF16)
    rows = lax.broadcasted_iota(jnp.int32, (TAIL, 1), 0)
    valid = (rows >= TAIL - N_META).astype(F32)
    ur_ref[...] = _dot(h, wr_ref[...]) * valid
    ul_ref[...] = _dot(h, wl_ref[...]) * valid


def _in_projection(x, meta, ln0_g, ln0_b, w_r, w_l):
    B, T, D = x.shape
    tm = 512
    ur, ul = pl.pallas_call(
        _inproj_kernel,
        out_shape=(jax.ShapeDtypeStruct((B, T, UR_W), F32), jax.ShapeDtypeStruct((B, T, UL_W), F32)),
        grid=(B, T // tm),
        in_specs=[pl.BlockSpec((1, tm, D), lambda b, i: (b, i, 0)), _full(ln0_g), _full(ln0_b), _full(w_r), _full(w_l)],
        out_specs=(pl.BlockSpec((1, tm, UR_W), lambda b, i: (b, i, 0)),
                   pl.BlockSpec((1, tm, UL_W), lambda b, i: (b, i, 0))),
        compiler_params=_cparams(("parallel", "parallel")),
        name="inproj",
    )(x, ln0_g, ln0_b, w_r, w_l)
    tail_x = jnp.concatenate([jnp.zeros((TAIL - N_META, D), F32), meta.astype(F32)], axis=0)
    ur_t, ul_t = pl.pallas_call(
        _inproj_tail_kernel,
        out_shape=(jax.ShapeDtypeStruct((TAIL, UR_W), F32), jax.ShapeDtypeStruct((TAIL, UL_W), F32)),
        grid=(1,),
        in_specs=[_full(tail_x), _full(ln0_g), _full(ln0_b), _full(w_r), _full(w_l)],
        out_specs=(pl.BlockSpec((TAIL, UR_W), lambda i: (0, 0)), pl.BlockSpec((TAIL, UL_W), lambda i: (0, 0))),
        compiler_params=_cparams(("arbitrary",)),
        name="inproj_tail",
    )(tail_x, ln0_g, ln0_b, w_r, w_l)
    return ur, ul, ur_t, ul_t


def _rwkv_pipe_kernel(u_ref, ut_ref, mu_ref, w0_ref, wdu_ref, a0_ref, wau_ref, wgu_ref, kk_ref, ka_ref, rk_ref,
                      gng_ref, gnb_ref, bones_ref, bm_ref, eye_ref, msl_ref, mil_ref,
                      m8_ref, m16_ref, m32_ref, m64_ref, y_ref,
                      s_ref, prev_ref, y0_s, q_s, mc_s, nc_s, we_s, bonus_s, g_s, yraw_s):
    s_id = pl.program_id(0)
    nb = u_ref.shape[0]
    blk = u_ref.shape[1]
    npc = blk // CHUNK
    nseq = nb * npc
    seq_rows = lambda q: slice(q * CHUNK, (q + 1) * CHUNK)
    per_seq = lambda f: jnp.concatenate([f(q) for q in range(nseq)], axis=0)
    w_slot = lax.rem(s_id, 2)
    r_slot = 1 - w_slot

    @pl.when(s_id == 0)
    def _():
        s_ref[...] = jnp.zeros_like(s_ref)
        prev_ref[...] = jnp.zeros_like(prev_ref)
        for ref in (y0_s, q_s, mc_s, nc_s, we_s, bonus_s, g_s):
            ref[1] = jnp.zeros(ref.shape[1:], ref.dtype)

    u_x = u_ref[...].reshape(nb * blk, UR_W)
    u = jnp.where(s_id == 0, jnp.concatenate([ut_ref[...]] * nb, axis=0), u_x)
    row = lax.broadcasted_iota(jnp.int32, u.shape, 0)
    prev_rows = jnp.concatenate([jnp.broadcast_to(prev_ref[b:b + 1, :], (blk, UR_W)) for b in range(nb)], axis=0)
    u_prev = jnp.where(jnp.bitwise_and(row, blk - 1) == 0, prev_rows, pltpu.roll(u, 1, 0))
    for b in range(nb):
        prev_ref[b:b + 1, :] = u[(b + 1) * blk - 1:(b + 1) * blk, :]
    x = u + (u_prev - u) * mu_ref[...]
    r = x[:, OFF_R:OFF_R + RWKV_W]
    k = x[:, OFF_K:OFF_K + RWKV_W]
    v = x[:, OFF_V:OFF_V + RWKV_W]
    zw = x[:, OFF_ZW:OFF_ZW + LANE]
    za = x[:, OFF_ZA:OFF_ZA + LANE]
    zg = x[:, OFF_ZG:OFF_ZG + ZG_SLOT]

    b16 = lambda t: t.astype(BF16)
    bones = bones_ref[...]
    head_sum = lambda t: _dot(b16(t), bones)

    z = w0_ref[...] + _dot(b16(jnp.tanh(zw)), wdu_ref[...])
    logw = -math.exp(-0.5) * jax.nn.sigmoid(z)
    a = jax.nn.sigmoid(a0_ref[...] + _dot(b16(za), wau_ref[...]))
    g = _dot(b16(jax.nn.sigmoid(zg)), wgu_ref[...])
    kk = k * kk_ref[...]
    kk = kk / jnp.maximum(jnp.sqrt(head_sum(kk * kk)), 1e-12)
    k = k * (1.0 + (a - 1.0) * ka_ref[...])
    kka = kk * a
    bonus_s[w_slot] = head_sum(r * k * rk_ref[...]) * v
    g_s[w_slot] = g

    cl = logw
    row_in_chunk = jnp.bitwise_and(lax.broadcasted_iota(jnp.int32, cl.shape, 0), CHUNK - 1)
    d = 1
    while d < CHUNK:
        cl = cl + jnp.where(row_in_chunk >= d, pltpu.roll(cl, d, 0), 0.0)
        d *= 2
    cl_last = per_seq(lambda q: jnp.broadcast_to(cl[(q + 1) * CHUNK - 1:(q + 1) * CHUNK, :], (CHUNK, RWKV_W)))
    e_neg = jnp.exp(-cl)
    e_end = jnp.exp(cl_last - cl)
    rt = r * jnp.exp(cl)
    kt = k * e_neg
    at = -kk * jnp.exp(cl - logw)
    bt = kka * e_neg
    kw = k * e_end
    bw = kka * e_end
    w_end = jnp.exp(cl_last)
    for q in range(nseq):
        we_s[w_slot, q * 8:(q + 1) * 8, :] = w_end[q * CHUNK:q * CHUNK + 8, :]

    bm = bm_ref[...]
    bm16 = b16(bm)
    eye = eye_ref[...]
    msl = msl_ref[...]
    mil = mil_ref[...]
    tile4 = lambda t: jnp.concatenate([t] * HEADS_PER_GROUP, axis=0)
    fold4 = lambda t: sum(t[i * CHUNK:(i + 1) * CHUNK] for i in range(HEADS_PER_GROUP))
    bd = lambda t: tile4(b16(t)) * bm16
    rows2 = lambda x, y: jnp.concatenate([x, y], axis=0)
    cols2 = lambda x, y: jnp.concatenate([x, y], axis=1)

    probs = [(seq_rows(q), slice(hg * GW, (hg + 1) * GW)) for q in range(nseq) for hg in range(N_HGROUPS)]
    each = lambda f, *ls: [f(*xs) for xs in zip(*ls)]
    pick = lambda t: [t[rq, sl] for rq, sl in probs]
    at_w, rt_w, v_w = pick(at), pick(rt), pick(v)
    lhs = each(lambda x, y: b16(rows2(x, y)), at_w, rt_w)
    ab = each(_dot_nt, lhs, each(bd, pick(bt)))
    ak = each(_dot_nt, lhs, each(bd, pick(kt)))
    a_ab = each(lambda t: t[:CHUNK] * msl, ab)
    a_rb = each(lambda t: b16(t[CHUNK:] * mil), ab)
    a_xk = each(lambda t: b16(rows2(t[:CHUNK] * msl, t[CHUNK:] * mil)), ak)

    a0 = each(lambda t: b16(t * m8_ref[...]), a_ab)
    a2 = each(lambda t: b16(_dot(t, bd(t))), a0)
    a4 = each(lambda t: b16(_dot(t, bd(t))), a2)
    p1 = each(lambda t: eye + t.astype(F32), a0)
    p1 = each(lambda p, t: p + _dot(b16(p), bd(t)), p1, a2)
    tt = each(lambda p, t: p + _dot(b16(p), bd(t)), p1, a4)
    for m_ref in (m16_ref, m32_ref, m64_ref):
        tb = each(b16, tt)
        off = each(lambda t: b16(t * m_ref[...]), a_ab)
        half = each(lambda x, y: b16(_dot(x, bd(y))), tb, off)
        tt = each(lambda t, x, y: t + _dot(x, bd(y)), tt, half, tb)
    tb = each(b16, tt)

    xv = each(lambda x, y: _dot(x, bd(y)), a_xk, v_w)
    u0 = each(lambda x, y: _dot(x, bd(y[:CHUNK])), tb, xv)
    ta = each(lambda x, y: _dot(x, bd(y)), tb, at_w)
    y0 = each(lambda x, y, z: _dot(x, bd(y)) + z[CHUNK:], a_rb, u0, xv)
    qq = each(lambda x, y, z: x + _dot(y, bd(z)), rt_w, a_rb, ta)
    left = each(lambda x, y, z: b16(rows2(cols2(x, y), cols2(jnp.zeros_like(z), z))), ta, u0, v_w)
    right = each(lambda x, y: b16(rows2(x, y)), pick(bw), pick(kw))
    mn = each(_dot_tn, left, right)
    for i, (rq, sl) in enumerate(probs):
        y0_s[w_slot, rq, sl] = y0[i]
        q_s[w_slot, rq, sl] = b16(qq[i])
        mc_s[w_slot, rq, sl] = b16(fold4(mn[i][:GW] * bm))
        nc_s[w_slot, rq, sl] = fold4(mn[i][GW:] * bm)

    for b in range(nb):
        for hg in range(N_HGROUPS):
            sl = slice(hg * GW, (hg + 1) * GW)
            s = s_ref[b, hg]
            for j in range(npc):
                q = b * npc + j
                rq = seq_rows(q)
                yraw_s[rq, sl] = y0_s[r_slot, rq, sl] + _dot_nt(q_s[r_slot, rq, sl], bd(s))
                s = (s * we_s[r_slot, q * 8:q * 8 + 1, sl] + _dot(b16(s), bd(mc_s[r_slot, rq, sl]))
                     + nc_s[r_slot, rq, sl])
            s_ref[b, hg] = s

    y = yraw_s[...]
    inv_n = 1.0 / RWKV_HEAD
    ym = head_sum(y) * inv_n
    yc = y - ym
    yv = head_sum(yc * yc) * inv_n
    yn = yc * lax.rsqrt(yv + GN_EPS) * gng_ref[...] + gnb_ref[...]
    y_ref[...] = ((yn + bonus_s[r_slot]) * g_s[r_slot]).astype(y_ref.dtype).reshape(y_ref.shape)


def _rwkv_masks():
    f = lambda m: m.astype(F32)
    i = jnp.arange(GW)[:, None]
    j = jnp.arange(GW)[None, :]
    bm = f((i // RWKV_HEAD) == (j // RWKV_HEAD))
    t = jnp.arange(CHUNK)[:, None]
    s = (jnp.arange(GW) % CHUNK)[None, :]
    same = lambda n: (t // n) == (s // n)
    msl = f(t > s)
    mil = f(t >= s)
    m8 = f(same(8))
    m16 = f(same(16) & ~same(8))
    m32 = f(same(32) & ~same(16))
    m64 = f(~same(32))
    eye = f(t == s)
    hi = jnp.arange(RWKV_W)
    bones = ((hi[:, None] // RWKV_HEAD) == (hi[None, :] // RWKV_HEAD)).astype(BF16)
    return bones, bm, eye, msl, mil, m8, m16, m32, m64


def _rwkv_pipe_mixer(ur, ur_tail, params):
    B, T, _ = ur.shape
    blk = TAIL
    assert T % blk == 0 and blk % CHUNK == 0 and CHUNK == RWKV_HEAD
    n_blocks = T // blk
    rows = B * blk
    consts = _rwkv_masks()
    in_map = lambda s: (0, jnp.clip(s - 1, 0, n_blocks - 1), 0)
    out_map = lambda s: (0, jnp.clip(s - 2, 0, n_blocks - 1), 0)
    slot2 = lambda w, dt: pltpu.VMEM((2, rows, w), dt)
    return pl.pallas_call(
        _rwkv_pipe_kernel,
        out_shape=jax.ShapeDtypeStruct((B, T, RWKV_W), BF16),
        grid=(n_blocks + 2,),
        in_specs=[pl.BlockSpec((B, blk, UR_W), in_map), _full(ur_tail)]
                 + [_full(p) for p in params] + [_full(m) for m in consts],
        out_specs=pl.BlockSpec((B, blk, RWKV_W), out_map),
        scratch_shapes=[pltpu.VMEM((B, N_HGROUPS, CHUNK, GW), F32), pltpu.VMEM((B, UR_W), F32),
                        slot2(RWKV_W, F32), slot2(RWKV_W, BF16), slot2(RWKV_W, BF16), slot2(RWKV_W, F32),
                        pltpu.VMEM((2, 8 * rows // CHUNK, RWKV_W), F32), slot2(RWKV_W, F32), slot2(RWKV_W, F32),
                        pltpu.VMEM((rows, RWKV_W), F32)],
        compiler_params=_cparams(("arbitrary",)),
        name="rwkv7",
    )(ur, ur_tail, *params, *consts)


def _gelu_tanh(x):
    return 0.5 * x * (1.0 + jnp.tanh(math.sqrt(2.0 / math.pi) * (x + 0.044715 * (x * x * x))))


def _lru_kernel(u_ref, ut_ref, cw_ref, cb_ref, wrg_ref, brg_ref, wig_ref, big_ref, lam_ref, y_ref,
                xprev_ref, hprev_ref):
    c = pl.program_id(1)

    @pl.when(c == 0)
    def _():
        xprev_ref[...] = jnp.zeros_like(xprev_ref)
        hprev_ref[...] = jnp.zeros_like(hprev_ref)

    u = jnp.where(c == 0, ut_ref[...], u_ref[0])
    xl = u[:, :LRU_W]
    gl = u[:, LRU_W:]
    row = lax.broadcasted_iota(jnp.int32, (LRU_TILE, LRU_W), 0)
    row8 = lax.broadcasted_iota(jnp.int32, (8, LRU_W), 0)
    xprev = xprev_ref[...]
    xc = cb_ref[...] + cw_ref[CONV_WIDTH - 1:CONV_WIDTH, :] * xl
    for d in range(1, CONV_WIDTH):
        rolled = pltpu.roll(xl, d, 0)
        head = jnp.where(row8 < d, pltpu.roll(xprev, d, 0), rolled[:8])
        shifted = jnp.concatenate([head, rolled[8:]], axis=0)
        xc = xc + cw_ref[CONV_WIDTH - 1 - d:CONV_WIDTH - d, :] * shifted
    xprev_ref[...] = xl[LRU_TILE - 8:, :]

    xcb = xc.astype(BF16)
    gate_r = jax.nn.sigmoid(_dot(xcb, wrg_ref[...]) + brg_ref[...])
    gate_i = jax.nn.sigmoid(_dot(xcb, wig_ref[...]) + big_ref[...])
    lam = lam_ref[...]
    log_sig = -(jnp.maximum(-lam, 0.0) + jnp.log1p(jnp.exp(-jnp.abs(lam))))
    log_a = LRU_C * gate_r * log_sig
    a = jnp.exp(log_a)
    mult = jnp.sqrt(jnp.maximum(1.0 - jnp.exp(2.0 * log_a), 0.0))
    b = mult * gate_i * xc
    b = jnp.where((c == 0) & (row < LRU_TILE - N_META), 0.0, b)

    d = 1
    while d < LRU_TILE:
        keep = row >= d
        a_sh = jnp.where(keep, pltpu.roll(a, d, 0), 1.0)
        b_sh = jnp.where(keep, pltpu.roll(b, d, 0), 0.0)
        b = a * b_sh + b
        a = a * a_sh
        d *= 2
    h = b + a * hprev_ref[...]
    hprev_ref[...] = h[LRU_TILE - 1:, :]
    y_ref[0] = (h * _gelu_tanh(gl)).astype(y_ref.dtype)


def _lru_mixer(ul, ul_tail, params):
    B, T, _ = ul.shape
    assert TAIL == LRU_TILE
    x_map = lambda b, c: (b, jnp.maximum(c - 1, 0), 0)
    return pl.pallas_call(
        _lru_kernel,
        out_shape=jax.ShapeDtypeStruct((B, T, LRU_W), BF16),
        grid=(B, T // LRU_TILE + 1),
        in_specs=[pl.BlockSpec((1, LRU_TILE, UL_W), x_map), _full(ul_tail)] + [_full(p) for p in params],
        out_specs=pl.BlockSpec((1, LRU_TILE, LRU_W), x_map),
        scratch_shapes=[pltpu.VMEM((8, LRU_W), F32), pltpu.VMEM((1, LRU_W), F32)],
        compiler_params=_cparams(("parallel", "arbitrary")),
        name="rglru",
    )(ul, ul_tail, *params)


def _route(lg):
    lane = lax.broadcasted_iota(jnp.int32, lg.shape, 1)
    neg = jnp.float32(-jnp.inf)
    rmax = lambda t: jnp.max(t, axis=1, keepdims=True)
    first = lambda hit: jnp.min(jnp.where(hit, lane, LANE), axis=1, keepdims=True)
    is_grp = lane < N_GROUPS
    gl = jnp.where(is_grp, lg, neg)
    gmax = rmax(gl)
    g_sel = first(gl == gmax)
    p_g = 1.0 / jnp.sum(jnp.where(is_grp, jnp.exp(lg - gmax), 0.0), axis=1, keepdims=True)
    ex = lane - N_GROUPS
    in_grp = (ex >= 0) & (ex < N_EXPERTS) & (jnp.right_shift(ex, 3) == g_sel)
    el = jnp.where(in_grp, lg, neg)
    v1 = rmax(el)
    i1 = first(el == v1)
    el2 = jnp.where(lane == i1, neg, el)
    v2 = rmax(el2)
    i2 = first(el2 == v2)
    t = jnp.exp(v2 - v1)
    gate1 = p_g / (1.0 + t)
    gate2 = p_g * t / (1.0 + t)
    e1 = (i1 - N_GROUPS).astype(F32)
    e2 = (i2 - N_GROUPS).astype(F32)
    return jnp.where(lane == 0, e1, jnp.where(lane == 1, e2, jnp.where(lane == 2, gate1, jnp.where(lane == 3, gate2, 0.0))))


SLABS = D_MODEL // LANE


def _store_token_tiles(ref, val):
    n = val.shape[0]
    for s in range(SLABS):
        ref[pl.ds(s, n, stride=SLABS), :] = val[:, s * LANE:(s + 1) * LANE]


def _load_token_slabs(ref, n):
    return [ref[pl.ds(s, n, stride=SLABS), :] for s in range(SLABS)]


def _outproj_kernel(x_ref, yr_ref, yl_ref, g0_ref, b0_ref, wor_ref, wol_ref, g1_ref, b1_ref,
                    wrt_hi_ref, wrt_lo_ref, brt_ref, h1_ref, rt_ref):
    h0 = _layer_norm(x_ref[0], g0_ref[...], b0_ref[...])
    mix = _dot(yr_ref[0], wor_ref[...]) + _dot(yl_ref[0], wol_ref[...])
    h1 = _layer_norm(DEEPNORM_ALPHA * h0 + mix, g1_ref[...], b1_ref[...])
    _store_token_tiles(h1_ref.at[0], h1)
    hi = h1.astype(BF16)
    lo = (h1 - hi.astype(F32)).astype(BF16)
    w_hi = wrt_hi_ref[...]
    lg = _dot(hi, w_hi) + (_dot(hi, wrt_lo_ref[...]) + _dot(lo, w_hi)) + brt_ref[...]
    rt_ref[0] = _route(lg)


def _out_projection(x, y_rwkv, y_lru, ln0_g, ln0_b, wo_r, wo_l, ln1_g, ln1_b, wrt_hi, wrt_lo, brt):
    B, T, D = x.shape
    tm = 512
    rows = lambda w: pl.BlockSpec((1, tm, w), lambda b, i: (b, i, 0))
    return pl.pallas_call(
        _outproj_kernel,
        out_shape=(jax.ShapeDtypeStruct((B, T * SLABS, LANE), F32), jax.ShapeDtypeStruct((B, T, LANE), F32)),
        grid=(B, T // tm),
        in_specs=[rows(D), rows(RWKV_W), rows(LRU_W), _full(ln0_g), _full(ln0_b), _full(wo_r), _full(wo_l),
                  _full(ln1_g), _full(ln1_b), _full(wrt_hi), _full(wrt_lo), _full(brt)],
        out_specs=(pl.BlockSpec((1, tm * SLABS, LANE), lambda b, i: (b, i, 0)), rows(LANE)),
        compiler_params=_cparams(("parallel", "parallel")),
        name="outproj",
    )(x, y_rwkv, y_lru, ln0_g, ln0_b, wo_r, wo_l, ln1_g, ln1_b, wrt_hi, wrt_lo, brt)


def _invert_kernel(pad_lo_ref, pad_hi_ref, dest_ref, out_ref):
    i = pl.program_id(0)

    @pl.when(i == 0)
    def _():
        def zero(j, carry):
            out_ref[j] = 0
            return carry
        for e in range(N_EXPERTS):
            lax.fori_loop(pad_lo_ref[e], pad_hi_ref[e], zero, 0)
        lax.fori_loop(pad_hi_ref[N_EXPERTS - 1], out_ref.shape[0], zero, 0)

    base = i * INVERT_BLOCK

    def body(j, carry):
        out_ref[dest_ref[j]] = base + j
        return carry

    lax.fori_loop(0, INVERT_BLOCK, body, 0, unroll=8)


def _invert_slots(dest, n_slots, pad_lo, pad_hi):
    A = dest.shape[0]
    grid_spec = pltpu.PrefetchScalarGridSpec(
        num_scalar_prefetch=2,
        grid=(A // INVERT_BLOCK,),
        in_specs=[pl.BlockSpec((INVERT_BLOCK,), lambda i, lo, hi: (i,), memory_space=pltpu.SMEM)],
        out_specs=pl.BlockSpec(memory_space=pltpu.SMEM),
    )
    return pl.pallas_call(
        _invert_kernel,
        out_shape=jax.ShapeDtypeStruct((n_slots,), jnp.int32),
        grid_spec=grid_spec,
        compiler_params=_cparams(("arbitrary",)),
        name="invert_slots",
    )(pad_lo, pad_hi, dest)


def _row_copy(src_hbm, src_row, dst_ref, dst_row, sem):
    return pltpu.make_async_copy(src_hbm.at[pl.ds(src_row * SLABS, SLABS), :],
                                 dst_ref.at[pl.ds(dst_row * SLABS, SLABS), :], sem)


def _wait_tiles(src_hbm, dst_ref, sem):
    pltpu.make_async_copy(src_hbm.at[pl.ds(0, dst_ref.shape[0]), :], dst_ref, sem).wait()


def _moe_kernel(te_ref, nv_ref, ra_cur_ref, ra_nxt_ref, h_hbm, wg_ref, wu_ref, wd_ref, o_ref,
                xbuf, sem, wgb_ref, wub_ref, wdb_ref):
    i = pl.program_id(0)
    n_valid = nv_ref[0]
    slot = lax.rem(i, 2)

    def start_gather(ra_ref, s):
        def body(jj, carry):
            for u in range(DMA_UNROLL):
                j = jj * DMA_UNROLL + u
                tok = lax.shift_right_logical(ra_ref[j], 1)
                _row_copy(h_hbm, tok, xbuf.at[s], j, sem.at[s]).start(priority=u % 2)
            return carry
        lax.fori_loop(0, MOE_TILE // DMA_UNROLL, body, 0)

    @pl.when(i == 0)
    def _():
        start_gather(ra_cur_ref, 0)

    @pl.when(i + 1 < n_valid)
    def _():
        start_gather(ra_nxt_ref, 1 - slot)

    e = te_ref[i]
    e_prev = te_ref[jnp.maximum(i - 1, 0)]

    @pl.when((i == 0) | (e != e_prev))
    def _():
        wgb_ref[...] = wg_ref[0].astype(BF16)
        wub_ref[...] = wu_ref[0].astype(BF16)
        wdb_ref[...] = wd_ref[0].astype(BF16)

    @pl.when(i < n_valid)
    def _():
        _wait_tiles(h_hbm, xbuf.at[slot], sem.at[slot])
        xb = jnp.concatenate(_load_token_slabs(xbuf.at[slot], MOE_TILE), axis=1).astype(BF16)
        hg = _dot(xb, wgb_ref[...])
        hu = _dot(xb, wub_ref[...])
        mid = (hg * jax.nn.sigmoid(hg) * hu).astype(BF16)
        _store_token_tiles(o_ref, _dot(mid, wdb_ref[...]))

    @pl.when(i >= n_valid)
    def _():
        o_ref[...] = jnp.zeros_like(o_ref)


def _moe_experts(h1, row_asg, tile_expert, n_valid, w_gate, w_up, w_down):
    D = D_MODEL
    n_tiles = row_asg.shape[0] // MOE_TILE
    smem_tile = lambda f: pl.BlockSpec((MOE_TILE,), f, memory_space=pltpu.SMEM)
    grid_spec = pltpu.PrefetchScalarGridSpec(
        num_scalar_prefetch=2,
        grid=(n_tiles,),
        in_specs=[smem_tile(lambda i, te, nv: (i,)),
                  smem_tile(lambda i, te, nv: (jnp.minimum(i + 1, n_tiles - 1),)),
                  pl.BlockSpec(memory_space=pl.ANY),
                  pl.BlockSpec((1, D, D_EXPERT), lambda i, te, nv: (te[i], 0, 0)),
                  pl.BlockSpec((1, D, D_EXPERT), lambda i, te, nv: (te[i], 0, 0)),
                  pl.BlockSpec((1, D_EXPERT, D), lambda i, te, nv: (te[i], 0, 0))],
        out_specs=pl.BlockSpec((MOE_TILE * SLABS, LANE), lambda i, te, nv: (i, 0)),
        scratch_shapes=[pltpu.VMEM((2, MOE_TILE * SLABS, LANE), F32), pltpu.SemaphoreType.DMA((2,)),
                        pltpu.VMEM((D, D_EXPERT), BF16), pltpu.VMEM((D, D_EXPERT), BF16),
                        pltpu.VMEM((D_EXPERT, D), BF16)],
    )
    return pl.pallas_call(
        _moe_kernel,
        out_shape=jax.ShapeDtypeStruct((n_tiles * MOE_TILE * SLABS, LANE), F32),
        grid_spec=grid_spec,
        compiler_params=_cparams(("arbitrary",)),
        name="moe_experts",
    )(tile_expert, n_valid, row_asg, row_asg, h1, w_gate, w_up, w_down)


def _combine_kernel(d_cur_ref, d_nxt_ref, h_ref, gate_ref, g_ref, b_ref, y_hbm, o_ref, ybuf, sem):
    i = pl.program_id(0)
    n = pl.num_programs(0)
    slot = lax.rem(i, 2)

    def start_gather(d_ref, s):
        def body(tt, carry):
            for u in range(DMA_UNROLL // TOP_K):
                t = tt * (DMA_UNROLL // TOP_K) + u
                for k in range(TOP_K):
                    _row_copy(y_hbm, d_ref[TOP_K * t + k], ybuf.at[s, k], t, sem.at[s]).start(priority=k % 2)
            return carry
        lax.fori_loop(0, COMBINE_TILE * TOP_K // DMA_UNROLL, body, 0)

    @pl.when(i == 0)
    def _():
        start_gather(d_cur_ref, 0)

    @pl.when(i + 1 < n)
    def _():
        start_gather(d_nxt_ref, 1 - slot)

    for k in range(TOP_K):
        _wait_tiles(y_hbm, ybuf.at[slot, k], sem.at[slot])

    gate = gate_ref[...]
    tm = COMBINE_TILE
    ga = jnp.broadcast_to(gate[:, 0:1], (tm, LANE))
    gb = jnp.broadcast_to(gate[:, 1:2], (tm, LANE))
    hs = _load_token_slabs(h_ref, tm)
    ya = _load_token_slabs(ybuf.at[slot, 0], tm)
    yb = _load_token_slabs(ybuf.at[slot, 1], tm)
    z = [DEEPNORM_ALPHA * h + (ga * a + gb * b) for h, a, b in zip(hs, ya, yb)]
    inv_d = 1.0 / D_MODEL
    mu = sum(jnp.sum(t, axis=1, keepdims=True) for t in z) * inv_d
    zc = [t - mu for t in z]
    var = sum(jnp.sum(t * t, axis=1, keepdims=True) for t in zc) * inv_d
    rstd = lax.rsqrt(var + LN_EPS)
    for s in range(SLABS):
        cols = slice(s * LANE, (s + 1) * LANE)
        o_ref[:, cols] = zc[s] * rstd * g_ref[:, cols] + b_ref[:, cols]


def _combine(h1, ybuf, dest, gates, ln2_g, ln2_b):
    D = D_MODEL
    M = h1.shape[0] // SLABS
    tm = COMBINE_TILE
    n = M // tm
    smem_tile = lambda f: pl.BlockSpec((TOP_K * tm,), f, memory_space=pltpu.SMEM)
    return pl.pallas_call(
        _combine_kernel,
        out_shape=jax.ShapeDtypeStruct((M, D), F32),
        grid=(n,),
        in_specs=[smem_tile(lambda i: (i,)), smem_tile(lambda i: (jnp.minimum(i + 1, n - 1),)),
                  pl.BlockSpec((tm * SLABS, LANE), lambda i: (i, 0)), pl.BlockSpec((tm, TOP_K), lambda i: (i, 0)),
                  _full(ln2_g), _full(ln2_b), pl.BlockSpec(memory_space=pl.ANY)],
        out_specs=pl.BlockSpec((tm, D), lambda i: (i, 0)),
        scratch_shapes=[pltpu.VMEM((2, TOP_K, tm * SLABS, LANE), F32), pltpu.SemaphoreType.DMA((2,))],
        compiler_params=_cparams(("arbitrary",)),
        name="combine",
    )(dest, dest, h1, gates, ln2_g, ln2_b, ybuf)


def _routing_plan(route):
    M = route.shape[0]
    eid = route[:, :TOP_K].astype(jnp.int32).reshape(-1)
    gates = route[:, TOP_K:2 * TOP_K]
    A = M * TOP_K
    onehot = (eid[:, None] == jnp.arange(N_EXPERTS, dtype=eid.dtype)[None, :]).astype(jnp.int32)
    csum = jnp.cumsum(onehot, axis=0)
    rank = jnp.sum(csum * onehot, axis=1) - 1
    counts = csum[-1]
    pcounts = (counts + MOE_TILE - 1) // MOE_TILE * MOE_TILE
    pends = jnp.cumsum(pcounts)
    pstarts = pends - pcounts
    dest = (jnp.sum(onehot * pstarts[None, :], axis=1) + rank).astype(jnp.int32)
    n_tiles = (A + N_EXPERTS * (MOE_TILE - 1) + MOE_TILE - 1) // MOE_TILE
    n_valid = (pends[-1] // MOE_TILE).astype(jnp.int32)
    tile_start = jnp.minimum(jnp.arange(n_tiles, dtype=jnp.int32) * MOE_TILE, pends[-1] - 1)
    tile_expert = jnp.sum((pends[None, :] <= tile_start[:, None]).astype(jnp.int32), axis=1)
    tile_expert = jnp.minimum(tile_expert, N_EXPERTS - 1).astype(jnp.int32)
    pad_lo = (pstarts + counts).astype(jnp.int32)
    pad_hi = pends.astype(jnp.int32)
    return gates, dest, n_tiles * MOE_TILE, tile_expert, n_valid.reshape(1), pad_lo, pad_hi


def kernel(x, meta, ln0_g, ln0_b, w_in, mu_shift, w0, w_decay_up, a0, w_a_up, w_g_up, k_k, k_a, r_k, gn_g, gn_b, conv_w, conv_b, w_rg, b_rg, w_ig, b_ig, lru_lambda, w_out, ln1_g, ln1_b, w_router_grp, b_router_grp, w_router_exp, b_router_exp, w_exp_gate, w_exp_up, w_exp_down, ln2_g, ln2_b):
    B, T, D = x.shape
    assert D == D_MODEL and T % 512 == 0 and w_in.shape[0] == 1
    assert (B * T * TOP_K) % INVERT_BLOCK == 0
    row = lambda p: p.reshape(1, -1).astype(F32)
    n_rw = 3 * RWKV_W
    w_in0 = w_in[0]

    def slots(p):
        pad = lambda a, n: jnp.pad(a, [(0, 0)] * (a.ndim - 1) + [(0, n - a.shape[-1])])
        zw = p[..., n_rw:n_rw + DECAY_RANK]
        za = p[..., n_rw + DECAY_RANK:n_rw + DECAY_RANK + AAA_RANK]
        zg = p[..., n_rw + DECAY_RANK + AAA_RANK:n_rw + DECAY_RANK + AAA_RANK + GATE_RANK]
        return jnp.concatenate([p[..., :n_rw], pad(zw, LANE), pad(za, LANE), pad(zg, ZG_SLOT)], axis=-1)

    rwkv_cols = n_rw + DECAY_RANK + AAA_RANK + GATE_RANK
    w_r = slots(w_in0[:, :rwkv_cols]).astype(BF16)
    w_l = w_in0[:, rwkv_cols:].astype(BF16)
    ur, ul, ur_t, ul_t = _in_projection(x, meta, row(ln0_g), row(ln0_b), w_r, w_l)

    pad_rows = lambda a, n: jnp.pad(a, ((0, n - a.shape[0]), (0, 0)))
    rwkv_params = (slots(mu_shift[0][None, :]).astype(F32), row(w0[0]), pad_rows(w_decay_up[0], LANE).astype(BF16),
                   row(a0[0]), pad_rows(w_a_up[0], LANE).astype(BF16), pad_rows(w_g_up[0], ZG_SLOT).astype(BF16),
                   row(k_k[0]), row(k_a[0]), row(r_k[0]), row(gn_g[0]), row(gn_b[0]))
    y_rwkv = _rwkv_pipe_mixer(ur, ur_t, rwkv_params)

    blockdiag = lambda w: jax.scipy.linalg.block_diag(*[w[i] for i in range(LRU_BLOCKS)]).astype(BF16)
    lru_params = (conv_w[0], row(conv_b[0]), blockdiag(w_rg[0]), row(b_rg[0]), blockdiag(w_ig[0]), row(b_ig[0]),
                  row(lru_lambda[0]))
    y_lru = _lru_mixer(ul, ul_t, lru_params)

    w_rt = jnp.concatenate([w_router_grp[0], w_router_exp[0]], axis=1)
    w_rt = jnp.pad(w_rt, ((0, 0), (0, LANE - w_rt.shape[1])))
    wrt_hi = w_rt.astype(BF16)
    wrt_lo = (w_rt - wrt_hi.astype(F32)).astype(BF16)
    b_rt = jnp.concatenate([b_router_grp[0], b_router_exp[0]])
    b_rt = jnp.pad(b_rt, (0, LANE - b_rt.shape[0])).reshape(1, LANE)
    wo = w_out[0].astype(BF16)
    h1, route = _out_projection(x, y_rwkv, y_lru, row(ln0_g), row(ln0_b), wo[:RWKV_W], wo[RWKV_W:],
                                row(ln1_g[0]), row(ln1_b[0]), wrt_hi, wrt_lo, b_rt)

    M = B * T
    h1 = h1.reshape(M * SLABS, LANE)
    gates, dest, n_slots, tile_expert, n_valid, pad_lo, pad_hi = _routing_plan(route.reshape(M, LANE))
    row_asg = _invert_slots(dest, n_slots, pad_lo, pad_hi)
    ybuf = _moe_experts(h1, row_asg, tile_expert, n_valid, w_exp_gate[0], w_exp_up[0], w_exp_down[0])
    out = _combine(h1, ybuf, dest, gates, row(ln2_g[0]), row(ln2_b[0]))
    return out.reshape(B, T, D)
```

```python
import math

import jax
import jax.numpy as jnp
from jax import lax
from jax.experimental import pallas as pl
from jax.experimental.pallas import tpu as pltpu

F32 = jnp.float32
BF16 = jnp.bfloat16

D_MODEL = 1024
N_META = 16
RWKV_W = 512
RWKV_HEAD = 64
DECAY_RANK = 64
AAA_RANK = 64
GATE_RANK = 160
LRU_W = 512
LRU_BLOCKS = 8
CONV_WIDTH = 4
LRU_C = 8.0
N_GROUPS = 4
EXPERTS_PER_GROUP = 8
N_EXPERTS = N_GROUPS * EXPERTS_PER_GROUP
TOP_K = 2
D_EXPERT = 512
LN_EPS = 1e-5
GN_EPS = 64e-5
DEEPNORM_ALPHA = 2.0 ** 0.25

LANE = 128
OFF_R, OFF_K, OFF_V = 0, RWKV_W, 2 * RWKV_W
OFF_ZW = 3 * RWKV_W
OFF_ZA = OFF_ZW + LANE
OFF_ZG = OFF_ZA + LANE
ZG_SLOT = 2 * LANE
UR_W = OFF_ZG + ZG_SLOT
UL_W = 2 * LRU_W

TAIL = 128
CHUNK = 64
HEADS_PER_GROUP = 4
GW = HEADS_PER_GROUP * RWKV_HEAD
N_HGROUPS = RWKV_W // GW
LRU_TILE = 128
MOE_TILE = 256
COMBINE_TILE = 256
INVERT_BLOCK = 4096
DMA_UNROLL = 8
V7X_VMEM_BYTES = 64 * 1024 * 1024
VMEM_LIMIT = V7X_VMEM_BYTES - 8 * 1024 * 1024


def _cparams(sem):
    return pltpu.CompilerParams(dimension_semantics=sem, vmem_limit_bytes=VMEM_LIMIT)


def _layer_norm(x, g, b):
    mu = jnp.mean(x, -1, keepdims=True)
    xc = x - mu
    var = jnp.mean(xc * xc, -1, keepdims=True)
    return xc * lax.rsqrt(var + LN_EPS) * g + b


def _dot(a, b):
    return jnp.dot(a, b, preferred_element_type=F32)


def _dot_nt(a, b):
    return lax.dot_general(a, b, (((1,), (1,)), ((), ())), preferred_element_type=F32)


def _dot_tn(a, b):
    return lax.dot_general(a, b, (((0,), (0,)), ((), ())), preferred_element_type=F32)


def _full(a):
    return pl.BlockSpec(a.shape, lambda *_: (0,) * a.ndim)


def _inproj_kernel(x_ref, g_ref, b_ref, wr_ref, wl_ref, ur_ref, ul_ref):
    h = _layer_norm(x_ref[0], g_ref[...], b_ref[...]).astype(BF16)
    ur_ref[0] = _dot(h, wr_ref[...])
    ul_ref[0] = _dot(h, wl_ref[...])


def _inproj_tail_kernel(x_ref, g_ref, b_ref, wr_ref, wl_ref, ur_ref, ul_ref):
    h = _layer_norm(x_ref[...], g_ref[...], b_ref[...]).astype(BF16)
    rows = lax.broadcasted_iota(jnp.int32, (TAIL, 1), 0)
    valid = (rows >= TAIL - N_META).astype(F32)
    ur_ref[...] = _dot(h, wr_ref[...]) * valid
    ul_ref[...] = _dot(h, wl_ref[...]) * valid


def _in_projection(x, meta, ln0_g, ln0_b, w_r, w_l):
    B, T, D = x.shape
    tm = 512
    ur, ul = pl.pallas_call(
        _inproj_kernel,
        out_shape=(jax.ShapeDtypeStruct((B, T, UR_W), F32), jax.ShapeDtypeStruct((B, T, UL_W), F32)),
        grid=(B, T // tm),
        in_specs=[pl.BlockSpec((1, tm, D), lambda b, i: (b, i, 0)), _full(ln0_g), _full(ln0_b), _full(w_r), _full(w_l)],
        out_specs=(pl.BlockSpec((1, tm, UR_W), lambda b, i: (b, i, 0)),
                   pl.BlockSpec((1, tm, UL_W), lambda b, i: (b, i, 0))),
        compiler_params=_cparams(("parallel", "parallel")),
        name="inproj",
    )(x, ln0_g, ln0_b, w_r, w_l)
    tail_x = jnp.concatenate([jnp.zeros((TAIL - N_META, D), F32), meta.astype(F32)], axis=0)
    ur_t, ul_t = pl.pallas_call(
        _inproj_tail_kernel,
        out_shape=(jax.ShapeDtypeStruct((TAIL, UR_W), F32), jax.ShapeDtypeStruct((TAIL, UL_W), F32)),
        grid=(1,),
        in_specs=[_full(tail_x), _full(ln0_g), _full(ln0_b), _full(w_r), _full(w_l)],
        out_specs=(pl.BlockSpec((TAIL, UR_W), lambda i: (0, 0)), pl.BlockSpec((TAIL, UL_W), lambda i: (0, 0))),
        compiler_params=_cparams(("arbitrary",)),
        name="inproj_tail",
    )(tail_x, ln0_g, ln0_b, w_r, w_l)
    return ur, ul, ur_t, ul_t


def _rwkv_pipe_kernel(u_ref, ut_ref, mu_ref, w0_ref, wdu_ref, a0_ref, wau_ref, wgu_ref, kk_ref, ka_ref, rk_ref,
                      gng_ref, gnb_ref, bones_ref, bm_ref, eye_ref, msl_ref, mil_ref,
                      m8_ref, m16_ref, m32_ref, m64_ref, y_ref,
                      s_ref, prev_ref, y0_s, q_s, mc_s, nc_s, we_s, bonus_s, g_s, yraw_s):
    s_id = pl.program_id(0)
    nb = u_ref.shape[0]
    blk = u_ref.shape[1]
    npc = blk // CHUNK
    nseq = nb * npc
    seq_rows = lambda q: slice(q * CHUNK, (q + 1) * CHUNK)
    per_seq = lambda f: jnp.concatenate([f(q) for q in range(nseq)], axis=0)
    w_slot = lax.rem(s_id, 2)
    r_slot = 1 - w_slot

    @pl.when(s_id == 0)
    def _():
        s_ref[...] = jnp.zeros_like(s_ref)
        prev_ref[...] = jnp.zeros_like(prev_ref)
        for ref in (y0_s, q_s, mc_s, nc_s, we_s, bonus_s, g_s):
            ref[1] = jnp.zeros(ref.shape[1:], ref.dtype)

    u_x = u_ref[...].reshape(nb * blk, UR_W)
    u = jnp.where(s_id == 0, jnp.concatenate([ut_ref[...]] * nb, axis=0), u_x)
    row = lax.broadcasted_iota(jnp.int32, u.shape, 0)
    prev_rows = jnp.concatenate([jnp.broadcast_to(prev_ref[b:b + 1, :], (blk, UR_W)) for b in range(nb)], axis=0)
    u_prev = jnp.where(jnp.bitwise_and(row, blk - 1) == 0, prev_rows, pltpu.roll(u, 1, 0))
    for b in range(nb):
        prev_ref[b:b + 1, :] = u[(b + 1) * blk - 1:(b + 1) * blk, :]
    x = u + (u_prev - u) * mu_ref[...]
    r = x[:, OFF_R:OFF_R + RWKV_W]
    k = x[:, OFF_K:OFF_K + RWKV_W]
    v = x[:, OFF_V:OFF_V + RWKV_W]
    zw = x[:, OFF_ZW:OFF_ZW + LANE]
    za = x[:, OFF_ZA:OFF_ZA + LANE]
    zg = x[:, OFF_ZG:OFF_ZG + ZG_SLOT]

    b16 = lambda t: t.astype(BF16)
    bones = bones_ref[...]
    head_sum = lambda t: _dot(b16(t), bones)

    z = w0_ref[...] + _dot(b16(jnp.tanh(zw)), wdu_ref[...])
    logw = -math.exp(-0.5) * jax.nn.sigmoid(z)
    a = jax.nn.sigmoid(a0_ref[...] + _dot(b16(za), wau_ref[...]))
    g = _dot(b16(jax.nn.sigmoid(zg)), wgu_ref[...])
    kk = k * kk_ref[...]
    kk = kk / jnp.maximum(jnp.sqrt(head_sum(kk * kk)), 1e-12)
    k = k * (1.0 + (a - 1.0) * ka_ref[...])
    kka = kk * a
    bonus_s[w_slot] = head_sum(r * k * rk_ref[...]) * v
    g_s[w_slot] = g

    cl = logw
    row_in_chunk = jnp.bitwise_and(lax.broadcasted_iota(jnp.int32, cl.shape, 0), CHUNK - 1)
    d = 1
    while d < CHUNK:
        cl = cl + jnp.where(row_in_chunk >= d, pltpu.roll(cl, d, 0), 0.0)
        d *= 2
    cl_last = per_seq(lambda q: jnp.broadcast_to(cl[(q + 1) * CHUNK - 1:(q + 1) * CHUNK, :], (CHUNK, RWKV_W)))
    e_neg = jnp.exp(-cl)
    e_end = jnp.exp(cl_last - cl)
    rt = r * jnp.exp(cl)
    kt = k * e_neg
    at = -kk * jnp.exp(cl - logw)
    bt = kka * e_neg
    kw = k * e_end
    bw = kka * e_end
    w_end = jnp.exp(cl_last)
    for q in range(nseq):
        we_s[w_slot, q * 8:(q + 1) * 8, :] = w_end[q * CHUNK:q * CHUNK + 8, :]

    bm = bm_ref[...]
    bm16 = b16(bm)
    eye = eye_ref[...]
    msl = msl_ref[...]
    mil = mil_ref[...]
    tile4 = lambda t: jnp.concatenate([t] * HEADS_PER_GROUP, axis=0)
    fold4 = lambda t: sum(t[i * CHUNK:(i + 1) * CHUNK] for i in range(HEADS_PER_GROUP))
    bd = lambda t: tile4(b16(t)) * bm16
    rows2 = lambda x, y: jnp.concatenate([x, y], axis=0)
    cols2 = lambda x, y: jnp.concatenate([x, y], axis=1)

    probs = [(seq_rows(q), slice(hg * GW, (hg + 1) * GW)) for q in range(nseq) for hg in range(N_HGROUPS)]
    each = lambda f, *ls: [f(*xs) for xs in zip(*ls)]
    pick = lambda t: [t[rq, sl] for rq, sl in probs]
    at_w, rt_w, v_w = pick(at), pick(rt), pick(v)
    lhs = each(lambda x, y: b16(rows2(x, y)), at_w, rt_w)
    ab = each(_dot_nt, lhs, each(bd, pick(bt)))
    ak = each(_dot_nt, lhs, each(bd, pick(kt)))
    a_ab = each(lambda t: t[:CHUNK] * msl, ab)
    a_rb = each(lambda t: b16(t[CHUNK:] * mil), ab)
    a_xk = each(lambda t: b16(rows2(t[:CHUNK] * msl, t[CHUNK:] * mil)), ak)

    a0 = each(lambda t: b16(t * m8_ref[...]), a_ab)
    a2 = each(lambda t: b16(_dot(t, bd(t))), a0)
    a4 = each(lambda t: b16(_dot(t, bd(t))), a2)
    p1 = each(lambda t: eye + t.astype(F32), a0)
    p1 = each(lambda p, t: p + _dot(b16(p), bd(t)), p1, a2)
    tt = each(lambda p, t: p + _dot(b16(p), bd(t)), p1, a4)
    for m_ref in (m16_ref, m32_ref, m64_ref):
        tb = each(b16, tt)
        off = each(lambda t: b16(t * m_ref[...]), a_ab)
        half = each(lambda x, y: b16(_dot(x, bd(y))), tb, off)
        tt = each(lambda t, x, y: t + _dot(x, bd(y)), tt, half, tb)
    tb = each(b16, tt)

    xv = each(lambda x, y: _dot(x, bd(y)), a_xk, v_w)
    u0 = each(lambda x, y: _dot(x, bd(y[:CHUNK])), tb, xv)
    ta = each(lambda x, y: _dot(x, bd(y)), tb, at_w)
    y0 = each(lambda x, y, z: _dot(x, bd(y)) + z[CHUNK:], a_rb, u0, xv)
    qq = each(lambda x, y, z: x + _dot(y, bd(z)), rt_w, a_rb, ta)
    left = each(lambda x, y, z: b16(rows2(cols2(x, y), cols2(jnp.zeros_like(z), z))), ta, u0, v_w)
    right = each(lambda x, y: b16(rows2(x, y)), pick(bw), pick(kw))
    mn = each(_dot_tn, left, right)
    for i, (rq, sl) in enumerate(probs):
        y0_s[w_slot, rq, sl] = y0[i]
        q_s[w_slot, rq, sl] = b16(qq[i])
        mc_s[w_slot, rq, sl] = b16(fold4(mn[i][:GW] * bm))
        nc_s[w_slot, rq, sl] = fold4(mn[i][GW:] * bm)

    for b in range(nb):
        for hg in range(N_HGROUPS):
            sl = slice(hg * GW, (hg + 1) * GW)
            s = s_ref[b, hg]
            for j in range(npc):
                q = b * npc + j
                rq = seq_rows(q)
                yraw_s[rq, sl] = y0_s[r_slot, rq, sl] + _dot_nt(q_s[r_slot, rq, sl], bd(s))
                s = (s * we_s[r_slot, q * 8:q * 8 + 1, sl] + _dot(b16(s), bd(mc_s[r_slot, rq, sl]))
                     + nc_s[r_slot, rq, sl])
            s_ref[b, hg] = s

    y = yraw_s[...]
    inv_n = 1.0 / RWKV_HEAD
    ym = head_sum(y) * inv_n
    yc = y - ym
    yv = head_sum(yc * yc) * inv_n
    yn = yc * lax.rsqrt(yv + GN_EPS) * gng_ref[...] + gnb_ref[...]
    y_ref[...] = ((yn + bonus_s[r_slot]) * g_s[r_slot]).astype(y_ref.dtype).reshape(y_ref.shape)


def _rwkv_masks():
    f = lambda m: m.astype(F32)
    i = jnp.arange(GW)[:, None]
    j = jnp.arange(GW)[None, :]
    bm = f((i // RWKV_HEAD) == (j // RWKV_HEAD))
    t = jnp.arange(CHUNK)[:, None]
    s = (jnp.arange(GW) % CHUNK)[None, :]
    same = lambda n: (t // n) == (s // n)
    msl = f(t > s)
    mil = f(t >= s)
    m8 = f(same(8))
    m16 = f(same(16) & ~same(8))
    m32 = f(same(32) & ~same(16))
    m64 = f(~same(32))
    eye = f(t == s)
    hi = jnp.arange(RWKV_W)
    bones = ((hi[:, None] // RWKV_HEAD) == (hi[None, :] // RWKV_HEAD)).astype(BF16)
    return bones, bm, eye, msl, mil, m8, m16, m32, m64


def _rwkv_pipe_mixer(ur, ur_tail, params):
    B, T, _ = ur.shape
    blk = TAIL
    assert T % blk == 0 and blk % CHUNK == 0 and CHUNK == RWKV_HEAD
    n_blocks = T // blk
    rows = B * blk
    consts = _rwkv_masks()
    in_map = lambda s: (0, jnp.clip(s - 1, 0, n_blocks - 1), 0)
    out_map = lambda s: (0, jnp.clip(s - 2, 0, n_blocks - 1), 0)
    slot2 = lambda w, dt: pltpu.VMEM((2, rows, w), dt)
    return pl.pallas_call(
        _rwkv_pipe_kernel,
        out_shape=jax.ShapeDtypeStruct((B, T, RWKV_W), BF16),
        grid=(n_blocks + 2,),
        in_specs=[pl.BlockSpec((B, blk, UR_W), in_map), _full(ur_tail)]
                 + [_full(p) for p in params] + [_full(m) for m in consts],
        out_specs=pl.BlockSpec((B, blk, RWKV_W), out_map),
        scratch_shapes=[pltpu.VMEM((B, N_HGROUPS, CHUNK, GW), F32), pltpu.VMEM((B, UR_W), F32),
                        slot2(RWKV_W, F32), slot2(RWKV_W, BF16), slot2(RWKV_W, BF16), slot2(RWKV_W, F32),
                        pltpu.VMEM((2, 8 * rows // CHUNK, RWKV_W), F32), slot2(RWKV_W, F32), slot2(RWKV_W, F32),
                        pltpu.VMEM((rows, RWKV_W), F32)],
        compiler_params=_cparams(("arbitrary",)),
        name="rwkv7",
    )(ur, ur_tail, *params, *consts)


def _gelu_tanh(x):
    return 0.5 * x * (1.0 + jnp.tanh(math.sqrt(2.0 / math.pi) * (x + 0.044715 * (x * x * x))))


LRU_CARRY = 8


def _lru_kernel(u_ref, ut_ref, cw_ref, cb_ref, wrg_ref, brg_ref, wig_ref, big_ref, lam_ref, y_ref,
                xs_ref, hprev_ref):
    c = pl.program_id(0)
    nb = u_ref.shape[0]
    nrow = nb * LRU_TILE

    @pl.when(c == 0)
    def _():
        xs_ref[...] = jnp.zeros_like(xs_ref)
        hprev_ref[...] = jnp.zeros_like(hprev_ref)

    u_x = u_ref[...].reshape(nrow, UL_W)
    u = jnp.where(c == 0, jnp.concatenate([ut_ref[...]] * nb, axis=0), u_x)
    xl = u[:, :LRU_W]
    gl = u[:, LRU_W:]
    row = jnp.bitwise_and(lax.broadcasted_iota(jnp.int32, (nrow, LRU_W), 0), LRU_TILE - 1)
    in_group = jnp.bitwise_and(row, 7)
    roll_in_group = lambda t, d: pltpu.roll(t.reshape(t.shape[0] // 8, 8, LRU_W), d, 1).reshape(t.shape)
    xl_prev = jnp.concatenate(
        [p for b in range(nb) for p in (xs_ref[b], xl[b * LRU_TILE:(b + 1) * LRU_TILE - 8])], axis=0)
    xc = cb_ref[...] + cw_ref[CONV_WIDTH - 1:CONV_WIDTH, :] * xl
    for d in range(1, CONV_WIDTH):
        tap = jnp.where(in_group >= d, roll_in_group(xl, d), roll_in_group(xl_prev, d))
        xc = xc + cw_ref[CONV_WIDTH - 1 - d:CONV_WIDTH - d, :] * tap
    for b in range(nb):
        xs_ref[b] = xl[(b + 1) * LRU_TILE - 8:(b + 1) * LRU_TILE]

    xcb = xc.astype(BF16)
    gate_r = jax.nn.sigmoid(_dot(xcb, wrg_ref[...]) + brg_ref[...])
    gate_i = jax.nn.sigmoid(_dot(xcb, wig_ref[...]) + big_ref[...])
    lam = lam_ref[...]
    log_sig = -(jnp.maximum(-lam, 0.0) + jnp.log1p(jnp.exp(-jnp.abs(lam))))
    log_a = LRU_C * gate_r * log_sig
    a = jnp.exp(log_a)
    mult = jnp.sqrt(jnp.maximum(1.0 - jnp.exp(2.0 * log_a), 0.0))
    b = mult * gate_i * xc
    b = jnp.where((c == 0) & (row < LRU_TILE - N_META), 0.0, b)

    d = 1
    while d < 8:
        keep = in_group >= d
        a_sh = jnp.where(keep, roll_in_group(a, d), 1.0)
        b_sh = jnp.where(keep, roll_in_group(b, d), 0.0)
        b = a * b_sh + b
        a = a * a_sh
        d *= 2
    groups = []
    for bi in range(nb):
        carry = hprev_ref[bi:bi + 1, :]
        for gi in range(LRU_TILE // 8):
            lo = bi * LRU_TILE + gi * 8
            hg = b[lo:lo + 8] + a[lo:lo + 8] * carry
            carry = hg[7:8, :]
            groups.append(hg)
        hprev_ref[bi:bi + 1, :] = carry
    h = jnp.concatenate(groups, axis=0)
    y_ref[...] = (h * _gelu_tanh(gl)).astype(y_ref.dtype).reshape(y_ref.shape)


def _lru_mixer(ul, ul_tail, params):
    B, T, _ = ul.shape
    assert TAIL == LRU_TILE
    x_map = lambda c: (0, jnp.maximum(c - 1, 0), 0)
    return pl.pallas_call(
        _lru_kernel,
        out_shape=jax.ShapeDtypeStruct((B, T, LRU_W), BF16),
        grid=(T // LRU_TILE + 1,),
        in_specs=[pl.BlockSpec((B, LRU_TILE, UL_W), x_map), _full(ul_tail)] + [_full(p) for p in params],
        out_specs=pl.BlockSpec((B, LRU_TILE, LRU_W), x_map),
        scratch_shapes=[pltpu.VMEM((B, LRU_CARRY, LRU_W), F32), pltpu.VMEM((B, LRU_W), F32)],
        compiler_params=_cparams(("arbitrary",)),
        name="rglru",
    )(ul, ul_tail, *params)


def _route(lg):
    lane = lax.broadcasted_iota(jnp.int32, lg.shape, 1)
    neg = jnp.float32(-jnp.inf)
    rmax = lambda t: jnp.max(t, axis=1, keepdims=True)
    first = lambda hit: jnp.min(jnp.where(hit, lane, LANE), axis=1, keepdims=True)
    is_grp = lane < N_GROUPS
    gl = jnp.where(is_grp, lg, neg)
    gmax = rmax(gl)
    g_sel = first(gl == gmax)
    p_g = 1.0 / jnp.sum(jnp.where(is_grp, jnp.exp(lg - gmax), 0.0), axis=1, keepdims=True)
    ex = lane - N_GROUPS
    in_grp = (ex >= 0) & (ex < N_EXPERTS) & (jnp.right_shift(ex, 3) == g_sel)
    el = jnp.where(in_grp, lg, neg)
    v1 = rmax(el)
    i1 = first(el == v1)
    el2 = jnp.where(lane == i1, neg, el)
    v2 = rmax(el2)
    i2 = first(el2 == v2)
    t = jnp.exp(v2 - v1)
    gate1 = p_g / (1.0 + t)
    gate2 = p_g * t / (1.0 + t)
    e1 = (i1 - N_GROUPS).astype(F32)
    e2 = (i2 - N_GROUPS).astype(F32)
    return jnp.where(lane == 0, e1, jnp.where(lane == 1, e2, jnp.where(lane == 2, gate1, jnp.where(lane == 3, gate2, 0.0))))


SLABS = D_MODEL // LANE


def _store_token_tiles(ref, val):
    n = val.shape[0]
    for s in range(SLABS):
        ref[pl.ds(s, n, stride=SLABS), :] = val[:, s * LANE:(s + 1) * LANE]


def _load_token_slabs(ref, n):
    return [ref[pl.ds(s, n, stride=SLABS), :] for s in range(SLABS)]


def _outproj_kernel(x_ref, yr_ref, yl_ref, g0_ref, b0_ref, wor_ref, wol_ref, g1_ref, b1_ref,
                    wrt_hi_ref, wrt_lo_ref, brt_ref, h1_ref, rt_ref):
    h0 = _layer_norm(x_ref[0], g0_ref[...], b0_ref[...])
    mix = _dot(yr_ref[0], wor_ref[...]) + _dot(yl_ref[0], wol_ref[...])
    h1 = _layer_norm(DEEPNORM_ALPHA * h0 + mix, g1_ref[...], b1_ref[...])
    _store_token_tiles(h1_ref.at[0], h1)
    hi = h1.astype(BF16)
    lo = (h1 - hi.astype(F32)).astype(BF16)
    w_hi = wrt_hi_ref[...]
    lg = _dot(hi, w_hi) + (_dot(hi, wrt_lo_ref[...]) + _dot(lo, w_hi)) + brt_ref[...]
    rt_ref[0] = _route(lg)


def _out_projection(x, y_rwkv, y_lru, ln0_g, ln0_b, wo_r, wo_l, ln1_g, ln1_b, wrt_hi, wrt_lo, brt):
    B, T, D = x.shape
    tm = 512
    rows = lambda w: pl.BlockSpec((1, tm, w), lambda b, i: (b, i, 0))
    return pl.pallas_call(
        _outproj_kernel,
        out_shape=(jax.ShapeDtypeStruct((B, T * SLABS, LANE), F32), jax.ShapeDtypeStruct((B, T, LANE), F32)),
        grid=(B, T // tm),
        in_specs=[rows(D), rows(RWKV_W), rows(LRU_W), _full(ln0_g), _full(ln0_b), _full(wo_r), _full(wo_l),
                  _full(ln1_g), _full(ln1_b), _full(wrt_hi), _full(wrt_lo), _full(brt)],
        out_specs=(pl.BlockSpec((1, tm * SLABS, LANE), lambda b, i: (b, i, 0)), rows(LANE)),
        compiler_params=_cparams(("parallel", "parallel")),
        name="outproj",
    )(x, y_rwkv, y_lru, ln0_g, ln0_b, wo_r, wo_l, ln1_g, ln1_b, wrt_hi, wrt_lo, brt)


def _invert_kernel(pad_lo_ref, pad_hi_ref, dest_ref, out_ref):
    i = pl.program_id(0)

    @pl.when(i == 0)
    def _():
        def zero(j, carry):
            out_ref[j] = 0
            return carry
        for e in range(N_EXPERTS):
            lax.fori_loop(pad_lo_ref[e], pad_hi_ref[e], zero, 0)
        lax.fori_loop(pad_hi_ref[N_EXPERTS - 1], out_ref.shape[0], zero, 0)

    base = i * INVERT_BLOCK

    def body(j, carry):
        out_ref[dest_ref[j]] = base + j
        return carry

    lax.fori_loop(0, INVERT_BLOCK, body, 0, unroll=8)


def _invert_slots(dest, n_slots, pad_lo, pad_hi):
    A = dest.shape[0]
    grid_spec = pltpu.PrefetchScalarGridSpec(
        num_scalar_prefetch=2,
        grid=(A // INVERT_BLOCK,),
        in_specs=[pl.BlockSpec((INVERT_BLOCK,), lambda i, lo, hi: (i,), memory_space=pltpu.SMEM)],
        out_specs=pl.BlockSpec(memory_space=pltpu.SMEM),
    )
    return pl.pallas_call(
        _invert_kernel,
        out_shape=jax.ShapeDtypeStruct((n_slots,), jnp.int32),
        grid_spec=grid_spec,
        compiler_params=_cparams(("arbitrary",)),
        name="invert_slots",
    )(pad_lo, pad_hi, dest)


def _row_copy(src_hbm, src_row, dst_ref, dst_row, sem):
    return pltpu.make_async_copy(src_hbm.at[pl.ds(src_row * SLABS, SLABS), :],
                                 dst_ref.at[pl.ds(dst_row * SLABS, SLABS), :], sem)


def _wait_tiles(src_hbm, dst_ref, sem):
    pltpu.make_async_copy(src_hbm.at[pl.ds(0, dst_ref.shape[0]), :], dst_ref, sem).wait()


def _moe_kernel(te_ref, nv_ref, ra_cur_ref, ra_nxt_ref, h_hbm, wg_ref, wu_ref, wd_ref, o_ref,
                xbuf, sem, wgb_ref, wub_ref, wdb_ref):
    i = pl.program_id(0)
    n_valid = nv_ref[0]
    slot = lax.rem(i, 2)

    def start_gather(ra_ref, s):
        def body(jj, carry):
            for u in range(DMA_UNROLL):
                j = jj * DMA_UNROLL + u
                tok = lax.shift_right_logical(ra_ref[j], 1)
                _row_copy(h_hbm, tok, xbuf.at[s], j, sem.at[s]).start(priority=u % 2)
            return carry
        lax.fori_loop(0, MOE_TILE // DMA_UNROLL, body, 0)

    @pl.when(i == 0)
    def _():
        start_gather(ra_cur_ref, 0)

    @pl.when(i + 1 < n_valid)
    def _():
        start_gather(ra_nxt_ref, 1 - slot)

    e = te_ref[i]
    e_prev = te_ref[jnp.maximum(i - 1, 0)]

    @pl.when((i == 0) | (e != e_prev))
    def _():
        wgb_ref[...] = wg_ref[0].astype(BF16)
        wub_ref[...] = wu_ref[0].astype(BF16)
        wdb_ref[...] = wd_ref[0].astype(BF16)

    @pl.when(i < n_valid)
    def _():
        _wait_tiles(h_hbm, xbuf.at[slot], sem.at[slot])
        xb = jnp.concatenate(_load_token_slabs(xbuf.at[slot], MOE_TILE), axis=1).astype(BF16)
        hg = _dot(xb, wgb_ref[...])
        hu = _dot(xb, wub_ref[...])
        mid = (hg * jax.nn.sigmoid(hg) * hu).astype(BF16)
        _store_token_tiles(o_ref, _dot(mid, wdb_ref[...]))

    @pl.when(i >= n_valid)
    def _():
        o_ref[...] = jnp.zeros_like(o_ref)


def _moe_experts(h1, row_asg, tile_expert, n_valid, w_gate, w_up, w_down):
    D = D_MODEL
    n_tiles = row_asg.shape[0] // MOE_TILE
    smem_tile = lambda f: pl.BlockSpec((MOE_TILE,), f, memory_space=pltpu.SMEM)
    grid_spec = pltpu.PrefetchScalarGridSpec(
        num_scalar_prefetch=2,
        grid=(n_tiles,),
        in_specs=[smem_tile(lambda i, te, nv: (i,)),
                  smem_tile(lambda i, te, nv: (jnp.minimum(i + 1, n_tiles - 1),)),
                  pl.BlockSpec(memory_space=pl.ANY),
                  pl.BlockSpec((1, D, D_EXPERT), lambda i, te, nv: (te[i], 0, 0)),
                  pl.BlockSpec((1, D, D_EXPERT), lambda i, te, nv: (te[i], 0, 0)),
                  pl.BlockSpec((1, D_EXPERT, D), lambda i, te, nv: (te[i], 0, 0))],
        out_specs=pl.BlockSpec((MOE_TILE * SLABS, LANE), lambda i, te, nv: (i, 0)),
        scratch_shapes=[pltpu.VMEM((2, MOE_TILE * SLABS, LANE), F32), pltpu.SemaphoreType.DMA((2,)),
                        pltpu.VMEM((D, D_EXPERT), BF16), pltpu.VMEM((D, D_EXPERT), BF16),
                        pltpu.VMEM((D_EXPERT, D), BF16)],
    )
    return pl.pallas_call(
        _moe_kernel,
        out_shape=jax.ShapeDtypeStruct((n_tiles * MOE_TILE * SLABS, LANE), F32),
        grid_spec=grid_spec,
        compiler_params=_cparams(("arbitrary",)),
        name="moe_experts",
    )(tile_expert, n_valid, row_asg, row_asg, h1, w_gate, w_up, w_down)


def _combine_kernel(d_cur_ref, d_nxt_ref, h_ref, gate_ref, g_ref, b_ref, y_hbm, o_ref, ybuf, sem):
    i = pl.program_id(0)
    n = pl.num_programs(0)
    slot = lax.rem(i, 2)

    def start_gather(d_ref, s):
        def body(tt, carry):
            for u in range(DMA_UNROLL // TOP_K):
                t = tt * (DMA_UNROLL // TOP_K) + u
                for k in range(TOP_K):
                    _row_copy(y_hbm, d_ref[TOP_K * t + k], ybuf.at[s, k], t, sem.at[s]).start(priority=k % 2)
            return carry
        lax.fori_loop(0, COMBINE_TILE * TOP_K // DMA_UNROLL, body, 0)

    @pl.when(i == 0)
    def _():
        start_gather(d_cur_ref, 0)

    @pl.when(i + 1 < n)
    def _():
        start_gather(d_nxt_ref, 1 - slot)

    for k in range(TOP_K):
        _wait_tiles(y_hbm, ybuf.at[slot, k], sem.at[slot])

    gate = gate_ref[...]
    tm = COMBINE_TILE
    ga = jnp.broadcast_to(gate[:, 0:1], (tm, LANE))
    gb = jnp.broadcast_to(gate[:, 1:2], (tm, LANE))
    hs = _load_token_slabs(h_ref, tm)
    ya = _load_token_slabs(ybuf.at[slot, 0], tm)
    yb = _load_token_slabs(ybuf.at[slot, 1], tm)
    z = [DEEPNORM_ALPHA * h + (ga * a + gb * b) for h, a, b in zip(hs, ya, yb)]
    inv_d = 1.0 / D_MODEL
    mu = sum(jnp.sum(t, axis=1, keepdims=True) for t in z) * inv_d
    zc = [t - mu for t in z]
    var = sum(jnp.sum(t * t, axis=1, keepdims=True) for t in zc) * inv_d
    rstd = lax.rsqrt(var + LN_EPS)
    for s in range(SLABS):
        cols = slice(s * LANE, (s + 1) * LANE)
        o_ref[:, cols] = zc[s] * rstd * g_ref[:, cols] + b_ref[:, cols]


def _combine(h1, ybuf, dest, gates, ln2_g, ln2_b):
    D = D_MODEL
    M = h1.shape[0] // SLABS
    tm = COMBINE_TILE
    n = M // tm
    smem_tile = lambda f: pl.BlockSpec((TOP_K * tm,), f, memory_space=pltpu.SMEM)
    return pl.pallas_call(
        _combine_kernel,
        out_shape=jax.ShapeDtypeStruct((M, D), F32),
        grid=(n,),
        in_specs=[smem_tile(lambda i: (i,)), smem_tile(lambda i: (jnp.minimum(i + 1, n - 1),)),
                  pl.BlockSpec((tm * SLABS, LANE), lambda i: (i, 0)), pl.BlockSpec((tm, TOP_K), lambda i: (i, 0)),
                  _full(ln2_g), _full(ln2_b), pl.BlockSpec(memory_space=pl.ANY)],
        out_specs=pl.BlockSpec((tm, D), lambda i: (i, 0)),
        scratch_shapes=[pltpu.VMEM((2, TOP_K, tm * SLABS, LANE), F32), pltpu.SemaphoreType.DMA((2,))],
        compiler_params=_cparams(("arbitrary",)),
        name="combine",
    )(dest, dest, h1, gates, ln2_g, ln2_b, ybuf)


def _routing_plan(route):
    M = route.shape[0]
    eid = route[:, :TOP_K].astype(jnp.int32).reshape(-1)
    gates = route[:, TOP_K:2 * TOP_K]
    A = M * TOP_K
    onehot = (eid[:, None] == jnp.arange(N_EXPERTS, dtype=eid.dtype)[None, :]).astype(jnp.int32)
    csum = jnp.cumsum(onehot, axis=0)
    rank = jnp.sum(csum * onehot, axis=1) - 1
    counts = csum[-1]
    pcounts = (counts + MOE_TILE - 1) // MOE_TILE * MOE_TILE
    pends = jnp.cumsum(pcounts)
    pstarts = pends - pcounts
    dest = (jnp.sum(onehot * pstarts[None, :], axis=1) + rank).astype(jnp.int32)
    n_tiles = (A + N_EXPERTS * (MOE_TILE - 1) + MOE_TILE - 1) // MOE_TILE
    n_valid = (pends[-1] // MOE_TILE).astype(jnp.int32)
    tile_start = jnp.minimum(jnp.arange(n_tiles, dtype=jnp.int32) * MOE_TILE, pends[-1] - 1)
    tile_expert = jnp.sum((pends[None, :] <= tile_start[:, None]).astype(jnp.int32), axis=1)
    tile_expert = jnp.minimum(tile_expert, N_EXPERTS - 1).astype(jnp.int32)
    pad_lo = (pstarts + counts).astype(jnp.int32)
    pad_hi = pends.astype(jnp.int32)
    return gates, dest, n_tiles * MOE_TILE, tile_expert, n_valid.reshape(1), pad_lo, pad_hi


def kernel(x, meta, ln0_g, ln0_b, w_in, mu_shift, w0, w_decay_up, a0, w_a_up, w_g_up, k_k, k_a, r_k, gn_g, gn_b, conv_w, conv_b, w_rg, b_rg, w_ig, b_ig, lru_lambda, w_out, ln1_g, ln1_b, w_router_grp, b_router_grp, w_router_exp, b_router_exp, w_exp_gate, w_exp_up, w_exp_down, ln2_g, ln2_b):
    B, T, D = x.shape
    assert D == D_MODEL and T % 512 == 0 and w_in.shape[0] == 1
    assert (B * T * TOP_K) % INVERT_BLOCK == 0
    row = lambda p: p.reshape(1, -1).astype(F32)
    n_rw = 3 * RWKV_W
    w_in0 = w_in[0]

    def slots(p):
        pad = lambda a, n: jnp.pad(a, [(0, 0)] * (a.ndim - 1) + [(0, n - a.shape[-1])])
        zw = p[..., n_rw:n_rw + DECAY_RANK]
        za = p[..., n_rw + DECAY_RANK:n_rw + DECAY_RANK + AAA_RANK]
        zg = p[..., n_rw + DECAY_RANK + AAA_RANK:n_rw + DECAY_RANK + AAA_RANK + GATE_RANK]
        return jnp.concatenate([p[..., :n_rw], pad(zw, LANE), pad(za, LANE), pad(zg, ZG_SLOT)], axis=-1)

    rwkv_cols = n_rw + DECAY_RANK + AAA_RANK + GATE_RANK
    w_r = slots(w_in0[:, :rwkv_cols]).astype(BF16)
    w_l = w_in0[:, rwkv_cols:].astype(BF16)
    ur, ul, ur_t, ul_t = _in_projection(x, meta, row(ln0_g), row(ln0_b), w_r, w_l)

    pad_rows = lambda a, n: jnp.pad(a, ((0, n - a.shape[0]), (0, 0)))
    rwkv_params = (slots(mu_shift[0][None, :]).astype(F32), row(w0[0]), pad_rows(w_decay_up[0], LANE).astype(BF16),
                   row(a0[0]), pad_rows(w_a_up[0], LANE).astype(BF16), pad_rows(w_g_up[0], ZG_SLOT).astype(BF16),
                   row(k_k[0]), row(k_a[0]), row(r_k[0]), row(gn_g[0]), row(gn_b[0]))
    y_rwkv = _rwkv_pipe_mixer(ur, ur_t, rwkv_params)

    blockdiag = lambda w: jax.scipy.linalg.block_diag(*[w[i] for i in range(LRU_BLOCKS)]).astype(BF16)
    lru_params = (conv_w[0], row(conv_b[0]), blockdiag(w_rg[0]), row(b_rg[0]), blockdiag(w_ig[0]), row(b_ig[0]),
                  row(lru_lambda[0]))
    y_lru = _lru_mixer(ul, ul_t, lru_params)

    w_rt = jnp.concatenate([w_router_grp[0], w_router_exp[0]], axis=1)
    w_rt = jnp.pad(w_rt, ((0, 0), (0, LANE - w_rt.shape[1])))
    wrt_hi = w_rt.astype(BF16)
    wrt_lo = (w_rt - wrt_hi.astype(F32)).astype(BF16)
    b_rt = jnp.concatenate([b_router_grp[0], b_router_exp[0]])
    b_rt = jnp.pad(b_rt, (0, LANE - b_rt.shape[0])).reshape(1, LANE)
    wo = w_out[0].astype(BF16)
    h1, route = _out_projection(x, y_rwkv, y_lru, row(ln0_g), row(ln0_b), wo[:RWKV_W], wo[RWKV_W:],
                                row(ln1_g[0]), row(ln1_b[0]), wrt_hi, wrt_lo, b_rt)

    M = B * T
    h1 = h1.reshape(M * SLABS, LANE)
    gates, dest, n_slots, tile_expert, n_valid, pad_lo, pad_hi = _routing_plan(route.reshape(M, LANE))
    row_asg = _invert_slots(dest, n_slots, pad_lo, pad_hi)
    ybuf = _moe_experts(h1, row_asg, tile_expert, n_valid, w_exp_gate[0], w_exp_up[0], w_exp_down[0])
    out = _combine(h1, ybuf, dest, gates, row(ln2_g[0]), row(ln2_b[0]))
    return out.reshape(B, T, D)
```

```python
import math

import jax
import jax.numpy as jnp
from jax import lax
from jax.experimental import pallas as pl
from jax.experimental.pallas import tpu as pltpu

F32 = jnp.float32
BF16 = jnp.bfloat16

D_MODEL = 1024
N_META = 16
RWKV_W = 512
RWKV_HEAD = 64
DECAY_RANK = 64
AAA_RANK = 64
GATE_RANK = 160
LRU_W = 512
LRU_BLOCKS = 8
CONV_WIDTH = 4
LRU_C = 8.0
N_GROUPS = 4
EXPERTS_PER_GROUP = 8
N_EXPERTS = N_GROUPS * EXPERTS_PER_GROUP
TOP_K = 2
D_EXPERT = 512
LN_EPS = 1e-5
GN_EPS = 64e-5
DEEPNORM_ALPHA = 2.0 ** 0.25

LANE = 128
OFF_R, OFF_K, OFF_V = 0, RWKV_W, 2 * RWKV_W
OFF_ZW = 3 * RWKV_W
OFF_ZA = OFF_ZW + LANE
OFF_ZG = OFF_ZA + LANE
ZG_SLOT = 2 * LANE
UR_W = OFF_ZG + ZG_SLOT
UL_W = 2 * LRU_W

TAIL = 128
CHUNK = 64
HEADS_PER_GROUP = 4
GW = HEADS_PER_GROUP * RWKV_HEAD
N_HGROUPS = RWKV_W // GW
LRU_TILE = 128
MOE_TILE = 256
COMBINE_TILE = 256
INVERT_BLOCK = 4096
DMA_UNROLL = 8
MOE_DMA_CHUNK = MOE_TILE // 8
V7X_VMEM_BYTES = 64 * 1024 * 1024
VMEM_LIMIT = V7X_VMEM_BYTES - 8 * 1024 * 1024


def _cparams(sem):
    return pltpu.CompilerParams(dimension_semantics=sem, vmem_limit_bytes=VMEM_LIMIT)


def _layer_norm(x, g, b):
    mu = jnp.mean(x, -1, keepdims=True)
    xc = x - mu
    var = jnp.mean(xc * xc, -1, keepdims=True)
    return xc * lax.rsqrt(var + LN_EPS) * g + b


def _dot(a, b):
    return jnp.dot(a, b, preferred_element_type=F32)


def _dot_nt(a, b):
    return lax.dot_general(a, b, (((1,), (1,)), ((), ())), preferred_element_type=F32)


def _dot_tn(a, b):
    return lax.dot_general(a, b, (((0,), (0,)), ((), ())), preferred_element_type=F32)


def _full(a):
    return pl.BlockSpec(a.shape, lambda *_: (0,) * a.ndim)


def _inproj_kernel(x_ref, g_ref, b_ref, wr_ref, wl_ref, ur_ref, ul_ref):
    h = _layer_norm(x_ref[0], g_ref[...], b_ref[...]).astype(BF16)
    ur_ref[0] = _dot(h, wr_ref[...])
    ul_ref[0] = _dot(h, wl_ref[...])


def _inproj_tail_kernel(x_ref, g_ref, b_ref, wr_ref, wl_ref, ur_ref, ul_ref):
    h = _layer_norm(x_ref[...], g_ref[...], b_ref[...]).astype(BF16)
    rows = lax.broadcasted_iota(jnp.int32, (TAIL, 1), 0)
    valid = (rows >= TAIL - N_META).astype(F32)
    ur_ref[...] = _dot(h, wr_ref[...]) * valid
    ul_ref[...] = _dot(h, wl_ref[...]) * valid


def _in_projection(x, meta, ln0_g, ln0_b, w_r, w_l):
    B, T, D = x.shape
    tm = 512
    ur, ul = pl.pallas_call(
        _inproj_kernel,
        out_shape=(jax.ShapeDtypeStruct((B, T, UR_W), F32), jax.ShapeDtypeStruct((B, T, UL_W), F32)),
        grid=(B, T // tm),
        in_specs=[pl.BlockSpec((1, tm, D), lambda b, i: (b, i, 0)), _full(ln0_g), _full(ln0_b), _full(w_r), _full(w_l)],
        out_specs=(pl.BlockSpec((1, tm, UR_W), lambda b, i: (b, i, 0)),
                   pl.BlockSpec((1, tm, UL_W), lambda b, i: (b, i, 0))),
        compiler_params=_cparams(("parallel", "parallel")),
        name="inproj",
    )(x, ln0_g, ln0_b, w_r, w_l)
    tail_x = jnp.concatenate([jnp.zeros((TAIL - N_META, D), F32), meta.astype(F32)], axis=0)
    ur_t, ul_t = pl.pallas_call(
        _inproj_tail_kernel,
        out_shape=(jax.ShapeDtypeStruct((TAIL, UR_W), F32), jax.ShapeDtypeStruct((TAIL, UL_W), F32)),
        grid=(1,),
        in_specs=[_full(tail_x), _full(ln0_g), _full(ln0_b), _full(w_r), _full(w_l)],
        out_specs=(pl.BlockSpec((TAIL, UR_W), lambda i: (0, 0)), pl.BlockSpec((TAIL, UL_W), lambda i: (0, 0))),
        compiler_params=_cparams(("arbitrary",)),
        name="inproj_tail",
    )(tail_x, ln0_g, ln0_b, w_r, w_l)
    return ur, ul, ur_t, ul_t


def _rwkv_pipe_kernel(u_ref, ut_ref, mu_ref, w0_ref, wdu_ref, a0_ref, wau_ref, wgu_ref, kk_ref, ka_ref, rk_ref,
                      gng_ref, gnb_ref, bones_ref, bm_ref, eye_ref, msl_ref, mil_ref,
                      m8_ref, m16_ref, m32_ref, m64_ref, y_ref,
                      s_ref, prev_ref, y0_s, q_s, mc_s, nc_s, we_s, bonus_s, g_s, yraw_s):
    s_id = pl.program_id(0)
    nb = u_ref.shape[0]
    blk = u_ref.shape[1]
    npc = blk // CHUNK
    nseq = nb * npc
    seq_rows = lambda q: slice(q * CHUNK, (q + 1) * CHUNK)
    per_seq = lambda f: jnp.concatenate([f(q) for q in range(nseq)], axis=0)
    w_slot = lax.rem(s_id, 2)
    r_slot = 1 - w_slot

    @pl.when(s_id == 0)
    def _():
        s_ref[...] = jnp.zeros_like(s_ref)
        prev_ref[...] = jnp.zeros_like(prev_ref)
        for ref in (y0_s, q_s, mc_s, nc_s, we_s, bonus_s, g_s):
            ref[1] = jnp.zeros(ref.shape[1:], ref.dtype)

    u_x = u_ref[...].reshape(nb * blk, UR_W)
    u = jnp.where(s_id == 0, jnp.concatenate([ut_ref[...]] * nb, axis=0), u_x)
    row = lax.broadcasted_iota(jnp.int32, u.shape, 0)
    prev_rows = jnp.concatenate([jnp.broadcast_to(prev_ref[b:b + 1, :], (blk, UR_W)) for b in range(nb)], axis=0)
    u_prev = jnp.where(jnp.bitwise_and(row, blk - 1) == 0, prev_rows, pltpu.roll(u, 1, 0))
    for b in range(nb):
        prev_ref[b:b + 1, :] = u[(b + 1) * blk - 1:(b + 1) * blk, :]
    x = u + (u_prev - u) * mu_ref[...]
    r = x[:, OFF_R:OFF_R + RWKV_W]
    k = x[:, OFF_K:OFF_K + RWKV_W]
    v = x[:, OFF_V:OFF_V + RWKV_W]
    zw = x[:, OFF_ZW:OFF_ZW + LANE]
    za = x[:, OFF_ZA:OFF_ZA + LANE]
    zg = x[:, OFF_ZG:OFF_ZG + ZG_SLOT]

    b16 = lambda t: t.astype(BF16)
    bones = bones_ref[...]
    head_sum = lambda t: _dot(b16(t), bones)

    z = w0_ref[...] + _dot(b16(jnp.tanh(zw)), wdu_ref[...])
    logw = -math.exp(-0.5) * jax.nn.sigmoid(z)
    a = jax.nn.sigmoid(a0_ref[...] + _dot(b16(za), wau_ref[...]))
    g = _dot(b16(jax.nn.sigmoid(zg)), wgu_ref[...])
    kk = k * kk_ref[...]
    kk = kk / jnp.maximum(jnp.sqrt(head_sum(kk * kk)), 1e-12)
    k = k * (1.0 + (a - 1.0) * ka_ref[...])
    kka = kk * a
    bonus_s[w_slot] = head_sum(r * k * rk_ref[...]) * v
    g_s[w_slot] = g

    cl = logw
    row_in_chunk = jnp.bitwise_and(lax.broadcasted_iota(jnp.int32, cl.shape, 0), CHUNK - 1)
    d = 1
    while d < CHUNK:
        cl = cl + jnp.where(row_in_chunk >= d, pltpu.roll(cl, d, 0), 0.0)
        d *= 2
    cl_last = per_seq(lambda q: jnp.broadcast_to(cl[(q + 1) * CHUNK - 1:(q + 1) * CHUNK, :], (CHUNK, RWKV_W)))
    e_neg = jnp.exp(-cl)
    e_end = jnp.exp(cl_last - cl)
    rt = r * jnp.exp(cl)
    kt = k * e_neg
    at = -kk * jnp.exp(cl - logw)
    bt = kka * e_neg
    kw = k * e_end
    bw = kka * e_end
    w_end = jnp.exp(cl_last)
    for q in range(nseq):
        we_s[w_slot, q * 8:(q + 1) * 8, :] = w_end[q * CHUNK:q * CHUNK + 8, :]

    bm = bm_ref[...]
    bm16 = b16(bm)
    eye = eye_ref[...]
    msl = msl_ref[...]
    mil = mil_ref[...]
    tile4 = lambda t: jnp.concatenate([t] * HEADS_PER_GROUP, axis=0)
    fold4 = lambda t: sum(t[i * CHUNK:(i + 1) * CHUNK] for i in range(HEADS_PER_GROUP))
    bd = lambda t: tile4(b16(t)) * bm16
    rows2 = lambda x, y: jnp.concatenate([x, y], axis=0)
    cols2 = lambda x, y: jnp.concatenate([x, y], axis=1)

    probs = [(seq_rows(q), slice(hg * GW, (hg + 1) * GW)) for q in range(nseq) for hg in range(N_HGROUPS)]
    each = lambda f, *ls: [f(*xs) for xs in zip(*ls)]
    pick = lambda t: [t[rq, sl] for rq, sl in probs]
    at_w, rt_w, v_w = pick(at), pick(rt), pick(v)
    lhs = each(lambda x, y: b16(rows2(x, y)), at_w, rt_w)
    ab = each(_dot_nt, lhs, each(bd, pick(bt)))
    ak = each(_dot_nt, lhs, each(bd, pick(kt)))
    a_ab = each(lambda t: t[:CHUNK] * msl, ab)
    a_rb = each(lambda t: b16(t[CHUNK:] * mil), ab)
    a_xk = each(lambda t: b16(rows2(t[:CHUNK] * msl, t[CHUNK:] * mil)), ak)

    a0 = each(lambda t: b16(t * m8_ref[...]), a_ab)
    a2 = each(lambda t: b16(_dot(t, bd(t))), a0)
    a4 = each(lambda t: b16(_dot(t, bd(t))), a2)
    p1 = each(lambda t: eye + t.astype(F32), a0)
    p1 = each(lambda p, t: p + _dot(b16(p), bd(t)), p1, a2)
    tt = each(lambda p, t: p + _dot(b16(p), bd(t)), p1, a4)
    for m_ref in (m16_ref, m32_ref, m64_ref):
        tb = each(b16, tt)
        off = each(lambda t: b16(t * m_ref[...]), a_ab)
        half = each(lambda x, y: b16(_dot(x, bd(y))), tb, off)
        tt = each(lambda t, x, y: t + _dot(x, bd(y)), tt, half, tb)
    tb = each(b16, tt)

    xv = each(lambda x, y: _dot(x, bd(y)), a_xk, v_w)
    u0 = each(lambda x, y: _dot(x, bd(y[:CHUNK])), tb, xv)
    ta = each(lambda x, y: _dot(x, bd(y)), tb, at_w)
    y0 = each(lambda x, y, z: _dot(x, bd(y)) + z[CHUNK:], a_rb, u0, xv)
    qq = each(lambda x, y, z: x + _dot(y, bd(z)), rt_w, a_rb, ta)
    left = each(lambda x, y, z: b16(rows2(cols2(x, y), cols2(jnp.zeros_like(z), z))), ta, u0, v_w)
    right = each(lambda x, y: b16(rows2(x, y)), pick(bw), pick(kw))
    mn = each(_dot_tn, left, right)
    for i, (rq, sl) in enumerate(probs):
        y0_s[w_slot, rq, sl] = y0[i]
        q_s[w_slot, rq, sl] = b16(qq[i])
        mc_s[w_slot, rq, sl] = b16(fold4(mn[i][:GW] * bm))
        nc_s[w_slot, rq, sl] = fold4(mn[i][GW:] * bm)

    for b in range(nb):
        for hg in range(N_HGROUPS):
            sl = slice(hg * GW, (hg + 1) * GW)
            s = s_ref[b, hg]
            for j in range(npc):
                q = b * npc + j
                rq = seq_rows(q)
                yraw_s[rq, sl] = y0_s[r_slot, rq, sl] + _dot_nt(q_s[r_slot, rq, sl], bd(s))
                s = (s * we_s[r_slot, q * 8:q * 8 + 1, sl] + _dot(b16(s), bd(mc_s[r_slot, rq, sl]))
                     + nc_s[r_slot, rq, sl])
            s_ref[b, hg] = s

    y = yraw_s[...]
    inv_n = 1.0 / RWKV_HEAD
    ym = head_sum(y) * inv_n
    yc = y - ym
    yv = head_sum(yc * yc) * inv_n
    yn = yc * lax.rsqrt(yv + GN_EPS) * gng_ref[...] + gnb_ref[...]
    y_ref[...] = ((yn + bonus_s[r_slot]) * g_s[r_slot]).astype(y_ref.dtype).reshape(y_ref.shape)


def _rwkv_masks():
    f = lambda m: m.astype(F32)
    i = jnp.arange(GW)[:, None]
    j = jnp.arange(GW)[None, :]
    bm = f((i // RWKV_HEAD) == (j // RWKV_HEAD))
    t = jnp.arange(CHUNK)[:, None]
    s = (jnp.arange(GW) % CHUNK)[None, :]
    same = lambda n: (t // n) == (s // n)
    msl = f(t > s)
    mil = f(t >= s)
    m8 = f(same(8))
    m16 = f(same(16) & ~same(8))
    m32 = f(same(32) & ~same(16))
    m64 = f(~same(32))
    eye = f(t == s)
    hi = jnp.arange(RWKV_W)
    bones = ((hi[:, None] // RWKV_HEAD) == (hi[None, :] // RWKV_HEAD)).astype(BF16)
    return bones, bm, eye, msl, mil, m8, m16, m32, m64


def _rwkv_pipe_mixer(ur, ur_tail, params):
    B, T, _ = ur.shape
    blk = TAIL
    assert T % blk == 0 and blk % CHUNK == 0 and CHUNK == RWKV_HEAD
    n_blocks = T // blk
    rows = B * blk
    consts = _rwkv_masks()
    in_map = lambda s: (0, jnp.clip(s - 1, 0, n_blocks - 1), 0)
    out_map = lambda s: (0, jnp.clip(s - 2, 0, n_blocks - 1), 0)
    slot2 = lambda w, dt: pltpu.VMEM((2, rows, w), dt)
    return pl.pallas_call(
        _rwkv_pipe_kernel,
        out_shape=jax.ShapeDtypeStruct((B, T, RWKV_W), BF16),
        grid=(n_blocks + 2,),
        in_specs=[pl.BlockSpec((B, blk, UR_W), in_map), _full(ur_tail)]
                 + [_full(p) for p in params] + [_full(m) for m in consts],
        out_specs=pl.BlockSpec((B, blk, RWKV_W), out_map),
        scratch_shapes=[pltpu.VMEM((B, N_HGROUPS, CHUNK, GW), F32), pltpu.VMEM((B, UR_W), F32),
                        slot2(RWKV_W, F32), slot2(RWKV_W, BF16), slot2(RWKV_W, BF16), slot2(RWKV_W, F32),
                        pltpu.VMEM((2, 8 * rows // CHUNK, RWKV_W), F32), slot2(RWKV_W, F32), slot2(RWKV_W, F32),
                        pltpu.VMEM((rows, RWKV_W), F32)],
        compiler_params=_cparams(("arbitrary",)),
        name="rwkv7",
    )(ur, ur_tail, *params, *consts)


def _gelu_tanh(x):
    return 0.5 * x * (1.0 + jnp.tanh(math.sqrt(2.0 / math.pi) * (x + 0.044715 * (x * x * x))))


LRU_CARRY = 8


def _lru_kernel(u_ref, ut_ref, cw_ref, cb_ref, wrg_ref, brg_ref, wig_ref, big_ref, lam_ref, y_ref,
                xs_ref, hprev_ref):
    c = pl.program_id(0)
    nb = u_ref.shape[0]
    nrow = nb * LRU_TILE

    @pl.when(c == 0)
    def _():
        xs_ref[...] = jnp.zeros_like(xs_ref)
        hprev_ref[...] = jnp.zeros_like(hprev_ref)

    u_x = u_ref[...].reshape(nrow, UL_W)
    u = jnp.where(c == 0, jnp.concatenate([ut_ref[...]] * nb, axis=0), u_x)
    xl = u[:, :LRU_W]
    gl = u[:, LRU_W:]
    row = jnp.bitwise_and(lax.broadcasted_iota(jnp.int32, (nrow, LRU_W), 0), LRU_TILE - 1)
    in_group = jnp.bitwise_and(row, 7)
    roll_in_group = lambda t, d: pltpu.roll(t.reshape(t.shape[0] // 8, 8, LRU_W), d, 1).reshape(t.shape)
    xl_prev = jnp.concatenate(
        [p for b in range(nb) for p in (xs_ref[b], xl[b * LRU_TILE:(b + 1) * LRU_TILE - 8])], axis=0)
    xc = cb_ref[...] + cw_ref[CONV_WIDTH - 1:CONV_WIDTH, :] * xl
    for d in range(1, CONV_WIDTH):
        tap = jnp.where(in_group >= d, roll_in_group(xl, d), roll_in_group(xl_prev, d))
        xc = xc + cw_ref[CONV_WIDTH - 1 - d:CONV_WIDTH - d, :] * tap
    for b in range(nb):
        xs_ref[b] = xl[(b + 1) * LRU_TILE - 8:(b + 1) * LRU_TILE]

    xcb = xc.astype(BF16)
    gate_r = jax.nn.sigmoid(_dot(xcb, wrg_ref[...]) + brg_ref[...])
    gate_i = jax.nn.sigmoid(_dot(xcb, wig_ref[...]) + big_ref[...])
    lam = lam_ref[...]
    log_sig = -(jnp.maximum(-lam, 0.0) + jnp.log1p(jnp.exp(-jnp.abs(lam))))
    log_a = LRU_C * gate_r * log_sig
    a = jnp.exp(log_a)
    mult = jnp.sqrt(jnp.maximum(1.0 - jnp.exp(2.0 * log_a), 0.0))
    b = mult * gate_i * xc
    b = jnp.where((c == 0) & (row < LRU_TILE - N_META), 0.0, b)

    d = 1
    while d < 8:
        keep = in_group >= d
        a_sh = jnp.where(keep, roll_in_group(a, d), 1.0)
        b_sh = jnp.where(keep, roll_in_group(b, d), 0.0)
        b = a * b_sh + b
        a = a * a_sh
        d *= 2
    groups = []
    for bi in range(nb):
        carry = hprev_ref[bi:bi + 1, :]
        for gi in range(LRU_TILE // 8):
            lo = bi * LRU_TILE + gi * 8
            hg = b[lo:lo + 8] + a[lo:lo + 8] * carry
            carry = hg[7:8, :]
            groups.append(hg)
        hprev_ref[bi:bi + 1, :] = carry
    h = jnp.concatenate(groups, axis=0)
    y_ref[...] = (h * _gelu_tanh(gl)).astype(y_ref.dtype).reshape(y_ref.shape)


def _lru_mixer(ul, ul_tail, params):
    B, T, _ = ul.shape
    assert TAIL == LRU_TILE
    x_map = lambda c: (0, jnp.maximum(c - 1, 0), 0)
    return pl.pallas_call(
        _lru_kernel,
        out_shape=jax.ShapeDtypeStruct((B, T, LRU_W), BF16),
        grid=(T // LRU_TILE + 1,),
        in_specs=[pl.BlockSpec((B, LRU_TILE, UL_W), x_map), _full(ul_tail)] + [_full(p) for p in params],
        out_specs=pl.BlockSpec((B, LRU_TILE, LRU_W), x_map),
        scratch_shapes=[pltpu.VMEM((B, LRU_CARRY, LRU_W), F32), pltpu.VMEM((B, LRU_W), F32)],
        compiler_params=_cparams(("arbitrary",)),
        name="rglru",
    )(ul, ul_tail, *params)


def _route(lg):
    lane = lax.broadcasted_iota(jnp.int32, lg.shape, 1)
    neg = jnp.float32(-jnp.inf)
    rmax = lambda t: jnp.max(t, axis=1, keepdims=True)
    first = lambda hit: jnp.min(jnp.where(hit, lane, LANE), axis=1, keepdims=True)
    is_grp = lane < N_GROUPS
    gl = jnp.where(is_grp, lg, neg)
    gmax = rmax(gl)
    g_sel = first(gl == gmax)
    p_g = 1.0 / jnp.sum(jnp.where(is_grp, jnp.exp(lg - gmax), 0.0), axis=1, keepdims=True)
    ex = lane - N_GROUPS
    in_grp = (ex >= 0) & (ex < N_EXPERTS) & (jnp.right_shift(ex, 3) == g_sel)
    el = jnp.where(in_grp, lg, neg)
    v1 = rmax(el)
    i1 = first(el == v1)
    el2 = jnp.where(lane == i1, neg, el)
    v2 = rmax(el2)
    i2 = first(el2 == v2)
    t = jnp.exp(v2 - v1)
    gate1 = p_g / (1.0 + t)
    gate2 = p_g * t / (1.0 + t)
    e1 = (i1 - N_GROUPS).astype(F32)
    e2 = (i2 - N_GROUPS).astype(F32)
    return jnp.where(lane == 0, e1, jnp.where(lane == 1, e2, jnp.where(lane == 2, gate1, jnp.where(lane == 3, gate2, 0.0))))


SLABS = D_MODEL // LANE


def _store_token_tiles(ref, val):
    n = val.shape[0]
    for s in range(SLABS):
        ref[pl.ds(s, n, stride=SLABS), :] = val[:, s * LANE:(s + 1) * LANE]


def _load_token_slabs(ref, n):
    return [ref[pl.ds(s, n, stride=SLABS), :] for s in range(SLABS)]


def _outproj_kernel(x_ref, yr_ref, yl_ref, g0_ref, b0_ref, wor_ref, wol_ref, g1_ref, b1_ref,
                    wrt_hi_ref, wrt_lo_ref, brt_ref, h1_ref, rt_ref):
    h0 = _layer_norm(x_ref[0], g0_ref[...], b0_ref[...])
    mix = _dot(yr_ref[0], wor_ref[...]) + _dot(yl_ref[0], wol_ref[...])
    h1 = _layer_norm(DEEPNORM_ALPHA * h0 + mix, g1_ref[...], b1_ref[...])
    _store_token_tiles(h1_ref.at[0], h1)
    hi = h1.astype(BF16)
    lo = (h1 - hi.astype(F32)).astype(BF16)
    w_hi = wrt_hi_ref[...]
    lg = _dot(hi, w_hi) + (_dot(hi, wrt_lo_ref[...]) + _dot(lo, w_hi)) + brt_ref[...]
    rt_ref[0] = _route(lg)


def _out_projection(x, y_rwkv, y_lru, ln0_g, ln0_b, wo_r, wo_l, ln1_g, ln1_b, wrt_hi, wrt_lo, brt):
    B, T, D = x.shape
    tm = 512
    rows = lambda w: pl.BlockSpec((1, tm, w), lambda b, i: (b, i, 0))
    return pl.pallas_call(
        _outproj_kernel,
        out_shape=(jax.ShapeDtypeStruct((B, T * SLABS, LANE), F32), jax.ShapeDtypeStruct((B, T, LANE), F32)),
        grid=(B, T // tm),
        in_specs=[rows(D), rows(RWKV_W), rows(LRU_W), _full(ln0_g), _full(ln0_b), _full(wo_r), _full(wo_l),
                  _full(ln1_g), _full(ln1_b), _full(wrt_hi), _full(wrt_lo), _full(brt)],
        out_specs=(pl.BlockSpec((1, tm * SLABS, LANE), lambda b, i: (b, i, 0)), rows(LANE)),
        compiler_params=_cparams(("parallel", "parallel")),
        name="outproj",
    )(x, y_rwkv, y_lru, ln0_g, ln0_b, wo_r, wo_l, ln1_g, ln1_b, wrt_hi, wrt_lo, brt)


def _invert_kernel(pad_lo_ref, pad_hi_ref, dest_ref, out_ref):
    i = pl.program_id(0)

    @pl.when(i == 0)
    def _():
        def zero(j, carry):
            out_ref[j] = 0
            return carry
        for e in range(N_EXPERTS):
            lax.fori_loop(pad_lo_ref[e], pad_hi_ref[e], zero, 0)
        lax.fori_loop(pad_hi_ref[N_EXPERTS - 1], out_ref.shape[0], zero, 0)

    base = i * INVERT_BLOCK

    def body(j, carry):
        out_ref[dest_ref[j]] = base + j
        return carry

    lax.fori_loop(0, INVERT_BLOCK, body, 0, unroll=8)


def _invert_slots(dest, n_slots, pad_lo, pad_hi):
    A = dest.shape[0]
    grid_spec = pltpu.PrefetchScalarGridSpec(
        num_scalar_prefetch=2,
        grid=(A // INVERT_BLOCK,),
        in_specs=[pl.BlockSpec((INVERT_BLOCK,), lambda i, lo, hi: (i,), memory_space=pltpu.SMEM)],
        out_specs=pl.BlockSpec(memory_space=pltpu.SMEM),
    )
    return pl.pallas_call(
        _invert_kernel,
        out_shape=jax.ShapeDtypeStruct((n_slots,), jnp.int32),
        grid_spec=grid_spec,
        compiler_params=_cparams(("arbitrary",)),
        name="invert_slots",
    )(pad_lo, pad_hi, dest)


def _row_copy(src_hbm, src_row, dst_ref, dst_row, sem):
    return pltpu.make_async_copy(src_hbm.at[pl.ds(src_row * SLABS, SLABS), :],
                                 dst_ref.at[pl.ds(dst_row * SLABS, SLABS), :], sem)


def _wait_tiles(src_hbm, dst_ref, sem):
    pltpu.make_async_copy(src_hbm.at[pl.ds(0, dst_ref.shape[0]), :], dst_ref, sem).wait()


def _moe_kernel(te_ref, nv_ref, ra_cur_ref, ra_nxt_ref, h_hbm, wg_ref, wu_ref, wd_ref, o_ref,
                xbuf, sem, wgb_ref, wub_ref, wdb_ref):
    i = pl.program_id(0)
    n = pl.num_programs(0)
    n_valid = nv_ref[0]
    slot = lax.rem(i, 2)
    other = 1 - slot

    def gather(ra_ref, s, j, prio):
        tok = lax.shift_right_logical(ra_ref[j], 1)
        _row_copy(h_hbm, tok, xbuf.at[s], j, sem.at[s]).start(priority=prio)

    @pl.when(i == 0)
    def _():
        def body(jj, carry):
            for u in range(DMA_UNROLL):
                gather(ra_cur_ref, 0, jj * DMA_UNROLL + u, u % 2)
            return carry
        lax.fori_loop(0, MOE_TILE // DMA_UNROLL, body, 0)

    e = te_ref[i]
    e_prev = te_ref[jnp.maximum(i - 1, 0)]

    @pl.when((i == 0) | (e != e_prev))
    def _():
        wgb_ref[...] = wg_ref[0].astype(BF16)
        wub_ref[...] = wu_ref[0].astype(BF16)
        wdb_ref[...] = wd_ref[0].astype(BF16)

    @pl.when(i < n_valid)
    def _():
        _wait_tiles(h_hbm, xbuf.at[slot], sem.at[slot])

        def next_rows(c):
            for j in range(c * MOE_DMA_CHUNK, (c + 1) * MOE_DMA_CHUNK):
                gather(ra_nxt_ref, other, j, j % 2)

        next_rows(0)
        next_rows(1)
        xb = jnp.concatenate(_load_token_slabs(xbuf.at[slot], MOE_TILE), axis=1).astype(BF16)
        next_rows(2)
        hg = _dot(xb, wgb_ref[...])
        next_rows(3)
        hu = _dot(xb, wub_ref[...])
        next_rows(4)
        mid = (hg * jax.nn.sigmoid(hg) * hu).astype(BF16)
        next_rows(5)
        y = _dot(mid, wdb_ref[...])
        next_rows(6)
        _store_token_tiles(o_ref, y)
        next_rows(7)

        @pl.when(i == n - 1)
        def _():
            _wait_tiles(h_hbm, xbuf.at[other], sem.at[other])

    @pl.when(i >= n_valid)
    def _():
        @pl.when(i == n_valid)
        def _():
            _wait_tiles(h_hbm, xbuf.at[slot], sem.at[slot])
        o_ref[...] = jnp.zeros_like(o_ref)


def _moe_experts(h1, row_asg, tile_expert, n_valid, w_gate, w_up, w_down):
    D = D_MODEL
    n_tiles = row_asg.shape[0] // MOE_TILE
    smem_tile = lambda f: pl.BlockSpec((MOE_TILE,), f, memory_space=pltpu.SMEM)
    grid_spec = pltpu.PrefetchScalarGridSpec(
        num_scalar_prefetch=2,
        grid=(n_tiles,),
        in_specs=[smem_tile(lambda i, te, nv: (i,)),
                  smem_tile(lambda i, te, nv: (jnp.minimum(i + 1, n_tiles - 1),)),
                  pl.BlockSpec(memory_space=pl.ANY),
                  pl.BlockSpec((1, D, D_EXPERT), lambda i, te, nv: (te[i], 0, 0)),
                  pl.BlockSpec((1, D, D_EXPERT), lambda i, te, nv: (te[i], 0, 0)),
                  pl.BlockSpec((1, D_EXPERT, D), lambda i, te, nv: (te[i], 0, 0))],
        out_specs=pl.BlockSpec((MOE_TILE * SLABS, LANE), lambda i, te, nv: (i, 0)),
        scratch_shapes=[pltpu.VMEM((2, MOE_TILE * SLABS, LANE), F32), pltpu.SemaphoreType.DMA((2,)),
                        pltpu.VMEM((D, D_EXPERT), BF16), pltpu.VMEM((D, D_EXPERT), BF16),
                        pltpu.VMEM((D_EXPERT, D), BF16)],
    )
    return pl.pallas_call(
        _moe_kernel,
        out_shape=jax.ShapeDtypeStruct((n_tiles * MOE_TILE * SLABS, LANE), F32),
        grid_spec=grid_spec,
        compiler_params=_cparams(("arbitrary",)),
        name="moe_experts",
    )(tile_expert, n_valid, row_asg, row_asg, h1, w_gate, w_up, w_down)


def _combine_kernel(d_cur_ref, d_nxt_ref, h_ref, gate_ref, g_ref, b_ref, y_hbm, o_ref, ybuf, sem):
    i = pl.program_id(0)
    n = pl.num_programs(0)
    slot = lax.rem(i, 2)

    def start_gather(d_ref, s):
        def body(tt, carry):
            for u in range(DMA_UNROLL // TOP_K):
                t = tt * (DMA_UNROLL // TOP_K) + u
                for k in range(TOP_K):
                    _row_copy(y_hbm, d_ref[TOP_K * t + k], ybuf.at[s, k], t, sem.at[s]).start(priority=k % 2)
            return carry
        lax.fori_loop(0, COMBINE_TILE * TOP_K // DMA_UNROLL, body, 0)

    @pl.when(i == 0)
    def _():
        start_gather(d_cur_ref, 0)

    @pl.when(i + 1 < n)
    def _():
        start_gather(d_nxt_ref, 1 - slot)

    for k in range(TOP_K):
        _wait_tiles(y_hbm, ybuf.at[slot, k], sem.at[slot])

    gate = gate_ref[...]
    tm = COMBINE_TILE
    ga = jnp.broadcast_to(gate[:, 0:1], (tm, LANE))
    gb = jnp.broadcast_to(gate[:, 1:2], (tm, LANE))
    hs = _load_token_slabs(h_ref, tm)
    ya = _load_token_slabs(ybuf.at[slot, 0], tm)
    yb = _load_token_slabs(ybuf.at[slot, 1], tm)
    z = [DEEPNORM_ALPHA * h + (ga * a + gb * b) for h, a, b in zip(hs, ya, yb)]
    inv_d = 1.0 / D_MODEL
    mu = sum(jnp.sum(t, axis=1, keepdims=True) for t in z) * inv_d
    zc = [t - mu for t in z]
    var = sum(jnp.sum(t * t, axis=1, keepdims=True) for t in zc) * inv_d
    rstd = lax.rsqrt(var + LN_EPS)
    for s in range(SLABS):
        cols = slice(s * LANE, (s + 1) * LANE)
        o_ref[:, cols] = zc[s] * rstd * g_ref[:, cols] + b_ref[:, cols]


def _combine(h1, ybuf, dest, gates, ln2_g, ln2_b):
    D = D_MODEL
    M = h1.shape[0] // SLABS
    tm = COMBINE_TILE
    n = M // tm
    smem_tile = lambda f: pl.BlockSpec((TOP_K * tm,), f, memory_space=pltpu.SMEM)
    return pl.pallas_call(
        _combine_kernel,
        out_shape=jax.ShapeDtypeStruct((M, D), F32),
        grid=(n,),
        in_specs=[smem_tile(lambda i: (i,)), smem_tile(lambda i: (jnp.minimum(i + 1, n - 1),)),
                  pl.BlockSpec((tm * SLABS, LANE), lambda i: (i, 0)), pl.BlockSpec((tm, TOP_K), lambda i: (i, 0)),
                  _full(ln2_g), _full(ln2_b), pl.BlockSpec(memory_space=pl.ANY)],
        out_specs=pl.BlockSpec((tm, D), lambda i: (i, 0)),
        scratch_shapes=[pltpu.VMEM((2, TOP_K, tm * SLABS, LANE), F32), pltpu.SemaphoreType.DMA((2,))],
        compiler_params=_cparams(("arbitrary",)),
        name="combine",
    )(dest, dest, h1, gates, ln2_g, ln2_b, ybuf)


def _routing_plan(route):
    M = route.shape[0]
    eid = route[:, :TOP_K].astype(jnp.int32).reshape(-1)
    gates = route[:, TOP_K:2 * TOP_K]
    A = M * TOP_K
    onehot = (eid[:, None] == jnp.arange(N_EXPERTS, dtype=eid.dtype)[None, :]).astype(jnp.int32)
    csum = jnp.cumsum(onehot, axis=0)
    rank = jnp.sum(csum * onehot, axis=1) - 1
    counts = csum[-1]
    pcounts = (counts + MOE_TILE - 1) // MOE_TILE * MOE_TILE
    pends = jnp.cumsum(pcounts)
    pstarts = pends - pcounts
    dest = (jnp.sum(onehot * pstarts[None, :], axis=1) + rank).astype(jnp.int32)
    n_tiles = (A + N_EXPERTS * (MOE_TILE - 1) + MOE_TILE - 1) // MOE_TILE
    n_valid = (pends[-1] // MOE_TILE).astype(jnp.int32)
    tile_start = jnp.minimum(jnp.arange(n_tiles, dtype=jnp.int32) * MOE_TILE, pends[-1] - 1)
    tile_expert = jnp.sum((pends[None, :] <= tile_start[:, None]).astype(jnp.int32), axis=1)
    tile_expert = jnp.minimum(tile_expert, N_EXPERTS - 1).astype(jnp.int32)
    pad_lo = (pstarts + counts).astype(jnp.int32)
    pad_hi = pends.astype(jnp.int32)
    return gates, dest, n_tiles * MOE_TILE, tile_expert, n_valid.reshape(1), pad_lo, pad_hi


def kernel(x, meta, ln0_g, ln0_b, w_in, mu_shift, w0, w_decay_up, a0, w_a_up, w_g_up, k_k, k_a, r_k, gn_g, gn_b, conv_w, conv_b, w_rg, b_rg, w_ig, b_ig, lru_lambda, w_out, ln1_g, ln1_b, w_router_grp, b_router_grp, w_router_exp, b_router_exp, w_exp_gate, w_exp_up, w_exp_down, ln2_g, ln2_b):
    B, T, D = x.shape
    assert D == D_MODEL and T % 512 == 0 and w_in.shape[0] == 1
    assert (B * T * TOP_K) % INVERT_BLOCK == 0
    row = lambda p: p.reshape(1, -1).astype(F32)
    n_rw = 3 * RWKV_W
    w_in0 = w_in[0]

    def slots(p):
        pad = lambda a, n: jnp.pad(a, [(0, 0)] * (a.ndim - 1) + [(0, n - a.shape[-1])])
        zw = p[..., n_rw:n_rw + DECAY_RANK]
        za = p[..., n_rw + DECAY_RANK:n_rw + DECAY_RANK + AAA_RANK]
        zg = p[..., n_rw + DECAY_RANK + AAA_RANK:n_rw + DECAY_RANK + AAA_RANK + GATE_RANK]
        return jnp.concatenate([p[..., :n_rw], pad(zw, LANE), pad(za, LANE), pad(zg, ZG_SLOT)], axis=-1)

    rwkv_cols = n_rw + DECAY_RANK + AAA_RANK + GATE_RANK
    w_r = slots(w_in0[:, :rwkv_cols]).astype(BF16)
    w_l = w_in0[:, rwkv_cols:].astype(BF16)
    ur, ul, ur_t, ul_t = _in_projection(x, meta, row(ln0_g), row(ln0_b), w_r, w_l)

    pad_rows = lambda a, n: jnp.pad(a, ((0, n - a.shape[0]), (0, 0)))
    rwkv_params = (slots(mu_shift[0][None, :]).astype(F32), row(w0[0]), pad_rows(w_decay_up[0], LANE).astype(BF16),
                   row(a0[0]), pad_rows(w_a_up[0], LANE).astype(BF16), pad_rows(w_g_up[0], ZG_SLOT).astype(BF16),
                   row(k_k[0]), row(k_a[0]), row(r_k[0]), row(gn_g[0]), row(gn_b[0]))
    y_rwkv = _rwkv_pipe_mixer(ur, ur_t, rwkv_params)

    blockdiag = lambda w: jax.scipy.linalg.block_diag(*[w[i] for i in range(LRU_BLOCKS)]).astype(BF16)
    lru_params = (conv_w[0], row(conv_b[0]), blockdiag(w_rg[0]), row(b_rg[0]), blockdiag(w_ig[0]), row(b_ig[0]),
                  row(lru_lambda[0]))
    y_lru = _lru_mixer(ul, ul_t, lru_params)

    w_rt = jnp.concatenate([w_router_grp[0], w_router_exp[0]], axis=1)
    w_rt = jnp.pad(w_rt, ((0, 0), (0, LANE - w_rt.shape[1])))
    wrt_hi = w_rt.astype(BF16)
    wrt_lo = (w_rt - wrt_hi.astype(F32)).astype(BF16)
    b_rt = jnp.concatenate([b_router_grp[0], b_router_exp[0]])
    b_rt = jnp.pad(b_rt, (0, LANE - b_rt.shape[0])).reshape(1, LANE)
    wo = w_out[0].astype(BF16)
    h1, route = _out_projection(x, y_rwkv, y_lru, row(ln0_g), row(ln0_b), wo[:RWKV_W], wo[RWKV_W:],
                                row(ln1_g[0]), row(ln1_b[0]), wrt_hi, wrt_lo, b_rt)

    M = B * T
    h1 = h1.reshape(M * SLABS, LANE)
    gates, dest, n_slots, tile_expert, n_valid, pad_lo, pad_hi = _routing_plan(route.reshape(M, LANE))
    row_asg = _invert_slots(dest, n_slots, pad_lo, pad_hi)
    ybuf = _moe_experts(h1, row_asg, tile_expert, n_valid, w_exp_gate[0], w_exp_up[0], w_exp_down[0])
    out = _combine(h1, ybuf, dest, gates, row(ln2_g[0]), row(ln2_b[0]))
    return out.reshape(B, T, D)
```

```python
import math

import jax
import jax.numpy as jnp
from jax import lax
from jax.experimental import pallas as pl
from jax.experimental.pallas import tpu as pltpu

F32 = jnp.float32
BF16 = jnp.bfloat16

D_MODEL = 1024
N_META = 16
RWKV_W = 512
RWKV_HEAD = 64
DECAY_RANK = 64
AAA_RANK = 64
GATE_RANK = 160
LRU_W = 512
LRU_BLOCKS = 8
CONV_WIDTH = 4
LRU_C = 8.0
N_GROUPS = 4
EXPERTS_PER_GROUP = 8
N_EXPERTS = N_GROUPS * EXPERTS_PER_GROUP
TOP_K = 2
D_EXPERT = 512
LN_EPS = 1e-5
GN_EPS = 64e-5
DEEPNORM_ALPHA = 2.0 ** 0.25

LANE = 128
OFF_R, OFF_K, OFF_V = 0, RWKV_W, 2 * RWKV_W
OFF_ZW = 3 * RWKV_W
OFF_ZA = OFF_ZW + LANE
OFF_ZG = OFF_ZA + LANE
ZG_SLOT = 2 * LANE
UR_W = OFF_ZG + ZG_SLOT
UL_W = 2 * LRU_W

TAIL = 256
CHUNK = 64
HEADS_PER_GROUP = 4
GW = HEADS_PER_GROUP * RWKV_HEAD
N_HGROUPS = RWKV_W // GW
LRU_TILE = TAIL
MOE_TILE = 256
COMBINE_TILE = 256
INVERT_BLOCK = 4096
DMA_UNROLL = 8
V7X_VMEM_BYTES = 64 * 1024 * 1024
VMEM_LIMIT = V7X_VMEM_BYTES - 8 * 1024 * 1024


def _cparams(sem):
    return pltpu.CompilerParams(dimension_semantics=sem, vmem_limit_bytes=VMEM_LIMIT)


def _layer_norm(x, g, b):
    mu = jnp.mean(x, -1, keepdims=True)
    xc = x - mu
    var = jnp.mean(xc * xc, -1, keepdims=True)
    return xc * lax.rsqrt(var + LN_EPS) * g + b


def _dot(a, b):
    return jnp.dot(a, b, preferred_element_type=F32)


def _dot_nt(a, b):
    return lax.dot_general(a, b, (((1,), (1,)), ((), ())), preferred_element_type=F32)


def _dot_tn(a, b):
    return lax.dot_general(a, b, (((0,), (0,)), ((), ())), preferred_element_type=F32)


def _full(a):
    return pl.BlockSpec(a.shape, lambda *_: (0,) * a.ndim)


def _inproj_kernel(x_ref, g_ref, b_ref, wr_ref, wl_ref, ur_ref, ul_ref):
    h = _layer_norm(x_ref[0], g_ref[...], b_ref[...]).astype(BF16)
    ur_ref[0] = _dot(h, wr_ref[...])
    ul_ref[0] = _dot(h, wl_ref[...])


def _inproj_tail_kernel(x_ref, g_ref, b_ref, wr_ref, wl_ref, ur_ref, ul_ref):
    h = _layer_norm(x_ref[...], g_ref[...], b_ref[...]).astype(BF16)
    rows = lax.broadcasted_iota(jnp.int32, (TAIL, 1), 0)
    valid = (rows >= TAIL - N_META).astype(F32)
    ur_ref[...] = _dot(h, wr_ref[...]) * valid
    ul_ref[...] = _dot(h, wl_ref[...]) * valid


def _in_projection(x, meta, ln0_g, ln0_b, w_r, w_l):
    B, T, D = x.shape
    tm = 512
    ur, ul = pl.pallas_call(
        _inproj_kernel,
        out_shape=(jax.ShapeDtypeStruct((B, T, UR_W), F32), jax.ShapeDtypeStruct((B, T, UL_W), F32)),
        grid=(B, T // tm),
        in_specs=[pl.BlockSpec((1, tm, D), lambda b, i: (b, i, 0)), _full(ln0_g), _full(ln0_b), _full(w_r), _full(w_l)],
        out_specs=(pl.BlockSpec((1, tm, UR_W), lambda b, i: (b, i, 0)),
                   pl.BlockSpec((1, tm, UL_W), lambda b, i: (b, i, 0))),
        compiler_params=_cparams(("parallel", "parallel")),
        name="inproj",
    )(x, ln0_g, ln0_b, w_r, w_l)
    tail_x = jnp.concatenate([jnp.zeros((TAIL - N_META, D), F32), meta.astype(F32)], axis=0)
    ur_t, ul_t = pl.pallas_call(
        _inproj_tail_kernel,
        out_shape=(jax.ShapeDtypeStruct((TAIL, UR_W), F32), jax.ShapeDtypeStruct((TAIL, UL_W), F32)),
        grid=(1,),
        in_specs=[_full(tail_x), _full(ln0_g), _full(ln0_b), _full(w_r), _full(w_l)],
        out_specs=(pl.BlockSpec((TAIL, UR_W), lambda i: (0, 0)), pl.BlockSpec((TAIL, UL_W), lambda i: (0, 0))),
        compiler_params=_cparams(("arbitrary",)),
        name="inproj_tail",
    )(tail_x, ln0_g, ln0_b, w_r, w_l)
    return ur, ul, ur_t, ul_t


def _rwkv_pipe_kernel(u_ref, ut_ref, mu_ref, w0_ref, wdu_ref, a0_ref, wau_ref, wgu_ref, kk_ref, ka_ref, rk_ref,
                      gng_ref, gnb_ref, bones_ref, bm_ref, eye_ref, msl_ref, mil_ref,
                      m8_ref, m16_ref, m32_ref, m64_ref, y_ref,
                      s_ref, prev_ref, y0_s, q_s, mc_s, nc_s, we_s, bonus_s, g_s, yraw_s):
    s_id = pl.program_id(0)
    nb = u_ref.shape[0]
    blk = u_ref.shape[1]
    npc = blk // CHUNK
    nseq = nb * npc
    seq_rows = lambda q: slice(q * CHUNK, (q + 1) * CHUNK)
    per_seq = lambda f: jnp.concatenate([f(q) for q in range(nseq)], axis=0)
    w_slot = lax.rem(s_id, 2)
    r_slot = 1 - w_slot

    @pl.when(s_id == 0)
    def _():
        s_ref[...] = jnp.zeros_like(s_ref)
        prev_ref[...] = jnp.zeros_like(prev_ref)
        for ref in (y0_s, q_s, mc_s, nc_s, we_s, bonus_s, g_s):
            ref[1] = jnp.zeros(ref.shape[1:], ref.dtype)

    u_x = u_ref[...].reshape(nb * blk, UR_W)
    u = jnp.where(s_id == 0, jnp.concatenate([ut_ref[...]] * nb, axis=0), u_x)
    row = lax.broadcasted_iota(jnp.int32, u.shape, 0)
    prev_rows = jnp.concatenate([jnp.broadcast_to(prev_ref[b:b + 1, :], (blk, UR_W)) for b in range(nb)], axis=0)
    u_prev = jnp.where(jnp.bitwise_and(row, blk - 1) == 0, prev_rows, pltpu.roll(u, 1, 0))
    for b in range(nb):
        prev_ref[b:b + 1, :] = u[(b + 1) * blk - 1:(b + 1) * blk, :]
    x = u + (u_prev - u) * mu_ref[...]
    r = x[:, OFF_R:OFF_R + RWKV_W]
    k = x[:, OFF_K:OFF_K + RWKV_W]
    v = x[:, OFF_V:OFF_V + RWKV_W]
    zw = x[:, OFF_ZW:OFF_ZW + LANE]
    za = x[:, OFF_ZA:OFF_ZA + LANE]
    zg = x[:, OFF_ZG:OFF_ZG + ZG_SLOT]

    b16 = lambda t: t.astype(BF16)
    bones = bones_ref[...]
    head_sum = lambda t: _dot(b16(t), bones)

    z = w0_ref[...] + _dot(b16(jnp.tanh(zw)), wdu_ref[...])
    logw = -math.exp(-0.5) * jax.nn.sigmoid(z)
    a = jax.nn.sigmoid(a0_ref[...] + _dot(b16(za), wau_ref[...]))
    g = _dot(b16(jax.nn.sigmoid(zg)), wgu_ref[...])
    kk = k * kk_ref[...]
    kk = kk / jnp.maximum(jnp.sqrt(head_sum(kk * kk)), 1e-12)
    k = k * (1.0 + (a - 1.0) * ka_ref[...])
    kka = kk * a
    bonus_s[w_slot] = head_sum(r * k * rk_ref[...]) * v
    g_s[w_slot] = g

    cl = logw
    row_in_chunk = jnp.bitwise_and(lax.broadcasted_iota(jnp.int32, cl.shape, 0), CHUNK - 1)
    d = 1
    while d < CHUNK:
        cl = cl + jnp.where(row_in_chunk >= d, pltpu.roll(cl, d, 0), 0.0)
        d *= 2
    cl_last = per_seq(lambda q: jnp.broadcast_to(cl[(q + 1) * CHUNK - 1:(q + 1) * CHUNK, :], (CHUNK, RWKV_W)))
    e_neg = jnp.exp(-cl)
    e_end = jnp.exp(cl_last - cl)
    rt = r * jnp.exp(cl)
    kt = k * e_neg
    at = -kk * jnp.exp(cl - logw)
    bt = kka * e_neg
    kw = k * e_end
    bw = kka * e_end
    w_end = jnp.exp(cl_last)
    for q in range(nseq):
        we_s[w_slot, q * 8:(q + 1) * 8, :] = w_end[q * CHUNK:q * CHUNK + 8, :]

    bm = bm_ref[...]
    bm16 = b16(bm)
    eye = eye_ref[...]
    msl = msl_ref[...]
    mil = mil_ref[...]
    tile4 = lambda t: jnp.concatenate([t] * HEADS_PER_GROUP, axis=0)
    fold4 = lambda t: sum(t[i * CHUNK:(i + 1) * CHUNK] for i in range(HEADS_PER_GROUP))
    bd = lambda t: tile4(b16(t)) * bm16
    rows2 = lambda x, y: jnp.concatenate([x, y], axis=0)
    cols2 = lambda x, y: jnp.concatenate([x, y], axis=1)

    probs = [(seq_rows(q), slice(hg * GW, (hg + 1) * GW)) for q in range(nseq) for hg in range(N_HGROUPS)]
    each = lambda f, *ls: [f(*xs) for xs in zip(*ls)]
    pick = lambda t: [t[rq, sl] for rq, sl in probs]
    at_w, rt_w, v_w = pick(at), pick(rt), pick(v)
    lhs = each(lambda x, y: b16(rows2(x, y)), at_w, rt_w)
    ab = each(_dot_nt, lhs, each(bd, pick(bt)))
    ak = each(_dot_nt, lhs, each(bd, pick(kt)))
    a_ab = each(lambda t: t[:CHUNK] * msl, ab)
    a_rb = each(lambda t: b16(t[CHUNK:] * mil), ab)
    a_xk = each(lambda t: b16(rows2(t[:CHUNK] * msl, t[CHUNK:] * mil)), ak)

    a0 = each(lambda t: b16(t * m8_ref[...]), a_ab)
    a2 = each(lambda t: b16(_dot(t, bd(t))), a0)
    a4 = each(lambda t: b16(_dot(t, bd(t))), a2)
    p1 = each(lambda t: eye + t.astype(F32), a0)
    p1 = each(lambda p, t: p + _dot(b16(p), bd(t)), p1, a2)
    tt = each(lambda p, t: p + _dot(b16(p), bd(t)), p1, a4)
    for m_ref in (m16_ref, m32_ref, m64_ref):
        tb = each(b16, tt)
        off = each(lambda t: b16(t * m_ref[...]), a_ab)
        half = each(lambda x, y: b16(_dot(x, bd(y))), tb, off)
        tt = each(lambda t, x, y: t + _dot(x, bd(y)), tt, half, tb)
    tb = each(b16, tt)

    xv = each(lambda x, y: _dot(x, bd(y)), a_xk, v_w)
    u0 = each(lambda x, y: _dot(x, bd(y[:CHUNK])), tb, xv)
    ta = each(lambda x, y: _dot(x, bd(y)), tb, at_w)
    y0 = each(lambda x, y, z: _dot(x, bd(y)) + z[CHUNK:], a_rb, u0, xv)
    qq = each(lambda x, y, z: x + _dot(y, bd(z)), rt_w, a_rb, ta)
    left = each(lambda x, y, z: b16(rows2(cols2(x, y), cols2(jnp.zeros_like(z), z))), ta, u0, v_w)
    right = each(lambda x, y: b16(rows2(x, y)), pick(bw), pick(kw))
    mn = each(_dot_tn, left, right)
    for i, (rq, sl) in enumerate(probs):
        y0_s[w_slot, rq, sl] = y0[i]
        q_s[w_slot, rq, sl] = b16(qq[i])
        mc_s[w_slot, rq, sl] = b16(fold4(mn[i][:GW] * bm))
        nc_s[w_slot, rq, sl] = fold4(mn[i][GW:] * bm)

    for b in range(nb):
        for hg in range(N_HGROUPS):
            sl = slice(hg * GW, (hg + 1) * GW)
            s = s_ref[b, hg]
            for j in range(npc):
                q = b * npc + j
                rq = seq_rows(q)
                yraw_s[rq, sl] = y0_s[r_slot, rq, sl] + _dot_nt(q_s[r_slot, rq, sl], bd(s))
                s = (s * we_s[r_slot, q * 8:q * 8 + 1, sl] + _dot(b16(s), bd(mc_s[r_slot, rq, sl]))
                     + nc_s[r_slot, rq, sl])
            s_ref[b, hg] = s

    y = yraw_s[...]
    inv_n = 1.0 / RWKV_HEAD
    ym = head_sum(y) * inv_n
    yc = y - ym
    yv = head_sum(yc * yc) * inv_n
    yn = yc * lax.rsqrt(yv + GN_EPS) * gng_ref[...] + gnb_ref[...]
    y_ref[...] = ((yn + bonus_s[r_slot]) * g_s[r_slot]).astype(y_ref.dtype).reshape(y_ref.shape)


def _rwkv_masks():
    f = lambda m: m.astype(F32)
    i = jnp.arange(GW)[:, None]
    j = jnp.arange(GW)[None, :]
    bm = f((i // RWKV_HEAD) == (j // RWKV_HEAD))
    t = jnp.arange(CHUNK)[:, None]
    s = (jnp.arange(GW) % CHUNK)[None, :]
    same = lambda n: (t // n) == (s // n)
    msl = f(t > s)
    mil = f(t >= s)
    m8 = f(same(8))
    m16 = f(same(16) & ~same(8))
    m32 = f(same(32) & ~same(16))
    m64 = f(~same(32))
    eye = f(t == s)
    hi = jnp.arange(RWKV_W)
    bones = ((hi[:, None] // RWKV_HEAD) == (hi[None, :] // RWKV_HEAD)).astype(BF16)
    return bones, bm, eye, msl, mil, m8, m16, m32, m64


def _rwkv_pipe_mixer(ur, ur_tail, params):
    B, T, _ = ur.shape
    blk = TAIL
    assert T % blk == 0 and blk % CHUNK == 0 and CHUNK == RWKV_HEAD
    n_blocks = T // blk
    rows = B * blk
    consts = _rwkv_masks()
    in_map = lambda s: (0, jnp.clip(s - 1, 0, n_blocks - 1), 0)
    out_map = lambda s: (0, jnp.clip(s - 2, 0, n_blocks - 1), 0)
    slot2 = lambda w, dt: pltpu.VMEM((2, rows, w), dt)
    return pl.pallas_call(
        _rwkv_pipe_kernel,
        out_shape=jax.ShapeDtypeStruct((B, T, RWKV_W), BF16),
        grid=(n_blocks + 2,),
        in_specs=[pl.BlockSpec((B, blk, UR_W), in_map), _full(ur_tail)]
                 + [_full(p) for p in params] + [_full(m) for m in consts],
        out_specs=pl.BlockSpec((B, blk, RWKV_W), out_map),
        scratch_shapes=[pltpu.VMEM((B, N_HGROUPS, CHUNK, GW), F32), pltpu.VMEM((B, UR_W), F32),
                        slot2(RWKV_W, F32), slot2(RWKV_W, BF16), slot2(RWKV_W, BF16), slot2(RWKV_W, F32),
                        pltpu.VMEM((2, 8 * rows // CHUNK, RWKV_W), F32), slot2(RWKV_W, F32), slot2(RWKV_W, F32),
                        pltpu.VMEM((rows, RWKV_W), F32)],
        compiler_params=_cparams(("arbitrary",)),
        name="rwkv7",
    )(ur, ur_tail, *params, *consts)


def _gelu_tanh(x):
    return 0.5 * x * (1.0 + jnp.tanh(math.sqrt(2.0 / math.pi) * (x + 0.044715 * (x * x * x))))


LRU_CARRY = 8


def _lru_kernel(u_ref, ut_ref, cw_ref, cb_ref, wrg_ref, brg_ref, wig_ref, big_ref, lam_ref, y_ref,
                xs_ref, hprev_ref):
    c = pl.program_id(0)
    nb = u_ref.shape[0]
    nrow = nb * LRU_TILE

    @pl.when(c == 0)
    def _():
        xs_ref[...] = jnp.zeros_like(xs_ref)
        hprev_ref[...] = jnp.zeros_like(hprev_ref)

    u_x = u_ref[...].reshape(nrow, UL_W)
    u = jnp.where(c == 0, jnp.concatenate([ut_ref[...]] * nb, axis=0), u_x)
    xl = u[:, :LRU_W]
    gl = u[:, LRU_W:]
    row = jnp.bitwise_and(lax.broadcasted_iota(jnp.int32, (nrow, LRU_W), 0), LRU_TILE - 1)
    in_group = jnp.bitwise_and(row, 7)
    roll_in_group = lambda t, d: pltpu.roll(t.reshape(t.shape[0] // 8, 8, LRU_W), d, 1).reshape(t.shape)
    xl_prev = jnp.concatenate(
        [p for b in range(nb) for p in (xs_ref[b], xl[b * LRU_TILE:(b + 1) * LRU_TILE - 8])], axis=0)
    xc = cb_ref[...] + cw_ref[CONV_WIDTH - 1:CONV_WIDTH, :] * xl
    for d in range(1, CONV_WIDTH):
        tap = jnp.where(in_group >= d, roll_in_group(xl, d), roll_in_group(xl_prev, d))
        xc = xc + cw_ref[CONV_WIDTH - 1 - d:CONV_WIDTH - d, :] * tap
    for b in range(nb):
        xs_ref[b] = xl[(b + 1) * LRU_TILE - 8:(b + 1) * LRU_TILE]

    xcb = xc.astype(BF16)
    gate_r = jax.nn.sigmoid(_dot(xcb, wrg_ref[...]) + brg_ref[...])
    gate_i = jax.nn.sigmoid(_dot(xcb, wig_ref[...]) + big_ref[...])
    lam = lam_ref[...]
    log_sig = -(jnp.maximum(-lam, 0.0) + jnp.log1p(jnp.exp(-jnp.abs(lam))))
    log_a = LRU_C * gate_r * log_sig
    a = jnp.exp(log_a)
    mult = jnp.sqrt(jnp.maximum(1.0 - jnp.exp(2.0 * log_a), 0.0))
    b = mult * gate_i * xc
    b = jnp.where((c == 0) & (row < LRU_TILE - N_META), 0.0, b)

    d = 1
    while d < 8:
        keep = in_group >= d
        a_sh = jnp.where(keep, roll_in_group(a, d), 1.0)
        b_sh = jnp.where(keep, roll_in_group(b, d), 0.0)
        b = a * b_sh + b
        a = a * a_sh
        d *= 2
    groups = []
    for bi in range(nb):
        carry = hprev_ref[bi:bi + 1, :]
        for gi in range(LRU_TILE // 8):
            lo = bi * LRU_TILE + gi * 8
            hg = b[lo:lo + 8] + a[lo:lo + 8] * carry
            carry = hg[7:8, :]
            groups.append(hg)
        hprev_ref[bi:bi + 1, :] = carry
    h = jnp.concatenate(groups, axis=0)
    y_ref[...] = (h * _gelu_tanh(gl)).astype(y_ref.dtype).reshape(y_ref.shape)


def _lru_mixer(ul, ul_tail, params):
    B, T, _ = ul.shape
    assert TAIL == LRU_TILE
    x_map = lambda c: (0, jnp.maximum(c - 1, 0), 0)
    return pl.pallas_call(
        _lru_kernel,
        out_shape=jax.ShapeDtypeStruct((B, T, LRU_W), BF16),
        grid=(T // LRU_TILE + 1,),
        in_specs=[pl.BlockSpec((B, LRU_TILE, UL_W), x_map), _full(ul_tail)] + [_full(p) for p in params],
        out_specs=pl.BlockSpec((B, LRU_TILE, LRU_W), x_map),
        scratch_shapes=[pltpu.VMEM((B, LRU_CARRY, LRU_W), F32), pltpu.VMEM((B, LRU_W), F32)],
        compiler_params=_cparams(("arbitrary",)),
        name="rglru",
    )(ul, ul_tail, *params)


def _route(lg):
    lane = lax.broadcasted_iota(jnp.int32, lg.shape, 1)
    neg = jnp.float32(-jnp.inf)
    rmax = lambda t: jnp.max(t, axis=1, keepdims=True)
    first = lambda hit: jnp.min(jnp.where(hit, lane, LANE), axis=1, keepdims=True)
    is_grp = lane < N_GROUPS
    gl = jnp.where(is_grp, lg, neg)
    gmax = rmax(gl)
    g_sel = first(gl == gmax)
    p_g = 1.0 / jnp.sum(jnp.where(is_grp, jnp.exp(lg - gmax), 0.0), axis=1, keepdims=True)
    ex = lane - N_GROUPS
    in_grp = (ex >= 0) & (ex < N_EXPERTS) & (jnp.right_shift(ex, 3) == g_sel)
    el = jnp.where(in_grp, lg, neg)
    v1 = rmax(el)
    i1 = first(el == v1)
    el2 = jnp.where(lane == i1, neg, el)
    v2 = rmax(el2)
    i2 = first(el2 == v2)
    t = jnp.exp(v2 - v1)
    gate1 = p_g / (1.0 + t)
    gate2 = p_g * t / (1.0 + t)
    e1 = (i1 - N_GROUPS).astype(F32)
    e2 = (i2 - N_GROUPS).astype(F32)
    return jnp.where(lane == 0, e1, jnp.where(lane == 1, e2, jnp.where(lane == 2, gate1, jnp.where(lane == 3, gate2, 0.0))))


SLABS = D_MODEL // LANE


def _store_token_tiles(ref, val):
    n = val.shape[0]
    for s in range(SLABS):
        ref[pl.ds(s, n, stride=SLABS), :] = val[:, s * LANE:(s + 1) * LANE]


def _load_token_slabs(ref, n):
    return [ref[pl.ds(s, n, stride=SLABS), :] for s in range(SLABS)]


def _outproj_kernel(x_ref, yr_ref, yl_ref, g0_ref, b0_ref, wor_ref, wol_ref, g1_ref, b1_ref,
                    wrt_hi_ref, wrt_lo_ref, brt_ref, h1_ref, rt_ref):
    h0 = _layer_norm(x_ref[0], g0_ref[...], b0_ref[...])
    mix = _dot(yr_ref[0], wor_ref[...]) + _dot(yl_ref[0], wol_ref[...])
    h1 = _layer_norm(DEEPNORM_ALPHA * h0 + mix, g1_ref[...], b1_ref[...])
    _store_token_tiles(h1_ref.at[0], h1)
    hi = h1.astype(BF16)
    lo = (h1 - hi.astype(F32)).astype(BF16)
    w_hi = wrt_hi_ref[...]
    lg = _dot(hi, w_hi) + (_dot(hi, wrt_lo_ref[...]) + _dot(lo, w_hi)) + brt_ref[...]
    rt_ref[0] = _route(lg)


def _out_projection(x, y_rwkv, y_lru, ln0_g, ln0_b, wo_r, wo_l, ln1_g, ln1_b, wrt_hi, wrt_lo, brt):
    B, T, D = x.shape
    tm = 512
    rows = lambda w: pl.BlockSpec((1, tm, w), lambda b, i: (b, i, 0))
    return pl.pallas_call(
        _outproj_kernel,
        out_shape=(jax.ShapeDtypeStruct((B, T * SLABS, LANE), F32), jax.ShapeDtypeStruct((B, T, LANE), F32)),
        grid=(B, T // tm),
        in_specs=[rows(D), rows(RWKV_W), rows(LRU_W), _full(ln0_g), _full(ln0_b), _full(wo_r), _full(wo_l),
                  _full(ln1_g), _full(ln1_b), _full(wrt_hi), _full(wrt_lo), _full(brt)],
        out_specs=(pl.BlockSpec((1, tm * SLABS, LANE), lambda b, i: (b, i, 0)), rows(LANE)),
        compiler_params=_cparams(("parallel", "parallel")),
        name="outproj",
    )(x, y_rwkv, y_lru, ln0_g, ln0_b, wo_r, wo_l, ln1_g, ln1_b, wrt_hi, wrt_lo, brt)


def _invert_kernel(pad_lo_ref, pad_hi_ref, dest_ref, out_ref):
    i = pl.program_id(0)

    @pl.when(i == 0)
    def _():
        def zero(j, carry):
            out_ref[j] = 0
            return carry
        for e in range(N_EXPERTS):
            lax.fori_loop(pad_lo_ref[e], pad_hi_ref[e], zero, 0)
        lax.fori_loop(pad_hi_ref[N_EXPERTS - 1], out_ref.shape[0], zero, 0)

    base = i * INVERT_BLOCK

    def body(j, carry):
        out_ref[dest_ref[j]] = base + j
        return carry

    lax.fori_loop(0, INVERT_BLOCK, body, 0, unroll=8)


def _invert_slots(dest, n_slots, pad_lo, pad_hi):
    A = dest.shape[0]
    grid_spec = pltpu.PrefetchScalarGridSpec(
        num_scalar_prefetch=2,
        grid=(A // INVERT_BLOCK,),
        in_specs=[pl.BlockSpec((INVERT_BLOCK,), lambda i, lo, hi: (i,), memory_space=pltpu.SMEM)],
        out_specs=pl.BlockSpec(memory_space=pltpu.SMEM),
    )
    return pl.pallas_call(
        _invert_kernel,
        out_shape=jax.ShapeDtypeStruct((n_slots,), jnp.int32),
        grid_spec=grid_spec,
        compiler_params=_cparams(("arbitrary",)),
        name="invert_slots",
    )(pad_lo, pad_hi, dest)


def _row_copy(src_hbm, src_row, dst_ref, dst_row, sem):
    return pltpu.make_async_copy(src_hbm.at[pl.ds(src_row * SLABS, SLABS), :],
                                 dst_ref.at[pl.ds(dst_row * SLABS, SLABS), :], sem)


def _wait_tiles(src_hbm, dst_ref, sem):
    pltpu.make_async_copy(src_hbm.at[pl.ds(0, dst_ref.shape[0]), :], dst_ref, sem).wait()


def _moe_kernel(te_ref, nv_ref, ra_cur_ref, ra_nxt_ref, h_hbm, wg_ref, wu_ref, wd_ref, o_ref,
                xbuf, sem, wgb_ref, wub_ref, wdb_ref):
    i = pl.program_id(0)
    n_valid = nv_ref[0]
    slot = lax.rem(i, 2)

    def start_gather(ra_ref, s):
        def body(jj, carry):
            for u in range(DMA_UNROLL):
                j = jj * DMA_UNROLL + u
                tok = lax.shift_right_logical(ra_ref[j], 1)
                _row_copy(h_hbm, tok, xbuf.at[s], j, sem.at[s]).start(priority=u % 2)
            return carry
        lax.fori_loop(0, MOE_TILE // DMA_UNROLL, body, 0)

    @pl.when(i == 0)
    def _():
        start_gather(ra_cur_ref, 0)

    @pl.when(i + 1 < n_valid)
    def _():
        start_gather(ra_nxt_ref, 1 - slot)

    e = te_ref[i]
    e_prev = te_ref[jnp.maximum(i - 1, 0)]

    @pl.when((i == 0) | (e != e_prev))
    def _():
        wgb_ref[...] = wg_ref[0].astype(BF16)
        wub_ref[...] = wu_ref[0].astype(BF16)
        wdb_ref[...] = wd_ref[0].astype(BF16)

    @pl.when(i < n_valid)
    def _():
        _wait_tiles(h_hbm, xbuf.at[slot], sem.at[slot])
        xb = jnp.concatenate(_load_token_slabs(xbuf.at[slot], MOE_TILE), axis=1).astype(BF16)
        hg = _dot(xb, wgb_ref[...])
        hu = _dot(xb, wub_ref[...])
        mid = (hg * jax.nn.sigmoid(hg) * hu).astype(BF16)
        _store_token_tiles(o_ref, _dot(mid, wdb_ref[...]))

    @pl.when(i >= n_valid)
    def _():
        o_ref[...] = jnp.zeros_like(o_ref)


def _moe_experts(h1, row_asg, tile_expert, n_valid, w_gate, w_up, w_down):
    D = D_MODEL
    n_tiles = row_asg.shape[0] // MOE_TILE
    smem_tile = lambda f: pl.BlockSpec((MOE_TILE,), f, memory_space=pltpu.SMEM)
    grid_spec = pltpu.PrefetchScalarGridSpec(
        num_scalar_prefetch=2,
        grid=(n_tiles,),
        in_specs=[smem_tile(lambda i, te, nv: (i,)),
                  smem_tile(lambda i, te, nv: (jnp.minimum(i + 1, n_tiles - 1),)),
                  pl.BlockSpec(memory_space=pl.ANY),
                  pl.BlockSpec((1, D, D_EXPERT), lambda i, te, nv: (te[i], 0, 0)),
                  pl.BlockSpec((1, D, D_EXPERT), lambda i, te, nv: (te[i], 0, 0)),
                  pl.BlockSpec((1, D_EXPERT, D), lambda i, te, nv: (te[i], 0, 0))],
        out_specs=pl.BlockSpec((MOE_TILE * SLABS, LANE), lambda i, te, nv: (i, 0)),
        scratch_shapes=[pltpu.VMEM((2, MOE_TILE * SLABS, LANE), F32), pltpu.SemaphoreType.DMA((2,)),
                        pltpu.VMEM((D, D_EXPERT), BF16), pltpu.VMEM((D, D_EXPERT), BF16),
                        pltpu.VMEM((D_EXPERT, D), BF16)],
    )
    return pl.pallas_call(
        _moe_kernel,
        out_shape=jax.ShapeDtypeStruct((n_tiles * MOE_TILE * SLABS, LANE), F32),
        grid_spec=grid_spec,
        compiler_params=_cparams(("arbitrary",)),
        name="moe_experts",
    )(tile_expert, n_valid, row_asg, row_asg, h1, w_gate, w_up, w_down)


def _combine_kernel(d_cur_ref, d_nxt_ref, h_ref, gate_ref, g_ref, b_ref, y_hbm, o_ref, ybuf, sem):
    i = pl.program_id(0)
    n = pl.num_programs(0)
    slot = lax.rem(i, 2)

    def start_gather(d_ref, s):
        def body(tt, carry):
            for u in range(DMA_UNROLL // TOP_K):
                t = tt * (DMA_UNROLL // TOP_K) + u
                for k in range(TOP_K):
                    _row_copy(y_hbm, d_ref[TOP_K * t + k], ybuf.at[s, k], t, sem.at[s]).start(priority=k % 2)
            return carry
        lax.fori_loop(0, COMBINE_TILE * TOP_K // DMA_UNROLL, body, 0)

    @pl.when(i == 0)
    def _():
        start_gather(d_cur_ref, 0)

    @pl.when(i + 1 < n)
    def _():
        start_gather(d_nxt_ref, 1 - slot)

    for k in range(TOP_K):
        _wait_tiles(y_hbm, ybuf.at[slot, k], sem.at[slot])

    gate = gate_ref[...]
    tm = COMBINE_TILE
    ga = jnp.broadcast_to(gate[:, 0:1], (tm, LANE))
    gb = jnp.broadcast_to(gate[:, 1:2], (tm, LANE))
    hs = _load_token_slabs(h_ref, tm)
    ya = _load_token_slabs(ybuf.at[slot, 0], tm)
    yb = _load_token_slabs(ybuf.at[slot, 1], tm)
    z = [DEEPNORM_ALPHA * h + (ga * a + gb * b) for h, a, b in zip(hs, ya, yb)]
    inv_d = 1.0 / D_MODEL
    mu = sum(jnp.sum(t, axis=1, keepdims=True) for t in z) * inv_d
    zc = [t - mu for t in z]
    var = sum(jnp.sum(t * t, axis=1, keepdims=True) for t in zc) * inv_d
    rstd = lax.rsqrt(var + LN_EPS)
    for s in range(SLABS):
        cols = slice(s * LANE, (s + 1) * LANE)
        o_ref[:, cols] = zc[s] * rstd * g_ref[:, cols] + b_ref[:, cols]


def _combine(h1, ybuf, dest, gates, ln2_g, ln2_b):
    D = D_MODEL
    M = h1.shape[0] // SLABS
    tm = COMBINE_TILE
    n = M // tm
    smem_tile = lambda f: pl.BlockSpec((TOP_K * tm,), f, memory_space=pltpu.SMEM)
    return pl.pallas_call(
        _combine_kernel,
        out_shape=jax.ShapeDtypeStruct((M, D), F32),
        grid=(n,),
        in_specs=[smem_tile(lambda i: (i,)), smem_tile(lambda i: (jnp.minimum(i + 1, n - 1),)),
                  pl.BlockSpec((tm * SLABS, LANE), lambda i: (i, 0)), pl.BlockSpec((tm, TOP_K), lambda i: (i, 0)),
                  _full(ln2_g), _full(ln2_b), pl.BlockSpec(memory_space=pl.ANY)],
        out_specs=pl.BlockSpec((tm, D), lambda i: (i, 0)),
        scratch_shapes=[pltpu.VMEM((2, TOP_K, tm * SLABS, LANE), F32), pltpu.SemaphoreType.DMA((2,))],
        compiler_params=_cparams(("arbitrary",)),
        name="combine",
    )(dest, dest, h1, gates, ln2_g, ln2_b, ybuf)


def _routing_plan(route):
    M = route.shape[0]
    eid = route[:, :TOP_K].astype(jnp.int32).reshape(-1)
    gates = route[:, TOP_K:2 * TOP_K]
    A = M * TOP_K
    onehot = (eid[:, None] == jnp.arange(N_EXPERTS, dtype=eid.dtype)[None, :]).astype(jnp.int32)
    csum = jnp.cumsum(onehot, axis=0)
    rank = jnp.sum(csum * onehot, axis=1) - 1
    counts = csum[-1]
    pcounts = (counts + MOE_TILE - 1) // MOE_TILE * MOE_TILE
    pends = jnp.cumsum(pcounts)
    pstarts = pends - pcounts
    dest = (jnp.sum(onehot * pstarts[None, :], axis=1) + rank).astype(jnp.int32)
    n_tiles = (A + N_EXPERTS * (MOE_TILE - 1) + MOE_TILE - 1) // MOE_TILE
    n_valid = (pends[-1] // MOE_TILE).astype(jnp.int32)
    tile_start = jnp.minimum(jnp.arange(n_tiles, dtype=jnp.int32) * MOE_TILE, pends[-1] - 1)
    tile_expert = jnp.sum((pends[None, :] <= tile_start[:, None]).astype(jnp.int32), axis=1)
    tile_expert = jnp.minimum(tile_expert, N_EXPERTS - 1).astype(jnp.int32)
    pad_lo = (pstarts + counts).astype(jnp.int32)
    pad_hi = pends.astype(jnp.int32)
    return gates, dest, n_tiles * MOE_TILE, tile_expert, n_valid.reshape(1), pad_lo, pad_hi


def kernel(x, meta, ln0_g, ln0_b, w_in, mu_shift, w0, w_decay_up, a0, w_a_up, w_g_up, k_k, k_a, r_k, gn_g, gn_b, conv_w, conv_b, w_rg, b_rg, w_ig, b_ig, lru_lambda, w_out, ln1_g, ln1_b, w_router_grp, b_router_grp, w_router_exp, b_router_exp, w_exp_gate, w_exp_up, w_exp_down, ln2_g, ln2_b):
    B, T, D = x.shape
    assert D == D_MODEL and T % 512 == 0 and w_in.shape[0] == 1
    assert (B * T * TOP_K) % INVERT_BLOCK == 0
    row = lambda p: p.reshape(1, -1).astype(F32)
    n_rw = 3 * RWKV_W
    w_in0 = w_in[0]

    def slots(p):
        pad = lambda a, n: jnp.pad(a, [(0, 0)] * (a.ndim - 1) + [(0, n - a.shape[-1])])
        zw = p[..., n_rw:n_rw + DECAY_RANK]
        za = p[..., n_rw + DECAY_RANK:n_rw + DECAY_RANK + AAA_RANK]
        zg = p[..., n_rw + DECAY_RANK + AAA_RANK:n_rw + DECAY_RANK + AAA_RANK + GATE_RANK]
        return jnp.concatenate([p[..., :n_rw], pad(zw, LANE), pad(za, LANE), pad(zg, ZG_SLOT)], axis=-1)

    rwkv_cols = n_rw + DECAY_RANK + AAA_RANK + GATE_RANK
    w_r = slots(w_in0[:, :rwkv_cols]).astype(BF16)
    w_l = w_in0[:, rwkv_cols:].astype(BF16)
    ur, ul, ur_t, ul_t = _in_projection(x, meta, row(ln0_g), row(ln0_b), w_r, w_l)

    pad_rows = lambda a, n: jnp.pad(a, ((0, n - a.shape[0]), (0, 0)))
    rwkv_params = (slots(mu_shift[0][None, :]).astype(F32), row(w0[0]), pad_rows(w_decay_up[0], LANE).astype(BF16),
                   row(a0[0]), pad_rows(w_a_up[0], LANE).astype(BF16), pad_rows(w_g_up[0], ZG_SLOT).astype(BF16),
                   row(k_k[0]), row(k_a[0]), row(r_k[0]), row(gn_g[0]), row(gn_b[0]))
    y_rwkv = _rwkv_pipe_mixer(ur, ur_t, rwkv_params)

    blockdiag = lambda w: jax.scipy.linalg.block_diag(*[w[i] for i in range(LRU_BLOCKS)]).astype(BF16)
    lru_params = (conv_w[0], row(conv_b[0]), blockdiag(w_rg[0]), row(b_rg[0]), blockdiag(w_ig[0]), row(b_ig[0]),
                  row(lru_lambda[0]))
    y_lru = _lru_mixer(ul, ul_t, lru_params)

    w_rt = jnp.concatenate([w_router_grp[0], w_router_exp[0]], axis=1)
    w_rt = jnp.pad(w_rt, ((0, 0), (0, LANE - w_rt.shape[1])))
    wrt_hi = w_rt.astype(BF16)
    wrt_lo = (w_rt - wrt_hi.astype(F32)).astype(BF16)
    b_rt = jnp.concatenate([b_router_grp[0], b_router_exp[0]])
    b_rt = jnp.pad(b_rt, (0, LANE - b_rt.shape[0])).reshape(1, LANE)
    wo = w_out[0].astype(BF16)
    h1, route = _out_projection(x, y_rwkv, y_lru, row(ln0_g), row(ln0_b), wo[:RWKV_W], wo[RWKV_W:],
                                row(ln1_g[0]), row(ln1_b[0]), wrt_hi, wrt_lo, b_rt)

    M = B * T
    h1 = h1.reshape(M * SLABS, LANE)
    gates, dest, n_slots, tile_expert, n_valid, pad_lo, pad_hi = _routing_plan(route.reshape(M, LANE))
    row_asg = _invert_slots(dest, n_slots, pad_lo, pad_hi)
    ybuf = _moe_experts(h1, row_asg, tile_expert, n_valid, w_exp_gate[0], w_exp_up[0], w_exp_down[0])
    out = _combine(h1, ybuf, dest, gates, row(ln2_g[0]), row(ln2_b[0]))
    return out.reshape(B, T, D)
```

```python
import functools
import math

import jax
import jax.numpy as jnp
from jax import lax
from jax.experimental import pallas as pl
from jax.experimental.pallas import tpu as pltpu
from jax.experimental.pallas import tpu_sc as plsc

F32 = jnp.float32
BF16 = jnp.bfloat16

D_MODEL = 1024
N_META = 16
RWKV_W = 512
RWKV_HEAD = 64
DECAY_RANK = 64
AAA_RANK = 64
GATE_RANK = 160
LRU_W = 512
LRU_BLOCKS = 8
CONV_WIDTH = 4
LRU_C = 8.0
N_GROUPS = 4
EXPERTS_PER_GROUP = 8
N_EXPERTS = N_GROUPS * EXPERTS_PER_GROUP
TOP_K = 2
D_EXPERT = 512
LN_EPS = 1e-5
GN_EPS = 64e-5
DEEPNORM_ALPHA = 2.0 ** 0.25

LANE = 128
OFF_R, OFF_K, OFF_V = 0, RWKV_W, 2 * RWKV_W
OFF_ZW = 3 * RWKV_W
OFF_ZA = OFF_ZW + LANE
OFF_ZG = OFF_ZA + LANE
ZG_SLOT = 2 * LANE
UR_W = OFF_ZG + ZG_SLOT
UL_W = 2 * LRU_W

TAIL = 256
CHUNK = 64
HEADS_PER_GROUP = 4
GW = HEADS_PER_GROUP * RWKV_HEAD
N_HGROUPS = RWKV_W // GW
LRU_TILE = TAIL
MOE_TILE = 256
COMBINE_TILE = 256
INVERT_BLOCK = 4096
DMA_UNROLL = 8
V7X_VMEM_BYTES = 64 * 1024 * 1024
VMEM_LIMIT = V7X_VMEM_BYTES - 8 * 1024 * 1024


def _cparams(sem):
    return pltpu.CompilerParams(dimension_semantics=sem, vmem_limit_bytes=VMEM_LIMIT)


def _layer_norm(x, g, b):
    mu = jnp.mean(x, -1, keepdims=True)
    xc = x - mu
    var = jnp.mean(xc * xc, -1, keepdims=True)
    return xc * lax.rsqrt(var + LN_EPS) * g + b


def _dot(a, b):
    return jnp.dot(a, b, preferred_element_type=F32)


def _dot_nt(a, b):
    return lax.dot_general(a, b, (((1,), (1,)), ((), ())), preferred_element_type=F32)


def _dot_tn(a, b):
    return lax.dot_general(a, b, (((0,), (0,)), ((), ())), preferred_element_type=F32)


def _full(a):
    return pl.BlockSpec(a.shape, lambda *_: (0,) * a.ndim)


def _inproj_kernel(x_ref, g_ref, b_ref, wr_ref, wl_ref, ur_ref, ul_ref):
    h = _layer_norm(x_ref[0], g_ref[...], b_ref[...]).astype(BF16)
    ur_ref[0] = _dot(h, wr_ref[...])
    ul_ref[0] = _dot(h, wl_ref[...])


def _inproj_tail_kernel(x_ref, g_ref, b_ref, wr_ref, wl_ref, ur_ref, ul_ref):
    h = _layer_norm(x_ref[...], g_ref[...], b_ref[...]).astype(BF16)
    rows = lax.broadcasted_iota(jnp.int32, (TAIL, 1), 0)
    valid = (rows >= TAIL - N_META).astype(F32)
    ur_ref[...] = _dot(h, wr_ref[...]) * valid
    ul_ref[...] = _dot(h, wl_ref[...]) * valid


def _in_projection(x, meta, ln0_g, ln0_b, w_r, w_l):
    B, T, D = x.shape
    tm = 512
    ur, ul = pl.pallas_call(
        _inproj_kernel,
        out_shape=(jax.ShapeDtypeStruct((B, T, UR_W), F32), jax.ShapeDtypeStruct((B, T, UL_W), F32)),
        grid=(B, T // tm),
        in_specs=[pl.BlockSpec((1, tm, D), lambda b, i: (b, i, 0)), _full(ln0_g), _full(ln0_b), _full(w_r), _full(w_l)],
        out_specs=(pl.BlockSpec((1, tm, UR_W), lambda b, i: (b, i, 0)),
                   pl.BlockSpec((1, tm, UL_W), lambda b, i: (b, i, 0))),
        compiler_params=_cparams(("parallel", "parallel")),
        name="inproj",
    )(x, ln0_g, ln0_b, w_r, w_l)
    tail_x = jnp.concatenate([jnp.zeros((TAIL - N_META, D), F32), meta.astype(F32)], axis=0)
    ur_t, ul_t = pl.pallas_call(
        _inproj_tail_kernel,
        out_shape=(jax.ShapeDtypeStruct((TAIL, UR_W), F32), jax.ShapeDtypeStruct((TAIL, UL_W), F32)),
        grid=(1,),
        in_specs=[_full(tail_x), _full(ln0_g), _full(ln0_b), _full(w_r), _full(w_l)],
        out_specs=(pl.BlockSpec((TAIL, UR_W), lambda i: (0, 0)), pl.BlockSpec((TAIL, UL_W), lambda i: (0, 0))),
        compiler_params=_cparams(("arbitrary",)),
        name="inproj_tail",
    )(tail_x, ln0_g, ln0_b, w_r, w_l)
    return ur, ul, ur_t, ul_t


def _rwkv_pipe_kernel(u_ref, ut_ref, mu_ref, w0_ref, wdu_ref, a0_ref, wau_ref, wgu_ref, kk_ref, ka_ref, rk_ref,
                      gng_ref, gnb_ref, bones_ref, bm_ref, eye_ref, msl_ref, mil_ref,
                      m8_ref, m16_ref, m32_ref, m64_ref, y_ref,
                      s_ref, prev_ref, y0_s, q_s, mc_s, nc_s, we_s, bonus_s, g_s, yraw_s):
    s_id = pl.program_id(0)
    nb = u_ref.shape[0]
    blk = u_ref.shape[1]
    npc = blk // CHUNK
    nseq = nb * npc
    seq_rows = lambda q: slice(q * CHUNK, (q + 1) * CHUNK)
    per_seq = lambda f: jnp.concatenate([f(q) for q in range(nseq)], axis=0)
    w_slot = lax.rem(s_id, 2)
    r_slot = 1 - w_slot

    @pl.when(s_id == 0)
    def _():
        s_ref[...] = jnp.zeros_like(s_ref)
        prev_ref[...] = jnp.zeros_like(prev_ref)
        for ref in (y0_s, q_s, mc_s, nc_s, we_s, bonus_s, g_s):
            ref[1] = jnp.zeros(ref.shape[1:], ref.dtype)

    u_x = u_ref[...].reshape(nb * blk, UR_W)
    u = jnp.where(s_id == 0, jnp.concatenate([ut_ref[...]] * nb, axis=0), u_x)
    row = lax.broadcasted_iota(jnp.int32, u.shape, 0)
    prev_rows = jnp.concatenate([jnp.broadcast_to(prev_ref[b:b + 1, :], (blk, UR_W)) for b in range(nb)], axis=0)
    u_prev = jnp.where(jnp.bitwise_and(row, blk - 1) == 0, prev_rows, pltpu.roll(u, 1, 0))
    for b in range(nb):
        prev_ref[b:b + 1, :] = u[(b + 1) * blk - 1:(b + 1) * blk, :]
    x = u + (u_prev - u) * mu_ref[...]
    r = x[:, OFF_R:OFF_R + RWKV_W]
    k = x[:, OFF_K:OFF_K + RWKV_W]
    v = x[:, OFF_V:OFF_V + RWKV_W]
    zw = x[:, OFF_ZW:OFF_ZW + LANE]
    za = x[:, OFF_ZA:OFF_ZA + LANE]
    zg = x[:, OFF_ZG:OFF_ZG + ZG_SLOT]

    b16 = lambda t: t.astype(BF16)
    bones = bones_ref[...]
    head_sum = lambda t: _dot(b16(t), bones)

    z = w0_ref[...] + _dot(b16(jnp.tanh(zw)), wdu_ref[...])
    logw = -math.exp(-0.5) * jax.nn.sigmoid(z)
    a = jax.nn.sigmoid(a0_ref[...] + _dot(b16(za), wau_ref[...]))
    g = _dot(b16(jax.nn.sigmoid(zg)), wgu_ref[...])
    kk = k * kk_ref[...]
    kk = kk / jnp.maximum(jnp.sqrt(head_sum(kk * kk)), 1e-12)
    k = k * (1.0 + (a - 1.0) * ka_ref[...])
    kka = kk * a
    bonus_s[w_slot] = head_sum(r * k * rk_ref[...]) * v
    g_s[w_slot] = g

    cl = logw
    row_in_chunk = jnp.bitwise_and(lax.broadcasted_iota(jnp.int32, cl.shape, 0), CHUNK - 1)
    d = 1
    while d < CHUNK:
        cl = cl + jnp.where(row_in_chunk >= d, pltpu.roll(cl, d, 0), 0.0)
        d *= 2
    cl_last = per_seq(lambda q: jnp.broadcast_to(cl[(q + 1) * CHUNK - 1:(q + 1) * CHUNK, :], (CHUNK, RWKV_W)))
    e_neg = jnp.exp(-cl)
    e_end = jnp.exp(cl_last - cl)
    rt = r * jnp.exp(cl)
    kt = k * e_neg
    at = -kk * jnp.exp(cl - logw)
    bt = kka * e_neg
    kw = k * e_end
    bw = kka * e_end
    w_end = jnp.exp(cl_last)
    for q in range(nseq):
        we_s[w_slot, q * 8:(q + 1) * 8, :] = w_end[q * CHUNK:q * CHUNK + 8, :]

    bm = bm_ref[...]
    bm16 = b16(bm)
    eye = eye_ref[...]
    msl = msl_ref[...]
    mil = mil_ref[...]
    tile4 = lambda t: jnp.concatenate([t] * HEADS_PER_GROUP, axis=0)
    fold4 = lambda t: sum(t[i * CHUNK:(i + 1) * CHUNK] for i in range(HEADS_PER_GROUP))
    bd = lambda t: tile4(b16(t)) * bm16
    rows2 = lambda x, y: jnp.concatenate([x, y], axis=0)
    cols2 = lambda x, y: jnp.concatenate([x, y], axis=1)

    probs = [(seq_rows(q), slice(hg * GW, (hg + 1) * GW)) for q in range(nseq) for hg in range(N_HGROUPS)]
    each = lambda f, *ls: [f(*xs) for xs in zip(*ls)]
    pick = lambda t: [t[rq, sl] for rq, sl in probs]
    at_w, rt_w, v_w = pick(at), pick(rt), pick(v)
    lhs = each(lambda x, y: b16(rows2(x, y)), at_w, rt_w)
    ab = each(_dot_nt, lhs, each(bd, pick(bt)))
    ak = each(_dot_nt, lhs, each(bd, pick(kt)))
    a_ab = each(lambda t: t[:CHUNK] * msl, ab)
    a_rb = each(lambda t: b16(t[CHUNK:] * mil), ab)
    a_xk = each(lambda t: b16(rows2(t[:CHUNK] * msl, t[CHUNK:] * mil)), ak)

    a0 = each(lambda t: b16(t * m8_ref[...]), a_ab)
    a2 = each(lambda t: b16(_dot(t, bd(t))), a0)
    a4 = each(lambda t: b16(_dot(t, bd(t))), a2)
    p1 = each(lambda t: eye + t.astype(F32), a0)
    p1 = each(lambda p, t: p + _dot(b16(p), bd(t)), p1, a2)
    tt = each(lambda p, t: p + _dot(b16(p), bd(t)), p1, a4)
    for m_ref in (m16_ref, m32_ref, m64_ref):
        tb = each(b16, tt)
        off = each(lambda t: b16(t * m_ref[...]), a_ab)
        half = each(lambda x, y: b16(_dot(x, bd(y))), tb, off)
        tt = each(lambda t, x, y: t + _dot(x, bd(y)), tt, half, tb)
    tb = each(b16, tt)

    xv = each(lambda x, y: _dot(x, bd(y)), a_xk, v_w)
    u0 = each(lambda x, y: _dot(x, bd(y[:CHUNK])), tb, xv)
    ta = each(lambda x, y: _dot(x, bd(y)), tb, at_w)
    y0 = each(lambda x, y, z: _dot(x, bd(y)) + z[CHUNK:], a_rb, u0, xv)
    qq = each(lambda x, y, z: x + _dot(y, bd(z)), rt_w, a_rb, ta)
    left = each(lambda x, y, z: b16(rows2(cols2(x, y), cols2(jnp.zeros_like(z), z))), ta, u0, v_w)
    right = each(lambda x, y: b16(rows2(x, y)), pick(bw), pick(kw))
    mn = each(_dot_tn, left, right)
    for i, (rq, sl) in enumerate(probs):
        y0_s[w_slot, rq, sl] = y0[i]
        q_s[w_slot, rq, sl] = b16(qq[i])
        mc_s[w_slot, rq, sl] = b16(fold4(mn[i][:GW] * bm))
        nc_s[w_slot, rq, sl] = fold4(mn[i][GW:] * bm)

    for b in range(nb):
        for hg in range(N_HGROUPS):
            sl = slice(hg * GW, (hg + 1) * GW)
            s = s_ref[b, hg]
            for j in range(npc):
                q = b * npc + j
                rq = seq_rows(q)
                yraw_s[rq, sl] = y0_s[r_slot, rq, sl] + _dot_nt(q_s[r_slot, rq, sl], bd(s))
                s = (s * we_s[r_slot, q * 8:q * 8 + 1, sl] + _dot(b16(s), bd(mc_s[r_slot, rq, sl]))
                     + nc_s[r_slot, rq, sl])
            s_ref[b, hg] = s

    y = yraw_s[...]
    inv_n = 1.0 / RWKV_HEAD
    ym = head_sum(y) * inv_n
    yc = y - ym
    yv = head_sum(yc * yc) * inv_n
    yn = yc * lax.rsqrt(yv + GN_EPS) * gng_ref[...] + gnb_ref[...]
    y_ref[...] = ((yn + bonus_s[r_slot]) * g_s[r_slot]).astype(y_ref.dtype).reshape(y_ref.shape)


def _rwkv_masks():
    f = lambda m: m.astype(F32)
    i = jnp.arange(GW)[:, None]
    j = jnp.arange(GW)[None, :]
    bm = f((i // RWKV_HEAD) == (j // RWKV_HEAD))
    t = jnp.arange(CHUNK)[:, None]
    s = (jnp.arange(GW) % CHUNK)[None, :]
    same = lambda n: (t // n) == (s // n)
    msl = f(t > s)
    mil = f(t >= s)
    m8 = f(same(8))
    m16 = f(same(16) & ~same(8))
    m32 = f(same(32) & ~same(16))
    m64 = f(~same(32))
    eye = f(t == s)
    hi = jnp.arange(RWKV_W)
    bones = ((hi[:, None] // RWKV_HEAD) == (hi[None, :] // RWKV_HEAD)).astype(BF16)
    return bones, bm, eye, msl, mil, m8, m16, m32, m64


def _rwkv_pipe_mixer(ur, ur_tail, params):
    B, T, _ = ur.shape
    blk = TAIL
    assert T % blk == 0 and blk % CHUNK == 0 and CHUNK == RWKV_HEAD
    n_blocks = T // blk
    rows = B * blk
    consts = _rwkv_masks()
    in_map = lambda s: (0, jnp.clip(s - 1, 0, n_blocks - 1), 0)
    out_map = lambda s: (0, jnp.clip(s - 2, 0, n_blocks - 1), 0)
    slot2 = lambda w, dt: pltpu.VMEM((2, rows, w), dt)
    return pl.pallas_call(
        _rwkv_pipe_kernel,
        out_shape=jax.ShapeDtypeStruct((B, T, RWKV_W), BF16),
        grid=(n_blocks + 2,),
        in_specs=[pl.BlockSpec((B, blk, UR_W), in_map), _full(ur_tail)]
                 + [_full(p) for p in params] + [_full(m) for m in consts],
        out_specs=pl.BlockSpec((B, blk, RWKV_W), out_map),
        scratch_shapes=[pltpu.VMEM((B, N_HGROUPS, CHUNK, GW), F32), pltpu.VMEM((B, UR_W), F32),
                        slot2(RWKV_W, F32), slot2(RWKV_W, BF16), slot2(RWKV_W, BF16), slot2(RWKV_W, F32),
                        pltpu.VMEM((2, 8 * rows // CHUNK, RWKV_W), F32), slot2(RWKV_W, F32), slot2(RWKV_W, F32),
                        pltpu.VMEM((rows, RWKV_W), F32)],
        compiler_params=_cparams(("arbitrary",)),
        name="rwkv7",
    )(ur, ur_tail, *params, *consts)


def _gelu_tanh(x):
    return 0.5 * x * (1.0 + jnp.tanh(math.sqrt(2.0 / math.pi) * (x + 0.044715 * (x * x * x))))


LRU_CARRY = 8


def _lru_kernel(u_ref, ut_ref, cw_ref, cb_ref, wrg_ref, brg_ref, wig_ref, big_ref, lam_ref, y_ref,
                xs_ref, hprev_ref):
    c = pl.program_id(0)
    nb = u_ref.shape[0]
    nrow = nb * LRU_TILE

    @pl.when(c == 0)
    def _():
        xs_ref[...] = jnp.zeros_like(xs_ref)
        hprev_ref[...] = jnp.zeros_like(hprev_ref)

    u_x = u_ref[...].reshape(nrow, UL_W)
    u = jnp.where(c == 0, jnp.concatenate([ut_ref[...]] * nb, axis=0), u_x)
    xl = u[:, :LRU_W]
    gl = u[:, LRU_W:]
    row = jnp.bitwise_and(lax.broadcasted_iota(jnp.int32, (nrow, LRU_W), 0), LRU_TILE - 1)
    in_group = jnp.bitwise_and(row, 7)
    roll_in_group = lambda t, d: pltpu.roll(t.reshape(t.shape[0] // 8, 8, LRU_W), d, 1).reshape(t.shape)
    xl_prev = jnp.concatenate(
        [p for b in range(nb) for p in (xs_ref[b], xl[b * LRU_TILE:(b + 1) * LRU_TILE - 8])], axis=0)
    xc = cb_ref[...] + cw_ref[CONV_WIDTH - 1:CONV_WIDTH, :] * xl
    for d in range(1, CONV_WIDTH):
        tap = jnp.where(in_group >= d, roll_in_group(xl, d), roll_in_group(xl_prev, d))
        xc = xc + cw_ref[CONV_WIDTH - 1 - d:CONV_WIDTH - d, :] * tap
    for b in range(nb):
        xs_ref[b] = xl[(b + 1) * LRU_TILE - 8:(b + 1) * LRU_TILE]

    xcb = xc.astype(BF16)
    gate_r = jax.nn.sigmoid(_dot(xcb, wrg_ref[...]) + brg_ref[...])
    gate_i = jax.nn.sigmoid(_dot(xcb, wig_ref[...]) + big_ref[...])
    lam = lam_ref[...]
    log_sig = -(jnp.maximum(-lam, 0.0) + jnp.log1p(jnp.exp(-jnp.abs(lam))))
    log_a = LRU_C * gate_r * log_sig
    a = jnp.exp(log_a)
    mult = jnp.sqrt(jnp.maximum(1.0 - jnp.exp(2.0 * log_a), 0.0))
    b = mult * gate_i * xc
    b = jnp.where((c == 0) & (row < LRU_TILE - N_META), 0.0, b)

    d = 1
    while d < 8:
        keep = in_group >= d
        a_sh = jnp.where(keep, roll_in_group(a, d), 1.0)
        b_sh = jnp.where(keep, roll_in_group(b, d), 0.0)
        b = a * b_sh + b
        a = a * a_sh
        d *= 2
    groups = []
    for bi in range(nb):
        carry = hprev_ref[bi:bi + 1, :]
        for gi in range(LRU_TILE // 8):
            lo = bi * LRU_TILE + gi * 8
            hg = b[lo:lo + 8] + a[lo:lo + 8] * carry
            carry = hg[7:8, :]
            groups.append(hg)
        hprev_ref[bi:bi + 1, :] = carry
    h = jnp.concatenate(groups, axis=0)
    y_ref[...] = (h * _gelu_tanh(gl)).astype(y_ref.dtype).reshape(y_ref.shape)


def _lru_mixer(ul, ul_tail, params):
    B, T, _ = ul.shape
    assert TAIL == LRU_TILE
    x_map = lambda c: (0, jnp.maximum(c - 1, 0), 0)
    return pl.pallas_call(
        _lru_kernel,
        out_shape=jax.ShapeDtypeStruct((B, T, LRU_W), BF16),
        grid=(T // LRU_TILE + 1,),
        in_specs=[pl.BlockSpec((B, LRU_TILE, UL_W), x_map), _full(ul_tail)] + [_full(p) for p in params],
        out_specs=pl.BlockSpec((B, LRU_TILE, LRU_W), x_map),
        scratch_shapes=[pltpu.VMEM((B, LRU_CARRY, LRU_W), F32), pltpu.VMEM((B, LRU_W), F32)],
        compiler_params=_cparams(("arbitrary",)),
        name="rglru",
    )(ul, ul_tail, *params)


def _route(lg):
    lane = lax.broadcasted_iota(jnp.int32, lg.shape, 1)
    neg = jnp.float32(-jnp.inf)
    rmax = lambda t: jnp.max(t, axis=1, keepdims=True)
    first = lambda hit: jnp.min(jnp.where(hit, lane, LANE), axis=1, keepdims=True)
    is_grp = lane < N_GROUPS
    gl = jnp.where(is_grp, lg, neg)
    gmax = rmax(gl)
    g_sel = first(gl == gmax)
    p_g = 1.0 / jnp.sum(jnp.where(is_grp, jnp.exp(lg - gmax), 0.0), axis=1, keepdims=True)
    ex = lane - N_GROUPS
    in_grp = (ex >= 0) & (ex < N_EXPERTS) & (jnp.right_shift(ex, 3) == g_sel)
    el = jnp.where(in_grp, lg, neg)
    v1 = rmax(el)
    i1 = first(el == v1)
    el2 = jnp.where(lane == i1, neg, el)
    v2 = rmax(el2)
    i2 = first(el2 == v2)
    t = jnp.exp(v2 - v1)
    gate1 = p_g / (1.0 + t)
    gate2 = p_g * t / (1.0 + t)
    e1 = (i1 - N_GROUPS).astype(F32)
    e2 = (i2 - N_GROUPS).astype(F32)
    return jnp.where(lane == 0, e1, jnp.where(lane == 1, e2, jnp.where(lane == 2, gate1, jnp.where(lane == 3, gate2, 0.0))))


SLABS = D_MODEL // LANE


def _store_token_tiles(ref, val):
    n = val.shape[0]
    for s in range(SLABS):
        ref[pl.ds(s, n, stride=SLABS), :] = val[:, s * LANE:(s + 1) * LANE]


def _load_token_slabs(ref, n):
    return [ref[pl.ds(s, n, stride=SLABS), :] for s in range(SLABS)]


def _outproj_kernel(x_ref, yr_ref, yl_ref, g0_ref, b0_ref, wor_ref, wol_ref, g1_ref, b1_ref,
                    wrt_hi_ref, wrt_lo_ref, brt_ref, h1_ref, rt_ref):
    h0 = _layer_norm(x_ref[0], g0_ref[...], b0_ref[...])
    mix = _dot(yr_ref[0], wor_ref[...]) + _dot(yl_ref[0], wol_ref[...])
    h1 = _layer_norm(DEEPNORM_ALPHA * h0 + mix, g1_ref[...], b1_ref[...])
    _store_token_tiles(h1_ref.at[0], h1)
    hi = h1.astype(BF16)
    lo = (h1 - hi.astype(F32)).astype(BF16)
    w_hi = wrt_hi_ref[...]
    lg = _dot(hi, w_hi) + (_dot(hi, wrt_lo_ref[...]) + _dot(lo, w_hi)) + brt_ref[...]
    rt_ref[0] = _route(lg)


def _out_projection(x, y_rwkv, y_lru, ln0_g, ln0_b, wo_r, wo_l, ln1_g, ln1_b, wrt_hi, wrt_lo, brt):
    B, T, D = x.shape
    tm = 512
    rows = lambda w: pl.BlockSpec((1, tm, w), lambda b, i: (b, i, 0))
    return pl.pallas_call(
        _outproj_kernel,
        out_shape=(jax.ShapeDtypeStruct((B, T * SLABS, LANE), F32), jax.ShapeDtypeStruct((B, T, LANE), F32)),
        grid=(B, T // tm),
        in_specs=[rows(D), rows(RWKV_W), rows(LRU_W), _full(ln0_g), _full(ln0_b), _full(wo_r), _full(wo_l),
                  _full(ln1_g), _full(ln1_b), _full(wrt_hi), _full(wrt_lo), _full(brt)],
        out_specs=(pl.BlockSpec((1, tm * SLABS, LANE), lambda b, i: (b, i, 0)), rows(LANE)),
        compiler_params=_cparams(("parallel", "parallel")),
        name="outproj",
    )(x, y_rwkv, y_lru, ln0_g, ln0_b, wo_r, wo_l, ln1_g, ln1_b, wrt_hi, wrt_lo, brt)


def _invert_kernel(pad_lo_ref, pad_hi_ref, dest_ref, out_ref):
    i = pl.program_id(0)

    @pl.when(i == 0)
    def _():
        def zero(j, carry):
            out_ref[j] = 0
            return carry
        for e in range(N_EXPERTS):
            lax.fori_loop(pad_lo_ref[e], pad_hi_ref[e], zero, 0)
        lax.fori_loop(pad_hi_ref[N_EXPERTS - 1], out_ref.shape[0], zero, 0)

    base = i * INVERT_BLOCK

    def body(j, carry):
        out_ref[dest_ref[j]] = base + j
        return carry

    lax.fori_loop(0, INVERT_BLOCK, body, 0, unroll=8)


def _invert_slots(dest, n_slots, pad_lo, pad_hi):
    A = dest.shape[0]
    grid_spec = pltpu.PrefetchScalarGridSpec(
        num_scalar_prefetch=2,
        grid=(A // INVERT_BLOCK,),
        in_specs=[pl.BlockSpec((INVERT_BLOCK,), lambda i, lo, hi: (i,), memory_space=pltpu.SMEM)],
        out_specs=pl.BlockSpec(memory_space=pltpu.SMEM),
    )
    return pl.pallas_call(
        _invert_kernel,
        out_shape=jax.ShapeDtypeStruct((n_slots,), jnp.int32),
        grid_spec=grid_spec,
        compiler_params=_cparams(("arbitrary",)),
        name="invert_slots",
    )(pad_lo, pad_hi, dest)


def _row_copy(src_hbm, src_row, dst_ref, dst_row, sem):
    return pltpu.make_async_copy(src_hbm.at[pl.ds(src_row * SLABS, SLABS), :],
                                 dst_ref.at[pl.ds(dst_row * SLABS, SLABS), :], sem)


def _wait_tiles(src_hbm, dst_ref, sem):
    pltpu.make_async_copy(src_hbm.at[pl.ds(0, dst_ref.shape[0]), :], dst_ref, sem).wait()


def _moe_kernel(te_ref, nv_ref, ra_cur_ref, ra_nxt_ref, h_hbm, wg_ref, wu_ref, wd_ref, o_ref,
                xbuf, sem, wgb_ref, wub_ref, wdb_ref):
    i = pl.program_id(0)
    n_valid = nv_ref[0]
    slot = lax.rem(i, 2)

    def start_gather(ra_ref, s):
        def body(jj, carry):
            for u in range(DMA_UNROLL):
                j = jj * DMA_UNROLL + u
                tok = lax.shift_right_logical(ra_ref[j], 1)
                _row_copy(h_hbm, tok, xbuf.at[s], j, sem.at[s]).start(priority=u % 2)
            return carry
        lax.fori_loop(0, MOE_TILE // DMA_UNROLL, body, 0)

    @pl.when(i == 0)
    def _():
        start_gather(ra_cur_ref, 0)

    @pl.when(i + 1 < n_valid)
    def _():
        start_gather(ra_nxt_ref, 1 - slot)

    e = te_ref[i]
    e_prev = te_ref[jnp.maximum(i - 1, 0)]

    @pl.when((i == 0) | (e != e_prev))
    def _():
        wgb_ref[...] = wg_ref[0].astype(BF16)
        wub_ref[...] = wu_ref[0].astype(BF16)
        wdb_ref[...] = wd_ref[0].astype(BF16)

    @pl.when(i < n_valid)
    def _():
        _wait_tiles(h_hbm, xbuf.at[slot], sem.at[slot])
        xb = jnp.concatenate(_load_token_slabs(xbuf.at[slot], MOE_TILE), axis=1).astype(BF16)
        hg = _dot(xb, wgb_ref[...])
        hu = _dot(xb, wub_ref[...])
        mid = (hg * jax.nn.sigmoid(hg) * hu).astype(BF16)
        _store_token_tiles(o_ref, _dot(mid, wdb_ref[...]))

    @pl.when(i >= n_valid)
    def _():
        o_ref[...] = jnp.zeros_like(o_ref)


def _moe_experts(h1, row_asg, tile_expert, n_valid, w_gate, w_up, w_down):
    D = D_MODEL
    n_tiles = row_asg.shape[0] // MOE_TILE
    smem_tile = lambda f: pl.BlockSpec((MOE_TILE,), f, memory_space=pltpu.SMEM)
    grid_spec = pltpu.PrefetchScalarGridSpec(
        num_scalar_prefetch=2,
        grid=(n_tiles,),
        in_specs=[smem_tile(lambda i, te, nv: (i,)),
                  smem_tile(lambda i, te, nv: (jnp.minimum(i + 1, n_tiles - 1),)),
                  pl.BlockSpec(memory_space=pl.ANY),
                  pl.BlockSpec((1, D, D_EXPERT), lambda i, te, nv: (te[i], 0, 0)),
                  pl.BlockSpec((1, D, D_EXPERT), lambda i, te, nv: (te[i], 0, 0)),
                  pl.BlockSpec((1, D_EXPERT, D), lambda i, te, nv: (te[i], 0, 0))],
        out_specs=pl.BlockSpec((MOE_TILE * SLABS, LANE), lambda i, te, nv: (i, 0)),
        scratch_shapes=[pltpu.VMEM((2, MOE_TILE * SLABS, LANE), F32), pltpu.SemaphoreType.DMA((2,)),
                        pltpu.VMEM((D, D_EXPERT), BF16), pltpu.VMEM((D, D_EXPERT), BF16),
                        pltpu.VMEM((D_EXPERT, D), BF16)],
    )
    return pl.pallas_call(
        _moe_kernel,
        out_shape=jax.ShapeDtypeStruct((n_tiles * MOE_TILE * SLABS, LANE), F32),
        grid_spec=grid_spec,
        compiler_params=_cparams(("arbitrary",)),
        name="moe_experts",
    )(tile_expert, n_valid, row_asg, row_asg, h1, w_gate, w_up, w_down)


SC_WINDOW = 64


def _sc_gather(table, idx):
    info = plsc.get_sparse_core_info()
    n_workers = info.num_cores * info.num_subcores
    n = idx.shape[0]
    per_worker = n // n_workers
    assert n % (n_workers * SC_WINDOW) == 0
    mesh = plsc.VectorSubcoreMesh(core_axis_name="c", subcore_axis_name="s")

    @functools.partial(
        pl.kernel, mesh=mesh,
        out_type=jax.ShapeDtypeStruct((n, SLABS, LANE), F32),
        scratch_types=[pltpu.VMEM((SC_WINDOW,), jnp.int32), pltpu.VMEM((SC_WINDOW, SLABS, LANE), F32),
                       pltpu.SemaphoreType.DMA],
        name="sc_row_gather",
    )
    def gather_kernel(table_hbm, idx_hbm, out_hbm, idx_v, rows_v, sem):
        worker = lax.axis_index("s") * info.num_cores + lax.axis_index("c")
        base = worker * per_worker

        @pl.loop(0, per_worker // SC_WINDOW)
        def _(it):
            off = base + it * SC_WINDOW
            pltpu.sync_copy(idx_hbm.at[pl.ds(off, SC_WINDOW)], idx_v)
            pltpu.async_copy(table_hbm.at[idx_v], rows_v, sem).wait()
            pltpu.sync_copy(rows_v, out_hbm.at[pl.ds(off, SC_WINDOW)])

    return gather_kernel(table, idx)


def _combine_planes_kernel(h_ref, ya_ref, yb_ref, gate_ref, g_ref, b_ref, o_ref):
    gate = gate_ref[...]
    tm = COMBINE_TILE
    ga = jnp.broadcast_to(gate[:, 0:1], (tm, LANE))
    gb = jnp.broadcast_to(gate[:, 1:2], (tm, LANE))
    hs = _load_token_slabs(h_ref, tm)
    ya = _load_token_slabs(ya_ref, tm)
    yb = _load_token_slabs(yb_ref, tm)
    z = [DEEPNORM_ALPHA * h + (ga * a + gb * b) for h, a, b in zip(hs, ya, yb)]
    inv_d = 1.0 / D_MODEL
    mu = sum(jnp.sum(t, axis=1, keepdims=True) for t in z) * inv_d
    zc = [t - mu for t in z]
    var = sum(jnp.sum(t * t, axis=1, keepdims=True) for t in zc) * inv_d
    rstd = lax.rsqrt(var + LN_EPS)
    for s in range(SLABS):
        cols = slice(s * LANE, (s + 1) * LANE)
        o_ref[:, cols] = zc[s] * rstd * g_ref[:, cols] + b_ref[:, cols]


def _combine_planes(h1, ypair, gates, ln2_g, ln2_b):
    D = D_MODEL
    M = h1.shape[0] // SLABS
    tm = COMBINE_TILE
    n = M // tm
    tiles = lambda f: pl.BlockSpec((tm * SLABS, LANE), f)
    return pl.pallas_call(
        _combine_planes_kernel,
        out_shape=jax.ShapeDtypeStruct((M, D), F32),
        grid=(n,),
        in_specs=[tiles(lambda i: (i, 0)), tiles(lambda i: (i, 0)), tiles(lambda i: (n + i, 0)),
                  pl.BlockSpec((tm, TOP_K), lambda i: (i, 0)), _full(ln2_g), _full(ln2_b)],
        out_specs=pl.BlockSpec((tm, D), lambda i: (i, 0)),
        compiler_params=_cparams(("parallel",)),
        name="combine",
    )(h1, ypair, ypair, gates, ln2_g, ln2_b)


def _combine_kernel(d_cur_ref, d_nxt_ref, h_ref, gate_ref, g_ref, b_ref, y_hbm, o_ref, ybuf, sem):
    i = pl.program_id(0)
    n = pl.num_programs(0)
    slot = lax.rem(i, 2)

    def start_gather(d_ref, s):
        def body(tt, carry):
            for u in range(DMA_UNROLL // TOP_K):
                t = tt * (DMA_UNROLL // TOP_K) + u
                for k in range(TOP_K):
                    _row_copy(y_hbm, d_ref[TOP_K * t + k], ybuf.at[s, k], t, sem.at[s]).start(priority=k % 2)
            return carry
        lax.fori_loop(0, COMBINE_TILE * TOP_K // DMA_UNROLL, body, 0)

    @pl.when(i == 0)
    def _():
        start_gather(d_cur_ref, 0)

    @pl.when(i + 1 < n)
    def _():
        start_gather(d_nxt_ref, 1 - slot)

    for k in range(TOP_K):
        _wait_tiles(y_hbm, ybuf.at[slot, k], sem.at[slot])

    gate = gate_ref[...]
    tm = COMBINE_TILE
    ga = jnp.broadcast_to(gate[:, 0:1], (tm, LANE))
    gb = jnp.broadcast_to(gate[:, 1:2], (tm, LANE))
    hs = _load_token_slabs(h_ref, tm)
    ya = _load_token_slabs(ybuf.at[slot, 0], tm)
    yb = _load_token_slabs(ybuf.at[slot, 1], tm)
    z = [DEEPNORM_ALPHA * h + (ga * a + gb * b) for h, a, b in zip(hs, ya, yb)]
    inv_d = 1.0 / D_MODEL
    mu = sum(jnp.sum(t, axis=1, keepdims=True) for t in z) * inv_d
    zc = [t - mu for t in z]
    var = sum(jnp.sum(t * t, axis=1, keepdims=True) for t in zc) * inv_d
    rstd = lax.rsqrt(var + LN_EPS)
    for s in range(SLABS):
        cols = slice(s * LANE, (s + 1) * LANE)
        o_ref[:, cols] = zc[s] * rstd * g_ref[:, cols] + b_ref[:, cols]


def _combine(h1, ybuf, dest, gates, ln2_g, ln2_b):
    D = D_MODEL
    M = h1.shape[0] // SLABS
    tm = COMBINE_TILE
    n = M // tm
    smem_tile = lambda f: pl.BlockSpec((TOP_K * tm,), f, memory_space=pltpu.SMEM)
    return pl.pallas_call(
        _combine_kernel,
        out_shape=jax.ShapeDtypeStruct((M, D), F32),
        grid=(n,),
        in_specs=[smem_tile(lambda i: (i,)), smem_tile(lambda i: (jnp.minimum(i + 1, n - 1),)),
                  pl.BlockSpec((tm * SLABS, LANE), lambda i: (i, 0)), pl.BlockSpec((tm, TOP_K), lambda i: (i, 0)),
                  _full(ln2_g), _full(ln2_b), pl.BlockSpec(memory_space=pl.ANY)],
        out_specs=pl.BlockSpec((tm, D), lambda i: (i, 0)),
        scratch_shapes=[pltpu.VMEM((2, TOP_K, tm * SLABS, LANE), F32), pltpu.SemaphoreType.DMA((2,))],
        compiler_params=_cparams(("arbitrary",)),
        name="combine",
    )(dest, dest, h1, gates, ln2_g, ln2_b, ybuf)


def _routing_plan(route):
    M = route.shape[0]
    eid = route[:, :TOP_K].astype(jnp.int32).reshape(-1)
    gates = route[:, TOP_K:2 * TOP_K]
    A = M * TOP_K
    onehot = (eid[:, None] == jnp.arange(N_EXPERTS, dtype=eid.dtype)[None, :]).astype(jnp.int32)
    csum = jnp.cumsum(onehot, axis=0)
    rank = jnp.sum(csum * onehot, axis=1) - 1
    counts = csum[-1]
    pcounts = (counts + MOE_TILE - 1) // MOE_TILE * MOE_TILE
    pends = jnp.cumsum(pcounts)
    pstarts = pends - pcounts
    dest = (jnp.sum(onehot * pstarts[None, :], axis=1) + rank).astype(jnp.int32)
    n_tiles = (A + N_EXPERTS * (MOE_TILE - 1) + MOE_TILE - 1) // MOE_TILE
    n_valid = (pends[-1] // MOE_TILE).astype(jnp.int32)
    tile_start = jnp.minimum(jnp.arange(n_tiles, dtype=jnp.int32) * MOE_TILE, pends[-1] - 1)
    tile_expert = jnp.sum((pends[None, :] <= tile_start[:, None]).astype(jnp.int32), axis=1)
    tile_expert = jnp.minimum(tile_expert, N_EXPERTS - 1).astype(jnp.int32)
    pad_lo = (pstarts + counts).astype(jnp.int32)
    pad_hi = pends.astype(jnp.int32)
    return gates, dest, n_tiles * MOE_TILE, tile_expert, n_valid.reshape(1), pad_lo, pad_hi


def kernel(x, meta, ln0_g, ln0_b, w_in, mu_shift, w0, w_decay_up, a0, w_a_up, w_g_up, k_k, k_a, r_k, gn_g, gn_b, conv_w, conv_b, w_rg, b_rg, w_ig, b_ig, lru_lambda, w_out, ln1_g, ln1_b, w_router_grp, b_router_grp, w_router_exp, b_router_exp, w_exp_gate, w_exp_up, w_exp_down, ln2_g, ln2_b):
    B, T, D = x.shape
    assert D == D_MODEL and T % 512 == 0 and w_in.shape[0] == 1
    assert (B * T * TOP_K) % INVERT_BLOCK == 0
    row = lambda p: p.reshape(1, -1).astype(F32)
    n_rw = 3 * RWKV_W
    w_in0 = w_in[0]

    def slots(p):
        pad = lambda a, n: jnp.pad(a, [(0, 0)] * (a.ndim - 1) + [(0, n - a.shape[-1])])
        zw = p[..., n_rw:n_rw + DECAY_RANK]
        za = p[..., n_rw + DECAY_RANK:n_rw + DECAY_RANK + AAA_RANK]
        zg = p[..., n_rw + DECAY_RANK + AAA_RANK:n_rw + DECAY_RANK + AAA_RANK + GATE_RANK]
        return jnp.concatenate([p[..., :n_rw], pad(zw, LANE), pad(za, LANE), pad(zg, ZG_SLOT)], axis=-1)

    rwkv_cols = n_rw + DECAY_RANK + AAA_RANK + GATE_RANK
    w_r = slots(w_in0[:, :rwkv_cols]).astype(BF16)
    w_l = w_in0[:, rwkv_cols:].astype(BF16)
    ur, ul, ur_t, ul_t = _in_projection(x, meta, row(ln0_g), row(ln0_b), w_r, w_l)

    pad_rows = lambda a, n: jnp.pad(a, ((0, n - a.shape[0]), (0, 0)))
    rwkv_params = (slots(mu_shift[0][None, :]).astype(F32), row(w0[0]), pad_rows(w_decay_up[0], LANE).astype(BF16),
                   row(a0[0]), pad_rows(w_a_up[0], LANE).astype(BF16), pad_rows(w_g_up[0], ZG_SLOT).astype(BF16),
                   row(k_k[0]), row(k_a[0]), row(r_k[0]), row(gn_g[0]), row(gn_b[0]))
    y_rwkv = _rwkv_pipe_mixer(ur, ur_t, rwkv_params)

    blockdiag = lambda w: jax.scipy.linalg.block_diag(*[w[i] for i in range(LRU_BLOCKS)]).astype(BF16)
    lru_params = (conv_w[0], row(conv_b[0]), blockdiag(w_rg[0]), row(b_rg[0]), blockdiag(w_ig[0]), row(b_ig[0]),
                  row(lru_lambda[0]))
    y_lru = _lru_mixer(ul, ul_t, lru_params)

    w_rt = jnp.concatenate([w_router_grp[0], w_router_exp[0]], axis=1)
    w_rt = jnp.pad(w_rt, ((0, 0), (0, LANE - w_rt.shape[1])))
    wrt_hi = w_rt.astype(BF16)
    wrt_lo = (w_rt - wrt_hi.astype(F32)).astype(BF16)
    b_rt = jnp.concatenate([b_router_grp[0], b_router_exp[0]])
    b_rt = jnp.pad(b_rt, (0, LANE - b_rt.shape[0])).reshape(1, LANE)
    wo = w_out[0].astype(BF16)
    h1, route = _out_projection(x, y_rwkv, y_lru, row(ln0_g), row(ln0_b), wo[:RWKV_W], wo[RWKV_W:],
                                row(ln1_g[0]), row(ln1_b[0]), wrt_hi, wrt_lo, b_rt)

    M = B * T
    h1 = h1.reshape(M * SLABS, LANE)
    gates, dest, n_slots, tile_expert, n_valid, pad_lo, pad_hi = _routing_plan(route.reshape(M, LANE))
    row_asg = _invert_slots(dest, n_slots, pad_lo, pad_hi)
    ybuf = _moe_experts(h1, row_asg, tile_expert, n_valid, w_exp_gate[0], w_exp_up[0], w_exp_down[0])
    dest_planes = dest.reshape(M, TOP_K).T.reshape(-1)
    ypair = _sc_gather(ybuf.reshape(-1, SLABS, LANE), dest_planes).reshape(-1, LANE)
    out = _combine_planes(h1, ypair, gates, row(ln2_g[0]), row(ln2_b[0]))
    return out.reshape(B, T, D)
```

```python
import math

import jax
import jax.numpy as jnp
from jax import lax
from jax.experimental import pallas as pl
from jax.experimental.pallas import tpu as pltpu

F32 = jnp.float32
BF16 = jnp.bfloat16

D_MODEL = 1024
N_META = 16
RWKV_W = 512
RWKV_HEAD = 64
DECAY_RANK = 64
AAA_RANK = 64
GATE_RANK = 160
LRU_W = 512
LRU_BLOCKS = 8
CONV_WIDTH = 4
LRU_C = 8.0
N_GROUPS = 4
EXPERTS_PER_GROUP = 8
N_EXPERTS = N_GROUPS * EXPERTS_PER_GROUP
TOP_K = 2
D_EXPERT = 512
LN_EPS = 1e-5
GN_EPS = 64e-5
DEEPNORM_ALPHA = 2.0 ** 0.25

LANE = 128
OFF_R, OFF_K, OFF_V = 0, RWKV_W, 2 * RWKV_W
OFF_ZW = 3 * RWKV_W
OFF_ZA = OFF_ZW + LANE
OFF_ZG = OFF_ZA + LANE
ZG_SLOT = 2 * LANE
UR_W = OFF_ZG + ZG_SLOT
UL_W = 2 * LRU_W

TAIL = 256
CHUNK = 64
HEADS_PER_GROUP = 4
GW = HEADS_PER_GROUP * RWKV_HEAD
N_HGROUPS = RWKV_W // GW
LRU_TILE = TAIL
MOE_TILE = 256
COMBINE_TILE = 256
INVERT_BLOCK = 4096
DMA_UNROLL = 8
V7X_VMEM_BYTES = 64 * 1024 * 1024
VMEM_LIMIT = V7X_VMEM_BYTES - 8 * 1024 * 1024


def _cparams(sem):
    return pltpu.CompilerParams(dimension_semantics=sem, vmem_limit_bytes=VMEM_LIMIT)


def _layer_norm(x, g, b):
    mu = jnp.mean(x, -1, keepdims=True)
    xc = x - mu
    var = jnp.mean(xc * xc, -1, keepdims=True)
    return xc * lax.rsqrt(var + LN_EPS) * g + b


def _dot(a, b):
    return jnp.dot(a, b, preferred_element_type=F32)


def _dot_nt(a, b):
    return lax.dot_general(a, b, (((1,), (1,)), ((), ())), preferred_element_type=F32)


def _dot_tn(a, b):
    return lax.dot_general(a, b, (((0,), (0,)), ((), ())), preferred_element_type=F32)


def _full(a):
    return pl.BlockSpec(a.shape, lambda *_: (0,) * a.ndim)


def _inproj_kernel(x_ref, g_ref, b_ref, wr_ref, wl_ref, ur_ref, ul_ref):
    h = _layer_norm(x_ref[0], g_ref[...], b_ref[...]).astype(BF16)
    ur_ref[0] = _dot(h, wr_ref[...])
    ul_ref[0] = _dot(h, wl_ref[...])


def _inproj_tail_kernel(x_ref, g_ref, b_ref, wr_ref, wl_ref, ur_ref, ul_ref):
    h = _layer_norm(x_ref[...], g_ref[...], b_ref[...]).astype(BF16)
    rows = lax.broadcasted_iota(jnp.int32, (TAIL, 1), 0)
    valid = (rows >= TAIL - N_META).astype(F32)
    ur_ref[...] = _dot(h, wr_ref[...]) * valid
    ul_ref[...] = _dot(h, wl_ref[...]) * valid


def _in_projection(x, meta, ln0_g, ln0_b, w_r, w_l):
    B, T, D = x.shape
    tm = 512
    ur, ul = pl.pallas_call(
        _inproj_kernel,
        out_shape=(jax.ShapeDtypeStruct((B, T, UR_W), F32), jax.ShapeDtypeStruct((B, T, UL_W), F32)),
        grid=(B, T // tm),
        in_specs=[pl.BlockSpec((1, tm, D), lambda b, i: (b, i, 0)), _full(ln0_g), _full(ln0_b), _full(w_r), _full(w_l)],
        out_specs=(pl.BlockSpec((1, tm, UR_W), lambda b, i: (b, i, 0)),
                   pl.BlockSpec((1, tm, UL_W), lambda b, i: (b, i, 0))),
        compiler_params=_cparams(("parallel", "parallel")),
        name="inproj",
    )(x, ln0_g, ln0_b, w_r, w_l)
    tail_x = jnp.concatenate([jnp.zeros((TAIL - N_META, D), F32), meta.astype(F32)], axis=0)
    ur_t, ul_t = pl.pallas_call(
        _inproj_tail_kernel,
        out_shape=(jax.ShapeDtypeStruct((TAIL, UR_W), F32), jax.ShapeDtypeStruct((TAIL, UL_W), F32)),
        grid=(1,),
        in_specs=[_full(tail_x), _full(ln0_g), _full(ln0_b), _full(w_r), _full(w_l)],
        out_specs=(pl.BlockSpec((TAIL, UR_W), lambda i: (0, 0)), pl.BlockSpec((TAIL, UL_W), lambda i: (0, 0))),
        compiler_params=_cparams(("arbitrary",)),
        name="inproj_tail",
    )(tail_x, ln0_g, ln0_b, w_r, w_l)
    return ur, ul, ur_t, ul_t


def _rwkv_pipe_kernel(u_ref, ut_ref, mu_ref, w0_ref, wdu_ref, a0_ref, wau_ref, wgu_ref, kk_ref, ka_ref, rk_ref,
                      gng_ref, gnb_ref, bones_ref, bm_ref, eye_ref, msl_ref, mil_ref,
                      m8_ref, m16_ref, m32_ref, m64_ref, y_ref,
                      s_ref, prev_ref, y0_s, q_s, mc_s, nc_s, we_s, bonus_s, g_s, yraw_s):
    s_id = pl.program_id(0)
    nb = u_ref.shape[0]
    blk = u_ref.shape[1]
    npc = blk // CHUNK
    nseq = nb * npc
    seq_rows = lambda q: slice(q * CHUNK, (q + 1) * CHUNK)
    per_seq = lambda f: jnp.concatenate([f(q) for q in range(nseq)], axis=0)
    w_slot = lax.rem(s_id, 2)
    r_slot = 1 - w_slot

    @pl.when(s_id == 0)
    def _():
        s_ref[...] = jnp.zeros_like(s_ref)
        prev_ref[...] = jnp.zeros_like(prev_ref)
        for ref in (y0_s, q_s, mc_s, nc_s, we_s, bonus_s, g_s):
            ref[1] = jnp.zeros(ref.shape[1:], ref.dtype)

    b16 = lambda t: t.astype(BF16)
    bones = bones_ref[...]
    head_sum = lambda t: _dot(b16(t), bones)
    bm = bm_ref[...]
    bm16 = b16(bm)
    tile4 = lambda t: jnp.concatenate([t] * HEADS_PER_GROUP, axis=0)
    fold4 = lambda t: sum(t[i * CHUNK:(i + 1) * CHUNK] for i in range(HEADS_PER_GROUP))
    bd = lambda t: tile4(b16(t)) * bm16

    chains = [(b, hg) for b in range(nb) for hg in range(N_HGROUPS)]
    states = {c: s_ref[c[0], c[1]] for c in chains}

    def recurrent_chunk(j):
        for b, hg in chains:
            sl = slice(hg * GW, (hg + 1) * GW)
            q = b * npc + j
            rq = seq_rows(q)
            s = states[(b, hg)]
            yraw_s[rq, sl] = y0_s[r_slot, rq, sl] + _dot_nt(q_s[r_slot, rq, sl], bd(s))
            states[(b, hg)] = (s * we_s[r_slot, q * 8:q * 8 + 1, sl] + _dot(b16(s), bd(mc_s[r_slot, rq, sl]))
                               + nc_s[r_slot, rq, sl])

    prepared = {}

    def prepare(b):
        lo = b * blk
        u = jnp.where(s_id == 0, ut_ref[...], u_ref[b])
        row = lax.broadcasted_iota(jnp.int32, u.shape, 0)
        u_prev = jnp.where(row == 0, prev_ref[b:b + 1, :], pltpu.roll(u, 1, 0))
        prev_ref[b:b + 1, :] = u[blk - 1:blk, :]
        x = u + (u_prev - u) * mu_ref[...]
        r = x[:, OFF_R:OFF_R + RWKV_W]
        k = x[:, OFF_K:OFF_K + RWKV_W]
        v = x[:, OFF_V:OFF_V + RWKV_W]
        zw = x[:, OFF_ZW:OFF_ZW + LANE]
        za = x[:, OFF_ZA:OFF_ZA + LANE]
        zg = x[:, OFF_ZG:OFF_ZG + ZG_SLOT]
        yield
        z = w0_ref[...] + _dot(b16(jnp.tanh(zw)), wdu_ref[...])
        logw = -math.exp(-0.5) * jax.nn.sigmoid(z)
        a = jax.nn.sigmoid(a0_ref[...] + _dot(b16(za), wau_ref[...]))
        g = _dot(b16(jax.nn.sigmoid(zg)), wgu_ref[...])
        kk = k * kk_ref[...]
        kk = kk / jnp.maximum(jnp.sqrt(head_sum(kk * kk)), 1e-12)
        k = k * (1.0 + (a - 1.0) * ka_ref[...])
        kka = kk * a
        bonus_s[w_slot, lo:lo + blk, :] = head_sum(r * k * rk_ref[...]) * v
        g_s[w_slot, lo:lo + blk, :] = g
        yield
        cl = logw
        row_in_chunk = jnp.bitwise_and(lax.broadcasted_iota(jnp.int32, cl.shape, 0), CHUNK - 1)
        d = 1
        while d < CHUNK:
            cl = cl + jnp.where(row_in_chunk >= d, pltpu.roll(cl, d, 0), 0.0)
            d *= 2
        yield
        cl_last = jnp.concatenate(
            [jnp.broadcast_to(cl[(j + 1) * CHUNK - 1:(j + 1) * CHUNK, :], (CHUNK, RWKV_W)) for j in range(npc)], axis=0)
        e_neg = jnp.exp(-cl)
        e_end = jnp.exp(cl_last - cl)
        w_end = jnp.exp(cl_last)
        for j in range(npc):
            q = b * npc + j
            we_s[w_slot, q * 8:(q + 1) * 8, :] = w_end[j * CHUNK:j * CHUNK + 8, :]
        prepared[b] = dict(rt=r * jnp.exp(cl), kt=k * e_neg, at=-kk * jnp.exp(cl - logw), bt=kka * e_neg,
                           kw=k * e_end, bw=kka * e_end, v=v)
        yield

    eye = eye_ref[...]
    msl = msl_ref[...]
    mil = mil_ref[...]
    rows2 = lambda x, y: jnp.concatenate([x, y], axis=0)
    cols2 = lambda x, y: jnp.concatenate([x, y], axis=1)
    each = lambda f, *ls: [f(*xs) for xs in zip(*ls)]

    def solve(b, p):
        probs = [(slice(j * CHUNK, (j + 1) * CHUNK), slice(hg * GW, (hg + 1) * GW))
                 for j in range(npc) for hg in range(N_HGROUPS)]
        pick = lambda t: [t[rq, sl] for rq, sl in probs]
        at_w, rt_w, v_w = pick(p["at"]), pick(p["rt"]), pick(p["v"])
        lhs = each(lambda x, y: b16(rows2(x, y)), at_w, rt_w)
        ab = each(_dot_nt, lhs, each(bd, pick(p["bt"])))
        ak = each(_dot_nt, lhs, each(bd, pick(p["kt"])))
        yield
        a_ab = each(lambda t: t[:CHUNK] * msl, ab)
        a_rb = each(lambda t: b16(t[CHUNK:] * mil), ab)
        a_xk = each(lambda t: b16(rows2(t[:CHUNK] * msl, t[CHUNK:] * mil)), ak)
        a0 = each(lambda t: b16(t * m8_ref[...]), a_ab)
        a2 = each(lambda t: b16(_dot(t, bd(t))), a0)
        yield
        a4 = each(lambda t: b16(_dot(t, bd(t))), a2)
        p1 = each(lambda t: eye + t.astype(F32), a0)
        p1 = each(lambda q, t: q + _dot(b16(q), bd(t)), p1, a2)
        yield
        tt = each(lambda q, t: q + _dot(b16(q), bd(t)), p1, a4)
        yield
        for m_ref in (m16_ref, m32_ref, m64_ref):
            tb = each(b16, tt)
            off = each(lambda t: b16(t * m_ref[...]), a_ab)
            half = each(lambda x, y: b16(_dot(x, bd(y))), tb, off)
            yield
            tt = each(lambda t, x, y: t + _dot(x, bd(y)), tt, half, tb)
            yield
        tb = each(b16, tt)
        xv = each(lambda x, y: _dot(x, bd(y)), a_xk, v_w)
        yield
        u0 = each(lambda x, y: _dot(x, bd(y[:CHUNK])), tb, xv)
        ta = each(lambda x, y: _dot(x, bd(y)), tb, at_w)
        yield
        y0 = each(lambda x, y, z: _dot(x, bd(y)) + z[CHUNK:], a_rb, u0, xv)
        qq = each(lambda x, y, z: x + _dot(y, bd(z)), rt_w, a_rb, ta)
        yield
        left = each(lambda x, y, z: b16(rows2(cols2(x, y), cols2(jnp.zeros_like(z), z))), ta, u0, v_w)
        right = each(lambda x, y: b16(rows2(x, y)), pick(p["bw"]), pick(p["kw"]))
        mn = each(_dot_tn, left, right)
        for i, (rq, sl) in enumerate(probs):
            rows = slice(b * blk + rq.start, b * blk + rq.stop)
            y0_s[w_slot, rows, sl] = y0[i]
            q_s[w_slot, rows, sl] = b16(qq[i])
            mc_s[w_slot, rows, sl] = b16(fold4(mn[i][:GW] * bm))
            nc_s[w_slot, rows, sl] = fold4(mn[i][GW:] * bm)
        yield

    first = prepare(0)
    for j in range(npc):
        recurrent_chunk(j)
        if j * 4 // npc != (j + 1) * 4 // npc or j == npc - 1:
            for _ in range((j + 1) * 4 // npc - j * 4 // npc):
                next(first, None)
    for _ in first:
        pass
    for (b, hg), s in states.items():
        s_ref[b, hg] = s

    y = yraw_s[...]
    inv_n = 1.0 / RWKV_HEAD
    ym = head_sum(y) * inv_n
    yc = y - ym
    yv = head_sum(yc * yc) * inv_n
    yn = yc * lax.rsqrt(yv + GN_EPS) * gng_ref[...] + gnb_ref[...]
    y_ref[...] = ((yn + bonus_s[r_slot]) * g_s[r_slot]).astype(y_ref.dtype).reshape(y_ref.shape)

    for b in range(nb):
        solver = solve(b, prepared[b])
        nxt = prepare(b + 1) if b + 1 < nb else iter(())
        for level, _ in enumerate(solver):
            if level % 4 == 3:
                next(nxt, None)
        for _ in nxt:
            pass


def _rwkv_masks():
    f = lambda m: m.astype(F32)
    i = jnp.arange(GW)[:, None]
    j = jnp.arange(GW)[None, :]
    bm = f((i // RWKV_HEAD) == (j // RWKV_HEAD))
    t = jnp.arange(CHUNK)[:, None]
    s = (jnp.arange(GW) % CHUNK)[None, :]
    same = lambda n: (t // n) == (s // n)
    msl = f(t > s)
    mil = f(t >= s)
    m8 = f(same(8))
    m16 = f(same(16) & ~same(8))
    m32 = f(same(32) & ~same(16))
    m64 = f(~same(32))
    eye = f(t == s)
    hi = jnp.arange(RWKV_W)
    bones = ((hi[:, None] // RWKV_HEAD) == (hi[None, :] // RWKV_HEAD)).astype(BF16)
    return bones, bm, eye, msl, mil, m8, m16, m32, m64


def _rwkv_pipe_mixer(ur, ur_tail, params):
    B, T, _ = ur.shape
    blk = TAIL
    assert T % blk == 0 and blk % CHUNK == 0 and CHUNK == RWKV_HEAD
    n_blocks = T // blk
    rows = B * blk
    consts = _rwkv_masks()
    in_map = lambda s: (0, jnp.clip(s - 1, 0, n_blocks - 1), 0)
    out_map = lambda s: (0, jnp.clip(s - 2, 0, n_blocks - 1), 0)
    slot2 = lambda w, dt: pltpu.VMEM((2, rows, w), dt)
    return pl.pallas_call(
        _rwkv_pipe_kernel,
        out_shape=jax.ShapeDtypeStruct((B, T, RWKV_W), BF16),
        grid=(n_blocks + 2,),
        in_specs=[pl.BlockSpec((B, blk, UR_W), in_map), _full(ur_tail)]
                 + [_full(p) for p in params] + [_full(m) for m in consts],
        out_specs=pl.BlockSpec((B, blk, RWKV_W), out_map),
        scratch_shapes=[pltpu.VMEM((B, N_HGROUPS, CHUNK, GW), F32), pltpu.VMEM((B, UR_W), F32),
                        slot2(RWKV_W, F32), slot2(RWKV_W, BF16), slot2(RWKV_W, BF16), slot2(RWKV_W, F32),
                        pltpu.VMEM((2, 8 * rows // CHUNK, RWKV_W), F32), slot2(RWKV_W, F32), slot2(RWKV_W, F32),
                        pltpu.VMEM((rows, RWKV_W), F32)],
        compiler_params=_cparams(("arbitrary",)),
        name="rwkv7",
    )(ur, ur_tail, *params, *consts)


def _gelu_tanh(x):
    return 0.5 * x * (1.0 + jnp.tanh(math.sqrt(2.0 / math.pi) * (x + 0.044715 * (x * x * x))))


LRU_CARRY = 8


def _lru_kernel(u_ref, ut_ref, cw_ref, cb_ref, wrg_ref, brg_ref, wig_ref, big_ref, lam_ref, y_ref,
                xs_ref, hprev_ref):
    c = pl.program_id(0)
    nb = u_ref.shape[0]
    nrow = nb * LRU_TILE

    @pl.when(c == 0)
    def _():
        xs_ref[...] = jnp.zeros_like(xs_ref)
        hprev_ref[...] = jnp.zeros_like(hprev_ref)

    u_x = u_ref[...].reshape(nrow, UL_W)
    u = jnp.where(c == 0, jnp.concatenate([ut_ref[...]] * nb, axis=0), u_x)
    xl = u[:, :LRU_W]
    gl = u[:, LRU_W:]
    row = jnp.bitwise_and(lax.broadcasted_iota(jnp.int32, (nrow, LRU_W), 0), LRU_TILE - 1)
    in_group = jnp.bitwise_and(row, 7)
    roll_in_group = lambda t, d: pltpu.roll(t.reshape(t.shape[0] // 8, 8, LRU_W), d, 1).reshape(t.shape)
    xl_prev = jnp.concatenate(
        [p for b in range(nb) for p in (xs_ref[b], xl[b * LRU_TILE:(b + 1) * LRU_TILE - 8])], axis=0)
    xc = cb_ref[...] + cw_ref[CONV_WIDTH - 1:CONV_WIDTH, :] * xl
    for d in range(1, CONV_WIDTH):
        tap = jnp.where(in_group >= d, roll_in_group(xl, d), roll_in_group(xl_prev, d))
        xc = xc + cw_ref[CONV_WIDTH - 1 - d:CONV_WIDTH - d, :] * tap
    for b in range(nb):
        xs_ref[b] = xl[(b + 1) * LRU_TILE - 8:(b + 1) * LRU_TILE]

    xcb = xc.astype(BF16)
    gate_r = jax.nn.sigmoid(_dot(xcb, wrg_ref[...]) + brg_ref[...])
    gate_i = jax.nn.sigmoid(_dot(xcb, wig_ref[...]) + big_ref[...])
    lam = lam_ref[...]
    log_sig = -(jnp.maximum(-lam, 0.0) + jnp.log1p(jnp.exp(-jnp.abs(lam))))
    log_a = LRU_C * gate_r * log_sig
    a = jnp.exp(log_a)
    mult = jnp.sqrt(jnp.maximum(1.0 - jnp.exp(2.0 * log_a), 0.0))
    b = mult * gate_i * xc
    b = jnp.where((c == 0) & (row < LRU_TILE - N_META), 0.0, b)

    d = 1
    while d < 8:
        keep = in_group >= d
        a_sh = jnp.where(keep, roll_in_group(a, d), 1.0)
        b_sh = jnp.where(keep, roll_in_group(b, d), 0.0)
        b = a * b_sh + b
        a = a * a_sh
        d *= 2
    groups = []
    for bi in range(nb):
        carry = hprev_ref[bi:bi + 1, :]
        for gi in range(LRU_TILE // 8):
            lo = bi * LRU_TILE + gi * 8
            hg = b[lo:lo + 8] + a[lo:lo + 8] * carry
            carry = hg[7:8, :]
            groups.append(hg)
        hprev_ref[bi:bi + 1, :] = carry
    h = jnp.concatenate(groups, axis=0)
    y_ref[...] = (h * _gelu_tanh(gl)).astype(y_ref.dtype).reshape(y_ref.shape)


def _lru_mixer(ul, ul_tail, params):
    B, T, _ = ul.shape
    assert TAIL == LRU_TILE
    x_map = lambda c: (0, jnp.maximum(c - 1, 0), 0)
    return pl.pallas_call(
        _lru_kernel,
        out_shape=jax.ShapeDtypeStruct((B, T, LRU_W), BF16),
        grid=(T // LRU_TILE + 1,),
        in_specs=[pl.BlockSpec((B, LRU_TILE, UL_W), x_map), _full(ul_tail)] + [_full(p) for p in params],
        out_specs=pl.BlockSpec((B, LRU_TILE, LRU_W), x_map),
        scratch_shapes=[pltpu.VMEM((B, LRU_CARRY, LRU_W), F32), pltpu.VMEM((B, LRU_W), F32)],
        compiler_params=_cparams(("arbitrary",)),
        name="rglru",
    )(ul, ul_tail, *params)


def _route(lg):
    lane = lax.broadcasted_iota(jnp.int32, lg.shape, 1)
    neg = jnp.float32(-jnp.inf)
    rmax = lambda t: jnp.max(t, axis=1, keepdims=True)
    first = lambda hit: jnp.min(jnp.where(hit, lane, LANE), axis=1, keepdims=True)
    is_grp = lane < N_GROUPS
    gl = jnp.where(is_grp, lg, neg)
    gmax = rmax(gl)
    g_sel = first(gl == gmax)
    p_g = 1.0 / jnp.sum(jnp.where(is_grp, jnp.exp(lg - gmax), 0.0), axis=1, keepdims=True)
    ex = lane - N_GROUPS
    in_grp = (ex >= 0) & (ex < N_EXPERTS) & (jnp.right_shift(ex, 3) == g_sel)
    el = jnp.where(in_grp, lg, neg)
    v1 = rmax(el)
    i1 = first(el == v1)
    el2 = jnp.where(lane == i1, neg, el)
    v2 = rmax(el2)
    i2 = first(el2 == v2)
    t = jnp.exp(v2 - v1)
    gate1 = p_g / (1.0 + t)
    gate2 = p_g * t / (1.0 + t)
    e1 = (i1 - N_GROUPS).astype(F32)
    e2 = (i2 - N_GROUPS).astype(F32)
    return jnp.where(lane == 0, e1, jnp.where(lane == 1, e2, jnp.where(lane == 2, gate1, jnp.where(lane == 3, gate2, 0.0))))


SLABS = D_MODEL // LANE


def _store_token_tiles(ref, val):
    n = val.shape[0]
    for s in range(SLABS):
        ref[pl.ds(s, n, stride=SLABS), :] = val[:, s * LANE:(s + 1) * LANE]


def _load_token_slabs(ref, n):
    return [ref[pl.ds(s, n, stride=SLABS), :] for s in range(SLABS)]


def _outproj_kernel(x_ref, yr_ref, yl_ref, g0_ref, b0_ref, wor_ref, wol_ref, g1_ref, b1_ref,
                    wrt_hi_ref, wrt_lo_ref, brt_ref, h1_ref, rt_ref):
    h0 = _layer_norm(x_ref[0], g0_ref[...], b0_ref[...])
    mix = _dot(yr_ref[0], wor_ref[...]) + _dot(yl_ref[0], wol_ref[...])
    h1 = _layer_norm(DEEPNORM_ALPHA * h0 + mix, g1_ref[...], b1_ref[...])
    _store_token_tiles(h1_ref.at[0], h1)
    hi = h1.astype(BF16)
    lo = (h1 - hi.astype(F32)).astype(BF16)
    w_hi = wrt_hi_ref[...]
    lg = _dot(hi, w_hi) + (_dot(hi, wrt_lo_ref[...]) + _dot(lo, w_hi)) + brt_ref[...]
    rt_ref[0] = _route(lg)


def _out_projection(x, y_rwkv, y_lru, ln0_g, ln0_b, wo_r, wo_l, ln1_g, ln1_b, wrt_hi, wrt_lo, brt):
    B, T, D = x.shape
    tm = 512
    rows = lambda w: pl.BlockSpec((1, tm, w), lambda b, i: (b, i, 0))
    return pl.pallas_call(
        _outproj_kernel,
        out_shape=(jax.ShapeDtypeStruct((B, T * SLABS, LANE), F32), jax.ShapeDtypeStruct((B, T, LANE), F32)),
        grid=(B, T // tm),
        in_specs=[rows(D), rows(RWKV_W), rows(LRU_W), _full(ln0_g), _full(ln0_b), _full(wo_r), _full(wo_l),
                  _full(ln1_g), _full(ln1_b), _full(wrt_hi), _full(wrt_lo), _full(brt)],
        out_specs=(pl.BlockSpec((1, tm * SLABS, LANE), lambda b, i: (b, i, 0)), rows(LANE)),
        compiler_params=_cparams(("parallel", "parallel")),
        name="outproj",
    )(x, y_rwkv, y_lru, ln0_g, ln0_b, wo_r, wo_l, ln1_g, ln1_b, wrt_hi, wrt_lo, brt)


def _invert_kernel(pad_lo_ref, pad_hi_ref, dest_ref, out_ref):
    i = pl.program_id(0)

    @pl.when(i == 0)
    def _():
        def zero(j, carry):
            out_ref[j] = 0
            return carry
        for e in range(N_EXPERTS):
            lax.fori_loop(pad_lo_ref[e], pad_hi_ref[e], zero, 0)
        lax.fori_loop(pad_hi_ref[N_EXPERTS - 1], out_ref.shape[0], zero, 0)

    base = i * INVERT_BLOCK

    def body(j, carry):
        out_ref[dest_ref[j]] = base + j
        return carry

    lax.fori_loop(0, INVERT_BLOCK, body, 0, unroll=8)


def _invert_slots(dest, n_slots, pad_lo, pad_hi):
    A = dest.shape[0]
    grid_spec = pltpu.PrefetchScalarGridSpec(
        num_scalar_prefetch=2,
        grid=(A // INVERT_BLOCK,),
        in_specs=[pl.BlockSpec((INVERT_BLOCK,), lambda i, lo, hi: (i,), memory_space=pltpu.SMEM)],
        out_specs=pl.BlockSpec(memory_space=pltpu.SMEM),
    )
    return pl.pallas_call(
        _invert_kernel,
        out_shape=jax.ShapeDtypeStruct((n_slots,), jnp.int32),
        grid_spec=grid_spec,
        compiler_params=_cparams(("arbitrary",)),
        name="invert_slots",
    )(pad_lo, pad_hi, dest)


def _row_copy(src_hbm, src_row, dst_ref, dst_row, sem):
    return pltpu.make_async_copy(src_hbm.at[pl.ds(src_row * SLABS, SLABS), :],
                                 dst_ref.at[pl.ds(dst_row * SLABS, SLABS), :], sem)


def _wait_tiles(src_hbm, dst_ref, sem):
    pltpu.make_async_copy(src_hbm.at[pl.ds(0, dst_ref.shape[0]), :], dst_ref, sem).wait()


def _moe_kernel(te_ref, nv_ref, ra_cur_ref, ra_nxt_ref, h_hbm, wg_ref, wu_ref, wd_ref, o_ref,
                xbuf, sem, wgb_ref, wub_ref, wdb_ref):
    i = pl.program_id(0)
    n_valid = nv_ref[0]
    slot = lax.rem(i, 2)

    def start_gather(ra_ref, s):
        def body(jj, carry):
            for u in range(DMA_UNROLL):
                j = jj * DMA_UNROLL + u
                tok = lax.shift_right_logical(ra_ref[j], 1)
                _row_copy(h_hbm, tok, xbuf.at[s], j, sem.at[s]).start(priority=u % 2)
            return carry
        lax.fori_loop(0, MOE_TILE // DMA_UNROLL, body, 0)

    @pl.when(i == 0)
    def _():
        start_gather(ra_cur_ref, 0)

    @pl.when(i + 1 < n_valid)
    def _():
        start_gather(ra_nxt_ref, 1 - slot)

    e = te_ref[i]
    e_prev = te_ref[jnp.maximum(i - 1, 0)]

    @pl.when((i == 0) | (e != e_prev))
    def _():
        wgb_ref[...] = wg_ref[0].astype(BF16)
        wub_ref[...] = wu_ref[0].astype(BF16)
        wdb_ref[...] = wd_ref[0].astype(BF16)

    @pl.when(i < n_valid)
    def _():
        _wait_tiles(h_hbm, xbuf.at[slot], sem.at[slot])
        xb = jnp.concatenate(_load_token_slabs(xbuf.at[slot], MOE_TILE), axis=1).astype(BF16)
        hg = _dot(xb, wgb_ref[...])
        hu = _dot(xb, wub_ref[...])
        mid = (hg * jax.nn.sigmoid(hg) * hu).astype(BF16)
        _store_token_tiles(o_ref, _dot(mid, wdb_ref[...]))

    @pl.when(i >= n_valid)
    def _():
        o_ref[...] = jnp.zeros_like(o_ref)


def _moe_experts(h1, row_asg, tile_expert, n_valid, w_gate, w_up, w_down):
    D = D_MODEL
    n_tiles = row_asg.shape[0] // MOE_TILE
    smem_tile = lambda f: pl.BlockSpec((MOE_TILE,), f, memory_space=pltpu.SMEM)
    grid_spec = pltpu.PrefetchScalarGridSpec(
        num_scalar_prefetch=2,
        grid=(n_tiles,),
        in_specs=[smem_tile(lambda i, te, nv: (i,)),
                  smem_tile(lambda i, te, nv: (jnp.minimum(i + 1, n_tiles - 1),)),
                  pl.BlockSpec(memory_space=pl.ANY),
                  pl.BlockSpec((1, D, D_EXPERT), lambda i, te, nv: (te[i], 0, 0)),
                  pl.BlockSpec((1, D, D_EXPERT), lambda i, te, nv: (te[i], 0, 0)),
                  pl.BlockSpec((1, D_EXPERT, D), lambda i, te, nv: (te[i], 0, 0))],
        out_specs=pl.BlockSpec((MOE_TILE * SLABS, LANE), lambda i, te, nv: (i, 0)),
        scratch_shapes=[pltpu.VMEM((2, MOE_TILE * SLABS, LANE), F32), pltpu.SemaphoreType.DMA((2,)),
                        pltpu.VMEM((D, D_EXPERT), BF16), pltpu.VMEM((D, D_EXPERT), BF16),
                        pltpu.VMEM((D_EXPERT, D), BF16)],
    )
    return pl.pallas_call(
        _moe_kernel,
        out_shape=jax.ShapeDtypeStruct((n_tiles * MOE_TILE * SLABS, LANE), F32),
        grid_spec=grid_spec,
        compiler_params=_cparams(("arbitrary",)),
        name="moe_experts",
    )(tile_expert, n_valid, row_asg, row_asg, h1, w_gate, w_up, w_down)


def _combine_kernel(d_cur_ref, d_nxt_ref, h_ref, gate_ref, g_ref, b_ref, y_hbm, o_ref, ybuf, sem):
    i = pl.program_id(0)
    n = pl.num_programs(0)
    slot = lax.rem(i, 2)

    def start_gather(d_ref, s):
        def body(tt, carry):
            for u in range(DMA_UNROLL // TOP_K):
                t = tt * (DMA_UNROLL // TOP_K) + u
                for k in range(TOP_K):
                    _row_copy(y_hbm, d_ref[TOP_K * t + k], ybuf.at[s, k], t, sem.at[s]).start(priority=k % 2)
            return carry
        lax.fori_loop(0, COMBINE_TILE * TOP_K // DMA_UNROLL, body, 0)

    @pl.when(i == 0)
    def _():
        start_gather(d_cur_ref, 0)

    @pl.when(i + 1 < n)
    def _():
        start_gather(d_nxt_ref, 1 - slot)

    for k in range(TOP_K):
        _wait_tiles(y_hbm, ybuf.at[slot, k], sem.at[slot])

    gate = gate_ref[...]
    tm = COMBINE_TILE
    ga = jnp.broadcast_to(gate[:, 0:1], (tm, LANE))
    gb = jnp.broadcast_to(gate[:, 1:2], (tm, LANE))
    hs = _load_token_slabs(h_ref, tm)
    ya = _load_token_slabs(ybuf.at[slot, 0], tm)
    yb = _load_token_slabs(ybuf.at[slot, 1], tm)
    z = [DEEPNORM_ALPHA * h + (ga * a + gb * b) for h, a, b in zip(hs, ya, yb)]
    inv_d = 1.0 / D_MODEL
    mu = sum(jnp.sum(t, axis=1, keepdims=True) for t in z) * inv_d
    zc = [t - mu for t in z]
    var = sum(jnp.sum(t * t, axis=1, keepdims=True) for t in zc) * inv_d
    rstd = lax.rsqrt(var + LN_EPS)
    for s in range(SLABS):
        cols = slice(s * LANE, (s + 1) * LANE)
        o_ref[:, cols] = zc[s] * rstd * g_ref[:, cols] + b_ref[:, cols]


def _combine(h1, ybuf, dest, gates, ln2_g, ln2_b):
    D = D_MODEL
    M = h1.shape[0] // SLABS
    tm = COMBINE_TILE
    n = M // tm
    smem_tile = lambda f: pl.BlockSpec((TOP_K * tm,), f, memory_space=pltpu.SMEM)
    return pl.pallas_call(
        _combine_kernel,
        out_shape=jax.ShapeDtypeStruct((M, D), F32),
        grid=(n,),
        in_specs=[smem_tile(lambda i: (i,)), smem_tile(lambda i: (jnp.minimum(i + 1, n - 1),)),
                  pl.BlockSpec((tm * SLABS, LANE), lambda i: (i, 0)), pl.BlockSpec((tm, TOP_K), lambda i: (i, 0)),
                  _full(ln2_g), _full(ln2_b), pl.BlockSpec(memory_space=pl.ANY)],
        out_specs=pl.BlockSpec((tm, D), lambda i: (i, 0)),
        scratch_shapes=[pltpu.VMEM((2, TOP_K, tm * SLABS, LANE), F32), pltpu.SemaphoreType.DMA((2,))],
        compiler_params=_cparams(("arbitrary",)),
        name="combine",
    )(dest, dest, h1, gates, ln2_g, ln2_b, ybuf)


def _routing_plan(route):
    M = route.shape[0]
    eid = route[:, :TOP_K].astype(jnp.int32).reshape(-1)
    gates = route[:, TOP_K:2 * TOP_K]
    A = M * TOP_K
    onehot = (eid[:, None] == jnp.arange(N_EXPERTS, dtype=eid.dtype)[None, :]).astype(jnp.int32)
    csum = jnp.cumsum(onehot, axis=0)
    rank = jnp.sum(csum * onehot, axis=1) - 1
    counts = csum[-1]
    pcounts = (counts + MOE_TILE - 1) // MOE_TILE * MOE_TILE
    pends = jnp.cumsum(pcounts)
    pstarts = pends - pcounts
    dest = (jnp.sum(onehot * pstarts[None, :], axis=1) + rank).astype(jnp.int32)
    n_tiles = (A + N_EXPERTS * (MOE_TILE - 1) + MOE_TILE - 1) // MOE_TILE
    n_valid = (pends[-1] // MOE_TILE).astype(jnp.int32)
    tile_start = jnp.minimum(jnp.arange(n_tiles, dtype=jnp.int32) * MOE_TILE, pends[-1] - 1)
    tile_expert = jnp.sum((pends[None, :] <= tile_start[:, None]).astype(jnp.int32), axis=1)
    tile_expert = jnp.minimum(tile_expert, N_EXPERTS - 1).astype(jnp.int32)
    pad_lo = (pstarts + counts).astype(jnp.int32)
    pad_hi = pends.astype(jnp.int32)
    return gates, dest, n_tiles * MOE_TILE, tile_expert, n_valid.reshape(1), pad_lo, pad_hi


def kernel(x, meta, ln0_g, ln0_b, w_in, mu_shift, w0, w_decay_up, a0, w_a_up, w_g_up, k_k, k_a, r_k, gn_g, gn_b, conv_w, conv_b, w_rg, b_rg, w_ig, b_ig, lru_lambda, w_out, ln1_g, ln1_b, w_router_grp, b_router_grp, w_router_exp, b_router_exp, w_exp_gate, w_exp_up, w_exp_down, ln2_g, ln2_b):
    B, T, D = x.shape
    assert D == D_MODEL and T % 512 == 0 and w_in.shape[0] == 1
    assert (B * T * TOP_K) % INVERT_BLOCK == 0
    row = lambda p: p.reshape(1, -1).astype(F32)
    n_rw = 3 * RWKV_W
    w_in0 = w_in[0]

    def slots(p):
        pad = lambda a, n: jnp.pad(a, [(0, 0)] * (a.ndim - 1) + [(0, n - a.shape[-1])])
        zw = p[..., n_rw:n_rw + DECAY_RANK]
        za = p[..., n_rw + DECAY_RANK:n_rw + DECAY_RANK + AAA_RANK]
        zg = p[..., n_rw + DECAY_RANK + AAA_RANK:n_rw + DECAY_RANK + AAA_RANK + GATE_RANK]
        return jnp.concatenate([p[..., :n_rw], pad(zw, LANE), pad(za, LANE), pad(zg, ZG_SLOT)], axis=-1)

    rwkv_cols = n_rw + DECAY_RANK + AAA_RANK + GATE_RANK
    w_r = slots(w_in0[:, :rwkv_cols]).astype(BF16)
    w_l = w_in0[:, rwkv_cols:].astype(BF16)
    ur, ul, ur_t, ul_t = _in_projection(x, meta, row(ln0_g), row(ln0_b), w_r, w_l)

    pad_rows = lambda a, n: jnp.pad(a, ((0, n - a.shape[0]), (0, 0)))
    rwkv_params = (slots(mu_shift[0][None, :]).astype(F32), row(w0[0]), pad_rows(w_decay_up[0], LANE).astype(BF16),
                   row(a0[0]), pad_rows(w_a_up[0], LANE).astype(BF16), pad_rows(w_g_up[0], ZG_SLOT).astype(BF16),
                   row(k_k[0]), row(k_a[0]), row(r_k[0]), row(gn_g[0]), row(gn_b[0]))
    y_rwkv = _rwkv_pipe_mixer(ur, ur_t, rwkv_params)

    blockdiag = lambda w: jax.scipy.linalg.block_diag(*[w[i] for i in range(LRU_BLOCKS)]).astype(BF16)
    lru_params = (conv_w[0], row(conv_b[0]), blockdiag(w_rg[0]), row(b_rg[0]), blockdiag(w_ig[0]), row(b_ig[0]),
                  row(lru_lambda[0]))
    y_lru = _lru_mixer(ul, ul_t, lru_params)

    w_rt = jnp.concatenate([w_router_grp[0], w_router_exp[0]], axis=1)
    w_rt = jnp.pad(w_rt, ((0, 0), (0, LANE - w_rt.shape[1])))
    wrt_hi = w_rt.astype(BF16)
    wrt_lo = (w_rt - wrt_hi.astype(F32)).astype(BF16)
    b_rt = jnp.concatenate([b_router_grp[0], b_router_exp[0]])
    b_rt = jnp.pad(b_rt, (0, LANE - b_rt.shape[0])).reshape(1, LANE)
    wo = w_out[0].astype(BF16)
    h1, route = _out_projection(x, y_rwkv, y_lru, row(ln0_g), row(ln0_b), wo[:RWKV_W], wo[RWKV_W:],
                                row(ln1_g[0]), row(ln1_b[0]), wrt_hi, wrt_lo, b_rt)

    M = B * T
    h1 = h1.reshape(M * SLABS, LANE)
    gates, dest, n_slots, tile_expert, n_valid, pad_lo, pad_hi = _routing_plan(route.reshape(M, LANE))
    row_asg = _invert_slots(dest, n_slots, pad_lo, pad_hi)
    ybuf = _moe_experts(h1, row_asg, tile_expert, n_valid, w_exp_gate[0], w_exp_up[0], w_exp_down[0])
    out = _combine(h1, ybuf, dest, gates, row(ln2_g[0]), row(ln2_b[0]))
    return out.reshape(B, T, D)
```

```python
import functools
import math

import jax
import jax.numpy as jnp
from jax import lax
from jax.experimental import pallas as pl
from jax.experimental.pallas import tpu as pltpu
from jax.experimental.pallas import tpu_sc as plsc

F32 = jnp.float32
BF16 = jnp.bfloat16

D_MODEL = 1024
N_META = 16
RWKV_W = 512
RWKV_HEAD = 64
DECAY_RANK = 64
AAA_RANK = 64
GATE_RANK = 160
LRU_W = 512
LRU_BLOCKS = 8
CONV_WIDTH = 4
LRU_C = 8.0
N_GROUPS = 4
EXPERTS_PER_GROUP = 8
N_EXPERTS = N_GROUPS * EXPERTS_PER_GROUP
TOP_K = 2
D_EXPERT = 512
LN_EPS = 1e-5
GN_EPS = 64e-5
DEEPNORM_ALPHA = 2.0 ** 0.25

LANE = 128
OFF_R, OFF_K, OFF_V = 0, RWKV_W, 2 * RWKV_W
OFF_ZW = 3 * RWKV_W
OFF_ZA = OFF_ZW + LANE
OFF_ZG = OFF_ZA + LANE
ZG_SLOT = 2 * LANE
UR_W = OFF_ZG + ZG_SLOT
UL_W = 2 * LRU_W

TAIL = 256
CHUNK = 64
HEADS_PER_GROUP = 4
GW = HEADS_PER_GROUP * RWKV_HEAD
N_HGROUPS = RWKV_W // GW
LRU_TILE = TAIL
MOE_TILE = 256
COMBINE_TILE = 256
INVERT_BLOCK = 4096
DMA_UNROLL = 8
V7X_VMEM_BYTES = 64 * 1024 * 1024
VMEM_LIMIT = V7X_VMEM_BYTES - 8 * 1024 * 1024


def _cparams(sem):
    return pltpu.CompilerParams(dimension_semantics=sem, vmem_limit_bytes=VMEM_LIMIT)


def _layer_norm(x, g, b):
    mu = jnp.mean(x, -1, keepdims=True)
    xc = x - mu
    var = jnp.mean(xc * xc, -1, keepdims=True)
    return xc * lax.rsqrt(var + LN_EPS) * g + b


def _dot(a, b):
    return jnp.dot(a, b, preferred_element_type=F32)


def _dot_nt(a, b):
    return lax.dot_general(a, b, (((1,), (1,)), ((), ())), preferred_element_type=F32)


def _dot_tn(a, b):
    return lax.dot_general(a, b, (((0,), (0,)), ((), ())), preferred_element_type=F32)


def _full(a):
    return pl.BlockSpec(a.shape, lambda *_: (0,) * a.ndim)


def _inproj_kernel(x_ref, g_ref, b_ref, wr_ref, wl_ref, ur_ref, ul_ref):
    h = _layer_norm(x_ref[0], g_ref[...], b_ref[...]).astype(BF16)
    ur_ref[0] = _dot(h, wr_ref[...])
    ul_ref[0] = _dot(h, wl_ref[...])


def _inproj_tail_kernel(x_ref, g_ref, b_ref, wr_ref, wl_ref, ur_ref, ul_ref):
    h = _layer_norm(x_ref[...], g_ref[...], b_ref[...]).astype(BF16)
    rows = lax.broadcasted_iota(jnp.int32, (TAIL, 1), 0)
    valid = (rows >= TAIL - N_META).astype(F32)
    ur_ref[...] = _dot(h, wr_ref[...]) * valid
    ul_ref[...] = _dot(h, wl_ref[...]) * valid


def _in_projection(x, meta, ln0_g, ln0_b, w_r, w_l):
    B, T, D = x.shape
    tm = 512
    ur, ul = pl.pallas_call(
        _inproj_kernel,
        out_shape=(jax.ShapeDtypeStruct((B, T, UR_W), F32), jax.ShapeDtypeStruct((B, T, UL_W), F32)),
        grid=(B, T // tm),
        in_specs=[pl.BlockSpec((1, tm, D), lambda b, i: (b, i, 0)), _full(ln0_g), _full(ln0_b), _full(w_r), _full(w_l)],
        out_specs=(pl.BlockSpec((1, tm, UR_W), lambda b, i: (b, i, 0)),
                   pl.BlockSpec((1, tm, UL_W), lambda b, i: (b, i, 0))),
        compiler_params=_cparams(("parallel", "parallel")),
        name="inproj",
    )(x, ln0_g, ln0_b, w_r, w_l)
    tail_x = jnp.concatenate([jnp.zeros((TAIL - N_META, D), F32), meta.astype(F32)], axis=0)
    ur_t, ul_t = pl.pallas_call(
        _inproj_tail_kernel,
        out_shape=(jax.ShapeDtypeStruct((TAIL, UR_W), F32), jax.ShapeDtypeStruct((TAIL, UL_W), F32)),
        grid=(1,),
        in_specs=[_full(tail_x), _full(ln0_g), _full(ln0_b), _full(w_r), _full(w_l)],
        out_specs=(pl.BlockSpec((TAIL, UR_W), lambda i: (0, 0)), pl.BlockSpec((TAIL, UL_W), lambda i: (0, 0))),
        compiler_params=_cparams(("arbitrary",)),
        name="inproj_tail",
    )(tail_x, ln0_g, ln0_b, w_r, w_l)
    return ur, ul, ur_t, ul_t


def _rwkv_pipe_kernel(u_ref, ut_ref, mu_ref, w0_ref, wdu_ref, a0_ref, wau_ref, wgu_ref, kk_ref, ka_ref, rk_ref,
                      gng_ref, gnb_ref, bones_ref, bm_ref, eye_ref, msl_ref, mil_ref,
                      m8_ref, m16_ref, m32_ref, m64_ref, y_ref,
                      s_ref, prev_ref, y0_s, q_s, mc_s, nc_s, we_s, bonus_s, g_s, yraw_s):
    s_id = pl.program_id(0)
    nb = u_ref.shape[0]
    blk = u_ref.shape[1]
    npc = blk // CHUNK
    nseq = nb * npc
    seq_rows = lambda q: slice(q * CHUNK, (q + 1) * CHUNK)
    per_seq = lambda f: jnp.concatenate([f(q) for q in range(nseq)], axis=0)
    w_slot = lax.rem(s_id, 2)
    r_slot = 1 - w_slot

    @pl.when(s_id == 0)
    def _():
        s_ref[...] = jnp.zeros_like(s_ref)
        prev_ref[...] = jnp.zeros_like(prev_ref)
        for ref in (y0_s, q_s, mc_s, nc_s, we_s, bonus_s, g_s):
            ref[1] = jnp.zeros(ref.shape[1:], ref.dtype)

    b16 = lambda t: t.astype(BF16)
    bones = bones_ref[...]
    head_sum = lambda t: _dot(b16(t), bones)
    bm = bm_ref[...]
    bm16 = b16(bm)
    tile4 = lambda t: jnp.concatenate([t] * HEADS_PER_GROUP, axis=0)
    fold4 = lambda t: sum(t[i * CHUNK:(i + 1) * CHUNK] for i in range(HEADS_PER_GROUP))
    bd = lambda t: tile4(b16(t)) * bm16

    chains = [(b, hg) for b in range(nb) for hg in range(N_HGROUPS)]
    states = {c: s_ref[c[0], c[1]] for c in chains}

    def recurrent_chunk(j):
        for b, hg in chains:
            sl = slice(hg * GW, (hg + 1) * GW)
            q = b * npc + j
            rq = seq_rows(q)
            s = states[(b, hg)]
            yraw_s[rq, sl] = y0_s[r_slot, rq, sl] + _dot_nt(q_s[r_slot, rq, sl], bd(s))
            states[(b, hg)] = (s * we_s[r_slot, q * 8:q * 8 + 1, sl] + _dot(b16(s), bd(mc_s[r_slot, rq, sl]))
                               + nc_s[r_slot, rq, sl])

    prepared = {}

    def prepare(b):
        lo = b * blk
        u = jnp.where(s_id == 0, ut_ref[...], u_ref[b])
        row = lax.broadcasted_iota(jnp.int32, u.shape, 0)
        u_prev = jnp.where(row == 0, prev_ref[b:b + 1, :], pltpu.roll(u, 1, 0))
        prev_ref[b:b + 1, :] = u[blk - 1:blk, :]
        x = u + (u_prev - u) * mu_ref[...]
        r = x[:, OFF_R:OFF_R + RWKV_W]
        k = x[:, OFF_K:OFF_K + RWKV_W]
        v = x[:, OFF_V:OFF_V + RWKV_W]
        zw = x[:, OFF_ZW:OFF_ZW + LANE]
        za = x[:, OFF_ZA:OFF_ZA + LANE]
        zg = x[:, OFF_ZG:OFF_ZG + ZG_SLOT]
        yield
        z = w0_ref[...] + _dot(b16(jnp.tanh(zw)), wdu_ref[...])
        logw = -math.exp(-0.5) * jax.nn.sigmoid(z)
        a = jax.nn.sigmoid(a0_ref[...] + _dot(b16(za), wau_ref[...]))
        g = _dot(b16(jax.nn.sigmoid(zg)), wgu_ref[...])
        kk = k * kk_ref[...]
        kk = kk / jnp.maximum(jnp.sqrt(head_sum(kk * kk)), 1e-12)
        k = k * (1.0 + (a - 1.0) * ka_ref[...])
        kka = kk * a
        bonus_s[w_slot, lo:lo + blk, :] = head_sum(r * k * rk_ref[...]) * v
        g_s[w_slot, lo:lo + blk, :] = g
        yield
        cl = logw
        row_in_chunk = jnp.bitwise_and(lax.broadcasted_iota(jnp.int32, cl.shape, 0), CHUNK - 1)
        d = 1
        while d < CHUNK:
            cl = cl + jnp.where(row_in_chunk >= d, pltpu.roll(cl, d, 0), 0.0)
            d *= 2
        yield
        cl_last = jnp.concatenate(
            [jnp.broadcast_to(cl[(j + 1) * CHUNK - 1:(j + 1) * CHUNK, :], (CHUNK, RWKV_W)) for j in range(npc)], axis=0)
        e_neg = jnp.exp(-cl)
        e_end = jnp.exp(cl_last - cl)
        w_end = jnp.exp(cl_last)
        for j in range(npc):
            q = b * npc + j
            we_s[w_slot, q * 8:(q + 1) * 8, :] = w_end[j * CHUNK:j * CHUNK + 8, :]
        prepared[b] = dict(rt=r * jnp.exp(cl), kt=k * e_neg, at=-kk * jnp.exp(cl - logw), bt=kka * e_neg,
                           kw=k * e_end, bw=kka * e_end, v=v)
        yield

    eye = eye_ref[...]
    msl = msl_ref[...]
    mil = mil_ref[...]
    rows2 = lambda x, y: jnp.concatenate([x, y], axis=0)
    cols2 = lambda x, y: jnp.concatenate([x, y], axis=1)
    each = lambda f, *ls: [f(*xs) for xs in zip(*ls)]

    def solve(b, p):
        probs = [(slice(j * CHUNK, (j + 1) * CHUNK), slice(hg * GW, (hg + 1) * GW))
                 for j in range(npc) for hg in range(N_HGROUPS)]
        pick = lambda t: [t[rq, sl] for rq, sl in probs]
        at_w, rt_w, v_w = pick(p["at"]), pick(p["rt"]), pick(p["v"])
        lhs = each(lambda x, y: b16(rows2(x, y)), at_w, rt_w)
        ab = each(_dot_nt, lhs, each(bd, pick(p["bt"])))
        ak = each(_dot_nt, lhs, each(bd, pick(p["kt"])))
        yield
        a_ab = each(lambda t: t[:CHUNK] * msl, ab)
        a_rb = each(lambda t: b16(t[CHUNK:] * mil), ab)
        a_xk = each(lambda t: b16(rows2(t[:CHUNK] * msl, t[CHUNK:] * mil)), ak)
        a0 = each(lambda t: b16(t * m8_ref[...]), a_ab)
        a2 = each(lambda t: b16(_dot(t, bd(t))), a0)
        yield
        a4 = each(lambda t: b16(_dot(t, bd(t))), a2)
        p1 = each(lambda t: eye + t.astype(F32), a0)
        p1 = each(lambda q, t: q + _dot(b16(q), bd(t)), p1, a2)
        yield
        tt = each(lambda q, t: q + _dot(b16(q), bd(t)), p1, a4)
        yield
        for m_ref in (m16_ref, m32_ref, m64_ref):
            tb = each(b16, tt)
            off = each(lambda t: b16(t * m_ref[...]), a_ab)
            half = each(lambda x, y: b16(_dot(x, bd(y))), tb, off)
            yield
            tt = each(lambda t, x, y: t + _dot(x, bd(y)), tt, half, tb)
            yield
        tb = each(b16, tt)
        xv = each(lambda x, y: _dot(x, bd(y)), a_xk, v_w)
        yield
        u0 = each(lambda x, y: _dot(x, bd(y[:CHUNK])), tb, xv)
        ta = each(lambda x, y: _dot(x, bd(y)), tb, at_w)
        yield
        y0 = each(lambda x, y, z: _dot(x, bd(y)) + z[CHUNK:], a_rb, u0, xv)
        qq = each(lambda x, y, z: x + _dot(y, bd(z)), rt_w, a_rb, ta)
        yield
        left = each(lambda x, y, z: b16(rows2(cols2(x, y), cols2(jnp.zeros_like(z), z))), ta, u0, v_w)
        right = each(lambda x, y: b16(rows2(x, y)), pick(p["bw"]), pick(p["kw"]))
        mn = each(_dot_tn, left, right)
        for i, (rq, sl) in enumerate(probs):
            rows = slice(b * blk + rq.start, b * blk + rq.stop)
            y0_s[w_slot, rows, sl] = y0[i]
            q_s[w_slot, rows, sl] = b16(qq[i])
            mc_s[w_slot, rows, sl] = b16(fold4(mn[i][:GW] * bm))
            nc_s[w_slot, rows, sl] = fold4(mn[i][GW:] * bm)
        yield

    first = prepare(0)
    for j in range(npc):
        recurrent_chunk(j)
        if j * 4 // npc != (j + 1) * 4 // npc or j == npc - 1:
            for _ in range((j + 1) * 4 // npc - j * 4 // npc):
                next(first, None)
    for _ in first:
        pass
    for (b, hg), s in states.items():
        s_ref[b, hg] = s

    y = yraw_s[...]
    inv_n = 1.0 / RWKV_HEAD
    ym = head_sum(y) * inv_n
    yc = y - ym
    yv = head_sum(yc * yc) * inv_n
    yn = yc * lax.rsqrt(yv + GN_EPS) * gng_ref[...] + gnb_ref[...]
    y_ref[...] = ((yn + bonus_s[r_slot]) * g_s[r_slot]).astype(y_ref.dtype).reshape(y_ref.shape)

    for b in range(nb):
        solver = solve(b, prepared[b])
        nxt = prepare(b + 1) if b + 1 < nb else iter(())
        for level, _ in enumerate(solver):
            if level % 4 == 3:
                next(nxt, None)
        for _ in nxt:
            pass


def _rwkv_masks():
    f = lambda m: m.astype(F32)
    i = jnp.arange(GW)[:, None]
    j = jnp.arange(GW)[None, :]
    bm = f((i // RWKV_HEAD) == (j // RWKV_HEAD))
    t = jnp.arange(CHUNK)[:, None]
    s = (jnp.arange(GW) % CHUNK)[None, :]
    same = lambda n: (t // n) == (s // n)
    msl = f(t > s)
    mil = f(t >= s)
    m8 = f(same(8))
    m16 = f(same(16) & ~same(8))
    m32 = f(same(32) & ~same(16))
    m64 = f(~same(32))
    eye = f(t == s)
    hi = jnp.arange(RWKV_W)
    bones = ((hi[:, None] // RWKV_HEAD) == (hi[None, :] // RWKV_HEAD)).astype(BF16)
    return bones, bm, eye, msl, mil, m8, m16, m32, m64


def _rwkv_pipe_mixer(ur, ur_tail, params):
    B, T, _ = ur.shape
    blk = TAIL
    assert T % blk == 0 and blk % CHUNK == 0 and CHUNK == RWKV_HEAD
    n_blocks = T // blk
    rows = B * blk
    consts = _rwkv_masks()
    in_map = lambda s: (0, jnp.clip(s - 1, 0, n_blocks - 1), 0)
    out_map = lambda s: (0, jnp.clip(s - 2, 0, n_blocks - 1), 0)
    slot2 = lambda w, dt: pltpu.VMEM((2, rows, w), dt)
    return pl.pallas_call(
        _rwkv_pipe_kernel,
        out_shape=jax.ShapeDtypeStruct((B, T, RWKV_W), BF16),
        grid=(n_blocks + 2,),
        in_specs=[pl.BlockSpec((B, blk, UR_W), in_map), _full(ur_tail)]
                 + [_full(p) for p in params] + [_full(m) for m in consts],
        out_specs=pl.BlockSpec((B, blk, RWKV_W), out_map),
        scratch_shapes=[pltpu.VMEM((B, N_HGROUPS, CHUNK, GW), F32), pltpu.VMEM((B, UR_W), F32),
                        slot2(RWKV_W, F32), slot2(RWKV_W, BF16), slot2(RWKV_W, BF16), slot2(RWKV_W, F32),
                        pltpu.VMEM((2, 8 * rows // CHUNK, RWKV_W), F32), slot2(RWKV_W, F32), slot2(RWKV_W, F32),
                        pltpu.VMEM((rows, RWKV_W), F32)],
        compiler_params=_cparams(("arbitrary",)),
        name="rwkv7",
    )(ur, ur_tail, *params, *consts)


def _gelu_tanh(x):
    return 0.5 * x * (1.0 + jnp.tanh(math.sqrt(2.0 / math.pi) * (x + 0.044715 * (x * x * x))))


LRU_CARRY = 8


def _lru_kernel(u_ref, ut_ref, cw_ref, cb_ref, wrg_ref, brg_ref, wig_ref, big_ref, lam_ref, y_ref,
                xs_ref, hprev_ref):
    c = pl.program_id(0)
    nb = u_ref.shape[0]
    nrow = nb * LRU_TILE

    @pl.when(c == 0)
    def _():
        xs_ref[...] = jnp.zeros_like(xs_ref)
        hprev_ref[...] = jnp.zeros_like(hprev_ref)

    u_x = u_ref[...].reshape(nrow, UL_W)
    u = jnp.where(c == 0, jnp.concatenate([ut_ref[...]] * nb, axis=0), u_x)
    xl = u[:, :LRU_W]
    gl = u[:, LRU_W:]
    row = jnp.bitwise_and(lax.broadcasted_iota(jnp.int32, (nrow, LRU_W), 0), LRU_TILE - 1)
    in_group = jnp.bitwise_and(row, 7)
    roll_in_group = lambda t, d: pltpu.roll(t.reshape(t.shape[0] // 8, 8, LRU_W), d, 1).reshape(t.shape)
    xl_prev = jnp.concatenate(
        [p for b in range(nb) for p in (xs_ref[b], xl[b * LRU_TILE:(b + 1) * LRU_TILE - 8])], axis=0)
    xc = cb_ref[...] + cw_ref[CONV_WIDTH - 1:CONV_WIDTH, :] * xl
    for d in range(1, CONV_WIDTH):
        tap = jnp.where(in_group >= d, roll_in_group(xl, d), roll_in_group(xl_prev, d))
        xc = xc + cw_ref[CONV_WIDTH - 1 - d:CONV_WIDTH - d, :] * tap
    for b in range(nb):
        xs_ref[b] = xl[(b + 1) * LRU_TILE - 8:(b + 1) * LRU_TILE]

    xcb = xc.astype(BF16)
    gate_r = jax.nn.sigmoid(_dot(xcb, wrg_ref[...]) + brg_ref[...])
    gate_i = jax.nn.sigmoid(_dot(xcb, wig_ref[...]) + big_ref[...])
    lam = lam_ref[...]
    log_sig = -(jnp.maximum(-lam, 0.0) + jnp.log1p(jnp.exp(-jnp.abs(lam))))
    log_a = LRU_C * gate_r * log_sig
    a = jnp.exp(log_a)
    mult = jnp.sqrt(jnp.maximum(1.0 - jnp.exp(2.0 * log_a), 0.0))
    b = mult * gate_i * xc
    b = jnp.where((c == 0) & (row < LRU_TILE - N_META), 0.0, b)

    d = 1
    while d < 8:
        keep = in_group >= d
        a_sh = jnp.where(keep, roll_in_group(a, d), 1.0)
        b_sh = jnp.where(keep, roll_in_group(b, d), 0.0)
        b = a * b_sh + b
        a = a * a_sh
        d *= 2
    groups = []
    for bi in range(nb):
        carry = hprev_ref[bi:bi + 1, :]
        for gi in range(LRU_TILE // 8):
            lo = bi * LRU_TILE + gi * 8
            hg = b[lo:lo + 8] + a[lo:lo + 8] * carry
            carry = hg[7:8, :]
            groups.append(hg)
        hprev_ref[bi:bi + 1, :] = carry
    h = jnp.concatenate(groups, axis=0)
    y_ref[...] = (h * _gelu_tanh(gl)).astype(y_ref.dtype).reshape(y_ref.shape)


def _lru_mixer(ul, ul_tail, params):
    B, T, _ = ul.shape
    assert TAIL == LRU_TILE
    x_map = lambda c: (0, jnp.maximum(c - 1, 0), 0)
    return pl.pallas_call(
        _lru_kernel,
        out_shape=jax.ShapeDtypeStruct((B, T, LRU_W), BF16),
        grid=(T // LRU_TILE + 1,),
        in_specs=[pl.BlockSpec((B, LRU_TILE, UL_W), x_map), _full(ul_tail)] + [_full(p) for p in params],
        out_specs=pl.BlockSpec((B, LRU_TILE, LRU_W), x_map),
        scratch_shapes=[pltpu.VMEM((B, LRU_CARRY, LRU_W), F32), pltpu.VMEM((B, LRU_W), F32)],
        compiler_params=_cparams(("arbitrary",)),
        name="rglru",
    )(ul, ul_tail, *params)


def _route(lg):
    lane = lax.broadcasted_iota(jnp.int32, lg.shape, 1)
    neg = jnp.float32(-jnp.inf)
    rmax = lambda t: jnp.max(t, axis=1, keepdims=True)
    first = lambda hit: jnp.min(jnp.where(hit, lane, LANE), axis=1, keepdims=True)
    is_grp = lane < N_GROUPS
    gl = jnp.where(is_grp, lg, neg)
    gmax = rmax(gl)
    g_sel = first(gl == gmax)
    p_g = 1.0 / jnp.sum(jnp.where(is_grp, jnp.exp(lg - gmax), 0.0), axis=1, keepdims=True)
    ex = lane - N_GROUPS
    in_grp = (ex >= 0) & (ex < N_EXPERTS) & (jnp.right_shift(ex, 3) == g_sel)
    el = jnp.where(in_grp, lg, neg)
    v1 = rmax(el)
    i1 = first(el == v1)
    el2 = jnp.where(lane == i1, neg, el)
    v2 = rmax(el2)
    i2 = first(el2 == v2)
    t = jnp.exp(v2 - v1)
    gate1 = p_g / (1.0 + t)
    gate2 = p_g * t / (1.0 + t)
    e1 = (i1 - N_GROUPS).astype(F32)
    e2 = (i2 - N_GROUPS).astype(F32)
    return jnp.where(lane == 0, e1, jnp.where(lane == 1, e2, jnp.where(lane == 2, gate1, jnp.where(lane == 3, gate2, 0.0))))


SLABS = D_MODEL // LANE


def _store_token_tiles(ref, val):
    n = val.shape[0]
    for s in range(SLABS):
        ref[pl.ds(s, n, stride=SLABS), :] = val[:, s * LANE:(s + 1) * LANE]


def _load_token_slabs(ref, n):
    return [ref[pl.ds(s, n, stride=SLABS), :] for s in range(SLABS)]


def _outproj_kernel(x_ref, yr_ref, yl_ref, g0_ref, b0_ref, wor_ref, wol_ref, g1_ref, b1_ref,
                    wrt_hi_ref, wrt_lo_ref, brt_ref, h1_ref, rt_ref):
    h0 = _layer_norm(x_ref[0], g0_ref[...], b0_ref[...])
    mix = _dot(yr_ref[0], wor_ref[...]) + _dot(yl_ref[0], wol_ref[...])
    h1 = _layer_norm(DEEPNORM_ALPHA * h0 + mix, g1_ref[...], b1_ref[...])
    _store_token_tiles(h1_ref.at[0], h1)
    hi = h1.astype(BF16)
    lo = (h1 - hi.astype(F32)).astype(BF16)
    w_hi = wrt_hi_ref[...]
    lg = _dot(hi, w_hi) + (_dot(hi, wrt_lo_ref[...]) + _dot(lo, w_hi)) + brt_ref[...]
    rt_ref[0] = _route(lg)


def _out_projection(x, y_rwkv, y_lru, ln0_g, ln0_b, wo_r, wo_l, ln1_g, ln1_b, wrt_hi, wrt_lo, brt):
    B, T, D = x.shape
    tm = 512
    rows = lambda w: pl.BlockSpec((1, tm, w), lambda b, i: (b, i, 0))
    return pl.pallas_call(
        _outproj_kernel,
        out_shape=(jax.ShapeDtypeStruct((B, T * SLABS, LANE), F32), jax.ShapeDtypeStruct((B, T, LANE), F32)),
        grid=(B, T // tm),
        in_specs=[rows(D), rows(RWKV_W), rows(LRU_W), _full(ln0_g), _full(ln0_b), _full(wo_r), _full(wo_l),
                  _full(ln1_g), _full(ln1_b), _full(wrt_hi), _full(wrt_lo), _full(brt)],
        out_specs=(pl.BlockSpec((1, tm * SLABS, LANE), lambda b, i: (b, i, 0)), rows(LANE)),
        compiler_params=_cparams(("parallel", "parallel")),
        name="outproj",
    )(x, y_rwkv, y_lru, ln0_g, ln0_b, wo_r, wo_l, ln1_g, ln1_b, wrt_hi, wrt_lo, brt)


def _invert_kernel(pad_lo_ref, pad_hi_ref, dest_ref, out_ref):
    i = pl.program_id(0)

    @pl.when(i == 0)
    def _():
        def zero(j, carry):
            out_ref[j] = 0
            return carry
        for e in range(N_EXPERTS):
            lax.fori_loop(pad_lo_ref[e], pad_hi_ref[e], zero, 0)
        lax.fori_loop(pad_hi_ref[N_EXPERTS - 1], out_ref.shape[0], zero, 0)

    base = i * INVERT_BLOCK

    def body(j, carry):
        out_ref[dest_ref[j]] = base + j
        return carry

    lax.fori_loop(0, INVERT_BLOCK, body, 0, unroll=8)


def _invert_slots(dest, n_slots, pad_lo, pad_hi):
    A = dest.shape[0]
    grid_spec = pltpu.PrefetchScalarGridSpec(
        num_scalar_prefetch=2,
        grid=(A // INVERT_BLOCK,),
        in_specs=[pl.BlockSpec((INVERT_BLOCK,), lambda i, lo, hi: (i,), memory_space=pltpu.SMEM)],
        out_specs=pl.BlockSpec(memory_space=pltpu.SMEM),
    )
    return pl.pallas_call(
        _invert_kernel,
        out_shape=jax.ShapeDtypeStruct((n_slots,), jnp.int32),
        grid_spec=grid_spec,
        compiler_params=_cparams(("arbitrary",)),
        name="invert_slots",
    )(pad_lo, pad_hi, dest)


SC_WINDOW = 64


def _sc_gather(table, idx):
    info = plsc.get_sparse_core_info()
    n_workers = info.num_cores * info.num_subcores
    n = idx.shape[0]
    per_worker = n // n_workers
    assert n % (n_workers * SC_WINDOW) == 0
    mesh = plsc.VectorSubcoreMesh(core_axis_name="c", subcore_axis_name="s")

    @functools.partial(
        pl.kernel, mesh=mesh,
        out_type=jax.ShapeDtypeStruct((n, SLABS, LANE), F32),
        scratch_types=[pltpu.VMEM((SC_WINDOW,), jnp.int32), pltpu.VMEM((SC_WINDOW, SLABS, LANE), F32),
                       pltpu.SemaphoreType.DMA],
        name="sc_row_gather",
    )
    def gather_kernel(table_hbm, idx_hbm, out_hbm, idx_v, rows_v, sem):
        worker = lax.axis_index("s") * info.num_cores + lax.axis_index("c")
        base = worker * per_worker

        @pl.loop(0, per_worker // SC_WINDOW)
        def _(it):
            off = base + it * SC_WINDOW
            pltpu.sync_copy(idx_hbm.at[pl.ds(off, SC_WINDOW)], idx_v)
            pltpu.async_copy(table_hbm.at[idx_v], rows_v, sem).wait()
            pltpu.sync_copy(rows_v, out_hbm.at[pl.ds(off, SC_WINDOW)])

    return gather_kernel(table, idx)


def _moe_rows_kernel(te_ref, nv_ref, x_ref, wg_ref, wu_ref, wd_ref, o_ref, wgb_ref, wub_ref, wdb_ref):
    i = pl.program_id(0)
    e = te_ref[i]
    e_prev = te_ref[jnp.maximum(i - 1, 0)]

    @pl.when((i == 0) | (e != e_prev))
    def _():
        wgb_ref[...] = wg_ref[0].astype(BF16)
        wub_ref[...] = wu_ref[0].astype(BF16)
        wdb_ref[...] = wd_ref[0].astype(BF16)

    @pl.when(i < nv_ref[0])
    def _():
        xb = jnp.concatenate(_load_token_slabs(x_ref, MOE_TILE), axis=1).astype(BF16)
        hg = _dot(xb, wgb_ref[...])
        hu = _dot(xb, wub_ref[...])
        mid = (hg * jax.nn.sigmoid(hg) * hu).astype(BF16)
        _store_token_tiles(o_ref, _dot(mid, wdb_ref[...]))

    @pl.when(i >= nv_ref[0])
    def _():
        o_ref[...] = jnp.zeros_like(o_ref)


def _moe_experts_rows(xbuf, tile_expert, n_valid, w_gate, w_up, w_down):
    D = D_MODEL
    n_tiles = xbuf.shape[0] // (MOE_TILE * SLABS)
    tiles = pl.BlockSpec((MOE_TILE * SLABS, LANE), lambda i, te, nv: (jnp.minimum(i, nv[0] - 1), 0))
    grid_spec = pltpu.PrefetchScalarGridSpec(
        num_scalar_prefetch=2,
        grid=(n_tiles,),
        in_specs=[tiles,
                  pl.BlockSpec((1, D, D_EXPERT), lambda i, te, nv: (te[i], 0, 0)),
                  pl.BlockSpec((1, D, D_EXPERT), lambda i, te, nv: (te[i], 0, 0)),
                  pl.BlockSpec((1, D_EXPERT, D), lambda i, te, nv: (te[i], 0, 0))],
        out_specs=pl.BlockSpec((MOE_TILE * SLABS, LANE), lambda i, te, nv: (i, 0)),
        scratch_shapes=[pltpu.VMEM((D, D_EXPERT), BF16), pltpu.VMEM((D, D_EXPERT), BF16),
                        pltpu.VMEM((D_EXPERT, D), BF16)],
    )
    return pl.pallas_call(
        _moe_rows_kernel,
        out_shape=jax.ShapeDtypeStruct((n_tiles * MOE_TILE * SLABS, LANE), F32),
        grid_spec=grid_spec,
        compiler_params=_cparams(("arbitrary",)),
        name="moe_experts",
    )(tile_expert, n_valid, xbuf, w_gate, w_up, w_down)


def _row_copy(src_hbm, src_row, dst_ref, dst_row, sem):
    return pltpu.make_async_copy(src_hbm.at[pl.ds(src_row * SLABS, SLABS), :],
                                 dst_ref.at[pl.ds(dst_row * SLABS, SLABS), :], sem)


def _wait_tiles(src_hbm, dst_ref, sem):
    pltpu.make_async_copy(src_hbm.at[pl.ds(0, dst_ref.shape[0]), :], dst_ref, sem).wait()


def _moe_kernel(te_ref, nv_ref, ra_cur_ref, ra_nxt_ref, h_hbm, wg_ref, wu_ref, wd_ref, o_ref,
                xbuf, sem, wgb_ref, wub_ref, wdb_ref):
    i = pl.program_id(0)
    n_valid = nv_ref[0]
    slot = lax.rem(i, 2)

    def start_gather(ra_ref, s):
        def body(jj, carry):
            for u in range(DMA_UNROLL):
                j = jj * DMA_UNROLL + u
                tok = lax.shift_right_logical(ra_ref[j], 1)
                _row_copy(h_hbm, tok, xbuf.at[s], j, sem.at[s]).start(priority=u % 2)
            return carry
        lax.fori_loop(0, MOE_TILE // DMA_UNROLL, body, 0)

    @pl.when(i == 0)
    def _():
        start_gather(ra_cur_ref, 0)

    @pl.when(i + 1 < n_valid)
    def _():
        start_gather(ra_nxt_ref, 1 - slot)

    e = te_ref[i]
    e_prev = te_ref[jnp.maximum(i - 1, 0)]

    @pl.when((i == 0) | (e != e_prev))
    def _():
        wgb_ref[...] = wg_ref[0].astype(BF16)
        wub_ref[...] = wu_ref[0].astype(BF16)
        wdb_ref[...] = wd_ref[0].astype(BF16)

    @pl.when(i < n_valid)
    def _():
        _wait_tiles(h_hbm, xbuf.at[slot], sem.at[slot])
        xb = jnp.concatenate(_load_token_slabs(xbuf.at[slot], MOE_TILE), axis=1).astype(BF16)
        hg = _dot(xb, wgb_ref[...])
        hu = _dot(xb, wub_ref[...])
        mid = (hg * jax.nn.sigmoid(hg) * hu).astype(BF16)
        _store_token_tiles(o_ref, _dot(mid, wdb_ref[...]))

    @pl.when(i >= n_valid)
    def _():
        o_ref[...] = jnp.zeros_like(o_ref)


def _moe_experts(h1, row_asg, tile_expert, n_valid, w_gate, w_up, w_down):
    D = D_MODEL
    n_tiles = row_asg.shape[0] // MOE_TILE
    smem_tile = lambda f: pl.BlockSpec((MOE_TILE,), f, memory_space=pltpu.SMEM)
    grid_spec = pltpu.PrefetchScalarGridSpec(
        num_scalar_prefetch=2,
        grid=(n_tiles,),
        in_specs=[smem_tile(lambda i, te, nv: (i,)),
                  smem_tile(lambda i, te, nv: (jnp.minimum(i + 1, n_tiles - 1),)),
                  pl.BlockSpec(memory_space=pl.ANY),
                  pl.BlockSpec((1, D, D_EXPERT), lambda i, te, nv: (te[i], 0, 0)),
                  pl.BlockSpec((1, D, D_EXPERT), lambda i, te, nv: (te[i], 0, 0)),
                  pl.BlockSpec((1, D_EXPERT, D), lambda i, te, nv: (te[i], 0, 0))],
        out_specs=pl.BlockSpec((MOE_TILE * SLABS, LANE), lambda i, te, nv: (i, 0)),
        scratch_shapes=[pltpu.VMEM((2, MOE_TILE * SLABS, LANE), F32), pltpu.SemaphoreType.DMA((2,)),
                        pltpu.VMEM((D, D_EXPERT), BF16), pltpu.VMEM((D, D_EXPERT), BF16),
                        pltpu.VMEM((D_EXPERT, D), BF16)],
    )
    return pl.pallas_call(
        _moe_kernel,
        out_shape=jax.ShapeDtypeStruct((n_tiles * MOE_TILE * SLABS, LANE), F32),
        grid_spec=grid_spec,
        compiler_params=_cparams(("arbitrary",)),
        name="moe_experts",
    )(tile_expert, n_valid, row_asg, row_asg, h1, w_gate, w_up, w_down)


def _combine_kernel(d_cur_ref, d_nxt_ref, h_ref, gate_ref, g_ref, b_ref, y_hbm, o_ref, ybuf, sem):
    i = pl.program_id(0)
    n = pl.num_programs(0)
    slot = lax.rem(i, 2)

    def start_gather(d_ref, s):
        def body(tt, carry):
            for u in range(DMA_UNROLL // TOP_K):
                t = tt * (DMA_UNROLL // TOP_K) + u
                for k in range(TOP_K):
                    _row_copy(y_hbm, d_ref[TOP_K * t + k], ybuf.at[s, k], t, sem.at[s]).start(priority=k % 2)
            return carry
        lax.fori_loop(0, COMBINE_TILE * TOP_K // DMA_UNROLL, body, 0)

    @pl.when(i == 0)
    def _():
        start_gather(d_cur_ref, 0)

    @pl.when(i + 1 < n)
    def _():
        start_gather(d_nxt_ref, 1 - slot)

    for k in range(TOP_K):
        _wait_tiles(y_hbm, ybuf.at[slot, k], sem.at[slot])

    gate = gate_ref[...]
    tm = COMBINE_TILE
    ga = jnp.broadcast_to(gate[:, 0:1], (tm, LANE))
    gb = jnp.broadcast_to(gate[:, 1:2], (tm, LANE))
    hs = _load_token_slabs(h_ref, tm)
    ya = _load_token_slabs(ybuf.at[slot, 0], tm)
    yb = _load_token_slabs(ybuf.at[slot, 1], tm)
    z = [DEEPNORM_ALPHA * h + (ga * a + gb * b) for h, a, b in zip(hs, ya, yb)]
    inv_d = 1.0 / D_MODEL
    mu = sum(jnp.sum(t, axis=1, keepdims=True) for t in z) * inv_d
    zc = [t - mu for t in z]
    var = sum(jnp.sum(t * t, axis=1, keepdims=True) for t in zc) * inv_d
    rstd = lax.rsqrt(var + LN_EPS)
    for s in range(SLABS):
        cols = slice(s * LANE, (s + 1) * LANE)
        o_ref[:, cols] = zc[s] * rstd * g_ref[:, cols] + b_ref[:, cols]


def _combine(h1, ybuf, dest, gates, ln2_g, ln2_b):
    D = D_MODEL
    M = h1.shape[0] // SLABS
    tm = COMBINE_TILE
    n = M // tm
    smem_tile = lambda f: pl.BlockSpec((TOP_K * tm,), f, memory_space=pltpu.SMEM)
    return pl.pallas_call(
        _combine_kernel,
        out_shape=jax.ShapeDtypeStruct((M, D), F32),
        grid=(n,),
        in_specs=[smem_tile(lambda i: (i,)), smem_tile(lambda i: (jnp.minimum(i + 1, n - 1),)),
                  pl.BlockSpec((tm * SLABS, LANE), lambda i: (i, 0)), pl.BlockSpec((tm, TOP_K), lambda i: (i, 0)),
                  _full(ln2_g), _full(ln2_b), pl.BlockSpec(memory_space=pl.ANY)],
        out_specs=pl.BlockSpec((tm, D), lambda i: (i, 0)),
        scratch_shapes=[pltpu.VMEM((2, TOP_K, tm * SLABS, LANE), F32), pltpu.SemaphoreType.DMA((2,))],
        compiler_params=_cparams(("arbitrary",)),
        name="combine",
    )(dest, dest, h1, gates, ln2_g, ln2_b, ybuf)


def _routing_plan(route):
    M = route.shape[0]
    eid = route[:, :TOP_K].astype(jnp.int32).reshape(-1)
    gates = route[:, TOP_K:2 * TOP_K]
    A = M * TOP_K
    onehot = (eid[:, None] == jnp.arange(N_EXPERTS, dtype=eid.dtype)[None, :]).astype(jnp.int32)
    csum = jnp.cumsum(onehot, axis=0)
    rank = jnp.sum(csum * onehot, axis=1) - 1
    counts = csum[-1]
    pcounts = (counts + MOE_TILE - 1) // MOE_TILE * MOE_TILE
    pends = jnp.cumsum(pcounts)
    pstarts = pends - pcounts
    dest = (jnp.sum(onehot * pstarts[None, :], axis=1) + rank).astype(jnp.int32)
    n_tiles = (A + N_EXPERTS * (MOE_TILE - 1) + MOE_TILE - 1) // MOE_TILE
    n_valid = (pends[-1] // MOE_TILE).astype(jnp.int32)
    tile_start = jnp.minimum(jnp.arange(n_tiles, dtype=jnp.int32) * MOE_TILE, pends[-1] - 1)
    tile_expert = jnp.sum((pends[None, :] <= tile_start[:, None]).astype(jnp.int32), axis=1)
    tile_expert = jnp.minimum(tile_expert, N_EXPERTS - 1).astype(jnp.int32)
    pad_lo = (pstarts + counts).astype(jnp.int32)
    pad_hi = pends.astype(jnp.int32)
    return gates, dest, n_tiles * MOE_TILE, tile_expert, n_valid.reshape(1), pad_lo, pad_hi


def kernel(x, meta, ln0_g, ln0_b, w_in, mu_shift, w0, w_decay_up, a0, w_a_up, w_g_up, k_k, k_a, r_k, gn_g, gn_b, conv_w, conv_b, w_rg, b_rg, w_ig, b_ig, lru_lambda, w_out, ln1_g, ln1_b, w_router_grp, b_router_grp, w_router_exp, b_router_exp, w_exp_gate, w_exp_up, w_exp_down, ln2_g, ln2_b):
    B, T, D = x.shape
    assert D == D_MODEL and T % 512 == 0 and w_in.shape[0] == 1
    assert (B * T * TOP_K) % INVERT_BLOCK == 0
    row = lambda p: p.reshape(1, -1).astype(F32)
    n_rw = 3 * RWKV_W
    w_in0 = w_in[0]

    def slots(p):
        pad = lambda a, n: jnp.pad(a, [(0, 0)] * (a.ndim - 1) + [(0, n - a.shape[-1])])
        zw = p[..., n_rw:n_rw + DECAY_RANK]
        za = p[..., n_rw + DECAY_RANK:n_rw + DECAY_RANK + AAA_RANK]
        zg = p[..., n_rw + DECAY_RANK + AAA_RANK:n_rw + DECAY_RANK + AAA_RANK + GATE_RANK]
        return jnp.concatenate([p[..., :n_rw], pad(zw, LANE), pad(za, LANE), pad(zg, ZG_SLOT)], axis=-1)

    rwkv_cols = n_rw + DECAY_RANK + AAA_RANK + GATE_RANK
    w_r = slots(w_in0[:, :rwkv_cols]).astype(BF16)
    w_l = w_in0[:, rwkv_cols:].astype(BF16)
    ur, ul, ur_t, ul_t = _in_projection(x, meta, row(ln0_g), row(ln0_b), w_r, w_l)

    pad_rows = lambda a, n: jnp.pad(a, ((0, n - a.shape[0]), (0, 0)))
    rwkv_params = (slots(mu_shift[0][None, :]).astype(F32), row(w0[0]), pad_rows(w_decay_up[0], LANE).astype(BF16),
                   row(a0[0]), pad_rows(w_a_up[0], LANE).astype(BF16), pad_rows(w_g_up[0], ZG_SLOT).astype(BF16),
                   row(k_k[0]), row(k_a[0]), row(r_k[0]), row(gn_g[0]), row(gn_b[0]))
    y_rwkv = _rwkv_pipe_mixer(ur, ur_t, rwkv_params)

    blockdiag = lambda w: jax.scipy.linalg.block_diag(*[w[i] for i in range(LRU_BLOCKS)]).astype(BF16)
    lru_params = (conv_w[0], row(conv_b[0]), blockdiag(w_rg[0]), row(b_rg[0]), blockdiag(w_ig[0]), row(b_ig[0]),
                  row(lru_lambda[0]))
    y_lru = _lru_mixer(ul, ul_t, lru_params)

    w_rt = jnp.concatenate([w_router_grp[0], w_router_exp[0]], axis=1)
    w_rt = jnp.pad(w_rt, ((0, 0), (0, LANE - w_rt.shape[1])))
    wrt_hi = w_rt.astype(BF16)
    wrt_lo = (w_rt - wrt_hi.astype(F32)).astype(BF16)
    b_rt = jnp.concatenate([b_router_grp[0], b_router_exp[0]])
    b_rt = jnp.pad(b_rt, (0, LANE - b_rt.shape[0])).reshape(1, LANE)
    wo = w_out[0].astype(BF16)
    h1, route = _out_projection(x, y_rwkv, y_lru, row(ln0_g), row(ln0_b), wo[:RWKV_W], wo[RWKV_W:],
                                row(ln1_g[0]), row(ln1_b[0]), wrt_hi, wrt_lo, b_rt)

    M = B * T
    h1 = h1.reshape(M * SLABS, LANE)
    gates, dest, n_slots, tile_expert, n_valid, pad_lo, pad_hi = _routing_plan(route.reshape(M, LANE))
    row_asg = _invert_slots(dest, n_slots, pad_lo, pad_hi)
    src_tok = lax.shift_right_logical(row_asg, 1)
    xbuf = _sc_gather(h1.reshape(M, SLABS, LANE), src_tok).reshape(-1, LANE)
    ybuf = _moe_experts_rows(xbuf, tile_expert, n_valid, w_exp_gate[0], w_exp_up[0], w_exp_down[0])
    out = _combine(h1, ybuf, dest, gates, row(ln2_g[0]), row(ln2_b[0]))
    return out.reshape(B, T, D)
```

```python
import functools
import math

import jax
import jax.numpy as jnp
from jax import lax
from jax.experimental import pallas as pl
from jax.experimental.pallas import tpu as pltpu
from jax.experimental.pallas import tpu_sc as plsc

F32 = jnp.float32
BF16 = jnp.bfloat16

D_MODEL = 1024
N_META = 16
RWKV_W = 512
RWKV_HEAD = 64
DECAY_RANK = 64
AAA_RANK = 64
GATE_RANK = 160
LRU_W = 512
LRU_BLOCKS = 8
CONV_WIDTH = 4
LRU_C = 8.0
N_GROUPS = 4
EXPERTS_PER_GROUP = 8
N_EXPERTS = N_GROUPS * EXPERTS_PER_GROUP
TOP_K = 2
D_EXPERT = 512
LN_EPS = 1e-5
GN_EPS = 64e-5
DEEPNORM_ALPHA = 2.0 ** 0.25

LANE = 128
OFF_R, OFF_K, OFF_V = 0, RWKV_W, 2 * RWKV_W
OFF_ZW = 3 * RWKV_W
OFF_ZA = OFF_ZW + LANE
OFF_ZG = OFF_ZA + LANE
ZG_SLOT = 2 * LANE
UR_W = OFF_ZG + ZG_SLOT
UL_W = 2 * LRU_W

TAIL = 256
CHUNK = 64
HEADS_PER_GROUP = 4
GW = HEADS_PER_GROUP * RWKV_HEAD
N_HGROUPS = RWKV_W // GW
LRU_TILE = TAIL
MOE_TILE = 256
COMBINE_TILE = 256
INVERT_BLOCK = 4096
DMA_UNROLL = 8
V7X_VMEM_BYTES = 64 * 1024 * 1024
VMEM_LIMIT = V7X_VMEM_BYTES - 8 * 1024 * 1024


def _cparams(sem):
    return pltpu.CompilerParams(dimension_semantics=sem, vmem_limit_bytes=VMEM_LIMIT)


def _layer_norm(x, g, b):
    mu = jnp.mean(x, -1, keepdims=True)
    xc = x - mu
    var = jnp.mean(xc * xc, -1, keepdims=True)
    return xc * lax.rsqrt(var + LN_EPS) * g + b


def _dot(a, b):
    return jnp.dot(a, b, preferred_element_type=F32)


def _dot_nt(a, b):
    return lax.dot_general(a, b, (((1,), (1,)), ((), ())), preferred_element_type=F32)


def _dot_tn(a, b):
    return lax.dot_general(a, b, (((0,), (0,)), ((), ())), preferred_element_type=F32)


def _full(a):
    return pl.BlockSpec(a.shape, lambda *_: (0,) * a.ndim)


def _inproj_kernel(x_ref, g_ref, b_ref, wr_ref, wl_ref, ur_ref, ul_ref):
    h = _layer_norm(x_ref[0], g_ref[...], b_ref[...]).astype(BF16)
    ur_ref[0] = _dot(h, wr_ref[...])
    ul_ref[0] = _dot(h, wl_ref[...])


def _inproj_tail_kernel(x_ref, g_ref, b_ref, wr_ref, wl_ref, ur_ref, ul_ref):
    h = _layer_norm(x_ref[...], g_ref[...], b_ref[...]).astype(BF16)
    rows = lax.broadcasted_iota(jnp.int32, (TAIL, 1), 0)
    valid = (rows >= TAIL - N_META).astype(F32)
    ur_ref[...] = _dot(h, wr_ref[...]) * valid
    ul_ref[...] = _dot(h, wl_ref[...]) * valid


def _in_projection(x, meta, ln0_g, ln0_b, w_r, w_l):
    B, T, D = x.shape
    tm = 512
    ur, ul = pl.pallas_call(
        _inproj_kernel,
        out_shape=(jax.ShapeDtypeStruct((B, T, UR_W), F32), jax.ShapeDtypeStruct((B, T, UL_W), F32)),
        grid=(B, T // tm),
        in_specs=[pl.BlockSpec((1, tm, D), lambda b, i: (b, i, 0)), _full(ln0_g), _full(ln0_b), _full(w_r), _full(w_l)],
        out_specs=(pl.BlockSpec((1, tm, UR_W), lambda b, i: (b, i, 0)),
                   pl.BlockSpec((1, tm, UL_W), lambda b, i: (b, i, 0))),
        compiler_params=_cparams(("parallel", "parallel")),
        name="inproj",
    )(x, ln0_g, ln0_b, w_r, w_l)
    tail_x = jnp.concatenate([jnp.zeros((TAIL - N_META, D), F32), meta.astype(F32)], axis=0)
    ur_t, ul_t = pl.pallas_call(
        _inproj_tail_kernel,
        out_shape=(jax.ShapeDtypeStruct((TAIL, UR_W), F32), jax.ShapeDtypeStruct((TAIL, UL_W), F32)),
        grid=(1,),
        in_specs=[_full(tail_x), _full(ln0_g), _full(ln0_b), _full(w_r), _full(w_l)],
        out_specs=(pl.BlockSpec((TAIL, UR_W), lambda i: (0, 0)), pl.BlockSpec((TAIL, UL_W), lambda i: (0, 0))),
        compiler_params=_cparams(("arbitrary",)),
        name="inproj_tail",
    )(tail_x, ln0_g, ln0_b, w_r, w_l)
    return ur, ul, ur_t, ul_t


def _rwkv_pipe_kernel(u_ref, ut_ref, mu_ref, w0_ref, wdu_ref, a0_ref, wau_ref, wgu_ref, kk_ref, ka_ref, rk_ref,
                      gng_ref, gnb_ref, bones_ref, bm_ref, eye_ref, msl_ref, mil_ref,
                      m8_ref, m16_ref, m32_ref, m64_ref, y_ref,
                      s_ref, prev_ref, y0_s, q_s, mc_s, nc_s, we_s, bonus_s, g_s, yraw_s):
    s_id = pl.program_id(0)
    nb = u_ref.shape[0]
    blk = u_ref.shape[1]
    npc = blk // CHUNK
    nseq = nb * npc
    seq_rows = lambda q: slice(q * CHUNK, (q + 1) * CHUNK)
    per_seq = lambda f: jnp.concatenate([f(q) for q in range(nseq)], axis=0)
    w_slot = lax.rem(s_id, 2)
    r_slot = 1 - w_slot

    @pl.when(s_id == 0)
    def _():
        s_ref[...] = jnp.zeros_like(s_ref)
        prev_ref[...] = jnp.zeros_like(prev_ref)
        for ref in (y0_s, q_s, mc_s, nc_s, we_s, bonus_s, g_s):
            ref[1] = jnp.zeros(ref.shape[1:], ref.dtype)

    b16 = lambda t: t.astype(BF16)
    bones = bones_ref[...]
    head_sum = lambda t: _dot(b16(t), bones)
    bm = bm_ref[...]
    bm16 = b16(bm)
    tile4 = lambda t: jnp.concatenate([t] * HEADS_PER_GROUP, axis=0)
    fold4 = lambda t: sum(t[i * CHUNK:(i + 1) * CHUNK] for i in range(HEADS_PER_GROUP))
    bd = lambda t: tile4(b16(t)) * bm16

    chains = [(b, hg) for b in range(nb) for hg in range(N_HGROUPS)]
    states = {c: s_ref[c[0], c[1]] for c in chains}

    def recurrent_chunk(j):
        for b, hg in chains:
            sl = slice(hg * GW, (hg + 1) * GW)
            q = b * npc + j
            rq = seq_rows(q)
            s = states[(b, hg)]
            yraw_s[rq, sl] = y0_s[r_slot, rq, sl] + _dot_nt(q_s[r_slot, rq, sl], bd(s))
            states[(b, hg)] = (s * we_s[r_slot, q * 8:q * 8 + 1, sl] + _dot(b16(s), bd(mc_s[r_slot, rq, sl]))
                               + nc_s[r_slot, rq, sl])

    prepared = {}

    def prepare(b):
        lo = b * blk
        u = jnp.where(s_id == 0, ut_ref[...], u_ref[b])
        row = lax.broadcasted_iota(jnp.int32, u.shape, 0)
        u_prev = jnp.where(row == 0, prev_ref[b:b + 1, :], pltpu.roll(u, 1, 0))
        prev_ref[b:b + 1, :] = u[blk - 1:blk, :]
        x = u + (u_prev - u) * mu_ref[...]
        r = x[:, OFF_R:OFF_R + RWKV_W]
        k = x[:, OFF_K:OFF_K + RWKV_W]
        v = x[:, OFF_V:OFF_V + RWKV_W]
        zw = x[:, OFF_ZW:OFF_ZW + LANE]
        za = x[:, OFF_ZA:OFF_ZA + LANE]
        zg = x[:, OFF_ZG:OFF_ZG + ZG_SLOT]
        yield
        z = w0_ref[...] + _dot(b16(jnp.tanh(zw)), wdu_ref[...])
        logw = -math.exp(-0.5) * jax.nn.sigmoid(z)
        a = jax.nn.sigmoid(a0_ref[...] + _dot(b16(za), wau_ref[...]))
        g = _dot(b16(jax.nn.sigmoid(zg)), wgu_ref[...])
        kk = k * kk_ref[...]
        kk = kk / jnp.maximum(jnp.sqrt(head_sum(kk * kk)), 1e-12)
        k = k * (1.0 + (a - 1.0) * ka_ref[...])
        kka = kk * a
        bonus_s[w_slot, lo:lo + blk, :] = head_sum(r * k * rk_ref[...]) * v
        g_s[w_slot, lo:lo + blk, :] = g
        yield
        cl = logw
        row_in_chunk = jnp.bitwise_and(lax.broadcasted_iota(jnp.int32, cl.shape, 0), CHUNK - 1)
        d = 1
        while d < CHUNK:
            cl = cl + jnp.where(row_in_chunk >= d, pltpu.roll(cl, d, 0), 0.0)
            d *= 2
        yield
        cl_last = jnp.concatenate(
            [jnp.broadcast_to(cl[(j + 1) * CHUNK - 1:(j + 1) * CHUNK, :], (CHUNK, RWKV_W)) for j in range(npc)], axis=0)
        e_neg = jnp.exp(-cl)
        e_end = jnp.exp(cl_last - cl)
        w_end = jnp.exp(cl_last)
        for j in range(npc):
            q = b * npc + j
            we_s[w_slot, q * 8:(q + 1) * 8, :] = w_end[j * CHUNK:j * CHUNK + 8, :]
        prepared[b] = dict(rt=r * jnp.exp(cl), kt=k * e_neg, at=-kk * jnp.exp(cl - logw), bt=kka * e_neg,
                           kw=k * e_end, bw=kka * e_end, v=v)
        yield

    eye = eye_ref[...]
    msl = msl_ref[...]
    mil = mil_ref[...]
    rows2 = lambda x, y: jnp.concatenate([x, y], axis=0)
    cols2 = lambda x, y: jnp.concatenate([x, y], axis=1)
    each = lambda f, *ls: [f(*xs) for xs in zip(*ls)]

    def solve(b, p):
        probs = [(slice(j * CHUNK, (j + 1) * CHUNK), slice(hg * GW, (hg + 1) * GW))
                 for j in range(npc) for hg in range(N_HGROUPS)]
        pick = lambda t: [t[rq, sl] for rq, sl in probs]
        at_w, rt_w, v_w = pick(p["at"]), pick(p["rt"]), pick(p["v"])
        lhs = each(lambda x, y: b16(rows2(x, y)), at_w, rt_w)
        ab = each(_dot_nt, lhs, each(bd, pick(p["bt"])))
        ak = each(_dot_nt, lhs, each(bd, pick(p["kt"])))
        yield
        a_ab = each(lambda t: t[:CHUNK] * msl, ab)
        a_rb = each(lambda t: b16(t[CHUNK:] * mil), ab)
        a_xk = each(lambda t: b16(rows2(t[:CHUNK] * msl, t[CHUNK:] * mil)), ak)
        a0 = each(lambda t: b16(t * m8_ref[...]), a_ab)
        a2 = each(lambda t: b16(_dot(t, bd(t))), a0)
        yield
        a4 = each(lambda t: b16(_dot(t, bd(t))), a2)
        p1 = each(lambda t: eye + t.astype(F32), a0)
        p1 = each(lambda q, t: q + _dot(b16(q), bd(t)), p1, a2)
        yield
        tt = each(lambda q, t: q + _dot(b16(q), bd(t)), p1, a4)
        yield
        for m_ref in (m16_ref, m32_ref, m64_ref):
            tb = each(b16, tt)
            off = each(lambda t: b16(t * m_ref[...]), a_ab)
            half = each(lambda x, y: b16(_dot(x, bd(y))), tb, off)
            yield
            tt = each(lambda t, x, y: t + _dot(x, bd(y)), tt, half, tb)
            yield
        tb = each(b16, tt)
        xv = each(lambda x, y: _dot(x, bd(y)), a_xk, v_w)
        yield
        u0 = each(lambda x, y: _dot(x, bd(y[:CHUNK])), tb, xv)
        ta = each(lambda x, y: _dot(x, bd(y)), tb, at_w)
        yield
        y0 = each(lambda x, y, z: _dot(x, bd(y)) + z[CHUNK:], a_rb, u0, xv)
        qq = each(lambda x, y, z: x + _dot(y, bd(z)), rt_w, a_rb, ta)
        yield
        left = each(lambda x, y, z: b16(rows2(cols2(x, y), cols2(jnp.zeros_like(z), z))), ta, u0, v_w)
        right = each(lambda x, y: b16(rows2(x, y)), pick(p["bw"]), pick(p["kw"]))
        mn = each(_dot_tn, left, right)
        for i, (rq, sl) in enumerate(probs):
            rows = slice(b * blk + rq.start, b * blk + rq.stop)
            y0_s[w_slot, rows, sl] = y0[i]
            q_s[w_slot, rows, sl] = b16(qq[i])
            mc_s[w_slot, rows, sl] = b16(fold4(mn[i][:GW] * bm))
            nc_s[w_slot, rows, sl] = fold4(mn[i][GW:] * bm)
        yield

    first = prepare(0)
    for j in range(npc):
        recurrent_chunk(j)
        if j * 4 // npc != (j + 1) * 4 // npc or j == npc - 1:
            for _ in range((j + 1) * 4 // npc - j * 4 // npc):
                next(first, None)
    for _ in first:
        pass
    for (b, hg), s in states.items():
        s_ref[b, hg] = s

    y = yraw_s[...]
    inv_n = 1.0 / RWKV_HEAD
    ym = head_sum(y) * inv_n
    yc = y - ym
    yv = head_sum(yc * yc) * inv_n
    yn = yc * lax.rsqrt(yv + GN_EPS) * gng_ref[...] + gnb_ref[...]
    y_ref[...] = ((yn + bonus_s[r_slot]) * g_s[r_slot]).astype(y_ref.dtype).reshape(y_ref.shape)

    for b in range(nb):
        solver = solve(b, prepared[b])
        nxt = prepare(b + 1) if b + 1 < nb else iter(())
        for level, _ in enumerate(solver):
            if level % 4 == 3:
                next(nxt, None)
        for _ in nxt:
            pass


def _rwkv_masks():
    f = lambda m: m.astype(F32)
    i = jnp.arange(GW)[:, None]
    j = jnp.arange(GW)[None, :]
    bm = f((i // RWKV_HEAD) == (j // RWKV_HEAD))
    t = jnp.arange(CHUNK)[:, None]
    s = (jnp.arange(GW) % CHUNK)[None, :]
    same = lambda n: (t // n) == (s // n)
    msl = f(t > s)
    mil = f(t >= s)
    m8 = f(same(8))
    m16 = f(same(16) & ~same(8))
    m32 = f(same(32) & ~same(16))
    m64 = f(~same(32))
    eye = f(t == s)
    hi = jnp.arange(RWKV_W)
    bones = ((hi[:, None] // RWKV_HEAD) == (hi[None, :] // RWKV_HEAD)).astype(BF16)
    return bones, bm, eye, msl, mil, m8, m16, m32, m64


def _rwkv_pipe_mixer(ur, ur_tail, params):
    B, T, _ = ur.shape
    blk = TAIL
    assert T % blk == 0 and blk % CHUNK == 0 and CHUNK == RWKV_HEAD
    n_blocks = T // blk
    rows = B * blk
    consts = _rwkv_masks()
    in_map = lambda s: (0, jnp.clip(s - 1, 0, n_blocks - 1), 0)
    out_map = lambda s: (0, jnp.clip(s - 2, 0, n_blocks - 1), 0)
    slot2 = lambda w, dt: pltpu.VMEM((2, rows, w), dt)
    return pl.pallas_call(
        _rwkv_pipe_kernel,
        out_shape=jax.ShapeDtypeStruct((B, T, RWKV_W), BF16),
        grid=(n_blocks + 2,),
        in_specs=[pl.BlockSpec((B, blk, UR_W), in_map), _full(ur_tail)]
                 + [_full(p) for p in params] + [_full(m) for m in consts],
        out_specs=pl.BlockSpec((B, blk, RWKV_W), out_map),
        scratch_shapes=[pltpu.VMEM((B, N_HGROUPS, CHUNK, GW), F32), pltpu.VMEM((B, UR_W), F32),
                        slot2(RWKV_W, F32), slot2(RWKV_W, BF16), slot2(RWKV_W, BF16), slot2(RWKV_W, F32),
                        pltpu.VMEM((2, 8 * rows // CHUNK, RWKV_W), F32), slot2(RWKV_W, F32), slot2(RWKV_W, F32),
                        pltpu.VMEM((rows, RWKV_W), F32)],
        compiler_params=_cparams(("arbitrary",)),
        name="rwkv7",
    )(ur, ur_tail, *params, *consts)


def _gelu_tanh(x):
    return 0.5 * x * (1.0 + jnp.tanh(math.sqrt(2.0 / math.pi) * (x + 0.044715 * (x * x * x))))


LRU_CARRY = 8


def _lru_kernel(u_ref, ut_ref, cw_ref, cb_ref, wrg_ref, brg_ref, wig_ref, big_ref, lam_ref, y_ref,
                xs_ref, hprev_ref):
    c = pl.program_id(0)
    nb = u_ref.shape[0]
    nrow = nb * LRU_TILE

    @pl.when(c == 0)
    def _():
        xs_ref[...] = jnp.zeros_like(xs_ref)
        hprev_ref[...] = jnp.zeros_like(hprev_ref)

    u_x = u_ref[...].reshape(nrow, UL_W)
    u = jnp.where(c == 0, jnp.concatenate([ut_ref[...]] * nb, axis=0), u_x)
    xl = u[:, :LRU_W]
    gl = u[:, LRU_W:]
    row = jnp.bitwise_and(lax.broadcasted_iota(jnp.int32, (nrow, LRU_W), 0), LRU_TILE - 1)
    in_group = jnp.bitwise_and(row, 7)
    roll_in_group = lambda t, d: pltpu.roll(t.reshape(t.shape[0] // 8, 8, LRU_W), d, 1).reshape(t.shape)
    xl_prev = jnp.concatenate(
        [p for b in range(nb) for p in (xs_ref[b], xl[b * LRU_TILE:(b + 1) * LRU_TILE - 8])], axis=0)
    xc = cb_ref[...] + cw_ref[CONV_WIDTH - 1:CONV_WIDTH, :] * xl
    for d in range(1, CONV_WIDTH):
        tap = jnp.where(in_group >= d, roll_in_group(xl, d), roll_in_group(xl_prev, d))
        xc = xc + cw_ref[CONV_WIDTH - 1 - d:CONV_WIDTH - d, :] * tap
    for b in range(nb):
        xs_ref[b] = xl[(b + 1) * LRU_TILE - 8:(b + 1) * LRU_TILE]

    xcb = xc.astype(BF16)
    gate_r = jax.nn.sigmoid(_dot(xcb, wrg_ref[...]) + brg_ref[...])
    gate_i = jax.nn.sigmoid(_dot(xcb, wig_ref[...]) + big_ref[...])
    lam = lam_ref[...]
    log_sig = -(jnp.maximum(-lam, 0.0) + jnp.log1p(jnp.exp(-jnp.abs(lam))))
    log_a = LRU_C * gate_r * log_sig
    a = jnp.exp(log_a)
    mult = jnp.sqrt(jnp.maximum(1.0 - jnp.exp(2.0 * log_a), 0.0))
    b = mult * gate_i * xc
    b = jnp.where((c == 0) & (row < LRU_TILE - N_META), 0.0, b)

    d = 1
    while d < 8:
        keep = in_group >= d
        a_sh = jnp.where(keep, roll_in_group(a, d), 1.0)
        b_sh = jnp.where(keep, roll_in_group(b, d), 0.0)
        b = a * b_sh + b
        a = a * a_sh
        d *= 2
    groups = []
    for bi in range(nb):
        carry = hprev_ref[bi:bi + 1, :]
        for gi in range(LRU_TILE // 8):
            lo = bi * LRU_TILE + gi * 8
            hg = b[lo:lo + 8] + a[lo:lo + 8] * carry
            carry = hg[7:8, :]
            groups.append(hg)
        hprev_ref[bi:bi + 1, :] = carry
    h = jnp.concatenate(groups, axis=0)
    y_ref[...] = (h * _gelu_tanh(gl)).astype(y_ref.dtype).reshape(y_ref.shape)


def _lru_mixer(ul, ul_tail, params):
    B, T, _ = ul.shape
    assert TAIL == LRU_TILE
    x_map = lambda c: (0, jnp.maximum(c - 1, 0), 0)
    return pl.pallas_call(
        _lru_kernel,
        out_shape=jax.ShapeDtypeStruct((B, T, LRU_W), BF16),
        grid=(T // LRU_TILE + 1,),
        in_specs=[pl.BlockSpec((B, LRU_TILE, UL_W), x_map), _full(ul_tail)] + [_full(p) for p in params],
        out_specs=pl.BlockSpec((B, LRU_TILE, LRU_W), x_map),
        scratch_shapes=[pltpu.VMEM((B, LRU_CARRY, LRU_W), F32), pltpu.VMEM((B, LRU_W), F32)],
        compiler_params=_cparams(("arbitrary",)),
        name="rglru",
    )(ul, ul_tail, *params)


def _route(lg):
    lane = lax.broadcasted_iota(jnp.int32, lg.shape, 1)
    neg = jnp.float32(-jnp.inf)
    rmax = lambda t: jnp.max(t, axis=1, keepdims=True)
    first = lambda hit: jnp.min(jnp.where(hit, lane, LANE), axis=1, keepdims=True)
    is_grp = lane < N_GROUPS
    gl = jnp.where(is_grp, lg, neg)
    gmax = rmax(gl)
    g_sel = first(gl == gmax)
    p_g = 1.0 / jnp.sum(jnp.where(is_grp, jnp.exp(lg - gmax), 0.0), axis=1, keepdims=True)
    ex = lane - N_GROUPS
    in_grp = (ex >= 0) & (ex < N_EXPERTS) & (jnp.right_shift(ex, 3) == g_sel)
    el = jnp.where(in_grp, lg, neg)
    v1 = rmax(el)
    i1 = first(el == v1)
    el2 = jnp.where(lane == i1, neg, el)
    v2 = rmax(el2)
    i2 = first(el2 == v2)
    t = jnp.exp(v2 - v1)
    gate1 = p_g / (1.0 + t)
    gate2 = p_g * t / (1.0 + t)
    e1 = (i1 - N_GROUPS).astype(F32)
    e2 = (i2 - N_GROUPS).astype(F32)
    return jnp.where(lane == 0, e1, jnp.where(lane == 1, e2, jnp.where(lane == 2, gate1, jnp.where(lane == 3, gate2, 0.0))))


SLABS = D_MODEL // LANE


def _store_token_tiles(ref, val):
    n = val.shape[0]
    for s in range(SLABS):
        ref[pl.ds(s, n, stride=SLABS), :] = val[:, s * LANE:(s + 1) * LANE]


def _load_token_slabs(ref, n):
    return [ref[pl.ds(s, n, stride=SLABS), :] for s in range(SLABS)]


def _outproj_kernel(x_ref, yr_ref, yl_ref, g0_ref, b0_ref, wor_ref, wol_ref, g1_ref, b1_ref,
                    wrt_hi_ref, wrt_lo_ref, brt_ref, h1_ref, rt_ref):
    h0 = _layer_norm(x_ref[0], g0_ref[...], b0_ref[...])
    mix = _dot(yr_ref[0], wor_ref[...]) + _dot(yl_ref[0], wol_ref[...])
    h1 = _layer_norm(DEEPNORM_ALPHA * h0 + mix, g1_ref[...], b1_ref[...])
    _store_token_tiles(h1_ref.at[0], h1)
    hi = h1.astype(BF16)
    lo = (h1 - hi.astype(F32)).astype(BF16)
    w_hi = wrt_hi_ref[...]
    lg = _dot(hi, w_hi) + (_dot(hi, wrt_lo_ref[...]) + _dot(lo, w_hi)) + brt_ref[...]
    rt_ref[0] = _route(lg)


def _out_projection(x, y_rwkv, y_lru, ln0_g, ln0_b, wo_r, wo_l, ln1_g, ln1_b, wrt_hi, wrt_lo, brt):
    B, T, D = x.shape
    tm = 512
    rows = lambda w: pl.BlockSpec((1, tm, w), lambda b, i: (b, i, 0))
    return pl.pallas_call(
        _outproj_kernel,
        out_shape=(jax.ShapeDtypeStruct((B, T * SLABS, LANE), F32), jax.ShapeDtypeStruct((B, T, LANE), F32)),
        grid=(B, T // tm),
        in_specs=[rows(D), rows(RWKV_W), rows(LRU_W), _full(ln0_g), _full(ln0_b), _full(wo_r), _full(wo_l),
                  _full(ln1_g), _full(ln1_b), _full(wrt_hi), _full(wrt_lo), _full(brt)],
        out_specs=(pl.BlockSpec((1, tm * SLABS, LANE), lambda b, i: (b, i, 0)), rows(LANE)),
        compiler_params=_cparams(("parallel", "parallel")),
        name="outproj",
    )(x, y_rwkv, y_lru, ln0_g, ln0_b, wo_r, wo_l, ln1_g, ln1_b, wrt_hi, wrt_lo, brt)


def _invert_kernel(pad_lo_ref, pad_hi_ref, dest_ref, out_ref):
    i = pl.program_id(0)

    @pl.when(i == 0)
    def _():
        n_asg = pl.num_programs(0) * INVERT_BLOCK

        def zero(j, carry):
            out_ref[j] = jnp.bitwise_and(j, n_asg - 1)
            return carry
        for e in range(N_EXPERTS):
            lax.fori_loop(pad_lo_ref[e], pad_hi_ref[e], zero, 0)
        lax.fori_loop(pad_hi_ref[N_EXPERTS - 1], out_ref.shape[0], zero, 0)

    base = i * INVERT_BLOCK

    def body(j, carry):
        out_ref[dest_ref[j]] = base + j
        return carry

    lax.fori_loop(0, INVERT_BLOCK, body, 0, unroll=8)


def _invert_slots(dest, n_slots, pad_lo, pad_hi):
    A = dest.shape[0]
    grid_spec = pltpu.PrefetchScalarGridSpec(
        num_scalar_prefetch=2,
        grid=(A // INVERT_BLOCK,),
        in_specs=[pl.BlockSpec((INVERT_BLOCK,), lambda i, lo, hi: (i,), memory_space=pltpu.SMEM)],
        out_specs=pl.BlockSpec(memory_space=pltpu.SMEM),
    )
    return pl.pallas_call(
        _invert_kernel,
        out_shape=jax.ShapeDtypeStruct((n_slots,), jnp.int32),
        grid_spec=grid_spec,
        compiler_params=_cparams(("arbitrary",)),
        name="invert_slots",
    )(pad_lo, pad_hi, dest)


SC_WINDOW = 64


def _sc_gather(table, idx):
    info = plsc.get_sparse_core_info()
    n_workers = info.num_cores * info.num_subcores
    n = idx.shape[0]
    per_worker = n // n_workers
    assert n % (n_workers * SC_WINDOW) == 0
    mesh = plsc.VectorSubcoreMesh(core_axis_name="c", subcore_axis_name="s")

    @functools.partial(
        pl.kernel, mesh=mesh,
        out_type=jax.ShapeDtypeStruct((n, SLABS, LANE), F32),
        scratch_types=[pltpu.VMEM((SC_WINDOW,), jnp.int32), pltpu.VMEM((SC_WINDOW, SLABS, LANE), F32),
                       pltpu.SemaphoreType.DMA],
        name="sc_row_gather",
    )
    def gather_kernel(table_hbm, idx_hbm, out_hbm, idx_v, rows_v, sem):
        worker = lax.axis_index("s") * info.num_cores + lax.axis_index("c")
        base = worker * per_worker

        @pl.loop(0, per_worker // SC_WINDOW)
        def _(it):
            off = base + it * SC_WINDOW
            pltpu.sync_copy(idx_hbm.at[pl.ds(off, SC_WINDOW)], idx_v)
            pltpu.async_copy(table_hbm.at[idx_v], rows_v, sem).wait()
            pltpu.sync_copy(rows_v, out_hbm.at[pl.ds(off, SC_WINDOW)])

    return gather_kernel(table, idx)


def _moe_rows_kernel(te_ref, nv_ref, x_ref, wg_ref, wu_ref, wd_ref, o_ref, wgb_ref, wub_ref, wdb_ref):
    i = pl.program_id(0)
    e = te_ref[i]
    e_prev = te_ref[jnp.maximum(i - 1, 0)]

    @pl.when((i == 0) | (e != e_prev))
    def _():
        wgb_ref[...] = wg_ref[0].astype(BF16)
        wub_ref[...] = wu_ref[0].astype(BF16)
        wdb_ref[...] = wd_ref[0].astype(BF16)

    @pl.when(i < nv_ref[0])
    def _():
        xb = jnp.concatenate(_load_token_slabs(x_ref, MOE_TILE), axis=1).astype(BF16)
        hg = _dot(xb, wgb_ref[...])
        hu = _dot(xb, wub_ref[...])
        mid = (hg * jax.nn.sigmoid(hg) * hu).astype(BF16)
        _store_token_tiles(o_ref, _dot(mid, wdb_ref[...]))

    @pl.when(i >= nv_ref[0])
    def _():
        o_ref[...] = jnp.zeros_like(o_ref)


def _moe_experts_rows(xbuf, tile_expert, n_valid, w_gate, w_up, w_down):
    D = D_MODEL
    n_tiles = xbuf.shape[0] // (MOE_TILE * SLABS)
    tiles = pl.BlockSpec((MOE_TILE * SLABS, LANE), lambda i, te, nv: (jnp.minimum(i, nv[0] - 1), 0))
    grid_spec = pltpu.PrefetchScalarGridSpec(
        num_scalar_prefetch=2,
        grid=(n_tiles,),
        in_specs=[tiles,
                  pl.BlockSpec((1, D, D_EXPERT), lambda i, te, nv: (te[i], 0, 0)),
                  pl.BlockSpec((1, D, D_EXPERT), lambda i, te, nv: (te[i], 0, 0)),
                  pl.BlockSpec((1, D_EXPERT, D), lambda i, te, nv: (te[i], 0, 0))],
        out_specs=pl.BlockSpec((MOE_TILE * SLABS, LANE), lambda i, te, nv: (i, 0)),
        scratch_shapes=[pltpu.VMEM((D, D_EXPERT), BF16), pltpu.VMEM((D, D_EXPERT), BF16),
                        pltpu.VMEM((D_EXPERT, D), BF16)],
    )
    return pl.pallas_call(
        _moe_rows_kernel,
        out_shape=jax.ShapeDtypeStruct((n_tiles * MOE_TILE * SLABS, LANE), F32),
        grid_spec=grid_spec,
        compiler_params=_cparams(("arbitrary",)),
        name="moe_experts",
    )(tile_expert, n_valid, xbuf, w_gate, w_up, w_down)


def _row_copy(src_hbm, src_row, dst_ref, dst_row, sem):
    return pltpu.make_async_copy(src_hbm.at[pl.ds(src_row * SLABS, SLABS), :],
                                 dst_ref.at[pl.ds(dst_row * SLABS, SLABS), :], sem)


def _wait_tiles(src_hbm, dst_ref, sem):
    pltpu.make_async_copy(src_hbm.at[pl.ds(0, dst_ref.shape[0]), :], dst_ref, sem).wait()


def _moe_kernel(te_ref, nv_ref, ra_cur_ref, ra_nxt_ref, h_hbm, wg_ref, wu_ref, wd_ref, o_ref,
                xbuf, sem, wgb_ref, wub_ref, wdb_ref):
    i = pl.program_id(0)
    n_valid = nv_ref[0]
    slot = lax.rem(i, 2)

    def start_gather(ra_ref, s):
        def body(jj, carry):
            for u in range(DMA_UNROLL):
                j = jj * DMA_UNROLL + u
                tok = lax.shift_right_logical(ra_ref[j], 1)
                _row_copy(h_hbm, tok, xbuf.at[s], j, sem.at[s]).start(priority=u % 2)
            return carry
        lax.fori_loop(0, MOE_TILE // DMA_UNROLL, body, 0)

    @pl.when(i == 0)
    def _():
        start_gather(ra_cur_ref, 0)

    @pl.when(i + 1 < n_valid)
    def _():
        start_gather(ra_nxt_ref, 1 - slot)

    e = te_ref[i]
    e_prev = te_ref[jnp.maximum(i - 1, 0)]

    @pl.when((i == 0) | (e != e_prev))
    def _():
        wgb_ref[...] = wg_ref[0].astype(BF16)
        wub_ref[...] = wu_ref[0].astype(BF16)
        wdb_ref[...] = wd_ref[0].astype(BF16)

    @pl.when(i < n_valid)
    def _():
        _wait_tiles(h_hbm, xbuf.at[slot], sem.at[slot])
        xb = jnp.concatenate(_load_token_slabs(xbuf.at[slot], MOE_TILE), axis=1).astype(BF16)
        hg = _dot(xb, wgb_ref[...])
        hu = _dot(xb, wub_ref[...])
        mid = (hg * jax.nn.sigmoid(hg) * hu).astype(BF16)
        _store_token_tiles(o_ref, _dot(mid, wdb_ref[...]))

    @pl.when(i >= n_valid)
    def _():
        o_ref[...] = jnp.zeros_like(o_ref)


def _moe_experts(h1, row_asg, tile_expert, n_valid, w_gate, w_up, w_down):
    D = D_MODEL
    n_tiles = row_asg.shape[0] // MOE_TILE
    smem_tile = lambda f: pl.BlockSpec((MOE_TILE,), f, memory_space=pltpu.SMEM)
    grid_spec = pltpu.PrefetchScalarGridSpec(
        num_scalar_prefetch=2,
        grid=(n_tiles,),
        in_specs=[smem_tile(lambda i, te, nv: (i,)),
                  smem_tile(lambda i, te, nv: (jnp.minimum(i + 1, n_tiles - 1),)),
                  pl.BlockSpec(memory_space=pl.ANY),
                  pl.BlockSpec((1, D, D_EXPERT), lambda i, te, nv: (te[i], 0, 0)),
                  pl.BlockSpec((1, D, D_EXPERT), lambda i, te, nv: (te[i], 0, 0)),
                  pl.BlockSpec((1, D_EXPERT, D), lambda i, te, nv: (te[i], 0, 0))],
        out_specs=pl.BlockSpec((MOE_TILE * SLABS, LANE), lambda i, te, nv: (i, 0)),
        scratch_shapes=[pltpu.VMEM((2, MOE_TILE * SLABS, LANE), F32), pltpu.SemaphoreType.DMA((2,)),
                        pltpu.VMEM((D, D_EXPERT), BF16), pltpu.VMEM((D, D_EXPERT), BF16),
                        pltpu.VMEM((D_EXPERT, D), BF16)],
    )
    return pl.pallas_call(
        _moe_kernel,
        out_shape=jax.ShapeDtypeStruct((n_tiles * MOE_TILE * SLABS, LANE), F32),
        grid_spec=grid_spec,
        compiler_params=_cparams(("arbitrary",)),
        name="moe_experts",
    )(tile_expert, n_valid, row_asg, row_asg, h1, w_gate, w_up, w_down)


def _combine_kernel(d_cur_ref, d_nxt_ref, h_ref, gate_ref, g_ref, b_ref, y_hbm, o_ref, ybuf, sem):
    i = pl.program_id(0)
    n = pl.num_programs(0)
    slot = lax.rem(i, 2)

    def start_gather(d_ref, s):
        def body(tt, carry):
            for u in range(DMA_UNROLL // TOP_K):
                t = tt * (DMA_UNROLL // TOP_K) + u
                for k in range(TOP_K):
                    _row_copy(y_hbm, d_ref[TOP_K * t + k], ybuf.at[s, k], t, sem.at[s]).start(priority=k % 2)
            return carry
        lax.fori_loop(0, COMBINE_TILE * TOP_K // DMA_UNROLL, body, 0)

    @pl.when(i == 0)
    def _():
        start_gather(d_cur_ref, 0)

    @pl.when(i + 1 < n)
    def _():
        start_gather(d_nxt_ref, 1 - slot)

    for k in range(TOP_K):
        _wait_tiles(y_hbm, ybuf.at[slot, k], sem.at[slot])

    gate = gate_ref[...]
    tm = COMBINE_TILE
    ga = jnp.broadcast_to(gate[:, 0:1], (tm, LANE))
    gb = jnp.broadcast_to(gate[:, 1:2], (tm, LANE))
    hs = _load_token_slabs(h_ref, tm)
    ya = _load_token_slabs(ybuf.at[slot, 0], tm)
    yb = _load_token_slabs(ybuf.at[slot, 1], tm)
    z = [DEEPNORM_ALPHA * h + (ga * a + gb * b) for h, a, b in zip(hs, ya, yb)]
    inv_d = 1.0 / D_MODEL
    mu = sum(jnp.sum(t, axis=1, keepdims=True) for t in z) * inv_d
    zc = [t - mu for t in z]
    var = sum(jnp.sum(t * t, axis=1, keepdims=True) for t in zc) * inv_d
    rstd = lax.rsqrt(var + LN_EPS)
    for s in range(SLABS):
        cols = slice(s * LANE, (s + 1) * LANE)
        o_ref[:, cols] = zc[s] * rstd * g_ref[:, cols] + b_ref[:, cols]


def _combine(h1, ybuf, dest, gates, ln2_g, ln2_b):
    D = D_MODEL
    M = h1.shape[0] // SLABS
    tm = COMBINE_TILE
    n = M // tm
    smem_tile = lambda f: pl.BlockSpec((TOP_K * tm,), f, memory_space=pltpu.SMEM)
    return pl.pallas_call(
        _combine_kernel,
        out_shape=jax.ShapeDtypeStruct((M, D), F32),
        grid=(n,),
        in_specs=[smem_tile(lambda i: (i,)), smem_tile(lambda i: (jnp.minimum(i + 1, n - 1),)),
                  pl.BlockSpec((tm * SLABS, LANE), lambda i: (i, 0)), pl.BlockSpec((tm, TOP_K), lambda i: (i, 0)),
                  _full(ln2_g), _full(ln2_b), pl.BlockSpec(memory_space=pl.ANY)],
        out_specs=pl.BlockSpec((tm, D), lambda i: (i, 0)),
        scratch_shapes=[pltpu.VMEM((2, TOP_K, tm * SLABS, LANE), F32), pltpu.SemaphoreType.DMA((2,))],
        compiler_params=_cparams(("arbitrary",)),
        name="combine",
    )(dest, dest, h1, gates, ln2_g, ln2_b, ybuf)


def _routing_plan(route):
    M = route.shape[0]
    eid = route[:, :TOP_K].astype(jnp.int32).reshape(-1)
    gates = route[:, TOP_K:2 * TOP_K]
    A = M * TOP_K
    onehot = (eid[:, None] == jnp.arange(N_EXPERTS, dtype=eid.dtype)[None, :]).astype(jnp.int32)
    csum = jnp.cumsum(onehot, axis=0)
    rank = jnp.sum(csum * onehot, axis=1) - 1
    counts = csum[-1]
    pcounts = (counts + MOE_TILE - 1) // MOE_TILE * MOE_TILE
    pends = jnp.cumsum(pcounts)
    pstarts = pends - pcounts
    dest = (jnp.sum(onehot * pstarts[None, :], axis=1) + rank).astype(jnp.int32)
    n_tiles = (A + N_EXPERTS * (MOE_TILE - 1) + MOE_TILE - 1) // MOE_TILE
    n_valid = (pends[-1] // MOE_TILE).astype(jnp.int32)
    tile_start = jnp.minimum(jnp.arange(n_tiles, dtype=jnp.int32) * MOE_TILE, pends[-1] - 1)
    tile_expert = jnp.sum((pends[None, :] <= tile_start[:, None]).astype(jnp.int32), axis=1)
    tile_expert = jnp.minimum(tile_expert, N_EXPERTS - 1).astype(jnp.int32)
    pad_lo = (pstarts + counts).astype(jnp.int32)
    pad_hi = pends.astype(jnp.int32)
    return gates, dest, n_tiles * MOE_TILE, tile_expert, n_valid.reshape(1), pad_lo, pad_hi


def kernel(x, meta, ln0_g, ln0_b, w_in, mu_shift, w0, w_decay_up, a0, w_a_up, w_g_up, k_k, k_a, r_k, gn_g, gn_b, conv_w, conv_b, w_rg, b_rg, w_ig, b_ig, lru_lambda, w_out, ln1_g, ln1_b, w_router_grp, b_router_grp, w_router_exp, b_router_exp, w_exp_gate, w_exp_up, w_exp_down, ln2_g, ln2_b):
    B, T, D = x.shape
    assert D == D_MODEL and T % 512 == 0 and w_in.shape[0] == 1
    assert (B * T * TOP_K) % INVERT_BLOCK == 0
    row = lambda p: p.reshape(1, -1).astype(F32)
    n_rw = 3 * RWKV_W
    w_in0 = w_in[0]

    def slots(p):
        pad = lambda a, n: jnp.pad(a, [(0, 0)] * (a.ndim - 1) + [(0, n - a.shape[-1])])
        zw = p[..., n_rw:n_rw + DECAY_RANK]
        za = p[..., n_rw + DECAY_RANK:n_rw + DECAY_RANK + AAA_RANK]
        zg = p[..., n_rw + DECAY_RANK + AAA_RANK:n_rw + DECAY_RANK + AAA_RANK + GATE_RANK]
        return jnp.concatenate([p[..., :n_rw], pad(zw, LANE), pad(za, LANE), pad(zg, ZG_SLOT)], axis=-1)

    rwkv_cols = n_rw + DECAY_RANK + AAA_RANK + GATE_RANK
    w_r = slots(w_in0[:, :rwkv_cols]).astype(BF16)
    w_l = w_in0[:, rwkv_cols:].astype(BF16)
    ur, ul, ur_t, ul_t = _in_projection(x, meta, row(ln0_g), row(ln0_b), w_r, w_l)

    pad_rows = lambda a, n: jnp.pad(a, ((0, n - a.shape[0]), (0, 0)))
    rwkv_params = (slots(mu_shift[0][None, :]).astype(F32), row(w0[0]), pad_rows(w_decay_up[0], LANE).astype(BF16),
                   row(a0[0]), pad_rows(w_a_up[0], LANE).astype(BF16), pad_rows(w_g_up[0], ZG_SLOT).astype(BF16),
                   row(k_k[0]), row(k_a[0]), row(r_k[0]), row(gn_g[0]), row(gn_b[0]))
    y_rwkv = _rwkv_pipe_mixer(ur, ur_t, rwkv_params)

    blockdiag = lambda w: jax.scipy.linalg.block_diag(*[w[i] for i in range(LRU_BLOCKS)]).astype(BF16)
    lru_params = (conv_w[0], row(conv_b[0]), blockdiag(w_rg[0]), row(b_rg[0]), blockdiag(w_ig[0]), row(b_ig[0]),
                  row(lru_lambda[0]))
    y_lru = _lru_mixer(ul, ul_t, lru_params)

    w_rt = jnp.concatenate([w_router_grp[0], w_router_exp[0]], axis=1)
    w_rt = jnp.pad(w_rt, ((0, 0), (0, LANE - w_rt.shape[1])))
    wrt_hi = w_rt.astype(BF16)
    wrt_lo = (w_rt - wrt_hi.astype(F32)).astype(BF16)
    b_rt = jnp.concatenate([b_router_grp[0], b_router_exp[0]])
    b_rt = jnp.pad(b_rt, (0, LANE - b_rt.shape[0])).reshape(1, LANE)
    wo = w_out[0].astype(BF16)
    h1, route = _out_projection(x, y_rwkv, y_lru, row(ln0_g), row(ln0_b), wo[:RWKV_W], wo[RWKV_W:],
                                row(ln1_g[0]), row(ln1_b[0]), wrt_hi, wrt_lo, b_rt)

    M = B * T
    h1 = h1.reshape(M * SLABS, LANE)
    gates, dest, n_slots, tile_expert, n_valid, pad_lo, pad_hi = _routing_plan(route.reshape(M, LANE))
    row_asg = _invert_slots(dest, n_slots, pad_lo, pad_hi)
    src_tok = lax.shift_right_logical(row_asg, 1)
    xbuf = _sc_gather(h1.reshape(M, SLABS, LANE), src_tok).reshape(-1, LANE)
    ybuf = _moe_experts_rows(xbuf, tile_expert, n_valid, w_exp_gate[0], w_exp_up[0], w_exp_down[0])
    out = _combine(h1, ybuf, dest, gates, row(ln2_g[0]), row(ln2_b[0]))
    return out.reshape(B, T, D)
```

```python
import functools
import math

import jax
import jax.numpy as jnp
from jax import lax
from jax.experimental import pallas as pl
from jax.experimental.pallas import tpu as pltpu
from jax.experimental.pallas import tpu_sc as plsc

F32 = jnp.float32
BF16 = jnp.bfloat16

D_MODEL = 1024
N_META = 16
RWKV_W = 512
RWKV_HEAD = 64
DECAY_RANK = 64
AAA_RANK = 64
GATE_RANK = 160
LRU_W = 512
LRU_BLOCKS = 8
CONV_WIDTH = 4
LRU_C = 8.0
N_GROUPS = 4
EXPERTS_PER_GROUP = 8
N_EXPERTS = N_GROUPS * EXPERTS_PER_GROUP
TOP_K = 2
D_EXPERT = 512
LN_EPS = 1e-5
GN_EPS = 64e-5
DEEPNORM_ALPHA = 2.0 ** 0.25

LANE = 128
OFF_R, OFF_K, OFF_V = 0, RWKV_W, 2 * RWKV_W
OFF_ZW = 3 * RWKV_W
OFF_ZA = OFF_ZW + LANE
OFF_ZG = OFF_ZA + LANE
ZG_SLOT = 2 * LANE
UR_W = OFF_ZG + ZG_SLOT
UL_W = 2 * LRU_W

TAIL = 256
CHUNK = 64
HEADS_PER_GROUP = 4
GW = HEADS_PER_GROUP * RWKV_HEAD
N_HGROUPS = RWKV_W // GW
LRU_TILE = TAIL
MOE_TILE = 256
COMBINE_TILE = 256
INVERT_BLOCK = 4096
DMA_UNROLL = 8
V7X_VMEM_BYTES = 64 * 1024 * 1024
VMEM_LIMIT = V7X_VMEM_BYTES - 8 * 1024 * 1024


def _cparams(sem):
    return pltpu.CompilerParams(dimension_semantics=sem, vmem_limit_bytes=VMEM_LIMIT)


def _layer_norm(x, g, b):
    mu = jnp.mean(x, -1, keepdims=True)
    xc = x - mu
    var = jnp.mean(xc * xc, -1, keepdims=True)
    return xc * lax.rsqrt(var + LN_EPS) * g + b


def _dot(a, b):
    return jnp.dot(a, b, preferred_element_type=F32)


def _dot_nt(a, b):
    return lax.dot_general(a, b, (((1,), (1,)), ((), ())), preferred_element_type=F32)


def _dot_tn(a, b):
    return lax.dot_general(a, b, (((0,), (0,)), ((), ())), preferred_element_type=F32)


def _full(a):
    return pl.BlockSpec(a.shape, lambda *_: (0,) * a.ndim)


def _inproj_kernel(x_ref, g_ref, b_ref, wr_ref, wl_ref, ur_ref, ul_ref):
    h = _layer_norm(x_ref[0], g_ref[...], b_ref[...]).astype(BF16)
    ur_ref[0] = _dot(h, wr_ref[...])
    ul_ref[0] = _dot(h, wl_ref[...])


def _inproj_tail_kernel(x_ref, g_ref, b_ref, wr_ref, wl_ref, ur_ref, ul_ref):
    h = _layer_norm(x_ref[...], g_ref[...], b_ref[...]).astype(BF16)
    rows = lax.broadcasted_iota(jnp.int32, (TAIL, 1), 0)
    valid = (rows >= TAIL - N_META).astype(F32)
    ur_ref[...] = _dot(h, wr_ref[...]) * valid
    ul_ref[...] = _dot(h, wl_ref[...]) * valid


def _in_projection(x, meta, ln0_g, ln0_b, w_r, w_l):
    B, T, D = x.shape
    tm = 512
    ur, ul = pl.pallas_call(
        _inproj_kernel,
        out_shape=(jax.ShapeDtypeStruct((B, T, UR_W), F32), jax.ShapeDtypeStruct((B, T, UL_W), F32)),
        grid=(B, T // tm),
        in_specs=[pl.BlockSpec((1, tm, D), lambda b, i: (b, i, 0)), _full(ln0_g), _full(ln0_b), _full(w_r), _full(w_l)],
        out_specs=(pl.BlockSpec((1, tm, UR_W), lambda b, i: (b, i, 0)),
                   pl.BlockSpec((1, tm, UL_W), lambda b, i: (b, i, 0))),
        compiler_params=_cparams(("parallel", "parallel")),
        name="inproj",
    )(x, ln0_g, ln0_b, w_r, w_l)
    tail_x = jnp.concatenate([jnp.zeros((TAIL - N_META, D), F32), meta.astype(F32)], axis=0)
    ur_t, ul_t = pl.pallas_call(
        _inproj_tail_kernel,
        out_shape=(jax.ShapeDtypeStruct((TAIL, UR_W), F32), jax.ShapeDtypeStruct((TAIL, UL_W), F32)),
        grid=(1,),
        in_specs=[_full(tail_x), _full(ln0_g), _full(ln0_b), _full(w_r), _full(w_l)],
        out_specs=(pl.BlockSpec((TAIL, UR_W), lambda i: (0, 0)), pl.BlockSpec((TAIL, UL_W), lambda i: (0, 0))),
        compiler_params=_cparams(("arbitrary",)),
        name="inproj_tail",
    )(tail_x, ln0_g, ln0_b, w_r, w_l)
    return ur, ul, ur_t, ul_t


def _rwkv_pipe_kernel(u_ref, ut_ref, mu_ref, w0_ref, wdu_ref, a0_ref, wau_ref, wgu_ref, kk_ref, ka_ref, rk_ref,
                      gng_ref, gnb_ref, bones_ref, bm_ref, eye_ref, msl_ref, mil_ref,
                      m8_ref, m16_ref, m32_ref, m64_ref, y_ref,
                      s_ref, prev_ref, y0_s, q_s, mc_s, nc_s, we_s, bonus_s, g_s, yraw_s):
    s_id = pl.program_id(0)
    nb = u_ref.shape[0]
    blk = u_ref.shape[1]
    npc = blk // CHUNK
    nseq = nb * npc
    seq_rows = lambda q: slice(q * CHUNK, (q + 1) * CHUNK)
    per_seq = lambda f: jnp.concatenate([f(q) for q in range(nseq)], axis=0)
    w_slot = lax.rem(s_id, 2)
    r_slot = 1 - w_slot

    @pl.when(s_id == 0)
    def _():
        s_ref[...] = jnp.zeros_like(s_ref)
        prev_ref[...] = jnp.zeros_like(prev_ref)
        for ref in (y0_s, q_s, mc_s, nc_s, we_s, bonus_s, g_s):
            ref[1] = jnp.zeros(ref.shape[1:], ref.dtype)

    b16 = lambda t: t.astype(BF16)
    bones = bones_ref[...]
    head_sum = lambda t: _dot(b16(t), bones)
    bm = bm_ref[...]
    bm16 = b16(bm)
    tile4 = lambda t: jnp.concatenate([t] * HEADS_PER_GROUP, axis=0)
    fold4 = lambda t: sum(t[i * CHUNK:(i + 1) * CHUNK] for i in range(HEADS_PER_GROUP))
    bd = lambda t: tile4(b16(t)) * bm16

    chains = [(b, hg) for b in range(nb) for hg in range(N_HGROUPS)]
    states = {c: s_ref[c[0], c[1]] for c in chains}

    def recurrent_chunk(j):
        for b, hg in chains:
            sl = slice(hg * GW, (hg + 1) * GW)
            q = b * npc + j
            rq = seq_rows(q)
            s = states[(b, hg)]
            yraw_s[rq, sl] = y0_s[r_slot, rq, sl] + _dot_nt(q_s[r_slot, rq, sl], bd(s))
            states[(b, hg)] = (s * we_s[r_slot, q * 8:q * 8 + 1, sl] + _dot(b16(s), bd(mc_s[r_slot, rq, sl]))
                               + nc_s[r_slot, rq, sl])

    prepared = {}

    def prepare(b):
        lo = b * blk
        u = jnp.where(s_id == 0, ut_ref[...], u_ref[b])
        row = lax.broadcasted_iota(jnp.int32, u.shape, 0)
        u_prev = jnp.where(row == 0, prev_ref[b:b + 1, :], pltpu.roll(u, 1, 0))
        prev_ref[b:b + 1, :] = u[blk - 1:blk, :]
        x = u + (u_prev - u) * mu_ref[...]
        r = x[:, OFF_R:OFF_R + RWKV_W]
        k = x[:, OFF_K:OFF_K + RWKV_W]
        v = x[:, OFF_V:OFF_V + RWKV_W]
        zw = x[:, OFF_ZW:OFF_ZW + LANE]
        za = x[:, OFF_ZA:OFF_ZA + LANE]
        zg = x[:, OFF_ZG:OFF_ZG + ZG_SLOT]
        yield
        z = w0_ref[...] + _dot(b16(jnp.tanh(zw)), wdu_ref[...])
        logw = -math.exp(-0.5) * jax.nn.sigmoid(z)
        a = jax.nn.sigmoid(a0_ref[...] + _dot(b16(za), wau_ref[...]))
        g = _dot(b16(jax.nn.sigmoid(zg)), wgu_ref[...])
        kk = k * kk_ref[...]
        kk = kk / jnp.maximum(jnp.sqrt(head_sum(kk * kk)), 1e-12)
        k = k * (1.0 + (a - 1.0) * ka_ref[...])
        kka = kk * a
        bonus_s[w_slot, lo:lo + blk, :] = head_sum(r * k * rk_ref[...]) * v
        g_s[w_slot, lo:lo + blk, :] = g
        yield
        cl = logw
        row_in_chunk = jnp.bitwise_and(lax.broadcasted_iota(jnp.int32, cl.shape, 0), CHUNK - 1)
        d = 1
        while d < CHUNK:
            cl = cl + jnp.where(row_in_chunk >= d, pltpu.roll(cl, d, 0), 0.0)
            d *= 2
        yield
        cl_last = jnp.concatenate(
            [jnp.broadcast_to(cl[(j + 1) * CHUNK - 1:(j + 1) * CHUNK, :], (CHUNK, RWKV_W)) for j in range(npc)], axis=0)
        e_neg = jnp.exp(-cl)
        e_end = jnp.exp(cl_last - cl)
        w_end = jnp.exp(cl_last)
        for j in range(npc):
            q = b * npc + j
            we_s[w_slot, q * 8:(q + 1) * 8, :] = w_end[j * CHUNK:j * CHUNK + 8, :]
        prepared[b] = dict(rt=r * jnp.exp(cl), kt=k * e_neg, at=-kk * jnp.exp(cl - logw), bt=kka * e_neg,
                           kw=k * e_end, bw=kka * e_end, v=v)
        yield

    eye = eye_ref[...]
    msl = msl_ref[...]
    mil = mil_ref[...]
    rows2 = lambda x, y: jnp.concatenate([x, y], axis=0)
    cols2 = lambda x, y: jnp.concatenate([x, y], axis=1)
    each = lambda f, *ls: [f(*xs) for xs in zip(*ls)]

    def solve(b, p):
        probs = [(slice(j * CHUNK, (j + 1) * CHUNK), slice(hg * GW, (hg + 1) * GW))
                 for j in range(npc) for hg in range(N_HGROUPS)]
        pick = lambda t: [t[rq, sl] for rq, sl in probs]
        at_w, rt_w, v_w = pick(p["at"]), pick(p["rt"]), pick(p["v"])
        lhs = each(lambda x, y: b16(rows2(x, y)), at_w, rt_w)
        ab = each(_dot_nt, lhs, each(bd, pick(p["bt"])))
        ak = each(_dot_nt, lhs, each(bd, pick(p["kt"])))
        yield
        a_ab = each(lambda t: t[:CHUNK] * msl, ab)
        a_rb = each(lambda t: b16(t[CHUNK:] * mil), ab)
        a_xk = each(lambda t: b16(rows2(t[:CHUNK] * msl, t[CHUNK:] * mil)), ak)
        a0 = each(lambda t: b16(t * m8_ref[...]), a_ab)
        a2 = each(lambda t: b16(_dot(t, bd(t))), a0)
        yield
        a4 = each(lambda t: b16(_dot(t, bd(t))), a2)
        p1 = each(lambda t: eye + t.astype(F32), a0)
        p1 = each(lambda q, t: q + _dot(b16(q), bd(t)), p1, a2)
        yield
        tt = each(lambda q, t: q + _dot(b16(q), bd(t)), p1, a4)
        yield
        for m_ref in (m16_ref, m32_ref, m64_ref):
            tb = each(b16, tt)
            off = each(lambda t: b16(t * m_ref[...]), a_ab)
            half = each(lambda x, y: b16(_dot(x, bd(y))), tb, off)
            yield
            tt = each(lambda t, x, y: t + _dot(x, bd(y)), tt, half, tb)
            yield
        tb = each(b16, tt)
        xv = each(lambda x, y: _dot(x, bd(y)), a_xk, v_w)
        yield
        u0 = each(lambda x, y: _dot(x, bd(y[:CHUNK])), tb, xv)
        ta = each(lambda x, y: _dot(x, bd(y)), tb, at_w)
        yield
        y0 = each(lambda x, y, z: _dot(x, bd(y)) + z[CHUNK:], a_rb, u0, xv)
        qq = each(lambda x, y, z: x + _dot(y, bd(z)), rt_w, a_rb, ta)
        yield
        left = each(lambda x, y, z: b16(rows2(cols2(x, y), cols2(jnp.zeros_like(z), z))), ta, u0, v_w)
        right = each(lambda x, y: b16(rows2(x, y)), pick(p["bw"]), pick(p["kw"]))
        mn = each(_dot_tn, left, right)
        for i, (rq, sl) in enumerate(probs):
            rows = slice(b * blk + rq.start, b * blk + rq.stop)
            y0_s[w_slot, rows, sl] = y0[i]
            q_s[w_slot, rows, sl] = b16(qq[i])
            mc_s[w_slot, rows, sl] = b16(fold4(mn[i][:GW] * bm))
            nc_s[w_slot, rows, sl] = fold4(mn[i][GW:] * bm)
        yield

    first = prepare(0)
    for j in range(npc):
        recurrent_chunk(j)
        if j * 4 // npc != (j + 1) * 4 // npc or j == npc - 1:
            for _ in range((j + 1) * 4 // npc - j * 4 // npc):
                next(first, None)
    for _ in first:
        pass
    for (b, hg), s in states.items():
        s_ref[b, hg] = s

    y = yraw_s[...]
    inv_n = 1.0 / RWKV_HEAD
    ym = head_sum(y) * inv_n
    yc = y - ym
    yv = head_sum(yc * yc) * inv_n
    yn = yc * lax.rsqrt(yv + GN_EPS) * gng_ref[...] + gnb_ref[...]
    y_ref[...] = ((yn + bonus_s[r_slot]) * g_s[r_slot]).astype(y_ref.dtype).reshape(y_ref.shape)

    for b in range(nb):
        solver = solve(b, prepared[b])
        nxt = prepare(b + 1) if b + 1 < nb else iter(())
        for level, _ in enumerate(solver):
            if level % 4 == 3:
                next(nxt, None)
        for _ in nxt:
            pass


def _rwkv_masks():
    f = lambda m: m.astype(F32)
    i = jnp.arange(GW)[:, None]
    j = jnp.arange(GW)[None, :]
    bm = f((i // RWKV_HEAD) == (j // RWKV_HEAD))
    t = jnp.arange(CHUNK)[:, None]
    s = (jnp.arange(GW) % CHUNK)[None, :]
    same = lambda n: (t // n) == (s // n)
    msl = f(t > s)
    mil = f(t >= s)
    m8 = f(same(8))
    m16 = f(same(16) & ~same(8))
    m32 = f(same(32) & ~same(16))
    m64 = f(~same(32))
    eye = f(t == s)
    hi = jnp.arange(RWKV_W)
    bones = ((hi[:, None] // RWKV_HEAD) == (hi[None, :] // RWKV_HEAD)).astype(BF16)
    return bones, bm, eye, msl, mil, m8, m16, m32, m64


def _rwkv_pipe_mixer(ur, ur_tail, params):
    B, T, _ = ur.shape
    blk = TAIL
    assert T % blk == 0 and blk % CHUNK == 0 and CHUNK == RWKV_HEAD
    n_blocks = T // blk
    rows = B * blk
    consts = _rwkv_masks()
    in_map = lambda s: (0, jnp.clip(s - 1, 0, n_blocks - 1), 0)
    out_map = lambda s: (0, jnp.clip(s - 2, 0, n_blocks - 1), 0)
    slot2 = lambda w, dt: pltpu.VMEM((2, rows, w), dt)
    return pl.pallas_call(
        _rwkv_pipe_kernel,
        out_shape=jax.ShapeDtypeStruct((B, T, RWKV_W), BF16),
        grid=(n_blocks + 2,),
        in_specs=[pl.BlockSpec((B, blk, UR_W), in_map), _full(ur_tail)]
                 + [_full(p) for p in params] + [_full(m) for m in consts],
        out_specs=pl.BlockSpec((B, blk, RWKV_W), out_map),
        scratch_shapes=[pltpu.VMEM((B, N_HGROUPS, CHUNK, GW), F32), pltpu.VMEM((B, UR_W), F32),
                        slot2(RWKV_W, F32), slot2(RWKV_W, BF16), slot2(RWKV_W, BF16), slot2(RWKV_W, F32),
                        pltpu.VMEM((2, 8 * rows // CHUNK, RWKV_W), F32), slot2(RWKV_W, F32), slot2(RWKV_W, F32),
                        pltpu.VMEM((rows, RWKV_W), F32)],
        compiler_params=_cparams(("arbitrary",)),
        name="rwkv7",
    )(ur, ur_tail, *params, *consts)


def _gelu_tanh(x):
    return 0.5 * x * (1.0 + jnp.tanh(math.sqrt(2.0 / math.pi) * (x + 0.044715 * (x * x * x))))


LRU_CARRY = 8


def _lru_kernel(u_ref, ut_ref, cw_ref, cb_ref, wrg_ref, brg_ref, wig_ref, big_ref, lam_ref, y_ref,
                xs_ref, hprev_ref):
    c = pl.program_id(0)
    nb = u_ref.shape[0]
    nrow = nb * LRU_TILE

    @pl.when(c == 0)
    def _():
        xs_ref[...] = jnp.zeros_like(xs_ref)
        hprev_ref[...] = jnp.zeros_like(hprev_ref)

    u_x = u_ref[...].reshape(nrow, UL_W)
    u = jnp.where(c == 0, jnp.concatenate([ut_ref[...]] * nb, axis=0), u_x)
    xl = u[:, :LRU_W]
    gl = u[:, LRU_W:]
    row = jnp.bitwise_and(lax.broadcasted_iota(jnp.int32, (nrow, LRU_W), 0), LRU_TILE - 1)
    in_group = jnp.bitwise_and(row, 7)
    roll_in_group = lambda t, d: pltpu.roll(t.reshape(t.shape[0] // 8, 8, LRU_W), d, 1).reshape(t.shape)
    xl_prev = jnp.concatenate(
        [p for b in range(nb) for p in (xs_ref[b], xl[b * LRU_TILE:(b + 1) * LRU_TILE - 8])], axis=0)
    xc = cb_ref[...] + cw_ref[CONV_WIDTH - 1:CONV_WIDTH, :] * xl
    for d in range(1, CONV_WIDTH):
        tap = jnp.where(in_group >= d, roll_in_group(xl, d), roll_in_group(xl_prev, d))
        xc = xc + cw_ref[CONV_WIDTH - 1 - d:CONV_WIDTH - d, :] * tap
    for b in range(nb):
        xs_ref[b] = xl[(b + 1) * LRU_TILE - 8:(b + 1) * LRU_TILE]

    xcb = xc.astype(BF16)
    gate_r = jax.nn.sigmoid(_dot(xcb, wrg_ref[...]) + brg_ref[...])
    gate_i = jax.nn.sigmoid(_dot(xcb, wig_ref[...]) + big_ref[...])
    lam = lam_ref[...]
    log_sig = -(jnp.maximum(-lam, 0.0) + jnp.log1p(jnp.exp(-jnp.abs(lam))))
    log_a = LRU_C * gate_r * log_sig
    a = jnp.exp(log_a)
    mult = jnp.sqrt(jnp.maximum(1.0 - jnp.exp(2.0 * log_a), 0.0))
    b = mult * gate_i * xc
    b = jnp.where((c == 0) & (row < LRU_TILE - N_META), 0.0, b)

    d = 1
    while d < 8:
        keep = in_group >= d
        a_sh = jnp.where(keep, roll_in_group(a, d), 1.0)
        b_sh = jnp.where(keep, roll_in_group(b, d), 0.0)
        b = a * b_sh + b
        a = a * a_sh
        d *= 2
    groups = []
    for bi in range(nb):
        carry = hprev_ref[bi:bi + 1, :]
        for gi in range(LRU_TILE // 8):
            lo = bi * LRU_TILE + gi * 8
            hg = b[lo:lo + 8] + a[lo:lo + 8] * carry
            carry = hg[7:8, :]
            groups.append(hg)
        hprev_ref[bi:bi + 1, :] = carry
    h = jnp.concatenate(groups, axis=0)
    y_ref[...] = (h * _gelu_tanh(gl)).astype(y_ref.dtype).reshape(y_ref.shape)


def _lru_mixer(ul, ul_tail, params):
    B, T, _ = ul.shape
    assert TAIL == LRU_TILE
    x_map = lambda c: (0, jnp.maximum(c - 1, 0), 0)
    return pl.pallas_call(
        _lru_kernel,
        out_shape=jax.ShapeDtypeStruct((B, T, LRU_W), BF16),
        grid=(T // LRU_TILE + 1,),
        in_specs=[pl.BlockSpec((B, LRU_TILE, UL_W), x_map), _full(ul_tail)] + [_full(p) for p in params],
        out_specs=pl.BlockSpec((B, LRU_TILE, LRU_W), x_map),
        scratch_shapes=[pltpu.VMEM((B, LRU_CARRY, LRU_W), F32), pltpu.VMEM((B, LRU_W), F32)],
        compiler_params=_cparams(("arbitrary",)),
        name="rglru",
    )(ul, ul_tail, *params)


def _route(lg):
    lane = lax.broadcasted_iota(jnp.int32, lg.shape, 1)
    neg = jnp.float32(-jnp.inf)
    rmax = lambda t: jnp.max(t, axis=1, keepdims=True)
    first = lambda hit: jnp.min(jnp.where(hit, lane, LANE), axis=1, keepdims=True)
    is_grp = lane < N_GROUPS
    gl = jnp.where(is_grp, lg, neg)
    gmax = rmax(gl)
    g_sel = first(gl == gmax)
    p_g = 1.0 / jnp.sum(jnp.where(is_grp, jnp.exp(lg - gmax), 0.0), axis=1, keepdims=True)
    ex = lane - N_GROUPS
    in_grp = (ex >= 0) & (ex < N_EXPERTS) & (jnp.right_shift(ex, 3) == g_sel)
    el = jnp.where(in_grp, lg, neg)
    v1 = rmax(el)
    i1 = first(el == v1)
    el2 = jnp.where(lane == i1, neg, el)
    v2 = rmax(el2)
    i2 = first(el2 == v2)
    t = jnp.exp(v2 - v1)
    gate1 = p_g / (1.0 + t)
    gate2 = p_g * t / (1.0 + t)
    e1 = (i1 - N_GROUPS).astype(F32)
    e2 = (i2 - N_GROUPS).astype(F32)
    return jnp.where(lane == 0, e1, jnp.where(lane == 1, e2, jnp.where(lane == 2, gate1, jnp.where(lane == 3, gate2, 0.0))))


SLABS = D_MODEL // LANE


def _store_token_tiles(ref, val):
    n = val.shape[0]
    for s in range(SLABS):
        ref[pl.ds(s, n, stride=SLABS), :] = val[:, s * LANE:(s + 1) * LANE]


def _load_token_slabs(ref, n):
    return [ref[pl.ds(s, n, stride=SLABS), :] for s in range(SLABS)]


def _outproj_kernel(x_ref, yr_ref, yl_ref, g0_ref, b0_ref, wor_ref, wol_ref, g1_ref, b1_ref,
                    wrt_hi_ref, wrt_lo_ref, brt_ref, h1_ref, rt_ref):
    h0 = _layer_norm(x_ref[0], g0_ref[...], b0_ref[...])
    mix = _dot(yr_ref[0], wor_ref[...]) + _dot(yl_ref[0], wol_ref[...])
    h1 = _layer_norm(DEEPNORM_ALPHA * h0 + mix, g1_ref[...], b1_ref[...])
    _store_token_tiles(h1_ref.at[0], h1)
    hi = h1.astype(BF16)
    lo = (h1 - hi.astype(F32)).astype(BF16)
    w_hi = wrt_hi_ref[...]
    lg = _dot(hi, w_hi) + (_dot(hi, wrt_lo_ref[...]) + _dot(lo, w_hi)) + brt_ref[...]
    rt_ref[0] = _route(lg)


def _out_projection(x, y_rwkv, y_lru, ln0_g, ln0_b, wo_r, wo_l, ln1_g, ln1_b, wrt_hi, wrt_lo, brt):
    B, T, D = x.shape
    tm = 512
    rows = lambda w: pl.BlockSpec((1, tm, w), lambda b, i: (b, i, 0))
    return pl.pallas_call(
        _outproj_kernel,
        out_shape=(jax.ShapeDtypeStruct((B, T * SLABS, LANE), F32), jax.ShapeDtypeStruct((B, T, LANE), F32)),
        grid=(B, T // tm),
        in_specs=[rows(D), rows(RWKV_W), rows(LRU_W), _full(ln0_g), _full(ln0_b), _full(wo_r), _full(wo_l),
                  _full(ln1_g), _full(ln1_b), _full(wrt_hi), _full(wrt_lo), _full(brt)],
        out_specs=(pl.BlockSpec((1, tm * SLABS, LANE), lambda b, i: (b, i, 0)), rows(LANE)),
        compiler_params=_cparams(("parallel", "parallel")),
        name="outproj",
    )(x, y_rwkv, y_lru, ln0_g, ln0_b, wo_r, wo_l, ln1_g, ln1_b, wrt_hi, wrt_lo, brt)


def _invert_kernel(pad_lo_ref, pad_hi_ref, dest_ref, out_ref):
    i = pl.program_id(0)

    @pl.when(i == 0)
    def _():
        n_asg = pl.num_programs(0) * INVERT_BLOCK

        def zero(j, carry):
            out_ref[j] = jnp.bitwise_and(j, n_asg - 1)
            return carry
        for e in range(N_EXPERTS):
            lax.fori_loop(pad_lo_ref[e], pad_hi_ref[e], zero, 0)
        lax.fori_loop(pad_hi_ref[N_EXPERTS - 1], out_ref.shape[0], zero, 0)

    base = i * INVERT_BLOCK

    def body(j, carry):
        out_ref[dest_ref[j]] = base + j
        return carry

    lax.fori_loop(0, INVERT_BLOCK, body, 0, unroll=8)


def _invert_slots(dest, n_slots, pad_lo, pad_hi):
    A = dest.shape[0]
    grid_spec = pltpu.PrefetchScalarGridSpec(
        num_scalar_prefetch=2,
        grid=(A // INVERT_BLOCK,),
        in_specs=[pl.BlockSpec((INVERT_BLOCK,), lambda i, lo, hi: (i,), memory_space=pltpu.SMEM)],
        out_specs=pl.BlockSpec(memory_space=pltpu.SMEM),
    )
    return pl.pallas_call(
        _invert_kernel,
        out_shape=jax.ShapeDtypeStruct((n_slots,), jnp.int32),
        grid_spec=grid_spec,
        compiler_params=_cparams(("arbitrary",)),
        name="invert_slots",
    )(pad_lo, pad_hi, dest)


SC_WINDOW = 32


def _sc_gather(table, idx):
    info = plsc.get_sparse_core_info()
    n_workers = info.num_cores * info.num_subcores
    n = idx.shape[0]
    per_worker = n // n_workers
    n_win = per_worker // SC_WINDOW
    assert n % (n_workers * SC_WINDOW * 2) == 0
    mesh = plsc.VectorSubcoreMesh(core_axis_name="c", subcore_axis_name="s")

    @functools.partial(
        pl.kernel, mesh=mesh,
        out_type=jax.ShapeDtypeStruct((n, SLABS, LANE), F32),
        scratch_types=[pltpu.VMEM((SC_WINDOW,), jnp.int32), pltpu.VMEM((SC_WINDOW,), jnp.int32),
                       pltpu.VMEM((SC_WINDOW, SLABS, LANE), F32), pltpu.VMEM((SC_WINDOW, SLABS, LANE), F32),
                       pltpu.SemaphoreType.DMA, pltpu.SemaphoreType.DMA],
        name="sc_row_gather",
    )
    def gather_kernel(table_hbm, idx_hbm, out_hbm, idx_a, idx_b, rows_a, rows_b, sem_a, sem_b):
        worker = lax.axis_index("s") * info.num_cores + lax.axis_index("c")
        base = worker * per_worker
        bufs = ((idx_a, rows_a, sem_a), (idx_b, rows_b, sem_b))

        def start(w, buf):
            idx_v, rows_v, sem = buf
            pltpu.sync_copy(idx_hbm.at[pl.ds(base + w * SC_WINDOW, SC_WINDOW)], idx_v)
            pltpu.async_copy(table_hbm.at[idx_v], rows_v, sem)

        def finish(w, buf):
            idx_v, rows_v, sem = buf
            pltpu.make_async_copy(table_hbm.at[idx_v], rows_v, sem).wait()
            pltpu.sync_copy(rows_v, out_hbm.at[pl.ds(base + w * SC_WINDOW, SC_WINDOW)])

        start(0, bufs[0])

        @pl.loop(0, n_win, step=2)
        def _(w):
            start(w + 1, bufs[1])
            finish(w, bufs[0])

            @pl.when(w + 2 < n_win)
            def _():
                start(w + 2, bufs[0])
            finish(w + 1, bufs[1])

    return gather_kernel(table, idx)


def _moe_rows_kernel(te_ref, nv_ref, x_ref, wg_ref, wu_ref, wd_ref, o_ref, wgb_ref, wub_ref, wdb_ref):
    i = pl.program_id(0)
    e = te_ref[i]
    e_prev = te_ref[jnp.maximum(i - 1, 0)]

    @pl.when((i == 0) | (e != e_prev))
    def _():
        wgb_ref[...] = wg_ref[0].astype(BF16)
        wub_ref[...] = wu_ref[0].astype(BF16)
        wdb_ref[...] = wd_ref[0].astype(BF16)

    @pl.when(i < nv_ref[0])
    def _():
        xb = jnp.concatenate(_load_token_slabs(x_ref, MOE_TILE), axis=1).astype(BF16)
        hg = _dot(xb, wgb_ref[...])
        hu = _dot(xb, wub_ref[...])
        mid = (hg * jax.nn.sigmoid(hg) * hu).astype(BF16)
        _store_token_tiles(o_ref, _dot(mid, wdb_ref[...]))

    @pl.when(i >= nv_ref[0])
    def _():
        o_ref[...] = jnp.zeros_like(o_ref)


def _moe_experts_rows(xbuf, tile_expert, n_valid, w_gate, w_up, w_down):
    D = D_MODEL
    n_tiles = xbuf.shape[0] // (MOE_TILE * SLABS)
    tiles = pl.BlockSpec((MOE_TILE * SLABS, LANE), lambda i, te, nv: (jnp.minimum(i, nv[0] - 1), 0))
    grid_spec = pltpu.PrefetchScalarGridSpec(
        num_scalar_prefetch=2,
        grid=(n_tiles,),
        in_specs=[tiles,
                  pl.BlockSpec((1, D, D_EXPERT), lambda i, te, nv: (te[i], 0, 0)),
                  pl.BlockSpec((1, D, D_EXPERT), lambda i, te, nv: (te[i], 0, 0)),
                  pl.BlockSpec((1, D_EXPERT, D), lambda i, te, nv: (te[i], 0, 0))],
        out_specs=pl.BlockSpec((MOE_TILE * SLABS, LANE), lambda i, te, nv: (i, 0)),
        scratch_shapes=[pltpu.VMEM((D, D_EXPERT), BF16), pltpu.VMEM((D, D_EXPERT), BF16),
                        pltpu.VMEM((D_EXPERT, D), BF16)],
    )
    return pl.pallas_call(
        _moe_rows_kernel,
        out_shape=jax.ShapeDtypeStruct((n_tiles * MOE_TILE * SLABS, LANE), F32),
        grid_spec=grid_spec,
        compiler_params=_cparams(("arbitrary",)),
        name="moe_experts",
    )(tile_expert, n_valid, xbuf, w_gate, w_up, w_down)


def _row_copy(src_hbm, src_row, dst_ref, dst_row, sem):
    return pltpu.make_async_copy(src_hbm.at[pl.ds(src_row * SLABS, SLABS), :],
                                 dst_ref.at[pl.ds(dst_row * SLABS, SLABS), :], sem)


def _wait_tiles(src_hbm, dst_ref, sem):
    pltpu.make_async_copy(src_hbm.at[pl.ds(0, dst_ref.shape[0]), :], dst_ref, sem).wait()


def _moe_kernel(te_ref, nv_ref, ra_cur_ref, ra_nxt_ref, h_hbm, wg_ref, wu_ref, wd_ref, o_ref,
                xbuf, sem, wgb_ref, wub_ref, wdb_ref):
    i = pl.program_id(0)
    n_valid = nv_ref[0]
    slot = lax.rem(i, 2)

    def start_gather(ra_ref, s):
        def body(jj, carry):
            for u in range(DMA_UNROLL):
                j = jj * DMA_UNROLL + u
                tok = lax.shift_right_logical(ra_ref[j], 1)
                _row_copy(h_hbm, tok, xbuf.at[s], j, sem.at[s]).start(priority=u % 2)
            return carry
        lax.fori_loop(0, MOE_TILE // DMA_UNROLL, body, 0)

    @pl.when(i == 0)
    def _():
        start_gather(ra_cur_ref, 0)

    @pl.when(i + 1 < n_valid)
    def _():
        start_gather(ra_nxt_ref, 1 - slot)

    e = te_ref[i]
    e_prev = te_ref[jnp.maximum(i - 1, 0)]

    @pl.when((i == 0) | (e != e_prev))
    def _():
        wgb_ref[...] = wg_ref[0].astype(BF16)
        wub_ref[...] = wu_ref[0].astype(BF16)
        wdb_ref[...] = wd_ref[0].astype(BF16)

    @pl.when(i < n_valid)
    def _():
        _wait_tiles(h_hbm, xbuf.at[slot], sem.at[slot])
        xb = jnp.concatenate(_load_token_slabs(xbuf.at[slot], MOE_TILE), axis=1).astype(BF16)
        hg = _dot(xb, wgb_ref[...])
        hu = _dot(xb, wub_ref[...])
        mid = (hg * jax.nn.sigmoid(hg) * hu).astype(BF16)
        _store_token_tiles(o_ref, _dot(mid, wdb_ref[...]))

    @pl.when(i >= n_valid)
    def _():
        o_ref[...] = jnp.zeros_like(o_ref)


def _moe_experts(h1, row_asg, tile_expert, n_valid, w_gate, w_up, w_down):
    D = D_MODEL
    n_tiles = row_asg.shape[0] // MOE_TILE
    smem_tile = lambda f: pl.BlockSpec((MOE_TILE,), f, memory_space=pltpu.SMEM)
    grid_spec = pltpu.PrefetchScalarGridSpec(
        num_scalar_prefetch=2,
        grid=(n_tiles,),
        in_specs=[smem_tile(lambda i, te, nv: (i,)),
                  smem_tile(lambda i, te, nv: (jnp.minimum(i + 1, n_tiles - 1),)),
                  pl.BlockSpec(memory_space=pl.ANY),
                  pl.BlockSpec((1, D, D_EXPERT), lambda i, te, nv: (te[i], 0, 0)),
                  pl.BlockSpec((1, D, D_EXPERT), lambda i, te, nv: (te[i], 0, 0)),
                  pl.BlockSpec((1, D_EXPERT, D), lambda i, te, nv: (te[i], 0, 0))],
        out_specs=pl.BlockSpec((MOE_TILE * SLABS, LANE), lambda i, te, nv: (i, 0)),
        scratch_shapes=[pltpu.VMEM((2, MOE_TILE * SLABS, LANE), F32), pltpu.SemaphoreType.DMA((2,)),
                        pltpu.VMEM((D, D_EXPERT), BF16), pltpu.VMEM((D, D_EXPERT), BF16),
                        pltpu.VMEM((D_EXPERT, D), BF16)],
    )
    return pl.pallas_call(
        _moe_kernel,
        out_shape=jax.ShapeDtypeStruct((n_tiles * MOE_TILE * SLABS, LANE), F32),
        grid_spec=grid_spec,
        compiler_params=_cparams(("arbitrary",)),
        name="moe_experts",
    )(tile_expert, n_valid, row_asg, row_asg, h1, w_gate, w_up, w_down)


def _combine_kernel(d_cur_ref, d_nxt_ref, h_ref, gate_ref, g_ref, b_ref, y_hbm, o_ref, ybuf, sem):
    i = pl.program_id(0)
    n = pl.num_programs(0)
    slot = lax.rem(i, 2)

    def start_gather(d_ref, s):
        def body(tt, carry):
            for u in range(DMA_UNROLL // TOP_K):
                t = tt * (DMA_UNROLL // TOP_K) + u
                for k in range(TOP_K):
                    _row_copy(y_hbm, d_ref[TOP_K * t + k], ybuf.at[s, k], t, sem.at[s]).start(priority=k % 2)
            return carry
        lax.fori_loop(0, COMBINE_TILE * TOP_K // DMA_UNROLL, body, 0)

    @pl.when(i == 0)
    def _():
        start_gather(d_cur_ref, 0)

    @pl.when(i + 1 < n)
    def _():
        start_gather(d_nxt_ref, 1 - slot)

    for k in range(TOP_K):
        _wait_tiles(y_hbm, ybuf.at[slot, k], sem.at[slot])

    gate = gate_ref[...]
    tm = COMBINE_TILE
    ga = jnp.broadcast_to(gate[:, 0:1], (tm, LANE))
    gb = jnp.broadcast_to(gate[:, 1:2], (tm, LANE))
    hs = _load_token_slabs(h_ref, tm)
    ya = _load_token_slabs(ybuf.at[slot, 0], tm)
    yb = _load_token_slabs(ybuf.at[slot, 1], tm)
    z = [DEEPNORM_ALPHA * h + (ga * a + gb * b) for h, a, b in zip(hs, ya, yb)]
    inv_d = 1.0 / D_MODEL
    mu = sum(jnp.sum(t, axis=1, keepdims=True) for t in z) * inv_d
    zc = [t - mu for t in z]
    var = sum(jnp.sum(t * t, axis=1, keepdims=True) for t in zc) * inv_d
    rstd = lax.rsqrt(var + LN_EPS)
    for s in range(SLABS):
        cols = slice(s * LANE, (s + 1) * LANE)
        o_ref[:, cols] = zc[s] * rstd * g_ref[:, cols] + b_ref[:, cols]


def _combine(h1, ybuf, dest, gates, ln2_g, ln2_b):
    D = D_MODEL
    M = h1.shape[0] // SLABS
    tm = COMBINE_TILE
    n = M // tm
    smem_tile = lambda f: pl.BlockSpec((TOP_K * tm,), f, memory_space=pltpu.SMEM)
    return pl.pallas_call(
        _combine_kernel,
        out_shape=jax.ShapeDtypeStruct((M, D), F32),
        grid=(n,),
        in_specs=[smem_tile(lambda i: (i,)), smem_tile(lambda i: (jnp.minimum(i + 1, n - 1),)),
                  pl.BlockSpec((tm * SLABS, LANE), lambda i: (i, 0)), pl.BlockSpec((tm, TOP_K), lambda i: (i, 0)),
                  _full(ln2_g), _full(ln2_b), pl.BlockSpec(memory_space=pl.ANY)],
        out_specs=pl.BlockSpec((tm, D), lambda i: (i, 0)),
        scratch_shapes=[pltpu.VMEM((2, TOP_K, tm * SLABS, LANE), F32), pltpu.SemaphoreType.DMA((2,))],
        compiler_params=_cparams(("arbitrary",)),
        name="combine",
    )(dest, dest, h1, gates, ln2_g, ln2_b, ybuf)


def _routing_plan(route):
    M = route.shape[0]
    eid = route[:, :TOP_K].astype(jnp.int32).reshape(-1)
    gates = route[:, TOP_K:2 * TOP_K]
    A = M * TOP_K
    onehot = (eid[:, None] == jnp.arange(N_EXPERTS, dtype=eid.dtype)[None, :]).astype(jnp.int32)
    csum = jnp.cumsum(onehot, axis=0)
    rank = jnp.sum(csum * onehot, axis=1) - 1
    counts = csum[-1]
    pcounts = (counts + MOE_TILE - 1) // MOE_TILE * MOE_TILE
    pends = jnp.cumsum(pcounts)
    pstarts = pends - pcounts
    dest = (jnp.sum(onehot * pstarts[None, :], axis=1) + rank).astype(jnp.int32)
    n_tiles = (A + N_EXPERTS * (MOE_TILE - 1) + MOE_TILE - 1) // MOE_TILE
    n_valid = (pends[-1] // MOE_TILE).astype(jnp.int32)
    tile_start = jnp.minimum(jnp.arange(n_tiles, dtype=jnp.int32) * MOE_TILE, pends[-1] - 1)
    tile_expert = jnp.sum((pends[None, :] <= tile_start[:, None]).astype(jnp.int32), axis=1)
    tile_expert = jnp.minimum(tile_expert, N_EXPERTS - 1).astype(jnp.int32)
    pad_lo = (pstarts + counts).astype(jnp.int32)
    pad_hi = pends.astype(jnp.int32)
    return gates, dest, n_tiles * MOE_TILE, tile_expert, n_valid.reshape(1), pad_lo, pad_hi


def kernel(x, meta, ln0_g, ln0_b, w_in, mu_shift, w0, w_decay_up, a0, w_a_up, w_g_up, k_k, k_a, r_k, gn_g, gn_b, conv_w, conv_b, w_rg, b_rg, w_ig, b_ig, lru_lambda, w_out, ln1_g, ln1_b, w_router_grp, b_router_grp, w_router_exp, b_router_exp, w_exp_gate, w_exp_up, w_exp_down, ln2_g, ln2_b):
    B, T, D = x.shape
    assert D == D_MODEL and T % 512 == 0 and w_in.shape[0] == 1
    assert (B * T * TOP_K) % INVERT_BLOCK == 0
    row = lambda p: p.reshape(1, -1).astype(F32)
    n_rw = 3 * RWKV_W
    w_in0 = w_in[0]

    def slots(p):
        pad = lambda a, n: jnp.pad(a, [(0, 0)] * (a.ndim - 1) + [(0, n - a.shape[-1])])
        zw = p[..., n_rw:n_rw + DECAY_RANK]
        za = p[..., n_rw + DECAY_RANK:n_rw + DECAY_RANK + AAA_RANK]
        zg = p[..., n_rw + DECAY_RANK + AAA_RANK:n_rw + DECAY_RANK + AAA_RANK + GATE_RANK]
        return jnp.concatenate([p[..., :n_rw], pad(zw, LANE), pad(za, LANE), pad(zg, ZG_SLOT)], axis=-1)

    rwkv_cols = n_rw + DECAY_RANK + AAA_RANK + GATE_RANK
    w_r = slots(w_in0[:, :rwkv_cols]).astype(BF16)
    w_l = w_in0[:, rwkv_cols:].astype(BF16)
    ur, ul, ur_t, ul_t = _in_projection(x, meta, row(ln0_g), row(ln0_b), w_r, w_l)

    pad_rows = lambda a, n: jnp.pad(a, ((0, n - a.shape[0]), (0, 0)))
    rwkv_params = (slots(mu_shift[0][None, :]).astype(F32), row(w0[0]), pad_rows(w_decay_up[0], LANE).astype(BF16),
                   row(a0[0]), pad_rows(w_a_up[0], LANE).astype(BF16), pad_rows(w_g_up[0], ZG_SLOT).astype(BF16),
                   row(k_k[0]), row(k_a[0]), row(r_k[0]), row(gn_g[0]), row(gn_b[0]))
    y_rwkv = _rwkv_pipe_mixer(ur, ur_t, rwkv_params)

    blockdiag = lambda w: jax.scipy.linalg.block_diag(*[w[i] for i in range(LRU_BLOCKS)]).astype(BF16)
    lru_params = (conv_w[0], row(conv_b[0]), blockdiag(w_rg[0]), row(b_rg[0]), blockdiag(w_ig[0]), row(b_ig[0]),
                  row(lru_lambda[0]))
    y_lru = _lru_mixer(ul, ul_t, lru_params)

    w_rt = jnp.concatenate([w_router_grp[0], w_router_exp[0]], axis=1)
    w_rt = jnp.pad(w_rt, ((0, 0), (0, LANE - w_rt.shape[1])))
    wrt_hi = w_rt.astype(BF16)
    wrt_lo = (w_rt - wrt_hi.astype(F32)).astype(BF16)
    b_rt = jnp.concatenate([b_router_grp[0], b_router_exp[0]])
    b_rt = jnp.pad(b_rt, (0, LANE - b_rt.shape[0])).reshape(1, LANE)
    wo = w_out[0].astype(BF16)
    h1, route = _out_projection(x, y_rwkv, y_lru, row(ln0_g), row(ln0_b), wo[:RWKV_W], wo[RWKV_W:],
                                row(ln1_g[0]), row(ln1_b[0]), wrt_hi, wrt_lo, b_rt)

    M = B * T
    h1 = h1.reshape(M * SLABS, LANE)
    gates, dest, n_slots, tile_expert, n_valid, pad_lo, pad_hi = _routing_plan(route.reshape(M, LANE))
    row_asg = _invert_slots(dest, n_slots, pad_lo, pad_hi)
    src_tok = lax.shift_right_logical(row_asg, 1)
    xbuf = _sc_gather(h1.reshape(M, SLABS, LANE), src_tok).reshape(-1, LANE)
    ybuf = _moe_experts_rows(xbuf, tile_expert, n_valid, w_exp_gate[0], w_exp_up[0], w_exp_down[0])
    out = _combine(h1, ybuf, dest, gates, row(ln2_g[0]), row(ln2_b[0]))
    return out.reshape(B, T, D)
```

```python
import functools
import math

import jax
import jax.numpy as jnp
from jax import lax
from jax.experimental import pallas as pl
from jax.experimental.pallas import tpu as pltpu
from jax.experimental.pallas import tpu_sc as plsc

F32 = jnp.float32
BF16 = jnp.bfloat16

D_MODEL = 1024
N_META = 16
RWKV_W = 512
RWKV_HEAD = 64
DECAY_RANK = 64
AAA_RANK = 64
GATE_RANK = 160
LRU_W = 512
LRU_BLOCKS = 8
CONV_WIDTH = 4
LRU_C = 8.0
N_GROUPS = 4
EXPERTS_PER_GROUP = 8
N_EXPERTS = N_GROUPS * EXPERTS_PER_GROUP
TOP_K = 2
D_EXPERT = 512
LN_EPS = 1e-5
GN_EPS = 64e-5
DEEPNORM_ALPHA = 2.0 ** 0.25

LANE = 128
OFF_R, OFF_K, OFF_V = 0, RWKV_W, 2 * RWKV_W
OFF_ZW = 3 * RWKV_W
OFF_ZA = OFF_ZW + LANE
OFF_ZG = OFF_ZA + LANE
ZG_SLOT = 2 * LANE
UR_W = OFF_ZG + ZG_SLOT
UL_W = 2 * LRU_W

TAIL = 256
CHUNK = 64
HEADS_PER_GROUP = 4
GW = HEADS_PER_GROUP * RWKV_HEAD
N_HGROUPS = RWKV_W // GW
LRU_TILE = TAIL
MOE_TILE = 256
COMBINE_TILE = 256
INVERT_BLOCK = 4096
DMA_UNROLL = 8
V7X_VMEM_BYTES = 64 * 1024 * 1024
VMEM_LIMIT = V7X_VMEM_BYTES - 8 * 1024 * 1024


def _cparams(sem):
    return pltpu.CompilerParams(dimension_semantics=sem, vmem_limit_bytes=VMEM_LIMIT)


def _layer_norm(x, g, b):
    mu = jnp.mean(x, -1, keepdims=True)
    xc = x - mu
    var = jnp.mean(xc * xc, -1, keepdims=True)
    return xc * lax.rsqrt(var + LN_EPS) * g + b


def _dot(a, b):
    return jnp.dot(a, b, preferred_element_type=F32)


def _dot_nt(a, b):
    return lax.dot_general(a, b, (((1,), (1,)), ((), ())), preferred_element_type=F32)


def _dot_tn(a, b):
    return lax.dot_general(a, b, (((0,), (0,)), ((), ())), preferred_element_type=F32)


def _full(a):
    return pl.BlockSpec(a.shape, lambda *_: (0,) * a.ndim)


def _inproj_kernel(x_ref, g_ref, b_ref, wr_ref, wl_ref, ur_ref, ul_ref):
    h = _layer_norm(x_ref[0], g_ref[...], b_ref[...]).astype(BF16)
    ur_ref[0] = _dot(h, wr_ref[...])
    ul_ref[0] = _dot(h, wl_ref[...])


def _inproj_tail_kernel(x_ref, g_ref, b_ref, wr_ref, wl_ref, ur_ref, ul_ref):
    h = _layer_norm(x_ref[...], g_ref[...], b_ref[...]).astype(BF16)
    rows = lax.broadcasted_iota(jnp.int32, (TAIL, 1), 0)
    valid = (rows >= TAIL - N_META).astype(F32)
    ur_ref[...] = _dot(h, wr_ref[...]) * valid
    ul_ref[...] = _dot(h, wl_ref[...]) * valid


def _in_projection(x, meta, ln0_g, ln0_b, w_r, w_l):
    B, T, D = x.shape
    tm = 512
    ur, ul = pl.pallas_call(
        _inproj_kernel,
        out_shape=(jax.ShapeDtypeStruct((B, T, UR_W), F32), jax.ShapeDtypeStruct((B, T, UL_W), F32)),
        grid=(B, T // tm),
        in_specs=[pl.BlockSpec((1, tm, D), lambda b, i: (b, i, 0)), _full(ln0_g), _full(ln0_b), _full(w_r), _full(w_l)],
        out_specs=(pl.BlockSpec((1, tm, UR_W), lambda b, i: (b, i, 0)),
                   pl.BlockSpec((1, tm, UL_W), lambda b, i: (b, i, 0))),
        compiler_params=_cparams(("parallel", "parallel")),
        name="inproj",
    )(x, ln0_g, ln0_b, w_r, w_l)
    tail_x = jnp.concatenate([jnp.zeros((TAIL - N_META, D), F32), meta.astype(F32)], axis=0)
    ur_t, ul_t = pl.pallas_call(
        _inproj_tail_kernel,
        out_shape=(jax.ShapeDtypeStruct((TAIL, UR_W), F32), jax.ShapeDtypeStruct((TAIL, UL_W), F32)),
        grid=(1,),
        in_specs=[_full(tail_x), _full(ln0_g), _full(ln0_b), _full(w_r), _full(w_l)],
        out_specs=(pl.BlockSpec((TAIL, UR_W), lambda i: (0, 0)), pl.BlockSpec((TAIL, UL_W), lambda i: (0, 0))),
        compiler_params=_cparams(("arbitrary",)),
        name="inproj_tail",
    )(tail_x, ln0_g, ln0_b, w_r, w_l)
    return ur, ul, ur_t, ul_t


def _rwkv_pipe_kernel(u_ref, ut_ref, mu_ref, w0_ref, wdu_ref, a0_ref, wau_ref, wgu_ref, kk_ref, ka_ref, rk_ref,
                      gng_ref, gnb_ref, bones_ref, bm_ref, eye_ref, msl_ref, mil_ref,
                      m8_ref, m16_ref, m32_ref, m64_ref, y_ref,
                      s_ref, prev_ref, y0_s, q_s, mc_s, nc_s, we_s, bonus_s, g_s, yraw_s):
    s_id = pl.program_id(0)
    nb = u_ref.shape[0]
    blk = u_ref.shape[1]
    npc = blk // CHUNK
    nseq = nb * npc
    seq_rows = lambda q: slice(q * CHUNK, (q + 1) * CHUNK)
    per_seq = lambda f: jnp.concatenate([f(q) for q in range(nseq)], axis=0)
    w_slot = lax.rem(s_id, 2)
    r_slot = 1 - w_slot

    @pl.when(s_id == 0)
    def _():
        s_ref[...] = jnp.zeros_like(s_ref)
        prev_ref[...] = jnp.zeros_like(prev_ref)
        for ref in (y0_s, q_s, mc_s, nc_s, we_s, bonus_s, g_s):
            ref[1] = jnp.zeros(ref.shape[1:], ref.dtype)

    b16 = lambda t: t.astype(BF16)
    bones = bones_ref[...]
    head_sum = lambda t: _dot(b16(t), bones)
    bm = bm_ref[...]
    bm16 = b16(bm)
    tile4 = lambda t: jnp.concatenate([t] * HEADS_PER_GROUP, axis=0)
    fold4 = lambda t: sum(t[i * CHUNK:(i + 1) * CHUNK] for i in range(HEADS_PER_GROUP))
    bd = lambda t: tile4(b16(t)) * bm16

    chains = [(b, hg) for b in range(nb) for hg in range(N_HGROUPS)]
    states = {c: s_ref[c[0], c[1]] for c in chains}

    def recurrent_chunk(j):
        for b, hg in chains:
            sl = slice(hg * GW, (hg + 1) * GW)
            q = b * npc + j
            rq = seq_rows(q)
            s = states[(b, hg)]
            yraw_s[rq, sl] = y0_s[r_slot, rq, sl] + _dot_nt(q_s[r_slot, rq, sl], bd(s))
            states[(b, hg)] = (s * we_s[r_slot, q * 8:q * 8 + 1, sl] + _dot(b16(s), bd(mc_s[r_slot, rq, sl]))
                               + nc_s[r_slot, rq, sl])

    prepared = {}

    def prepare(b):
        lo = b * blk
        u = jnp.where(s_id == 0, ut_ref[...], u_ref[b])
        row = lax.broadcasted_iota(jnp.int32, u.shape, 0)
        u_prev = jnp.where(row == 0, prev_ref[b:b + 1, :], pltpu.roll(u, 1, 0))
        prev_ref[b:b + 1, :] = u[blk - 1:blk, :]
        x = u + (u_prev - u) * mu_ref[...]
        r = x[:, OFF_R:OFF_R + RWKV_W]
        k = x[:, OFF_K:OFF_K + RWKV_W]
        v = x[:, OFF_V:OFF_V + RWKV_W]
        zw = x[:, OFF_ZW:OFF_ZW + LANE]
        za = x[:, OFF_ZA:OFF_ZA + LANE]
        zg = x[:, OFF_ZG:OFF_ZG + ZG_SLOT]
        yield
        z = w0_ref[...] + _dot(b16(jnp.tanh(zw)), wdu_ref[...])
        logw = -math.exp(-0.5) * jax.nn.sigmoid(z)
        a = jax.nn.sigmoid(a0_ref[...] + _dot(b16(za), wau_ref[...]))
        g = _dot(b16(jax.nn.sigmoid(zg)), wgu_ref[...])
        kk = k * kk_ref[...]
        kk = kk / jnp.maximum(jnp.sqrt(head_sum(kk * kk)), 1e-12)
        k = k * (1.0 + (a - 1.0) * ka_ref[...])
        kka = kk * a
        bonus_s[w_slot, lo:lo + blk, :] = head_sum(r * k * rk_ref[...]) * v
        g_s[w_slot, lo:lo + blk, :] = g
        yield
        cl = logw
        row_in_chunk = jnp.bitwise_and(lax.broadcasted_iota(jnp.int32, cl.shape, 0), CHUNK - 1)
        d = 1
        while d < CHUNK:
            cl = cl + jnp.where(row_in_chunk >= d, pltpu.roll(cl, d, 0), 0.0)
            d *= 2
        yield
        cl_last = jnp.concatenate(
            [jnp.broadcast_to(cl[(j + 1) * CHUNK - 1:(j + 1) * CHUNK, :], (CHUNK, RWKV_W)) for j in range(npc)], axis=0)
        e_neg = jnp.exp(-cl)
        e_end = jnp.exp(cl_last - cl)
        w_end = jnp.exp(cl_last)
        for j in range(npc):
            q = b * npc + j
            we_s[w_slot, q * 8:(q + 1) * 8, :] = w_end[j * CHUNK:j * CHUNK + 8, :]
        prepared[b] = dict(rt=r * jnp.exp(cl), kt=k * e_neg, at=-kk * jnp.exp(cl - logw), bt=kka * e_neg,
                           kw=k * e_end, bw=kka * e_end, v=v)
        yield

    eye = eye_ref[...]
    msl = msl_ref[...]
    mil = mil_ref[...]
    rows2 = lambda x, y: jnp.concatenate([x, y], axis=0)
    cols2 = lambda x, y: jnp.concatenate([x, y], axis=1)
    each = lambda f, *ls: [f(*xs) for xs in zip(*ls)]

    def solve(b, p):
        probs = [(slice(j * CHUNK, (j + 1) * CHUNK), slice(hg * GW, (hg + 1) * GW))
                 for j in range(npc) for hg in range(N_HGROUPS)]
        pick = lambda t: [t[rq, sl] for rq, sl in probs]
        at_w, rt_w, v_w = pick(p["at"]), pick(p["rt"]), pick(p["v"])
        lhs = each(lambda x, y: b16(rows2(x, y)), at_w, rt_w)
        ab = each(_dot_nt, lhs, each(bd, pick(p["bt"])))
        ak = each(_dot_nt, lhs, each(bd, pick(p["kt"])))
        yield
        a_ab = each(lambda t: t[:CHUNK] * msl, ab)
        a_rb = each(lambda t: b16(t[CHUNK:] * mil), ab)
        a_xk = each(lambda t: b16(rows2(t[:CHUNK] * msl, t[CHUNK:] * mil)), ak)
        a0 = each(lambda t: b16(t * m8_ref[...]), a_ab)
        a2 = each(lambda t: b16(_dot(t, bd(t))), a0)
        yield
        a4 = each(lambda t: b16(_dot(t, bd(t))), a2)
        p1 = each(lambda t: eye + t.astype(F32), a0)
        p1 = each(lambda q, t: q + _dot(b16(q), bd(t)), p1, a2)
        yield
        tt = each(lambda q, t: q + _dot(b16(q), bd(t)), p1, a4)
        yield
        for m_ref in (m16_ref, m32_ref, m64_ref):
            tb = each(b16, tt)
            off = each(lambda t: b16(t * m_ref[...]), a_ab)
            half = each(lambda x, y: b16(_dot(x, bd(y))), tb, off)
            yield
            tt = each(lambda t, x, y: t + _dot(x, bd(y)), tt, half, tb)
            yield
        tb = each(b16, tt)
        xv = each(lambda x, y: _dot(x, bd(y)), a_xk, v_w)
        yield
        u0 = each(lambda x, y: _dot(x, bd(y[:CHUNK])), tb, xv)
        ta = each(lambda x, y: _dot(x, bd(y)), tb, at_w)
        yield
        y0 = each(lambda x, y, z: _dot(x, bd(y)) + z[CHUNK:], a_rb, u0, xv)
        qq = each(lambda x, y, z: x + _dot(y, bd(z)), rt_w, a_rb, ta)
        yield
        left = each(lambda x, y, z: b16(rows2(cols2(x, y), cols2(jnp.zeros_like(z), z))), ta, u0, v_w)
        right = each(lambda x, y: b16(rows2(x, y)), pick(p["bw"]), pick(p["kw"]))
        mn = each(_dot_tn, left, right)
        for i, (rq, sl) in enumerate(probs):
            rows = slice(b * blk + rq.start, b * blk + rq.stop)
            y0_s[w_slot, rows, sl] = y0[i]
            q_s[w_slot, rows, sl] = b16(qq[i])
            mc_s[w_slot, rows, sl] = b16(fold4(mn[i][:GW] * bm))
            nc_s[w_slot, rows, sl] = fold4(mn[i][GW:] * bm)
        yield

    first = prepare(0)
    for j in range(npc):
        recurrent_chunk(j)
        if j * 4 // npc != (j + 1) * 4 // npc or j == npc - 1:
            for _ in range((j + 1) * 4 // npc - j * 4 // npc):
                next(first, None)
    for _ in first:
        pass
    for (b, hg), s in states.items():
        s_ref[b, hg] = s

    y = yraw_s[...]
    inv_n = 1.0 / RWKV_HEAD
    ym = head_sum(y) * inv_n
    yc = y - ym
    yv = head_sum(yc * yc) * inv_n
    yn = yc * lax.rsqrt(yv + GN_EPS) * gng_ref[...] + gnb_ref[...]
    y_ref[...] = ((yn + bonus_s[r_slot]) * g_s[r_slot]).astype(y_ref.dtype).reshape(y_ref.shape)

    for b in range(nb):
        solver = solve(b, prepared[b])
        nxt = prepare(b + 1) if b + 1 < nb else iter(())
        for level, _ in enumerate(solver):
            if level % 4 == 3:
                next(nxt, None)
        for _ in nxt:
            pass


def _rwkv_masks():
    f = lambda m: m.astype(F32)
    i = jnp.arange(GW)[:, None]
    j = jnp.arange(GW)[None, :]
    bm = f((i // RWKV_HEAD) == (j // RWKV_HEAD))
    t = jnp.arange(CHUNK)[:, None]
    s = (jnp.arange(GW) % CHUNK)[None, :]
    same = lambda n: (t // n) == (s // n)
    msl = f(t > s)
    mil = f(t >= s)
    m8 = f(same(8))
    m16 = f(same(16) & ~same(8))
    m32 = f(same(32) & ~same(16))
    m64 = f(~same(32))
    eye = f(t == s)
    hi = jnp.arange(RWKV_W)
    bones = ((hi[:, None] // RWKV_HEAD) == (hi[None, :] // RWKV_HEAD)).astype(BF16)
    return bones, bm, eye, msl, mil, m8, m16, m32, m64


def _rwkv_pipe_mixer(ur, ur_tail, params):
    B, T, _ = ur.shape
    blk = TAIL
    assert T % blk == 0 and blk % CHUNK == 0 and CHUNK == RWKV_HEAD
    n_blocks = T // blk
    rows = B * blk
    consts = _rwkv_masks()
    in_map = lambda s: (0, jnp.clip(s - 1, 0, n_blocks - 1), 0)
    out_map = lambda s: (0, jnp.clip(s - 2, 0, n_blocks - 1), 0)
    slot2 = lambda w, dt: pltpu.VMEM((2, rows, w), dt)
    return pl.pallas_call(
        _rwkv_pipe_kernel,
        out_shape=jax.ShapeDtypeStruct((B, T, RWKV_W), BF16),
        grid=(n_blocks + 2,),
        in_specs=[pl.BlockSpec((B, blk, UR_W), in_map), _full(ur_tail)]
                 + [_full(p) for p in params] + [_full(m) for m in consts],
        out_specs=pl.BlockSpec((B, blk, RWKV_W), out_map),
        scratch_shapes=[pltpu.VMEM((B, N_HGROUPS, CHUNK, GW), F32), pltpu.VMEM((B, UR_W), F32),
                        slot2(RWKV_W, F32), slot2(RWKV_W, BF16), slot2(RWKV_W, BF16), slot2(RWKV_W, F32),
                        pltpu.VMEM((2, 8 * rows // CHUNK, RWKV_W), F32), slot2(RWKV_W, F32), slot2(RWKV_W, F32),
                        pltpu.VMEM((rows, RWKV_W), F32)],
        compiler_params=_cparams(("arbitrary",)),
        name="rwkv7",
    )(ur, ur_tail, *params, *consts)


def _gelu_tanh(x):
    return 0.5 * x * (1.0 + jnp.tanh(math.sqrt(2.0 / math.pi) * (x + 0.044715 * (x * x * x))))


LRU_CARRY = 8


def _lru_kernel(u_ref, ut_ref, cw_ref, cb_ref, wrg_ref, brg_ref, wig_ref, big_ref, lam_ref, y_ref,
                xs_ref, hprev_ref):
    c = pl.program_id(0)
    nb = u_ref.shape[0]
    nrow = nb * LRU_TILE

    @pl.when(c == 0)
    def _():
        xs_ref[...] = jnp.zeros_like(xs_ref)
        hprev_ref[...] = jnp.zeros_like(hprev_ref)

    u_x = u_ref[...].reshape(nrow, UL_W)
    u = jnp.where(c == 0, jnp.concatenate([ut_ref[...]] * nb, axis=0), u_x)
    xl = u[:, :LRU_W]
    gl = u[:, LRU_W:]
    row = jnp.bitwise_and(lax.broadcasted_iota(jnp.int32, (nrow, LRU_W), 0), LRU_TILE - 1)
    in_group = jnp.bitwise_and(row, 7)
    roll_in_group = lambda t, d: pltpu.roll(t.reshape(t.shape[0] // 8, 8, LRU_W), d, 1).reshape(t.shape)
    xl_prev = jnp.concatenate(
        [p for b in range(nb) for p in (xs_ref[b], xl[b * LRU_TILE:(b + 1) * LRU_TILE - 8])], axis=0)
    xc = cb_ref[...] + cw_ref[CONV_WIDTH - 1:CONV_WIDTH, :] * xl
    for d in range(1, CONV_WIDTH):
        tap = jnp.where(in_group >= d, roll_in_group(xl, d), roll_in_group(xl_prev, d))
        xc = xc + cw_ref[CONV_WIDTH - 1 - d:CONV_WIDTH - d, :] * tap
    for b in range(nb):
        xs_ref[b] = xl[(b + 1) * LRU_TILE - 8:(b + 1) * LRU_TILE]

    xcb = xc.astype(BF16)
    gate_r = jax.nn.sigmoid(_dot(xcb, wrg_ref[...]) + brg_ref[...])
    gate_i = jax.nn.sigmoid(_dot(xcb, wig_ref[...]) + big_ref[...])
    lam = lam_ref[...]
    log_sig = -(jnp.maximum(-lam, 0.0) + jnp.log1p(jnp.exp(-jnp.abs(lam))))
    log_a = LRU_C * gate_r * log_sig
    a = jnp.exp(log_a)
    mult = jnp.sqrt(jnp.maximum(1.0 - jnp.exp(2.0 * log_a), 0.0))
    b = mult * gate_i * xc
    b = jnp.where((c == 0) & (row < LRU_TILE - N_META), 0.0, b)

    d = 1
    while d < 8:
        keep = in_group >= d
        a_sh = jnp.where(keep, roll_in_group(a, d), 1.0)
        b_sh = jnp.where(keep, roll_in_group(b, d), 0.0)
        b = a * b_sh + b
        a = a * a_sh
        d *= 2
    groups = []
    for bi in range(nb):
        carry = hprev_ref[bi:bi + 1, :]
        for gi in range(LRU_TILE // 8):
            lo = bi * LRU_TILE + gi * 8
            hg = b[lo:lo + 8] + a[lo:lo + 8] * carry
            carry = hg[7:8, :]
            groups.append(hg)
        hprev_ref[bi:bi + 1, :] = carry
    h = jnp.concatenate(groups, axis=0)
    y_ref[...] = (h * _gelu_tanh(gl)).astype(y_ref.dtype).reshape(y_ref.shape)


def _lru_mixer(ul, ul_tail, params):
    B, T, _ = ul.shape
    assert TAIL == LRU_TILE
    x_map = lambda c: (0, jnp.maximum(c - 1, 0), 0)
    return pl.pallas_call(
        _lru_kernel,
        out_shape=jax.ShapeDtypeStruct((B, T, LRU_W), BF16),
        grid=(T // LRU_TILE + 1,),
        in_specs=[pl.BlockSpec((B, LRU_TILE, UL_W), x_map), _full(ul_tail)] + [_full(p) for p in params],
        out_specs=pl.BlockSpec((B, LRU_TILE, LRU_W), x_map),
        scratch_shapes=[pltpu.VMEM((B, LRU_CARRY, LRU_W), F32), pltpu.VMEM((B, LRU_W), F32)],
        compiler_params=_cparams(("arbitrary",)),
        name="rglru",
    )(ul, ul_tail, *params)


def _route(lg):
    lane = lax.broadcasted_iota(jnp.int32, lg.shape, 1)
    neg = jnp.float32(-jnp.inf)
    rmax = lambda t: jnp.max(t, axis=1, keepdims=True)
    first = lambda hit: jnp.min(jnp.where(hit, lane, LANE), axis=1, keepdims=True)
    is_grp = lane < N_GROUPS
    gl = jnp.where(is_grp, lg, neg)
    gmax = rmax(gl)
    g_sel = first(gl == gmax)
    p_g = 1.0 / jnp.sum(jnp.where(is_grp, jnp.exp(lg - gmax), 0.0), axis=1, keepdims=True)
    ex = lane - N_GROUPS
    in_grp = (ex >= 0) & (ex < N_EXPERTS) & (jnp.right_shift(ex, 3) == g_sel)
    el = jnp.where(in_grp, lg, neg)
    v1 = rmax(el)
    i1 = first(el == v1)
    el2 = jnp.where(lane == i1, neg, el)
    v2 = rmax(el2)
    i2 = first(el2 == v2)
    t = jnp.exp(v2 - v1)
    gate1 = p_g / (1.0 + t)
    gate2 = p_g * t / (1.0 + t)
    e1 = (i1 - N_GROUPS).astype(F32)
    e2 = (i2 - N_GROUPS).astype(F32)
    return jnp.where(lane == 0, e1, jnp.where(lane == 1, e2, jnp.where(lane == 2, gate1, jnp.where(lane == 3, gate2, 0.0))))


SLABS = D_MODEL // LANE


def _store_token_tiles(ref, val):
    n = val.shape[0]
    for s in range(SLABS):
        ref[pl.ds(s, n, stride=SLABS), :] = val[:, s * LANE:(s + 1) * LANE]


def _load_token_slabs(ref, n):
    return [ref[pl.ds(s, n, stride=SLABS), :] for s in range(SLABS)]


def _outproj_kernel(x_ref, yr_ref, yl_ref, g0_ref, b0_ref, wor_ref, wol_ref, g1_ref, b1_ref,
                    wrt_hi_ref, wrt_lo_ref, brt_ref, h1_ref, rt_ref):
    h0 = _layer_norm(x_ref[0], g0_ref[...], b0_ref[...])
    mix = _dot(yr_ref[0], wor_ref[...]) + _dot(yl_ref[0], wol_ref[...])
    h1 = _layer_norm(DEEPNORM_ALPHA * h0 + mix, g1_ref[...], b1_ref[...])
    _store_token_tiles(h1_ref.at[0], h1)
    hi = h1.astype(BF16)
    lo = (h1 - hi.astype(F32)).astype(BF16)
    w_hi = wrt_hi_ref[...]
    lg = _dot(hi, w_hi) + (_dot(hi, wrt_lo_ref[...]) + _dot(lo, w_hi)) + brt_ref[...]
    rt_ref[0] = _route(lg)


def _out_projection(x, y_rwkv, y_lru, ln0_g, ln0_b, wo_r, wo_l, ln1_g, ln1_b, wrt_hi, wrt_lo, brt):
    B, T, D = x.shape
    tm = 512
    rows = lambda w: pl.BlockSpec((1, tm, w), lambda b, i: (b, i, 0))
    return pl.pallas_call(
        _outproj_kernel,
        out_shape=(jax.ShapeDtypeStruct((B, T * SLABS, LANE), F32), jax.ShapeDtypeStruct((B, T, LANE), F32)),
        grid=(B, T // tm),
        in_specs=[rows(D), rows(RWKV_W), rows(LRU_W), _full(ln0_g), _full(ln0_b), _full(wo_r), _full(wo_l),
                  _full(ln1_g), _full(ln1_b), _full(wrt_hi), _full(wrt_lo), _full(brt)],
        out_specs=(pl.BlockSpec((1, tm * SLABS, LANE), lambda b, i: (b, i, 0)), rows(LANE)),
        compiler_params=_cparams(("parallel", "parallel")),
        name="outproj",
    )(x, y_rwkv, y_lru, ln0_g, ln0_b, wo_r, wo_l, ln1_g, ln1_b, wrt_hi, wrt_lo, brt)


def _invert_kernel(pad_lo_ref, pad_hi_ref, dest_ref, out_ref):
    i = pl.program_id(0)

    @pl.when(i == 0)
    def _():
        n_asg = pl.num_programs(0) * INVERT_BLOCK

        def zero(j, carry):
            out_ref[j] = jnp.bitwise_and(j, n_asg - 1)
            return carry
        for e in range(N_EXPERTS):
            lax.fori_loop(pad_lo_ref[e], pad_hi_ref[e], zero, 0)
        lax.fori_loop(pad_hi_ref[N_EXPERTS - 1], out_ref.shape[0], zero, 0)

    base = i * INVERT_BLOCK

    def body(j, carry):
        out_ref[dest_ref[j]] = base + j
        return carry

    lax.fori_loop(0, INVERT_BLOCK, body, 0, unroll=8)


def _invert_slots(dest, n_slots, pad_lo, pad_hi):
    A = dest.shape[0]
    grid_spec = pltpu.PrefetchScalarGridSpec(
        num_scalar_prefetch=2,
        grid=(A // INVERT_BLOCK,),
        in_specs=[pl.BlockSpec((INVERT_BLOCK,), lambda i, lo, hi: (i,), memory_space=pltpu.SMEM)],
        out_specs=pl.BlockSpec(memory_space=pltpu.SMEM),
    )
    return pl.pallas_call(
        _invert_kernel,
        out_shape=jax.ShapeDtypeStruct((n_slots,), jnp.int32),
        grid_spec=grid_spec,
        compiler_params=_cparams(("arbitrary",)),
        name="invert_slots",
    )(pad_lo, pad_hi, dest)


SC_WINDOW = 32


def _sc_gather(table, idx):
    info = plsc.get_sparse_core_info()
    n_workers = info.num_cores * info.num_subcores
    n = idx.shape[0]
    per_worker = n // n_workers
    n_win = per_worker // SC_WINDOW
    assert n % (n_workers * SC_WINDOW * 2) == 0
    mesh = plsc.VectorSubcoreMesh(core_axis_name="c", subcore_axis_name="s")

    @functools.partial(
        pl.kernel, mesh=mesh,
        out_type=jax.ShapeDtypeStruct((n, SLABS, LANE), F32),
        scratch_types=[pltpu.VMEM((SC_WINDOW,), jnp.int32), pltpu.VMEM((SC_WINDOW,), jnp.int32),
                       pltpu.VMEM((SC_WINDOW, SLABS, LANE), F32), pltpu.VMEM((SC_WINDOW, SLABS, LANE), F32),
                       pltpu.SemaphoreType.DMA, pltpu.SemaphoreType.DMA],
        name="sc_row_gather",
    )
    def gather_kernel(table_hbm, idx_hbm, out_hbm, idx_a, idx_b, rows_a, rows_b, sem_a, sem_b):
        worker = lax.axis_index("s") * info.num_cores + lax.axis_index("c")
        base = worker * per_worker
        bufs = ((idx_a, rows_a, sem_a), (idx_b, rows_b, sem_b))

        def start(w, buf):
            idx_v, rows_v, sem = buf
            pltpu.sync_copy(idx_hbm.at[pl.ds(base + w * SC_WINDOW, SC_WINDOW)], idx_v)
            pltpu.async_copy(table_hbm.at[idx_v], rows_v, sem)

        def finish(w, buf):
            idx_v, rows_v, sem = buf
            pltpu.make_async_copy(table_hbm.at[idx_v], rows_v, sem).wait()
            pltpu.sync_copy(rows_v, out_hbm.at[pl.ds(base + w * SC_WINDOW, SC_WINDOW)])

        start(0, bufs[0])

        @pl.loop(0, n_win, step=2)
        def _(w):
            start(w + 1, bufs[1])
            finish(w, bufs[0])

            @pl.when(w + 2 < n_win)
            def _():
                start(w + 2, bufs[0])
            finish(w + 1, bufs[1])

    return gather_kernel(table, idx)


SC_LANES = 16


def _sc_invert_slots(dest, n_slots):
    n_asg = dest.shape[0]
    assert n_asg & (n_asg - 1) == 0 and n_asg % SC_LANES == 0 and n_slots % SC_LANES == 0
    info = plsc.get_sparse_core_info()
    mesh = plsc.VectorSubcoreMesh(core_axis_name="c", subcore_axis_name="s")

    @functools.partial(
        pl.kernel, mesh=mesh,
        out_type=jax.ShapeDtypeStruct((n_slots,), jnp.int32),
        scratch_types=[pltpu.VMEM((n_asg,), jnp.int32), pltpu.VMEM((n_slots,), jnp.int32)],
        compiler_params=pltpu.CompilerParams(needs_layout_passes=False),
        name="sc_invert_slots",
    )
    def invert_kernel(dest_hbm, out_hbm, dest_v, out_v):
        worker = lax.axis_index("s") * info.num_cores + lax.axis_index("c")

        @pl.when(worker == 0)
        def _():
            pltpu.sync_copy(dest_hbm, dest_v)
            lane = lax.iota(jnp.int32, SC_LANES)

            @pl.loop(0, n_slots // SC_LANES)
            def _(c):
                out_v[pl.ds(c * SC_LANES, SC_LANES)] = jnp.bitwise_and(c * SC_LANES + lane, n_asg - 1)

            @pl.loop(0, n_asg // SC_LANES)
            def _(c):
                plsc.store_scatter(out_v, [dest_v[pl.ds(c * SC_LANES, SC_LANES)]], c * SC_LANES + lane)

            pltpu.sync_copy(out_v, out_hbm)

    return invert_kernel(dest)


def _moe_rows_kernel(te_ref, nv_ref, x_ref, wg_ref, wu_ref, wd_ref, o_ref, wgb_ref, wub_ref, wdb_ref):
    i = pl.program_id(0)
    e = te_ref[i]
    e_prev = te_ref[jnp.maximum(i - 1, 0)]

    @pl.when((i == 0) | (e != e_prev))
    def _():
        wgb_ref[...] = wg_ref[0].astype(BF16)
        wub_ref[...] = wu_ref[0].astype(BF16)
        wdb_ref[...] = wd_ref[0].astype(BF16)

    @pl.when(i < nv_ref[0])
    def _():
        xb = jnp.concatenate(_load_token_slabs(x_ref, MOE_TILE), axis=1).astype(BF16)
        hg = _dot(xb, wgb_ref[...])
        hu = _dot(xb, wub_ref[...])
        mid = (hg * jax.nn.sigmoid(hg) * hu).astype(BF16)
        _store_token_tiles(o_ref, _dot(mid, wdb_ref[...]))

    @pl.when(i >= nv_ref[0])
    def _():
        o_ref[...] = jnp.zeros_like(o_ref)


def _moe_experts_rows(xbuf, tile_expert, n_valid, w_gate, w_up, w_down):
    D = D_MODEL
    n_tiles = xbuf.shape[0] // (MOE_TILE * SLABS)
    tiles = pl.BlockSpec((MOE_TILE * SLABS, LANE), lambda i, te, nv: (jnp.minimum(i, nv[0] - 1), 0))
    grid_spec = pltpu.PrefetchScalarGridSpec(
        num_scalar_prefetch=2,
        grid=(n_tiles,),
        in_specs=[tiles,
                  pl.BlockSpec((1, D, D_EXPERT), lambda i, te, nv: (te[i], 0, 0)),
                  pl.BlockSpec((1, D, D_EXPERT), lambda i, te, nv: (te[i], 0, 0)),
                  pl.BlockSpec((1, D_EXPERT, D), lambda i, te, nv: (te[i], 0, 0))],
        out_specs=pl.BlockSpec((MOE_TILE * SLABS, LANE), lambda i, te, nv: (i, 0)),
        scratch_shapes=[pltpu.VMEM((D, D_EXPERT), BF16), pltpu.VMEM((D, D_EXPERT), BF16),
                        pltpu.VMEM((D_EXPERT, D), BF16)],
    )
    return pl.pallas_call(
        _moe_rows_kernel,
        out_shape=jax.ShapeDtypeStruct((n_tiles * MOE_TILE * SLABS, LANE), F32),
        grid_spec=grid_spec,
        compiler_params=_cparams(("arbitrary",)),
        name="moe_experts",
    )(tile_expert, n_valid, xbuf, w_gate, w_up, w_down)


def _row_copy(src_hbm, src_row, dst_ref, dst_row, sem):
    return pltpu.make_async_copy(src_hbm.at[pl.ds(src_row * SLABS, SLABS), :],
                                 dst_ref.at[pl.ds(dst_row * SLABS, SLABS), :], sem)


def _wait_tiles(src_hbm, dst_ref, sem):
    pltpu.make_async_copy(src_hbm.at[pl.ds(0, dst_ref.shape[0]), :], dst_ref, sem).wait()


def _moe_kernel(te_ref, nv_ref, ra_cur_ref, ra_nxt_ref, h_hbm, wg_ref, wu_ref, wd_ref, o_ref,
                xbuf, sem, wgb_ref, wub_ref, wdb_ref):
    i = pl.program_id(0)
    n_valid = nv_ref[0]
    slot = lax.rem(i, 2)

    def start_gather(ra_ref, s):
        def body(jj, carry):
            for u in range(DMA_UNROLL):
                j = jj * DMA_UNROLL + u
                tok = lax.shift_right_logical(ra_ref[j], 1)
                _row_copy(h_hbm, tok, xbuf.at[s], j, sem.at[s]).start(priority=u % 2)
            return carry
        lax.fori_loop(0, MOE_TILE // DMA_UNROLL, body, 0)

    @pl.when(i == 0)
    def _():
        start_gather(ra_cur_ref, 0)

    @pl.when(i + 1 < n_valid)
    def _():
        start_gather(ra_nxt_ref, 1 - slot)

    e = te_ref[i]
    e_prev = te_ref[jnp.maximum(i - 1, 0)]

    @pl.when((i == 0) | (e != e_prev))
    def _():
        wgb_ref[...] = wg_ref[0].astype(BF16)
        wub_ref[...] = wu_ref[0].astype(BF16)
        wdb_ref[...] = wd_ref[0].astype(BF16)

    @pl.when(i < n_valid)
    def _():
        _wait_tiles(h_hbm, xbuf.at[slot], sem.at[slot])
        xb = jnp.concatenate(_load_token_slabs(xbuf.at[slot], MOE_TILE), axis=1).astype(BF16)
        hg = _dot(xb, wgb_ref[...])
        hu = _dot(xb, wub_ref[...])
        mid = (hg * jax.nn.sigmoid(hg) * hu).astype(BF16)
        _store_token_tiles(o_ref, _dot(mid, wdb_ref[...]))

    @pl.when(i >= n_valid)
    def _():
        o_ref[...] = jnp.zeros_like(o_ref)


def _moe_experts(h1, row_asg, tile_expert, n_valid, w_gate, w_up, w_down):
    D = D_MODEL
    n_tiles = row_asg.shape[0] // MOE_TILE
    smem_tile = lambda f: pl.BlockSpec((MOE_TILE,), f, memory_space=pltpu.SMEM)
    grid_spec = pltpu.PrefetchScalarGridSpec(
        num_scalar_prefetch=2,
        grid=(n_tiles,),
        in_specs=[smem_tile(lambda i, te, nv: (i,)),
                  smem_tile(lambda i, te, nv: (jnp.minimum(i + 1, n_tiles - 1),)),
                  pl.BlockSpec(memory_space=pl.ANY),
                  pl.BlockSpec((1, D, D_EXPERT), lambda i, te, nv: (te[i], 0, 0)),
                  pl.BlockSpec((1, D, D_EXPERT), lambda i, te, nv: (te[i], 0, 0)),
                  pl.BlockSpec((1, D_EXPERT, D), lambda i, te, nv: (te[i], 0, 0))],
        out_specs=pl.BlockSpec((MOE_TILE * SLABS, LANE), lambda i, te, nv: (i, 0)),
        scratch_shapes=[pltpu.VMEM((2, MOE_TILE * SLABS, LANE), F32), pltpu.SemaphoreType.DMA((2,)),
                        pltpu.VMEM((D, D_EXPERT), BF16), pltpu.VMEM((D, D_EXPERT), BF16),
                        pltpu.VMEM((D_EXPERT, D), BF16)],
    )
    return pl.pallas_call(
        _moe_kernel,
        out_shape=jax.ShapeDtypeStruct((n_tiles * MOE_TILE * SLABS, LANE), F32),
        grid_spec=grid_spec,
        compiler_params=_cparams(("arbitrary",)),
        name="moe_experts",
    )(tile_expert, n_valid, row_asg, row_asg, h1, w_gate, w_up, w_down)


def _combine_kernel(d_cur_ref, d_nxt_ref, h_ref, gate_ref, g_ref, b_ref, y_hbm, o_ref, ybuf, sem):
    i = pl.program_id(0)
    n = pl.num_programs(0)
    slot = lax.rem(i, 2)

    def start_gather(d_ref, s):
        def body(tt, carry):
            for u in range(DMA_UNROLL // TOP_K):
                t = tt * (DMA_UNROLL // TOP_K) + u
                for k in range(TOP_K):
                    _row_copy(y_hbm, d_ref[TOP_K * t + k], ybuf.at[s, k], t, sem.at[s]).start(priority=k % 2)
            return carry
        lax.fori_loop(0, COMBINE_TILE * TOP_K // DMA_UNROLL, body, 0)

    @pl.when(i == 0)
    def _():
        start_gather(d_cur_ref, 0)

    @pl.when(i + 1 < n)
    def _():
        start_gather(d_nxt_ref, 1 - slot)

    for k in range(TOP_K):
        _wait_tiles(y_hbm, ybuf.at[slot, k], sem.at[slot])

    gate = gate_ref[...]
    tm = COMBINE_TILE
    ga = jnp.broadcast_to(gate[:, 0:1], (tm, LANE))
    gb = jnp.broadcast_to(gate[:, 1:2], (tm, LANE))
    hs = _load_token_slabs(h_ref, tm)
    ya = _load_token_slabs(ybuf.at[slot, 0], tm)
    yb = _load_token_slabs(ybuf.at[slot, 1], tm)
    z = [DEEPNORM_ALPHA * h + (ga * a + gb * b) for h, a, b in zip(hs, ya, yb)]
    inv_d = 1.0 / D_MODEL
    mu = sum(jnp.sum(t, axis=1, keepdims=True) for t in z) * inv_d
    zc = [t - mu for t in z]
    var = sum(jnp.sum(t * t, axis=1, keepdims=True) for t in zc) * inv_d
    rstd = lax.rsqrt(var + LN_EPS)
    for s in range(SLABS):
        cols = slice(s * LANE, (s + 1) * LANE)
        o_ref[:, cols] = zc[s] * rstd * g_ref[:, cols] + b_ref[:, cols]


def _combine(h1, ybuf, dest, gates, ln2_g, ln2_b):
    D = D_MODEL
    M = h1.shape[0] // SLABS
    tm = COMBINE_TILE
    n = M // tm
    smem_tile = lambda f: pl.BlockSpec((TOP_K * tm,), f, memory_space=pltpu.SMEM)
    return pl.pallas_call(
        _combine_kernel,
        out_shape=jax.ShapeDtypeStruct((M, D), F32),
        grid=(n,),
        in_specs=[smem_tile(lambda i: (i,)), smem_tile(lambda i: (jnp.minimum(i + 1, n - 1),)),
                  pl.BlockSpec((tm * SLABS, LANE), lambda i: (i, 0)), pl.BlockSpec((tm, TOP_K), lambda i: (i, 0)),
                  _full(ln2_g), _full(ln2_b), pl.BlockSpec(memory_space=pl.ANY)],
        out_specs=pl.BlockSpec((tm, D), lambda i: (i, 0)),
        scratch_shapes=[pltpu.VMEM((2, TOP_K, tm * SLABS, LANE), F32), pltpu.SemaphoreType.DMA((2,))],
        compiler_params=_cparams(("arbitrary",)),
        name="combine",
    )(dest, dest, h1, gates, ln2_g, ln2_b, ybuf)


def _routing_plan(route):
    M = route.shape[0]
    eid = route[:, :TOP_K].astype(jnp.int32).reshape(-1)
    gates = route[:, TOP_K:2 * TOP_K]
    A = M * TOP_K
    onehot = (eid[:, None] == jnp.arange(N_EXPERTS, dtype=eid.dtype)[None, :]).astype(jnp.int32)
    csum = jnp.cumsum(onehot, axis=0)
    rank = jnp.sum(csum * onehot, axis=1) - 1
    counts = csum[-1]
    pcounts = (counts + MOE_TILE - 1) // MOE_TILE * MOE_TILE
    pends = jnp.cumsum(pcounts)
    pstarts = pends - pcounts
    dest = (jnp.sum(onehot * pstarts[None, :], axis=1) + rank).astype(jnp.int32)
    n_tiles = (A + N_EXPERTS * (MOE_TILE - 1) + MOE_TILE - 1) // MOE_TILE
    n_valid = (pends[-1] // MOE_TILE).astype(jnp.int32)
    tile_start = jnp.minimum(jnp.arange(n_tiles, dtype=jnp.int32) * MOE_TILE, pends[-1] - 1)
    tile_expert = jnp.sum((pends[None, :] <= tile_start[:, None]).astype(jnp.int32), axis=1)
    tile_expert = jnp.minimum(tile_expert, N_EXPERTS - 1).astype(jnp.int32)
    pad_lo = (pstarts + counts).astype(jnp.int32)
    pad_hi = pends.astype(jnp.int32)
    return gates, dest, n_tiles * MOE_TILE, tile_expert, n_valid.reshape(1), pad_lo, pad_hi


def kernel(x, meta, ln0_g, ln0_b, w_in, mu_shift, w0, w_decay_up, a0, w_a_up, w_g_up, k_k, k_a, r_k, gn_g, gn_b, conv_w, conv_b, w_rg, b_rg, w_ig, b_ig, lru_lambda, w_out, ln1_g, ln1_b, w_router_grp, b_router_grp, w_router_exp, b_router_exp, w_exp_gate, w_exp_up, w_exp_down, ln2_g, ln2_b):
    B, T, D = x.shape
    assert D == D_MODEL and T % 512 == 0 and w_in.shape[0] == 1
    assert (B * T * TOP_K) % INVERT_BLOCK == 0
    row = lambda p: p.reshape(1, -1).astype(F32)
    n_rw = 3 * RWKV_W
    w_in0 = w_in[0]

    def slots(p):
        pad = lambda a, n: jnp.pad(a, [(0, 0)] * (a.ndim - 1) + [(0, n - a.shape[-1])])
        zw = p[..., n_rw:n_rw + DECAY_RANK]
        za = p[..., n_rw + DECAY_RANK:n_rw + DECAY_RANK + AAA_RANK]
        zg = p[..., n_rw + DECAY_RANK + AAA_RANK:n_rw + DECAY_RANK + AAA_RANK + GATE_RANK]
        return jnp.concatenate([p[..., :n_rw], pad(zw, LANE), pad(za, LANE), pad(zg, ZG_SLOT)], axis=-1)

    rwkv_cols = n_rw + DECAY_RANK + AAA_RANK + GATE_RANK
    w_r = slots(w_in0[:, :rwkv_cols]).astype(BF16)
    w_l = w_in0[:, rwkv_cols:].astype(BF16)
    ur, ul, ur_t, ul_t = _in_projection(x, meta, row(ln0_g), row(ln0_b), w_r, w_l)

    pad_rows = lambda a, n: jnp.pad(a, ((0, n - a.shape[0]), (0, 0)))
    rwkv_params = (slots(mu_shift[0][None, :]).astype(F32), row(w0[0]), pad_rows(w_decay_up[0], LANE).astype(BF16),
                   row(a0[0]), pad_rows(w_a_up[0], LANE).astype(BF16), pad_rows(w_g_up[0], ZG_SLOT).astype(BF16),
                   row(k_k[0]), row(k_a[0]), row(r_k[0]), row(gn_g[0]), row(gn_b[0]))
    y_rwkv = _rwkv_pipe_mixer(ur, ur_t, rwkv_params)

    blockdiag = lambda w: jax.scipy.linalg.block_diag(*[w[i] for i in range(LRU_BLOCKS)]).astype(BF16)
    lru_params = (conv_w[0], row(conv_b[0]), blockdiag(w_rg[0]), row(b_rg[0]), blockdiag(w_ig[0]), row(b_ig[0]),
                  row(lru_lambda[0]))
    y_lru = _lru_mixer(ul, ul_t, lru_params)

    w_rt = jnp.concatenate([w_router_grp[0], w_router_exp[0]], axis=1)
    w_rt = jnp.pad(w_rt, ((0, 0), (0, LANE - w_rt.shape[1])))
    wrt_hi = w_rt.astype(BF16)
    wrt_lo = (w_rt - wrt_hi.astype(F32)).astype(BF16)
    b_rt = jnp.concatenate([b_router_grp[0], b_router_exp[0]])
    b_rt = jnp.pad(b_rt, (0, LANE - b_rt.shape[0])).reshape(1, LANE)
    wo = w_out[0].astype(BF16)
    h1, route = _out_projection(x, y_rwkv, y_lru, row(ln0_g), row(ln0_b), wo[:RWKV_W], wo[RWKV_W:],
                                row(ln1_g[0]), row(ln1_b[0]), wrt_hi, wrt_lo, b_rt)

    M = B * T
    h1 = h1.reshape(M * SLABS, LANE)
    gates, dest, n_slots, tile_expert, n_valid, pad_lo, pad_hi = _routing_plan(route.reshape(M, LANE))
    row_asg = _sc_invert_slots(dest, n_slots)
    src_tok = lax.shift_right_logical(row_asg, 1)
    xbuf = _sc_gather(h1.reshape(M, SLABS, LANE), src_tok).reshape(-1, LANE)
    ybuf = _moe_experts_rows(xbuf, tile_expert, n_valid, w_exp_gate[0], w_exp_up[0], w_exp_down[0])
    out = _combine(h1, ybuf, dest, gates, row(ln2_g[0]), row(ln2_b[0]))
    return out.reshape(B, T, D)
```

```python
import functools
import math

import jax
import jax.numpy as jnp
from jax import lax
from jax.experimental import pallas as pl
from jax.experimental.pallas import tpu as pltpu
from jax.experimental.pallas import tpu_sc as plsc

F32 = jnp.float32
BF16 = jnp.bfloat16

D_MODEL = 1024
N_META = 16
RWKV_W = 512
RWKV_HEAD = 64
DECAY_RANK = 64
AAA_RANK = 64
GATE_RANK = 160
LRU_W = 512
LRU_BLOCKS = 8
CONV_WIDTH = 4
LRU_C = 8.0
N_GROUPS = 4
EXPERTS_PER_GROUP = 8
N_EXPERTS = N_GROUPS * EXPERTS_PER_GROUP
TOP_K = 2
D_EXPERT = 512
LN_EPS = 1e-5
GN_EPS = 64e-5
DEEPNORM_ALPHA = 2.0 ** 0.25

LANE = 128
OFF_R, OFF_K, OFF_V = 0, RWKV_W, 2 * RWKV_W
OFF_ZW = 3 * RWKV_W
OFF_ZA = OFF_ZW + LANE
OFF_ZG = OFF_ZA + LANE
ZG_SLOT = 2 * LANE
UR_W = OFF_ZG + ZG_SLOT
UL_W = 2 * LRU_W

TAIL = 256
CHUNK = 64
HEADS_PER_GROUP = 4
GW = HEADS_PER_GROUP * RWKV_HEAD
N_HGROUPS = RWKV_W // GW
LRU_TILE = TAIL
MOE_TILE = 256
COMBINE_TILE = 256
INVERT_BLOCK = 4096
DMA_UNROLL = 8
V7X_VMEM_BYTES = 64 * 1024 * 1024
VMEM_LIMIT = V7X_VMEM_BYTES - 8 * 1024 * 1024


def _cparams(sem):
    return pltpu.CompilerParams(dimension_semantics=sem, vmem_limit_bytes=VMEM_LIMIT)


def _layer_norm(x, g, b):
    mu = jnp.mean(x, -1, keepdims=True)
    xc = x - mu
    var = jnp.mean(xc * xc, -1, keepdims=True)
    return xc * lax.rsqrt(var + LN_EPS) * g + b


def _dot(a, b):
    return jnp.dot(a, b, preferred_element_type=F32)


def _dot_nt(a, b):
    return lax.dot_general(a, b, (((1,), (1,)), ((), ())), preferred_element_type=F32)


def _dot_tn(a, b):
    return lax.dot_general(a, b, (((0,), (0,)), ((), ())), preferred_element_type=F32)


def _full(a):
    return pl.BlockSpec(a.shape, lambda *_: (0,) * a.ndim)


def _inproj_kernel(x_ref, g_ref, b_ref, wr_ref, wl_ref, ur_ref, ul_ref):
    h = _layer_norm(x_ref[0], g_ref[...], b_ref[...]).astype(BF16)
    ur_ref[0] = _dot(h, wr_ref[...])
    ul_ref[0] = _dot(h, wl_ref[...])


def _inproj_tail_kernel(x_ref, g_ref, b_ref, wr_ref, wl_ref, ur_ref, ul_ref):
    h = _layer_norm(x_ref[...], g_ref[...], b_ref[...]).astype(BF16)
    rows = lax.broadcasted_iota(jnp.int32, (TAIL, 1), 0)
    valid = (rows >= TAIL - N_META).astype(F32)
    ur_ref[...] = _dot(h, wr_ref[...]) * valid
    ul_ref[...] = _dot(h, wl_ref[...]) * valid


def _in_projection(x, meta, ln0_g, ln0_b, w_r, w_l):
    B, T, D = x.shape
    tm = 512
    ur, ul = pl.pallas_call(
        _inproj_kernel,
        out_shape=(jax.ShapeDtypeStruct((B, T, UR_W), F32), jax.ShapeDtypeStruct((B, T, UL_W), F32)),
        grid=(B, T // tm),
        in_specs=[pl.BlockSpec((1, tm, D), lambda b, i: (b, i, 0)), _full(ln0_g), _full(ln0_b), _full(w_r), _full(w_l)],
        out_specs=(pl.BlockSpec((1, tm, UR_W), lambda b, i: (b, i, 0)),
                   pl.BlockSpec((1, tm, UL_W), lambda b, i: (b, i, 0))),
        compiler_params=_cparams(("parallel", "parallel")),
        name="inproj",
    )(x, ln0_g, ln0_b, w_r, w_l)
    tail_x = jnp.concatenate([jnp.zeros((TAIL - N_META, D), F32), meta.astype(F32)], axis=0)
    ur_t, ul_t = pl.pallas_call(
        _inproj_tail_kernel,
        out_shape=(jax.ShapeDtypeStruct((TAIL, UR_W), F32), jax.ShapeDtypeStruct((TAIL, UL_W), F32)),
        grid=(1,),
        in_specs=[_full(tail_x), _full(ln0_g), _full(ln0_b), _full(w_r), _full(w_l)],
        out_specs=(pl.BlockSpec((TAIL, UR_W), lambda i: (0, 0)), pl.BlockSpec((TAIL, UL_W), lambda i: (0, 0))),
        compiler_params=_cparams(("arbitrary",)),
        name="inproj_tail",
    )(tail_x, ln0_g, ln0_b, w_r, w_l)
    return ur, ul, ur_t, ul_t


def _rwkv_pipe_kernel(u_ref, ut_ref, mu_ref, w0_ref, wdu_ref, a0_ref, wau_ref, wgu_ref, kk_ref, ka_ref, rk_ref,
                      gng_ref, gnb_ref, bones_ref, bm_ref, eye_ref, msl_ref, mil_ref,
                      m8_ref, m16_ref, m32_ref, m64_ref, y_ref,
                      s_ref, prev_ref, y0_s, q_s, mc_s, nc_s, we_s, bonus_s, g_s, yraw_s):
    s_id = pl.program_id(0)
    nb = u_ref.shape[0]
    blk = u_ref.shape[1]
    npc = blk // CHUNK
    nseq = nb * npc
    seq_rows = lambda q: slice(q * CHUNK, (q + 1) * CHUNK)
    per_seq = lambda f: jnp.concatenate([f(q) for q in range(nseq)], axis=0)
    w_slot = lax.rem(s_id, 2)
    r_slot = 1 - w_slot

    @pl.when(s_id == 0)
    def _():
        s_ref[...] = jnp.zeros_like(s_ref)
        prev_ref[...] = jnp.zeros_like(prev_ref)
        for ref in (y0_s, q_s, mc_s, nc_s, we_s, bonus_s, g_s):
            ref[1] = jnp.zeros(ref.shape[1:], ref.dtype)

    b16 = lambda t: t.astype(BF16)
    bones = bones_ref[...]
    head_sum = lambda t: _dot(b16(t), bones)
    bm = bm_ref[...]
    bm16 = b16(bm)
    tile4 = lambda t: jnp.concatenate([t] * HEADS_PER_GROUP, axis=0)
    fold4 = lambda t: sum(t[i * CHUNK:(i + 1) * CHUNK] for i in range(HEADS_PER_GROUP))
    bd = lambda t: tile4(b16(t)) * bm16

    chains = [(b, hg) for b in range(nb) for hg in range(N_HGROUPS)]
    states = {c: s_ref[c[0], c[1]] for c in chains}

    def recurrent_chunk(j):
        for b, hg in chains:
            sl = slice(hg * GW, (hg + 1) * GW)
            q = b * npc + j
            rq = seq_rows(q)
            s = states[(b, hg)]
            yraw_s[rq, sl] = y0_s[r_slot, rq, sl] + _dot_nt(q_s[r_slot, rq, sl], bd(s))
            states[(b, hg)] = (s * we_s[r_slot, q * 8:q * 8 + 1, sl] + _dot(b16(s), bd(mc_s[r_slot, rq, sl]))
                               + nc_s[r_slot, rq, sl])

    prepared = {}

    def prepare(b):
        lo = b * blk
        u = jnp.where(s_id == 0, ut_ref[...], u_ref[b])
        row = lax.broadcasted_iota(jnp.int32, u.shape, 0)
        u_prev = jnp.where(row == 0, prev_ref[b:b + 1, :], pltpu.roll(u, 1, 0))
        prev_ref[b:b + 1, :] = u[blk - 1:blk, :]
        x = u + (u_prev - u) * mu_ref[...]
        r = x[:, OFF_R:OFF_R + RWKV_W]
        k = x[:, OFF_K:OFF_K + RWKV_W]
        v = x[:, OFF_V:OFF_V + RWKV_W]
        zw = x[:, OFF_ZW:OFF_ZW + LANE]
        za = x[:, OFF_ZA:OFF_ZA + LANE]
        zg = x[:, OFF_ZG:OFF_ZG + ZG_SLOT]
        yield
        z = w0_ref[...] + _dot(b16(jnp.tanh(zw)), wdu_ref[...])
        logw = -math.exp(-0.5) * jax.nn.sigmoid(z)
        a = jax.nn.sigmoid(a0_ref[...] + _dot(b16(za), wau_ref[...]))
        g = _dot(b16(jax.nn.sigmoid(zg)), wgu_ref[...])
        kk = k * kk_ref[...]
        kk = kk / jnp.maximum(jnp.sqrt(head_sum(kk * kk)), 1e-12)
        k = k * (1.0 + (a - 1.0) * ka_ref[...])
        kka = kk * a
        bonus_s[w_slot, lo:lo + blk, :] = head_sum(r * k * rk_ref[...]) * v
        g_s[w_slot, lo:lo + blk, :] = g
        yield
        cl = logw
        row_in_chunk = jnp.bitwise_and(lax.broadcasted_iota(jnp.int32, cl.shape, 0), CHUNK - 1)
        d = 1
        while d < CHUNK:
            cl = cl + jnp.where(row_in_chunk >= d, pltpu.roll(cl, d, 0), 0.0)
            d *= 2
        yield
        cl_last = jnp.concatenate(
            [jnp.broadcast_to(cl[(j + 1) * CHUNK - 1:(j + 1) * CHUNK, :], (CHUNK, RWKV_W)) for j in range(npc)], axis=0)
        e_neg = jnp.exp(-cl)
        e_end = jnp.exp(cl_last - cl)
        w_end = jnp.exp(cl_last)
        for j in range(npc):
            q = b * npc + j
            we_s[w_slot, q * 8:(q + 1) * 8, :] = w_end[j * CHUNK:j * CHUNK + 8, :]
        prepared[b] = dict(rt=r * jnp.exp(cl), kt=k * e_neg, at=-kk * jnp.exp(cl - logw), bt=kka * e_neg,
                           kw=k * e_end, bw=kka * e_end, v=v)
        yield

    eye = eye_ref[...]
    msl = msl_ref[...]
    mil = mil_ref[...]
    rows2 = lambda x, y: jnp.concatenate([x, y], axis=0)
    cols2 = lambda x, y: jnp.concatenate([x, y], axis=1)
    each = lambda f, *ls: [f(*xs) for xs in zip(*ls)]

    def solve(b, p):
        probs = [(slice(j * CHUNK, (j + 1) * CHUNK), slice(hg * GW, (hg + 1) * GW))
                 for j in range(npc) for hg in range(N_HGROUPS)]
        pick = lambda t: [t[rq, sl] for rq, sl in probs]
        at_w, rt_w, v_w = pick(p["at"]), pick(p["rt"]), pick(p["v"])
        lhs = each(lambda x, y: b16(rows2(x, y)), at_w, rt_w)
        ab = each(_dot_nt, lhs, each(bd, pick(p["bt"])))
        ak = each(_dot_nt, lhs, each(bd, pick(p["kt"])))
        yield
        a_ab = each(lambda t: t[:CHUNK] * msl, ab)
        a_rb = each(lambda t: b16(t[CHUNK:] * mil), ab)
        a_xk = each(lambda t: b16(rows2(t[:CHUNK] * msl, t[CHUNK:] * mil)), ak)
        a0 = each(lambda t: b16(t * m8_ref[...]), a_ab)
        a2 = each(lambda t: b16(_dot(t, bd(t))), a0)
        yield
        a4 = each(lambda t: b16(_dot(t, bd(t))), a2)
        p1 = each(lambda t: eye + t.astype(F32), a0)
        p1 = each(lambda q, t: q + _dot(b16(q), bd(t)), p1, a2)
        yield
        tt = each(lambda q, t: q + _dot(b16(q), bd(t)), p1, a4)
        yield
        for m_ref in (m16_ref, m32_ref, m64_ref):
            tb = each(b16, tt)
            off = each(lambda t: b16(t * m_ref[...]), a_ab)
            half = each(lambda x, y: b16(_dot(x, bd(y))), tb, off)
            yield
            tt = each(lambda t, x, y: t + _dot(x, bd(y)), tt, half, tb)
            yield
        tb = each(b16, tt)
        xv = each(lambda x, y: _dot(x, bd(y)), a_xk, v_w)
        yield
        u0 = each(lambda x, y: _dot(x, bd(y[:CHUNK])), tb, xv)
        ta = each(lambda x, y: _dot(x, bd(y)), tb, at_w)
        yield
        y0 = each(lambda x, y, z: _dot(x, bd(y)) + z[CHUNK:], a_rb, u0, xv)
        qq = each(lambda x, y, z: x + _dot(y, bd(z)), rt_w, a_rb, ta)
        yield
        left = each(lambda x, y, z: b16(rows2(cols2(x, y), cols2(jnp.zeros_like(z), z))), ta, u0, v_w)
        right = each(lambda x, y: b16(rows2(x, y)), pick(p["bw"]), pick(p["kw"]))
        mn = each(_dot_tn, left, right)
        for i, (rq, sl) in enumerate(probs):
            rows = slice(b * blk + rq.start, b * blk + rq.stop)
            y0_s[w_slot, rows, sl] = y0[i]
            q_s[w_slot, rows, sl] = b16(qq[i])
            mc_s[w_slot, rows, sl] = b16(fold4(mn[i][:GW] * bm))
            nc_s[w_slot, rows, sl] = fold4(mn[i][GW:] * bm)
        yield

    first = prepare(0)
    for j in range(npc):
        recurrent_chunk(j)
        if j * 4 // npc != (j + 1) * 4 // npc or j == npc - 1:
            for _ in range((j + 1) * 4 // npc - j * 4 // npc):
                next(first, None)
    for _ in first:
        pass
    for (b, hg), s in states.items():
        s_ref[b, hg] = s

    y = yraw_s[...]
    inv_n = 1.0 / RWKV_HEAD
    ym = head_sum(y) * inv_n
    yc = y - ym
    yv = head_sum(yc * yc) * inv_n
    yn = yc * lax.rsqrt(yv + GN_EPS) * gng_ref[...] + gnb_ref[...]
    y_ref[...] = ((yn + bonus_s[r_slot]) * g_s[r_slot]).astype(y_ref.dtype).reshape(y_ref.shape)

    for b in range(nb):
        solver = solve(b, prepared[b])
        nxt = prepare(b + 1) if b + 1 < nb else iter(())
        for level, _ in enumerate(solver):
            if level % 4 == 3:
                next(nxt, None)
        for _ in nxt:
            pass


def _rwkv_masks():
    f = lambda m: m.astype(F32)
    i = jnp.arange(GW)[:, None]
    j = jnp.arange(GW)[None, :]
    bm = f((i // RWKV_HEAD) == (j // RWKV_HEAD))
    t = jnp.arange(CHUNK)[:, None]
    s = (jnp.arange(GW) % CHUNK)[None, :]
    same = lambda n: (t // n) == (s // n)
    msl = f(t > s)
    mil = f(t >= s)
    m8 = f(same(8))
    m16 = f(same(16) & ~same(8))
    m32 = f(same(32) & ~same(16))
    m64 = f(~same(32))
    eye = f(t == s)
    hi = jnp.arange(RWKV_W)
    bones = ((hi[:, None] // RWKV_HEAD) == (hi[None, :] // RWKV_HEAD)).astype(BF16)
    return bones, bm, eye, msl, mil, m8, m16, m32, m64


def _rwkv_pipe_mixer(ur, ur_tail, params):
    B, T, _ = ur.shape
    blk = TAIL
    assert T % blk == 0 and blk % CHUNK == 0 and CHUNK == RWKV_HEAD
    n_blocks = T // blk
    rows = B * blk
    consts = _rwkv_masks()
    in_map = lambda s: (0, jnp.clip(s - 1, 0, n_blocks - 1), 0)
    out_map = lambda s: (0, jnp.clip(s - 2, 0, n_blocks - 1), 0)
    slot2 = lambda w, dt: pltpu.VMEM((2, rows, w), dt)
    return pl.pallas_call(
        _rwkv_pipe_kernel,
        out_shape=jax.ShapeDtypeStruct((B, T, RWKV_W), BF16),
        grid=(n_blocks + 2,),
        in_specs=[pl.BlockSpec((B, blk, UR_W), in_map), _full(ur_tail)]
                 + [_full(p) for p in params] + [_full(m) for m in consts],
        out_specs=pl.BlockSpec((B, blk, RWKV_W), out_map),
        scratch_shapes=[pltpu.VMEM((B, N_HGROUPS, CHUNK, GW), F32), pltpu.VMEM((B, UR_W), F32),
                        slot2(RWKV_W, F32), slot2(RWKV_W, BF16), slot2(RWKV_W, BF16), slot2(RWKV_W, F32),
                        pltpu.VMEM((2, 8 * rows // CHUNK, RWKV_W), F32), slot2(RWKV_W, F32), slot2(RWKV_W, F32),
                        pltpu.VMEM((rows, RWKV_W), F32)],
        compiler_params=_cparams(("arbitrary",)),
        name="rwkv7",
    )(ur, ur_tail, *params, *consts)


def _gelu_tanh(x):
    return 0.5 * x * (1.0 + jnp.tanh(math.sqrt(2.0 / math.pi) * (x + 0.044715 * (x * x * x))))


LRU_CARRY = 8


def _lru_kernel(u_ref, ut_ref, cw_ref, cb_ref, wrg_ref, brg_ref, wig_ref, big_ref, lam_ref, y_ref,
                xs_ref, hprev_ref):
    c = pl.program_id(0)
    nb = u_ref.shape[0]
    nrow = nb * LRU_TILE

    @pl.when(c == 0)
    def _():
        xs_ref[...] = jnp.zeros_like(xs_ref)
        hprev_ref[...] = jnp.zeros_like(hprev_ref)

    u_x = u_ref[...].reshape(nrow, UL_W)
    u = jnp.where(c == 0, jnp.concatenate([ut_ref[...]] * nb, axis=0), u_x)
    xl = u[:, :LRU_W]
    gl = u[:, LRU_W:]
    row = jnp.bitwise_and(lax.broadcasted_iota(jnp.int32, (nrow, LRU_W), 0), LRU_TILE - 1)
    in_group = jnp.bitwise_and(row, 7)
    roll_in_group = lambda t, d: pltpu.roll(t.reshape(t.shape[0] // 8, 8, LRU_W), d, 1).reshape(t.shape)
    xl_prev = jnp.concatenate(
        [p for b in range(nb) for p in (xs_ref[b], xl[b * LRU_TILE:(b + 1) * LRU_TILE - 8])], axis=0)
    xc = cb_ref[...] + cw_ref[CONV_WIDTH - 1:CONV_WIDTH, :] * xl
    for d in range(1, CONV_WIDTH):
        tap = jnp.where(in_group >= d, roll_in_group(xl, d), roll_in_group(xl_prev, d))
        xc = xc + cw_ref[CONV_WIDTH - 1 - d:CONV_WIDTH - d, :] * tap
    for b in range(nb):
        xs_ref[b] = xl[(b + 1) * LRU_TILE - 8:(b + 1) * LRU_TILE]

    xcb = xc.astype(BF16)
    gate_r = jax.nn.sigmoid(_dot(xcb, wrg_ref[...]) + brg_ref[...])
    gate_i = jax.nn.sigmoid(_dot(xcb, wig_ref[...]) + big_ref[...])
    lam = lam_ref[...]
    log_sig = -(jnp.maximum(-lam, 0.0) + jnp.log1p(jnp.exp(-jnp.abs(lam))))
    log_a = LRU_C * gate_r * log_sig
    a = jnp.exp(log_a)
    mult = jnp.sqrt(jnp.maximum(1.0 - jnp.exp(2.0 * log_a), 0.0))
    b = mult * gate_i * xc
    b = jnp.where((c == 0) & (row < LRU_TILE - N_META), 0.0, b)

    d = 1
    while d < 8:
        keep = in_group >= d
        a_sh = jnp.where(keep, roll_in_group(a, d), 1.0)
        b_sh = jnp.where(keep, roll_in_group(b, d), 0.0)
        b = a * b_sh + b
        a = a * a_sh
        d *= 2
    groups = []
    for bi in range(nb):
        carry = hprev_ref[bi:bi + 1, :]
        for gi in range(LRU_TILE // 8):
            lo = bi * LRU_TILE + gi * 8
            hg = b[lo:lo + 8] + a[lo:lo + 8] * carry
            carry = hg[7:8, :]
            groups.append(hg)
        hprev_ref[bi:bi + 1, :] = carry
    h = jnp.concatenate(groups, axis=0)
    y_ref[...] = (h * _gelu_tanh(gl)).astype(y_ref.dtype).reshape(y_ref.shape)


def _lru_mixer(ul, ul_tail, params):
    B, T, _ = ul.shape
    assert TAIL == LRU_TILE
    x_map = lambda c: (0, jnp.maximum(c - 1, 0), 0)
    return pl.pallas_call(
        _lru_kernel,
        out_shape=jax.ShapeDtypeStruct((B, T, LRU_W), BF16),
        grid=(T // LRU_TILE + 1,),
        in_specs=[pl.BlockSpec((B, LRU_TILE, UL_W), x_map), _full(ul_tail)] + [_full(p) for p in params],
        out_specs=pl.BlockSpec((B, LRU_TILE, LRU_W), x_map),
        scratch_shapes=[pltpu.VMEM((B, LRU_CARRY, LRU_W), F32), pltpu.VMEM((B, LRU_W), F32)],
        compiler_params=_cparams(("arbitrary",)),
        name="rglru",
    )(ul, ul_tail, *params)


def _route(lg):
    lane = lax.broadcasted_iota(jnp.int32, lg.shape, 1)
    neg = jnp.float32(-jnp.inf)
    rmax = lambda t: jnp.max(t, axis=1, keepdims=True)
    first = lambda hit: jnp.min(jnp.where(hit, lane, LANE), axis=1, keepdims=True)
    is_grp = lane < N_GROUPS
    gl = jnp.where(is_grp, lg, neg)
    gmax = rmax(gl)
    g_sel = first(gl == gmax)
    p_g = 1.0 / jnp.sum(jnp.where(is_grp, jnp.exp(lg - gmax), 0.0), axis=1, keepdims=True)
    ex = lane - N_GROUPS
    in_grp = (ex >= 0) & (ex < N_EXPERTS) & (jnp.right_shift(ex, 3) == g_sel)
    el = jnp.where(in_grp, lg, neg)
    v1 = rmax(el)
    i1 = first(el == v1)
    el2 = jnp.where(lane == i1, neg, el)
    v2 = rmax(el2)
    i2 = first(el2 == v2)
    t = jnp.exp(v2 - v1)
    gate1 = p_g / (1.0 + t)
    gate2 = p_g * t / (1.0 + t)
    e1 = (i1 - N_GROUPS).astype(F32)
    e2 = (i2 - N_GROUPS).astype(F32)
    return jnp.where(lane == 0, e1, jnp.where(lane == 1, e2, jnp.where(lane == 2, gate1, jnp.where(lane == 3, gate2, 0.0))))


SLABS = D_MODEL // LANE


def _store_token_tiles(ref, val):
    n = val.shape[0]
    for s in range(SLABS):
        ref[pl.ds(s, n, stride=SLABS), :] = val[:, s * LANE:(s + 1) * LANE]


def _load_token_slabs(ref, n):
    return [ref[pl.ds(s, n, stride=SLABS), :] for s in range(SLABS)]


def _outproj_kernel(x_ref, yr_ref, yl_ref, g0_ref, b0_ref, wor_ref, wol_ref, g1_ref, b1_ref,
                    wrt_hi_ref, wrt_lo_ref, brt_ref, h1_ref, rt_ref):
    h0 = _layer_norm(x_ref[0], g0_ref[...], b0_ref[...])
    mix = _dot(yr_ref[0], wor_ref[...]) + _dot(yl_ref[0], wol_ref[...])
    h1 = _layer_norm(DEEPNORM_ALPHA * h0 + mix, g1_ref[...], b1_ref[...])
    _store_token_tiles(h1_ref.at[0], h1)
    hi = h1.astype(BF16)
    lo = (h1 - hi.astype(F32)).astype(BF16)
    w_hi = wrt_hi_ref[...]
    lg = _dot(hi, w_hi) + (_dot(hi, wrt_lo_ref[...]) + _dot(lo, w_hi)) + brt_ref[...]
    rt_ref[0] = _route(lg)


def _out_projection(x, y_rwkv, y_lru, ln0_g, ln0_b, wo_r, wo_l, ln1_g, ln1_b, wrt_hi, wrt_lo, brt):
    B, T, D = x.shape
    tm = 512
    rows = lambda w: pl.BlockSpec((1, tm, w), lambda b, i: (b, i, 0))
    return pl.pallas_call(
        _outproj_kernel,
        out_shape=(jax.ShapeDtypeStruct((B, T * SLABS, LANE), F32), jax.ShapeDtypeStruct((B, T, LANE), F32)),
        grid=(B, T // tm),
        in_specs=[rows(D), rows(RWKV_W), rows(LRU_W), _full(ln0_g), _full(ln0_b), _full(wo_r), _full(wo_l),
                  _full(ln1_g), _full(ln1_b), _full(wrt_hi), _full(wrt_lo), _full(brt)],
        out_specs=(pl.BlockSpec((1, tm * SLABS, LANE), lambda b, i: (b, i, 0)), rows(LANE)),
        compiler_params=_cparams(("parallel", "parallel")),
        name="outproj",
    )(x, y_rwkv, y_lru, ln0_g, ln0_b, wo_r, wo_l, ln1_g, ln1_b, wrt_hi, wrt_lo, brt)


def _invert_kernel(pad_lo_ref, pad_hi_ref, dest_ref, out_ref):
    i = pl.program_id(0)

    @pl.when(i == 0)
    def _():
        n_asg = pl.num_programs(0) * INVERT_BLOCK

        def zero(j, carry):
            out_ref[j] = jnp.bitwise_and(j, n_asg - 1)
            return carry
        for e in range(N_EXPERTS):
            lax.fori_loop(pad_lo_ref[e], pad_hi_ref[e], zero, 0)
        lax.fori_loop(pad_hi_ref[N_EXPERTS - 1], out_ref.shape[0], zero, 0)

    base = i * INVERT_BLOCK

    def body(j, carry):
        out_ref[dest_ref[j]] = base + j
        return carry

    lax.fori_loop(0, INVERT_BLOCK, body, 0, unroll=8)


def _invert_slots(dest, n_slots, pad_lo, pad_hi):
    A = dest.shape[0]
    grid_spec = pltpu.PrefetchScalarGridSpec(
        num_scalar_prefetch=2,
        grid=(A // INVERT_BLOCK,),
        in_specs=[pl.BlockSpec((INVERT_BLOCK,), lambda i, lo, hi: (i,), memory_space=pltpu.SMEM)],
        out_specs=pl.BlockSpec(memory_space=pltpu.SMEM),
    )
    return pl.pallas_call(
        _invert_kernel,
        out_shape=jax.ShapeDtypeStruct((n_slots,), jnp.int32),
        grid_spec=grid_spec,
        compiler_params=_cparams(("arbitrary",)),
        name="invert_slots",
    )(pad_lo, pad_hi, dest)


SC_WINDOW = 32


def _sc_gather(table, idx):
    info = plsc.get_sparse_core_info()
    n_workers = info.num_cores * info.num_subcores
    n = idx.shape[0]
    per_worker = n // n_workers
    n_win = per_worker // SC_WINDOW
    assert n % (n_workers * SC_WINDOW * 2) == 0
    mesh = plsc.VectorSubcoreMesh(core_axis_name="c", subcore_axis_name="s")

    @functools.partial(
        pl.kernel, mesh=mesh,
        out_type=jax.ShapeDtypeStruct((n, SLABS, LANE), F32),
        scratch_types=[pltpu.VMEM((SC_WINDOW,), jnp.int32), pltpu.VMEM((SC_WINDOW,), jnp.int32),
                       pltpu.VMEM((SC_WINDOW, SLABS, LANE), F32), pltpu.VMEM((SC_WINDOW, SLABS, LANE), F32),
                       pltpu.SemaphoreType.DMA, pltpu.SemaphoreType.DMA],
        name="sc_row_gather",
    )
    def gather_kernel(table_hbm, idx_hbm, out_hbm, idx_a, idx_b, rows_a, rows_b, sem_a, sem_b):
        worker = lax.axis_index("s") * info.num_cores + lax.axis_index("c")
        base = worker * per_worker
        bufs = ((idx_a, rows_a, sem_a), (idx_b, rows_b, sem_b))

        def start(w, buf):
            idx_v, rows_v, sem = buf
            pltpu.sync_copy(idx_hbm.at[pl.ds(base + w * SC_WINDOW, SC_WINDOW)], idx_v)
            pltpu.async_copy(table_hbm.at[idx_v], rows_v, sem)

        def finish(w, buf):
            idx_v, rows_v, sem = buf
            pltpu.make_async_copy(table_hbm.at[idx_v], rows_v, sem).wait()
            pltpu.sync_copy(rows_v, out_hbm.at[pl.ds(base + w * SC_WINDOW, SC_WINDOW)])

        start(0, bufs[0])

        @pl.loop(0, n_win, step=2)
        def _(w):
            start(w + 1, bufs[1])
            finish(w, bufs[0])

            @pl.when(w + 2 < n_win)
            def _():
                start(w + 2, bufs[0])
            finish(w + 1, bufs[1])

    return gather_kernel(table, idx)


SC_LANES = 16


def _sc_invert_slots(dest, n_slots):
    n_asg = dest.shape[0]
    assert n_asg & (n_asg - 1) == 0 and n_asg % SC_LANES == 0 and n_slots % SC_LANES == 0
    info = plsc.get_sparse_core_info()
    mesh = plsc.VectorSubcoreMesh(core_axis_name="c", subcore_axis_name="s")

    @functools.partial(
        pl.kernel, mesh=mesh,
        out_type=jax.ShapeDtypeStruct((n_slots,), jnp.int32),
        scratch_types=[pltpu.VMEM((n_asg,), jnp.int32), pltpu.VMEM((n_slots,), jnp.int32)],
        compiler_params=pltpu.CompilerParams(needs_layout_passes=False),
        name="sc_invert_slots",
    )
    def invert_kernel(dest_hbm, out_hbm, dest_v, out_v):
        worker = lax.axis_index("s") * info.num_cores + lax.axis_index("c")

        @pl.when(worker == 0)
        def _():
            pltpu.sync_copy(dest_hbm, dest_v)
            lane = lax.iota(jnp.int32, SC_LANES)

            @pl.loop(0, n_slots // SC_LANES)
            def _(c):
                out_v[pl.ds(c * SC_LANES, SC_LANES)] = jnp.bitwise_and(c * SC_LANES + lane, n_asg - 1)

            @pl.loop(0, n_asg // SC_LANES)
            def _(c):
                plsc.store_scatter(out_v, [dest_v[pl.ds(c * SC_LANES, SC_LANES)]], c * SC_LANES + lane)

            pltpu.sync_copy(out_v, out_hbm)

    return invert_kernel(dest)


def _moe_rows_kernel(te_ref, nv_ref, x_ref, wg_ref, wu_ref, wd_ref, o_ref, wgb_ref, wub_ref, wdb_ref):
    i = pl.program_id(0)
    e = te_ref[i]
    e_prev = te_ref[jnp.maximum(i - 1, 0)]

    @pl.when((i == 0) | (e != e_prev))
    def _():
        wgb_ref[...] = wg_ref[0].astype(BF16)
        wub_ref[...] = wu_ref[0].astype(BF16)
        wdb_ref[...] = wd_ref[0].astype(BF16)

    @pl.when(i < nv_ref[0])
    def _():
        xb = jnp.concatenate(_load_token_slabs(x_ref, MOE_TILE), axis=1).astype(BF16)
        hg = _dot(xb, wgb_ref[...])
        hu = _dot(xb, wub_ref[...])
        mid = (hg * jax.nn.sigmoid(hg) * hu).astype(BF16)
        _store_token_tiles(o_ref, _dot(mid, wdb_ref[...]))

    @pl.when(i >= nv_ref[0])
    def _():
        o_ref[...] = jnp.zeros_like(o_ref)


def _moe_rows_into_kernel(te_ref, nv_ref, x_ref, wg_ref, wu_ref, wd_ref, prev_ref, o_ref, *scratch):
    del prev_ref
    _moe_rows_kernel(te_ref, nv_ref, x_ref, wg_ref, wu_ref, wd_ref, o_ref, *scratch)


def _moe_experts_rows(xbuf, tile_expert, n_valid, w_gate, w_up, w_down, total_tiles, tile_offset, prev=None):
    D = D_MODEL
    n_tiles = xbuf.shape[0] // (MOE_TILE * SLABS)
    tiles = pl.BlockSpec((MOE_TILE * SLABS, LANE), lambda i, te, nv: (jnp.maximum(jnp.minimum(i, nv[0] - 1), 0), 0))
    in_specs = [tiles,
                pl.BlockSpec((1, D, D_EXPERT), lambda i, te, nv: (te[i], 0, 0)),
                pl.BlockSpec((1, D, D_EXPERT), lambda i, te, nv: (te[i], 0, 0)),
                pl.BlockSpec((1, D_EXPERT, D), lambda i, te, nv: (te[i], 0, 0))]
    operands = [tile_expert, n_valid, xbuf, w_gate, w_up, w_down]
    if prev is not None:
        in_specs.append(pl.BlockSpec(memory_space=pl.ANY))
        operands.append(prev)
    grid_spec = pltpu.PrefetchScalarGridSpec(
        num_scalar_prefetch=2,
        grid=(n_tiles,),
        in_specs=in_specs,
        out_specs=pl.BlockSpec((MOE_TILE * SLABS, LANE), lambda i, te, nv: (i + tile_offset, 0)),
        scratch_shapes=[pltpu.VMEM((D, D_EXPERT), BF16), pltpu.VMEM((D, D_EXPERT), BF16),
                        pltpu.VMEM((D_EXPERT, D), BF16)],
    )
    return pl.pallas_call(
        _moe_rows_kernel if prev is None else _moe_rows_into_kernel,
        out_shape=jax.ShapeDtypeStruct((total_tiles * MOE_TILE * SLABS, LANE), F32),
        grid_spec=grid_spec,
        input_output_aliases={} if prev is None else {len(operands) - 1: 0},
        compiler_params=_cparams(("arbitrary",)),
        name="moe_experts",
    )(*operands)


def _row_copy(src_hbm, src_row, dst_ref, dst_row, sem):
    return pltpu.make_async_copy(src_hbm.at[pl.ds(src_row * SLABS, SLABS), :],
                                 dst_ref.at[pl.ds(dst_row * SLABS, SLABS), :], sem)


def _wait_tiles(src_hbm, dst_ref, sem):
    pltpu.make_async_copy(src_hbm.at[pl.ds(0, dst_ref.shape[0]), :], dst_ref, sem).wait()


def _moe_kernel(te_ref, nv_ref, ra_cur_ref, ra_nxt_ref, h_hbm, wg_ref, wu_ref, wd_ref, o_ref,
                xbuf, sem, wgb_ref, wub_ref, wdb_ref):
    i = pl.program_id(0)
    n_valid = nv_ref[0]
    slot = lax.rem(i, 2)

    def start_gather(ra_ref, s):
        def body(jj, carry):
            for u in range(DMA_UNROLL):
                j = jj * DMA_UNROLL + u
                tok = lax.shift_right_logical(ra_ref[j], 1)
                _row_copy(h_hbm, tok, xbuf.at[s], j, sem.at[s]).start(priority=u % 2)
            return carry
        lax.fori_loop(0, MOE_TILE // DMA_UNROLL, body, 0)

    @pl.when(i == 0)
    def _():
        start_gather(ra_cur_ref, 0)

    @pl.when(i + 1 < n_valid)
    def _():
        start_gather(ra_nxt_ref, 1 - slot)

    e = te_ref[i]
    e_prev = te_ref[jnp.maximum(i - 1, 0)]

    @pl.when((i == 0) | (e != e_prev))
    def _():
        wgb_ref[...] = wg_ref[0].astype(BF16)
        wub_ref[...] = wu_ref[0].astype(BF16)
        wdb_ref[...] = wd_ref[0].astype(BF16)

    @pl.when(i < n_valid)
    def _():
        _wait_tiles(h_hbm, xbuf.at[slot], sem.at[slot])
        xb = jnp.concatenate(_load_token_slabs(xbuf.at[slot], MOE_TILE), axis=1).astype(BF16)
        hg = _dot(xb, wgb_ref[...])
        hu = _dot(xb, wub_ref[...])
        mid = (hg * jax.nn.sigmoid(hg) * hu).astype(BF16)
        _store_token_tiles(o_ref, _dot(mid, wdb_ref[...]))

    @pl.when(i >= n_valid)
    def _():
        o_ref[...] = jnp.zeros_like(o_ref)


def _moe_experts(h1, row_asg, tile_expert, n_valid, w_gate, w_up, w_down):
    D = D_MODEL
    n_tiles = row_asg.shape[0] // MOE_TILE
    smem_tile = lambda f: pl.BlockSpec((MOE_TILE,), f, memory_space=pltpu.SMEM)
    grid_spec = pltpu.PrefetchScalarGridSpec(
        num_scalar_prefetch=2,
        grid=(n_tiles,),
        in_specs=[smem_tile(lambda i, te, nv: (i,)),
                  smem_tile(lambda i, te, nv: (jnp.minimum(i + 1, n_tiles - 1),)),
                  pl.BlockSpec(memory_space=pl.ANY),
                  pl.BlockSpec((1, D, D_EXPERT), lambda i, te, nv: (te[i], 0, 0)),
                  pl.BlockSpec((1, D, D_EXPERT), lambda i, te, nv: (te[i], 0, 0)),
                  pl.BlockSpec((1, D_EXPERT, D), lambda i, te, nv: (te[i], 0, 0))],
        out_specs=pl.BlockSpec((MOE_TILE * SLABS, LANE), lambda i, te, nv: (i, 0)),
        scratch_shapes=[pltpu.VMEM((2, MOE_TILE * SLABS, LANE), F32), pltpu.SemaphoreType.DMA((2,)),
                        pltpu.VMEM((D, D_EXPERT), BF16), pltpu.VMEM((D, D_EXPERT), BF16),
                        pltpu.VMEM((D_EXPERT, D), BF16)],
    )
    return pl.pallas_call(
        _moe_kernel,
        out_shape=jax.ShapeDtypeStruct((n_tiles * MOE_TILE * SLABS, LANE), F32),
        grid_spec=grid_spec,
        compiler_params=_cparams(("arbitrary",)),
        name="moe_experts",
    )(tile_expert, n_valid, row_asg, row_asg, h1, w_gate, w_up, w_down)


def _combine_kernel(d_cur_ref, d_nxt_ref, h_ref, gate_ref, g_ref, b_ref, y_hbm, o_ref, ybuf, sem):
    i = pl.program_id(0)
    n = pl.num_programs(0)
    slot = lax.rem(i, 2)

    def start_gather(d_ref, s):
        def body(tt, carry):
            for u in range(DMA_UNROLL // TOP_K):
                t = tt * (DMA_UNROLL // TOP_K) + u
                for k in range(TOP_K):
                    _row_copy(y_hbm, d_ref[TOP_K * t + k], ybuf.at[s, k], t, sem.at[s]).start(priority=k % 2)
            return carry
        lax.fori_loop(0, COMBINE_TILE * TOP_K // DMA_UNROLL, body, 0)

    @pl.when(i == 0)
    def _():
        start_gather(d_cur_ref, 0)

    @pl.when(i + 1 < n)
    def _():
        start_gather(d_nxt_ref, 1 - slot)

    for k in range(TOP_K):
        _wait_tiles(y_hbm, ybuf.at[slot, k], sem.at[slot])

    gate = gate_ref[...]
    tm = COMBINE_TILE
    ga = jnp.broadcast_to(gate[:, 0:1], (tm, LANE))
    gb = jnp.broadcast_to(gate[:, 1:2], (tm, LANE))
    hs = _load_token_slabs(h_ref, tm)
    ya = _load_token_slabs(ybuf.at[slot, 0], tm)
    yb = _load_token_slabs(ybuf.at[slot, 1], tm)
    z = [DEEPNORM_ALPHA * h + (ga * a + gb * b) for h, a, b in zip(hs, ya, yb)]
    inv_d = 1.0 / D_MODEL
    mu = sum(jnp.sum(t, axis=1, keepdims=True) for t in z) * inv_d
    zc = [t - mu for t in z]
    var = sum(jnp.sum(t * t, axis=1, keepdims=True) for t in zc) * inv_d
    rstd = lax.rsqrt(var + LN_EPS)
    for s in range(SLABS):
        cols = slice(s * LANE, (s + 1) * LANE)
        o_ref[:, cols] = zc[s] * rstd * g_ref[:, cols] + b_ref[:, cols]


def _combine(h1, ybuf, dest, gates, ln2_g, ln2_b):
    D = D_MODEL
    M = h1.shape[0] // SLABS
    tm = COMBINE_TILE
    n = M // tm
    smem_tile = lambda f: pl.BlockSpec((TOP_K * tm,), f, memory_space=pltpu.SMEM)
    return pl.pallas_call(
        _combine_kernel,
        out_shape=jax.ShapeDtypeStruct((M, D), F32),
        grid=(n,),
        in_specs=[smem_tile(lambda i: (i,)), smem_tile(lambda i: (jnp.minimum(i + 1, n - 1),)),
                  pl.BlockSpec((tm * SLABS, LANE), lambda i: (i, 0)), pl.BlockSpec((tm, TOP_K), lambda i: (i, 0)),
                  _full(ln2_g), _full(ln2_b), pl.BlockSpec(memory_space=pl.ANY)],
        out_specs=pl.BlockSpec((tm, D), lambda i: (i, 0)),
        scratch_shapes=[pltpu.VMEM((2, TOP_K, tm * SLABS, LANE), F32), pltpu.SemaphoreType.DMA((2,))],
        compiler_params=_cparams(("arbitrary",)),
        name="combine",
    )(dest, dest, h1, gates, ln2_g, ln2_b, ybuf)


def _routing_plan(route):
    M = route.shape[0]
    eid = route[:, :TOP_K].astype(jnp.int32).reshape(-1)
    gates = route[:, TOP_K:2 * TOP_K]
    A = M * TOP_K
    onehot = (eid[:, None] == jnp.arange(N_EXPERTS, dtype=eid.dtype)[None, :]).astype(jnp.int32)
    csum = jnp.cumsum(onehot, axis=0)
    rank = jnp.sum(csum * onehot, axis=1) - 1
    counts = csum[-1]
    pcounts = (counts + MOE_TILE - 1) // MOE_TILE * MOE_TILE
    pends = jnp.cumsum(pcounts)
    pstarts = pends - pcounts
    dest = (jnp.sum(onehot * pstarts[None, :], axis=1) + rank).astype(jnp.int32)
    n_tiles = (A + N_EXPERTS * (MOE_TILE - 1) + MOE_TILE - 1) // MOE_TILE
    n_valid = (pends[-1] // MOE_TILE).astype(jnp.int32)
    tile_start = jnp.minimum(jnp.arange(n_tiles, dtype=jnp.int32) * MOE_TILE, pends[-1] - 1)
    tile_expert = jnp.sum((pends[None, :] <= tile_start[:, None]).astype(jnp.int32), axis=1)
    tile_expert = jnp.minimum(tile_expert, N_EXPERTS - 1).astype(jnp.int32)
    pad_lo = (pstarts + counts).astype(jnp.int32)
    pad_hi = pends.astype(jnp.int32)
    return gates, dest, n_tiles * MOE_TILE, tile_expert, n_valid.reshape(1), pad_lo, pad_hi


def kernel(x, meta, ln0_g, ln0_b, w_in, mu_shift, w0, w_decay_up, a0, w_a_up, w_g_up, k_k, k_a, r_k, gn_g, gn_b, conv_w, conv_b, w_rg, b_rg, w_ig, b_ig, lru_lambda, w_out, ln1_g, ln1_b, w_router_grp, b_router_grp, w_router_exp, b_router_exp, w_exp_gate, w_exp_up, w_exp_down, ln2_g, ln2_b):
    B, T, D = x.shape
    assert D == D_MODEL and T % 512 == 0 and w_in.shape[0] == 1
    assert (B * T * TOP_K) % INVERT_BLOCK == 0
    row = lambda p: p.reshape(1, -1).astype(F32)
    n_rw = 3 * RWKV_W
    w_in0 = w_in[0]

    def slots(p):
        pad = lambda a, n: jnp.pad(a, [(0, 0)] * (a.ndim - 1) + [(0, n - a.shape[-1])])
        zw = p[..., n_rw:n_rw + DECAY_RANK]
        za = p[..., n_rw + DECAY_RANK:n_rw + DECAY_RANK + AAA_RANK]
        zg = p[..., n_rw + DECAY_RANK + AAA_RANK:n_rw + DECAY_RANK + AAA_RANK + GATE_RANK]
        return jnp.concatenate([p[..., :n_rw], pad(zw, LANE), pad(za, LANE), pad(zg, ZG_SLOT)], axis=-1)

    rwkv_cols = n_rw + DECAY_RANK + AAA_RANK + GATE_RANK
    w_r = slots(w_in0[:, :rwkv_cols]).astype(BF16)
    w_l = w_in0[:, rwkv_cols:].astype(BF16)
    ur, ul, ur_t, ul_t = _in_projection(x, meta, row(ln0_g), row(ln0_b), w_r, w_l)

    pad_rows = lambda a, n: jnp.pad(a, ((0, n - a.shape[0]), (0, 0)))
    rwkv_params = (slots(mu_shift[0][None, :]).astype(F32), row(w0[0]), pad_rows(w_decay_up[0], LANE).astype(BF16),
                   row(a0[0]), pad_rows(w_a_up[0], LANE).astype(BF16), pad_rows(w_g_up[0], ZG_SLOT).astype(BF16),
                   row(k_k[0]), row(k_a[0]), row(r_k[0]), row(gn_g[0]), row(gn_b[0]))
    y_rwkv = _rwkv_pipe_mixer(ur, ur_t, rwkv_params)

    blockdiag = lambda w: jax.scipy.linalg.block_diag(*[w[i] for i in range(LRU_BLOCKS)]).astype(BF16)
    lru_params = (conv_w[0], row(conv_b[0]), blockdiag(w_rg[0]), row(b_rg[0]), blockdiag(w_ig[0]), row(b_ig[0]),
                  row(lru_lambda[0]))
    y_lru = _lru_mixer(ul, ul_t, lru_params)

    w_rt = jnp.concatenate([w_router_grp[0], w_router_exp[0]], axis=1)
    w_rt = jnp.pad(w_rt, ((0, 0), (0, LANE - w_rt.shape[1])))
    wrt_hi = w_rt.astype(BF16)
    wrt_lo = (w_rt - wrt_hi.astype(F32)).astype(BF16)
    b_rt = jnp.concatenate([b_router_grp[0], b_router_exp[0]])
    b_rt = jnp.pad(b_rt, (0, LANE - b_rt.shape[0])).reshape(1, LANE)
    wo = w_out[0].astype(BF16)
    h1, route = _out_projection(x, y_rwkv, y_lru, row(ln0_g), row(ln0_b), wo[:RWKV_W], wo[RWKV_W:],
                                row(ln1_g[0]), row(ln1_b[0]), wrt_hi, wrt_lo, b_rt)

    M = B * T
    h1 = h1.reshape(M * SLABS, LANE)
    gates, dest, n_slots, tile_expert, n_valid, pad_lo, pad_hi = _routing_plan(route.reshape(M, LANE))
    row_asg = _sc_invert_slots(dest, n_slots)
    src_tok = lax.shift_right_logical(row_asg, 1)
    n_tiles = n_slots // MOE_TILE
    half = n_tiles // 2
    h1_tiles = h1.reshape(M, SLABS, LANE)
    ybuf = None
    for lo, hi in ((0, half), (half, n_tiles)):
        xbuf = _sc_gather(h1_tiles, src_tok[lo * MOE_TILE:hi * MOE_TILE]).reshape(-1, LANE)
        nv = jnp.clip(n_valid - lo, 0, hi - lo)
        ybuf = _moe_experts_rows(xbuf, tile_expert[lo:hi], nv, w_exp_gate[0], w_exp_up[0], w_exp_down[0],
                                 total_tiles=n_tiles, tile_offset=lo, prev=ybuf)
    out = _combine(h1, ybuf, dest, gates, row(ln2_g[0]), row(ln2_b[0]))
    return out.reshape(B, T, D)
```

```python
import functools
import math

import jax
import jax.numpy as jnp
from jax import lax
from jax.experimental import pallas as pl
from jax.experimental.pallas import tpu as pltpu
from jax.experimental.pallas import tpu_sc as plsc

F32 = jnp.float32
BF16 = jnp.bfloat16

D_MODEL = 1024
N_META = 16
RWKV_W = 512
RWKV_HEAD = 64
DECAY_RANK = 64
AAA_RANK = 64
GATE_RANK = 160
LRU_W = 512
LRU_BLOCKS = 8
CONV_WIDTH = 4
LRU_C = 8.0
N_GROUPS = 4
EXPERTS_PER_GROUP = 8
N_EXPERTS = N_GROUPS * EXPERTS_PER_GROUP
TOP_K = 2
D_EXPERT = 512
LN_EPS = 1e-5
GN_EPS = 64e-5
DEEPNORM_ALPHA = 2.0 ** 0.25

LANE = 128
OFF_R, OFF_K, OFF_V = 0, RWKV_W, 2 * RWKV_W
OFF_ZW = 3 * RWKV_W
OFF_ZA = OFF_ZW + LANE
OFF_ZG = OFF_ZA + LANE
ZG_SLOT = 2 * LANE
UR_W = OFF_ZG + ZG_SLOT
UL_W = 2 * LRU_W

TAIL = 256
CHUNK = 64
HEADS_PER_GROUP = 4
GW = HEADS_PER_GROUP * RWKV_HEAD
N_HGROUPS = RWKV_W // GW
LRU_TILE = TAIL
MOE_TILE = 256
COMBINE_TILE = 256
DMA_UNROLL = 8
V7X_VMEM_BYTES = 64 * 1024 * 1024
VMEM_LIMIT = V7X_VMEM_BYTES - 8 * 1024 * 1024


def _cparams(sem):
    return pltpu.CompilerParams(dimension_semantics=sem, vmem_limit_bytes=VMEM_LIMIT)


def _layer_norm(x, g, b):
    mu = jnp.mean(x, -1, keepdims=True)
    xc = x - mu
    var = jnp.mean(xc * xc, -1, keepdims=True)
    return xc * lax.rsqrt(var + LN_EPS) * g + b


def _dot(a, b):
    return jnp.dot(a, b, preferred_element_type=F32)


def _dot_nt(a, b):
    return lax.dot_general(a, b, (((1,), (1,)), ((), ())), preferred_element_type=F32)


def _dot_tn(a, b):
    return lax.dot_general(a, b, (((0,), (0,)), ((), ())), preferred_element_type=F32)


def _full(a):
    return pl.BlockSpec(a.shape, lambda *_: (0,) * a.ndim)


def _inproj_kernel(x_ref, g_ref, b_ref, wr_ref, wl_ref, ur_ref, ul_ref):
    h = _layer_norm(x_ref[0], g_ref[...], b_ref[...]).astype(BF16)
    ur_ref[0] = _dot(h, wr_ref[...])
    ul_ref[0] = _dot(h, wl_ref[...])


def _inproj_tail_kernel(x_ref, g_ref, b_ref, wr_ref, wl_ref, ur_ref, ul_ref):
    h = _layer_norm(x_ref[...], g_ref[...], b_ref[...]).astype(BF16)
    rows = lax.broadcasted_iota(jnp.int32, (TAIL, 1), 0)
    valid = (rows >= TAIL - N_META).astype(F32)
    ur_ref[...] = _dot(h, wr_ref[...]) * valid
    ul_ref[...] = _dot(h, wl_ref[...]) * valid


def _in_projection(x, meta, ln0_g, ln0_b, w_r, w_l):
    B, T, D = x.shape
    tm = 512
    ur, ul = pl.pallas_call(
        _inproj_kernel,
        out_shape=(jax.ShapeDtypeStruct((B, T, UR_W), F32), jax.ShapeDtypeStruct((B, T, UL_W), F32)),
        grid=(B, T // tm),
        in_specs=[pl.BlockSpec((1, tm, D), lambda b, i: (b, i, 0)), _full(ln0_g), _full(ln0_b), _full(w_r), _full(w_l)],
        out_specs=(pl.BlockSpec((1, tm, UR_W), lambda b, i: (b, i, 0)),
                   pl.BlockSpec((1, tm, UL_W), lambda b, i: (b, i, 0))),
        compiler_params=_cparams(("parallel", "parallel")),
        name="inproj",
    )(x, ln0_g, ln0_b, w_r, w_l)
    tail_x = jnp.concatenate([jnp.zeros((TAIL - N_META, D), F32), meta.astype(F32)], axis=0)
    ur_t, ul_t = pl.pallas_call(
        _inproj_tail_kernel,
        out_shape=(jax.ShapeDtypeStruct((TAIL, UR_W), F32), jax.ShapeDtypeStruct((TAIL, UL_W), F32)),
        grid=(1,),
        in_specs=[_full(tail_x), _full(ln0_g), _full(ln0_b), _full(w_r), _full(w_l)],
        out_specs=(pl.BlockSpec((TAIL, UR_W), lambda i: (0, 0)), pl.BlockSpec((TAIL, UL_W), lambda i: (0, 0))),
        compiler_params=_cparams(("arbitrary",)),
        name="inproj_tail",
    )(tail_x, ln0_g, ln0_b, w_r, w_l)
    return ur, ul, ur_t, ul_t


def _rwkv_pipe_kernel(u_ref, ut_ref, mu_ref, w0_ref, wdu_ref, a0_ref, wau_ref, wgu_ref, kk_ref, ka_ref, rk_ref,
                      gng_ref, gnb_ref, bones_ref, bm_ref, eye_ref, msl_ref, mil_ref,
                      m8_ref, m16_ref, m32_ref, m64_ref, y_ref,
                      s_ref, prev_ref, y0_s, q_s, mc_s, nc_s, we_s, bonus_s, g_s, yraw_s):
    s_id = pl.program_id(0)
    nb = u_ref.shape[0]
    blk = u_ref.shape[1]
    npc = blk // CHUNK
    seq_rows = lambda q: slice(q * CHUNK, (q + 1) * CHUNK)
    w_slot = lax.rem(s_id, 2)
    r_slot = 1 - w_slot

    @pl.when(s_id == 0)
    def _():
        s_ref[...] = jnp.zeros_like(s_ref)
        prev_ref[...] = jnp.zeros_like(prev_ref)
        for ref in (y0_s, q_s, mc_s, nc_s, we_s, bonus_s, g_s):
            ref[1] = jnp.zeros(ref.shape[1:], ref.dtype)

    b16 = lambda t: t.astype(BF16)
    bones = bones_ref[...]
    head_sum = lambda t: _dot(b16(t), bones)
    bm = bm_ref[...]
    bm16 = b16(bm)
    tile4 = lambda t: jnp.concatenate([t] * HEADS_PER_GROUP, axis=0)
    fold4 = lambda t: sum(t[i * CHUNK:(i + 1) * CHUNK] for i in range(HEADS_PER_GROUP))
    bd = lambda t: tile4(b16(t)) * bm16

    chains = [(b, hg) for b in range(nb) for hg in range(N_HGROUPS)]
    states = {c: s_ref[c[0], c[1]] for c in chains}

    def recurrent_chunk(j):
        for b, hg in chains:
            sl = slice(hg * GW, (hg + 1) * GW)
            q = b * npc + j
            rq = seq_rows(q)
            s = states[(b, hg)]
            yraw_s[rq, sl] = y0_s[r_slot, rq, sl] + _dot_nt(q_s[r_slot, rq, sl], bd(s))
            states[(b, hg)] = (s * we_s[r_slot, q * 8:q * 8 + 1, sl] + _dot(b16(s), bd(mc_s[r_slot, rq, sl]))
                               + nc_s[r_slot, rq, sl])

    prepared = {}

    def prepare(b):
        lo = b * blk
        u = jnp.where(s_id == 0, ut_ref[...], u_ref[b])
        row = lax.broadcasted_iota(jnp.int32, u.shape, 0)
        u_prev = jnp.where(row == 0, prev_ref[b:b + 1, :], pltpu.roll(u, 1, 0))
        prev_ref[b:b + 1, :] = u[blk - 1:blk, :]
        x = u + (u_prev - u) * mu_ref[...]
        r = x[:, OFF_R:OFF_R + RWKV_W]
        k = x[:, OFF_K:OFF_K + RWKV_W]
        v = x[:, OFF_V:OFF_V + RWKV_W]
        zw = x[:, OFF_ZW:OFF_ZW + LANE]
        za = x[:, OFF_ZA:OFF_ZA + LANE]
        zg = x[:, OFF_ZG:OFF_ZG + ZG_SLOT]
        yield
        z = w0_ref[...] + _dot(b16(jnp.tanh(zw)), wdu_ref[...])
        logw = -math.exp(-0.5) * jax.nn.sigmoid(z)
        a = jax.nn.sigmoid(a0_ref[...] + _dot(b16(za), wau_ref[...]))
        g = _dot(b16(jax.nn.sigmoid(zg)), wgu_ref[...])
        kk = k * kk_ref[...]
        kk = kk / jnp.maximum(jnp.sqrt(head_sum(kk * kk)), 1e-12)
        k = k * (1.0 + (a - 1.0) * ka_ref[...])
        kka = kk * a
        bonus_s[w_slot, lo:lo + blk, :] = head_sum(r * k * rk_ref[...]) * v
        g_s[w_slot, lo:lo + blk, :] = g
        yield
        cl = logw
        row_in_chunk = jnp.bitwise_and(lax.broadcasted_iota(jnp.int32, cl.shape, 0), CHUNK - 1)
        d = 1
        while d < CHUNK:
            cl = cl + jnp.where(row_in_chunk >= d, pltpu.roll(cl, d, 0), 0.0)
            d *= 2
        yield
        cl_last = jnp.concatenate(
            [jnp.broadcast_to(cl[(j + 1) * CHUNK - 1:(j + 1) * CHUNK, :], (CHUNK, RWKV_W)) for j in range(npc)], axis=0)
        e_neg = jnp.exp(-cl)
        e_end = jnp.exp(cl_last - cl)
        w_end = jnp.exp(cl_last)
        for j in range(npc):
            q = b * npc + j
            we_s[w_slot, q * 8:(q + 1) * 8, :] = w_end[j * CHUNK:j * CHUNK + 8, :]
        prepared[b] = dict(rt=r * jnp.exp(cl), kt=k * e_neg, at=-kk * jnp.exp(cl - logw), bt=kka * e_neg,
                           kw=k * e_end, bw=kka * e_end, v=v)
        yield

    eye = eye_ref[...]
    msl = msl_ref[...]
    mil = mil_ref[...]
    rows2 = lambda x, y: jnp.concatenate([x, y], axis=0)
    cols2 = lambda x, y: jnp.concatenate([x, y], axis=1)
    each = lambda f, *ls: [f(*xs) for xs in zip(*ls)]

    def solve(b, p):
        probs = [(slice(j * CHUNK, (j + 1) * CHUNK), slice(hg * GW, (hg + 1) * GW))
                 for j in range(npc) for hg in range(N_HGROUPS)]
        pick = lambda t: [t[rq, sl] for rq, sl in probs]
        at_w, rt_w, v_w = pick(p["at"]), pick(p["rt"]), pick(p["v"])
        lhs = each(lambda x, y: b16(rows2(x, y)), at_w, rt_w)
        ab = each(_dot_nt, lhs, each(bd, pick(p["bt"])))
        ak = each(_dot_nt, lhs, each(bd, pick(p["kt"])))
        yield
        a_ab = each(lambda t: t[:CHUNK] * msl, ab)
        a_rb = each(lambda t: b16(t[CHUNK:] * mil), ab)
        a_xk = each(lambda t: b16(rows2(t[:CHUNK] * msl, t[CHUNK:] * mil)), ak)
        a0 = each(lambda t: b16(t * m8_ref[...]), a_ab)
        a2 = each(lambda t: b16(_dot(t, bd(t))), a0)
        yield
        a4 = each(lambda t: b16(_dot(t, bd(t))), a2)
        p1 = each(lambda t: eye + t.astype(F32), a0)
        p1 = each(lambda q, t: q + _dot(b16(q), bd(t)), p1, a2)
        yield
        tt = each(lambda q, t: q + _dot(b16(q), bd(t)), p1, a4)
        yield
        for m_ref in (m16_ref, m32_ref, m64_ref):
            tb = each(b16, tt)
            off = each(lambda t: b16(t * m_ref[...]), a_ab)
            half = each(lambda x, y: b16(_dot(x, bd(y))), tb, off)
            yield
            tt = each(lambda t, x, y: t + _dot(x, bd(y)), tt, half, tb)
            yield
        tb = each(b16, tt)
        xv = each(lambda x, y: _dot(x, bd(y)), a_xk, v_w)
        yield
        u0 = each(lambda x, y: _dot(x, bd(y[:CHUNK])), tb, xv)
        ta = each(lambda x, y: _dot(x, bd(y)), tb, at_w)
        yield
        y0 = each(lambda x, y, z: _dot(x, bd(y)) + z[CHUNK:], a_rb, u0, xv)
        qq = each(lambda x, y, z: x + _dot(y, bd(z)), rt_w, a_rb, ta)
        yield
        left = each(lambda x, y, z: b16(rows2(cols2(x, y), cols2(jnp.zeros_like(z), z))), ta, u0, v_w)
        right = each(lambda x, y: b16(rows2(x, y)), pick(p["bw"]), pick(p["kw"]))
        mn = each(_dot_tn, left, right)
        for i, (rq, sl) in enumerate(probs):
            rows = slice(b * blk + rq.start, b * blk + rq.stop)
            y0_s[w_slot, rows, sl] = y0[i]
            q_s[w_slot, rows, sl] = b16(qq[i])
            mc_s[w_slot, rows, sl] = b16(fold4(mn[i][:GW] * bm))
            nc_s[w_slot, rows, sl] = fold4(mn[i][GW:] * bm)
        yield

    first = prepare(0)
    for j in range(npc):
        recurrent_chunk(j)
        if j * 4 // npc != (j + 1) * 4 // npc or j == npc - 1:
            for _ in range((j + 1) * 4 // npc - j * 4 // npc):
                next(first, None)
    for _ in first:
        pass
    for (b, hg), s in states.items():
        s_ref[b, hg] = s

    y = yraw_s[...]
    inv_n = 1.0 / RWKV_HEAD
    ym = head_sum(y) * inv_n
    yc = y - ym
    yv = head_sum(yc * yc) * inv_n
    yn = yc * lax.rsqrt(yv + GN_EPS) * gng_ref[...] + gnb_ref[...]
    y_ref[...] = ((yn + bonus_s[r_slot]) * g_s[r_slot]).astype(y_ref.dtype).reshape(y_ref.shape)

    for b in range(nb):
        solver = solve(b, prepared[b])
        nxt = prepare(b + 1) if b + 1 < nb else iter(())
        for level, _ in enumerate(solver):
            if level % 4 == 3:
                next(nxt, None)
        for _ in nxt:
            pass


def _rwkv_masks():
    f = lambda m: m.astype(F32)
    i = jnp.arange(GW)[:, None]
    j = jnp.arange(GW)[None, :]
    bm = f((i // RWKV_HEAD) == (j // RWKV_HEAD))
    t = jnp.arange(CHUNK)[:, None]
    s = (jnp.arange(GW) % CHUNK)[None, :]
    same = lambda n: (t // n) == (s // n)
    msl = f(t > s)
    mil = f(t >= s)
    m8 = f(same(8))
    m16 = f(same(16) & ~same(8))
    m32 = f(same(32) & ~same(16))
    m64 = f(~same(32))
    eye = f(t == s)
    hi = jnp.arange(RWKV_W)
    bones = ((hi[:, None] // RWKV_HEAD) == (hi[None, :] // RWKV_HEAD)).astype(BF16)
    return bones, bm, eye, msl, mil, m8, m16, m32, m64


def _rwkv_pipe_mixer(ur, ur_tail, params):
    B, T, _ = ur.shape
    blk = TAIL
    assert T % blk == 0 and blk % CHUNK == 0 and CHUNK == RWKV_HEAD
    n_blocks = T // blk
    rows = B * blk
    consts = _rwkv_masks()
    in_map = lambda s: (0, jnp.clip(s - 1, 0, n_blocks - 1), 0)
    out_map = lambda s: (0, jnp.clip(s - 2, 0, n_blocks - 1), 0)
    slot2 = lambda w, dt: pltpu.VMEM((2, rows, w), dt)
    return pl.pallas_call(
        _rwkv_pipe_kernel,
        out_shape=jax.ShapeDtypeStruct((B, T, RWKV_W), BF16),
        grid=(n_blocks + 2,),
        in_specs=[pl.BlockSpec((B, blk, UR_W), in_map), _full(ur_tail)]
                 + [_full(p) for p in params] + [_full(m) for m in consts],
        out_specs=pl.BlockSpec((B, blk, RWKV_W), out_map),
        scratch_shapes=[pltpu.VMEM((B, N_HGROUPS, CHUNK, GW), F32), pltpu.VMEM((B, UR_W), F32),
                        slot2(RWKV_W, F32), slot2(RWKV_W, BF16), slot2(RWKV_W, BF16), slot2(RWKV_W, F32),
                        pltpu.VMEM((2, 8 * rows // CHUNK, RWKV_W), F32), slot2(RWKV_W, F32), slot2(RWKV_W, F32),
                        pltpu.VMEM((rows, RWKV_W), F32)],
        compiler_params=_cparams(("arbitrary",)),
        name="rwkv7",
    )(ur, ur_tail, *params, *consts)


def _gelu_tanh(x):
    return 0.5 * x * (1.0 + jnp.tanh(math.sqrt(2.0 / math.pi) * (x + 0.044715 * (x * x * x))))


LRU_CARRY = 8


def _lru_kernel(u_ref, ut_ref, cw_ref, cb_ref, wrg_ref, brg_ref, wig_ref, big_ref, lam_ref, y_ref,
                xs_ref, hprev_ref):
    c = pl.program_id(0)
    nb = u_ref.shape[0]
    nrow = nb * LRU_TILE

    @pl.when(c == 0)
    def _():
        xs_ref[...] = jnp.zeros_like(xs_ref)
        hprev_ref[...] = jnp.zeros_like(hprev_ref)

    u_x = u_ref[...].reshape(nrow, UL_W)
    u = jnp.where(c == 0, jnp.concatenate([ut_ref[...]] * nb, axis=0), u_x)
    xl = u[:, :LRU_W]
    gl = u[:, LRU_W:]
    row = jnp.bitwise_and(lax.broadcasted_iota(jnp.int32, (nrow, LRU_W), 0), LRU_TILE - 1)
    in_group = jnp.bitwise_and(row, 7)
    roll_in_group = lambda t, d: pltpu.roll(t.reshape(t.shape[0] // 8, 8, LRU_W), d, 1).reshape(t.shape)
    xl_prev = jnp.concatenate(
        [p for b in range(nb) for p in (xs_ref[b], xl[b * LRU_TILE:(b + 1) * LRU_TILE - 8])], axis=0)
    xc = cb_ref[...] + cw_ref[CONV_WIDTH - 1:CONV_WIDTH, :] * xl
    for d in range(1, CONV_WIDTH):
        tap = jnp.where(in_group >= d, roll_in_group(xl, d), roll_in_group(xl_prev, d))
        xc = xc + cw_ref[CONV_WIDTH - 1 - d:CONV_WIDTH - d, :] * tap
    for b in range(nb):
        xs_ref[b] = xl[(b + 1) * LRU_TILE - 8:(b + 1) * LRU_TILE]

    xcb = xc.astype(BF16)
    gate_r = jax.nn.sigmoid(_dot(xcb, wrg_ref[...]) + brg_ref[...])
    gate_i = jax.nn.sigmoid(_dot(xcb, wig_ref[...]) + big_ref[...])
    lam = lam_ref[...]
    log_sig = -(jnp.maximum(-lam, 0.0) + jnp.log1p(jnp.exp(-jnp.abs(lam))))
    log_a = LRU_C * gate_r * log_sig
    a = jnp.exp(log_a)
    mult = jnp.sqrt(jnp.maximum(1.0 - jnp.exp(2.0 * log_a), 0.0))
    b = mult * gate_i * xc
    b = jnp.where((c == 0) & (row < LRU_TILE - N_META), 0.0, b)

    d = 1
    while d < 8:
        keep = in_group >= d
        a_sh = jnp.where(keep, roll_in_group(a, d), 1.0)
        b_sh = jnp.where(keep, roll_in_group(b, d), 0.0)
        b = a * b_sh + b
        a = a * a_sh
        d *= 2
    groups = []
    for bi in range(nb):
        carry = hprev_ref[bi:bi + 1, :]
        for gi in range(LRU_TILE // 8):
            lo = bi * LRU_TILE + gi * 8
            hg = b[lo:lo + 8] + a[lo:lo + 8] * carry
            carry = hg[7:8, :]
            groups.append(hg)
        hprev_ref[bi:bi + 1, :] = carry
    h = jnp.concatenate(groups, axis=0)
    y_ref[...] = (h * _gelu_tanh(gl)).astype(y_ref.dtype).reshape(y_ref.shape)


def _lru_mixer(ul, ul_tail, params):
    B, T, _ = ul.shape
    assert TAIL == LRU_TILE
    x_map = lambda c: (0, jnp.maximum(c - 1, 0), 0)
    return pl.pallas_call(
        _lru_kernel,
        out_shape=jax.ShapeDtypeStruct((B, T, LRU_W), BF16),
        grid=(T // LRU_TILE + 1,),
        in_specs=[pl.BlockSpec((B, LRU_TILE, UL_W), x_map), _full(ul_tail)] + [_full(p) for p in params],
        out_specs=pl.BlockSpec((B, LRU_TILE, LRU_W), x_map),
        scratch_shapes=[pltpu.VMEM((B, LRU_CARRY, LRU_W), F32), pltpu.VMEM((B, LRU_W), F32)],
        compiler_params=_cparams(("arbitrary",)),
        name="rglru",
    )(ul, ul_tail, *params)


def _route(lg):
    lane = lax.broadcasted_iota(jnp.int32, lg.shape, 1)
    neg = jnp.float32(-jnp.inf)
    rmax = lambda t: jnp.max(t, axis=1, keepdims=True)
    first = lambda hit: jnp.min(jnp.where(hit, lane, LANE), axis=1, keepdims=True)
    is_grp = lane < N_GROUPS
    gl = jnp.where(is_grp, lg, neg)
    gmax = rmax(gl)
    g_sel = first(gl == gmax)
    p_g = 1.0 / jnp.sum(jnp.where(is_grp, jnp.exp(lg - gmax), 0.0), axis=1, keepdims=True)
    ex = lane - N_GROUPS
    in_grp = (ex >= 0) & (ex < N_EXPERTS) & (jnp.right_shift(ex, 3) == g_sel)
    el = jnp.where(in_grp, lg, neg)
    v1 = rmax(el)
    i1 = first(el == v1)
    el2 = jnp.where(lane == i1, neg, el)
    v2 = rmax(el2)
    i2 = first(el2 == v2)
    t = jnp.exp(v2 - v1)
    gate1 = p_g / (1.0 + t)
    gate2 = p_g * t / (1.0 + t)
    e1 = (i1 - N_GROUPS).astype(F32)
    e2 = (i2 - N_GROUPS).astype(F32)
    return jnp.where(lane == 0, e1, jnp.where(lane == 1, e2, jnp.where(lane == 2, gate1, jnp.where(lane == 3, gate2, 0.0))))


SLABS = D_MODEL // LANE


def _store_token_tiles(ref, val):
    n = val.shape[0]
    for s in range(SLABS):
        ref[pl.ds(s, n, stride=SLABS), :] = val[:, s * LANE:(s + 1) * LANE]


def _load_token_slabs(ref, n):
    return [ref[pl.ds(s, n, stride=SLABS), :] for s in range(SLABS)]


def _outproj_kernel(x_ref, yr_ref, yl_ref, g0_ref, b0_ref, wor_ref, wol_ref, g1_ref, b1_ref,
                    wrt_hi_ref, wrt_lo_ref, brt_ref, h1_ref, rt_ref):
    h0 = _layer_norm(x_ref[0], g0_ref[...], b0_ref[...])
    mix = _dot(yr_ref[0], wor_ref[...]) + _dot(yl_ref[0], wol_ref[...])
    h1 = _layer_norm(DEEPNORM_ALPHA * h0 + mix, g1_ref[...], b1_ref[...])
    _store_token_tiles(h1_ref.at[0], h1)
    hi = h1.astype(BF16)
    lo = (h1 - hi.astype(F32)).astype(BF16)
    w_hi = wrt_hi_ref[...]
    lg = _dot(hi, w_hi) + (_dot(hi, wrt_lo_ref[...]) + _dot(lo, w_hi)) + brt_ref[...]
    rt_ref[0] = _route(lg)


def _out_projection(x, y_rwkv, y_lru, ln0_g, ln0_b, wo_r, wo_l, ln1_g, ln1_b, wrt_hi, wrt_lo, brt):
    B, T, D = x.shape
    tm = 512
    rows = lambda w: pl.BlockSpec((1, tm, w), lambda b, i: (b, i, 0))
    return pl.pallas_call(
        _outproj_kernel,
        out_shape=(jax.ShapeDtypeStruct((B, T * SLABS, LANE), F32), jax.ShapeDtypeStruct((B, T, LANE), F32)),
        grid=(B, T // tm),
        in_specs=[rows(D), rows(RWKV_W), rows(LRU_W), _full(ln0_g), _full(ln0_b), _full(wo_r), _full(wo_l),
                  _full(ln1_g), _full(ln1_b), _full(wrt_hi), _full(wrt_lo), _full(brt)],
        out_specs=(pl.BlockSpec((1, tm * SLABS, LANE), lambda b, i: (b, i, 0)), rows(LANE)),
        compiler_params=_cparams(("parallel", "parallel")),
        name="outproj",
    )(x, y_rwkv, y_lru, ln0_g, ln0_b, wo_r, wo_l, ln1_g, ln1_b, wrt_hi, wrt_lo, brt)


SC_WINDOW = 32


def _sc_gather(table, idx):
    info = plsc.get_sparse_core_info()
    n_workers = info.num_cores * info.num_subcores
    n = idx.shape[0]
    per_worker = n // n_workers
    n_win = per_worker // SC_WINDOW
    assert n % (n_workers * SC_WINDOW * 2) == 0
    mesh = plsc.VectorSubcoreMesh(core_axis_name="c", subcore_axis_name="s")

    @functools.partial(
        pl.kernel, mesh=mesh,
        out_type=jax.ShapeDtypeStruct((n, SLABS, LANE), F32),
        scratch_types=[pltpu.VMEM((SC_WINDOW,), jnp.int32), pltpu.VMEM((SC_WINDOW,), jnp.int32),
                       pltpu.VMEM((SC_WINDOW, SLABS, LANE), F32), pltpu.VMEM((SC_WINDOW, SLABS, LANE), F32),
                       pltpu.SemaphoreType.DMA, pltpu.SemaphoreType.DMA],
        name="sc_row_gather",
    )
    def gather_kernel(table_hbm, idx_hbm, out_hbm, idx_a, idx_b, rows_a, rows_b, sem_a, sem_b):
        worker = lax.axis_index("s") * info.num_cores + lax.axis_index("c")
        base = worker * per_worker
        bufs = ((idx_a, rows_a, sem_a), (idx_b, rows_b, sem_b))

        def start(w, buf):
            idx_v, rows_v, sem = buf
            pltpu.sync_copy(idx_hbm.at[pl.ds(base + w * SC_WINDOW, SC_WINDOW)], idx_v)
            pltpu.async_copy(table_hbm.at[idx_v], rows_v, sem)

        def finish(w, buf):
            idx_v, rows_v, sem = buf
            pltpu.make_async_copy(table_hbm.at[idx_v], rows_v, sem).wait()
            pltpu.sync_copy(rows_v, out_hbm.at[pl.ds(base + w * SC_WINDOW, SC_WINDOW)])

        start(0, bufs[0])

        @pl.loop(0, n_win, step=2)
        def _(w):
            start(w + 1, bufs[1])
            finish(w, bufs[0])

            @pl.when(w + 2 < n_win)
            def _():
                start(w + 2, bufs[0])
            finish(w + 1, bufs[1])

    return gather_kernel(table, idx)


SC_LANES = 16


def _sc_invert_slots(dest, n_slots):
    n_asg = dest.shape[0]
    assert n_asg & (n_asg - 1) == 0 and n_asg % SC_LANES == 0 and n_slots % SC_LANES == 0
    info = plsc.get_sparse_core_info()
    mesh = plsc.VectorSubcoreMesh(core_axis_name="c", subcore_axis_name="s")

    @functools.partial(
        pl.kernel, mesh=mesh,
        out_type=jax.ShapeDtypeStruct((n_slots,), jnp.int32),
        scratch_types=[pltpu.VMEM((n_asg,), jnp.int32), pltpu.VMEM((n_slots,), jnp.int32)],
        compiler_params=pltpu.CompilerParams(needs_layout_passes=False),
        name="sc_invert_slots",
    )
    def invert_kernel(dest_hbm, out_hbm, dest_v, out_v):
        worker = lax.axis_index("s") * info.num_cores + lax.axis_index("c")

        @pl.when(worker == 0)
        def _():
            pltpu.sync_copy(dest_hbm, dest_v)
            lane = lax.iota(jnp.int32, SC_LANES)

            @pl.loop(0, n_slots // SC_LANES)
            def _(c):
                out_v[pl.ds(c * SC_LANES, SC_LANES)] = jnp.bitwise_and(c * SC_LANES + lane, n_asg - 1)

            @pl.loop(0, n_asg // SC_LANES)
            def _(c):
                plsc.store_scatter(out_v, [dest_v[pl.ds(c * SC_LANES, SC_LANES)]], c * SC_LANES + lane)

            pltpu.sync_copy(out_v, out_hbm)

    return invert_kernel(dest)


def _moe_rows_kernel(te_ref, nv_ref, x_ref, wg_ref, wu_ref, wd_ref, o_ref, wgb_ref, wub_ref, wdb_ref):
    i = pl.program_id(0)
    e = te_ref[i]
    e_prev = te_ref[jnp.maximum(i - 1, 0)]

    @pl.when((i == 0) | (e != e_prev))
    def _():
        wgb_ref[...] = wg_ref[0].astype(BF16)
        wub_ref[...] = wu_ref[0].astype(BF16)
        wdb_ref[...] = wd_ref[0].astype(BF16)

    @pl.when(i < nv_ref[0])
    def _():
        xb = jnp.concatenate(_load_token_slabs(x_ref, MOE_TILE), axis=1).astype(BF16)
        hg = _dot(xb, wgb_ref[...])
        hu = _dot(xb, wub_ref[...])
        mid = (hg * jax.nn.sigmoid(hg) * hu).astype(BF16)
        _store_token_tiles(o_ref, _dot(mid, wdb_ref[...]))

    @pl.when(i >= nv_ref[0])
    def _():
        o_ref[...] = jnp.zeros_like(o_ref)


def _moe_rows_into_kernel(te_ref, nv_ref, x_ref, wg_ref, wu_ref, wd_ref, prev_ref, o_ref, *scratch):
    del prev_ref
    _moe_rows_kernel(te_ref, nv_ref, x_ref, wg_ref, wu_ref, wd_ref, o_ref, *scratch)


def _moe_experts_rows(xbuf, tile_expert, n_valid, w_gate, w_up, w_down, total_tiles, tile_offset, prev=None):
    D = D_MODEL
    n_tiles = xbuf.shape[0] // (MOE_TILE * SLABS)
    tiles = pl.BlockSpec((MOE_TILE * SLABS, LANE), lambda i, te, nv: (jnp.maximum(jnp.minimum(i, nv[0] - 1), 0), 0))
    in_specs = [tiles,
                pl.BlockSpec((1, D, D_EXPERT), lambda i, te, nv: (te[i], 0, 0)),
                pl.BlockSpec((1, D, D_EXPERT), lambda i, te, nv: (te[i], 0, 0)),
                pl.BlockSpec((1, D_EXPERT, D), lambda i, te, nv: (te[i], 0, 0))]
    operands = [tile_expert, n_valid, xbuf, w_gate, w_up, w_down]
    if prev is not None:
        in_specs.append(pl.BlockSpec(memory_space=pl.ANY))
        operands.append(prev)
    grid_spec = pltpu.PrefetchScalarGridSpec(
        num_scalar_prefetch=2,
        grid=(n_tiles,),
        in_specs=in_specs,
        out_specs=pl.BlockSpec((MOE_TILE * SLABS, LANE), lambda i, te, nv: (i + tile_offset, 0)),
        scratch_shapes=[pltpu.VMEM((D, D_EXPERT), BF16), pltpu.VMEM((D, D_EXPERT), BF16),
                        pltpu.VMEM((D_EXPERT, D), BF16)],
    )
    return pl.pallas_call(
        _moe_rows_kernel if prev is None else _moe_rows_into_kernel,
        out_shape=jax.ShapeDtypeStruct((total_tiles * MOE_TILE * SLABS, LANE), F32),
        grid_spec=grid_spec,
        input_output_aliases={} if prev is None else {len(operands) - 1: 0},
        compiler_params=_cparams(("arbitrary",)),
        name="moe_experts",
    )(*operands)


def _row_copy(src_hbm, src_row, dst_ref, dst_row, sem):
    return pltpu.make_async_copy(src_hbm.at[pl.ds(src_row * SLABS, SLABS), :],
                                 dst_ref.at[pl.ds(dst_row * SLABS, SLABS), :], sem)


def _wait_tiles(src_hbm, dst_ref, sem):
    pltpu.make_async_copy(src_hbm.at[pl.ds(0, dst_ref.shape[0]), :], dst_ref, sem).wait()


def _combine_kernel(d_cur_ref, d_nxt_ref, h_ref, gate_ref, g_ref, b_ref, y_hbm, o_ref, ybuf, sem):
    i = pl.program_id(0)
    n = pl.num_programs(0)
    slot = lax.rem(i, 2)

    def start_gather(d_ref, s):
        def body(tt, carry):
            for u in range(DMA_UNROLL // TOP_K):
                t = tt * (DMA_UNROLL // TOP_K) + u
                for k in range(TOP_K):
                    _row_copy(y_hbm, d_ref[TOP_K * t + k], ybuf.at[s, k], t, sem.at[s]).start(priority=k % 2)
            return carry
        lax.fori_loop(0, COMBINE_TILE * TOP_K // DMA_UNROLL, body, 0)

    @pl.when(i == 0)
    def _():
        start_gather(d_cur_ref, 0)

    @pl.when(i + 1 < n)
    def _():
        start_gather(d_nxt_ref, 1 - slot)

    for k in range(TOP_K):
        _wait_tiles(y_hbm, ybuf.at[slot, k], sem.at[slot])

    gate = gate_ref[...]
    tm = COMBINE_TILE
    ga = jnp.broadcast_to(gate[:, 0:1], (tm, LANE))
    gb = jnp.broadcast_to(gate[:, 1:2], (tm, LANE))
    hs = _load_token_slabs(h_ref, tm)
    ya = _load_token_slabs(ybuf.at[slot, 0], tm)
    yb = _load_token_slabs(ybuf.at[slot, 1], tm)
    z = [DEEPNORM_ALPHA * h + (ga * a + gb * b) for h, a, b in zip(hs, ya, yb)]
    inv_d = 1.0 / D_MODEL
    mu = sum(jnp.sum(t, axis=1, keepdims=True) for t in z) * inv_d
    zc = [t - mu for t in z]
    var = sum(jnp.sum(t * t, axis=1, keepdims=True) for t in zc) * inv_d
    rstd = lax.rsqrt(var + LN_EPS)
    for s in range(SLABS):
        cols = slice(s * LANE, (s + 1) * LANE)
        o_ref[:, cols] = zc[s] * rstd * g_ref[:, cols] + b_ref[:, cols]


def _combine(h1, ybuf, dest, gates, ln2_g, ln2_b):
    D = D_MODEL
    M = h1.shape[0] // SLABS
    tm = COMBINE_TILE
    n = M // tm
    smem_tile = lambda f: pl.BlockSpec((TOP_K * tm,), f, memory_space=pltpu.SMEM)
    return pl.pallas_call(
        _combine_kernel,
        out_shape=jax.ShapeDtypeStruct((M, D), F32),
        grid=(n,),
        in_specs=[smem_tile(lambda i: (i,)), smem_tile(lambda i: (jnp.minimum(i + 1, n - 1),)),
                  pl.BlockSpec((tm * SLABS, LANE), lambda i: (i, 0)), pl.BlockSpec((tm, TOP_K), lambda i: (i, 0)),
                  _full(ln2_g), _full(ln2_b), pl.BlockSpec(memory_space=pl.ANY)],
        out_specs=pl.BlockSpec((tm, D), lambda i: (i, 0)),
        scratch_shapes=[pltpu.VMEM((2, TOP_K, tm * SLABS, LANE), F32), pltpu.SemaphoreType.DMA((2,))],
        compiler_params=_cparams(("arbitrary",)),
        name="combine",
    )(dest, dest, h1, gates, ln2_g, ln2_b, ybuf)


def _routing_plan(route):
    M = route.shape[0]
    eid = route[:, :TOP_K].astype(jnp.int32).reshape(-1)
    gates = route[:, TOP_K:2 * TOP_K]
    A = M * TOP_K
    onehot = (eid[:, None] == jnp.arange(N_EXPERTS, dtype=eid.dtype)[None, :]).astype(jnp.int32)
    csum = jnp.cumsum(onehot, axis=0)
    rank = jnp.sum(csum * onehot, axis=1) - 1
    counts = csum[-1]
    pcounts = (counts + MOE_TILE - 1) // MOE_TILE * MOE_TILE
    pends = jnp.cumsum(pcounts)
    pstarts = pends - pcounts
    dest = (jnp.sum(onehot * pstarts[None, :], axis=1) + rank).astype(jnp.int32)
    n_tiles = (A + N_EXPERTS * (MOE_TILE - 1) + MOE_TILE - 1) // MOE_TILE
    n_valid = (pends[-1] // MOE_TILE).astype(jnp.int32)
    tile_start = jnp.minimum(jnp.arange(n_tiles, dtype=jnp.int32) * MOE_TILE, pends[-1] - 1)
    tile_expert = jnp.sum((pends[None, :] <= tile_start[:, None]).astype(jnp.int32), axis=1)
    tile_expert = jnp.minimum(tile_expert, N_EXPERTS - 1).astype(jnp.int32)
    return gates, dest, n_tiles * MOE_TILE, tile_expert, n_valid.reshape(1)


def kernel(x, meta, ln0_g, ln0_b, w_in, mu_shift, w0, w_decay_up, a0, w_a_up, w_g_up, k_k, k_a, r_k, gn_g, gn_b, conv_w, conv_b, w_rg, b_rg, w_ig, b_ig, lru_lambda, w_out, ln1_g, ln1_b, w_router_grp, b_router_grp, w_router_exp, b_router_exp, w_exp_gate, w_exp_up, w_exp_down, ln2_g, ln2_b):
    B, T, D = x.shape
    assert D == D_MODEL and T % 512 == 0 and w_in.shape[0] == 1
    row = lambda p: p.reshape(1, -1).astype(F32)
    n_rw = 3 * RWKV_W
    w_in0 = w_in[0]

    def slots(p):
        pad = lambda a, n: jnp.pad(a, [(0, 0)] * (a.ndim - 1) + [(0, n - a.shape[-1])])
        zw = p[..., n_rw:n_rw + DECAY_RANK]
        za = p[..., n_rw + DECAY_RANK:n_rw + DECAY_RANK + AAA_RANK]
        zg = p[..., n_rw + DECAY_RANK + AAA_RANK:n_rw + DECAY_RANK + AAA_RANK + GATE_RANK]
        return jnp.concatenate([p[..., :n_rw], pad(zw, LANE), pad(za, LANE), pad(zg, ZG_SLOT)], axis=-1)

    rwkv_cols = n_rw + DECAY_RANK + AAA_RANK + GATE_RANK
    w_r = slots(w_in0[:, :rwkv_cols]).astype(BF16)
    w_l = w_in0[:, rwkv_cols:].astype(BF16)
    ur, ul, ur_t, ul_t = _in_projection(x, meta, row(ln0_g), row(ln0_b), w_r, w_l)

    pad_rows = lambda a, n: jnp.pad(a, ((0, n - a.shape[0]), (0, 0)))
    rwkv_params = (slots(mu_shift[0][None, :]).astype(F32), row(w0[0]), pad_rows(w_decay_up[0], LANE).astype(BF16),
                   row(a0[0]), pad_rows(w_a_up[0], LANE).astype(BF16), pad_rows(w_g_up[0], ZG_SLOT).astype(BF16),
                   row(k_k[0]), row(k_a[0]), row(r_k[0]), row(gn_g[0]), row(gn_b[0]))
    y_rwkv = _rwkv_pipe_mixer(ur, ur_t, rwkv_params)

    blockdiag = lambda w: jax.scipy.linalg.block_diag(*[w[i] for i in range(LRU_BLOCKS)]).astype(BF16)
    lru_params = (conv_w[0], row(conv_b[0]), blockdiag(w_rg[0]), row(b_rg[0]), blockdiag(w_ig[0]), row(b_ig[0]),
                  row(lru_lambda[0]))
    y_lru = _lru_mixer(ul, ul_t, lru_params)

    w_rt = jnp.concatenate([w_router_grp[0], w_router_exp[0]], axis=1)
    w_rt = jnp.pad(w_rt, ((0, 0), (0, LANE - w_rt.shape[1])))
    wrt_hi = w_rt.astype(BF16)
    wrt_lo = (w_rt - wrt_hi.astype(F32)).astype(BF16)
    b_rt = jnp.concatenate([b_router_grp[0], b_router_exp[0]])
    b_rt = jnp.pad(b_rt, (0, LANE - b_rt.shape[0])).reshape(1, LANE)
    wo = w_out[0].astype(BF16)
    h1, route = _out_projection(x, y_rwkv, y_lru, row(ln0_g), row(ln0_b), wo[:RWKV_W], wo[RWKV_W:],
                                row(ln1_g[0]), row(ln1_b[0]), wrt_hi, wrt_lo, b_rt)

    M = B * T
    h1 = h1.reshape(M * SLABS, LANE)
    gates, dest, n_slots, tile_expert, n_valid = _routing_plan(route.reshape(M, LANE))
    row_asg = _sc_invert_slots(dest, n_slots)
    src_tok = lax.shift_right_logical(row_asg, 1)
    n_tiles = n_slots // MOE_TILE
    half = n_tiles // 2
    h1_tiles = h1.reshape(M, SLABS, LANE)
    ybuf = None
    for lo, hi in ((0, half), (half, n_tiles)):
        xbuf = _sc_gather(h1_tiles, src_tok[lo * MOE_TILE:hi * MOE_TILE]).reshape(-1, LANE)
        nv = jnp.clip(n_valid - lo, 0, hi - lo)
        ybuf = _moe_experts_rows(xbuf, tile_expert[lo:hi], nv, w_exp_gate[0], w_exp_up[0], w_exp_down[0],
                                 total_tiles=n_tiles, tile_offset=lo, prev=ybuf)
    out = _combine(h1, ybuf, dest, gates, row(ln2_g[0]), row(ln2_b[0]))
    return out.reshape(B, T, D)
```

```python
import functools
import math

import jax
import jax.numpy as jnp
from jax import lax
from jax.experimental import pallas as pl
from jax.experimental.pallas import tpu as pltpu
from jax.experimental.pallas import tpu_sc as plsc

F32 = jnp.float32
BF16 = jnp.bfloat16

D_MODEL = 1024
N_META = 16
RWKV_W = 512
RWKV_HEAD = 64
DECAY_RANK = 64
AAA_RANK = 64
GATE_RANK = 160
LRU_W = 512
LRU_BLOCKS = 8
CONV_WIDTH = 4
LRU_C = 8.0
N_GROUPS = 4
EXPERTS_PER_GROUP = 8
N_EXPERTS = N_GROUPS * EXPERTS_PER_GROUP
TOP_K = 2
D_EXPERT = 512
LN_EPS = 1e-5
GN_EPS = 64e-5
DEEPNORM_ALPHA = 2.0 ** 0.25

LANE = 128
OFF_R, OFF_K, OFF_V = 0, RWKV_W, 2 * RWKV_W
OFF_ZW = 3 * RWKV_W
OFF_ZA = OFF_ZW + LANE
OFF_ZG = OFF_ZA + LANE
ZG_SLOT = 2 * LANE
UR_W = OFF_ZG + ZG_SLOT
UL_W = 2 * LRU_W

TAIL = 256
CHUNK = 64
HEADS_PER_GROUP = 4
GW = HEADS_PER_GROUP * RWKV_HEAD
N_HGROUPS = RWKV_W // GW
LRU_TILE = TAIL
MOE_TILE = 256
COMBINE_TILE = 512
DMA_UNROLL = 8
V7X_VMEM_BYTES = 64 * 1024 * 1024
VMEM_LIMIT = V7X_VMEM_BYTES - 8 * 1024 * 1024


def _cparams(sem):
    return pltpu.CompilerParams(dimension_semantics=sem, vmem_limit_bytes=VMEM_LIMIT)


def _layer_norm(x, g, b):
    mu = jnp.mean(x, -1, keepdims=True)
    xc = x - mu
    var = jnp.mean(xc * xc, -1, keepdims=True)
    return xc * lax.rsqrt(var + LN_EPS) * g + b


def _dot(a, b):
    return jnp.dot(a, b, preferred_element_type=F32)


def _dot_nt(a, b):
    return lax.dot_general(a, b, (((1,), (1,)), ((), ())), preferred_element_type=F32)


def _dot_tn(a, b):
    return lax.dot_general(a, b, (((0,), (0,)), ((), ())), preferred_element_type=F32)


def _full(a):
    return pl.BlockSpec(a.shape, lambda *_: (0,) * a.ndim)


def _inproj_kernel(x_ref, g_ref, b_ref, wr_ref, wl_ref, ur_ref, ul_ref):
    h = _layer_norm(x_ref[0], g_ref[...], b_ref[...]).astype(BF16)
    ur_ref[0] = _dot(h, wr_ref[...])
    ul_ref[0] = _dot(h, wl_ref[...])


def _inproj_tail_kernel(x_ref, g_ref, b_ref, wr_ref, wl_ref, ur_ref, ul_ref):
    h = _layer_norm(x_ref[...], g_ref[...], b_ref[...]).astype(BF16)
    rows = lax.broadcasted_iota(jnp.int32, (TAIL, 1), 0)
    valid = (rows >= TAIL - N_META).astype(F32)
    ur_ref[...] = _dot(h, wr_ref[...]) * valid
    ul_ref[...] = _dot(h, wl_ref[...]) * valid


def _in_projection(x, meta, ln0_g, ln0_b, w_r, w_l):
    B, T, D = x.shape
    tm = 512
    ur, ul = pl.pallas_call(
        _inproj_kernel,
        out_shape=(jax.ShapeDtypeStruct((B, T, UR_W), F32), jax.ShapeDtypeStruct((B, T, UL_W), F32)),
        grid=(B, T // tm),
        in_specs=[pl.BlockSpec((1, tm, D), lambda b, i: (b, i, 0)), _full(ln0_g), _full(ln0_b), _full(w_r), _full(w_l)],
        out_specs=(pl.BlockSpec((1, tm, UR_W), lambda b, i: (b, i, 0)),
                   pl.BlockSpec((1, tm, UL_W), lambda b, i: (b, i, 0))),
        compiler_params=_cparams(("parallel", "parallel")),
        name="inproj",
    )(x, ln0_g, ln0_b, w_r, w_l)
    tail_x = jnp.concatenate([jnp.zeros((TAIL - N_META, D), F32), meta.astype(F32)], axis=0)
    ur_t, ul_t = pl.pallas_call(
        _inproj_tail_kernel,
        out_shape=(jax.ShapeDtypeStruct((TAIL, UR_W), F32), jax.ShapeDtypeStruct((TAIL, UL_W), F32)),
        grid=(1,),
        in_specs=[_full(tail_x), _full(ln0_g), _full(ln0_b), _full(w_r), _full(w_l)],
        out_specs=(pl.BlockSpec((TAIL, UR_W), lambda i: (0, 0)), pl.BlockSpec((TAIL, UL_W), lambda i: (0, 0))),
        compiler_params=_cparams(("arbitrary",)),
        name="inproj_tail",
    )(tail_x, ln0_g, ln0_b, w_r, w_l)
    return ur, ul, ur_t, ul_t


def _rwkv_pipe_kernel(u_ref, ut_ref, mu_ref, w0_ref, wdu_ref, a0_ref, wau_ref, wgu_ref, kk_ref, ka_ref, rk_ref,
                      gng_ref, gnb_ref, bones_ref, bm_ref, eye_ref, msl_ref, mil_ref,
                      m8_ref, m16_ref, m32_ref, m64_ref, y_ref,
                      s_ref, prev_ref, y0_s, q_s, mc_s, nc_s, we_s, bonus_s, g_s, yraw_s):
    s_id = pl.program_id(0)
    nb = u_ref.shape[0]
    blk = u_ref.shape[1]
    npc = blk // CHUNK
    seq_rows = lambda q: slice(q * CHUNK, (q + 1) * CHUNK)
    w_slot = lax.rem(s_id, 2)
    r_slot = 1 - w_slot

    @pl.when(s_id == 0)
    def _():
        s_ref[...] = jnp.zeros_like(s_ref)
        prev_ref[...] = jnp.zeros_like(prev_ref)
        for ref in (y0_s, q_s, mc_s, nc_s, we_s, bonus_s, g_s):
            ref[1] = jnp.zeros(ref.shape[1:], ref.dtype)

    b16 = lambda t: t.astype(BF16)
    bones = bones_ref[...]
    head_sum = lambda t: _dot(b16(t), bones)
    bm = bm_ref[...]
    bm16 = b16(bm)
    tile4 = lambda t: jnp.concatenate([t] * HEADS_PER_GROUP, axis=0)
    fold4 = lambda t: sum(t[i * CHUNK:(i + 1) * CHUNK] for i in range(HEADS_PER_GROUP))
    bd = lambda t: tile4(b16(t)) * bm16

    chains = [(b, hg) for b in range(nb) for hg in range(N_HGROUPS)]
    states = {c: s_ref[c[0], c[1]] for c in chains}

    def recurrent_chunk(j):
        for b, hg in chains:
            sl = slice(hg * GW, (hg + 1) * GW)
            q = b * npc + j
            rq = seq_rows(q)
            s = states[(b, hg)]
            yraw_s[rq, sl] = y0_s[r_slot, rq, sl] + _dot_nt(q_s[r_slot, rq, sl], bd(s))
            states[(b, hg)] = (s * we_s[r_slot, q * 8:q * 8 + 1, sl] + _dot(b16(s), bd(mc_s[r_slot, rq, sl]))
                               + nc_s[r_slot, rq, sl])

    prepared = {}

    def prepare(b):
        lo = b * blk
        u = jnp.where(s_id == 0, ut_ref[...], u_ref[b])
        row = lax.broadcasted_iota(jnp.int32, u.shape, 0)
        u_prev = jnp.where(row == 0, prev_ref[b:b + 1, :], pltpu.roll(u, 1, 0))
        prev_ref[b:b + 1, :] = u[blk - 1:blk, :]
        x = u + (u_prev - u) * mu_ref[...]
        r = x[:, OFF_R:OFF_R + RWKV_W]
        k = x[:, OFF_K:OFF_K + RWKV_W]
        v = x[:, OFF_V:OFF_V + RWKV_W]
        zw = x[:, OFF_ZW:OFF_ZW + LANE]
        za = x[:, OFF_ZA:OFF_ZA + LANE]
        zg = x[:, OFF_ZG:OFF_ZG + ZG_SLOT]
        yield
        z = w0_ref[...] + _dot(b16(jnp.tanh(zw)), wdu_ref[...])
        logw = -math.exp(-0.5) * jax.nn.sigmoid(z)
        a = jax.nn.sigmoid(a0_ref[...] + _dot(b16(za), wau_ref[...]))
        g = _dot(b16(jax.nn.sigmoid(zg)), wgu_ref[...])
        kk = k * kk_ref[...]
        kk = kk / jnp.maximum(jnp.sqrt(head_sum(kk * kk)), 1e-12)
        k = k * (1.0 + (a - 1.0) * ka_ref[...])
        kka = kk * a
        bonus_s[w_slot, lo:lo + blk, :] = head_sum(r * k * rk_ref[...]) * v
        g_s[w_slot, lo:lo + blk, :] = g
        yield
        cl = logw
        row_in_chunk = jnp.bitwise_and(lax.broadcasted_iota(jnp.int32, cl.shape, 0), CHUNK - 1)
        d = 1
        while d < CHUNK:
            cl = cl + jnp.where(row_in_chunk >= d, pltpu.roll(cl, d, 0), 0.0)
            d *= 2
        yield
        cl_last = jnp.concatenate(
            [jnp.broadcast_to(cl[(j + 1) * CHUNK - 1:(j + 1) * CHUNK, :], (CHUNK, RWKV_W)) for j in range(npc)], axis=0)
        e_neg = jnp.exp(-cl)
        e_end = jnp.exp(cl_last - cl)
        w_end = jnp.exp(cl_last)
        for j in range(npc):
            q = b * npc + j
            we_s[w_slot, q * 8:(q + 1) * 8, :] = w_end[j * CHUNK:j * CHUNK + 8, :]
        prepared[b] = dict(rt=r * jnp.exp(cl), kt=k * e_neg, at=-kk * jnp.exp(cl - logw), bt=kka * e_neg,
                           kw=k * e_end, bw=kka * e_end, v=v)
        yield

    eye = eye_ref[...]
    msl = msl_ref[...]
    mil = mil_ref[...]
    rows2 = lambda x, y: jnp.concatenate([x, y], axis=0)
    cols2 = lambda x, y: jnp.concatenate([x, y], axis=1)
    each = lambda f, *ls: [f(*xs) for xs in zip(*ls)]

    def solve(b, p):
        probs = [(slice(j * CHUNK, (j + 1) * CHUNK), slice(hg * GW, (hg + 1) * GW))
                 for j in range(npc) for hg in range(N_HGROUPS)]
        pick = lambda t: [t[rq, sl] for rq, sl in probs]
        at_w, rt_w, v_w = pick(p["at"]), pick(p["rt"]), pick(p["v"])
        lhs = each(lambda x, y: b16(rows2(x, y)), at_w, rt_w)
        ab = each(_dot_nt, lhs, each(bd, pick(p["bt"])))
        ak = each(_dot_nt, lhs, each(bd, pick(p["kt"])))
        yield
        a_ab = each(lambda t: t[:CHUNK] * msl, ab)
        a_rb = each(lambda t: b16(t[CHUNK:] * mil), ab)
        a_xk = each(lambda t: b16(rows2(t[:CHUNK] * msl, t[CHUNK:] * mil)), ak)
        a0 = each(lambda t: b16(t * m8_ref[...]), a_ab)
        a2 = each(lambda t: b16(_dot(t, bd(t))), a0)
        yield
        a4 = each(lambda t: b16(_dot(t, bd(t))), a2)
        p1 = each(lambda t: eye + t.astype(F32), a0)
        p1 = each(lambda q, t: q + _dot(b16(q), bd(t)), p1, a2)
        yield
        tt = each(lambda q, t: q + _dot(b16(q), bd(t)), p1, a4)
        yield
        for m_ref in (m16_ref, m32_ref, m64_ref):
            tb = each(b16, tt)
            off = each(lambda t: b16(t * m_ref[...]), a_ab)
            half = each(lambda x, y: b16(_dot(x, bd(y))), tb, off)
            yield
            tt = each(lambda t, x, y: t + _dot(x, bd(y)), tt, half, tb)
            yield
        tb = each(b16, tt)
        xv = each(lambda x, y: _dot(x, bd(y)), a_xk, v_w)
        yield
        u0 = each(lambda x, y: _dot(x, bd(y[:CHUNK])), tb, xv)
        ta = each(lambda x, y: _dot(x, bd(y)), tb, at_w)
        yield
        y0 = each(lambda x, y, z: _dot(x, bd(y)) + z[CHUNK:], a_rb, u0, xv)
        qq = each(lambda x, y, z: x + _dot(y, bd(z)), rt_w, a_rb, ta)
        yield
        left = each(lambda x, y, z: b16(rows2(cols2(x, y), cols2(jnp.zeros_like(z), z))), ta, u0, v_w)
        right = each(lambda x, y: b16(rows2(x, y)), pick(p["bw"]), pick(p["kw"]))
        mn = each(_dot_tn, left, right)
        for i, (rq, sl) in enumerate(probs):
            rows = slice(b * blk + rq.start, b * blk + rq.stop)
            y0_s[w_slot, rows, sl] = y0[i]
            q_s[w_slot, rows, sl] = b16(qq[i])
            mc_s[w_slot, rows, sl] = b16(fold4(mn[i][:GW] * bm))
            nc_s[w_slot, rows, sl] = fold4(mn[i][GW:] * bm)
        yield

    first = prepare(0)
    for j in range(npc):
        recurrent_chunk(j)
        if j * 4 // npc != (j + 1) * 4 // npc or j == npc - 1:
            for _ in range((j + 1) * 4 // npc - j * 4 // npc):
                next(first, None)
    for _ in first:
        pass
    for (b, hg), s in states.items():
        s_ref[b, hg] = s

    y = yraw_s[...]
    inv_n = 1.0 / RWKV_HEAD
    ym = head_sum(y) * inv_n
    yc = y - ym
    yv = head_sum(yc * yc) * inv_n
    yn = yc * lax.rsqrt(yv + GN_EPS) * gng_ref[...] + gnb_ref[...]
    y_ref[...] = ((yn + bonus_s[r_slot]) * g_s[r_slot]).astype(y_ref.dtype).reshape(y_ref.shape)

    for b in range(nb):
        solver = solve(b, prepared[b])
        nxt = prepare(b + 1) if b + 1 < nb else iter(())
        for level, _ in enumerate(solver):
            if level % 4 == 3:
                next(nxt, None)
        for _ in nxt:
            pass


def _rwkv_masks():
    f = lambda m: m.astype(F32)
    i = jnp.arange(GW)[:, None]
    j = jnp.arange(GW)[None, :]
    bm = f((i // RWKV_HEAD) == (j // RWKV_HEAD))
    t = jnp.arange(CHUNK)[:, None]
    s = (jnp.arange(GW) % CHUNK)[None, :]
    same = lambda n: (t // n) == (s // n)
    msl = f(t > s)
    mil = f(t >= s)
    m8 = f(same(8))
    m16 = f(same(16) & ~same(8))
    m32 = f(same(32) & ~same(16))
    m64 = f(~same(32))
    eye = f(t == s)
    hi = jnp.arange(RWKV_W)
    bones = ((hi[:, None] // RWKV_HEAD) == (hi[None, :] // RWKV_HEAD)).astype(BF16)
    return bones, bm, eye, msl, mil, m8, m16, m32, m64


def _rwkv_pipe_mixer(ur, ur_tail, params):
    B, T, _ = ur.shape
    blk = TAIL
    assert T % blk == 0 and blk % CHUNK == 0 and CHUNK == RWKV_HEAD
    n_blocks = T // blk
    rows = B * blk
    consts = _rwkv_masks()
    in_map = lambda s: (0, jnp.clip(s - 1, 0, n_blocks - 1), 0)
    out_map = lambda s: (0, jnp.clip(s - 2, 0, n_blocks - 1), 0)
    slot2 = lambda w, dt: pltpu.VMEM((2, rows, w), dt)
    return pl.pallas_call(
        _rwkv_pipe_kernel,
        out_shape=jax.ShapeDtypeStruct((B, T, RWKV_W), BF16),
        grid=(n_blocks + 2,),
        in_specs=[pl.BlockSpec((B, blk, UR_W), in_map), _full(ur_tail)]
                 + [_full(p) for p in params] + [_full(m) for m in consts],
        out_specs=pl.BlockSpec((B, blk, RWKV_W), out_map),
        scratch_shapes=[pltpu.VMEM((B, N_HGROUPS, CHUNK, GW), F32), pltpu.VMEM((B, UR_W), F32),
                        slot2(RWKV_W, F32), slot2(RWKV_W, BF16), slot2(RWKV_W, BF16), slot2(RWKV_W, F32),
                        pltpu.VMEM((2, 8 * rows // CHUNK, RWKV_W), F32), slot2(RWKV_W, F32), slot2(RWKV_W, F32),
                        pltpu.VMEM((rows, RWKV_W), F32)],
        compiler_params=_cparams(("arbitrary",)),
        name="rwkv7",
    )(ur, ur_tail, *params, *consts)


def _gelu_tanh(x):
    return 0.5 * x * (1.0 + jnp.tanh(math.sqrt(2.0 / math.pi) * (x + 0.044715 * (x * x * x))))


LRU_CARRY = 8


def _lru_kernel(u_ref, ut_ref, cw_ref, cb_ref, wrg_ref, brg_ref, wig_ref, big_ref, lam_ref, y_ref,
                xs_ref, hprev_ref):
    c = pl.program_id(0)
    nb = u_ref.shape[0]
    nrow = nb * LRU_TILE

    @pl.when(c == 0)
    def _():
        xs_ref[...] = jnp.zeros_like(xs_ref)
        hprev_ref[...] = jnp.zeros_like(hprev_ref)

    u_x = u_ref[...].reshape(nrow, UL_W)
    u = jnp.where(c == 0, jnp.concatenate([ut_ref[...]] * nb, axis=0), u_x)
    xl = u[:, :LRU_W]
    gl = u[:, LRU_W:]
    row = jnp.bitwise_and(lax.broadcasted_iota(jnp.int32, (nrow, LRU_W), 0), LRU_TILE - 1)
    in_group = jnp.bitwise_and(row, 7)
    roll_in_group = lambda t, d: pltpu.roll(t.reshape(t.shape[0] // 8, 8, LRU_W), d, 1).reshape(t.shape)
    xl_prev = jnp.concatenate(
        [p for b in range(nb) for p in (xs_ref[b], xl[b * LRU_TILE:(b + 1) * LRU_TILE - 8])], axis=0)
    xc = cb_ref[...] + cw_ref[CONV_WIDTH - 1:CONV_WIDTH, :] * xl
    for d in range(1, CONV_WIDTH):
        tap = jnp.where(in_group >= d, roll_in_group(xl, d), roll_in_group(xl_prev, d))
        xc = xc + cw_ref[CONV_WIDTH - 1 - d:CONV_WIDTH - d, :] * tap
    for b in range(nb):
        xs_ref[b] = xl[(b + 1) * LRU_TILE - 8:(b + 1) * LRU_TILE]

    xcb = xc.astype(BF16)
    gate_r = jax.nn.sigmoid(_dot(xcb, wrg_ref[...]) + brg_ref[...])
    gate_i = jax.nn.sigmoid(_dot(xcb, wig_ref[...]) + big_ref[...])
    lam = lam_ref[...]
    log_sig = -(jnp.maximum(-lam, 0.0) + jnp.log1p(jnp.exp(-jnp.abs(lam))))
    log_a = LRU_C * gate_r * log_sig
    a = jnp.exp(log_a)
    mult = jnp.sqrt(jnp.maximum(1.0 - jnp.exp(2.0 * log_a), 0.0))
    b = mult * gate_i * xc
    b = jnp.where((c == 0) & (row < LRU_TILE - N_META), 0.0, b)

    d = 1
    while d < 8:
        keep = in_group >= d
        a_sh = jnp.where(keep, roll_in_group(a, d), 1.0)
        b_sh = jnp.where(keep, roll_in_group(b, d), 0.0)
        b = a * b_sh + b
        a = a * a_sh
        d *= 2
    groups = []
    for bi in range(nb):
        carry = hprev_ref[bi:bi + 1, :]
        for gi in range(LRU_TILE // 8):
            lo = bi * LRU_TILE + gi * 8
            hg = b[lo:lo + 8] + a[lo:lo + 8] * carry
            carry = hg[7:8, :]
            groups.append(hg)
        hprev_ref[bi:bi + 1, :] = carry
    h = jnp.concatenate(groups, axis=0)
    y_ref[...] = (h * _gelu_tanh(gl)).astype(y_ref.dtype).reshape(y_ref.shape)


def _lru_mixer(ul, ul_tail, params):
    B, T, _ = ul.shape
    assert TAIL == LRU_TILE
    x_map = lambda c: (0, jnp.maximum(c - 1, 0), 0)
    return pl.pallas_call(
        _lru_kernel,
        out_shape=jax.ShapeDtypeStruct((B, T, LRU_W), BF16),
        grid=(T // LRU_TILE + 1,),
        in_specs=[pl.BlockSpec((B, LRU_TILE, UL_W), x_map), _full(ul_tail)] + [_full(p) for p in params],
        out_specs=pl.BlockSpec((B, LRU_TILE, LRU_W), x_map),
        scratch_shapes=[pltpu.VMEM((B, LRU_CARRY, LRU_W), F32), pltpu.VMEM((B, LRU_W), F32)],
        compiler_params=_cparams(("arbitrary",)),
        name="rglru",
    )(ul, ul_tail, *params)


def _route(lg):
    lane = lax.broadcasted_iota(jnp.int32, lg.shape, 1)
    neg = jnp.float32(-jnp.inf)
    rmax = lambda t: jnp.max(t, axis=1, keepdims=True)
    first = lambda hit: jnp.min(jnp.where(hit, lane, LANE), axis=1, keepdims=True)
    is_grp = lane < N_GROUPS
    gl = jnp.where(is_grp, lg, neg)
    gmax = rmax(gl)
    g_sel = first(gl == gmax)
    p_g = 1.0 / jnp.sum(jnp.where(is_grp, jnp.exp(lg - gmax), 0.0), axis=1, keepdims=True)
    ex = lane - N_GROUPS
    in_grp = (ex >= 0) & (ex < N_EXPERTS) & (jnp.right_shift(ex, 3) == g_sel)
    el = jnp.where(in_grp, lg, neg)
    v1 = rmax(el)
    i1 = first(el == v1)
    el2 = jnp.where(lane == i1, neg, el)
    v2 = rmax(el2)
    i2 = first(el2 == v2)
    t = jnp.exp(v2 - v1)
    gate1 = p_g / (1.0 + t)
    gate2 = p_g * t / (1.0 + t)
    e1 = (i1 - N_GROUPS).astype(F32)
    e2 = (i2 - N_GROUPS).astype(F32)
    return jnp.where(lane == 0, e1, jnp.where(lane == 1, e2, jnp.where(lane == 2, gate1, jnp.where(lane == 3, gate2, 0.0))))


SLABS = D_MODEL // LANE


def _store_token_tiles(ref, val):
    n = val.shape[0]
    for s in range(SLABS):
        ref[pl.ds(s, n, stride=SLABS), :] = val[:, s * LANE:(s + 1) * LANE]


def _load_token_slabs(ref, n):
    return [ref[pl.ds(s, n, stride=SLABS), :] for s in range(SLABS)]


def _outproj_kernel(x_ref, yr_ref, yl_ref, g0_ref, b0_ref, wor_ref, wol_ref, g1_ref, b1_ref,
                    wrt_hi_ref, wrt_lo_ref, brt_ref, h1_ref, rt_ref):
    h0 = _layer_norm(x_ref[0], g0_ref[...], b0_ref[...])
    mix = _dot(yr_ref[0], wor_ref[...]) + _dot(yl_ref[0], wol_ref[...])
    h1 = _layer_norm(DEEPNORM_ALPHA * h0 + mix, g1_ref[...], b1_ref[...])
    _store_token_tiles(h1_ref.at[0], h1)
    hi = h1.astype(BF16)
    lo = (h1 - hi.astype(F32)).astype(BF16)
    w_hi = wrt_hi_ref[...]
    lg = _dot(hi, w_hi) + (_dot(hi, wrt_lo_ref[...]) + _dot(lo, w_hi)) + brt_ref[...]
    rt_ref[0] = _route(lg)


def _out_projection(x, y_rwkv, y_lru, ln0_g, ln0_b, wo_r, wo_l, ln1_g, ln1_b, wrt_hi, wrt_lo, brt):
    B, T, D = x.shape
    tm = 512
    rows = lambda w: pl.BlockSpec((1, tm, w), lambda b, i: (b, i, 0))
    return pl.pallas_call(
        _outproj_kernel,
        out_shape=(jax.ShapeDtypeStruct((B, T * SLABS, LANE), F32), jax.ShapeDtypeStruct((B, T, LANE), F32)),
        grid=(B, T // tm),
        in_specs=[rows(D), rows(RWKV_W), rows(LRU_W), _full(ln0_g), _full(ln0_b), _full(wo_r), _full(wo_l),
                  _full(ln1_g), _full(ln1_b), _full(wrt_hi), _full(wrt_lo), _full(brt)],
        out_specs=(pl.BlockSpec((1, tm * SLABS, LANE), lambda b, i: (b, i, 0)), rows(LANE)),
        compiler_params=_cparams(("parallel", "parallel")),
        name="outproj",
    )(x, y_rwkv, y_lru, ln0_g, ln0_b, wo_r, wo_l, ln1_g, ln1_b, wrt_hi, wrt_lo, brt)


SC_WINDOW = 32


def _sc_gather(table, idx):
    info = plsc.get_sparse_core_info()
    n_workers = info.num_cores * info.num_subcores
    n = idx.shape[0]
    per_worker = n // n_workers
    n_win = per_worker // SC_WINDOW
    assert n % (n_workers * SC_WINDOW * 2) == 0
    mesh = plsc.VectorSubcoreMesh(core_axis_name="c", subcore_axis_name="s")

    @functools.partial(
        pl.kernel, mesh=mesh,
        out_type=jax.ShapeDtypeStruct((n, SLABS, LANE), F32),
        scratch_types=[pltpu.VMEM((SC_WINDOW,), jnp.int32), pltpu.VMEM((SC_WINDOW,), jnp.int32),
                       pltpu.VMEM((SC_WINDOW, SLABS, LANE), F32), pltpu.VMEM((SC_WINDOW, SLABS, LANE), F32),
                       pltpu.SemaphoreType.DMA, pltpu.SemaphoreType.DMA],
        name="sc_row_gather",
    )
    def gather_kernel(table_hbm, idx_hbm, out_hbm, idx_a, idx_b, rows_a, rows_b, sem_a, sem_b):
        worker = lax.axis_index("s") * info.num_cores + lax.axis_index("c")
        base = worker * per_worker
        bufs = ((idx_a, rows_a, sem_a), (idx_b, rows_b, sem_b))

        def start(w, buf):
            idx_v, rows_v, sem = buf
            pltpu.sync_copy(idx_hbm.at[pl.ds(base + w * SC_WINDOW, SC_WINDOW)], idx_v)
            pltpu.async_copy(table_hbm.at[idx_v], rows_v, sem)

        def finish(w, buf):
            idx_v, rows_v, sem = buf
            pltpu.make_async_copy(table_hbm.at[idx_v], rows_v, sem).wait()
            pltpu.sync_copy(rows_v, out_hbm.at[pl.ds(base + w * SC_WINDOW, SC_WINDOW)])

        start(0, bufs[0])

        @pl.loop(0, n_win, step=2)
        def _(w):
            start(w + 1, bufs[1])
            finish(w, bufs[0])

            @pl.when(w + 2 < n_win)
            def _():
                start(w + 2, bufs[0])
            finish(w + 1, bufs[1])

    return gather_kernel(table, idx)


SC_LANES = 16


def _sc_invert_slots(dest, n_slots):
    n_asg = dest.shape[0]
    assert n_asg & (n_asg - 1) == 0 and n_asg % SC_LANES == 0 and n_slots % SC_LANES == 0
    info = plsc.get_sparse_core_info()
    mesh = plsc.VectorSubcoreMesh(core_axis_name="c", subcore_axis_name="s")

    @functools.partial(
        pl.kernel, mesh=mesh,
        out_type=jax.ShapeDtypeStruct((n_slots,), jnp.int32),
        scratch_types=[pltpu.VMEM((n_asg,), jnp.int32), pltpu.VMEM((n_slots,), jnp.int32)],
        compiler_params=pltpu.CompilerParams(needs_layout_passes=False),
        name="sc_invert_slots",
    )
    def invert_kernel(dest_hbm, out_hbm, dest_v, out_v):
        worker = lax.axis_index("s") * info.num_cores + lax.axis_index("c")

        @pl.when(worker == 0)
        def _():
            pltpu.sync_copy(dest_hbm, dest_v)
            lane = lax.iota(jnp.int32, SC_LANES)

            @pl.loop(0, n_slots // SC_LANES)
            def _(c):
                out_v[pl.ds(c * SC_LANES, SC_LANES)] = jnp.bitwise_and(c * SC_LANES + lane, n_asg - 1)

            @pl.loop(0, n_asg // SC_LANES)
            def _(c):
                plsc.store_scatter(out_v, [dest_v[pl.ds(c * SC_LANES, SC_LANES)]], c * SC_LANES + lane)

            pltpu.sync_copy(out_v, out_hbm)

    return invert_kernel(dest)


def _moe_rows_kernel(te_ref, nv_ref, x_ref, wg_ref, wu_ref, wd_ref, o_ref, wgb_ref, wub_ref, wdb_ref):
    i = pl.program_id(0)
    e = te_ref[i]
    e_prev = te_ref[jnp.maximum(i - 1, 0)]

    @pl.when((i == 0) | (e != e_prev))
    def _():
        wgb_ref[...] = wg_ref[0].astype(BF16)
        wub_ref[...] = wu_ref[0].astype(BF16)
        wdb_ref[...] = wd_ref[0].astype(BF16)

    @pl.when(i < nv_ref[0])
    def _():
        xb = jnp.concatenate(_load_token_slabs(x_ref, MOE_TILE), axis=1).astype(BF16)
        hg = _dot(xb, wgb_ref[...])
        hu = _dot(xb, wub_ref[...])
        mid = (hg * jax.nn.sigmoid(hg) * hu).astype(BF16)
        _store_token_tiles(o_ref, _dot(mid, wdb_ref[...]))

    @pl.when(i >= nv_ref[0])
    def _():
        o_ref[...] = jnp.zeros_like(o_ref)


def _moe_rows_into_kernel(te_ref, nv_ref, x_ref, wg_ref, wu_ref, wd_ref, prev_ref, o_ref, *scratch):
    del prev_ref
    _moe_rows_kernel(te_ref, nv_ref, x_ref, wg_ref, wu_ref, wd_ref, o_ref, *scratch)


def _moe_experts_rows(xbuf, tile_expert, n_valid, w_gate, w_up, w_down, total_tiles, tile_offset, prev=None):
    D = D_MODEL
    n_tiles = xbuf.shape[0] // (MOE_TILE * SLABS)
    tiles = pl.BlockSpec((MOE_TILE * SLABS, LANE), lambda i, te, nv: (jnp.maximum(jnp.minimum(i, nv[0] - 1), 0), 0))
    in_specs = [tiles,
                pl.BlockSpec((1, D, D_EXPERT), lambda i, te, nv: (te[i], 0, 0)),
                pl.BlockSpec((1, D, D_EXPERT), lambda i, te, nv: (te[i], 0, 0)),
                pl.BlockSpec((1, D_EXPERT, D), lambda i, te, nv: (te[i], 0, 0))]
    operands = [tile_expert, n_valid, xbuf, w_gate, w_up, w_down]
    if prev is not None:
        in_specs.append(pl.BlockSpec(memory_space=pl.ANY))
        operands.append(prev)
    grid_spec = pltpu.PrefetchScalarGridSpec(
        num_scalar_prefetch=2,
        grid=(n_tiles,),
        in_specs=in_specs,
        out_specs=pl.BlockSpec((MOE_TILE * SLABS, LANE), lambda i, te, nv: (i + tile_offset, 0)),
        scratch_shapes=[pltpu.VMEM((D, D_EXPERT), BF16), pltpu.VMEM((D, D_EXPERT), BF16),
                        pltpu.VMEM((D_EXPERT, D), BF16)],
    )
    return pl.pallas_call(
        _moe_rows_kernel if prev is None else _moe_rows_into_kernel,
        out_shape=jax.ShapeDtypeStruct((total_tiles * MOE_TILE * SLABS, LANE), F32),
        grid_spec=grid_spec,
        input_output_aliases={} if prev is None else {len(operands) - 1: 0},
        compiler_params=_cparams(("arbitrary",)),
        name="moe_experts",
    )(*operands)


def _row_copy(src_hbm, src_row, dst_ref, dst_row, sem):
    return pltpu.make_async_copy(src_hbm.at[pl.ds(src_row * SLABS, SLABS), :],
                                 dst_ref.at[pl.ds(dst_row * SLABS, SLABS), :], sem)


def _wait_tiles(src_hbm, dst_ref, sem):
    pltpu.make_async_copy(src_hbm.at[pl.ds(0, dst_ref.shape[0]), :], dst_ref, sem).wait()


def _combine_kernel(d_cur_ref, d_nxt_ref, h_ref, gate_ref, g_ref, b_ref, y_hbm, o_ref, ybuf, sem):
    i = pl.program_id(0)
    n = pl.num_programs(0)
    slot = lax.rem(i, 2)

    def start_gather(d_ref, s):
        def body(tt, carry):
            for u in range(DMA_UNROLL // TOP_K):
                t = tt * (DMA_UNROLL // TOP_K) + u
                for k in range(TOP_K):
                    _row_copy(y_hbm, d_ref[TOP_K * t + k], ybuf.at[s, k], t, sem.at[s]).start(priority=k % 2)
            return carry
        lax.fori_loop(0, COMBINE_TILE * TOP_K // DMA_UNROLL, body, 0)

    @pl.when(i == 0)
    def _():
        start_gather(d_cur_ref, 0)

    @pl.when(i + 1 < n)
    def _():
        start_gather(d_nxt_ref, 1 - slot)

    for k in range(TOP_K):
        _wait_tiles(y_hbm, ybuf.at[slot, k], sem.at[slot])

    gate = gate_ref[...]
    tm = COMBINE_TILE
    ga = jnp.broadcast_to(gate[:, 0:1], (tm, LANE))
    gb = jnp.broadcast_to(gate[:, 1:2], (tm, LANE))
    hs = _load_token_slabs(h_ref, tm)
    ya = _load_token_slabs(ybuf.at[slot, 0], tm)
    yb = _load_token_slabs(ybuf.at[slot, 1], tm)
    z = [DEEPNORM_ALPHA * h + (ga * a + gb * b) for h, a, b in zip(hs, ya, yb)]
    inv_d = 1.0 / D_MODEL
    mu = sum(jnp.sum(t, axis=1, keepdims=True) for t in z) * inv_d
    zc = [t - mu for t in z]
    var = sum(jnp.sum(t * t, axis=1, keepdims=True) for t in zc) * inv_d
    rstd = lax.rsqrt(var + LN_EPS)
    for s in range(SLABS):
        cols = slice(s * LANE, (s + 1) * LANE)
        o_ref[:, cols] = zc[s] * rstd * g_ref[:, cols] + b_ref[:, cols]


def _combine(h1, ybuf, dest, gates, ln2_g, ln2_b):
    D = D_MODEL
    M = h1.shape[0] // SLABS
    tm = COMBINE_TILE
    n = M // tm
    smem_tile = lambda f: pl.BlockSpec((TOP_K * tm,), f, memory_space=pltpu.SMEM)
    return pl.pallas_call(
        _combine_kernel,
        out_shape=jax.ShapeDtypeStruct((M, D), F32),
        grid=(n,),
        in_specs=[smem_tile(lambda i: (i,)), smem_tile(lambda i: (jnp.minimum(i + 1, n - 1),)),
                  pl.BlockSpec((tm * SLABS, LANE), lambda i: (i, 0)), pl.BlockSpec((tm, TOP_K), lambda i: (i, 0)),
                  _full(ln2_g), _full(ln2_b), pl.BlockSpec(memory_space=pl.ANY)],
        out_specs=pl.BlockSpec((tm, D), lambda i: (i, 0)),
        scratch_shapes=[pltpu.VMEM((2, TOP_K, tm * SLABS, LANE), F32), pltpu.SemaphoreType.DMA((2,))],
        compiler_params=_cparams(("arbitrary",)),
        name="combine",
    )(dest, dest, h1, gates, ln2_g, ln2_b, ybuf)


def _routing_plan(route):
    M = route.shape[0]
    eid = route[:, :TOP_K].astype(jnp.int32).reshape(-1)
    gates = route[:, TOP_K:2 * TOP_K]
    A = M * TOP_K
    onehot = (eid[:, None] == jnp.arange(N_EXPERTS, dtype=eid.dtype)[None, :]).astype(jnp.int32)
    csum = jnp.cumsum(onehot, axis=0)
    rank = jnp.sum(csum * onehot, axis=1) - 1
    counts = csum[-1]
    pcounts = (counts + MOE_TILE - 1) // MOE_TILE * MOE_TILE
    pends = jnp.cumsum(pcounts)
    pstarts = pends - pcounts
    dest = (jnp.sum(onehot * pstarts[None, :], axis=1) + rank).astype(jnp.int32)
    n_tiles = (A + N_EXPERTS * (MOE_TILE - 1) + MOE_TILE - 1) // MOE_TILE
    n_valid = (pends[-1] // MOE_TILE).astype(jnp.int32)
    tile_start = jnp.minimum(jnp.arange(n_tiles, dtype=jnp.int32) * MOE_TILE, pends[-1] - 1)
    tile_expert = jnp.sum((pends[None, :] <= tile_start[:, None]).astype(jnp.int32), axis=1)
    tile_expert = jnp.minimum(tile_expert, N_EXPERTS - 1).astype(jnp.int32)
    return gates, dest, n_tiles * MOE_TILE, tile_expert, n_valid.reshape(1)


def kernel(x, meta, ln0_g, ln0_b, w_in, mu_shift, w0, w_decay_up, a0, w_a_up, w_g_up, k_k, k_a, r_k, gn_g, gn_b, conv_w, conv_b, w_rg, b_rg, w_ig, b_ig, lru_lambda, w_out, ln1_g, ln1_b, w_router_grp, b_router_grp, w_router_exp, b_router_exp, w_exp_gate, w_exp_up, w_exp_down, ln2_g, ln2_b):
    B, T, D = x.shape
    assert D == D_MODEL and T % 512 == 0 and w_in.shape[0] == 1
    row = lambda p: p.reshape(1, -1).astype(F32)
    n_rw = 3 * RWKV_W
    w_in0 = w_in[0]

    def slots(p):
        pad = lambda a, n: jnp.pad(a, [(0, 0)] * (a.ndim - 1) + [(0, n - a.shape[-1])])
        zw = p[..., n_rw:n_rw + DECAY_RANK]
        za = p[..., n_rw + DECAY_RANK:n_rw + DECAY_RANK + AAA_RANK]
        zg = p[..., n_rw + DECAY_RANK + AAA_RANK:n_rw + DECAY_RANK + AAA_RANK + GATE_RANK]
        return jnp.concatenate([p[..., :n_rw], pad(zw, LANE), pad(za, LANE), pad(zg, ZG_SLOT)], axis=-1)

    rwkv_cols = n_rw + DECAY_RANK + AAA_RANK + GATE_RANK
    w_r = slots(w_in0[:, :rwkv_cols]).astype(BF16)
    w_l = w_in0[:, rwkv_cols:].astype(BF16)
    ur, ul, ur_t, ul_t = _in_projection(x, meta, row(ln0_g), row(ln0_b), w_r, w_l)

    pad_rows = lambda a, n: jnp.pad(a, ((0, n - a.shape[0]), (0, 0)))
    rwkv_params = (slots(mu_shift[0][None, :]).astype(F32), row(w0[0]), pad_rows(w_decay_up[0], LANE).astype(BF16),
                   row(a0[0]), pad_rows(w_a_up[0], LANE).astype(BF16), pad_rows(w_g_up[0], ZG_SLOT).astype(BF16),
                   row(k_k[0]), row(k_a[0]), row(r_k[0]), row(gn_g[0]), row(gn_b[0]))
    y_rwkv = _rwkv_pipe_mixer(ur, ur_t, rwkv_params)

    blockdiag = lambda w: jax.scipy.linalg.block_diag(*[w[i] for i in range(LRU_BLOCKS)]).astype(BF16)
    lru_params = (conv_w[0], row(conv_b[0]), blockdiag(w_rg[0]), row(b_rg[0]), blockdiag(w_ig[0]), row(b_ig[0]),
                  row(lru_lambda[0]))
    y_lru = _lru_mixer(ul, ul_t, lru_params)

    w_rt = jnp.concatenate([w_router_grp[0], w_router_exp[0]], axis=1)
    w_rt = jnp.pad(w_rt, ((0, 0), (0, LANE - w_rt.shape[1])))
    wrt_hi = w_rt.astype(BF16)
    wrt_lo = (w_rt - wrt_hi.astype(F32)).astype(BF16)
    b_rt = jnp.concatenate([b_router_grp[0], b_router_exp[0]])
    b_rt = jnp.pad(b_rt, (0, LANE - b_rt.shape[0])).reshape(1, LANE)
    wo = w_out[0].astype(BF16)
    h1, route = _out_projection(x, y_rwkv, y_lru, row(ln0_g), row(ln0_b), wo[:RWKV_W], wo[RWKV_W:],
                                row(ln1_g[0]), row(ln1_b[0]), wrt_hi, wrt_lo, b_rt)

    M = B * T
    h1 = h1.reshape(M * SLABS, LANE)
    gates, dest, n_slots, tile_expert, n_valid = _routing_plan(route.reshape(M, LANE))
    row_asg = _sc_invert_slots(dest, n_slots)
    src_tok = lax.shift_right_logical(row_asg, 1)
    n_tiles = n_slots // MOE_TILE
    half = n_tiles // 2
    h1_tiles = h1.reshape(M, SLABS, LANE)
    ybuf = None
    for lo, hi in ((0, half), (half, n_tiles)):
        xbuf = _sc_gather(h1_tiles, src_tok[lo * MOE_TILE:hi * MOE_TILE]).reshape(-1, LANE)
        nv = jnp.clip(n_valid - lo, 0, hi - lo)
        ybuf = _moe_experts_rows(xbuf, tile_expert[lo:hi], nv, w_exp_gate[0], w_exp_up[0], w_exp_down[0],
                                 total_tiles=n_tiles, tile_offset=lo, prev=ybuf)
    out = _combine(h1, ybuf, dest, gates, row(ln2_g[0]), row(ln2_b[0]))
    return out.reshape(B, T, D)
```

```python
import functools
import math

import jax
import jax.numpy as jnp
from jax import lax
from jax.experimental import pallas as pl
from jax.experimental.pallas import tpu as pltpu
from jax.experimental.pallas import tpu_sc as plsc

F32 = jnp.float32
BF16 = jnp.bfloat16

D_MODEL = 1024
N_META = 16
RWKV_W = 512
RWKV_HEAD = 64
DECAY_RANK = 64
AAA_RANK = 64
GATE_RANK = 160
LRU_W = 512
LRU_BLOCKS = 8
CONV_WIDTH = 4
LRU_C = 8.0
N_GROUPS = 4
EXPERTS_PER_GROUP = 8
N_EXPERTS = N_GROUPS * EXPERTS_PER_GROUP
TOP_K = 2
D_EXPERT = 512
LN_EPS = 1e-5
GN_EPS = 64e-5
DEEPNORM_ALPHA = 2.0 ** 0.25

LANE = 128
OFF_R, OFF_K, OFF_V = 0, RWKV_W, 2 * RWKV_W
OFF_ZW = 3 * RWKV_W
OFF_ZA = OFF_ZW + LANE
OFF_ZG = OFF_ZA + LANE
ZG_SLOT = 2 * LANE
UR_W = OFF_ZG + ZG_SLOT
UL_W = 2 * LRU_W

TAIL = 256
CHUNK = 64
HEADS_PER_GROUP = 4
GW = HEADS_PER_GROUP * RWKV_HEAD
N_HGROUPS = RWKV_W // GW
LRU_TILE = TAIL
MOE_TILE = 512
COMBINE_TILE = 256
DMA_UNROLL = 8
V7X_VMEM_BYTES = 64 * 1024 * 1024
VMEM_LIMIT = V7X_VMEM_BYTES - 8 * 1024 * 1024


def _cparams(sem):
    return pltpu.CompilerParams(dimension_semantics=sem, vmem_limit_bytes=VMEM_LIMIT)


def _layer_norm(x, g, b):
    mu = jnp.mean(x, -1, keepdims=True)
    xc = x - mu
    var = jnp.mean(xc * xc, -1, keepdims=True)
    return xc * lax.rsqrt(var + LN_EPS) * g + b


def _dot(a, b):
    return jnp.dot(a, b, preferred_element_type=F32)


def _dot_nt(a, b):
    return lax.dot_general(a, b, (((1,), (1,)), ((), ())), preferred_element_type=F32)


def _dot_tn(a, b):
    return lax.dot_general(a, b, (((0,), (0,)), ((), ())), preferred_element_type=F32)


def _full(a):
    return pl.BlockSpec(a.shape, lambda *_: (0,) * a.ndim)


def _inproj_kernel(x_ref, g_ref, b_ref, wr_ref, wl_ref, ur_ref, ul_ref):
    h = _layer_norm(x_ref[0], g_ref[...], b_ref[...]).astype(BF16)
    ur_ref[0] = _dot(h, wr_ref[...])
    ul_ref[0] = _dot(h, wl_ref[...])


def _inproj_tail_kernel(x_ref, g_ref, b_ref, wr_ref, wl_ref, ur_ref, ul_ref):
    h = _layer_norm(x_ref[...], g_ref[...], b_ref[...]).astype(BF16)
    rows = lax.broadcasted_iota(jnp.int32, (TAIL, 1), 0)
    valid = (rows >= TAIL - N_META).astype(F32)
    ur_ref[...] = _dot(h, wr_ref[...]) * valid
    ul_ref[...] = _dot(h, wl_ref[...]) * valid


def _in_projection(x, meta, ln0_g, ln0_b, w_r, w_l):
    B, T, D = x.shape
    tm = 512
    ur, ul = pl.pallas_call(
        _inproj_kernel,
        out_shape=(jax.ShapeDtypeStruct((B, T, UR_W), F32), jax.ShapeDtypeStruct((B, T, UL_W), F32)),
        grid=(B, T // tm),
        in_specs=[pl.BlockSpec((1, tm, D), lambda b, i: (b, i, 0)), _full(ln0_g), _full(ln0_b), _full(w_r), _full(w_l)],
        out_specs=(pl.BlockSpec((1, tm, UR_W), lambda b, i: (b, i, 0)),
                   pl.BlockSpec((1, tm, UL_W), lambda b, i: (b, i, 0))),
        compiler_params=_cparams(("parallel", "parallel")),
        name="inproj",
    )(x, ln0_g, ln0_b, w_r, w_l)
    tail_x = jnp.concatenate([jnp.zeros((TAIL - N_META, D), F32), meta.astype(F32)], axis=0)
    ur_t, ul_t = pl.pallas_call(
        _inproj_tail_kernel,
        out_shape=(jax.ShapeDtypeStruct((TAIL, UR_W), F32), jax.ShapeDtypeStruct((TAIL, UL_W), F32)),
        grid=(1,),
        in_specs=[_full(tail_x), _full(ln0_g), _full(ln0_b), _full(w_r), _full(w_l)],
        out_specs=(pl.BlockSpec((TAIL, UR_W), lambda i: (0, 0)), pl.BlockSpec((TAIL, UL_W), lambda i: (0, 0))),
        compiler_params=_cparams(("arbitrary",)),
        name="inproj_tail",
    )(tail_x, ln0_g, ln0_b, w_r, w_l)
    return ur, ul, ur_t, ul_t


def _rwkv_pipe_kernel(u_ref, ut_ref, mu_ref, w0_ref, wdu_ref, a0_ref, wau_ref, wgu_ref, kk_ref, ka_ref, rk_ref,
                      gng_ref, gnb_ref, bones_ref, bm_ref, eye_ref, msl_ref, mil_ref,
                      m8_ref, m16_ref, m32_ref, m64_ref, y_ref,
                      s_ref, prev_ref, y0_s, q_s, mc_s, nc_s, we_s, bonus_s, g_s, yraw_s):
    s_id = pl.program_id(0)
    nb = u_ref.shape[0]
    blk = u_ref.shape[1]
    npc = blk // CHUNK
    seq_rows = lambda q: slice(q * CHUNK, (q + 1) * CHUNK)
    w_slot = lax.rem(s_id, 2)
    r_slot = 1 - w_slot

    @pl.when(s_id == 0)
    def _():
        s_ref[...] = jnp.zeros_like(s_ref)
        prev_ref[...] = jnp.zeros_like(prev_ref)
        for ref in (y0_s, q_s, mc_s, nc_s, we_s, bonus_s, g_s):
            ref[1] = jnp.zeros(ref.shape[1:], ref.dtype)

    b16 = lambda t: t.astype(BF16)
    bones = bones_ref[...]
    head_sum = lambda t: _dot(b16(t), bones)
    bm = bm_ref[...]
    bm16 = b16(bm)
    tile4 = lambda t: jnp.concatenate([t] * HEADS_PER_GROUP, axis=0)
    fold4 = lambda t: sum(t[i * CHUNK:(i + 1) * CHUNK] for i in range(HEADS_PER_GROUP))
    bd = lambda t: tile4(b16(t)) * bm16

    chains = [(b, hg) for b in range(nb) for hg in range(N_HGROUPS)]
    states = {c: s_ref[c[0], c[1]] for c in chains}

    def recurrent_chunk(j):
        for b, hg in chains:
            sl = slice(hg * GW, (hg + 1) * GW)
            q = b * npc + j
            rq = seq_rows(q)
            s = states[(b, hg)]
            yraw_s[rq, sl] = y0_s[r_slot, rq, sl] + _dot_nt(q_s[r_slot, rq, sl], bd(s))
            states[(b, hg)] = (s * we_s[r_slot, q * 8:q * 8 + 1, sl] + _dot(b16(s), bd(mc_s[r_slot, rq, sl]))
                               + nc_s[r_slot, rq, sl])

    prepared = {}

    def prepare(b):
        lo = b * blk
        u = jnp.where(s_id == 0, ut_ref[...], u_ref[b])
        row = lax.broadcasted_iota(jnp.int32, u.shape, 0)
        u_prev = jnp.where(row == 0, prev_ref[b:b + 1, :], pltpu.roll(u, 1, 0))
        prev_ref[b:b + 1, :] = u[blk - 1:blk, :]
        x = u + (u_prev - u) * mu_ref[...]
        r = x[:, OFF_R:OFF_R + RWKV_W]
        k = x[:, OFF_K:OFF_K + RWKV_W]
        v = x[:, OFF_V:OFF_V + RWKV_W]
        zw = x[:, OFF_ZW:OFF_ZW + LANE]
        za = x[:, OFF_ZA:OFF_ZA + LANE]
        zg = x[:, OFF_ZG:OFF_ZG + ZG_SLOT]
        yield
        z = w0_ref[...] + _dot(b16(jnp.tanh(zw)), wdu_ref[...])
        logw = -math.exp(-0.5) * jax.nn.sigmoid(z)
        a = jax.nn.sigmoid(a0_ref[...] + _dot(b16(za), wau_ref[...]))
        g = _dot(b16(jax.nn.sigmoid(zg)), wgu_ref[...])
        kk = k * kk_ref[...]
        kk = kk / jnp.maximum(jnp.sqrt(head_sum(kk * kk)), 1e-12)
        k = k * (1.0 + (a - 1.0) * ka_ref[...])
        kka = kk * a
        bonus_s[w_slot, lo:lo + blk, :] = head_sum(r * k * rk_ref[...]) * v
        g_s[w_slot, lo:lo + blk, :] = g
        yield
        cl = logw
        row_in_chunk = jnp.bitwise_and(lax.broadcasted_iota(jnp.int32, cl.shape, 0), CHUNK - 1)
        d = 1
        while d < CHUNK:
            cl = cl + jnp.where(row_in_chunk >= d, pltpu.roll(cl, d, 0), 0.0)
            d *= 2
        yield
        cl_last = jnp.concatenate(
            [jnp.broadcast_to(cl[(j + 1) * CHUNK - 1:(j + 1) * CHUNK, :], (CHUNK, RWKV_W)) for j in range(npc)], axis=0)
        e_neg = jnp.exp(-cl)
        e_end = jnp.exp(cl_last - cl)
        w_end = jnp.exp(cl_last)
        for j in range(npc):
            q = b * npc + j
            we_s[w_slot, q * 8:(q + 1) * 8, :] = w_end[j * CHUNK:j * CHUNK + 8, :]
        prepared[b] = dict(rt=r * jnp.exp(cl), kt=k * e_neg, at=-kk * jnp.exp(cl - logw), bt=kka * e_neg,
                           kw=k * e_end, bw=kka * e_end, v=v)
        yield

    eye = eye_ref[...]
    msl = msl_ref[...]
    mil = mil_ref[...]
    rows2 = lambda x, y: jnp.concatenate([x, y], axis=0)
    cols2 = lambda x, y: jnp.concatenate([x, y], axis=1)
    each = lambda f, *ls: [f(*xs) for xs in zip(*ls)]

    def solve(b, p):
        probs = [(slice(j * CHUNK, (j + 1) * CHUNK), slice(hg * GW, (hg + 1) * GW))
                 for j in range(npc) for hg in range(N_HGROUPS)]
        pick = lambda t: [t[rq, sl] for rq, sl in probs]
        at_w, rt_w, v_w = pick(p["at"]), pick(p["rt"]), pick(p["v"])
        lhs = each(lambda x, y: b16(rows2(x, y)), at_w, rt_w)
        ab = each(_dot_nt, lhs, each(bd, pick(p["bt"])))
        ak = each(_dot_nt, lhs, each(bd, pick(p["kt"])))
        yield
        a_ab = each(lambda t: t[:CHUNK] * msl, ab)
        a_rb = each(lambda t: b16(t[CHUNK:] * mil), ab)
        a_xk = each(lambda t: b16(rows2(t[:CHUNK] * msl, t[CHUNK:] * mil)), ak)
        a0 = each(lambda t: b16(t * m8_ref[...]), a_ab)
        a2 = each(lambda t: b16(_dot(t, bd(t))), a0)
        yield
        a4 = each(lambda t: b16(_dot(t, bd(t))), a2)
        p1 = each(lambda t: eye + t.astype(F32), a0)
        p1 = each(lambda q, t: q + _dot(b16(q), bd(t)), p1, a2)
        yield
        tt = each(lambda q, t: q + _dot(b16(q), bd(t)), p1, a4)
        yield
        for m_ref in (m16_ref, m32_ref, m64_ref):
            tb = each(b16, tt)
            off = each(lambda t: b16(t * m_ref[...]), a_ab)
            half = each(lambda x, y: b16(_dot(x, bd(y))), tb, off)
            yield
            tt = each(lambda t, x, y: t + _dot(x, bd(y)), tt, half, tb)
            yield
        tb = each(b16, tt)
        xv = each(lambda x, y: _dot(x, bd(y)), a_xk, v_w)
        yield
        u0 = each(lambda x, y: _dot(x, bd(y[:CHUNK])), tb, xv)
        ta = each(lambda x, y: _dot(x, bd(y)), tb, at_w)
        yield
        y0 = each(lambda x, y, z: _dot(x, bd(y)) + z[CHUNK:], a_rb, u0, xv)
        qq = each(lambda x, y, z: x + _dot(y, bd(z)), rt_w, a_rb, ta)
        yield
        left = each(lambda x, y, z: b16(rows2(cols2(x, y), cols2(jnp.zeros_like(z), z))), ta, u0, v_w)
        right = each(lambda x, y: b16(rows2(x, y)), pick(p["bw"]), pick(p["kw"]))
        mn = each(_dot_tn, left, right)
        for i, (rq, sl) in enumerate(probs):
            rows = slice(b * blk + rq.start, b * blk + rq.stop)
            y0_s[w_slot, rows, sl] = y0[i]
            q_s[w_slot, rows, sl] = b16(qq[i])
            mc_s[w_slot, rows, sl] = b16(fold4(mn[i][:GW] * bm))
            nc_s[w_slot, rows, sl] = fold4(mn[i][GW:] * bm)
        yield

    first = prepare(0)
    for j in range(npc):
        recurrent_chunk(j)
        if j * 4 // npc != (j + 1) * 4 // npc or j == npc - 1:
            for _ in range((j + 1) * 4 // npc - j * 4 // npc):
                next(first, None)
    for _ in first:
        pass
    for (b, hg), s in states.items():
        s_ref[b, hg] = s

    y = yraw_s[...]
    inv_n = 1.0 / RWKV_HEAD
    ym = head_sum(y) * inv_n
    yc = y - ym
    yv = head_sum(yc * yc) * inv_n
    yn = yc * lax.rsqrt(yv + GN_EPS) * gng_ref[...] + gnb_ref[...]
    y_ref[...] = ((yn + bonus_s[r_slot]) * g_s[r_slot]).astype(y_ref.dtype).reshape(y_ref.shape)

    for b in range(nb):
        solver = solve(b, prepared[b])
        nxt = prepare(b + 1) if b + 1 < nb else iter(())
        for level, _ in enumerate(solver):
            if level % 4 == 3:
                next(nxt, None)
        for _ in nxt:
            pass


def _rwkv_masks():
    f = lambda m: m.astype(F32)
    i = jnp.arange(GW)[:, None]
    j = jnp.arange(GW)[None, :]
    bm = f((i // RWKV_HEAD) == (j // RWKV_HEAD))
    t = jnp.arange(CHUNK)[:, None]
    s = (jnp.arange(GW) % CHUNK)[None, :]
    same = lambda n: (t // n) == (s // n)
    msl = f(t > s)
    mil = f(t >= s)
    m8 = f(same(8))
    m16 = f(same(16) & ~same(8))
    m32 = f(same(32) & ~same(16))
    m64 = f(~same(32))
    eye = f(t == s)
    hi = jnp.arange(RWKV_W)
    bones = ((hi[:, None] // RWKV_HEAD) == (hi[None, :] // RWKV_HEAD)).astype(BF16)
    return bones, bm, eye, msl, mil, m8, m16, m32, m64


def _rwkv_pipe_mixer(ur, ur_tail, params):
    B, T, _ = ur.shape
    blk = TAIL
    assert T % blk == 0 and blk % CHUNK == 0 and CHUNK == RWKV_HEAD
    n_blocks = T // blk
    rows = B * blk
    consts = _rwkv_masks()
    in_map = lambda s: (0, jnp.clip(s - 1, 0, n_blocks - 1), 0)
    out_map = lambda s: (0, jnp.clip(s - 2, 0, n_blocks - 1), 0)
    slot2 = lambda w, dt: pltpu.VMEM((2, rows, w), dt)
    return pl.pallas_call(
        _rwkv_pipe_kernel,
        out_shape=jax.ShapeDtypeStruct((B, T, RWKV_W), BF16),
        grid=(n_blocks + 2,),
        in_specs=[pl.BlockSpec((B, blk, UR_W), in_map), _full(ur_tail)]
                 + [_full(p) for p in params] + [_full(m) for m in consts],
        out_specs=pl.BlockSpec((B, blk, RWKV_W), out_map),
        scratch_shapes=[pltpu.VMEM((B, N_HGROUPS, CHUNK, GW), F32), pltpu.VMEM((B, UR_W), F32),
                        slot2(RWKV_W, F32), slot2(RWKV_W, BF16), slot2(RWKV_W, BF16), slot2(RWKV_W, F32),
                        pltpu.VMEM((2, 8 * rows // CHUNK, RWKV_W), F32), slot2(RWKV_W, F32), slot2(RWKV_W, F32),
                        pltpu.VMEM((rows, RWKV_W), F32)],
        compiler_params=_cparams(("arbitrary",)),
        name="rwkv7",
    )(ur, ur_tail, *params, *consts)


def _gelu_tanh(x):
    return 0.5 * x * (1.0 + jnp.tanh(math.sqrt(2.0 / math.pi) * (x + 0.044715 * (x * x * x))))


LRU_CARRY = 8


def _lru_kernel(u_ref, ut_ref, cw_ref, cb_ref, wrg_ref, brg_ref, wig_ref, big_ref, lam_ref, y_ref,
                xs_ref, hprev_ref):
    c = pl.program_id(0)
    nb = u_ref.shape[0]
    nrow = nb * LRU_TILE

    @pl.when(c == 0)
    def _():
        xs_ref[...] = jnp.zeros_like(xs_ref)
        hprev_ref[...] = jnp.zeros_like(hprev_ref)

    u_x = u_ref[...].reshape(nrow, UL_W)
    u = jnp.where(c == 0, jnp.concatenate([ut_ref[...]] * nb, axis=0), u_x)
    xl = u[:, :LRU_W]
    gl = u[:, LRU_W:]
    row = jnp.bitwise_and(lax.broadcasted_iota(jnp.int32, (nrow, LRU_W), 0), LRU_TILE - 1)
    in_group = jnp.bitwise_and(row, 7)
    roll_in_group = lambda t, d: pltpu.roll(t.reshape(t.shape[0] // 8, 8, LRU_W), d, 1).reshape(t.shape)
    xl_prev = jnp.concatenate(
        [p for b in range(nb) for p in (xs_ref[b], xl[b * LRU_TILE:(b + 1) * LRU_TILE - 8])], axis=0)
    xc = cb_ref[...] + cw_ref[CONV_WIDTH - 1:CONV_WIDTH, :] * xl
    for d in range(1, CONV_WIDTH):
        tap = jnp.where(in_group >= d, roll_in_group(xl, d), roll_in_group(xl_prev, d))
        xc = xc + cw_ref[CONV_WIDTH - 1 - d:CONV_WIDTH - d, :] * tap
    for b in range(nb):
        xs_ref[b] = xl[(b + 1) * LRU_TILE - 8:(b + 1) * LRU_TILE]

    xcb = xc.astype(BF16)
    gate_r = jax.nn.sigmoid(_dot(xcb, wrg_ref[...]) + brg_ref[...])
    gate_i = jax.nn.sigmoid(_dot(xcb, wig_ref[...]) + big_ref[...])
    lam = lam_ref[...]
    log_sig = -(jnp.maximum(-lam, 0.0) + jnp.log1p(jnp.exp(-jnp.abs(lam))))
    log_a = LRU_C * gate_r * log_sig
    a = jnp.exp(log_a)
    mult = jnp.sqrt(jnp.maximum(1.0 - jnp.exp(2.0 * log_a), 0.0))
    b = mult * gate_i * xc
    b = jnp.where((c == 0) & (row < LRU_TILE - N_META), 0.0, b)

    d = 1
    while d < 8:
        keep = in_group >= d
        a_sh = jnp.where(keep, roll_in_group(a, d), 1.0)
        b_sh = jnp.where(keep, roll_in_group(b, d), 0.0)
        b = a * b_sh + b
        a = a * a_sh
        d *= 2
    groups = []
    for bi in range(nb):
        carry = hprev_ref[bi:bi + 1, :]
        for gi in range(LRU_TILE // 8):
            lo = bi * LRU_TILE + gi * 8
            hg = b[lo:lo + 8] + a[lo:lo + 8] * carry
            carry = hg[7:8, :]
            groups.append(hg)
        hprev_ref[bi:bi + 1, :] = carry
    h = jnp.concatenate(groups, axis=0)
    y_ref[...] = (h * _gelu_tanh(gl)).astype(y_ref.dtype).reshape(y_ref.shape)


def _lru_mixer(ul, ul_tail, params):
    B, T, _ = ul.shape
    assert TAIL == LRU_TILE
    x_map = lambda c: (0, jnp.maximum(c - 1, 0), 0)
    return pl.pallas_call(
        _lru_kernel,
        out_shape=jax.ShapeDtypeStruct((B, T, LRU_W), BF16),
        grid=(T // LRU_TILE + 1,),
        in_specs=[pl.BlockSpec((B, LRU_TILE, UL_W), x_map), _full(ul_tail)] + [_full(p) for p in params],
        out_specs=pl.BlockSpec((B, LRU_TILE, LRU_W), x_map),
        scratch_shapes=[pltpu.VMEM((B, LRU_CARRY, LRU_W), F32), pltpu.VMEM((B, LRU_W), F32)],
        compiler_params=_cparams(("arbitrary",)),
        name="rglru",
    )(ul, ul_tail, *params)


def _route(lg):
    lane = lax.broadcasted_iota(jnp.int32, lg.shape, 1)
    neg = jnp.float32(-jnp.inf)
    rmax = lambda t: jnp.max(t, axis=1, keepdims=True)
    first = lambda hit: jnp.min(jnp.where(hit, lane, LANE), axis=1, keepdims=True)
    is_grp = lane < N_GROUPS
    gl = jnp.where(is_grp, lg, neg)
    gmax = rmax(gl)
    g_sel = first(gl == gmax)
    p_g = 1.0 / jnp.sum(jnp.where(is_grp, jnp.exp(lg - gmax), 0.0), axis=1, keepdims=True)
    ex = lane - N_GROUPS
    in_grp = (ex >= 0) & (ex < N_EXPERTS) & (jnp.right_shift(ex, 3) == g_sel)
    el = jnp.where(in_grp, lg, neg)
    v1 = rmax(el)
    i1 = first(el == v1)
    el2 = jnp.where(lane == i1, neg, el)
    v2 = rmax(el2)
    i2 = first(el2 == v2)
    t = jnp.exp(v2 - v1)
    gate1 = p_g / (1.0 + t)
    gate2 = p_g * t / (1.0 + t)
    e1 = (i1 - N_GROUPS).astype(F32)
    e2 = (i2 - N_GROUPS).astype(F32)
    return jnp.where(lane == 0, e1, jnp.where(lane == 1, e2, jnp.where(lane == 2, gate1, jnp.where(lane == 3, gate2, 0.0))))


SLABS = D_MODEL // LANE


def _store_token_tiles(ref, val):
    n = val.shape[0]
    for s in range(SLABS):
        ref[pl.ds(s, n, stride=SLABS), :] = val[:, s * LANE:(s + 1) * LANE]


def _load_token_slabs(ref, n):
    return [ref[pl.ds(s, n, stride=SLABS), :] for s in range(SLABS)]


def _outproj_kernel(x_ref, yr_ref, yl_ref, g0_ref, b0_ref, wor_ref, wol_ref, g1_ref, b1_ref,
                    wrt_hi_ref, wrt_lo_ref, brt_ref, h1_ref, rt_ref):
    h0 = _layer_norm(x_ref[0], g0_ref[...], b0_ref[...])
    mix = _dot(yr_ref[0], wor_ref[...]) + _dot(yl_ref[0], wol_ref[...])
    h1 = _layer_norm(DEEPNORM_ALPHA * h0 + mix, g1_ref[...], b1_ref[...])
    _store_token_tiles(h1_ref.at[0], h1)
    hi = h1.astype(BF16)
    lo = (h1 - hi.astype(F32)).astype(BF16)
    w_hi = wrt_hi_ref[...]
    lg = _dot(hi, w_hi) + (_dot(hi, wrt_lo_ref[...]) + _dot(lo, w_hi)) + brt_ref[...]
    rt_ref[0] = _route(lg)


def _out_projection(x, y_rwkv, y_lru, ln0_g, ln0_b, wo_r, wo_l, ln1_g, ln1_b, wrt_hi, wrt_lo, brt):
    B, T, D = x.shape
    tm = 512
    rows = lambda w: pl.BlockSpec((1, tm, w), lambda b, i: (b, i, 0))
    return pl.pallas_call(
        _outproj_kernel,
        out_shape=(jax.ShapeDtypeStruct((B, T * SLABS, LANE), F32), jax.ShapeDtypeStruct((B, T, LANE), F32)),
        grid=(B, T // tm),
        in_specs=[rows(D), rows(RWKV_W), rows(LRU_W), _full(ln0_g), _full(ln0_b), _full(wo_r), _full(wo_l),
                  _full(ln1_g), _full(ln1_b), _full(wrt_hi), _full(wrt_lo), _full(brt)],
        out_specs=(pl.BlockSpec((1, tm * SLABS, LANE), lambda b, i: (b, i, 0)), rows(LANE)),
        compiler_params=_cparams(("parallel", "parallel")),
        name="outproj",
    )(x, y_rwkv, y_lru, ln0_g, ln0_b, wo_r, wo_l, ln1_g, ln1_b, wrt_hi, wrt_lo, brt)


SC_WINDOW = 32


def _sc_gather(table, idx):
    info = plsc.get_sparse_core_info()
    n_workers = info.num_cores * info.num_subcores
    n = idx.shape[0]
    per_worker = n // n_workers
    n_win = per_worker // SC_WINDOW
    assert n % (n_workers * SC_WINDOW * 2) == 0
    mesh = plsc.VectorSubcoreMesh(core_axis_name="c", subcore_axis_name="s")

    @functools.partial(
        pl.kernel, mesh=mesh,
        out_type=jax.ShapeDtypeStruct((n, SLABS, LANE), F32),
        scratch_types=[pltpu.VMEM((SC_WINDOW,), jnp.int32), pltpu.VMEM((SC_WINDOW,), jnp.int32),
                       pltpu.VMEM((SC_WINDOW, SLABS, LANE), F32), pltpu.VMEM((SC_WINDOW, SLABS, LANE), F32),
                       pltpu.SemaphoreType.DMA, pltpu.SemaphoreType.DMA],
        name="sc_row_gather",
    )
    def gather_kernel(table_hbm, idx_hbm, out_hbm, idx_a, idx_b, rows_a, rows_b, sem_a, sem_b):
        worker = lax.axis_index("s") * info.num_cores + lax.axis_index("c")
        base = worker * per_worker
        bufs = ((idx_a, rows_a, sem_a), (idx_b, rows_b, sem_b))

        def start(w, buf):
            idx_v, rows_v, sem = buf
            pltpu.sync_copy(idx_hbm.at[pl.ds(base + w * SC_WINDOW, SC_WINDOW)], idx_v)
            pltpu.async_copy(table_hbm.at[idx_v], rows_v, sem)

        def finish(w, buf):
            idx_v, rows_v, sem = buf
            pltpu.make_async_copy(table_hbm.at[idx_v], rows_v, sem).wait()
            pltpu.sync_copy(rows_v, out_hbm.at[pl.ds(base + w * SC_WINDOW, SC_WINDOW)])

        start(0, bufs[0])

        @pl.loop(0, n_win, step=2)
        def _(w):
            start(w + 1, bufs[1])
            finish(w, bufs[0])

            @pl.when(w + 2 < n_win)
            def _():
                start(w + 2, bufs[0])
            finish(w + 1, bufs[1])

    return gather_kernel(table, idx)


SC_LANES = 16


def _sc_invert_slots(dest, n_slots):
    n_asg = dest.shape[0]
    assert n_asg & (n_asg - 1) == 0 and n_asg % SC_LANES == 0 and n_slots % SC_LANES == 0
    info = plsc.get_sparse_core_info()
    mesh = plsc.VectorSubcoreMesh(core_axis_name="c", subcore_axis_name="s")

    @functools.partial(
        pl.kernel, mesh=mesh,
        out_type=jax.ShapeDtypeStruct((n_slots,), jnp.int32),
        scratch_types=[pltpu.VMEM((n_asg,), jnp.int32), pltpu.VMEM((n_slots,), jnp.int32)],
        compiler_params=pltpu.CompilerParams(needs_layout_passes=False),
        name="sc_invert_slots",
    )
    def invert_kernel(dest_hbm, out_hbm, dest_v, out_v):
        worker = lax.axis_index("s") * info.num_cores + lax.axis_index("c")

        @pl.when(worker == 0)
        def _():
            pltpu.sync_copy(dest_hbm, dest_v)
            lane = lax.iota(jnp.int32, SC_LANES)

            @pl.loop(0, n_slots // SC_LANES)
            def _(c):
                out_v[pl.ds(c * SC_LANES, SC_LANES)] = jnp.bitwise_and(c * SC_LANES + lane, n_asg - 1)

            @pl.loop(0, n_asg // SC_LANES)
            def _(c):
                plsc.store_scatter(out_v, [dest_v[pl.ds(c * SC_LANES, SC_LANES)]], c * SC_LANES + lane)

            pltpu.sync_copy(out_v, out_hbm)

    return invert_kernel(dest)


def _moe_rows_kernel(te_ref, nv_ref, x_ref, wg_ref, wu_ref, wd_ref, o_ref, wgb_ref, wub_ref, wdb_ref):
    i = pl.program_id(0)
    e = te_ref[i]
    e_prev = te_ref[jnp.maximum(i - 1, 0)]

    @pl.when((i == 0) | (e != e_prev))
    def _():
        wgb_ref[...] = wg_ref[0].astype(BF16)
        wub_ref[...] = wu_ref[0].astype(BF16)
        wdb_ref[...] = wd_ref[0].astype(BF16)

    @pl.when(i < nv_ref[0])
    def _():
        xb = jnp.concatenate(_load_token_slabs(x_ref, MOE_TILE), axis=1).astype(BF16)
        hg = _dot(xb, wgb_ref[...])
        hu = _dot(xb, wub_ref[...])
        mid = (hg * jax.nn.sigmoid(hg) * hu).astype(BF16)
        _store_token_tiles(o_ref, _dot(mid, wdb_ref[...]))

    @pl.when(i >= nv_ref[0])
    def _():
        o_ref[...] = jnp.zeros_like(o_ref)


def _moe_rows_into_kernel(te_ref, nv_ref, x_ref, wg_ref, wu_ref, wd_ref, prev_ref, o_ref, *scratch):
    del prev_ref
    _moe_rows_kernel(te_ref, nv_ref, x_ref, wg_ref, wu_ref, wd_ref, o_ref, *scratch)


def _moe_experts_rows(xbuf, tile_expert, n_valid, w_gate, w_up, w_down, total_tiles, tile_offset, prev=None):
    D = D_MODEL
    n_tiles = xbuf.shape[0] // (MOE_TILE * SLABS)
    tiles = pl.BlockSpec((MOE_TILE * SLABS, LANE), lambda i, te, nv: (jnp.maximum(jnp.minimum(i, nv[0] - 1), 0), 0))
    in_specs = [tiles,
                pl.BlockSpec((1, D, D_EXPERT), lambda i, te, nv: (te[i], 0, 0)),
                pl.BlockSpec((1, D, D_EXPERT), lambda i, te, nv: (te[i], 0, 0)),
                pl.BlockSpec((1, D_EXPERT, D), lambda i, te, nv: (te[i], 0, 0))]
    operands = [tile_expert, n_valid, xbuf, w_gate, w_up, w_down]
    if prev is not None:
        in_specs.append(pl.BlockSpec(memory_space=pl.ANY))
        operands.append(prev)
    grid_spec = pltpu.PrefetchScalarGridSpec(
        num_scalar_prefetch=2,
        grid=(n_tiles,),
        in_specs=in_specs,
        out_specs=pl.BlockSpec((MOE_TILE * SLABS, LANE), lambda i, te, nv: (i + tile_offset, 0)),
        scratch_shapes=[pltpu.VMEM((D, D_EXPERT), BF16), pltpu.VMEM((D, D_EXPERT), BF16),
                        pltpu.VMEM((D_EXPERT, D), BF16)],
    )
    return pl.pallas_call(
        _moe_rows_kernel if prev is None else _moe_rows_into_kernel,
        out_shape=jax.ShapeDtypeStruct((total_tiles * MOE_TILE * SLABS, LANE), F32),
        grid_spec=grid_spec,
        input_output_aliases={} if prev is None else {len(operands) - 1: 0},
        compiler_params=_cparams(("arbitrary",)),
        name="moe_experts",
    )(*operands)


def _row_copy(src_hbm, src_row, dst_ref, dst_row, sem):
    return pltpu.make_async_copy(src_hbm.at[pl.ds(src_row * SLABS, SLABS), :],
                                 dst_ref.at[pl.ds(dst_row * SLABS, SLABS), :], sem)


def _wait_tiles(src_hbm, dst_ref, sem):
    pltpu.make_async_copy(src_hbm.at[pl.ds(0, dst_ref.shape[0]), :], dst_ref, sem).wait()


def _combine_kernel(d_cur_ref, d_nxt_ref, h_ref, gate_ref, g_ref, b_ref, y_hbm, o_ref, ybuf, sem):
    i = pl.program_id(0)
    n = pl.num_programs(0)
    slot = lax.rem(i, 2)

    def start_gather(d_ref, s):
        def body(tt, carry):
            for u in range(DMA_UNROLL // TOP_K):
                t = tt * (DMA_UNROLL // TOP_K) + u
                for k in range(TOP_K):
                    _row_copy(y_hbm, d_ref[TOP_K * t + k], ybuf.at[s, k], t, sem.at[s]).start(priority=k % 2)
            return carry
        lax.fori_loop(0, COMBINE_TILE * TOP_K // DMA_UNROLL, body, 0)

    @pl.when(i == 0)
    def _():
        start_gather(d_cur_ref, 0)

    @pl.when(i + 1 < n)
    def _():
        start_gather(d_nxt_ref, 1 - slot)

    for k in range(TOP_K):
        _wait_tiles(y_hbm, ybuf.at[slot, k], sem.at[slot])

    gate = gate_ref[...]
    tm = COMBINE_TILE
    ga = jnp.broadcast_to(gate[:, 0:1], (tm, LANE))
    gb = jnp.broadcast_to(gate[:, 1:2], (tm, LANE))
    hs = _load_token_slabs(h_ref, tm)
    ya = _load_token_slabs(ybuf.at[slot, 0], tm)
    yb = _load_token_slabs(ybuf.at[slot, 1], tm)
    z = [DEEPNORM_ALPHA * h + (ga * a + gb * b) for h, a, b in zip(hs, ya, yb)]
    inv_d = 1.0 / D_MODEL
    mu = sum(jnp.sum(t, axis=1, keepdims=True) for t in z) * inv_d
    zc = [t - mu for t in z]
    var = sum(jnp.sum(t * t, axis=1, keepdims=True) for t in zc) * inv_d
    rstd = lax.rsqrt(var + LN_EPS)
    for s in range(SLABS):
        cols = slice(s * LANE, (s + 1) * LANE)
        o_ref[:, cols] = zc[s] * rstd * g_ref[:, cols] + b_ref[:, cols]


def _combine(h1, ybuf, dest, gates, ln2_g, ln2_b):
    D = D_MODEL
    M = h1.shape[0] // SLABS
    tm = COMBINE_TILE
    n = M // tm
    smem_tile = lambda f: pl.BlockSpec((TOP_K * tm,), f, memory_space=pltpu.SMEM)
    return pl.pallas_call(
        _combine_kernel,
        out_shape=jax.ShapeDtypeStruct((M, D), F32),
        grid=(n,),
        in_specs=[smem_tile(lambda i: (i,)), smem_tile(lambda i: (jnp.minimum(i + 1, n - 1),)),
                  pl.BlockSpec((tm * SLABS, LANE), lambda i: (i, 0)), pl.BlockSpec((tm, TOP_K), lambda i: (i, 0)),
                  _full(ln2_g), _full(ln2_b), pl.BlockSpec(memory_space=pl.ANY)],
        out_specs=pl.BlockSpec((tm, D), lambda i: (i, 0)),
        scratch_shapes=[pltpu.VMEM((2, TOP_K, tm * SLABS, LANE), F32), pltpu.SemaphoreType.DMA((2,))],
        compiler_params=_cparams(("arbitrary",)),
        name="combine",
    )(dest, dest, h1, gates, ln2_g, ln2_b, ybuf)


def _routing_plan(route):
    M = route.shape[0]
    eid = route[:, :TOP_K].astype(jnp.int32).reshape(-1)
    gates = route[:, TOP_K:2 * TOP_K]
    A = M * TOP_K
    onehot = (eid[:, None] == jnp.arange(N_EXPERTS, dtype=eid.dtype)[None, :]).astype(jnp.int32)
    csum = jnp.cumsum(onehot, axis=0)
    rank = jnp.sum(csum * onehot, axis=1) - 1
    counts = csum[-1]
    pcounts = (counts + MOE_TILE - 1) // MOE_TILE * MOE_TILE
    pends = jnp.cumsum(pcounts)
    pstarts = pends - pcounts
    dest = (jnp.sum(onehot * pstarts[None, :], axis=1) + rank).astype(jnp.int32)
    n_tiles = (A + N_EXPERTS * (MOE_TILE - 1) + MOE_TILE - 1) // MOE_TILE
    n_valid = (pends[-1] // MOE_TILE).astype(jnp.int32)
    tile_start = jnp.minimum(jnp.arange(n_tiles, dtype=jnp.int32) * MOE_TILE, pends[-1] - 1)
    tile_expert = jnp.sum((pends[None, :] <= tile_start[:, None]).astype(jnp.int32), axis=1)
    tile_expert = jnp.minimum(tile_expert, N_EXPERTS - 1).astype(jnp.int32)
    return gates, dest, n_tiles * MOE_TILE, tile_expert, n_valid.reshape(1)


def kernel(x, meta, ln0_g, ln0_b, w_in, mu_shift, w0, w_decay_up, a0, w_a_up, w_g_up, k_k, k_a, r_k, gn_g, gn_b, conv_w, conv_b, w_rg, b_rg, w_ig, b_ig, lru_lambda, w_out, ln1_g, ln1_b, w_router_grp, b_router_grp, w_router_exp, b_router_exp, w_exp_gate, w_exp_up, w_exp_down, ln2_g, ln2_b):
    B, T, D = x.shape
    assert D == D_MODEL and T % 512 == 0 and w_in.shape[0] == 1
    row = lambda p: p.reshape(1, -1).astype(F32)
    n_rw = 3 * RWKV_W
    w_in0 = w_in[0]

    def slots(p):
        pad = lambda a, n: jnp.pad(a, [(0, 0)] * (a.ndim - 1) + [(0, n - a.shape[-1])])
        zw = p[..., n_rw:n_rw + DECAY_RANK]
        za = p[..., n_rw + DECAY_RANK:n_rw + DECAY_RANK + AAA_RANK]
        zg = p[..., n_rw + DECAY_RANK + AAA_RANK:n_rw + DECAY_RANK + AAA_RANK + GATE_RANK]
        return jnp.concatenate([p[..., :n_rw], pad(zw, LANE), pad(za, LANE), pad(zg, ZG_SLOT)], axis=-1)

    rwkv_cols = n_rw + DECAY_RANK + AAA_RANK + GATE_RANK
    w_r = slots(w_in0[:, :rwkv_cols]).astype(BF16)
    w_l = w_in0[:, rwkv_cols:].astype(BF16)
    ur, ul, ur_t, ul_t = _in_projection(x, meta, row(ln0_g), row(ln0_b), w_r, w_l)

    pad_rows = lambda a, n: jnp.pad(a, ((0, n - a.shape[0]), (0, 0)))
    rwkv_params = (slots(mu_shift[0][None, :]).astype(F32), row(w0[0]), pad_rows(w_decay_up[0], LANE).astype(BF16),
                   row(a0[0]), pad_rows(w_a_up[0], LANE).astype(BF16), pad_rows(w_g_up[0], ZG_SLOT).astype(BF16),
                   row(k_k[0]), row(k_a[0]), row(r_k[0]), row(gn_g[0]), row(gn_b[0]))
    y_rwkv = _rwkv_pipe_mixer(ur, ur_t, rwkv_params)

    blockdiag = lambda w: jax.scipy.linalg.block_diag(*[w[i] for i in range(LRU_BLOCKS)]).astype(BF16)
    lru_params = (conv_w[0], row(conv_b[0]), blockdiag(w_rg[0]), row(b_rg[0]), blockdiag(w_ig[0]), row(b_ig[0]),
                  row(lru_lambda[0]))
    y_lru = _lru_mixer(ul, ul_t, lru_params)

    w_rt = jnp.concatenate([w_router_grp[0], w_router_exp[0]], axis=1)
    w_rt = jnp.pad(w_rt, ((0, 0), (0, LANE - w_rt.shape[1])))
    wrt_hi = w_rt.astype(BF16)
    wrt_lo = (w_rt - wrt_hi.astype(F32)).astype(BF16)
    b_rt = jnp.concatenate([b_router_grp[0], b_router_exp[0]])
    b_rt = jnp.pad(b_rt, (0, LANE - b_rt.shape[0])).reshape(1, LANE)
    wo = w_out[0].astype(BF16)
    h1, route = _out_projection(x, y_rwkv, y_lru, row(ln0_g), row(ln0_b), wo[:RWKV_W], wo[RWKV_W:],
                                row(ln1_g[0]), row(ln1_b[0]), wrt_hi, wrt_lo, b_rt)

    M = B * T
    h1 = h1.reshape(M * SLABS, LANE)
    gates, dest, n_slots, tile_expert, n_valid = _routing_plan(route.reshape(M, LANE))
    row_asg = _sc_invert_slots(dest, n_slots)
    src_tok = lax.shift_right_logical(row_asg, 1)
    n_tiles = n_slots // MOE_TILE
    half = n_tiles // 2
    h1_tiles = h1.reshape(M, SLABS, LANE)
    ybuf = None
    for lo, hi in ((0, half), (half, n_tiles)):
        xbuf = _sc_gather(h1_tiles, src_tok[lo * MOE_TILE:hi * MOE_TILE]).reshape(-1, LANE)
        nv = jnp.clip(n_valid - lo, 0, hi - lo)
        ybuf = _moe_experts_rows(xbuf, tile_expert[lo:hi], nv, w_exp_gate[0], w_exp_up[0], w_exp_down[0],
                                 total_tiles=n_tiles, tile_offset=lo, prev=ybuf)
    out = _combine(h1, ybuf, dest, gates, row(ln2_g[0]), row(ln2_b[0]))
    return out.reshape(B, T, D)
```

```python
import functools
import math

import jax
import jax.numpy as jnp
from jax import lax
from jax.experimental import pallas as pl
from jax.experimental.pallas import tpu as pltpu
from jax.experimental.pallas import tpu_sc as plsc

F32 = jnp.float32
BF16 = jnp.bfloat16

D_MODEL = 1024
N_META = 16
RWKV_W = 512
RWKV_HEAD = 64
DECAY_RANK = 64
AAA_RANK = 64
GATE_RANK = 160
LRU_W = 512
LRU_BLOCKS = 8
CONV_WIDTH = 4
LRU_C = 8.0
N_GROUPS = 4
EXPERTS_PER_GROUP = 8
N_EXPERTS = N_GROUPS * EXPERTS_PER_GROUP
TOP_K = 2
D_EXPERT = 512
LN_EPS = 1e-5
GN_EPS = 64e-5
DEEPNORM_ALPHA = 2.0 ** 0.25

LANE = 128
OFF_R, OFF_K, OFF_V = 0, RWKV_W, 2 * RWKV_W
OFF_ZW = 3 * RWKV_W
OFF_ZA = OFF_ZW + LANE
OFF_ZG = OFF_ZA + LANE
ZG_SLOT = 2 * LANE
UR_W = OFF_ZG + ZG_SLOT
UL_W = 2 * LRU_W

TAIL = 256
CHUNK = 64
HEADS_PER_GROUP = 4
GW = HEADS_PER_GROUP * RWKV_HEAD
N_HGROUPS = RWKV_W // GW
LRU_TILE = TAIL
MOE_TILE = 256
COMBINE_TILE = 256
DMA_UNROLL = 8
V7X_VMEM_BYTES = 64 * 1024 * 1024
VMEM_LIMIT = V7X_VMEM_BYTES - 8 * 1024 * 1024


def _cparams(sem):
    return pltpu.CompilerParams(dimension_semantics=sem, vmem_limit_bytes=VMEM_LIMIT)


def _layer_norm(x, g, b):
    mu = jnp.mean(x, -1, keepdims=True)
    xc = x - mu
    var = jnp.mean(xc * xc, -1, keepdims=True)
    return xc * lax.rsqrt(var + LN_EPS) * g + b


def _dot(a, b):
    return jnp.dot(a, b, preferred_element_type=F32)


def _dot_nt(a, b):
    return lax.dot_general(a, b, (((1,), (1,)), ((), ())), preferred_element_type=F32)


def _dot_tn(a, b):
    return lax.dot_general(a, b, (((0,), (0,)), ((), ())), preferred_element_type=F32)


def _full(a):
    return pl.BlockSpec(a.shape, lambda *_: (0,) * a.ndim)


def _inproj_kernel(x_ref, g_ref, b_ref, wr_ref, wl_ref, ur_ref, ul_ref, h0_ref):
    h0 = _layer_norm(x_ref[0], g_ref[...], b_ref[...])
    h0_ref[0] = h0
    h = h0.astype(BF16)
    ur_ref[0] = _dot(h, wr_ref[...])
    ul_ref[0] = _dot(h, wl_ref[...])


def _inproj_tail_kernel(x_ref, g_ref, b_ref, wr_ref, wl_ref, ur_ref, ul_ref):
    h = _layer_norm(x_ref[...], g_ref[...], b_ref[...]).astype(BF16)
    rows = lax.broadcasted_iota(jnp.int32, (TAIL, 1), 0)
    valid = (rows >= TAIL - N_META).astype(F32)
    ur_ref[...] = _dot(h, wr_ref[...]) * valid
    ul_ref[...] = _dot(h, wl_ref[...]) * valid


def _in_projection(x, meta, ln0_g, ln0_b, w_r, w_l):
    B, T, D = x.shape
    tm = 512
    ur, ul, h0 = pl.pallas_call(
        _inproj_kernel,
        out_shape=(jax.ShapeDtypeStruct((B, T, UR_W), F32), jax.ShapeDtypeStruct((B, T, UL_W), F32),
                   jax.ShapeDtypeStruct((B, T, D), F32)),
        grid=(B, T // tm),
        in_specs=[pl.BlockSpec((1, tm, D), lambda b, i: (b, i, 0)), _full(ln0_g), _full(ln0_b), _full(w_r), _full(w_l)],
        out_specs=(pl.BlockSpec((1, tm, UR_W), lambda b, i: (b, i, 0)),
                   pl.BlockSpec((1, tm, UL_W), lambda b, i: (b, i, 0)),
                   pl.BlockSpec((1, tm, D), lambda b, i: (b, i, 0))),
        compiler_params=_cparams(("parallel", "parallel")),
        name="inproj",
    )(x, ln0_g, ln0_b, w_r, w_l)
    tail_x = jnp.concatenate([jnp.zeros((TAIL - N_META, D), F32), meta.astype(F32)], axis=0)
    ur_t, ul_t = pl.pallas_call(
        _inproj_tail_kernel,
        out_shape=(jax.ShapeDtypeStruct((TAIL, UR_W), F32), jax.ShapeDtypeStruct((TAIL, UL_W), F32)),
        grid=(1,),
        in_specs=[_full(tail_x), _full(ln0_g), _full(ln0_b), _full(w_r), _full(w_l)],
        out_specs=(pl.BlockSpec((TAIL, UR_W), lambda i: (0, 0)), pl.BlockSpec((TAIL, UL_W), lambda i: (0, 0))),
        compiler_params=_cparams(("arbitrary",)),
        name="inproj_tail",
    )(tail_x, ln0_g, ln0_b, w_r, w_l)
    return ur, ul, ur_t, ul_t, h0


def _rwkv_pipe_kernel(u_ref, ut_ref, mu_ref, w0_ref, wdu_ref, a0_ref, wau_ref, wgu_ref, kk_ref, ka_ref, rk_ref,
                      gng_ref, gnb_ref, bones_ref, bm_ref, eye_ref, msl_ref, mil_ref,
                      m8_ref, m16_ref, m32_ref, m64_ref, y_ref,
                      s_ref, prev_ref, y0_s, q_s, mc_s, nc_s, we_s, bonus_s, g_s, yraw_s):
    s_id = pl.program_id(0)
    nb = u_ref.shape[0]
    blk = u_ref.shape[1]
    npc = blk // CHUNK
    seq_rows = lambda q: slice(q * CHUNK, (q + 1) * CHUNK)
    w_slot = lax.rem(s_id, 2)
    r_slot = 1 - w_slot

    @pl.when(s_id == 0)
    def _():
        s_ref[...] = jnp.zeros_like(s_ref)
        prev_ref[...] = jnp.zeros_like(prev_ref)
        for ref in (y0_s, q_s, mc_s, nc_s, we_s, bonus_s, g_s):
            ref[1] = jnp.zeros(ref.shape[1:], ref.dtype)

    b16 = lambda t: t.astype(BF16)
    bones = bones_ref[...]
    head_sum = lambda t: _dot(b16(t), bones)
    bm = bm_ref[...]
    bm16 = b16(bm)
    tile4 = lambda t: jnp.concatenate([t] * HEADS_PER_GROUP, axis=0)
    fold4 = lambda t: sum(t[i * CHUNK:(i + 1) * CHUNK] for i in range(HEADS_PER_GROUP))
    bd = lambda t: tile4(b16(t)) * bm16

    chains = [(b, hg) for b in range(nb) for hg in range(N_HGROUPS)]
    states = {c: s_ref[c[0], c[1]] for c in chains}

    def recurrent_chunk(j):
        for b, hg in chains:
            sl = slice(hg * GW, (hg + 1) * GW)
            q = b * npc + j
            rq = seq_rows(q)
            s = states[(b, hg)]
            yraw_s[rq, sl] = y0_s[r_slot, rq, sl] + _dot_nt(q_s[r_slot, rq, sl], bd(s))
            states[(b, hg)] = (s * we_s[r_slot, q * 8:q * 8 + 1, sl] + _dot(b16(s), bd(mc_s[r_slot, rq, sl]))
                               + nc_s[r_slot, rq, sl])

    prepared = {}

    def prepare(b):
        lo = b * blk
        u = jnp.where(s_id == 0, ut_ref[...], u_ref[b])
        row = lax.broadcasted_iota(jnp.int32, u.shape, 0)
        u_prev = jnp.where(row == 0, prev_ref[b:b + 1, :], pltpu.roll(u, 1, 0))
        prev_ref[b:b + 1, :] = u[blk - 1:blk, :]
        x = u + (u_prev - u) * mu_ref[...]
        r = x[:, OFF_R:OFF_R + RWKV_W]
        k = x[:, OFF_K:OFF_K + RWKV_W]
        v = x[:, OFF_V:OFF_V + RWKV_W]
        zw = x[:, OFF_ZW:OFF_ZW + LANE]
        za = x[:, OFF_ZA:OFF_ZA + LANE]
        zg = x[:, OFF_ZG:OFF_ZG + ZG_SLOT]
        yield
        z = w0_ref[...] + _dot(b16(jnp.tanh(zw)), wdu_ref[...])
        logw = -math.exp(-0.5) * jax.nn.sigmoid(z)
        a = jax.nn.sigmoid(a0_ref[...] + _dot(b16(za), wau_ref[...]))
        g = _dot(b16(jax.nn.sigmoid(zg)), wgu_ref[...])
        kk = k * kk_ref[...]
        kk = kk / jnp.maximum(jnp.sqrt(head_sum(kk * kk)), 1e-12)
        k = k * (1.0 + (a - 1.0) * ka_ref[...])
        kka = kk * a
        bonus_s[w_slot, lo:lo + blk, :] = head_sum(r * k * rk_ref[...]) * v
        g_s[w_slot, lo:lo + blk, :] = g
        yield
        cl = logw
        row_in_chunk = jnp.bitwise_and(lax.broadcasted_iota(jnp.int32, cl.shape, 0), CHUNK - 1)
        d = 1
        while d < CHUNK:
            cl = cl + jnp.where(row_in_chunk >= d, pltpu.roll(cl, d, 0), 0.0)
            d *= 2
        yield
        cl_last = jnp.concatenate(
            [jnp.broadcast_to(cl[(j + 1) * CHUNK - 1:(j + 1) * CHUNK, :], (CHUNK, RWKV_W)) for j in range(npc)], axis=0)
        e_neg = jnp.exp(-cl)
        e_end = jnp.exp(cl_last - cl)
        w_end = jnp.exp(cl_last)
        for j in range(npc):
            q = b * npc + j
            we_s[w_slot, q * 8:(q + 1) * 8, :] = w_end[j * CHUNK:j * CHUNK + 8, :]
        prepared[b] = dict(rt=r * jnp.exp(cl), kt=k * e_neg, at=-kk * jnp.exp(cl - logw), bt=kka * e_neg,
                           kw=k * e_end, bw=kka * e_end, v=v)
        yield

    eye = eye_ref[...]
    msl = msl_ref[...]
    mil = mil_ref[...]
    rows2 = lambda x, y: jnp.concatenate([x, y], axis=0)
    cols2 = lambda x, y: jnp.concatenate([x, y], axis=1)
    each = lambda f, *ls: [f(*xs) for xs in zip(*ls)]

    def solve(b, p):
        probs = [(slice(j * CHUNK, (j + 1) * CHUNK), slice(hg * GW, (hg + 1) * GW))
                 for j in range(npc) for hg in range(N_HGROUPS)]
        pick = lambda t: [t[rq, sl] for rq, sl in probs]
        at_w, rt_w, v_w = pick(p["at"]), pick(p["rt"]), pick(p["v"])
        lhs = each(lambda x, y: b16(rows2(x, y)), at_w, rt_w)
        ab = each(_dot_nt, lhs, each(bd, pick(p["bt"])))
        ak = each(_dot_nt, lhs, each(bd, pick(p["kt"])))
        yield
        a_ab = each(lambda t: t[:CHUNK] * msl, ab)
        a_rb = each(lambda t: b16(t[CHUNK:] * mil), ab)
        a_xk = each(lambda t: b16(rows2(t[:CHUNK] * msl, t[CHUNK:] * mil)), ak)
        a0 = each(lambda t: b16(t * m8_ref[...]), a_ab)
        a2 = each(lambda t: b16(_dot(t, bd(t))), a0)
        yield
        a4 = each(lambda t: b16(_dot(t, bd(t))), a2)
        p1 = each(lambda t: eye + t.astype(F32), a0)
        p1 = each(lambda q, t: q + _dot(b16(q), bd(t)), p1, a2)
        yield
        tt = each(lambda q, t: q + _dot(b16(q), bd(t)), p1, a4)
        yield
        for m_ref in (m16_ref, m32_ref, m64_ref):
            tb = each(b16, tt)
            off = each(lambda t: b16(t * m_ref[...]), a_ab)
            half = each(lambda x, y: b16(_dot(x, bd(y))), tb, off)
            yield
            tt = each(lambda t, x, y: t + _dot(x, bd(y)), tt, half, tb)
            yield
        tb = each(b16, tt)
        xv = each(lambda x, y: _dot(x, bd(y)), a_xk, v_w)
        yield
        u0 = each(lambda x, y: _dot(x, bd(y[:CHUNK])), tb, xv)
        ta = each(lambda x, y: _dot(x, bd(y)), tb, at_w)
        yield
        y0 = each(lambda x, y, z: _dot(x, bd(y)) + z[CHUNK:], a_rb, u0, xv)
        qq = each(lambda x, y, z: x + _dot(y, bd(z)), rt_w, a_rb, ta)
        yield
        left = each(lambda x, y, z: b16(rows2(cols2(x, y), cols2(jnp.zeros_like(z), z))), ta, u0, v_w)
        right = each(lambda x, y: b16(rows2(x, y)), pick(p["bw"]), pick(p["kw"]))
        mn = each(_dot_tn, left, right)
        for i, (rq, sl) in enumerate(probs):
            rows = slice(b * blk + rq.start, b * blk + rq.stop)
            y0_s[w_slot, rows, sl] = y0[i]
            q_s[w_slot, rows, sl] = b16(qq[i])
            mc_s[w_slot, rows, sl] = b16(fold4(mn[i][:GW] * bm))
            nc_s[w_slot, rows, sl] = fold4(mn[i][GW:] * bm)
        yield

    first = prepare(0)
    for j in range(npc):
        recurrent_chunk(j)
        if j * 4 // npc != (j + 1) * 4 // npc or j == npc - 1:
            for _ in range((j + 1) * 4 // npc - j * 4 // npc):
                next(first, None)
    for _ in first:
        pass
    for (b, hg), s in states.items():
        s_ref[b, hg] = s

    y = yraw_s[...]
    inv_n = 1.0 / RWKV_HEAD
    ym = head_sum(y) * inv_n
    yc = y - ym
    yv = head_sum(yc * yc) * inv_n
    yn = yc * lax.rsqrt(yv + GN_EPS) * gng_ref[...] + gnb_ref[...]
    y_ref[...] = ((yn + bonus_s[r_slot]) * g_s[r_slot]).astype(y_ref.dtype).reshape(y_ref.shape)

    for b in range(nb):
        solver = solve(b, prepared[b])
        nxt = prepare(b + 1) if b + 1 < nb else iter(())
        for level, _ in enumerate(solver):
            if level % 4 == 3:
                next(nxt, None)
        for _ in nxt:
            pass


def _rwkv_masks():
    f = lambda m: m.astype(F32)
    i = jnp.arange(GW)[:, None]
    j = jnp.arange(GW)[None, :]
    bm = f((i // RWKV_HEAD) == (j // RWKV_HEAD))
    t = jnp.arange(CHUNK)[:, None]
    s = (jnp.arange(GW) % CHUNK)[None, :]
    same = lambda n: (t // n) == (s // n)
    msl = f(t > s)
    mil = f(t >= s)
    m8 = f(same(8))
    m16 = f(same(16) & ~same(8))
    m32 = f(same(32) & ~same(16))
    m64 = f(~same(32))
    eye = f(t == s)
    hi = jnp.arange(RWKV_W)
    bones = ((hi[:, None] // RWKV_HEAD) == (hi[None, :] // RWKV_HEAD)).astype(BF16)
    return bones, bm, eye, msl, mil, m8, m16, m32, m64


def _rwkv_pipe_mixer(ur, ur_tail, params):
    B, T, _ = ur.shape
    blk = TAIL
    assert T % blk == 0 and blk % CHUNK == 0 and CHUNK == RWKV_HEAD
    n_blocks = T // blk
    rows = B * blk
    consts = _rwkv_masks()
    in_map = lambda s: (0, jnp.clip(s - 1, 0, n_blocks - 1), 0)
    out_map = lambda s: (0, jnp.clip(s - 2, 0, n_blocks - 1), 0)
    slot2 = lambda w, dt: pltpu.VMEM((2, rows, w), dt)
    return pl.pallas_call(
        _rwkv_pipe_kernel,
        out_shape=jax.ShapeDtypeStruct((B, T, RWKV_W), BF16),
        grid=(n_blocks + 2,),
        in_specs=[pl.BlockSpec((B, blk, UR_W), in_map), _full(ur_tail)]
                 + [_full(p) for p in params] + [_full(m) for m in consts],
        out_specs=pl.BlockSpec((B, blk, RWKV_W), out_map),
        scratch_shapes=[pltpu.VMEM((B, N_HGROUPS, CHUNK, GW), F32), pltpu.VMEM((B, UR_W), F32),
                        slot2(RWKV_W, F32), slot2(RWKV_W, BF16), slot2(RWKV_W, BF16), slot2(RWKV_W, F32),
                        pltpu.VMEM((2, 8 * rows // CHUNK, RWKV_W), F32), slot2(RWKV_W, F32), slot2(RWKV_W, F32),
                        pltpu.VMEM((rows, RWKV_W), F32)],
        compiler_params=_cparams(("arbitrary",)),
        name="rwkv7",
    )(ur, ur_tail, *params, *consts)


def _gelu_tanh(x):
    return 0.5 * x * (1.0 + jnp.tanh(math.sqrt(2.0 / math.pi) * (x + 0.044715 * (x * x * x))))


LRU_CARRY = 8


def _lru_kernel(u_ref, ut_ref, cw_ref, cb_ref, wrg_ref, brg_ref, wig_ref, big_ref, lam_ref, y_ref,
                xs_ref, hprev_ref):
    c = pl.program_id(0)
    nb = u_ref.shape[0]
    nrow = nb * LRU_TILE

    @pl.when(c == 0)
    def _():
        xs_ref[...] = jnp.zeros_like(xs_ref)
        hprev_ref[...] = jnp.zeros_like(hprev_ref)

    u_x = u_ref[...].reshape(nrow, UL_W)
    u = jnp.where(c == 0, jnp.concatenate([ut_ref[...]] * nb, axis=0), u_x)
    xl = u[:, :LRU_W]
    gl = u[:, LRU_W:]
    row = jnp.bitwise_and(lax.broadcasted_iota(jnp.int32, (nrow, LRU_W), 0), LRU_TILE - 1)
    in_group = jnp.bitwise_and(row, 7)
    roll_in_group = lambda t, d: pltpu.roll(t.reshape(t.shape[0] // 8, 8, LRU_W), d, 1).reshape(t.shape)
    xl_prev = jnp.concatenate(
        [p for b in range(nb) for p in (xs_ref[b], xl[b * LRU_TILE:(b + 1) * LRU_TILE - 8])], axis=0)
    xc = cb_ref[...] + cw_ref[CONV_WIDTH - 1:CONV_WIDTH, :] * xl
    for d in range(1, CONV_WIDTH):
        tap = jnp.where(in_group >= d, roll_in_group(xl, d), roll_in_group(xl_prev, d))
        xc = xc + cw_ref[CONV_WIDTH - 1 - d:CONV_WIDTH - d, :] * tap
    for b in range(nb):
        xs_ref[b] = xl[(b + 1) * LRU_TILE - 8:(b + 1) * LRU_TILE]

    xcb = xc.astype(BF16)
    gate_r = jax.nn.sigmoid(_dot(xcb, wrg_ref[...]) + brg_ref[...])
    gate_i = jax.nn.sigmoid(_dot(xcb, wig_ref[...]) + big_ref[...])
    lam = lam_ref[...]
    log_sig = -(jnp.maximum(-lam, 0.0) + jnp.log1p(jnp.exp(-jnp.abs(lam))))
    log_a = LRU_C * gate_r * log_sig
    a = jnp.exp(log_a)
    mult = jnp.sqrt(jnp.maximum(1.0 - jnp.exp(2.0 * log_a), 0.0))
    b = mult * gate_i * xc
    b = jnp.where((c == 0) & (row < LRU_TILE - N_META), 0.0, b)

    d = 1
    while d < 8:
        keep = in_group >= d
        a_sh = jnp.where(keep, roll_in_group(a, d), 1.0)
        b_sh = jnp.where(keep, roll_in_group(b, d), 0.0)
        b = a * b_sh + b
        a = a * a_sh
        d *= 2
    groups = []
    for bi in range(nb):
        carry = hprev_ref[bi:bi + 1, :]
        for gi in range(LRU_TILE // 8):
            lo = bi * LRU_TILE + gi * 8
            hg = b[lo:lo + 8] + a[lo:lo + 8] * carry
            carry = hg[7:8, :]
            groups.append(hg)
        hprev_ref[bi:bi + 1, :] = carry
    h = jnp.concatenate(groups, axis=0)
    y_ref[...] = (h * _gelu_tanh(gl)).astype(y_ref.dtype).reshape(y_ref.shape)


def _lru_mixer(ul, ul_tail, params):
    B, T, _ = ul.shape
    assert TAIL == LRU_TILE
    x_map = lambda c: (0, jnp.maximum(c - 1, 0), 0)
    return pl.pallas_call(
        _lru_kernel,
        out_shape=jax.ShapeDtypeStruct((B, T, LRU_W), BF16),
        grid=(T // LRU_TILE + 1,),
        in_specs=[pl.BlockSpec((B, LRU_TILE, UL_W), x_map), _full(ul_tail)] + [_full(p) for p in params],
        out_specs=pl.BlockSpec((B, LRU_TILE, LRU_W), x_map),
        scratch_shapes=[pltpu.VMEM((B, LRU_CARRY, LRU_W), F32), pltpu.VMEM((B, LRU_W), F32)],
        compiler_params=_cparams(("arbitrary",)),
        name="rglru",
    )(ul, ul_tail, *params)


def _route(lg):
    lane = lax.broadcasted_iota(jnp.int32, lg.shape, 1)
    neg = jnp.float32(-jnp.inf)
    rmax = lambda t: jnp.max(t, axis=1, keepdims=True)
    first = lambda hit: jnp.min(jnp.where(hit, lane, LANE), axis=1, keepdims=True)
    is_grp = lane < N_GROUPS
    gl = jnp.where(is_grp, lg, neg)
    gmax = rmax(gl)
    g_sel = first(gl == gmax)
    p_g = 1.0 / jnp.sum(jnp.where(is_grp, jnp.exp(lg - gmax), 0.0), axis=1, keepdims=True)
    ex = lane - N_GROUPS
    in_grp = (ex >= 0) & (ex < N_EXPERTS) & (jnp.right_shift(ex, 3) == g_sel)
    el = jnp.where(in_grp, lg, neg)
    v1 = rmax(el)
    i1 = first(el == v1)
    el2 = jnp.where(lane == i1, neg, el)
    v2 = rmax(el2)
    i2 = first(el2 == v2)
    t = jnp.exp(v2 - v1)
    gate1 = p_g / (1.0 + t)
    gate2 = p_g * t / (1.0 + t)
    e1 = (i1 - N_GROUPS).astype(F32)
    e2 = (i2 - N_GROUPS).astype(F32)
    return jnp.where(lane == 0, e1, jnp.where(lane == 1, e2, jnp.where(lane == 2, gate1, jnp.where(lane == 3, gate2, 0.0))))


SLABS = D_MODEL // LANE


def _store_token_tiles(ref, val):
    n = val.shape[0]
    for s in range(SLABS):
        ref[pl.ds(s, n, stride=SLABS), :] = val[:, s * LANE:(s + 1) * LANE]


def _load_token_slabs(ref, n):
    return [ref[pl.ds(s, n, stride=SLABS), :] for s in range(SLABS)]


def _outproj_kernel(h0_ref, yr_ref, yl_ref, wor_ref, wol_ref, g1_ref, b1_ref,
                    wrt_hi_ref, wrt_lo_ref, brt_ref, h1_ref, rt_ref):
    h0 = h0_ref[0]
    mix = _dot(yr_ref[0], wor_ref[...]) + _dot(yl_ref[0], wol_ref[...])
    h1 = _layer_norm(DEEPNORM_ALPHA * h0 + mix, g1_ref[...], b1_ref[...])
    _store_token_tiles(h1_ref.at[0], h1)
    hi = h1.astype(BF16)
    lo = (h1 - hi.astype(F32)).astype(BF16)
    w_hi = wrt_hi_ref[...]
    lg = _dot(hi, w_hi) + (_dot(hi, wrt_lo_ref[...]) + _dot(lo, w_hi)) + brt_ref[...]
    rt_ref[0] = _route(lg)


def _out_projection(h0, y_rwkv, y_lru, wo_r, wo_l, ln1_g, ln1_b, wrt_hi, wrt_lo, brt):
    B, T, D = h0.shape
    tm = 512
    rows = lambda w: pl.BlockSpec((1, tm, w), lambda b, i: (b, i, 0))
    return pl.pallas_call(
        _outproj_kernel,
        out_shape=(jax.ShapeDtypeStruct((B, T * SLABS, LANE), F32), jax.ShapeDtypeStruct((B, T, LANE), F32)),
        grid=(B, T // tm),
        in_specs=[rows(D), rows(RWKV_W), rows(LRU_W), _full(wo_r), _full(wo_l),
                  _full(ln1_g), _full(ln1_b), _full(wrt_hi), _full(wrt_lo), _full(brt)],
        out_specs=(pl.BlockSpec((1, tm * SLABS, LANE), lambda b, i: (b, i, 0)), rows(LANE)),
        compiler_params=_cparams(("parallel", "parallel")),
        name="outproj",
    )(h0, y_rwkv, y_lru, wo_r, wo_l, ln1_g, ln1_b, wrt_hi, wrt_lo, brt)


SC_WINDOW = 32


def _sc_gather(table, idx):
    info = plsc.get_sparse_core_info()
    n_workers = info.num_cores * info.num_subcores
    n = idx.shape[0]
    per_worker = n // n_workers
    n_win = per_worker // SC_WINDOW
    assert n % (n_workers * SC_WINDOW * 2) == 0
    mesh = plsc.VectorSubcoreMesh(core_axis_name="c", subcore_axis_name="s")

    @functools.partial(
        pl.kernel, mesh=mesh,
        out_type=jax.ShapeDtypeStruct((n, SLABS, LANE), F32),
        scratch_types=[pltpu.VMEM((SC_WINDOW,), jnp.int32), pltpu.VMEM((SC_WINDOW,), jnp.int32),
                       pltpu.VMEM((SC_WINDOW, SLABS, LANE), F32), pltpu.VMEM((SC_WINDOW, SLABS, LANE), F32),
                       pltpu.SemaphoreType.DMA, pltpu.SemaphoreType.DMA],
        name="sc_row_gather",
    )
    def gather_kernel(table_hbm, idx_hbm, out_hbm, idx_a, idx_b, rows_a, rows_b, sem_a, sem_b):
        worker = lax.axis_index("s") * info.num_cores + lax.axis_index("c")
        base = worker * per_worker
        bufs = ((idx_a, rows_a, sem_a), (idx_b, rows_b, sem_b))

        def start(w, buf):
            idx_v, rows_v, sem = buf
            pltpu.sync_copy(idx_hbm.at[pl.ds(base + w * SC_WINDOW, SC_WINDOW)], idx_v)
            pltpu.async_copy(table_hbm.at[idx_v], rows_v, sem)

        def finish(w, buf):
            idx_v, rows_v, sem = buf
            pltpu.make_async_copy(table_hbm.at[idx_v], rows_v, sem).wait()
            pltpu.sync_copy(rows_v, out_hbm.at[pl.ds(base + w * SC_WINDOW, SC_WINDOW)])

        start(0, bufs[0])

        @pl.loop(0, n_win, step=2)
        def _(w):
            start(w + 1, bufs[1])
            finish(w, bufs[0])

            @pl.when(w + 2 < n_win)
            def _():
                start(w + 2, bufs[0])
            finish(w + 1, bufs[1])

    return gather_kernel(table, idx)


SC_LANES = 16


def _sc_invert_slots(dest, n_slots):
    n_asg = dest.shape[0]
    assert n_asg & (n_asg - 1) == 0 and n_asg % SC_LANES == 0 and n_slots % SC_LANES == 0
    info = plsc.get_sparse_core_info()
    mesh = plsc.VectorSubcoreMesh(core_axis_name="c", subcore_axis_name="s")

    @functools.partial(
        pl.kernel, mesh=mesh,
        out_type=jax.ShapeDtypeStruct((n_slots,), jnp.int32),
        scratch_types=[pltpu.VMEM((n_asg,), jnp.int32), pltpu.VMEM((n_slots,), jnp.int32)],
        compiler_params=pltpu.CompilerParams(needs_layout_passes=False),
        name="sc_invert_slots",
    )
    def invert_kernel(dest_hbm, out_hbm, dest_v, out_v):
        worker = lax.axis_index("s") * info.num_cores + lax.axis_index("c")

        @pl.when(worker == 0)
        def _():
            pltpu.sync_copy(dest_hbm, dest_v)
            lane = lax.iota(jnp.int32, SC_LANES)

            @pl.loop(0, n_slots // SC_LANES)
            def _(c):
                out_v[pl.ds(c * SC_LANES, SC_LANES)] = jnp.bitwise_and(c * SC_LANES + lane, n_asg - 1)

            @pl.loop(0, n_asg // SC_LANES)
            def _(c):
                plsc.store_scatter(out_v, [dest_v[pl.ds(c * SC_LANES, SC_LANES)]], c * SC_LANES + lane)

            pltpu.sync_copy(out_v, out_hbm)

    return invert_kernel(dest)


def _moe_rows_kernel(te_ref, nv_ref, x_ref, wg_ref, wu_ref, wd_ref, o_ref, wgb_ref, wub_ref, wdb_ref):
    i = pl.program_id(0)
    e = te_ref[i]
    e_prev = te_ref[jnp.maximum(i - 1, 0)]

    @pl.when((i == 0) | (e != e_prev))
    def _():
        wgb_ref[...] = wg_ref[0].astype(BF16)
        wub_ref[...] = wu_ref[0].astype(BF16)
        wdb_ref[...] = wd_ref[0].astype(BF16)

    @pl.when(i < nv_ref[0])
    def _():
        xb = jnp.concatenate(_load_token_slabs(x_ref, MOE_TILE), axis=1).astype(BF16)
        hg = _dot(xb, wgb_ref[...])
        hu = _dot(xb, wub_ref[...])
        mid = (hg * jax.nn.sigmoid(hg) * hu).astype(BF16)
        _store_token_tiles(o_ref, _dot(mid, wdb_ref[...]))

    @pl.when(i >= nv_ref[0])
    def _():
        o_ref[...] = jnp.zeros_like(o_ref)


def _moe_rows_into_kernel(te_ref, nv_ref, x_ref, wg_ref, wu_ref, wd_ref, prev_ref, o_ref, *scratch):
    del prev_ref
    _moe_rows_kernel(te_ref, nv_ref, x_ref, wg_ref, wu_ref, wd_ref, o_ref, *scratch)


def _moe_experts_rows(xbuf, tile_expert, n_valid, w_gate, w_up, w_down, total_tiles, tile_offset, prev=None):
    D = D_MODEL
    n_tiles = xbuf.shape[0] // (MOE_TILE * SLABS)
    tiles = pl.BlockSpec((MOE_TILE * SLABS, LANE), lambda i, te, nv: (jnp.maximum(jnp.minimum(i, nv[0] - 1), 0), 0))
    in_specs = [tiles,
                pl.BlockSpec((1, D, D_EXPERT), lambda i, te, nv: (te[i], 0, 0)),
                pl.BlockSpec((1, D, D_EXPERT), lambda i, te, nv: (te[i], 0, 0)),
                pl.BlockSpec((1, D_EXPERT, D), lambda i, te, nv: (te[i], 0, 0))]
    operands = [tile_expert, n_valid, xbuf, w_gate, w_up, w_down]
    if prev is not None:
        in_specs.append(pl.BlockSpec(memory_space=pl.ANY))
        operands.append(prev)
    grid_spec = pltpu.PrefetchScalarGridSpec(
        num_scalar_prefetch=2,
        grid=(n_tiles,),
        in_specs=in_specs,
        out_specs=pl.BlockSpec((MOE_TILE * SLABS, LANE), lambda i, te, nv: (i + tile_offset, 0)),
        scratch_shapes=[pltpu.VMEM((D, D_EXPERT), BF16), pltpu.VMEM((D, D_EXPERT), BF16),
                        pltpu.VMEM((D_EXPERT, D), BF16)],
    )
    return pl.pallas_call(
        _moe_rows_kernel if prev is None else _moe_rows_into_kernel,
        out_shape=jax.ShapeDtypeStruct((total_tiles * MOE_TILE * SLABS, LANE), F32),
        grid_spec=grid_spec,
        input_output_aliases={} if prev is None else {len(operands) - 1: 0},
        compiler_params=_cparams(("arbitrary",)),
        name="moe_experts",
    )(*operands)


def _row_copy(src_hbm, src_row, dst_ref, dst_row, sem):
    return pltpu.make_async_copy(src_hbm.at[pl.ds(src_row * SLABS, SLABS), :],
                                 dst_ref.at[pl.ds(dst_row * SLABS, SLABS), :], sem)


def _wait_tiles(src_hbm, dst_ref, sem):
    pltpu.make_async_copy(src_hbm.at[pl.ds(0, dst_ref.shape[0]), :], dst_ref, sem).wait()


def _combine_kernel(d_cur_ref, d_nxt_ref, h_ref, gate_ref, g_ref, b_ref, y_hbm, o_ref, ybuf, sem):
    i = pl.program_id(0)
    n = pl.num_programs(0)
    slot = lax.rem(i, 2)

    def start_gather(d_ref, s):
        def body(tt, carry):
            for u in range(DMA_UNROLL // TOP_K):
                t = tt * (DMA_UNROLL // TOP_K) + u
                for k in range(TOP_K):
                    _row_copy(y_hbm, d_ref[TOP_K * t + k], ybuf.at[s, k], t, sem.at[s]).start(priority=k % 2)
            return carry
        lax.fori_loop(0, COMBINE_TILE * TOP_K // DMA_UNROLL, body, 0)

    @pl.when(i == 0)
    def _():
        start_gather(d_cur_ref, 0)

    @pl.when(i + 1 < n)
    def _():
        start_gather(d_nxt_ref, 1 - slot)

    for k in range(TOP_K):
        _wait_tiles(y_hbm, ybuf.at[slot, k], sem.at[slot])

    gate = gate_ref[...]
    tm = COMBINE_TILE
    ga = jnp.broadcast_to(gate[:, 0:1], (tm, LANE))
    gb = jnp.broadcast_to(gate[:, 1:2], (tm, LANE))
    hs = _load_token_slabs(h_ref, tm)
    ya = _load_token_slabs(ybuf.at[slot, 0], tm)
    yb = _load_token_slabs(ybuf.at[slot, 1], tm)
    z = [DEEPNORM_ALPHA * h + (ga * a + gb * b) for h, a, b in zip(hs, ya, yb)]
    inv_d = 1.0 / D_MODEL
    mu = sum(jnp.sum(t, axis=1, keepdims=True) for t in z) * inv_d
    zc = [t - mu for t in z]
    var = sum(jnp.sum(t * t, axis=1, keepdims=True) for t in zc) * inv_d
    rstd = lax.rsqrt(var + LN_EPS)
    for s in range(SLABS):
        cols = slice(s * LANE, (s + 1) * LANE)
        o_ref[:, cols] = zc[s] * rstd * g_ref[:, cols] + b_ref[:, cols]


def _combine(h1, ybuf, dest, gates, ln2_g, ln2_b):
    D = D_MODEL
    M = h1.shape[0] // SLABS
    tm = COMBINE_TILE
    n = M // tm
    smem_tile = lambda f: pl.BlockSpec((TOP_K * tm,), f, memory_space=pltpu.SMEM)
    return pl.pallas_call(
        _combine_kernel,
        out_shape=jax.ShapeDtypeStruct((M, D), F32),
        grid=(n,),
        in_specs=[smem_tile(lambda i: (i,)), smem_tile(lambda i: (jnp.minimum(i + 1, n - 1),)),
                  pl.BlockSpec((tm * SLABS, LANE), lambda i: (i, 0)), pl.BlockSpec((tm, TOP_K), lambda i: (i, 0)),
                  _full(ln2_g), _full(ln2_b), pl.BlockSpec(memory_space=pl.ANY)],
        out_specs=pl.BlockSpec((tm, D), lambda i: (i, 0)),
        scratch_shapes=[pltpu.VMEM((2, TOP_K, tm * SLABS, LANE), F32), pltpu.SemaphoreType.DMA((2,))],
        compiler_params=_cparams(("arbitrary",)),
        name="combine",
    )(dest, dest, h1, gates, ln2_g, ln2_b, ybuf)


def _routing_plan(route):
    M = route.shape[0]
    eid = route[:, :TOP_K].astype(jnp.int32).reshape(-1)
    gates = route[:, TOP_K:2 * TOP_K]
    A = M * TOP_K
    onehot = (eid[:, None] == jnp.arange(N_EXPERTS, dtype=eid.dtype)[None, :]).astype(jnp.int32)
    csum = jnp.cumsum(onehot, axis=0)
    rank = jnp.sum(csum * onehot, axis=1) - 1
    counts = csum[-1]
    pcounts = (counts + MOE_TILE - 1) // MOE_TILE * MOE_TILE
    pends = jnp.cumsum(pcounts)
    pstarts = pends - pcounts
    dest = (jnp.sum(onehot * pstarts[None, :], axis=1) + rank).astype(jnp.int32)
    n_tiles = (A + N_EXPERTS * (MOE_TILE - 1) + MOE_TILE - 1) // MOE_TILE
    n_valid = (pends[-1] // MOE_TILE).astype(jnp.int32)
    tile_start = jnp.minimum(jnp.arange(n_tiles, dtype=jnp.int32) * MOE_TILE, pends[-1] - 1)
    tile_expert = jnp.sum((pends[None, :] <= tile_start[:, None]).astype(jnp.int32), axis=1)
    tile_expert = jnp.minimum(tile_expert, N_EXPERTS - 1).astype(jnp.int32)
    return gates, dest, n_tiles * MOE_TILE, tile_expert, n_valid.reshape(1)


def kernel(x, meta, ln0_g, ln0_b, w_in, mu_shift, w0, w_decay_up, a0, w_a_up, w_g_up, k_k, k_a, r_k, gn_g, gn_b, conv_w, conv_b, w_rg, b_rg, w_ig, b_ig, lru_lambda, w_out, ln1_g, ln1_b, w_router_grp, b_router_grp, w_router_exp, b_router_exp, w_exp_gate, w_exp_up, w_exp_down, ln2_g, ln2_b):
    B, T, D = x.shape
    assert D == D_MODEL and T % 512 == 0 and w_in.shape[0] == 1
    row = lambda p: p.reshape(1, -1).astype(F32)
    n_rw = 3 * RWKV_W
    w_in0 = w_in[0]

    def slots(p):
        pad = lambda a, n: jnp.pad(a, [(0, 0)] * (a.ndim - 1) + [(0, n - a.shape[-1])])
        zw = p[..., n_rw:n_rw + DECAY_RANK]
        za = p[..., n_rw + DECAY_RANK:n_rw + DECAY_RANK + AAA_RANK]
        zg = p[..., n_rw + DECAY_RANK + AAA_RANK:n_rw + DECAY_RANK + AAA_RANK + GATE_RANK]
        return jnp.concatenate([p[..., :n_rw], pad(zw, LANE), pad(za, LANE), pad(zg, ZG_SLOT)], axis=-1)

    rwkv_cols = n_rw + DECAY_RANK + AAA_RANK + GATE_RANK
    w_r = slots(w_in0[:, :rwkv_cols]).astype(BF16)
    w_l = w_in0[:, rwkv_cols:].astype(BF16)
    ur, ul, ur_t, ul_t, h0 = _in_projection(x, meta, row(ln0_g), row(ln0_b), w_r, w_l)

    pad_rows = lambda a, n: jnp.pad(a, ((0, n - a.shape[0]), (0, 0)))
    rwkv_params = (slots(mu_shift[0][None, :]).astype(F32), row(w0[0]), pad_rows(w_decay_up[0], LANE).astype(BF16),
                   row(a0[0]), pad_rows(w_a_up[0], LANE).astype(BF16), pad_rows(w_g_up[0], ZG_SLOT).astype(BF16),
                   row(k_k[0]), row(k_a[0]), row(r_k[0]), row(gn_g[0]), row(gn_b[0]))
    y_rwkv = _rwkv_pipe_mixer(ur, ur_t, rwkv_params)

    blockdiag = lambda w: jax.scipy.linalg.block_diag(*[w[i] for i in range(LRU_BLOCKS)]).astype(BF16)
    lru_params = (conv_w[0], row(conv_b[0]), blockdiag(w_rg[0]), row(b_rg[0]), blockdiag(w_ig[0]), row(b_ig[0]),
                  row(lru_lambda[0]))
    y_lru = _lru_mixer(ul, ul_t, lru_params)

    w_rt = jnp.concatenate([w_router_grp[0], w_router_exp[0]], axis=1)
    w_rt = jnp.pad(w_rt, ((0, 0), (0, LANE - w_rt.shape[1])))
    wrt_hi = w_rt.astype(BF16)
    wrt_lo = (w_rt - wrt_hi.astype(F32)).astype(BF16)
    b_rt = jnp.concatenate([b_router_grp[0], b_router_exp[0]])
    b_rt = jnp.pad(b_rt, (0, LANE - b_rt.shape[0])).reshape(1, LANE)
    wo = w_out[0].astype(BF16)
    h1, route = _out_projection(h0, y_rwkv, y_lru, wo[:RWKV_W], wo[RWKV_W:],
                                row(ln1_g[0]), row(ln1_b[0]), wrt_hi, wrt_lo, b_rt)

    M = B * T
    h1 = h1.reshape(M * SLABS, LANE)
    gates, dest, n_slots, tile_expert, n_valid = _routing_plan(route.reshape(M, LANE))
    row_asg = _sc_invert_slots(dest, n_slots)
    src_tok = lax.shift_right_logical(row_asg, 1)
    n_tiles = n_slots // MOE_TILE
    half = n_tiles // 2
    h1_tiles = h1.reshape(M, SLABS, LANE)
    ybuf = None
    for lo, hi in ((0, half), (half, n_tiles)):
        xbuf = _sc_gather(h1_tiles, src_tok[lo * MOE_TILE:hi * MOE_TILE]).reshape(-1, LANE)
        nv = jnp.clip(n_valid - lo, 0, hi - lo)
        ybuf = _moe_experts_rows(xbuf, tile_expert[lo:hi], nv, w_exp_gate[0], w_exp_up[0], w_exp_down[0],
                                 total_tiles=n_tiles, tile_offset=lo, prev=ybuf)
    out = _combine(h1, ybuf, dest, gates, row(ln2_g[0]), row(ln2_b[0]))
    return out.reshape(B, T, D)
```

```python
import functools
import math

import jax
import jax.numpy as jnp
from jax import lax
from jax.experimental import pallas as pl
from jax.experimental.pallas import tpu as pltpu
from jax.experimental.pallas import tpu_sc as plsc

F32 = jnp.float32
BF16 = jnp.bfloat16

D_MODEL = 1024
N_META = 16
RWKV_W = 512
RWKV_HEAD = 64
DECAY_RANK = 64
AAA_RANK = 64
GATE_RANK = 160
LRU_W = 512
LRU_BLOCKS = 8
CONV_WIDTH = 4
LRU_C = 8.0
N_GROUPS = 4
EXPERTS_PER_GROUP = 8
N_EXPERTS = N_GROUPS * EXPERTS_PER_GROUP
TOP_K = 2
D_EXPERT = 512
LN_EPS = 1e-5
GN_EPS = 64e-5
DEEPNORM_ALPHA = 2.0 ** 0.25

LANE = 128
OFF_R, OFF_K, OFF_V = 0, RWKV_W, 2 * RWKV_W
OFF_ZW = 3 * RWKV_W
OFF_ZA = OFF_ZW + LANE
OFF_ZG = OFF_ZA + LANE
ZG_SLOT = 2 * LANE
UR_W = OFF_ZG + ZG_SLOT
UL_W = 2 * LRU_W

TAIL = 256
CHUNK = 64
HEADS_PER_GROUP = 4
GW = HEADS_PER_GROUP * RWKV_HEAD
N_HGROUPS = RWKV_W // GW
LRU_TILE = TAIL
MOE_TILE = 256
COMBINE_TILE = 256
DMA_UNROLL = 8
V7X_VMEM_BYTES = 64 * 1024 * 1024
VMEM_LIMIT = V7X_VMEM_BYTES - 8 * 1024 * 1024


def _cparams(sem):
    return pltpu.CompilerParams(dimension_semantics=sem, vmem_limit_bytes=VMEM_LIMIT)


def _layer_norm(x, g, b):
    mu = jnp.mean(x, -1, keepdims=True)
    xc = x - mu
    var = jnp.mean(xc * xc, -1, keepdims=True)
    return xc * lax.rsqrt(var + LN_EPS) * g + b


def _dot(a, b):
    return jnp.dot(a, b, preferred_element_type=F32)


def _dot_nt(a, b):
    return lax.dot_general(a, b, (((1,), (1,)), ((), ())), preferred_element_type=F32)


def _dot_tn(a, b):
    return lax.dot_general(a, b, (((0,), (0,)), ((), ())), preferred_element_type=F32)


def _full(a):
    return pl.BlockSpec(a.shape, lambda *_: (0,) * a.ndim)


def _inproj_kernel(x_ref, g_ref, b_ref, wr_ref, wl_ref, ur_ref, ul_ref, h0_ref):
    h0 = _layer_norm(x_ref[0], g_ref[...], b_ref[...])
    h0_ref[0] = h0
    h = h0.astype(BF16)
    ur_ref[0] = _dot(h, wr_ref[...])
    ul_ref[0] = _dot(h, wl_ref[...])


def _inproj_tail_kernel(x_ref, g_ref, b_ref, wr_ref, wl_ref, ur_ref, ul_ref):
    h = _layer_norm(x_ref[...], g_ref[...], b_ref[...]).astype(BF16)
    rows = lax.broadcasted_iota(jnp.int32, (TAIL, 1), 0)
    valid = (rows >= TAIL - N_META).astype(F32)
    ur_ref[...] = _dot(h, wr_ref[...]) * valid
    ul_ref[...] = _dot(h, wl_ref[...]) * valid


def _in_projection(x, meta, ln0_g, ln0_b, w_r, w_l):
    B, T, D = x.shape
    tm = 512
    ur, ul, h0 = pl.pallas_call(
        _inproj_kernel,
        out_shape=(jax.ShapeDtypeStruct((B, T, UR_W), F32), jax.ShapeDtypeStruct((B, T, UL_W), F32),
                   jax.ShapeDtypeStruct((B, T, D), F32)),
        grid=(B, T // tm),
        in_specs=[pl.BlockSpec((1, tm, D), lambda b, i: (b, i, 0)), _full(ln0_g), _full(ln0_b), _full(w_r), _full(w_l)],
        out_specs=(pl.BlockSpec((1, tm, UR_W), lambda b, i: (b, i, 0)),
                   pl.BlockSpec((1, tm, UL_W), lambda b, i: (b, i, 0)),
                   pl.BlockSpec((1, tm, D), lambda b, i: (b, i, 0))),
        compiler_params=_cparams(("parallel", "parallel")),
        name="inproj",
    )(x, ln0_g, ln0_b, w_r, w_l)
    tail_x = jnp.concatenate([jnp.zeros((TAIL - N_META, D), F32), meta.astype(F32)], axis=0)
    ur_t, ul_t = pl.pallas_call(
        _inproj_tail_kernel,
        out_shape=(jax.ShapeDtypeStruct((TAIL, UR_W), F32), jax.ShapeDtypeStruct((TAIL, UL_W), F32)),
        grid=(1,),
        in_specs=[_full(tail_x), _full(ln0_g), _full(ln0_b), _full(w_r), _full(w_l)],
        out_specs=(pl.BlockSpec((TAIL, UR_W), lambda i: (0, 0)), pl.BlockSpec((TAIL, UL_W), lambda i: (0, 0))),
        compiler_params=_cparams(("arbitrary",)),
        name="inproj_tail",
    )(tail_x, ln0_g, ln0_b, w_r, w_l)
    return ur, ul, ur_t, ul_t, h0


def _rwkv_pipe_kernel(u_ref, ut_ref, mu_ref, w0_ref, wdu_ref, a0_ref, wau_ref, wgu_ref, kk_ref, ka_ref, rk_ref,
                      gng_ref, gnb_ref, bones_ref, bm_ref, eye_ref, msl_ref, mil_ref,
                      m8_ref, m16_ref, m32_ref, m64_ref, y_ref,
                      s_ref, prev_ref, y0_s, q_s, mc_s, nc_s, we_s, bonus_s, g_s, yraw_s):
    s_id = pl.program_id(0)
    nb = u_ref.shape[0]
    blk = u_ref.shape[1]
    npc = blk // CHUNK
    seq_rows = lambda q: slice(q * CHUNK, (q + 1) * CHUNK)
    w_slot = lax.rem(s_id, 2)
    r_slot = 1 - w_slot

    @pl.when(s_id == 0)
    def _():
        s_ref[...] = jnp.zeros_like(s_ref)
        prev_ref[...] = jnp.zeros_like(prev_ref)
        for ref in (y0_s, q_s, mc_s, nc_s, we_s, bonus_s, g_s):
            ref[1] = jnp.zeros(ref.shape[1:], ref.dtype)

    b16 = lambda t: t.astype(BF16)
    bones = bones_ref[...]
    head_sum = lambda t: _dot(b16(t), bones)
    bm = bm_ref[...]
    bm16 = b16(bm)
    tile4 = lambda t: jnp.concatenate([t] * HEADS_PER_GROUP, axis=0)
    fold4 = lambda t: sum(t[i * CHUNK:(i + 1) * CHUNK] for i in range(HEADS_PER_GROUP))
    bd = lambda t: tile4(b16(t)) * bm16

    chains = [(b, hg) for b in range(nb) for hg in range(N_HGROUPS)]
    states = {c: s_ref[c[0], c[1]] for c in chains}

    def recurrent_chunk(j):
        for b, hg in chains:
            sl = slice(hg * GW, (hg + 1) * GW)
            q = b * npc + j
            rq = seq_rows(q)
            s = states[(b, hg)]
            yraw_s[rq, sl] = y0_s[r_slot, rq, sl] + _dot_nt(q_s[r_slot, rq, sl], bd(s))
            states[(b, hg)] = (s * we_s[r_slot, q * 8:q * 8 + 1, sl] + _dot(b16(s), bd(mc_s[r_slot, rq, sl]))
                               + nc_s[r_slot, rq, sl])

    prepared = {}

    def prepare(b):
        lo = b * blk
        u = jnp.where(s_id == 0, ut_ref[...], u_ref[b])
        row = lax.broadcasted_iota(jnp.int32, u.shape, 0)
        u_prev = jnp.where(row == 0, prev_ref[b:b + 1, :], pltpu.roll(u, 1, 0))
        prev_ref[b:b + 1, :] = u[blk - 1:blk, :]
        x = u + (u_prev - u) * mu_ref[...]
        r = x[:, OFF_R:OFF_R + RWKV_W]
        k = x[:, OFF_K:OFF_K + RWKV_W]
        v = x[:, OFF_V:OFF_V + RWKV_W]
        zw = x[:, OFF_ZW:OFF_ZW + LANE]
        za = x[:, OFF_ZA:OFF_ZA + LANE]
        zg = x[:, OFF_ZG:OFF_ZG + ZG_SLOT]
        yield
        z = w0_ref[...] + _dot(b16(jnp.tanh(zw)), wdu_ref[...])
        logw = -math.exp(-0.5) * jax.nn.sigmoid(z)
        a = jax.nn.sigmoid(a0_ref[...] + _dot(b16(za), wau_ref[...]))
        g = _dot(b16(jax.nn.sigmoid(zg)), wgu_ref[...])
        kk = k * kk_ref[...]
        kk = kk / jnp.maximum(jnp.sqrt(head_sum(kk * kk)), 1e-12)
        k = k * (1.0 + (a - 1.0) * ka_ref[...])
        kka = kk * a
        bonus_s[w_slot, lo:lo + blk, :] = head_sum(r * k * rk_ref[...]) * v
        g_s[w_slot, lo:lo + blk, :] = g
        yield
        cl = logw
        row_in_chunk = jnp.bitwise_and(lax.broadcasted_iota(jnp.int32, cl.shape, 0), CHUNK - 1)
        d = 1
        while d < CHUNK:
            cl = cl + jnp.where(row_in_chunk >= d, pltpu.roll(cl, d, 0), 0.0)
            d *= 2
        yield
        cl_last = jnp.concatenate(
            [jnp.broadcast_to(cl[(j + 1) * CHUNK - 1:(j + 1) * CHUNK, :], (CHUNK, RWKV_W)) for j in range(npc)], axis=0)
        e_neg = jnp.exp(-cl)
        e_end = jnp.exp(cl_last - cl)
        w_end = jnp.exp(cl_last)
        for j in range(npc):
            q = b * npc + j
            we_s[w_slot, q * 8:(q + 1) * 8, :] = w_end[j * CHUNK:j * CHUNK + 8, :]
        prepared[b] = dict(rt=r * jnp.exp(cl), kt=k * e_neg, at=-kk * jnp.exp(cl - logw), bt=kka * e_neg,
                           kw=k * e_end, bw=kka * e_end, v=v)
        yield

    eye = eye_ref[...]
    msl = msl_ref[...]
    mil = mil_ref[...]
    rows2 = lambda x, y: jnp.concatenate([x, y], axis=0)
    cols2 = lambda x, y: jnp.concatenate([x, y], axis=1)
    each = lambda f, *ls: [f(*xs) for xs in zip(*ls)]

    def solve(b, p):
        probs = [(slice(j * CHUNK, (j + 1) * CHUNK), slice(hg * GW, (hg + 1) * GW))
                 for j in range(npc) for hg in range(N_HGROUPS)]
        pick = lambda t: [t[rq, sl] for rq, sl in probs]
        at_w, rt_w, v_w = pick(p["at"]), pick(p["rt"]), pick(p["v"])
        lhs = each(lambda x, y: b16(rows2(x, y)), at_w, rt_w)
        ab = each(_dot_nt, lhs, each(bd, pick(p["bt"])))
        ak = each(_dot_nt, lhs, each(bd, pick(p["kt"])))
        yield
        a_ab = each(lambda t: t[:CHUNK] * msl, ab)
        a_rb = each(lambda t: b16(t[CHUNK:] * mil), ab)
        a_xk = each(lambda t: b16(rows2(t[:CHUNK] * msl, t[CHUNK:] * mil)), ak)
        a0 = each(lambda t: b16(t * m8_ref[...]), a_ab)
        a2 = each(lambda t: b16(_dot(t, bd(t))), a0)
        yield
        a4 = each(lambda t: b16(_dot(t, bd(t))), a2)
        p1 = each(lambda t: eye + t.astype(F32), a0)
        p1 = each(lambda q, t: q + _dot(b16(q), bd(t)), p1, a2)
        yield
        tt = each(lambda q, t: q + _dot(b16(q), bd(t)), p1, a4)
        yield
        for m_ref in (m16_ref, m32_ref, m64_ref):
            tb = each(b16, tt)
            off = each(lambda t: b16(t * m_ref[...]), a_ab)
            half = each(lambda x, y: b16(_dot(x, bd(y))), tb, off)
            yield
            tt = each(lambda t, x, y: t + _dot(x, bd(y)), tt, half, tb)
            yield
        tb = each(b16, tt)
        xv = each(lambda x, y: _dot(x, bd(y)), a_xk, v_w)
        yield
        u0 = each(lambda x, y: _dot(x, bd(y[:CHUNK])), tb, xv)
        ta = each(lambda x, y: _dot(x, bd(y)), tb, at_w)
        yield
        y0 = each(lambda x, y, z: _dot(x, bd(y)) + z[CHUNK:], a_rb, u0, xv)
        qq = each(lambda x, y, z: x + _dot(y, bd(z)), rt_w, a_rb, ta)
        yield
        left = each(lambda x, y, z: b16(rows2(cols2(x, y), cols2(jnp.zeros_like(z), z))), ta, u0, v_w)
        right = each(lambda x, y: b16(rows2(x, y)), pick(p["bw"]), pick(p["kw"]))
        mn = each(_dot_tn, left, right)
        for i, (rq, sl) in enumerate(probs):
            rows = slice(b * blk + rq.start, b * blk + rq.stop)
            y0_s[w_slot, rows, sl] = y0[i]
            q_s[w_slot, rows, sl] = b16(qq[i])
            mc_s[w_slot, rows, sl] = b16(fold4(mn[i][:GW] * bm))
            nc_s[w_slot, rows, sl] = fold4(mn[i][GW:] * bm)
        yield

    first = prepare(0)
    for j in range(npc):
        recurrent_chunk(j)
        if j * 4 // npc != (j + 1) * 4 // npc or j == npc - 1:
            for _ in range((j + 1) * 4 // npc - j * 4 // npc):
                next(first, None)
    for _ in first:
        pass
    for (b, hg), s in states.items():
        s_ref[b, hg] = s

    y = yraw_s[...]
    inv_n = 1.0 / RWKV_HEAD
    ym = head_sum(y) * inv_n
    yc = y - ym
    yv = head_sum(yc * yc) * inv_n
    yn = yc * lax.rsqrt(yv + GN_EPS) * gng_ref[...] + gnb_ref[...]
    y_ref[...] = ((yn + bonus_s[r_slot]) * g_s[r_slot]).astype(y_ref.dtype).reshape(y_ref.shape)

    for b in range(nb):
        solver = solve(b, prepared[b])
        nxt = prepare(b + 1) if b + 1 < nb else iter(())
        for level, _ in enumerate(solver):
            if level % 4 == 3:
                next(nxt, None)
        for _ in nxt:
            pass


def _rwkv_masks():
    f = lambda m: m.astype(F32)
    i = jnp.arange(GW)[:, None]
    j = jnp.arange(GW)[None, :]
    bm = f((i // RWKV_HEAD) == (j // RWKV_HEAD))
    t = jnp.arange(CHUNK)[:, None]
    s = (jnp.arange(GW) % CHUNK)[None, :]
    same = lambda n: (t // n) == (s // n)
    msl = f(t > s)
    mil = f(t >= s)
    m8 = f(same(8))
    m16 = f(same(16) & ~same(8))
    m32 = f(same(32) & ~same(16))
    m64 = f(~same(32))
    eye = f(t == s)
    hi = jnp.arange(RWKV_W)
    bones = ((hi[:, None] // RWKV_HEAD) == (hi[None, :] // RWKV_HEAD)).astype(BF16)
    return bones, bm, eye, msl, mil, m8, m16, m32, m64


def _rwkv_pipe_mixer(ur, ur_tail, params):
    B, T, _ = ur.shape
    blk = TAIL
    assert T % blk == 0 and blk % CHUNK == 0 and CHUNK == RWKV_HEAD
    n_blocks = T // blk
    rows = B * blk
    consts = _rwkv_masks()
    in_map = lambda s: (0, jnp.clip(s - 1, 0, n_blocks - 1), 0)
    out_map = lambda s: (0, jnp.clip(s - 2, 0, n_blocks - 1), 0)
    slot2 = lambda w, dt: pltpu.VMEM((2, rows, w), dt)
    return pl.pallas_call(
        _rwkv_pipe_kernel,
        out_shape=jax.ShapeDtypeStruct((B, T, RWKV_W), BF16),
        grid=(n_blocks + 2,),
        in_specs=[pl.BlockSpec((B, blk, UR_W), in_map), _full(ur_tail)]
                 + [_full(p) for p in params] + [_full(m) for m in consts],
        out_specs=pl.BlockSpec((B, blk, RWKV_W), out_map),
        scratch_shapes=[pltpu.VMEM((B, N_HGROUPS, CHUNK, GW), F32), pltpu.VMEM((B, UR_W), F32),
                        slot2(RWKV_W, F32), slot2(RWKV_W, BF16), slot2(RWKV_W, BF16), slot2(RWKV_W, F32),
                        pltpu.VMEM((2, 8 * rows // CHUNK, RWKV_W), F32), slot2(RWKV_W, F32), slot2(RWKV_W, F32),
                        pltpu.VMEM((rows, RWKV_W), F32)],
        compiler_params=_cparams(("arbitrary",)),
        name="rwkv7",
    )(ur, ur_tail, *params, *consts)


def _gelu_tanh(x):
    return 0.5 * x * (1.0 + jnp.tanh(math.sqrt(2.0 / math.pi) * (x + 0.044715 * (x * x * x))))


LRU_CARRY = 8


def _lru_kernel(u_ref, ut_ref, cw_ref, cb_ref, wrg_ref, brg_ref, wig_ref, big_ref, lam_ref, y_ref,
                xs_ref, hprev_ref):
    c = pl.program_id(0)
    nb = u_ref.shape[0]
    nrow = nb * LRU_TILE

    @pl.when(c == 0)
    def _():
        xs_ref[...] = jnp.zeros_like(xs_ref)
        hprev_ref[...] = jnp.zeros_like(hprev_ref)

    u_x = u_ref[...].reshape(nrow, UL_W)
    u = jnp.where(c == 0, jnp.concatenate([ut_ref[...]] * nb, axis=0), u_x)
    xl = u[:, :LRU_W]
    gl = u[:, LRU_W:]
    row = jnp.bitwise_and(lax.broadcasted_iota(jnp.int32, (nrow, LRU_W), 0), LRU_TILE - 1)
    in_group = jnp.bitwise_and(row, 7)
    roll_in_group = lambda t, d: pltpu.roll(t.reshape(t.shape[0] // 8, 8, LRU_W), d, 1).reshape(t.shape)
    xl_prev = jnp.concatenate(
        [p for b in range(nb) for p in (xs_ref[b], xl[b * LRU_TILE:(b + 1) * LRU_TILE - 8])], axis=0)
    xc = cb_ref[...] + cw_ref[CONV_WIDTH - 1:CONV_WIDTH, :] * xl
    for d in range(1, CONV_WIDTH):
        tap = jnp.where(in_group >= d, roll_in_group(xl, d), roll_in_group(xl_prev, d))
        xc = xc + cw_ref[CONV_WIDTH - 1 - d:CONV_WIDTH - d, :] * tap
    for b in range(nb):
        xs_ref[b] = xl[(b + 1) * LRU_TILE - 8:(b + 1) * LRU_TILE]

    xcb = xc.astype(BF16)
    gate_r = jax.nn.sigmoid(_dot(xcb, wrg_ref[...]) + brg_ref[...])
    gate_i = jax.nn.sigmoid(_dot(xcb, wig_ref[...]) + big_ref[...])
    lam = lam_ref[...]
    log_sig = -(jnp.maximum(-lam, 0.0) + jnp.log1p(jnp.exp(-jnp.abs(lam))))
    log_a = LRU_C * gate_r * log_sig
    a = jnp.exp(log_a)
    mult = jnp.sqrt(jnp.maximum(1.0 - jnp.exp(2.0 * log_a), 0.0))
    b = mult * gate_i * xc
    b = jnp.where((c == 0) & (row < LRU_TILE - N_META), 0.0, b)

    d = 1
    while d < 8:
        keep = in_group >= d
        a_sh = jnp.where(keep, roll_in_group(a, d), 1.0)
        b_sh = jnp.where(keep, roll_in_group(b, d), 0.0)
        b = a * b_sh + b
        a = a * a_sh
        d *= 2
    groups = []
    for bi in range(nb):
        carry = hprev_ref[bi:bi + 1, :]
        for gi in range(LRU_TILE // 8):
            lo = bi * LRU_TILE + gi * 8
            hg = b[lo:lo + 8] + a[lo:lo + 8] * carry
            carry = hg[7:8, :]
            groups.append(hg)
        hprev_ref[bi:bi + 1, :] = carry
    h = jnp.concatenate(groups, axis=0)
    y_ref[...] = (h * _gelu_tanh(gl)).astype(y_ref.dtype).reshape(y_ref.shape)


def _lru_mixer(ul, ul_tail, params):
    B, T, _ = ul.shape
    assert TAIL == LRU_TILE
    x_map = lambda c: (0, jnp.maximum(c - 1, 0), 0)
    return pl.pallas_call(
        _lru_kernel,
        out_shape=jax.ShapeDtypeStruct((B, T, LRU_W), BF16),
        grid=(T // LRU_TILE + 1,),
        in_specs=[pl.BlockSpec((B, LRU_TILE, UL_W), x_map), _full(ul_tail)] + [_full(p) for p in params],
        out_specs=pl.BlockSpec((B, LRU_TILE, LRU_W), x_map),
        scratch_shapes=[pltpu.VMEM((B, LRU_CARRY, LRU_W), F32), pltpu.VMEM((B, LRU_W), F32)],
        compiler_params=_cparams(("arbitrary",)),
        name="rglru",
    )(ul, ul_tail, *params)


def _route(lg):
    lane = lax.broadcasted_iota(jnp.int32, lg.shape, 1)
    neg = jnp.float32(-jnp.inf)
    rmax = lambda t: jnp.max(t, axis=1, keepdims=True)
    first = lambda hit: jnp.min(jnp.where(hit, lane, LANE), axis=1, keepdims=True)
    is_grp = lane < N_GROUPS
    gl = jnp.where(is_grp, lg, neg)
    gmax = rmax(gl)
    g_sel = first(gl == gmax)
    p_g = 1.0 / jnp.sum(jnp.where(is_grp, jnp.exp(lg - gmax), 0.0), axis=1, keepdims=True)
    ex = lane - N_GROUPS
    in_grp = (ex >= 0) & (ex < N_EXPERTS) & (jnp.right_shift(ex, 3) == g_sel)
    el = jnp.where(in_grp, lg, neg)
    v1 = rmax(el)
    i1 = first(el == v1)
    el2 = jnp.where(lane == i1, neg, el)
    v2 = rmax(el2)
    i2 = first(el2 == v2)
    t = jnp.exp(v2 - v1)
    gate1 = p_g / (1.0 + t)
    gate2 = p_g * t / (1.0 + t)
    e1 = (i1 - N_GROUPS).astype(F32)
    e2 = (i2 - N_GROUPS).astype(F32)
    return jnp.where(lane == 0, e1, jnp.where(lane == 1, e2, jnp.where(lane == 2, gate1, jnp.where(lane == 3, gate2, 0.0))))


SLABS = D_MODEL // LANE


def _store_token_tiles(ref, val):
    n = val.shape[0]
    for s in range(SLABS):
        ref[pl.ds(s, n, stride=SLABS), :] = val[:, s * LANE:(s + 1) * LANE]


def _load_token_slabs(ref, n):
    return [ref[pl.ds(s, n, stride=SLABS), :] for s in range(SLABS)]


def _outproj_kernel(h0_ref, yr_ref, yl_ref, wor_ref, wol_ref, g1_ref, b1_ref,
                    wrt_hi_ref, wrt_lo_ref, brt_ref, h1_ref, rt_ref):
    h0 = h0_ref[0]
    mix = _dot(yr_ref[0], wor_ref[...]) + _dot(yl_ref[0], wol_ref[...])
    h1 = _layer_norm(DEEPNORM_ALPHA * h0 + mix, g1_ref[...], b1_ref[...])
    _store_token_tiles(h1_ref.at[0], h1)
    hi = h1.astype(BF16)
    lo = (h1 - hi.astype(F32)).astype(BF16)
    w_hi = wrt_hi_ref[...]
    lg = _dot(hi, w_hi) + (_dot(hi, wrt_lo_ref[...]) + _dot(lo, w_hi)) + brt_ref[...]
    rt_ref[0] = _route(lg)


def _out_projection(h0, y_rwkv, y_lru, wo_r, wo_l, ln1_g, ln1_b, wrt_hi, wrt_lo, brt):
    B, T, D = h0.shape
    tm = 512
    rows = lambda w: pl.BlockSpec((1, tm, w), lambda b, i: (b, i, 0))
    return pl.pallas_call(
        _outproj_kernel,
        out_shape=(jax.ShapeDtypeStruct((B, T * SLABS, LANE), F32), jax.ShapeDtypeStruct((B, T, LANE), F32)),
        grid=(B, T // tm),
        in_specs=[rows(D), rows(RWKV_W), rows(LRU_W), _full(wo_r), _full(wo_l),
                  _full(ln1_g), _full(ln1_b), _full(wrt_hi), _full(wrt_lo), _full(brt)],
        out_specs=(pl.BlockSpec((1, tm * SLABS, LANE), lambda b, i: (b, i, 0)), rows(LANE)),
        compiler_params=_cparams(("parallel", "parallel")),
        name="outproj",
    )(h0, y_rwkv, y_lru, wo_r, wo_l, ln1_g, ln1_b, wrt_hi, wrt_lo, brt)


SC_WINDOW = 32


def _sc_gather(table, idx):
    info = plsc.get_sparse_core_info()
    n_workers = info.num_cores * info.num_subcores
    n = idx.shape[0]
    per_worker = n // n_workers
    n_win = per_worker // SC_WINDOW
    assert n % (n_workers * SC_WINDOW * 2) == 0
    mesh = plsc.VectorSubcoreMesh(core_axis_name="c", subcore_axis_name="s")

    @functools.partial(
        pl.kernel, mesh=mesh,
        out_type=jax.ShapeDtypeStruct((n, SLABS, LANE), F32),
        scratch_types=[pltpu.VMEM((SC_WINDOW,), jnp.int32), pltpu.VMEM((SC_WINDOW,), jnp.int32),
                       pltpu.VMEM((SC_WINDOW, SLABS, LANE), F32), pltpu.VMEM((SC_WINDOW, SLABS, LANE), F32),
                       pltpu.SemaphoreType.DMA, pltpu.SemaphoreType.DMA],
        name="sc_row_gather",
    )
    def gather_kernel(table_hbm, idx_hbm, out_hbm, idx_a, idx_b, rows_a, rows_b, sem_a, sem_b):
        worker = lax.axis_index("s") * info.num_cores + lax.axis_index("c")
        base = worker * per_worker
        bufs = ((idx_a, rows_a, sem_a), (idx_b, rows_b, sem_b))

        def start(w, buf):
            idx_v, rows_v, sem = buf
            pltpu.sync_copy(idx_hbm.at[pl.ds(base + w * SC_WINDOW, SC_WINDOW)], idx_v)
            pltpu.async_copy(table_hbm.at[idx_v], rows_v, sem)

        def finish(w, buf):
            idx_v, rows_v, sem = buf
            pltpu.make_async_copy(table_hbm.at[idx_v], rows_v, sem).wait()
            pltpu.sync_copy(rows_v, out_hbm.at[pl.ds(base + w * SC_WINDOW, SC_WINDOW)])

        start(0, bufs[0])

        @pl.loop(0, n_win, step=2)
        def _(w):
            start(w + 1, bufs[1])
            finish(w, bufs[0])

            @pl.when(w + 2 < n_win)
            def _():
                start(w + 2, bufs[0])
            finish(w + 1, bufs[1])

    return gather_kernel(table, idx)


SC_LANES = 16


def _sc_invert_slots(dest, n_slots):
    n_asg = dest.shape[0]
    assert n_asg & (n_asg - 1) == 0 and n_asg % SC_LANES == 0 and n_slots % SC_LANES == 0
    info = plsc.get_sparse_core_info()
    mesh = plsc.VectorSubcoreMesh(core_axis_name="c", subcore_axis_name="s")

    @functools.partial(
        pl.kernel, mesh=mesh,
        out_type=jax.ShapeDtypeStruct((n_slots,), jnp.int32),
        scratch_types=[pltpu.VMEM((n_asg,), jnp.int32), pltpu.VMEM((n_slots,), jnp.int32)],
        compiler_params=pltpu.CompilerParams(needs_layout_passes=False),
        name="sc_invert_slots",
    )
    def invert_kernel(dest_hbm, out_hbm, dest_v, out_v):
        worker = lax.axis_index("s") * info.num_cores + lax.axis_index("c")

        @pl.when(worker == 0)
        def _():
            pltpu.sync_copy(dest_hbm, dest_v)
            lane = lax.iota(jnp.int32, SC_LANES)

            @pl.loop(0, n_slots // SC_LANES)
            def _(c):
                out_v[pl.ds(c * SC_LANES, SC_LANES)] = jnp.bitwise_and(c * SC_LANES + lane, n_asg - 1)

            @pl.loop(0, n_asg // SC_LANES)
            def _(c):
                plsc.store_scatter(out_v, [dest_v[pl.ds(c * SC_LANES, SC_LANES)]], c * SC_LANES + lane)

            pltpu.sync_copy(out_v, out_hbm)

    return invert_kernel(dest)


def _moe_rows_kernel(te_ref, nv_ref, ne_ref, rs_ref, x_ref, wg_hbm, wu_hbm, wd_hbm, o_ref,
                     wgb_ref, wub_ref, wdb_ref, wgf_ref, wuf_ref, wdf_ref, wsem):
    i = pl.program_id(0)
    e = te_ref[i]
    e_prev = te_ref[jnp.maximum(i - 1, 0)]
    slot = rs_ref[i]

    def weight_copies(expert, s):
        return (pltpu.make_async_copy(wg_hbm.at[expert], wgf_ref.at[s], wsem.at[s]),
                pltpu.make_async_copy(wu_hbm.at[expert], wuf_ref.at[s], wsem.at[s]),
                pltpu.make_async_copy(wd_hbm.at[expert], wdf_ref.at[s], wsem.at[s]))

    @pl.when(i == 0)
    def _():
        for c in weight_copies(e, slot):
            c.start()

    @pl.when((i == 0) | (e != e_prev))
    def _():
        for c in weight_copies(e, slot):
            c.wait()
        wgb_ref[...] = wgf_ref[slot].astype(BF16)
        wub_ref[...] = wuf_ref[slot].astype(BF16)
        wdb_ref[...] = wdf_ref[slot].astype(BF16)
        nxt = ne_ref[i]

        @pl.when(nxt != e)
        def _():
            for c in weight_copies(nxt, 1 - slot):
                c.start()

    @pl.when(i < nv_ref[0])
    def _():
        xb = jnp.concatenate(_load_token_slabs(x_ref, MOE_TILE), axis=1).astype(BF16)
        hg = _dot(xb, wgb_ref[...])
        hu = _dot(xb, wub_ref[...])
        mid = (hg * jax.nn.sigmoid(hg) * hu).astype(BF16)
        _store_token_tiles(o_ref, _dot(mid, wdb_ref[...]))

    @pl.when(i >= nv_ref[0])
    def _():
        o_ref[...] = jnp.zeros_like(o_ref)


def _moe_rows_into_kernel(te_ref, nv_ref, ne_ref, rs_ref, x_ref, wg_ref, wu_ref, wd_ref, prev_ref, o_ref, *scratch):
    del prev_ref
    _moe_rows_kernel(te_ref, nv_ref, ne_ref, rs_ref, x_ref, wg_ref, wu_ref, wd_ref, o_ref, *scratch)


def _moe_experts_rows(xbuf, tile_expert, n_valid, w_gate, w_up, w_down, total_tiles, tile_offset, prev=None):
    D = D_MODEL
    n_tiles = xbuf.shape[0] // (MOE_TILE * SLABS)
    first_of_run = jnp.concatenate([jnp.ones((1,), jnp.bool_), tile_expert[1:] != tile_expert[:-1]])
    run_slot = ((jnp.cumsum(first_of_run.astype(jnp.int32)) - 1) % 2).astype(jnp.int32)
    after = jnp.sum((tile_expert[None, :] <= tile_expert[:, None]).astype(jnp.int32), axis=1)
    next_expert = jnp.where(after < n_tiles, tile_expert[jnp.minimum(after, n_tiles - 1)], tile_expert)
    any_spec = pl.BlockSpec(memory_space=pl.ANY)
    tiles = pl.BlockSpec((MOE_TILE * SLABS, LANE),
                         lambda i, te, nv, ne, rs: (jnp.maximum(jnp.minimum(i, nv[0] - 1), 0), 0))
    in_specs = [tiles, any_spec, any_spec, any_spec]
    operands = [tile_expert, n_valid, next_expert.astype(jnp.int32), run_slot, xbuf, w_gate, w_up, w_down]
    if prev is not None:
        in_specs.append(any_spec)
        operands.append(prev)
    grid_spec = pltpu.PrefetchScalarGridSpec(
        num_scalar_prefetch=4,
        grid=(n_tiles,),
        in_specs=in_specs,
        out_specs=pl.BlockSpec((MOE_TILE * SLABS, LANE), lambda i, te, nv, ne, rs: (i + tile_offset, 0)),
        scratch_shapes=[pltpu.VMEM((D, D_EXPERT), BF16), pltpu.VMEM((D, D_EXPERT), BF16),
                        pltpu.VMEM((D_EXPERT, D), BF16),
                        pltpu.VMEM((2, D, D_EXPERT), F32), pltpu.VMEM((2, D, D_EXPERT), F32),
                        pltpu.VMEM((2, D_EXPERT, D), F32), pltpu.SemaphoreType.DMA((2,))],
    )
    return pl.pallas_call(
        _moe_rows_kernel if prev is None else _moe_rows_into_kernel,
        out_shape=jax.ShapeDtypeStruct((total_tiles * MOE_TILE * SLABS, LANE), F32),
        grid_spec=grid_spec,
        input_output_aliases={} if prev is None else {len(operands) - 1: 0},
        compiler_params=_cparams(("arbitrary",)),
        name="moe_experts",
    )(*operands)


def _row_copy(src_hbm, src_row, dst_ref, dst_row, sem):
    return pltpu.make_async_copy(src_hbm.at[pl.ds(src_row * SLABS, SLABS), :],
                                 dst_ref.at[pl.ds(dst_row * SLABS, SLABS), :], sem)


def _wait_tiles(src_hbm, dst_ref, sem):
    pltpu.make_async_copy(src_hbm.at[pl.ds(0, dst_ref.shape[0]), :], dst_ref, sem).wait()


def _combine_kernel(d_cur_ref, d_nxt_ref, h_ref, gate_ref, g_ref, b_ref, y_hbm, o_ref, ybuf, sem):
    i = pl.program_id(0)
    n = pl.num_programs(0)
    slot = lax.rem(i, 2)

    def start_gather(d_ref, s):
        def body(tt, carry):
            for u in range(DMA_UNROLL // TOP_K):
                t = tt * (DMA_UNROLL // TOP_K) + u
                for k in range(TOP_K):
                    _row_copy(y_hbm, d_ref[TOP_K * t + k], ybuf.at[s, k], t, sem.at[s]).start(priority=k % 2)
            return carry
        lax.fori_loop(0, COMBINE_TILE * TOP_K // DMA_UNROLL, body, 0)

    @pl.when(i == 0)
    def _():
        start_gather(d_cur_ref, 0)

    @pl.when(i + 1 < n)
    def _():
        start_gather(d_nxt_ref, 1 - slot)

    for k in range(TOP_K):
        _wait_tiles(y_hbm, ybuf.at[slot, k], sem.at[slot])

    gate = gate_ref[...]
    tm = COMBINE_TILE
    ga = jnp.broadcast_to(gate[:, 0:1], (tm, LANE))
    gb = jnp.broadcast_to(gate[:, 1:2], (tm, LANE))
    hs = _load_token_slabs(h_ref, tm)
    ya = _load_token_slabs(ybuf.at[slot, 0], tm)
    yb = _load_token_slabs(ybuf.at[slot, 1], tm)
    z = [DEEPNORM_ALPHA * h + (ga * a + gb * b) for h, a, b in zip(hs, ya, yb)]
    inv_d = 1.0 / D_MODEL
    mu = sum(jnp.sum(t, axis=1, keepdims=True) for t in z) * inv_d
    zc = [t - mu for t in z]
    var = sum(jnp.sum(t * t, axis=1, keepdims=True) for t in zc) * inv_d
    rstd = lax.rsqrt(var + LN_EPS)
    for s in range(SLABS):
        cols = slice(s * LANE, (s + 1) * LANE)
        o_ref[:, cols] = zc[s] * rstd * g_ref[:, cols] + b_ref[:, cols]


def _combine(h1, ybuf, dest, gates, ln2_g, ln2_b):
    D = D_MODEL
    M = h1.shape[0] // SLABS
    tm = COMBINE_TILE
    n = M // tm
    smem_tile = lambda f: pl.BlockSpec((TOP_K * tm,), f, memory_space=pltpu.SMEM)
    return pl.pallas_call(
        _combine_kernel,
        out_shape=jax.ShapeDtypeStruct((M, D), F32),
        grid=(n,),
        in_specs=[smem_tile(lambda i: (i,)), smem_tile(lambda i: (jnp.minimum(i + 1, n - 1),)),
                  pl.BlockSpec((tm * SLABS, LANE), lambda i: (i, 0)), pl.BlockSpec((tm, TOP_K), lambda i: (i, 0)),
                  _full(ln2_g), _full(ln2_b), pl.BlockSpec(memory_space=pl.ANY)],
        out_specs=pl.BlockSpec((tm, D), lambda i: (i, 0)),
        scratch_shapes=[pltpu.VMEM((2, TOP_K, tm * SLABS, LANE), F32), pltpu.SemaphoreType.DMA((2,))],
        compiler_params=_cparams(("arbitrary",)),
        name="combine",
    )(dest, dest, h1, gates, ln2_g, ln2_b, ybuf)


def _routing_plan(route):
    M = route.shape[0]
    eid = route[:, :TOP_K].astype(jnp.int32).reshape(-1)
    gates = route[:, TOP_K:2 * TOP_K]
    A = M * TOP_K
    onehot = (eid[:, None] == jnp.arange(N_EXPERTS, dtype=eid.dtype)[None, :]).astype(jnp.int32)
    csum = jnp.cumsum(onehot, axis=0)
    rank = jnp.sum(csum * onehot, axis=1) - 1
    counts = csum[-1]
    pcounts = (counts + MOE_TILE - 1) // MOE_TILE * MOE_TILE
    pends = jnp.cumsum(pcounts)
    pstarts = pends - pcounts
    dest = (jnp.sum(onehot * pstarts[None, :], axis=1) + rank).astype(jnp.int32)
    n_tiles = (A + N_EXPERTS * (MOE_TILE - 1) + MOE_TILE - 1) // MOE_TILE
    n_valid = (pends[-1] // MOE_TILE).astype(jnp.int32)
    tile_start = jnp.minimum(jnp.arange(n_tiles, dtype=jnp.int32) * MOE_TILE, pends[-1] - 1)
    tile_expert = jnp.sum((pends[None, :] <= tile_start[:, None]).astype(jnp.int32), axis=1)
    tile_expert = jnp.minimum(tile_expert, N_EXPERTS - 1).astype(jnp.int32)
    return gates, dest, n_tiles * MOE_TILE, tile_expert, n_valid.reshape(1)


def kernel(x, meta, ln0_g, ln0_b, w_in, mu_shift, w0, w_decay_up, a0, w_a_up, w_g_up, k_k, k_a, r_k, gn_g, gn_b, conv_w, conv_b, w_rg, b_rg, w_ig, b_ig, lru_lambda, w_out, ln1_g, ln1_b, w_router_grp, b_router_grp, w_router_exp, b_router_exp, w_exp_gate, w_exp_up, w_exp_down, ln2_g, ln2_b):
    B, T, D = x.shape
    assert D == D_MODEL and T % 512 == 0 and w_in.shape[0] == 1
    row = lambda p: p.reshape(1, -1).astype(F32)
    n_rw = 3 * RWKV_W
    w_in0 = w_in[0]

    def slots(p):
        pad = lambda a, n: jnp.pad(a, [(0, 0)] * (a.ndim - 1) + [(0, n - a.shape[-1])])
        zw = p[..., n_rw:n_rw + DECAY_RANK]
        za = p[..., n_rw + DECAY_RANK:n_rw + DECAY_RANK + AAA_RANK]
        zg = p[..., n_rw + DECAY_RANK + AAA_RANK:n_rw + DECAY_RANK + AAA_RANK + GATE_RANK]
        return jnp.concatenate([p[..., :n_rw], pad(zw, LANE), pad(za, LANE), pad(zg, ZG_SLOT)], axis=-1)

    rwkv_cols = n_rw + DECAY_RANK + AAA_RANK + GATE_RANK
    w_r = slots(w_in0[:, :rwkv_cols]).astype(BF16)
    w_l = w_in0[:, rwkv_cols:].astype(BF16)
    ur, ul, ur_t, ul_t, h0 = _in_projection(x, meta, row(ln0_g), row(ln0_b), w_r, w_l)

    pad_rows = lambda a, n: jnp.pad(a, ((0, n - a.shape[0]), (0, 0)))
    rwkv_params = (slots(mu_shift[0][None, :]).astype(F32), row(w0[0]), pad_rows(w_decay_up[0], LANE).astype(BF16),
                   row(a0[0]), pad_rows(w_a_up[0], LANE).astype(BF16), pad_rows(w_g_up[0], ZG_SLOT).astype(BF16),
                   row(k_k[0]), row(k_a[0]), row(r_k[0]), row(gn_g[0]), row(gn_b[0]))
    y_rwkv = _rwkv_pipe_mixer(ur, ur_t, rwkv_params)

    blockdiag = lambda w: jax.scipy.linalg.block_diag(*[w[i] for i in range(LRU_BLOCKS)]).astype(BF16)
    lru_params = (conv_w[0], row(conv_b[0]), blockdiag(w_rg[0]), row(b_rg[0]), blockdiag(w_ig[0]), row(b_ig[0]),
                  row(lru_lambda[0]))
    y_lru = _lru_mixer(ul, ul_t, lru_params)

    w_rt = jnp.concatenate([w_router_grp[0], w_router_exp[0]], axis=1)
    w_rt = jnp.pad(w_rt, ((0, 0), (0, LANE - w_rt.shape[1])))
    wrt_hi = w_rt.astype(BF16)
    wrt_lo = (w_rt - wrt_hi.astype(F32)).astype(BF16)
    b_rt = jnp.concatenate([b_router_grp[0], b_router_exp[0]])
    b_rt = jnp.pad(b_rt, (0, LANE - b_rt.shape[0])).reshape(1, LANE)
    wo = w_out[0].astype(BF16)
    h1, route = _out_projection(h0, y_rwkv, y_lru, wo[:RWKV_W], wo[RWKV_W:],
                                row(ln1_g[0]), row(ln1_b[0]), wrt_hi, wrt_lo, b_rt)

    M = B * T
    h1 = h1.reshape(M * SLABS, LANE)
    gates, dest, n_slots, tile_expert, n_valid = _routing_plan(route.reshape(M, LANE))
    row_asg = _sc_invert_slots(dest, n_slots)
    src_tok = lax.shift_right_logical(row_asg, 1)
    n_tiles = n_slots // MOE_TILE
    half = n_tiles // 2
    h1_tiles = h1.reshape(M, SLABS, LANE)
    ybuf = None
    for lo, hi in ((0, half), (half, n_tiles)):
        xbuf = _sc_gather(h1_tiles, src_tok[lo * MOE_TILE:hi * MOE_TILE]).reshape(-1, LANE)
        nv = jnp.clip(n_valid - lo, 0, hi - lo)
        ybuf = _moe_experts_rows(xbuf, tile_expert[lo:hi], nv, w_exp_gate[0], w_exp_up[0], w_exp_down[0],
                                 total_tiles=n_tiles, tile_offset=lo, prev=ybuf)
    out = _combine(h1, ybuf, dest, gates, row(ln2_g[0]), row(ln2_b[0]))
    return out.reshape(B, T, D)
```
